```python
import jax, jax.numpy as jnp
from jax import lax
import numpy as np

D_MODEL = 1024
BATCH = 2
SEQ = 16384
DEPTH = 1
DEC_BATCH = 1
DEC_SEQ = 16384
PAST_LEN = 128

HEAD_DIM = 64
N_HEADS_A = 8
N_KV_HEADS_A = 2
WINDOW = 128
BAND_BLOCK = 128
N_HEADS_B = 8
GRID_W = 64
NA_ROWS = 8
NA_COLS = 16
N_GROUPS = 4
EXPERTS_PER_GROUP = 8
N_EXPERTS = N_GROUPS * EXPERTS_PER_GROUP
TOP_K = 2
D_EXPERT = D_MODEL // 2
MOE_BLOCK = 128
LN_EPS = 1e-5
DEEPNORM_ALPHA = (2.0 * DEPTH) ** 0.25
DEEPNORM_BETA = (8.0 * DEPTH) ** -0.25

WIDTH_A = N_HEADS_A * HEAD_DIM
KV_WIDTH_A = N_KV_HEADS_A * HEAD_DIM
WIDTH_B = N_HEADS_B * HEAD_DIM
IN_SPLITS = (WIDTH_A, KV_WIDTH_A, KV_WIDTH_A, WIDTH_B, WIDTH_B, WIDTH_B, D_MODEL, D_MODEL)
D_IN = sum(IN_SPLITS)

kernel_name = "hybrid_window_gqa_natten_hmoe_encoder"


def layer_norm(x, g, b):
    xf = x.astype(jnp.float32)
    mu = jnp.mean(xf, axis=-1, keepdims=True)
    var = jnp.mean(jnp.square(xf - mu), axis=-1, keepdims=True)
    y = (xf - mu) * lax.rsqrt(var + LN_EPS) * g.astype(jnp.float32) + b.astype(jnp.float32)
    return y.astype(x.dtype)


def window_gqa(q, k, v, sink, slopes):
    B, T, H, _ = q.shape
    KVH = k.shape[2]
    G = H // KVH
    nb = T // BAND_BLOCK
    L = 3 * BAND_BLOCK
    qb = q.reshape(B, nb, BAND_BLOCK, KVH, G, HEAD_DIM)

    def band(a):
        ap = jnp.pad(a, ((0, 0), (BAND_BLOCK, BAND_BLOCK), (0, 0), (0, 0)))
        ap = ap.reshape(B, nb + 2, BAND_BLOCK, KVH, HEAD_DIM)
        return jnp.concatenate([ap[:, :-2], ap[:, 1:-1], ap[:, 2:]], axis=2)

    kb, vb = band(k), band(v)
    s = jnp.einsum('bnqkgd,bnskd->bnkgqs', qb, kb,
                   preferred_element_type=jnp.float32) * (HEAD_DIM ** -0.5)
    qi = jnp.arange(BAND_BLOCK)
    kj = jnp.arange(L)
    rel = kj[None, :] - BAND_BLOCK - qi[:, None]
    kpos = jnp.arange(nb)[:, None] * BAND_BLOCK - BAND_BLOCK + kj[None, :]
    mask = (jnp.abs(rel) <= WINDOW)[None] & ((kpos >= 0) & (kpos < T))[:, None, :]
    s = s - slopes[None, None, :, :, None, None] * jnp.abs(rel).astype(jnp.float32)
    s = jnp.where(mask[None, :, None, None], s, -jnp.inf)
    sink_col = jnp.broadcast_to(
        sink.astype(jnp.float32).reshape(KVH, G)[None, None, :, :, None, None], s.shape[:-1] + (1,))
    p = jax.nn.softmax(jnp.concatenate([s, sink_col], axis=-1), axis=-1)[..., :L]
    o = jnp.einsum('bnkgqs,bnskd->bnqkgd', p.astype(v.dtype), vb)
    return o.reshape(B, T, H * HEAD_DIM)


def neighborhood_attn(q, k, v, rpb):
    B, T, H, _ = q.shape
    R = T // GRID_W
    KR = min(NA_ROWS, R)
    qg = q.reshape(B, R, GRID_W, H, HEAD_DIM)
    kg = k.reshape(B, R, GRID_W, H, HEAD_DIM)
    vg = v.reshape(B, R, GRID_W, H, HEAD_DIM)
    r = jnp.arange(R)
    rs = jnp.clip(r - KR // 2, 0, R - KR)
    row_idx = rs[:, None] + jnp.arange(KR)[None, :]
    kw = kg[:, row_idx]
    vw = vg[:, row_idx]
    s = jnp.einsum('brqhd,brkchd->brhqkc', qg, kw,
                   preferred_element_type=jnp.float32) * (HEAD_DIM ** -0.5)
    c = jnp.arange(GRID_W)
    cs = jnp.clip(c - NA_COLS // 2, 0, GRID_W - NA_COLS)
    col_mask = (c[None, :] >= cs[:, None]) & (c[None, :] < cs[:, None] + NA_COLS)
    dr = row_idx - r[:, None] + (NA_ROWS - 1)
    dc = jnp.clip(c[None, :] - c[:, None] + (NA_COLS - 1), 0, 2 * NA_COLS - 2)
    bias = rpb[:, dr[:, None, :, None], dc[None, :, None, :]]
    s = s + bias.transpose(1, 0, 2, 3, 4)[None].astype(jnp.float32)
    s = jnp.where(col_mask[:, None, :], s, -jnp.inf)
    p = jax.nn.softmax(s.reshape(B, R, H, GRID_W, KR * GRID_W), axis=-1).reshape(s.shape)
    o = jnp.einsum('brhqkc,brkchd->brqhd', p.astype(v.dtype), vw)
    return o.reshape(B, T, H * HEAD_DIM)


def hierarchical_moe(x, w_rg, b_rg, w_re, b_re, w_gate, w_up, w_down):
    B, T, D = x.shape
    N = B * T
    xt = x.reshape(N, D)
    lg = jnp.dot(xt, w_rg, preferred_element_type=jnp.float32) + b_rg.astype(jnp.float32)
    pg = jax.nn.softmax(lg, axis=-1)
    g = jnp.argmax(pg, axis=-1)
    pg_sel = jnp.take_along_axis(pg, g[:, None], axis=-1)[:, 0]
    le = (jnp.dot(xt, w_re, preferred_element_type=jnp.float32) + b_re.astype(jnp.float32))
    le = le.reshape(N, N_GROUPS, EXPERTS_PER_GROUP)
    le_sel = jnp.take_along_axis(le, g[:, None, None], axis=1)[:, 0]
    pe = jax.nn.softmax(le_sel, axis=-1)
    top_p, top_i = lax.top_k(pe, TOP_K)
    top_p = top_p / jnp.sum(top_p, axis=-1, keepdims=True)
    weights = pg_sel[:, None] * top_p
    eid = g[:, None].astype(jnp.int32) * EXPERTS_PER_GROUP + top_i.astype(jnp.int32)

    NA = N * TOP_K
    a_e = eid.reshape(NA)
    a_w = weights.reshape(NA)
    a_tok = jnp.repeat(jnp.arange(N, dtype=jnp.int32), TOP_K)
    order = jnp.argsort(a_e)
    se = a_e[order]
    s_tok = a_tok[order]
    s_w = a_w[order]
    counts = jnp.bincount(a_e, length=N_EXPERTS)
    offs = jnp.cumsum(counts) - counts
    rank = jnp.arange(NA, dtype=jnp.int32) - offs[se]
    padded = ((counts + MOE_BLOCK - 1) // MOE_BLOCK) * MOE_BLOCK
    pend = jnp.cumsum(padded)
    pstart = pend - padded
    dest = pstart[se] + rank
    P = NA + N_EXPERTS * MOE_BLOCK
    nblk = P // MOE_BLOCK
    xs = jnp.zeros((P, D), x.dtype).at[dest].set(xt[s_tok])
    blk_e = jnp.clip(jnp.searchsorted(pend, jnp.arange(nblk) * MOE_BLOCK, side='right'),
                     0, N_EXPERTS - 1)

    def expert_block(args):
        xb, e = args
        h = jax.nn.silu(xb @ w_gate[e]) * (xb @ w_up[e])
        return h @ w_down[e]

    ys = lax.map(expert_block, (xs.reshape(nblk, MOE_BLOCK, D), blk_e)).reshape(P, D)
    contrib = ys[dest] * s_w[:, None].astype(ys.dtype)
    y = jax.ops.segment_sum(contrib, s_tok, num_segments=N)
    return y.reshape(B, T, D)


def encode(x, ln_in_g, ln_in_b, w_in, attn_sink, rel_pos_bias, w_proj_a, w_proj_b, w_out,
           ln1_g, ln1_b, w_route_group, b_route_group, w_route_expert, b_route_expert,
           w_gate, w_up, w_down, ln2_g, ln2_b):
    B, T, _ = x.shape
    slopes = jnp.exp2(-8.0 * jnp.arange(1, N_HEADS_A + 1, dtype=jnp.float32) / N_HEADS_A)
    slopes = slopes.reshape(N_KV_HEADS_A, N_HEADS_A // N_KV_HEADS_A)
    split_points = [int(p) for p in np.cumsum(IN_SPLITS)[:-1]]
    h = layer_norm(x, ln_in_g, ln_in_b)
    for l in range(DEPTH):
        proj = jnp.einsum('btd,de->bte', h, w_in[l])
        qa, ka, va, qb, kb, vb, ga, gb = jnp.split(proj, split_points, axis=-1)
        oa = window_gqa(qa.reshape(B, T, N_HEADS_A, HEAD_DIM),
                        ka.reshape(B, T, N_KV_HEADS_A, HEAD_DIM),
                        va.reshape(B, T, N_KV_HEADS_A, HEAD_DIM),
                        attn_sink[l], slopes) @ w_proj_a[l]
        ob = neighborhood_attn(qb.reshape(B, T, N_HEADS_B, HEAD_DIM),
                               kb.reshape(B, T, N_HEADS_B, HEAD_DIM),
                               vb.reshape(B, T, N_HEADS_B, HEAD_DIM),
                               rel_pos_bias[l]) @ w_proj_b[l]
        mix = (jax.nn.sigmoid(ga) * oa + jax.nn.sigmoid(gb) * ob) @ w_out[l]
        h = layer_norm(DEEPNORM_ALPHA * h + mix, ln1_g[l], ln1_b[l])
        moe = hierarchical_moe(h, w_route_group[l], b_route_group[l], w_route_expert[l],
                               b_route_expert[l], w_gate[l], w_up[l], w_down[l])
        h = layer_norm(DEEPNORM_ALPHA * h + moe, ln2_g[l], ln2_b[l])
    return h


def setup_inputs(seed: int = 0) -> dict:
    key = jax.random.key(seed)
    ks = jax.random.split(key, 24)
    f32 = jnp.float32

    def nrm(k, shape, scale):
        return jax.random.normal(k, shape, f32) * scale

    return {
        'x_prompt': nrm(ks[0], (BATCH, SEQ, D_MODEL), 1.0),
        'x_sample': nrm(ks[1], (DEC_BATCH, DEC_SEQ, D_MODEL), 1.0),
        'ln_in_g': 1.0 + nrm(ks[2], (D_MODEL,), 0.02),
        'ln_in_b': nrm(ks[3], (D_MODEL,), 0.02),
        'w_in': nrm(ks[4], (DEPTH, D_MODEL, D_IN), D_MODEL ** -0.5),
        'attn_sink': nrm(ks[5], (DEPTH, N_HEADS_A), 0.5),
        'rel_pos_bias': nrm(ks[6], (DEPTH, N_HEADS_B, 2 * NA_ROWS - 1, 2 * NA_COLS - 1), 0.1),
        'w_proj_a': nrm(ks[7], (DEPTH, WIDTH_A, D_MODEL), WIDTH_A ** -0.5 * DEEPNORM_BETA),
        'w_proj_b': nrm(ks[8], (DEPTH, WIDTH_B, D_MODEL), WIDTH_B ** -0.5 * DEEPNORM_BETA),
        'w_out': nrm(ks[9], (DEPTH, D_MODEL, D_MODEL), D_MODEL ** -0.5 * DEEPNORM_BETA),
        'ln1_g': 1.0 + nrm(ks[10], (DEPTH, D_MODEL), 0.02),
        'ln1_b': nrm(ks[11], (DEPTH, D_MODEL), 0.02),
        'w_route_group': nrm(ks[12], (DEPTH, D_MODEL, N_GROUPS), D_MODEL ** -0.5),
        'b_route_group': nrm(ks[13], (DEPTH, N_GROUPS), 0.01),
        'w_route_expert': nrm(ks[14], (DEPTH, D_MODEL, N_EXPERTS), D_MODEL ** -0.5),
        'b_route_expert': nrm(ks[15], (DEPTH, N_EXPERTS), 0.01),
        'w_gate': nrm(ks[16], (DEPTH, N_EXPERTS, D_MODEL, D_EXPERT), D_MODEL ** -0.5),
        'w_up': nrm(ks[17], (DEPTH, N_EXPERTS, D_MODEL, D_EXPERT), D_MODEL ** -0.5),
        'w_down': nrm(ks[18], (DEPTH, N_EXPERTS, D_EXPERT, D_MODEL), D_EXPERT ** -0.5 * DEEPNORM_BETA),
        'ln2_g': 1.0 + nrm(ks[19], (DEPTH, D_MODEL), 0.02),
        'ln2_b': nrm(ks[20], (DEPTH, D_MODEL), 0.02),
    }


def reference(x_prompt, x_sample, ln_in_g, ln_in_b, w_in, attn_sink, rel_pos_bias, w_proj_a,
              w_proj_b, w_out, ln1_g, ln1_b, w_route_group, b_route_group, w_route_expert,
              b_route_expert, w_gate, w_up, w_down, ln2_g, ln2_b):
    weights = (ln_in_g, ln_in_b, w_in, attn_sink, rel_pos_bias, w_proj_a, w_proj_b, w_out,
               ln1_g, ln1_b, w_route_group, b_route_group, w_route_expert, b_route_expert,
               w_gate, w_up, w_down, ln2_g, ln2_b)
    y_prompt = encode(x_prompt, *weights)
    y_sample = encode(x_sample, *weights)
    return (y_prompt, y_sample)
```

```python
import functools

import numpy as np
import jax
import jax.numpy as jnp
from jax import lax
from jax.experimental import pallas as pl
from jax.experimental.pallas import tpu as pltpu

D_MODEL = 1024
HEAD_DIM = 64
N_HEADS_A = 8
N_KV_HEADS_A = 2
WINDOW = 128
N_HEADS_B = 8
GRID_W = 64
NA_ROWS = 8
NA_COLS = 16
N_GROUPS = 4
EXPERTS_PER_GROUP = 8
N_EXPERTS = N_GROUPS * EXPERTS_PER_GROUP
TOP_K = 2
D_EXPERT = D_MODEL // 2
LN_EPS = 1e-5
DEPTH = 1
DEEPNORM_ALPHA = (2.0 * DEPTH) ** 0.25
WIDTH_A = N_HEADS_A * HEAD_DIM
KV_WIDTH_A = N_KV_HEADS_A * HEAD_DIM
WIDTH_B = N_HEADS_B * HEAD_DIM
QKV_WIDTH = WIDTH_A + 2 * KV_WIDTH_A + 3 * WIDTH_B

LANES = 128
VMEM_LIMIT_BYTES = 56 * 1024 * 1024

NEG_BIG = -1e30

QA_COL, QB_COL, KB_COL, VB_COL = 0, WIDTH_A, WIDTH_A + WIDTH_B, WIDTH_A + 2 * WIDTH_B
KA_COL = WIDTH_A + 3 * WIDTH_B
VA_COL = KA_COL + KV_WIDTH_A

PERM_A = np.array([0, 4, 1, 5, 2, 6, 3, 7])


def _cparams(*sem):
    return pltpu.CompilerParams(dimension_semantics=sem, vmem_limit_bytes=VMEM_LIMIT_BYTES)


def _layer_norm(x, g, b):
    mu = jnp.mean(x, axis=-1, keepdims=True)
    xc = x - mu
    var = jnp.mean(xc * xc, axis=-1, keepdims=True)
    return xc * lax.rsqrt(var + LN_EPS) * g + b


def _qkv_kernel(x_ref, g_ref, b_ref, w_ref, o_ref):
    h = _layer_norm(x_ref[...], g_ref[...], b_ref[...])
    y = jnp.dot(h.astype(jnp.bfloat16), w_ref[...], preferred_element_type=jnp.float32)
    col = lax.broadcasted_iota(jnp.int32, (1, QKV_WIDTH), 1)
    y = y * jnp.where(col < KB_COL, HEAD_DIM ** -0.5, 1.0)
    o_ref[...] = y.astype(jnp.bfloat16)


def _qkv(x2, ln_g, ln_b, w_qkv, tm):
    n = x2.shape[0]
    return pl.pallas_call(
        _qkv_kernel,
        out_shape=jax.ShapeDtypeStruct((n, QKV_WIDTH), jnp.bfloat16),
        grid=(n // tm,),
        in_specs=[
            pl.BlockSpec((tm, D_MODEL), lambda i: (i, 0)),
            pl.BlockSpec((1, D_MODEL), lambda i: (0, 0)),
            pl.BlockSpec((1, D_MODEL), lambda i: (0, 0)),
            pl.BlockSpec((D_MODEL, QKV_WIDTH), lambda i: (0, 0)),
        ],
        out_specs=pl.BlockSpec((tm, QKV_WIDTH), lambda i: (i, 0)),
        compiler_params=_cparams("parallel"),
        name="qkv",
    )(x2, ln_g, ln_b, w_qkv)


WIN_BLK = 128


def _win_bias_table():
    qi = np.arange(WIN_BLK)[:, None]
    kj = np.arange(3 * WIN_BLK)[None, :]
    dist = np.abs(kj - WIN_BLK - qi).astype(np.float64)
    slopes = 2.0 ** (-8.0 * np.arange(1, N_HEADS_A + 1) / N_HEADS_A)
    per_head = np.where(dist <= WINDOW, -slopes[:, None, None] * dist[None], NEG_BIG)
    groups = [np.concatenate([per_head[j], per_head[j + 4]], axis=0) for j in range(4)]
    return np.stack(groups).astype(np.float32)


def _win_kernel(sink_ref, q_ref, kp_ref, km_ref, kn_ref, vp_ref, vm_ref, vn_ref, bias_ref, o_ref,
                *, nsub, nblk_seq):
    i = pl.program_id(1)
    kcat = jnp.concatenate([kp_ref[...], km_ref[...], kn_ref[...]], axis=0)
    vcat = jnp.concatenate([vp_ref[...], vm_ref[...], vn_ref[...]], axis=0)
    lo = lax.broadcasted_iota(jnp.int32, (1, LANES), 1) < HEAD_DIM
    col = lax.broadcasted_iota(jnp.int32, (1, 3 * WIN_BLK), 1)
    top = lax.broadcasted_iota(jnp.int32, (2 * WIN_BLK, 1), 0) < WIN_BLK
    zero = jnp.zeros((), jnp.bfloat16)
    for j in range(nsub):
        n = i * nsub + j
        off_seq = ((col < WIN_BLK) & (n == 0)) | ((col >= 2 * WIN_BLK) & (n == nblk_seq - 1))
        edge = jnp.where(off_seq, NEG_BIG, 0.0)
        kj = kcat[WIN_BLK * j:WIN_BLK * (j + 3)]
        vj = vcat[WIN_BLK * j:WIN_BLK * (j + 3)]
        for g in range(4):
            rows = slice(WIN_BLK * j, WIN_BLK * (j + 1))
            cols = slice(LANES * g, LANES * (g + 1))
            qg = q_ref[rows, cols]
            qm = jnp.concatenate([jnp.where(lo, qg, zero), jnp.where(lo, zero, qg)], axis=0)
            s = lax.dot_general(qm, kj, (((1,), (1,)), ((), ())), preferred_element_type=jnp.float32)
            s = s + bias_ref[g] + edge
            sink = jnp.where(top, sink_ref[g], sink_ref[g + 4])
            m = jnp.maximum(jnp.max(s, axis=-1, keepdims=True), sink)
            p = jnp.exp(s - m)
            l = jnp.sum(p, axis=-1, keepdims=True) + jnp.exp(sink - m)
            o2 = jnp.dot(p.astype(jnp.bfloat16), vj, preferred_element_type=jnp.float32)
            o2 = o2 * (1.0 / l)
            o_ref[rows, cols] = jnp.where(lo, o2[:WIN_BLK], o2[WIN_BLK:]).astype(jnp.bfloat16)


def _win_attention(qkv, sink, bsz, t, tq):
    n = bsz * t
    nsub = tq // WIN_BLK
    nblk_seq = t // WIN_BLK
    ntile = t // tq
    bias = jnp.asarray(_win_bias_table())

    def main_map(col):
        return lambda b, i, *_: (b * ntile + i, col)

    def prev_map(col):
        return lambda b, i, *_: (b * nblk_seq + jnp.maximum(i * nsub - 1, 0), col)

    def next_map(col):
        return lambda b, i, *_: (b * nblk_seq + jnp.minimum(i * nsub + nsub, nblk_seq - 1), col)

    halo = (WIN_BLK, LANES)
    ka, va = KA_COL // LANES, VA_COL // LANES
    grid_spec = pltpu.PrefetchScalarGridSpec(
        num_scalar_prefetch=1,
        grid=(bsz, ntile),
        in_specs=[
            pl.BlockSpec((tq, WIDTH_A), main_map(QA_COL // WIDTH_A)),
            pl.BlockSpec(halo, prev_map(ka)),
            pl.BlockSpec((tq, LANES), main_map(ka)),
            pl.BlockSpec(halo, next_map(ka)),
            pl.BlockSpec(halo, prev_map(va)),
            pl.BlockSpec((tq, LANES), main_map(va)),
            pl.BlockSpec(halo, next_map(va)),
            pl.BlockSpec((4, 2 * WIN_BLK, 3 * WIN_BLK), lambda b, i, *_: (0, 0, 0)),
        ],
        out_specs=pl.BlockSpec((tq, WIDTH_A), main_map(0)),
    )
    return pl.pallas_call(
        functools.partial(_win_kernel, nsub=nsub, nblk_seq=nblk_seq),
        out_shape=jax.ShapeDtypeStruct((n, WIDTH_A), jnp.bfloat16),
        grid_spec=grid_spec,
        compiler_params=_cparams("parallel", "parallel"),
        name="win",
    )(sink, qkv, qkv, qkv, qkv, qkv, qkv, qkv, bias)


NAT_ROWS_PER_STEP = 8
NAT_HALO_ROWS = NA_ROWS // 2
NAT_KEYS = NA_ROWS * GRID_W


def _nat_bias_table(rpb):
    c = np.arange(GRID_W)
    cs = np.clip(c - NA_COLS // 2, 0, GRID_W - NA_COLS)
    col_mask = (c[None, :] >= cs[:, None]) & (c[None, :] < cs[:, None] + NA_COLS)
    dc = np.clip(c[None, :] - c[:, None] + (NA_COLS - 1), 0, 2 * NA_COLS - 2)
    t1 = jnp.where(col_mask[None, None], rpb[:, :, dc], NEG_BIG)
    per_shift = []
    for sh in range(NA_ROWS):
        w = t1[:, sh:sh + NA_ROWS]
        w = jnp.transpose(w, (0, 2, 1, 3)).reshape(N_HEADS_B // 2, 2 * GRID_W, NAT_KEYS)
        per_shift.append(w)
    return jnp.stack(per_shift, axis=1).astype(jnp.float32)


def _nat_kernel(q_ref, kp_ref, km_ref, kn_ref, vp_ref, vm_ref, vn_ref, tb_ref, o_ref, kcat, vcat,
                *, rows_seq):
    i = pl.program_id(1)
    halo = NAT_HALO_ROWS * GRID_W
    main = NAT_ROWS_PER_STEP * GRID_W
    kcat[0:halo] = kp_ref[...]
    kcat[halo:halo + main] = km_ref[...]
    kcat[halo + main:2 * halo + main] = kn_ref[...]
    vcat[0:halo] = vp_ref[...]
    vcat[halo:halo + main] = vm_ref[...]
    vcat[halo + main:2 * halo + main] = vn_ref[...]
    lo = lax.broadcasted_iota(jnp.int32, (1, LANES), 1) < HEAD_DIM
    zero = jnp.zeros((), jnp.bfloat16)
    r0 = i * NAT_ROWS_PER_STEP

    def row_body(qr, carry):
        r = r0 + qr
        rs = jnp.clip(r - NA_ROWS // 2, 0, rows_seq - NA_ROWS)
        koff = pl.multiple_of((rs - r0 + NAT_HALO_ROWS) * GRID_W, GRID_W)
        sh = rs - r + (NA_ROWS - 1)
        qoff = pl.multiple_of(qr * GRID_W, GRID_W)
        for p in range(N_HEADS_B // 2):
            cols = slice(LANES * p, LANES * (p + 1))
            qp = q_ref[pl.ds(qoff, GRID_W), cols]
            qm = jnp.concatenate([jnp.where(lo, qp, zero), jnp.where(lo, zero, qp)], axis=0)
            kw = kcat[pl.ds(koff, NAT_KEYS), cols]
            vw = vcat[pl.ds(koff, NAT_KEYS), cols]
            s = lax.dot_general(qm, kw, (((1,), (1,)), ((), ())), preferred_element_type=jnp.float32)
            s = s + tb_ref[p, sh]
            m = jnp.max(s, axis=-1, keepdims=True)
            pe = jnp.exp(s - m)
            l = jnp.sum(pe, axis=-1, keepdims=True)
            o2 = jnp.dot(pe.astype(jnp.bfloat16), vw, preferred_element_type=jnp.float32)
            o2 = o2 * (1.0 / l)
            o_ref[pl.ds(qoff, GRID_W), cols] = jnp.where(lo, o2[:GRID_W], o2[GRID_W:]).astype(jnp.bfloat16)
        return carry

    lax.fori_loop(0, NAT_ROWS_PER_STEP, row_body, 0)


def _nat_attention(qkv, tb, bsz, t):
    n = bsz * t
    rows_seq = t // GRID_W
    main = NAT_ROWS_PER_STEP * GRID_W
    halo = NAT_HALO_ROWS * GRID_W
    ntile = t // main
    nhalo_seq = t // halo
    per = main // halo

    def main_map(col):
        return lambda b, i: (b * ntile + i, col)

    def prev_map(col):
        return lambda b, i: (b * nhalo_seq + jnp.maximum(i * per - 1, 0), col)

    def next_map(col):
        return lambda b, i: (b * nhalo_seq + jnp.minimum(i * per + per, nhalo_seq - 1), col)

    qb, kb, vb = QB_COL // WIDTH_B, KB_COL // WIDTH_B, VB_COL // WIDTH_B
    return pl.pallas_call(
        functools.partial(_nat_kernel, rows_seq=rows_seq),
        out_shape=jax.ShapeDtypeStruct((n, WIDTH_B), jnp.bfloat16),
        grid=(bsz, ntile),
        in_specs=[
            pl.BlockSpec((main, WIDTH_B), main_map(qb)),
            pl.BlockSpec((halo, WIDTH_B), prev_map(kb)),
            pl.BlockSpec((main, WIDTH_B), main_map(kb)),
            pl.BlockSpec((halo, WIDTH_B), next_map(kb)),
            pl.BlockSpec((halo, WIDTH_B), prev_map(vb)),
            pl.BlockSpec((main, WIDTH_B), main_map(vb)),
            pl.BlockSpec((halo, WIDTH_B), next_map(vb)),
            pl.BlockSpec((N_HEADS_B // 2, NA_ROWS, 2 * GRID_W, NAT_KEYS), lambda b, i: (0, 0, 0, 0)),
        ],
        out_specs=pl.BlockSpec((main, WIDTH_B), main_map(0)),
        scratch_shapes=[pltpu.VMEM((main + 2 * halo, WIDTH_B), jnp.bfloat16),
                        pltpu.VMEM((main + 2 * halo, WIDTH_B), jnp.bfloat16)],
        compiler_params=_cparams("parallel", "parallel"),
        name="nat",
    )(qkv, qkv, qkv, qkv, qkv, qkv, qkv, tb)


EXPERT_LANE0 = N_GROUPS
INFO_E1, INFO_E2, INFO_R1, INFO_R2, INFO_W1, INFO_W2 = range(6)


def _route(logits, carry, tri):
    tm = logits.shape[0]
    lane = lax.broadcasted_iota(jnp.int32, (tm, LANES), 1).astype(jnp.float32)
    none = jnp.float32(LANES)

    def first_max(sel):
        m = jnp.max(jnp.where(sel, logits, NEG_BIG), axis=-1, keepdims=True)
        idx = jnp.min(jnp.where(sel & (logits == m), lane, none), axis=-1, keepdims=True)
        return m, idx

    is_group = lane < N_GROUPS
    mg, g = first_max(is_group)
    pg_sel = 1.0 / jnp.sum(jnp.where(is_group, jnp.exp(logits - mg), 0.0), axis=-1, keepdims=True)
    lane0 = EXPERT_LANE0 + EXPERTS_PER_GROUP * g
    in_group = (lane >= lane0) & (lane < lane0 + EXPERTS_PER_GROUP)
    m1, i1 = first_max(in_group)
    m2, i2 = first_max(in_group & (lane != i1))
    e2 = jnp.exp(m2 - m1)
    w1 = pg_sel / (1.0 + e2)
    w2 = pg_sel * e2 / (1.0 + e2)

    oh1 = lane == i1
    oh2 = lane == i2
    both = (oh1 | oh2).astype(jnp.bfloat16)
    before = jnp.dot(tri, both, preferred_element_type=jnp.float32) + carry
    r1 = jnp.sum(jnp.where(oh1, before, 0.0), axis=-1, keepdims=True)
    r2 = jnp.sum(jnp.where(oh2, before, 0.0), axis=-1, keepdims=True)
    new_carry = carry + jnp.sum(both.astype(jnp.float32), axis=0, keepdims=True)

    info = jnp.zeros((tm, LANES), jnp.float32)
    for k, v in ((INFO_E1, i1 - EXPERT_LANE0), (INFO_E2, i2 - EXPERT_LANE0), (INFO_R1, r1), (INFO_R2, r2),
                 (INFO_W1, w1), (INFO_W2, w2)):
        info = jnp.where(lane == k, v, info)
    return info, new_carry


def _merge_kernel(x_ref, oa_ref, ob_ref, lng_ref, lnb_ref, wg_ref, wpa_ref, wpb_ref, wo_ref,
                  l1g_ref, l1b_ref, wr_ref, br_ref, cnt0_ref,
                  h1_ref, info_ref, cnt_ref, carry_ref, tri_ref):
    tm = x_ref.shape[0]

    @pl.when(pl.program_id(0) == 0)
    def _():
        carry_ref[...] = cnt0_ref[...]
        r = lax.broadcasted_iota(jnp.int32, (tm, tm), 0)
        c = lax.broadcasted_iota(jnp.int32, (tm, tm), 1)
        tri_ref[...] = (c < r).astype(jnp.bfloat16)

    h = _layer_norm(x_ref[...], lng_ref[...], lnb_ref[...])
    gates = jnp.dot(h.astype(jnp.bfloat16), wg_ref[...], preferred_element_type=jnp.float32)
    pa = jnp.dot(oa_ref[...], wpa_ref[...], preferred_element_type=jnp.float32)
    pb = jnp.dot(ob_ref[...], wpb_ref[...], preferred_element_type=jnp.float32)
    mixin = jax.nn.sigmoid(gates[:, :D_MODEL]) * pa + jax.nn.sigmoid(gates[:, D_MODEL:]) * pb
    mix = jnp.dot(mixin.astype(jnp.bfloat16), wo_ref[...], preferred_element_type=jnp.float32)
    h1 = _layer_norm(DEEPNORM_ALPHA * h + mix, l1g_ref[...], l1b_ref[...])
    h1_ref[...] = h1

    hi = h1.astype(jnp.bfloat16)
    lo = (h1 - hi.astype(jnp.float32)).astype(jnp.bfloat16)
    lhs = jnp.concatenate([hi, lo, hi], axis=1)
    logits = jnp.dot(lhs, wr_ref[...], preferred_element_type=jnp.float32) + br_ref[...]
    info, carry = _route(logits, carry_ref[...], tri_ref[...])
    info_ref[...] = info
    carry_ref[...] = carry
    cnt_ref[...] = carry


def _merge(x2, oa, ob, ln_g, ln_b, w_gates, w_pa, w_pb, w_o, l1g, l1b, w_r, b_r, cnt0, tm):
    n = x2.shape[0]

    def const(shape):
        return pl.BlockSpec(shape, lambda i: (0,) * len(shape))

    def rows(width):
        return pl.BlockSpec((tm, width), lambda i: (i, 0))

    return pl.pallas_call(
        _merge_kernel,
        out_shape=(jax.ShapeDtypeStruct((n, D_MODEL), jnp.float32),
                   jax.ShapeDtypeStruct((n, LANES), jnp.float32),
                   jax.ShapeDtypeStruct((1, LANES), jnp.float32)),
        grid=(n // tm,),
        in_specs=[
            rows(D_MODEL), rows(WIDTH_A), rows(WIDTH_B),
            const((1, D_MODEL)), const((1, D_MODEL)),
            const((D_MODEL, 2 * D_MODEL)),
            const((WIDTH_A, D_MODEL)), const((WIDTH_B, D_MODEL)),
            const((D_MODEL, D_MODEL)),
            const((1, D_MODEL)), const((1, D_MODEL)),
            const((3 * D_MODEL, LANES)), const((1, LANES)), const((1, LANES)),
        ],
        out_specs=(rows(D_MODEL), rows(LANES), const((1, LANES))),
        scratch_shapes=[pltpu.VMEM((1, LANES), jnp.float32), pltpu.VMEM((tm, tm), jnp.bfloat16)],
        compiler_params=_cparams("arbitrary"),
        name="merge",
    )(x2, oa, ob, ln_g, ln_b, w_gates, w_pa, w_pb, w_o, l1g, l1b, w_r, b_r, cnt0)


def _row_copy(src_ref, src_row, dst_ref, dst_row, sem):
    return pltpu.make_async_copy(src_ref.at[pl.ds(src_row, 1)], dst_ref.at[pl.ds(dst_row, 1)], sem)


def _dispatch_kernel(dest_ref, h_ref, xs_ref, sem):
    tt = h_ref.shape[0]

    def issue(t, c):
        for k in range(TOP_K):
            _row_copy(h_ref, t, xs_ref, dest_ref[0, 0, TOP_K * t + k], sem).start()
        return c

    def drain(t, c):
        for k in range(TOP_K):
            _row_copy(h_ref, 0, xs_ref, 0, sem).wait()
        return c

    lax.fori_loop(0, tt, issue, 0)
    lax.fori_loop(0, tt, drain, 0)


def _dispatch(h1, dest, tt):
    n = h1.shape[0]
    dest3 = dest.reshape(n // tt, 1, TOP_K * tt)
    return pl.pallas_call(
        _dispatch_kernel,
        out_shape=jax.ShapeDtypeStruct((TOP_K * n, D_MODEL), jnp.float32),
        grid=(n // tt,),
        in_specs=[
            pl.BlockSpec((1, 1, TOP_K * tt), lambda i: (i, 0, 0), memory_space=pltpu.SMEM),
            pl.BlockSpec((tt, D_MODEL), lambda i: (i, 0)),
        ],
        out_specs=pl.BlockSpec(memory_space=pl.ANY),
        scratch_shapes=[pltpu.SemaphoreType.DMA(())],
        compiler_params=_cparams("arbitrary"),
        name="dispatch",
    )(dest3, h1)


def _moe_plan(counts, na, bm):
    ends = jnp.cumsum(counts)
    starts = ends - counts
    first_blk = starts // bm
    tiles = jnp.where(counts > 0, (ends - 1) // bm - first_blk + 1, 0)
    item_end = jnp.cumsum(tiles)
    item_start = item_end - tiles
    total = item_end[-1]
    wmax = na // bm + N_EXPERTS - 1
    w = jnp.arange(wmax, dtype=jnp.int32)
    wc = jnp.minimum(w, total - 1)
    e = jnp.minimum(jnp.searchsorted(item_end, wc, side="right"), N_EXPERTS - 1).astype(jnp.int32)
    blk = (first_blk[e] + (wc - item_start[e])).astype(jnp.int32)
    valid = w < total
    lo = jnp.where(valid, jnp.maximum(starts[e], blk * bm), 0).astype(jnp.int32)
    hi = jnp.where(valid, jnp.minimum(ends[e], (blk + 1) * bm), 0).astype(jnp.int32)
    prev_blk = jnp.concatenate([jnp.full((1,), -1, jnp.int32), blk[:-1]])
    prev_e = jnp.concatenate([jnp.full((1,), -1, jnp.int32), e[:-1]])
    flags = (valid.astype(jnp.int32)
             + 2 * (valid & (blk != prev_blk)).astype(jnp.int32)
             + 4 * (valid & (e != prev_e)).astype(jnp.int32))
    return blk, e, lo, hi, flags


FLAG_VALID, FLAG_NEW_BLOCK, FLAG_NEW_EXPERT = 1, 2, 4


def _expert_kernel(blk_ref, e_ref, lo_ref, hi_ref, flag_ref, x_ref, wg_ref, wu_ref, wd_ref, o_ref,
                   wg_b, wu_b, wd_b):
    w = pl.program_id(0)
    bm = x_ref.shape[0]
    flags = flag_ref[w]

    @pl.when((flags & FLAG_NEW_EXPERT) != 0)
    def _():
        wg_b[...] = wg_ref[0].astype(jnp.bfloat16)
        wu_b[...] = wu_ref[0].astype(jnp.bfloat16)
        wd_b[...] = wd_ref[0].astype(jnp.bfloat16)

    @pl.when((flags & FLAG_VALID) != 0)
    def _():
        x = x_ref[...].astype(jnp.bfloat16)
        g = jnp.dot(x, wg_b[...], preferred_element_type=jnp.float32)
        u = jnp.dot(x, wu_b[...], preferred_element_type=jnp.float32)
        hmid = (jax.nn.silu(g) * u).astype(jnp.bfloat16)
        y = jnp.dot(hmid, wd_b[...], preferred_element_type=jnp.float32)
        row = blk_ref[w] * bm + lax.broadcasted_iota(jnp.int32, (bm, 1), 0)
        y = jnp.where((row >= lo_ref[w]) & (row < hi_ref[w]), y, 0.0)

        @pl.when((flags & FLAG_NEW_BLOCK) != 0)
        def _():
            o_ref[...] = y

        @pl.when((flags & FLAG_NEW_BLOCK) == 0)
        def _():
            o_ref[...] += y


def _experts(xs, plan, w_gate, w_up, w_down, bm):
    na = xs.shape[0]
    blk, e, lo, hi, flags = plan
    nitems = blk.shape[0]
    grid_spec = pltpu.PrefetchScalarGridSpec(
        num_scalar_prefetch=5,
        grid=(nitems,),
        in_specs=[
            pl.BlockSpec((bm, D_MODEL), lambda w, blk, e, *_: (blk[w], 0)),
            pl.BlockSpec((1, D_MODEL, D_EXPERT), lambda w, blk, e, *_: (e[w], 0, 0)),
            pl.BlockSpec((1, D_MODEL, D_EXPERT), lambda w, blk, e, *_: (e[w], 0, 0)),
            pl.BlockSpec((1, D_EXPERT, D_MODEL), lambda w, blk, e, *_: (e[w], 0, 0)),
        ],
        out_specs=pl.BlockSpec((bm, D_MODEL), lambda w, blk, e, *_: (blk[w], 0)),
        scratch_shapes=[pltpu.VMEM((D_MODEL, D_EXPERT), jnp.bfloat16),
                        pltpu.VMEM((D_MODEL, D_EXPERT), jnp.bfloat16),
                        pltpu.VMEM((D_EXPERT, D_MODEL), jnp.bfloat16)],
    )
    return pl.pallas_call(
        _expert_kernel,
        out_shape=jax.ShapeDtypeStruct((na, D_MODEL), jnp.float32),
        grid_spec=grid_spec,
        compiler_params=_cparams("arbitrary"),
        name="experts",
    )(blk, e, lo, hi, flags, xs, w_gate, w_up, w_down)


def _combine_kernel(dest_ref, h1_ref, info_ref, g_ref, b_ref, ys_ref, o_ref, ybuf, sem):
    tt = h1_ref.shape[0]

    def issue(t, c):
        for k in range(TOP_K):
            _row_copy(ys_ref, dest_ref[0, 0, TOP_K * t + k], ybuf.at[k], t, sem).start()
        return c

    def drain(t, c):
        for k in range(TOP_K):
            _row_copy(ys_ref, 0, ybuf.at[k], 0, sem).wait()
        return c

    lax.fori_loop(0, tt, issue, 0)
    lax.fori_loop(0, tt, drain, 0)
    info = info_ref[...]
    moe = ybuf[0] * info[:, INFO_W1:INFO_W1 + 1] + ybuf[1] * info[:, INFO_W2:INFO_W2 + 1]
    o_ref[...] = _layer_norm(DEEPNORM_ALPHA * h1_ref[...] + moe, g_ref[...], b_ref[...])


def _combine(h1, info, dest, ys, ln_g, ln_b, tt):
    n = h1.shape[0]
    dest3 = dest.reshape(n // tt, 1, TOP_K * tt)
    return pl.pallas_call(
        _combine_kernel,
        out_shape=jax.ShapeDtypeStruct((n, D_MODEL), jnp.float32),
        grid=(n // tt,),
        in_specs=[
            pl.BlockSpec((1, 1, TOP_K * tt), lambda i: (i, 0, 0), memory_space=pltpu.SMEM),
            pl.BlockSpec((tt, D_MODEL), lambda i: (i, 0)),
            pl.BlockSpec((tt, LANES), lambda i: (i, 0)),
            pl.BlockSpec((1, D_MODEL), lambda i: (0, 0)),
            pl.BlockSpec((1, D_MODEL), lambda i: (0, 0)),
            pl.BlockSpec(memory_space=pl.ANY),
        ],
        out_specs=pl.BlockSpec((tt, D_MODEL), lambda i: (i, 0)),
        scratch_shapes=[pltpu.VMEM((TOP_K, tt, D_MODEL), jnp.float32), pltpu.SemaphoreType.DMA(())],
        compiler_params=_cparams("arbitrary"),
        name="combine",
    )(dest3, h1, info, ln_g, ln_b, ys)


TM_QKV = 512
TQ_WIN = 512
TM_MERGE = 256
TT_ROWS = 256
BM_EXPERT = 256


def _prepare_weights(ln_in_g, ln_in_b, w_in, attn_sink, rel_pos_bias, w_proj_a, w_proj_b, w_out,
                     ln1_g, ln1_b, w_route_group, b_route_group, w_route_expert, b_route_expert,
                     ln2_g, ln2_b):
    bf = jnp.bfloat16
    w = w_in[0]
    splits = np.cumsum([WIDTH_A, KV_WIDTH_A, KV_WIDTH_A, WIDTH_B, WIDTH_B, WIDTH_B, D_MODEL])
    wqa, wka, wva, wqb, wkb, wvb, wga, wgb = jnp.split(w, [int(s) for s in splits], axis=1)
    wqa = wqa.reshape(D_MODEL, N_HEADS_A, HEAD_DIM)[:, PERM_A].reshape(D_MODEL, WIDTH_A)
    w_qkv = jnp.concatenate([wqa, wqb, wkb, wvb, wka, wva], axis=1).astype(bf)
    w_gates = jnp.concatenate([wga, wgb], axis=1).astype(bf)
    w_pa = w_proj_a[0].reshape(N_HEADS_A, HEAD_DIM, D_MODEL)[PERM_A].reshape(WIDTH_A, D_MODEL).astype(bf)
    w_pb = w_proj_b[0].astype(bf)
    w_o = w_out[0].astype(bf)
    pad = LANES - N_GROUPS - N_EXPERTS
    w_r = jnp.concatenate([w_route_group[0], w_route_expert[0], jnp.zeros((D_MODEL, pad), jnp.float32)], axis=1)
    w_r_hi = w_r.astype(bf)
    w_r_lo = (w_r - w_r_hi.astype(jnp.float32)).astype(bf)
    w_r3 = jnp.concatenate([w_r_hi, w_r_hi, w_r_lo], axis=0)
    b_r = jnp.concatenate([b_route_group[0], b_route_expert[0], jnp.zeros((pad,), jnp.float32)])[None, :]
    row = lambda v: v.reshape(1, D_MODEL)
    return dict(
        ln_in_g=row(ln_in_g), ln_in_b=row(ln_in_b), w_qkv=w_qkv, w_gates=w_gates,
        sink=attn_sink[0].astype(jnp.float32), nat_bias=_nat_bias_table(rel_pos_bias[0]),
        w_pa=w_pa, w_pb=w_pb, w_o=w_o, ln1_g=row(ln1_g[0]), ln1_b=row(ln1_b[0]),
        w_r3=w_r3, b_r=b_r, ln2_g=row(ln2_g[0]), ln2_b=row(ln2_b[0]))


def _encode(x, p, w_gate, w_up, w_down):
    bsz, t, _ = x.shape
    n = bsz * t
    x2 = x.reshape(n, D_MODEL)
    qkv = _qkv(x2, p["ln_in_g"], p["ln_in_b"], p["w_qkv"], TM_QKV)
    oa = _win_attention(qkv, p["sink"], bsz, t, TQ_WIN)
    ob = _nat_attention(qkv, p["nat_bias"], bsz, t)
    cnt0 = jnp.zeros((1, LANES), jnp.float32)
    h1, info, cnt = _merge(x2, oa, ob, p["ln_in_g"], p["ln_in_b"], p["w_gates"], p["w_pa"], p["w_pb"],
                           p["w_o"], p["ln1_g"], p["ln1_b"], p["w_r3"], p["b_r"], cnt0, TM_MERGE)
    counts = cnt[0, EXPERT_LANE0:EXPERT_LANE0 + N_EXPERTS].astype(jnp.int32)
    starts = jnp.cumsum(counts) - counts
    eid = info[:, INFO_E1:INFO_E2 + 1].astype(jnp.int32)
    rank = info[:, INFO_R1:INFO_R2 + 1].astype(jnp.int32)
    dest = starts[eid] + rank
    xs = _dispatch(h1, dest, TT_ROWS)
    plan = _moe_plan(counts, TOP_K * n, BM_EXPERT)
    ys = _experts(xs, plan, w_gate[0], w_up[0], w_down[0], BM_EXPERT)
    out = _combine(h1, info, dest, ys, p["ln2_g"], p["ln2_b"], TT_ROWS)
    return out.reshape(bsz, t, D_MODEL)


def kernel(x_prompt, x_sample, ln_in_g, ln_in_b, w_in, attn_sink, rel_pos_bias, w_proj_a, w_proj_b, w_out,
           ln1_g, ln1_b, w_route_group, b_route_group, w_route_expert, b_route_expert,
           w_gate, w_up, w_down, ln2_g, ln2_b):
    p = _prepare_weights(ln_in_g, ln_in_b, w_in, attn_sink, rel_pos_bias, w_proj_a, w_proj_b, w_out,
                         ln1_g, ln1_b, w_route_group, b_route_group, w_route_expert, b_route_expert,
                         ln2_g, ln2_b)
    return (_encode(x_prompt, p, w_gate, w_up, w_down), _encode(x_sample, p, w_gate, w_up, w_down))
```

```python
import functools

import numpy as np
import jax
import jax.numpy as jnp
from jax import lax
from jax.experimental import pallas as pl
from jax.experimental.pallas import tpu as pltpu

D_MODEL = 1024
HEAD_DIM = 64
N_HEADS_A = 8
N_KV_HEADS_A = 2
WINDOW = 128
N_HEADS_B = 8
GRID_W = 64
NA_ROWS = 8
NA_COLS = 16
N_GROUPS = 4
EXPERTS_PER_GROUP = 8
N_EXPERTS = N_GROUPS * EXPERTS_PER_GROUP
TOP_K = 2
D_EXPERT = D_MODEL // 2
LN_EPS = 1e-5
DEPTH = 1
DEEPNORM_ALPHA = (2.0 * DEPTH) ** 0.25
WIDTH_A = N_HEADS_A * HEAD_DIM
KV_WIDTH_A = N_KV_HEADS_A * HEAD_DIM
WIDTH_B = N_HEADS_B * HEAD_DIM
QKV_WIDTH = WIDTH_A + 2 * KV_WIDTH_A + 3 * WIDTH_B

LANES = 128
VMEM_LIMIT_BYTES = 56 * 1024 * 1024

NEG_BIG = -1e30

QA_COL, QB_COL, KB_COL, VB_COL = 0, WIDTH_A, WIDTH_A + WIDTH_B, WIDTH_A + 2 * WIDTH_B
KA_COL = WIDTH_A + 3 * WIDTH_B
VA_COL = KA_COL + KV_WIDTH_A

PERM_A = np.array([0, 4, 1, 5, 2, 6, 3, 7])


def _cparams(*sem):
    return pltpu.CompilerParams(dimension_semantics=sem, vmem_limit_bytes=VMEM_LIMIT_BYTES)


def _layer_norm(x, g, b):
    mu = jnp.mean(x, axis=-1, keepdims=True)
    xc = x - mu
    var = jnp.mean(xc * xc, axis=-1, keepdims=True)
    return xc * lax.rsqrt(var + LN_EPS) * g + b


def _qkv_kernel(x_ref, g_ref, b_ref, w_ref, o_ref):
    h = _layer_norm(x_ref[...], g_ref[...], b_ref[...])
    y = jnp.dot(h.astype(jnp.bfloat16), w_ref[...], preferred_element_type=jnp.float32)
    col = lax.broadcasted_iota(jnp.int32, (1, QKV_WIDTH), 1)
    y = y * jnp.where(col < KB_COL, HEAD_DIM ** -0.5, 1.0)
    o_ref[...] = y.astype(jnp.bfloat16)


def _qkv(x2, ln_g, ln_b, w_qkv, tm):
    n = x2.shape[0]
    return pl.pallas_call(
        _qkv_kernel,
        out_shape=jax.ShapeDtypeStruct((n, QKV_WIDTH), jnp.bfloat16),
        grid=(n // tm,),
        in_specs=[
            pl.BlockSpec((tm, D_MODEL), lambda i: (i, 0)),
            pl.BlockSpec((1, D_MODEL), lambda i: (0, 0)),
            pl.BlockSpec((1, D_MODEL), lambda i: (0, 0)),
            pl.BlockSpec((D_MODEL, QKV_WIDTH), lambda i: (0, 0)),
        ],
        out_specs=pl.BlockSpec((tm, QKV_WIDTH), lambda i: (i, 0)),
        compiler_params=_cparams("parallel"),
        name="qkv",
    )(x2, ln_g, ln_b, w_qkv)


WIN_BLK = 128


def _win_bias_table():
    qi = np.arange(WIN_BLK)[:, None]
    kj = np.arange(3 * WIN_BLK)[None, :]
    dist = np.abs(kj - WIN_BLK - qi).astype(np.float64)
    slopes = 2.0 ** (-8.0 * np.arange(1, N_HEADS_A + 1) / N_HEADS_A)
    per_head = np.where(dist <= WINDOW, -slopes[:, None, None] * dist[None], NEG_BIG)
    groups = [np.concatenate([per_head[j], per_head[j + 4]], axis=0) for j in range(4)]
    return np.stack(groups).astype(np.float32)


def _win_kernel(sink_ref, q_ref, kp_ref, km_ref, kn_ref, vp_ref, vm_ref, vn_ref, bias_ref, o_ref,
                *, nsub, nblk_seq):
    i = pl.program_id(1)
    kcat = jnp.concatenate([kp_ref[...], km_ref[...], kn_ref[...]], axis=0)
    vcat = jnp.concatenate([vp_ref[...], vm_ref[...], vn_ref[...]], axis=0)
    lo = lax.broadcasted_iota(jnp.int32, (1, LANES), 1) < HEAD_DIM
    col = lax.broadcasted_iota(jnp.int32, (1, 3 * WIN_BLK), 1)
    top = lax.broadcasted_iota(jnp.int32, (2 * WIN_BLK, 1), 0) < WIN_BLK
    zero = jnp.zeros((), jnp.bfloat16)
    for j in range(nsub):
        n = i * nsub + j
        off_seq = ((col < WIN_BLK) & (n == 0)) | ((col >= 2 * WIN_BLK) & (n == nblk_seq - 1))
        edge = jnp.where(off_seq, NEG_BIG, 0.0)
        kj = kcat[WIN_BLK * j:WIN_BLK * (j + 3)]
        vj = vcat[WIN_BLK * j:WIN_BLK * (j + 3)]
        for g in range(4):
            rows = slice(WIN_BLK * j, WIN_BLK * (j + 1))
            cols = slice(LANES * g, LANES * (g + 1))
            qg = q_ref[rows, cols]
            qm = jnp.concatenate([jnp.where(lo, qg, zero), jnp.where(lo, zero, qg)], axis=0)
            s = lax.dot_general(qm, kj, (((1,), (1,)), ((), ())), preferred_element_type=jnp.float32)
            s = s + bias_ref[g] + edge
            sink = jnp.where(top, sink_ref[g], sink_ref[g + 4])
            m = jnp.maximum(jnp.max(s, axis=-1, keepdims=True), sink)
            p = jnp.exp(s - m)
            l = jnp.sum(p, axis=-1, keepdims=True) + jnp.exp(sink - m)
            o2 = jnp.dot(p.astype(jnp.bfloat16), vj, preferred_element_type=jnp.float32)
            o2 = o2 * (1.0 / l)
            o_ref[rows, cols] = jnp.where(lo, o2[:WIN_BLK], o2[WIN_BLK:]).astype(jnp.bfloat16)


def _win_attention(qkv, sink, bsz, t, tq):
    n = bsz * t
    nsub = tq // WIN_BLK
    nblk_seq = t // WIN_BLK
    ntile = t // tq
    bias = jnp.asarray(_win_bias_table())

    def main_map(col):
        return lambda b, i, *_: (b * ntile + i, col)

    def prev_map(col):
        return lambda b, i, *_: (b * nblk_seq + jnp.maximum(i * nsub - 1, 0), col)

    def next_map(col):
        return lambda b, i, *_: (b * nblk_seq + jnp.minimum(i * nsub + nsub, nblk_seq - 1), col)

    halo = (WIN_BLK, LANES)
    ka, va = KA_COL // LANES, VA_COL // LANES
    grid_spec = pltpu.PrefetchScalarGridSpec(
        num_scalar_prefetch=1,
        grid=(bsz, ntile),
        in_specs=[
            pl.BlockSpec((tq, WIDTH_A), main_map(QA_COL // WIDTH_A)),
            pl.BlockSpec(halo, prev_map(ka)),
            pl.BlockSpec((tq, LANES), main_map(ka)),
            pl.BlockSpec(halo, next_map(ka)),
            pl.BlockSpec(halo, prev_map(va)),
            pl.BlockSpec((tq, LANES), main_map(va)),
            pl.BlockSpec(halo, next_map(va)),
            pl.BlockSpec((4, 2 * WIN_BLK, 3 * WIN_BLK), lambda b, i, *_: (0, 0, 0)),
        ],
        out_specs=pl.BlockSpec((tq, WIDTH_A), main_map(0)),
    )
    return pl.pallas_call(
        functools.partial(_win_kernel, nsub=nsub, nblk_seq=nblk_seq),
        out_shape=jax.ShapeDtypeStruct((n, WIDTH_A), jnp.bfloat16),
        grid_spec=grid_spec,
        compiler_params=_cparams("parallel", "parallel"),
        name="win",
    )(sink, qkv, qkv, qkv, qkv, qkv, qkv, qkv, bias)


NAT_ROWS_PER_STEP = 8
NAT_HALO_ROWS = NA_ROWS // 2
NAT_KEYS = NA_ROWS * GRID_W


def _nat_bias_table(rpb):
    c = np.arange(GRID_W)
    cs = np.clip(c - NA_COLS // 2, 0, GRID_W - NA_COLS)
    col_mask = (c[None, :] >= cs[:, None]) & (c[None, :] < cs[:, None] + NA_COLS)
    dc = np.clip(c[None, :] - c[:, None] + (NA_COLS - 1), 0, 2 * NA_COLS - 2)
    t1 = jnp.where(col_mask[None, None], rpb[:, :, dc], NEG_BIG)
    per_shift = []
    for sh in range(NA_ROWS):
        w = t1[:, sh:sh + NA_ROWS]
        w = jnp.transpose(w, (0, 2, 1, 3)).reshape(N_HEADS_B // 2, 2 * GRID_W, NAT_KEYS)
        per_shift.append(w)
    return jnp.stack(per_shift, axis=1).astype(jnp.float32)


def _nat_kernel(q_ref, kp_ref, km_ref, kn_ref, vp_ref, vm_ref, vn_ref, tb_ref, o_ref, kcat, vcat,
                *, rows_seq):
    i = pl.program_id(1)
    halo = NAT_HALO_ROWS * GRID_W
    main = NAT_ROWS_PER_STEP * GRID_W
    kcat[0:halo] = kp_ref[...]
    kcat[halo:halo + main] = km_ref[...]
    kcat[halo + main:2 * halo + main] = kn_ref[...]
    vcat[0:halo] = vp_ref[...]
    vcat[halo:halo + main] = vm_ref[...]
    vcat[halo + main:2 * halo + main] = vn_ref[...]
    lo = lax.broadcasted_iota(jnp.int32, (1, LANES), 1) < HEAD_DIM
    zero = jnp.zeros((), jnp.bfloat16)
    r0 = i * NAT_ROWS_PER_STEP

    def row_body(qr, carry):
        r = r0 + qr
        rs = jnp.clip(r - NA_ROWS // 2, 0, rows_seq - NA_ROWS)
        koff = pl.multiple_of((rs - r0 + NAT_HALO_ROWS) * GRID_W, GRID_W)
        sh = rs - r + (NA_ROWS - 1)
        qoff = pl.multiple_of(qr * GRID_W, GRID_W)
        for p in range(N_HEADS_B // 2):
            cols = slice(LANES * p, LANES * (p + 1))
            qp = q_ref[pl.ds(qoff, GRID_W), cols]
            qm = jnp.concatenate([jnp.where(lo, qp, zero), jnp.where(lo, zero, qp)], axis=0)
            kw = kcat[pl.ds(koff, NAT_KEYS), cols]
            vw = vcat[pl.ds(koff, NAT_KEYS), cols]
            s = lax.dot_general(qm, kw, (((1,), (1,)), ((), ())), preferred_element_type=jnp.float32)
            s = s + tb_ref[p, sh]
            m = jnp.max(s, axis=-1, keepdims=True)
            pe = jnp.exp(s - m)
            l = jnp.sum(pe, axis=-1, keepdims=True)
            o2 = jnp.dot(pe.astype(jnp.bfloat16), vw, preferred_element_type=jnp.float32)
            o2 = o2 * (1.0 / l)
            o_ref[pl.ds(qoff, GRID_W), cols] = jnp.where(lo, o2[:GRID_W], o2[GRID_W:]).astype(jnp.bfloat16)
        return carry

    lax.fori_loop(0, NAT_ROWS_PER_STEP, row_body, 0, unroll=2)


def _nat_attention(qkv, tb, bsz, t):
    n = bsz * t
    rows_seq = t // GRID_W
    main = NAT_ROWS_PER_STEP * GRID_W
    halo = NAT_HALO_ROWS * GRID_W
    ntile = t // main
    nhalo_seq = t // halo
    per = main // halo

    def main_map(col):
        return lambda b, i: (b * ntile + i, col)

    def prev_map(col):
        return lambda b, i: (b * nhalo_seq + jnp.maximum(i * per - 1, 0), col)

    def next_map(col):
        return lambda b, i: (b * nhalo_seq + jnp.minimum(i * per + per, nhalo_seq - 1), col)

    qb, kb, vb = QB_COL // WIDTH_B, KB_COL // WIDTH_B, VB_COL // WIDTH_B
    return pl.pallas_call(
        functools.partial(_nat_kernel, rows_seq=rows_seq),
        out_shape=jax.ShapeDtypeStruct((n, WIDTH_B), jnp.bfloat16),
        grid=(bsz, ntile),
        in_specs=[
            pl.BlockSpec((main, WIDTH_B), main_map(qb)),
            pl.BlockSpec((halo, WIDTH_B), prev_map(kb)),
            pl.BlockSpec((main, WIDTH_B), main_map(kb)),
            pl.BlockSpec((halo, WIDTH_B), next_map(kb)),
            pl.BlockSpec((halo, WIDTH_B), prev_map(vb)),
            pl.BlockSpec((main, WIDTH_B), main_map(vb)),
            pl.BlockSpec((halo, WIDTH_B), next_map(vb)),
            pl.BlockSpec((N_HEADS_B // 2, NA_ROWS, 2 * GRID_W, NAT_KEYS), lambda b, i: (0, 0, 0, 0)),
        ],
        out_specs=pl.BlockSpec((main, WIDTH_B), main_map(0)),
        scratch_shapes=[pltpu.VMEM((main + 2 * halo, WIDTH_B), jnp.bfloat16),
                        pltpu.VMEM((main + 2 * halo, WIDTH_B), jnp.bfloat16)],
        compiler_params=_cparams("parallel", "parallel"),
        name="nat",
    )(qkv, qkv, qkv, qkv, qkv, qkv, qkv, tb)


EXPERT_ROW0 = N_GROUPS
ROUTE_ROWS = 48
INFO_E1, INFO_E2, INFO_R1, INFO_R2, INFO_W1, INFO_W2 = range(6)
INFO_ROWS = 8


def _route(lt, carry, tri):
    rr, tm = lt.shape
    row = lax.broadcasted_iota(jnp.int32, (rr, tm), 0).astype(jnp.float32)
    none = jnp.float32(rr)

    def first_max(sel):
        m = jnp.max(jnp.where(sel, lt, NEG_BIG), axis=0, keepdims=True)
        idx = jnp.min(jnp.where(sel & (lt == m), row, none), axis=0, keepdims=True)
        return m, idx

    is_group = row < N_GROUPS
    mg, g = first_max(is_group)
    pg_sel = 1.0 / jnp.sum(jnp.where(is_group, jnp.exp(jnp.where(is_group, lt, mg) - mg), 0.0),
                           axis=0, keepdims=True)
    row0 = EXPERT_ROW0 + EXPERTS_PER_GROUP * g
    in_group = (row >= row0) & (row < row0 + EXPERTS_PER_GROUP)
    m1, i1 = first_max(in_group)
    m2, i2 = first_max(in_group & (row != i1))
    e2 = jnp.exp(m2 - m1)
    w1 = pg_sel / (1.0 + e2)
    w2 = pg_sel * e2 / (1.0 + e2)

    oh1 = row == i1
    oh2 = row == i2
    both = (oh1 | oh2).astype(jnp.bfloat16)
    before = jnp.dot(both, tri, preferred_element_type=jnp.float32) + carry
    r1 = jnp.sum(jnp.where(oh1, before, 0.0), axis=0, keepdims=True)
    r2 = jnp.sum(jnp.where(oh2, before, 0.0), axis=0, keepdims=True)
    new_carry = carry + jnp.sum(both.astype(jnp.float32), axis=1, keepdims=True)

    field = lax.broadcasted_iota(jnp.int32, (INFO_ROWS, tm), 0)
    info = jnp.zeros((INFO_ROWS, tm), jnp.float32)
    for k, v in ((INFO_E1, i1 - EXPERT_ROW0), (INFO_E2, i2 - EXPERT_ROW0), (INFO_R1, r1), (INFO_R2, r2),
                 (INFO_W1, w1), (INFO_W2, w2)):
        info = jnp.where(field == k, v, info)
    return info, new_carry


def _merge_kernel(x_ref, oa_ref, ob_ref, lng_ref, lnb_ref, wg_ref, wpa_ref, wpb_ref, wo_ref,
                  l1g_ref, l1b_ref, wr_ref, br_ref, cnt0_ref,
                  h1_ref, info_ref, cnt_ref, carry_ref, tri_ref):
    tm = x_ref.shape[0]

    @pl.when(pl.program_id(0) == 0)
    def _():
        carry_ref[...] = cnt0_ref[...]
        r = lax.broadcasted_iota(jnp.int32, (tm, tm), 0)
        c = lax.broadcasted_iota(jnp.int32, (tm, tm), 1)
        tri_ref[...] = (r < c).astype(jnp.bfloat16)

    h = _layer_norm(x_ref[...], lng_ref[...], lnb_ref[...])
    gates = jnp.dot(h.astype(jnp.bfloat16), wg_ref[...], preferred_element_type=jnp.float32)
    pa = jnp.dot(oa_ref[...], wpa_ref[...], preferred_element_type=jnp.float32)
    pb = jnp.dot(ob_ref[...], wpb_ref[...], preferred_element_type=jnp.float32)
    mixin = jax.nn.sigmoid(gates[:, :D_MODEL]) * pa + jax.nn.sigmoid(gates[:, D_MODEL:]) * pb
    mix = jnp.dot(mixin.astype(jnp.bfloat16), wo_ref[...], preferred_element_type=jnp.float32)
    h1 = _layer_norm(DEEPNORM_ALPHA * h + mix, l1g_ref[...], l1b_ref[...])
    h1_ref[...] = h1

    hi = h1.astype(jnp.bfloat16)
    lo = (h1 - hi.astype(jnp.float32)).astype(jnp.bfloat16)
    lhs = jnp.concatenate([hi, lo, hi], axis=1)
    logits_t = lax.dot_general(wr_ref[...], lhs, (((1,), (1,)), ((), ())),
                               preferred_element_type=jnp.float32) + br_ref[...]
    info, carry = _route(logits_t, carry_ref[...], tri_ref[...])
    info_ref[...] = info
    carry_ref[...] = carry
    cnt_ref[...] = carry[:, :LANES]


def _merge(x2, oa, ob, ln_g, ln_b, w_gates, w_pa, w_pb, w_o, l1g, l1b, w_r, b_r, cnt0, tm):
    n = x2.shape[0]

    def const(shape):
        return pl.BlockSpec(shape, lambda i: (0,) * len(shape))

    def rows(width):
        return pl.BlockSpec((tm, width), lambda i: (i, 0))

    return pl.pallas_call(
        _merge_kernel,
        out_shape=(jax.ShapeDtypeStruct((n, D_MODEL), jnp.float32),
                   jax.ShapeDtypeStruct((INFO_ROWS, n), jnp.float32),
                   jax.ShapeDtypeStruct((ROUTE_ROWS, LANES), jnp.float32)),
        grid=(n // tm,),
        in_specs=[
            rows(D_MODEL), rows(WIDTH_A), rows(WIDTH_B),
            const((1, D_MODEL)), const((1, D_MODEL)),
            const((D_MODEL, 2 * D_MODEL)),
            const((WIDTH_A, D_MODEL)), const((WIDTH_B, D_MODEL)),
            const((D_MODEL, D_MODEL)),
            const((1, D_MODEL)), const((1, D_MODEL)),
            const((ROUTE_ROWS, 3 * D_MODEL)), const((ROUTE_ROWS, tm)), const((ROUTE_ROWS, tm)),
        ],
        out_specs=(rows(D_MODEL), pl.BlockSpec((INFO_ROWS, tm), lambda i: (0, i)),
                   const((ROUTE_ROWS, LANES))),
        scratch_shapes=[pltpu.VMEM((ROUTE_ROWS, tm), jnp.float32), pltpu.VMEM((tm, tm), jnp.bfloat16)],
        compiler_params=_cparams("arbitrary"),
        name="merge",
    )(x2, oa, ob, ln_g, ln_b, w_gates, w_pa, w_pb, w_o, l1g, l1b, w_r, b_r, cnt0)


def _row_copy(src_ref, src_row, dst_ref, dst_row, sem):
    return pltpu.make_async_copy(src_ref.at[pl.ds(src_row, 1)], dst_ref.at[pl.ds(dst_row, 1)], sem)


ROW_DMA_UNROLL = 8


def _tile_dest(dest_t, tt):
    n = dest_t.shape[1]
    return dest_t.reshape(TOP_K, n // tt, tt).transpose(1, 0, 2).reshape(n // tt, 1, TOP_K * tt)


def _dispatch_kernel(dest_ref, h_ref, xs_ref, sem):
    tt = h_ref.shape[0]

    def issue(t, c):
        for k in range(TOP_K):
            _row_copy(h_ref, t, xs_ref, dest_ref[0, 0, k * tt + t], sem).start()
        return c

    lax.fori_loop(0, tt, issue, 0, unroll=ROW_DMA_UNROLL)
    for k in range(TOP_K):
        pltpu.make_async_copy(h_ref, xs_ref.at[pl.ds(0, tt)], sem).wait()


def _dispatch(h1, dest_t, tt):
    n = h1.shape[0]
    dest3 = _tile_dest(dest_t, tt)
    return pl.pallas_call(
        _dispatch_kernel,
        out_shape=jax.ShapeDtypeStruct((TOP_K * n, D_MODEL), jnp.float32),
        grid=(n // tt,),
        in_specs=[
            pl.BlockSpec((1, 1, TOP_K * tt), lambda i: (i, 0, 0), memory_space=pltpu.SMEM),
            pl.BlockSpec((tt, D_MODEL), lambda i: (i, 0)),
        ],
        out_specs=pl.BlockSpec(memory_space=pl.ANY),
        scratch_shapes=[pltpu.SemaphoreType.DMA(())],
        compiler_params=_cparams("arbitrary"),
        name="dispatch",
    )(dest3, h1)


def _moe_plan(counts, na, bm):
    ends = jnp.cumsum(counts)
    starts = ends - counts
    first_blk = starts // bm
    tiles = jnp.where(counts > 0, (ends - 1) // bm - first_blk + 1, 0)
    item_end = jnp.cumsum(tiles)
    item_start = item_end - tiles
    total = item_end[-1]
    wmax = na // bm + N_EXPERTS - 1
    w = jnp.arange(wmax, dtype=jnp.int32)
    wc = jnp.minimum(w, total - 1)
    e = jnp.sum((item_end[None, :] <= wc[:, None]).astype(jnp.int32), axis=1)
    e = jnp.minimum(e, N_EXPERTS - 1)
    blk = (first_blk[e] + (wc - item_start[e])).astype(jnp.int32)
    valid = w < total
    lo = jnp.where(valid, jnp.maximum(starts[e], blk * bm), 0).astype(jnp.int32)
    hi = jnp.where(valid, jnp.minimum(ends[e], (blk + 1) * bm), 0).astype(jnp.int32)
    prev_blk = jnp.concatenate([jnp.full((1,), -1, jnp.int32), blk[:-1]])
    prev_e = jnp.concatenate([jnp.full((1,), -1, jnp.int32), e[:-1]])
    flags = (valid.astype(jnp.int32)
             + 2 * (valid & (blk != prev_blk)).astype(jnp.int32)
             + 4 * (valid & (e != prev_e)).astype(jnp.int32))
    return blk, e, lo, hi, flags


FLAG_VALID, FLAG_NEW_BLOCK, FLAG_NEW_EXPERT = 1, 2, 4


def _expert_kernel(blk_ref, e_ref, lo_ref, hi_ref, flag_ref, x_ref, wg_ref, wu_ref, wd_ref, o_ref,
                   wg_b, wu_b, wd_b):
    w = pl.program_id(0)
    bm = x_ref.shape[0]
    flags = flag_ref[w]

    @pl.when((flags & FLAG_NEW_EXPERT) != 0)
    def _():
        wg_b[...] = wg_ref[0].astype(jnp.bfloat16)
        wu_b[...] = wu_ref[0].astype(jnp.bfloat16)
        wd_b[...] = wd_ref[0].astype(jnp.bfloat16)

    @pl.when((flags & FLAG_VALID) != 0)
    def _():
        x = x_ref[...].astype(jnp.bfloat16)
        g = jnp.dot(x, wg_b[...], preferred_element_type=jnp.float32)
        u = jnp.dot(x, wu_b[...], preferred_element_type=jnp.float32)
        hmid = (jax.nn.silu(g) * u).astype(jnp.bfloat16)
        y = jnp.dot(hmid, wd_b[...], preferred_element_type=jnp.float32)
        row = blk_ref[w] * bm + lax.broadcasted_iota(jnp.int32, (bm, 1), 0)
        y = jnp.where((row >= lo_ref[w]) & (row < hi_ref[w]), y, 0.0)

        @pl.when((flags & FLAG_NEW_BLOCK) != 0)
        def _():
            o_ref[...] = y

        @pl.when((flags & FLAG_NEW_BLOCK) == 0)
        def _():
            o_ref[...] += y


def _experts(xs, plan, w_gate, w_up, w_down, bm):
    na = xs.shape[0]
    blk, e, lo, hi, flags = plan
    nitems = blk.shape[0]
    grid_spec = pltpu.PrefetchScalarGridSpec(
        num_scalar_prefetch=5,
        grid=(nitems,),
        in_specs=[
            pl.BlockSpec((bm, D_MODEL), lambda w, blk, e, *_: (blk[w], 0)),
            pl.BlockSpec((1, D_MODEL, D_EXPERT), lambda w, blk, e, *_: (e[w], 0, 0)),
            pl.BlockSpec((1, D_MODEL, D_EXPERT), lambda w, blk, e, *_: (e[w], 0, 0)),
            pl.BlockSpec((1, D_EXPERT, D_MODEL), lambda w, blk, e, *_: (e[w], 0, 0)),
        ],
        out_specs=pl.BlockSpec((bm, D_MODEL), lambda w, blk, e, *_: (blk[w], 0)),
        scratch_shapes=[pltpu.VMEM((D_MODEL, D_EXPERT), jnp.bfloat16),
                        pltpu.VMEM((D_MODEL, D_EXPERT), jnp.bfloat16),
                        pltpu.VMEM((D_EXPERT, D_MODEL), jnp.bfloat16)],
    )
    return pl.pallas_call(
        _expert_kernel,
        out_shape=jax.ShapeDtypeStruct((na, D_MODEL), jnp.float32),
        grid_spec=grid_spec,
        compiler_params=_cparams("arbitrary"),
        name="experts",
    )(blk, e, lo, hi, flags, xs, w_gate, w_up, w_down)


def _combine_kernel(dest_ref, dest_next_ref, h1_ref, info_ref, g_ref, b_ref, ys_ref, o_ref, ybuf, sem):
    i = pl.program_id(0)
    nsteps = pl.num_programs(0)
    tt = h1_ref.shape[0]
    slot = i % 2

    def gather(d_ref, s):
        def issue(t, c):
            for k in range(TOP_K):
                _row_copy(ys_ref, d_ref[0, 0, k * tt + t], ybuf.at[s, k], t, sem.at[s]).start()
            return c
        lax.fori_loop(0, tt, issue, 0, unroll=ROW_DMA_UNROLL)

    @pl.when(i == 0)
    def _():
        gather(dest_ref, 0)

    @pl.when(i + 1 < nsteps)
    def _():
        gather(dest_next_ref, 1 - slot)

    for k in range(TOP_K):
        pltpu.make_async_copy(ys_ref.at[pl.ds(0, tt)], ybuf.at[slot, k], sem.at[slot]).wait()
    pad = jnp.zeros((LANES - INFO_ROWS, tt), jnp.float32)
    info = jnp.concatenate([info_ref[...], pad], axis=0).T
    moe = ybuf[slot, 0] * info[:, INFO_W1:INFO_W1 + 1] + ybuf[slot, 1] * info[:, INFO_W2:INFO_W2 + 1]
    o_ref[...] = _layer_norm(DEEPNORM_ALPHA * h1_ref[...] + moe, g_ref[...], b_ref[...])


def _combine(h1, info, dest_t, ys, ln_g, ln_b, tt):
    n = h1.shape[0]
    nsteps = n // tt
    dest3 = _tile_dest(dest_t, tt)
    return pl.pallas_call(
        _combine_kernel,
        out_shape=jax.ShapeDtypeStruct((n, D_MODEL), jnp.float32),
        grid=(nsteps,),
        in_specs=[
            pl.BlockSpec((1, 1, TOP_K * tt), lambda i: (i, 0, 0), memory_space=pltpu.SMEM),
            pl.BlockSpec((1, 1, TOP_K * tt), lambda i: (jnp.minimum(i + 1, nsteps - 1), 0, 0),
                         memory_space=pltpu.SMEM),
            pl.BlockSpec((tt, D_MODEL), lambda i: (i, 0)),
            pl.BlockSpec((INFO_ROWS, tt), lambda i: (0, i)),
            pl.BlockSpec((1, D_MODEL), lambda i: (0, 0)),
            pl.BlockSpec((1, D_MODEL), lambda i: (0, 0)),
            pl.BlockSpec(memory_space=pl.ANY),
        ],
        out_specs=pl.BlockSpec((tt, D_MODEL), lambda i: (i, 0)),
        scratch_shapes=[pltpu.VMEM((2, TOP_K, tt, D_MODEL), jnp.float32), pltpu.SemaphoreType.DMA((2,))],
        compiler_params=_cparams("arbitrary"),
        name="combine",
    )(dest3, dest3, h1, info, ln_g, ln_b, ys)


TM_QKV = 512
TQ_WIN = 512
TM_MERGE = 512
TT_ROWS = 256
BM_EXPERT = 256


def _prepare_weights(ln_in_g, ln_in_b, w_in, attn_sink, rel_pos_bias, w_proj_a, w_proj_b, w_out,
                     ln1_g, ln1_b, w_route_group, b_route_group, w_route_expert, b_route_expert,
                     ln2_g, ln2_b):
    bf = jnp.bfloat16
    w = w_in[0]
    splits = np.cumsum([WIDTH_A, KV_WIDTH_A, KV_WIDTH_A, WIDTH_B, WIDTH_B, WIDTH_B, D_MODEL])
    wqa, wka, wva, wqb, wkb, wvb, wga, wgb = jnp.split(w, [int(s) for s in splits], axis=1)
    wqa = wqa.reshape(D_MODEL, N_HEADS_A, HEAD_DIM)[:, PERM_A].reshape(D_MODEL, WIDTH_A)
    w_qkv = jnp.concatenate([wqa, wqb, wkb, wvb, wka, wva], axis=1).astype(bf)
    w_gates = jnp.concatenate([wga, wgb], axis=1).astype(bf)
    w_pa = w_proj_a[0].reshape(N_HEADS_A, HEAD_DIM, D_MODEL)[PERM_A].reshape(WIDTH_A, D_MODEL).astype(bf)
    w_pb = w_proj_b[0].astype(bf)
    w_o = w_out[0].astype(bf)
    pad = ROUTE_ROWS - N_GROUPS - N_EXPERTS
    w_r = jnp.concatenate([w_route_group[0].T, w_route_expert[0].T, jnp.zeros((pad, D_MODEL), jnp.float32)], axis=0)
    w_r_hi = w_r.astype(bf)
    w_r_lo = (w_r - w_r_hi.astype(jnp.float32)).astype(bf)
    w_r3 = jnp.concatenate([w_r_hi, w_r_hi, w_r_lo], axis=1)
    b_r = jnp.concatenate([b_route_group[0], b_route_expert[0], jnp.zeros((pad,), jnp.float32)])
    b_r = jnp.broadcast_to(b_r[:, None], (ROUTE_ROWS, TM_MERGE))
    row = lambda v: v.reshape(1, D_MODEL)
    return dict(
        ln_in_g=row(ln_in_g), ln_in_b=row(ln_in_b), w_qkv=w_qkv, w_gates=w_gates,
        sink=attn_sink[0].astype(jnp.float32), nat_bias=_nat_bias_table(rel_pos_bias[0]),
        w_pa=w_pa, w_pb=w_pb, w_o=w_o, ln1_g=row(ln1_g[0]), ln1_b=row(ln1_b[0]),
        w_r3=w_r3, b_r=b_r, ln2_g=row(ln2_g[0]), ln2_b=row(ln2_b[0]))


def _encode(x, p, w_gate, w_up, w_down):
    bsz, t, _ = x.shape
    n = bsz * t
    x2 = x.reshape(n, D_MODEL)
    qkv = _qkv(x2, p["ln_in_g"], p["ln_in_b"], p["w_qkv"], TM_QKV)
    oa = _win_attention(qkv, p["sink"], bsz, t, TQ_WIN)
    ob = _nat_attention(qkv, p["nat_bias"], bsz, t)
    cnt0 = jnp.zeros((ROUTE_ROWS, TM_MERGE), jnp.float32)
    h1, info, cnt = _merge(x2, oa, ob, p["ln_in_g"], p["ln_in_b"], p["w_gates"], p["w_pa"], p["w_pb"],
                           p["w_o"], p["ln1_g"], p["ln1_b"], p["w_r3"], p["b_r"], cnt0, TM_MERGE)
    counts = cnt[EXPERT_ROW0:EXPERT_ROW0 + N_EXPERTS, 0].astype(jnp.int32)
    starts = jnp.cumsum(counts) - counts
    eid = info[INFO_E1:INFO_E2 + 1].astype(jnp.int32)
    rank = info[INFO_R1:INFO_R2 + 1].astype(jnp.int32)
    expert = jnp.arange(N_EXPERTS, dtype=jnp.int32)[:, None, None]
    dest_t = rank + jnp.sum(jnp.where(eid[None] == expert, starts[:, None, None], 0), axis=0)
    xs = _dispatch(h1, dest_t, TT_ROWS)
    plan = _moe_plan(counts, TOP_K * n, BM_EXPERT)
    ys = _experts(xs, plan, w_gate[0], w_up[0], w_down[0], BM_EXPERT)
    out = _combine(h1, info, dest_t, ys, p["ln2_g"], p["ln2_b"], TT_ROWS)
    return out.reshape(bsz, t, D_MODEL)


def kernel(x_prompt, x_sample, ln_in_g, ln_in_b, w_in, attn_sink, rel_pos_bias, w_proj_a, w_proj_b, w_out,
           ln1_g, ln1_b, w_route_group, b_route_group, w_route_expert, b_route_expert,
           w_gate, w_up, w_down, ln2_g, ln2_b):
    p = _prepare_weights(ln_in_g, ln_in_b, w_in, attn_sink, rel_pos_bias, w_proj_a, w_proj_b, w_out,
                         ln1_g, ln1_b, w_route_group, b_route_group, w_route_expert, b_route_expert,
                         ln2_g, ln2_b)
    return (_encode(x_prompt, p, w_gate, w_up, w_down), _encode(x_sample, p, w_gate, w_up, w_down))
```

```python
import functools

import numpy as np
import jax
import jax.numpy as jnp
from jax import lax
from jax.experimental import pallas as pl
from jax.experimental.pallas import tpu as pltpu

D_MODEL = 1024
HEAD_DIM = 64
N_HEADS_A = 8
N_KV_HEADS_A = 2
WINDOW = 128
N_HEADS_B = 8
GRID_W = 64
NA_ROWS = 8
NA_COLS = 16
N_GROUPS = 4
EXPERTS_PER_GROUP = 8
N_EXPERTS = N_GROUPS * EXPERTS_PER_GROUP
TOP_K = 2
D_EXPERT = D_MODEL // 2
LN_EPS = 1e-5
DEPTH = 1
DEEPNORM_ALPHA = (2.0 * DEPTH) ** 0.25
WIDTH_A = N_HEADS_A * HEAD_DIM
KV_WIDTH_A = N_KV_HEADS_A * HEAD_DIM
WIDTH_B = N_HEADS_B * HEAD_DIM
QKV_WIDTH = WIDTH_A + 2 * KV_WIDTH_A + 3 * WIDTH_B

LANES = 128
VMEM_LIMIT_BYTES = 56 * 1024 * 1024

NEG_BIG = -1e30

QA_COL, QB_COL, KB_COL, VB_COL = 0, WIDTH_A, WIDTH_A + WIDTH_B, WIDTH_A + 2 * WIDTH_B
KA_COL = WIDTH_A + 3 * WIDTH_B
VA_COL = KA_COL + KV_WIDTH_A

PERM_A = np.array([0, 4, 1, 5, 2, 6, 3, 7])


def _cparams(*sem):
    return pltpu.CompilerParams(dimension_semantics=sem, vmem_limit_bytes=VMEM_LIMIT_BYTES)


def _layer_norm(x, g, b):
    mu = jnp.mean(x, axis=-1, keepdims=True)
    xc = x - mu
    var = jnp.mean(xc * xc, axis=-1, keepdims=True)
    return xc * lax.rsqrt(var + LN_EPS) * g + b


def _qkv_kernel(x_ref, g_ref, b_ref, w_ref, o_ref):
    h = _layer_norm(x_ref[...], g_ref[...], b_ref[...])
    y = jnp.dot(h.astype(jnp.bfloat16), w_ref[...], preferred_element_type=jnp.float32)
    col = lax.broadcasted_iota(jnp.int32, (1, QKV_WIDTH), 1)
    y = y * jnp.where(col < KB_COL, HEAD_DIM ** -0.5, 1.0)
    o_ref[...] = y.astype(jnp.bfloat16)


def _qkv(x2, ln_g, ln_b, w_qkv, tm):
    n = x2.shape[0]
    return pl.pallas_call(
        _qkv_kernel,
        out_shape=jax.ShapeDtypeStruct((n, QKV_WIDTH), jnp.bfloat16),
        grid=(n // tm,),
        in_specs=[
            pl.BlockSpec((tm, D_MODEL), lambda i: (i, 0)),
            pl.BlockSpec((1, D_MODEL), lambda i: (0, 0)),
            pl.BlockSpec((1, D_MODEL), lambda i: (0, 0)),
            pl.BlockSpec((D_MODEL, QKV_WIDTH), lambda i: (0, 0)),
        ],
        out_specs=pl.BlockSpec((tm, QKV_WIDTH), lambda i: (i, 0)),
        compiler_params=_cparams("parallel"),
        name="qkv",
    )(x2, ln_g, ln_b, w_qkv)


WIN_BLK = 128
WIN_LOOKAHEAD = 2


def _win_bias_table():
    qi = np.arange(WIN_BLK)[:, None]
    kj = np.arange(3 * WIN_BLK)[None, :]
    dist = np.abs(kj - WIN_BLK - qi).astype(np.float64)
    slopes = 2.0 ** (-8.0 * np.arange(1, N_HEADS_A + 1) / N_HEADS_A)
    per_head = np.where(dist <= WINDOW, -slopes[:, None, None] * dist[None], NEG_BIG)
    groups = [np.concatenate([per_head[j], per_head[j + 4]], axis=0) for j in range(4)]
    return np.stack(groups).astype(np.float32)


def _win_kernel(sink_ref, q_ref, kp_ref, km_ref, kn_ref, vp_ref, vm_ref, vn_ref, bias_ref, o_ref,
                *, nsub, nblk_seq):
    i = pl.program_id(1)
    kcat = jnp.concatenate([kp_ref[...], km_ref[...], kn_ref[...]], axis=0)
    vcat = jnp.concatenate([vp_ref[...], vm_ref[...], vn_ref[...]], axis=0)
    lo = lax.broadcasted_iota(jnp.int32, (1, LANES), 1) < HEAD_DIM
    col = lax.broadcasted_iota(jnp.int32, (1, 3 * WIN_BLK), 1)
    top = lax.broadcasted_iota(jnp.int32, (2 * WIN_BLK, 1), 0) < WIN_BLK
    zero = jnp.zeros((), jnp.bfloat16)

    def scores(j, g):
        n = i * nsub + j
        off_seq = ((col < WIN_BLK) & (n == 0)) | ((col >= 2 * WIN_BLK) & (n == nblk_seq - 1))
        edge = jnp.where(off_seq, NEG_BIG, 0.0)
        qg = q_ref[WIN_BLK * j:WIN_BLK * (j + 1), LANES * g:LANES * (g + 1)]
        qm = jnp.concatenate([jnp.where(lo, qg, zero), jnp.where(lo, zero, qg)], axis=0)
        kj = kcat[WIN_BLK * j:WIN_BLK * (j + 3)]
        s = lax.dot_general(qm, kj, (((1,), (1,)), ((), ())), preferred_element_type=jnp.float32)
        return s + bias_ref[g] + edge

    def attend(s, j, g):
        vj = vcat[WIN_BLK * j:WIN_BLK * (j + 3)]
        sink = jnp.where(top, sink_ref[g], sink_ref[g + 4])
        m = jnp.maximum(jnp.max(s, axis=-1, keepdims=True), sink)
        p = jnp.exp(s - m)
        l = jnp.sum(p, axis=-1, keepdims=True) + jnp.exp(sink - m)
        o2 = jnp.dot(p.astype(jnp.bfloat16), vj, preferred_element_type=jnp.float32)
        o2 = o2 * (1.0 / l)
        o_ref[WIN_BLK * j:WIN_BLK * (j + 1), LANES * g:LANES * (g + 1)] = (
            jnp.where(lo, o2[:WIN_BLK], o2[WIN_BLK:]).astype(jnp.bfloat16))

    chains = [(j, g) for j in range(nsub) for g in range(4)]
    pending = [scores(*c) for c in chains[:WIN_LOOKAHEAD]]
    for idx, c in enumerate(chains):
        s = pending.pop(0)
        if idx + WIN_LOOKAHEAD < len(chains):
            pending.append(scores(*chains[idx + WIN_LOOKAHEAD]))
        attend(s, *c)


def _win_attention(qkv, sink, bsz, t, tq):
    n = bsz * t
    nsub = tq // WIN_BLK
    nblk_seq = t // WIN_BLK
    ntile = t // tq
    bias = jnp.asarray(_win_bias_table())

    def main_map(col):
        return lambda b, i, *_: (b * ntile + i, col)

    def prev_map(col):
        return lambda b, i, *_: (b * nblk_seq + jnp.maximum(i * nsub - 1, 0), col)

    def next_map(col):
        return lambda b, i, *_: (b * nblk_seq + jnp.minimum(i * nsub + nsub, nblk_seq - 1), col)

    halo = (WIN_BLK, LANES)
    ka, va = KA_COL // LANES, VA_COL // LANES
    grid_spec = pltpu.PrefetchScalarGridSpec(
        num_scalar_prefetch=1,
        grid=(bsz, ntile),
        in_specs=[
            pl.BlockSpec((tq, WIDTH_A), main_map(QA_COL // WIDTH_A)),
            pl.BlockSpec(halo, prev_map(ka)),
            pl.BlockSpec((tq, LANES), main_map(ka)),
            pl.BlockSpec(halo, next_map(ka)),
            pl.BlockSpec(halo, prev_map(va)),
            pl.BlockSpec((tq, LANES), main_map(va)),
            pl.BlockSpec(halo, next_map(va)),
            pl.BlockSpec((4, 2 * WIN_BLK, 3 * WIN_BLK), lambda b, i, *_: (0, 0, 0)),
        ],
        out_specs=pl.BlockSpec((tq, WIDTH_A), main_map(0)),
    )
    return pl.pallas_call(
        functools.partial(_win_kernel, nsub=nsub, nblk_seq=nblk_seq),
        out_shape=jax.ShapeDtypeStruct((n, WIDTH_A), jnp.bfloat16),
        grid_spec=grid_spec,
        compiler_params=_cparams("parallel", "parallel"),
        name="win",
    )(sink, qkv, qkv, qkv, qkv, qkv, qkv, qkv, bias)


NAT_ROWS_PER_STEP = 8
NAT_HALO_ROWS = NA_ROWS // 2
NAT_KEYS = NA_ROWS * GRID_W
NAT_ROWS_PER_TRIP = 8
NAT_LOOKAHEAD = 4


def _nat_bias_table(rpb):
    c = np.arange(GRID_W)
    cs = np.clip(c - NA_COLS // 2, 0, GRID_W - NA_COLS)
    col_mask = (c[None, :] >= cs[:, None]) & (c[None, :] < cs[:, None] + NA_COLS)
    dc = np.clip(c[None, :] - c[:, None] + (NA_COLS - 1), 0, 2 * NA_COLS - 2)
    t1 = jnp.where(col_mask[None, None], rpb[:, :, dc], NEG_BIG)
    per_shift = []
    for sh in range(NA_ROWS):
        w = t1[:, sh:sh + NA_ROWS]
        w = jnp.transpose(w, (0, 2, 1, 3)).reshape(N_HEADS_B // 2, 2 * GRID_W, NAT_KEYS)
        per_shift.append(w)
    return jnp.stack(per_shift, axis=1).astype(jnp.float32)


def _nat_kernel(q_ref, kp_ref, km_ref, kn_ref, vp_ref, vm_ref, vn_ref, tb_ref, o_ref, kcat, vcat,
                *, rows_seq):
    i = pl.program_id(1)
    halo = NAT_HALO_ROWS * GRID_W
    main = NAT_ROWS_PER_STEP * GRID_W
    kcat[0:halo] = kp_ref[...]
    kcat[halo:halo + main] = km_ref[...]
    kcat[halo + main:2 * halo + main] = kn_ref[...]
    vcat[0:halo] = vp_ref[...]
    vcat[halo:halo + main] = vm_ref[...]
    vcat[halo + main:2 * halo + main] = vn_ref[...]
    lo = lax.broadcasted_iota(jnp.int32, (1, LANES), 1) < HEAD_DIM
    zero = jnp.zeros((), jnp.bfloat16)
    r0 = i * NAT_ROWS_PER_STEP

    def scores(qr, p):
        r = r0 + qr
        rs = jnp.clip(r - NA_ROWS // 2, 0, rows_seq - NA_ROWS)
        koff = pl.multiple_of((rs - r0 + NAT_HALO_ROWS) * GRID_W, GRID_W)
        sh = rs - r + (NA_ROWS - 1)
        qoff = pl.multiple_of(qr * GRID_W, GRID_W)
        cols = slice(LANES * p, LANES * (p + 1))
        qp = q_ref[pl.ds(qoff, GRID_W), cols]
        qm = jnp.concatenate([jnp.where(lo, qp, zero), jnp.where(lo, zero, qp)], axis=0)
        kw = kcat[pl.ds(koff, NAT_KEYS), cols]
        s = lax.dot_general(qm, kw, (((1,), (1,)), ((), ())), preferred_element_type=jnp.float32)
        return s + tb_ref[p, sh], koff, qoff

    def attend(s, koff, qoff, p):
        cols = slice(LANES * p, LANES * (p + 1))
        vw = vcat[pl.ds(koff, NAT_KEYS), cols]
        m = jnp.max(s, axis=-1, keepdims=True)
        pe = jnp.exp(s - m)
        l = jnp.sum(pe, axis=-1, keepdims=True)
        o2 = jnp.dot(pe.astype(jnp.bfloat16), vw, preferred_element_type=jnp.float32)
        o2 = o2 * (1.0 / l)
        o_ref[pl.ds(qoff, GRID_W), cols] = jnp.where(lo, o2[:GRID_W], o2[GRID_W:]).astype(jnp.bfloat16)

    def trip(j, carry):
        chains = [(j * NAT_ROWS_PER_TRIP + q, p) for q in range(NAT_ROWS_PER_TRIP) for p in range(N_HEADS_B // 2)]
        pending = [scores(*c) for c in chains[:NAT_LOOKAHEAD]]
        for idx, (_, p) in enumerate(chains):
            s, koff, qoff = pending.pop(0)
            if idx + NAT_LOOKAHEAD < len(chains):
                pending.append(scores(*chains[idx + NAT_LOOKAHEAD]))
            attend(s, koff, qoff, p)
        return carry

    lax.fori_loop(0, NAT_ROWS_PER_STEP // NAT_ROWS_PER_TRIP, trip, 0)


def _nat_attention(qkv, tb, bsz, t):
    n = bsz * t
    rows_seq = t // GRID_W
    main = NAT_ROWS_PER_STEP * GRID_W
    halo = NAT_HALO_ROWS * GRID_W
    ntile = t // main
    nhalo_seq = t // halo
    per = main // halo

    def main_map(col):
        return lambda b, i: (b * ntile + i, col)

    def prev_map(col):
        return lambda b, i: (b * nhalo_seq + jnp.maximum(i * per - 1, 0), col)

    def next_map(col):
        return lambda b, i: (b * nhalo_seq + jnp.minimum(i * per + per, nhalo_seq - 1), col)

    qb, kb, vb = QB_COL // WIDTH_B, KB_COL // WIDTH_B, VB_COL // WIDTH_B
    return pl.pallas_call(
        functools.partial(_nat_kernel, rows_seq=rows_seq),
        out_shape=jax.ShapeDtypeStruct((n, WIDTH_B), jnp.bfloat16),
        grid=(bsz, ntile),
        in_specs=[
            pl.BlockSpec((main, WIDTH_B), main_map(qb)),
            pl.BlockSpec((halo, WIDTH_B), prev_map(kb)),
            pl.BlockSpec((main, WIDTH_B), main_map(kb)),
            pl.BlockSpec((halo, WIDTH_B), next_map(kb)),
            pl.BlockSpec((halo, WIDTH_B), prev_map(vb)),
            pl.BlockSpec((main, WIDTH_B), main_map(vb)),
            pl.BlockSpec((halo, WIDTH_B), next_map(vb)),
            pl.BlockSpec((N_HEADS_B // 2, NA_ROWS, 2 * GRID_W, NAT_KEYS), lambda b, i: (0, 0, 0, 0)),
        ],
        out_specs=pl.BlockSpec((main, WIDTH_B), main_map(0)),
        scratch_shapes=[pltpu.VMEM((main + 2 * halo, WIDTH_B), jnp.bfloat16),
                        pltpu.VMEM((main + 2 * halo, WIDTH_B), jnp.bfloat16)],
        compiler_params=_cparams("parallel", "parallel"),
        name="nat",
    )(qkv, qkv, qkv, qkv, qkv, qkv, qkv, tb)


EXPERT_ROW0 = N_GROUPS
ROUTE_ROWS = 48
INFO_E1, INFO_E2, INFO_R1, INFO_R2, INFO_W1, INFO_W2 = range(6)
INFO_ROWS = 8
MERGE_SUBTILES = 2


def _route(lt, carry, tri):
    rr, tm = lt.shape
    row = lax.broadcasted_iota(jnp.int32, (rr, tm), 0).astype(jnp.float32)
    none = jnp.float32(rr)

    def first_max(sel):
        m = jnp.max(jnp.where(sel, lt, NEG_BIG), axis=0, keepdims=True)
        idx = jnp.min(jnp.where(sel & (lt == m), row, none), axis=0, keepdims=True)
        return m, idx

    is_group = row < N_GROUPS
    mg, g = first_max(is_group)
    pg_sel = 1.0 / jnp.sum(jnp.where(is_group, jnp.exp(jnp.where(is_group, lt, mg) - mg), 0.0),
                           axis=0, keepdims=True)
    row0 = EXPERT_ROW0 + EXPERTS_PER_GROUP * g
    in_group = (row >= row0) & (row < row0 + EXPERTS_PER_GROUP)
    m1, i1 = first_max(in_group)
    m2, i2 = first_max(in_group & (row != i1))
    e2 = jnp.exp(m2 - m1)
    w1 = pg_sel / (1.0 + e2)
    w2 = pg_sel * e2 / (1.0 + e2)

    oh1 = row == i1
    oh2 = row == i2
    both = (oh1 | oh2).astype(jnp.bfloat16)
    before = jnp.dot(both, tri, preferred_element_type=jnp.float32) + carry
    r1 = jnp.sum(jnp.where(oh1, before, 0.0), axis=0, keepdims=True)
    r2 = jnp.sum(jnp.where(oh2, before, 0.0), axis=0, keepdims=True)
    new_carry = carry + jnp.sum(both.astype(jnp.float32), axis=1, keepdims=True)

    field = lax.broadcasted_iota(jnp.int32, (INFO_ROWS, tm), 0)
    info = jnp.zeros((INFO_ROWS, tm), jnp.float32)
    for k, v in ((INFO_E1, i1 - EXPERT_ROW0), (INFO_E2, i2 - EXPERT_ROW0), (INFO_R1, r1), (INFO_R2, r2),
                 (INFO_W1, w1), (INFO_W2, w2)):
        info = jnp.where(field == k, v, info)
    return info, new_carry


def _merge_kernel(x_ref, oa_ref, ob_ref, lng_ref, lnb_ref, wg_ref, wpa_ref, wpb_ref, wo_ref,
                  l1g_ref, l1b_ref, wr_ref, br_ref, cnt0_ref,
                  h1_ref, info_ref, cnt_ref, carry_ref, tri_ref):
    tm = x_ref.shape[0]

    @pl.when(pl.program_id(0) == 0)
    def _():
        carry_ref[...] = cnt0_ref[...]
        r = lax.broadcasted_iota(jnp.int32, (tm, tm), 0)
        c = lax.broadcasted_iota(jnp.int32, (tm, tm), 1)
        tri_ref[...] = (r < c).astype(jnp.bfloat16)

    def project(rows):
        h = _layer_norm(x_ref[rows], lng_ref[...], lnb_ref[...])
        gates = jnp.dot(h.astype(jnp.bfloat16), wg_ref[...], preferred_element_type=jnp.float32)
        pa = jnp.dot(oa_ref[rows], wpa_ref[...], preferred_element_type=jnp.float32)
        pb = jnp.dot(ob_ref[rows], wpb_ref[...], preferred_element_type=jnp.float32)
        return h, gates, pa, pb

    def mix(h, gates, pa, pb):
        mixin = jax.nn.sigmoid(gates[:, :D_MODEL]) * pa + jax.nn.sigmoid(gates[:, D_MODEL:]) * pb
        return DEEPNORM_ALPHA * h + jnp.dot(mixin.astype(jnp.bfloat16), wo_ref[...],
                                            preferred_element_type=jnp.float32)

    def norm_and_logits(pre, rows):
        h1 = _layer_norm(pre, l1g_ref[...], l1b_ref[...])
        h1_ref[rows] = h1
        hi = h1.astype(jnp.bfloat16)
        lo = (h1 - hi.astype(jnp.float32)).astype(jnp.bfloat16)
        lhs = jnp.concatenate([hi, lo, hi], axis=1)
        return lax.dot_general(wr_ref[...], lhs, (((1,), (1,)), ((), ())), preferred_element_type=jnp.float32)

    sub = tm // MERGE_SUBTILES
    parts = [slice(k * sub, (k + 1) * sub) for k in range(MERGE_SUBTILES)]
    projected = [project(rows) for rows in parts]
    mixed = [mix(*pr) for pr in projected]
    logits_t = jnp.concatenate([norm_and_logits(pre, rows) for pre, rows in zip(mixed, parts)], axis=1)
    logits_t = logits_t + br_ref[...]
    info, carry = _route(logits_t, carry_ref[...], tri_ref[...])
    info_ref[...] = info
    carry_ref[...] = carry
    cnt_ref[...] = carry[:, :LANES]


def _merge(x2, oa, ob, ln_g, ln_b, w_gates, w_pa, w_pb, w_o, l1g, l1b, w_r, b_r, cnt0, tm):
    n = x2.shape[0]

    def const(shape):
        return pl.BlockSpec(shape, lambda i: (0,) * len(shape))

    def rows(width):
        return pl.BlockSpec((tm, width), lambda i: (i, 0))

    return pl.pallas_call(
        _merge_kernel,
        out_shape=(jax.ShapeDtypeStruct((n, D_MODEL), jnp.float32),
                   jax.ShapeDtypeStruct((INFO_ROWS, n), jnp.float32),
                   jax.ShapeDtypeStruct((ROUTE_ROWS, LANES), jnp.float32)),
        grid=(n // tm,),
        in_specs=[
            rows(D_MODEL), rows(WIDTH_A), rows(WIDTH_B),
            const((1, D_MODEL)), const((1, D_MODEL)),
            const((D_MODEL, 2 * D_MODEL)),
            const((WIDTH_A, D_MODEL)), const((WIDTH_B, D_MODEL)),
            const((D_MODEL, D_MODEL)),
            const((1, D_MODEL)), const((1, D_MODEL)),
            const((ROUTE_ROWS, 3 * D_MODEL)), const((ROUTE_ROWS, tm)), const((ROUTE_ROWS, tm)),
        ],
        out_specs=(rows(D_MODEL), pl.BlockSpec((INFO_ROWS, tm), lambda i: (0, i)),
                   const((ROUTE_ROWS, LANES))),
        scratch_shapes=[pltpu.VMEM((ROUTE_ROWS, tm), jnp.float32), pltpu.VMEM((tm, tm), jnp.bfloat16)],
        compiler_params=_cparams("arbitrary"),
        name="merge",
    )(x2, oa, ob, ln_g, ln_b, w_gates, w_pa, w_pb, w_o, l1g, l1b, w_r, b_r, cnt0)


def _row_copy(src_ref, src_row, dst_ref, dst_row, sem):
    return pltpu.make_async_copy(src_ref.at[pl.ds(src_row, 1)], dst_ref.at[pl.ds(dst_row, 1)], sem)


ROW_DMA_UNROLL = 8


def _tile_dest(dest_t, tt):
    n = dest_t.shape[1]
    return dest_t.reshape(TOP_K, n // tt, tt).transpose(1, 0, 2).reshape(n // tt, 1, TOP_K * tt)


def _dispatch_kernel(dest_ref, h_ref, xs_ref, sem):
    tt = h_ref.shape[0]

    def issue(t, c):
        for k in range(TOP_K):
            _row_copy(h_ref, t, xs_ref, dest_ref[0, 0, k * tt + t], sem).start()
        return c

    lax.fori_loop(0, tt, issue, 0, unroll=ROW_DMA_UNROLL)
    for k in range(TOP_K):
        pltpu.make_async_copy(h_ref, xs_ref.at[pl.ds(0, tt)], sem).wait()


def _dispatch(h1, dest_t, tt):
    n = h1.shape[0]
    dest3 = _tile_dest(dest_t, tt)
    return pl.pallas_call(
        _dispatch_kernel,
        out_shape=jax.ShapeDtypeStruct((TOP_K * n, D_MODEL), jnp.float32),
        grid=(n // tt,),
        in_specs=[
            pl.BlockSpec((1, 1, TOP_K * tt), lambda i: (i, 0, 0), memory_space=pltpu.SMEM),
            pl.BlockSpec((tt, D_MODEL), lambda i: (i, 0)),
        ],
        out_specs=pl.BlockSpec(memory_space=pl.ANY),
        scratch_shapes=[pltpu.SemaphoreType.DMA(())],
        compiler_params=_cparams("arbitrary"),
        name="dispatch",
    )(dest3, h1)


def _moe_plan(counts, na, bm):
    ends = jnp.cumsum(counts)
    starts = ends - counts
    first_blk = starts // bm
    tiles = jnp.where(counts > 0, (ends - 1) // bm - first_blk + 1, 0)
    item_end = jnp.cumsum(tiles)
    item_start = item_end - tiles
    total = item_end[-1]
    wmax = na // bm + N_EXPERTS - 1
    w = jnp.arange(wmax, dtype=jnp.int32)
    wc = jnp.minimum(w, total - 1)
    e = jnp.sum((item_end[None, :] <= wc[:, None]).astype(jnp.int32), axis=1)
    e = jnp.minimum(e, N_EXPERTS - 1)
    blk = (first_blk[e] + (wc - item_start[e])).astype(jnp.int32)
    valid = w < total
    lo = jnp.where(valid, jnp.maximum(starts[e], blk * bm), 0).astype(jnp.int32)
    hi = jnp.where(valid, jnp.minimum(ends[e], (blk + 1) * bm), 0).astype(jnp.int32)
    prev_blk = jnp.concatenate([jnp.full((1,), -1, jnp.int32), blk[:-1]])
    prev_e = jnp.concatenate([jnp.full((1,), -1, jnp.int32), e[:-1]])
    flags = (valid.astype(jnp.int32)
             + 2 * (valid & (blk != prev_blk)).astype(jnp.int32)
             + 4 * (valid & (e != prev_e)).astype(jnp.int32))
    return blk, e, lo, hi, flags


FLAG_VALID, FLAG_NEW_BLOCK, FLAG_NEW_EXPERT = 1, 2, 4
EXPERT_HIDDEN_CHUNKS = 2


def _expert_kernel(blk_ref, e_ref, lo_ref, hi_ref, flag_ref, x_ref, wg_ref, wu_ref, wd_ref, o_ref,
                   wg_b, wu_b, wd_b):
    w = pl.program_id(0)
    bm = x_ref.shape[0]
    flags = flag_ref[w]

    @pl.when((flags & FLAG_NEW_EXPERT) != 0)
    def _():
        wg_b[...] = wg_ref[0].astype(jnp.bfloat16)
        wu_b[...] = wu_ref[0].astype(jnp.bfloat16)
        wd_b[...] = wd_ref[0].astype(jnp.bfloat16)

    @pl.when((flags & FLAG_VALID) != 0)
    def _():
        x = x_ref[...].astype(jnp.bfloat16)
        nh = D_EXPERT // EXPERT_HIDDEN_CHUNKS
        gu = [(jnp.dot(x, wg_b[:, c * nh:(c + 1) * nh], preferred_element_type=jnp.float32),
               jnp.dot(x, wu_b[:, c * nh:(c + 1) * nh], preferred_element_type=jnp.float32))
              for c in range(EXPERT_HIDDEN_CHUNKS)]
        hmid = [(jax.nn.silu(g) * u).astype(jnp.bfloat16) for g, u in gu]
        row = blk_ref[w] * bm + lax.broadcasted_iota(jnp.int32, (bm, 1), 0)
        mine = (row >= lo_ref[w]) & (row < hi_ref[w])

        def down():
            y = sum(jnp.dot(hmid[c], wd_b[c * nh:(c + 1) * nh], preferred_element_type=jnp.float32)
                    for c in range(EXPERT_HIDDEN_CHUNKS))
            return jnp.where(mine, y, 0.0)

        @pl.when((flags & FLAG_NEW_BLOCK) != 0)
        def _():
            o_ref[...] = down()

        @pl.when((flags & FLAG_NEW_BLOCK) == 0)
        def _():
            o_ref[...] += down()


def _experts(xs, plan, w_gate, w_up, w_down, bm):
    na = xs.shape[0]
    blk, e, lo, hi, flags = plan
    nitems = blk.shape[0]
    grid_spec = pltpu.PrefetchScalarGridSpec(
        num_scalar_prefetch=5,
        grid=(nitems,),
        in_specs=[
            pl.BlockSpec((bm, D_MODEL), lambda w, blk, e, *_: (blk[w], 0)),
            pl.BlockSpec((1, D_MODEL, D_EXPERT), lambda w, blk, e, *_: (e[w], 0, 0)),
            pl.BlockSpec((1, D_MODEL, D_EXPERT), lambda w, blk, e, *_: (e[w], 0, 0)),
            pl.BlockSpec((1, D_EXPERT, D_MODEL), lambda w, blk, e, *_: (e[w], 0, 0)),
        ],
        out_specs=pl.BlockSpec((bm, D_MODEL), lambda w, blk, e, *_: (blk[w], 0)),
        scratch_shapes=[pltpu.VMEM((D_MODEL, D_EXPERT), jnp.bfloat16),
                        pltpu.VMEM((D_MODEL, D_EXPERT), jnp.bfloat16),
                        pltpu.VMEM((D_EXPERT, D_MODEL), jnp.bfloat16)],
    )
    return pl.pallas_call(
        _expert_kernel,
        out_shape=jax.ShapeDtypeStruct((na, D_MODEL), jnp.float32),
        grid_spec=grid_spec,
        compiler_params=_cparams("arbitrary"),
        name="experts",
    )(blk, e, lo, hi, flags, xs, w_gate, w_up, w_down)


def _combine_kernel(dest_ref, dest_next_ref, h1_ref, info_ref, g_ref, b_ref, ys_ref, o_ref, ybuf, sem):
    i = pl.program_id(0)
    nsteps = pl.num_programs(0)
    tt = h1_ref.shape[0]
    slot = i % 2

    def gather(d_ref, s):
        def issue(t, c):
            for k in range(TOP_K):
                _row_copy(ys_ref, d_ref[0, 0, k * tt + t], ybuf.at[s, k], t, sem.at[s]).start()
            return c
        lax.fori_loop(0, tt, issue, 0, unroll=ROW_DMA_UNROLL)

    @pl.when(i == 0)
    def _():
        gather(dest_ref, 0)

    @pl.when(i + 1 < nsteps)
    def _():
        gather(dest_next_ref, 1 - slot)

    for k in range(TOP_K):
        pltpu.make_async_copy(ys_ref.at[pl.ds(0, tt)], ybuf.at[slot, k], sem.at[slot]).wait()
    pad = jnp.zeros((LANES - INFO_ROWS, tt), jnp.float32)
    info = jnp.concatenate([info_ref[...], pad], axis=0).T
    moe = ybuf[slot, 0] * info[:, INFO_W1:INFO_W1 + 1] + ybuf[slot, 1] * info[:, INFO_W2:INFO_W2 + 1]
    o_ref[...] = _layer_norm(DEEPNORM_ALPHA * h1_ref[...] + moe, g_ref[...], b_ref[...])


def _combine(h1, info, dest_t, ys, ln_g, ln_b, tt):
    n = h1.shape[0]
    nsteps = n // tt
    dest3 = _tile_dest(dest_t, tt)
    return pl.pallas_call(
        _combine_kernel,
        out_shape=jax.ShapeDtypeStruct((n, D_MODEL), jnp.float32),
        grid=(nsteps,),
        in_specs=[
            pl.BlockSpec((1, 1, TOP_K * tt), lambda i: (i, 0, 0), memory_space=pltpu.SMEM),
            pl.BlockSpec((1, 1, TOP_K * tt), lambda i: (jnp.minimum(i + 1, nsteps - 1), 0, 0),
                         memory_space=pltpu.SMEM),
            pl.BlockSpec((tt, D_MODEL), lambda i: (i, 0)),
            pl.BlockSpec((INFO_ROWS, tt), lambda i: (0, i)),
            pl.BlockSpec((1, D_MODEL), lambda i: (0, 0)),
            pl.BlockSpec((1, D_MODEL), lambda i: (0, 0)),
            pl.BlockSpec(memory_space=pl.ANY),
        ],
        out_specs=pl.BlockSpec((tt, D_MODEL), lambda i: (i, 0)),
        scratch_shapes=[pltpu.VMEM((2, TOP_K, tt, D_MODEL), jnp.float32), pltpu.SemaphoreType.DMA((2,))],
        compiler_params=_cparams("arbitrary"),
        name="combine",
    )(dest3, dest3, h1, info, ln_g, ln_b, ys)


TM_QKV = 512
TQ_WIN = 512
TM_MERGE = 512
TT_ROWS = 256
BM_EXPERT = 256


def _prepare_weights(ln_in_g, ln_in_b, w_in, attn_sink, rel_pos_bias, w_proj_a, w_proj_b, w_out,
                     ln1_g, ln1_b, w_route_group, b_route_group, w_route_expert, b_route_expert,
                     ln2_g, ln2_b):
    bf = jnp.bfloat16
    w = w_in[0]
    splits = np.cumsum([WIDTH_A, KV_WIDTH_A, KV_WIDTH_A, WIDTH_B, WIDTH_B, WIDTH_B, D_MODEL])
    wqa, wka, wva, wqb, wkb, wvb, wga, wgb = jnp.split(w, [int(s) for s in splits], axis=1)
    wqa = wqa.reshape(D_MODEL, N_HEADS_A, HEAD_DIM)[:, PERM_A].reshape(D_MODEL, WIDTH_A)
    w_qkv = jnp.concatenate([wqa, wqb, wkb, wvb, wka, wva], axis=1).astype(bf)
    w_gates = jnp.concatenate([wga, wgb], axis=1).astype(bf)
    w_pa = w_proj_a[0].reshape(N_HEADS_A, HEAD_DIM, D_MODEL)[PERM_A].reshape(WIDTH_A, D_MODEL).astype(bf)
    w_pb = w_proj_b[0].astype(bf)
    w_o = w_out[0].astype(bf)
    pad = ROUTE_ROWS - N_GROUPS - N_EXPERTS
    w_r = jnp.concatenate([w_route_group[0].T, w_route_expert[0].T, jnp.zeros((pad, D_MODEL), jnp.float32)], axis=0)
    w_r_hi = w_r.astype(bf)
    w_r_lo = (w_r - w_r_hi.astype(jnp.float32)).astype(bf)
    w_r3 = jnp.concatenate([w_r_hi, w_r_hi, w_r_lo], axis=1)
    b_r = jnp.concatenate([b_route_group[0], b_route_expert[0], jnp.zeros((pad,), jnp.float32)])
    b_r = jnp.broadcast_to(b_r[:, None], (ROUTE_ROWS, TM_MERGE))
    row = lambda v: v.reshape(1, D_MODEL)
    return dict(
        ln_in_g=row(ln_in_g), ln_in_b=row(ln_in_b), w_qkv=w_qkv, w_gates=w_gates,
        sink=attn_sink[0].astype(jnp.float32), nat_bias=_nat_bias_table(rel_pos_bias[0]),
        w_pa=w_pa, w_pb=w_pb, w_o=w_o, ln1_g=row(ln1_g[0]), ln1_b=row(ln1_b[0]),
        w_r3=w_r3, b_r=b_r, ln2_g=row(ln2_g[0]), ln2_b=row(ln2_b[0]))


def _encode(x, p, w_gate, w_up, w_down):
    bsz, t, _ = x.shape
    n = bsz * t
    x2 = x.reshape(n, D_MODEL)
    qkv = _qkv(x2, p["ln_in_g"], p["ln_in_b"], p["w_qkv"], TM_QKV)
    oa = _win_attention(qkv, p["sink"], bsz, t, TQ_WIN)
    ob = _nat_attention(qkv, p["nat_bias"], bsz, t)
    cnt0 = jnp.zeros((ROUTE_ROWS, TM_MERGE), jnp.float32)
    h1, info, cnt = _merge(x2, oa, ob, p["ln_in_g"], p["ln_in_b"], p["w_gates"], p["w_pa"], p["w_pb"],
                           p["w_o"], p["ln1_g"], p["ln1_b"], p["w_r3"], p["b_r"], cnt0, TM_MERGE)
    counts = cnt[EXPERT_ROW0:EXPERT_ROW0 + N_EXPERTS, 0].astype(jnp.int32)
    starts = jnp.cumsum(counts) - counts
    eid = info[INFO_E1:INFO_E2 + 1].astype(jnp.int32)
    rank = info[INFO_R1:INFO_R2 + 1].astype(jnp.int32)
    expert = jnp.arange(N_EXPERTS, dtype=jnp.int32)[:, None, None]
    dest_t = rank + jnp.sum(jnp.where(eid[None] == expert, starts[:, None, None], 0), axis=0)
    xs = _dispatch(h1, dest_t, TT_ROWS)
    plan = _moe_plan(counts, TOP_K * n, BM_EXPERT)
    ys = _experts(xs, plan, w_gate[0], w_up[0], w_down[0], BM_EXPERT)
    out = _combine(h1, info, dest_t, ys, p["ln2_g"], p["ln2_b"], TT_ROWS)
    return out.reshape(bsz, t, D_MODEL)


def kernel(x_prompt, x_sample, ln_in_g, ln_in_b, w_in, attn_sink, rel_pos_bias, w_proj_a, w_proj_b, w_out,
           ln1_g, ln1_b, w_route_group, b_route_group, w_route_expert, b_route_expert,
           w_gate, w_up, w_down, ln2_g, ln2_b):
    p = _prepare_weights(ln_in_g, ln_in_b, w_in, attn_sink, rel_pos_bias, w_proj_a, w_proj_b, w_out,
                         ln1_g, ln1_b, w_route_group, b_route_group, w_route_expert, b_route_expert,
                         ln2_g, ln2_b)
    return (_encode(x_prompt, p, w_gate, w_up, w_down), _encode(x_sample, p, w_gate, w_up, w_down))
```

```python
import functools

import numpy as np
import jax
import jax.numpy as jnp
from jax import lax
from jax.experimental import pallas as pl
from jax.experimental.pallas import tpu as pltpu

D_MODEL = 1024
HEAD_DIM = 64
N_HEADS_A = 8
N_KV_HEADS_A = 2
WINDOW = 128
N_HEADS_B = 8
GRID_W = 64
NA_ROWS = 8
NA_COLS = 16
N_GROUPS = 4
EXPERTS_PER_GROUP = 8
N_EXPERTS = N_GROUPS * EXPERTS_PER_GROUP
TOP_K = 2
D_EXPERT = D_MODEL // 2
LN_EPS = 1e-5
DEPTH = 1
DEEPNORM_ALPHA = (2.0 * DEPTH) ** 0.25
WIDTH_A = N_HEADS_A * HEAD_DIM
KV_WIDTH_A = N_KV_HEADS_A * HEAD_DIM
WIDTH_B = N_HEADS_B * HEAD_DIM
QKV_WIDTH = WIDTH_A + 2 * KV_WIDTH_A + 3 * WIDTH_B

LANES = 128
VMEM_LIMIT_BYTES = 56 * 1024 * 1024

NEG_BIG = -1e30

QA_COL, QB_COL, KB_COL, VB_COL = 0, WIDTH_A, WIDTH_A + WIDTH_B, WIDTH_A + 2 * WIDTH_B
KA_COL = WIDTH_A + 3 * WIDTH_B
VA_COL = KA_COL + KV_WIDTH_A

GQA_GROUP = N_HEADS_A // N_KV_HEADS_A


def _cparams(*sem):
    return pltpu.CompilerParams(dimension_semantics=sem, vmem_limit_bytes=VMEM_LIMIT_BYTES)


def _layer_norm(x, g, b):
    mu = jnp.mean(x, axis=-1, keepdims=True)
    xc = x - mu
    var = jnp.mean(xc * xc, axis=-1, keepdims=True)
    return xc * lax.rsqrt(var + LN_EPS) * g + b


PACKED_WIDTH = D_MODEL // 2


def _pack_rows(x):
    def rne(v):
        return v + jnp.uint32(0x7FFF) + ((v >> 16) & jnp.uint32(1))
    hi = lax.bitcast_convert_type(x[:, :PACKED_WIDTH], jnp.uint32)
    lo = lax.bitcast_convert_type(x[:, PACKED_WIDTH:], jnp.uint32)
    return (rne(hi) & jnp.uint32(0xFFFF0000)) | (rne(lo) >> 16)


def _unpack_rows(w):
    hi = lax.bitcast_convert_type(w & jnp.uint32(0xFFFF0000), jnp.float32)
    lo = lax.bitcast_convert_type(w << 16, jnp.float32)
    return jnp.concatenate([hi, lo], axis=1)


def _qkv_kernel(x_ref, g_ref, b_ref, w_ref, o_ref):
    h = _layer_norm(x_ref[...], g_ref[...], b_ref[...])
    y = jnp.dot(h.astype(jnp.bfloat16), w_ref[...], preferred_element_type=jnp.float32)
    col = lax.broadcasted_iota(jnp.int32, (1, QKV_WIDTH), 1)
    y = y * jnp.where(col < KB_COL, HEAD_DIM ** -0.5, 1.0)
    o_ref[...] = y.astype(jnp.bfloat16)


def _qkv(x2, ln_g, ln_b, w_qkv, tm):
    n = x2.shape[0]
    return pl.pallas_call(
        _qkv_kernel,
        out_shape=jax.ShapeDtypeStruct((n, QKV_WIDTH), jnp.bfloat16),
        grid=(n // tm,),
        in_specs=[
            pl.BlockSpec((tm, D_MODEL), lambda i: (i, 0)),
            pl.BlockSpec((1, D_MODEL), lambda i: (0, 0)),
            pl.BlockSpec((1, D_MODEL), lambda i: (0, 0)),
            pl.BlockSpec((D_MODEL, QKV_WIDTH), lambda i: (0, 0)),
        ],
        out_specs=pl.BlockSpec((tm, QKV_WIDTH), lambda i: (i, 0)),
        compiler_params=_cparams("parallel"),
        name="qkv",
    )(x2, ln_g, ln_b, w_qkv)


WIN_BLK = 128
WIN_LOOKAHEAD = 2


def _win_bias_table():
    qi = np.arange(WIN_BLK)[:, None]
    kj = np.arange(3 * WIN_BLK)[None, :]
    dist = np.abs(kj - WIN_BLK - qi).astype(np.float64)
    slopes = 2.0 ** (-8.0 * np.arange(1, N_HEADS_A + 1) / N_HEADS_A)
    per_head = np.where(dist <= WINDOW, -slopes[:, None, None] * dist[None], NEG_BIG)
    groups = [np.concatenate([per_head[j], per_head[j + 4]], axis=0) for j in range(4)]
    return np.stack(groups).astype(np.float32)


def _win_kernel(sink_ref, q_ref, kp_ref, km_ref, kn_ref, vp_ref, vm_ref, vn_ref, bias_ref, o_ref,
                *, nsub, nblk_seq):
    i = pl.program_id(1)
    kcat = jnp.concatenate([kp_ref[...], km_ref[...], kn_ref[...]], axis=0)
    vcat = jnp.concatenate([vp_ref[...], vm_ref[...], vn_ref[...]], axis=0)
    lo = lax.broadcasted_iota(jnp.int32, (1, LANES), 1) < HEAD_DIM
    col = lax.broadcasted_iota(jnp.int32, (1, 3 * WIN_BLK), 1)
    top = lax.broadcasted_iota(jnp.int32, (2 * WIN_BLK, 1), 0) < WIN_BLK
    zero = jnp.zeros((), jnp.bfloat16)

    def scores(j, g):
        n = i * nsub + j
        off_seq = ((col < WIN_BLK) & (n == 0)) | ((col >= 2 * WIN_BLK) & (n == nblk_seq - 1))
        edge = jnp.where(off_seq, NEG_BIG, 0.0)
        qg = q_ref[WIN_BLK * j:WIN_BLK * (j + 1), LANES * g:LANES * (g + 1)]
        qm = jnp.concatenate([jnp.where(lo, qg, zero), jnp.where(lo, zero, qg)], axis=0)
        kj = kcat[WIN_BLK * j:WIN_BLK * (j + 3)]
        s = lax.dot_general(qm, kj, (((1,), (1,)), ((), ())), preferred_element_type=jnp.float32)
        return s + bias_ref[g] + edge

    def attend(s, j, g):
        vj = vcat[WIN_BLK * j:WIN_BLK * (j + 3)]
        sink = jnp.where(top, sink_ref[g], sink_ref[g + 4])
        m = jnp.maximum(jnp.max(s, axis=-1, keepdims=True), sink)
        p = jnp.exp(s - m)
        l = jnp.sum(p, axis=-1, keepdims=True) + jnp.exp(sink - m)
        o2 = jnp.dot(p.astype(jnp.bfloat16), vj, preferred_element_type=jnp.float32)
        o2 = o2 * (1.0 / l)
        o_ref[WIN_BLK * j:WIN_BLK * (j + 1), LANES * g:LANES * (g + 1)] = (
            jnp.where(lo, o2[:WIN_BLK], o2[WIN_BLK:]).astype(jnp.bfloat16))

    chains = [(j, g) for j in range(nsub) for g in range(4)]
    pending = [scores(*c) for c in chains[:WIN_LOOKAHEAD]]
    for idx, c in enumerate(chains):
        s = pending.pop(0)
        if idx + WIN_LOOKAHEAD < len(chains):
            pending.append(scores(*chains[idx + WIN_LOOKAHEAD]))
        attend(s, *c)


def _win_attention(qkv, sink, bsz, t, tq):
    n = bsz * t
    nsub = tq // WIN_BLK
    nblk_seq = t // WIN_BLK
    ntile = t // tq
    bias = jnp.asarray(_win_bias_table())

    def main_map(col):
        return lambda b, i, *_: (b * ntile + i, col)

    def prev_map(col):
        return lambda b, i, *_: (b * nblk_seq + jnp.maximum(i * nsub - 1, 0), col)

    def next_map(col):
        return lambda b, i, *_: (b * nblk_seq + jnp.minimum(i * nsub + nsub, nblk_seq - 1), col)

    halo = (WIN_BLK, LANES)
    ka, va = KA_COL // LANES, VA_COL // LANES
    grid_spec = pltpu.PrefetchScalarGridSpec(
        num_scalar_prefetch=1,
        grid=(bsz, ntile),
        in_specs=[
            pl.BlockSpec((tq, WIDTH_A), main_map(QA_COL // WIDTH_A)),
            pl.BlockSpec(halo, prev_map(ka)),
            pl.BlockSpec((tq, LANES), main_map(ka)),
            pl.BlockSpec(halo, next_map(ka)),
            pl.BlockSpec(halo, prev_map(va)),
            pl.BlockSpec((tq, LANES), main_map(va)),
            pl.BlockSpec(halo, next_map(va)),
            pl.BlockSpec((4, 2 * WIN_BLK, 3 * WIN_BLK), lambda b, i, *_: (0, 0, 0)),
        ],
        out_specs=pl.BlockSpec((tq, WIDTH_A), main_map(0)),
    )
    return pl.pallas_call(
        functools.partial(_win_kernel, nsub=nsub, nblk_seq=nblk_seq),
        out_shape=jax.ShapeDtypeStruct((n, WIDTH_A), jnp.bfloat16),
        grid_spec=grid_spec,
        compiler_params=_cparams("parallel", "parallel"),
        name="win",
    )(sink, qkv, qkv, qkv, qkv, qkv, qkv, qkv, bias)


NAT_ROWS_PER_STEP = 8
NAT_HALO_ROWS = NA_ROWS // 2
NAT_KEYS = NA_ROWS * GRID_W
NAT_ROWS_PER_TRIP = 8
NAT_LOOKAHEAD = 4


def _nat_bias_table(rpb):
    c = np.arange(GRID_W)
    cs = np.clip(c - NA_COLS // 2, 0, GRID_W - NA_COLS)
    col_mask = (c[None, :] >= cs[:, None]) & (c[None, :] < cs[:, None] + NA_COLS)
    dc = np.clip(c[None, :] - c[:, None] + (NA_COLS - 1), 0, 2 * NA_COLS - 2)
    onehot = jnp.asarray(dc[None] == np.arange(2 * NA_COLS - 1)[:, None, None], jnp.float32)
    picked = jnp.einsum("hdj,jqc->hdqc", rpb, onehot, precision=lax.Precision.HIGHEST)
    t1 = jnp.where(col_mask[None, None], picked, NEG_BIG)
    per_shift = []
    for sh in range(NA_ROWS):
        w = t1[:, sh:sh + NA_ROWS]
        w = jnp.transpose(w, (0, 2, 1, 3)).reshape(N_HEADS_B // 2, 2 * GRID_W, NAT_KEYS)
        per_shift.append(w)
    return jnp.stack(per_shift, axis=1).astype(jnp.float32)


def _nat_kernel(q_ref, kp_ref, km_ref, kn_ref, vp_ref, vm_ref, vn_ref, tb_ref, o_ref, kcat, vcat,
                *, rows_seq):
    i = pl.program_id(1)
    halo = NAT_HALO_ROWS * GRID_W
    main = NAT_ROWS_PER_STEP * GRID_W
    kcat[0:halo] = kp_ref[...]
    kcat[halo:halo + main] = km_ref[...]
    kcat[halo + main:2 * halo + main] = kn_ref[...]
    vcat[0:halo] = vp_ref[...]
    vcat[halo:halo + main] = vm_ref[...]
    vcat[halo + main:2 * halo + main] = vn_ref[...]
    lo = lax.broadcasted_iota(jnp.int32, (1, LANES), 1) < HEAD_DIM
    zero = jnp.zeros((), jnp.bfloat16)
    r0 = i * NAT_ROWS_PER_STEP

    def scores(qr, p):
        r = r0 + qr
        rs = jnp.clip(r - NA_ROWS // 2, 0, rows_seq - NA_ROWS)
        koff = pl.multiple_of((rs - r0 + NAT_HALO_ROWS) * GRID_W, GRID_W)
        sh = rs - r + (NA_ROWS - 1)
        qoff = pl.multiple_of(qr * GRID_W, GRID_W)
        cols = slice(LANES * p, LANES * (p + 1))
        qp = q_ref[pl.ds(qoff, GRID_W), cols]
        qm = jnp.concatenate([jnp.where(lo, qp, zero), jnp.where(lo, zero, qp)], axis=0)
        kw = kcat[pl.ds(koff, NAT_KEYS), cols]
        s = lax.dot_general(qm, kw, (((1,), (1,)), ((), ())), preferred_element_type=jnp.float32)
        return s + tb_ref[p, sh], koff, qoff

    def attend(s, koff, qoff, p):
        cols = slice(LANES * p, LANES * (p + 1))
        vw = vcat[pl.ds(koff, NAT_KEYS), cols]
        m = jnp.max(s, axis=-1, keepdims=True)
        pe = jnp.exp(s - m)
        l = jnp.sum(pe, axis=-1, keepdims=True)
        o2 = jnp.dot(pe.astype(jnp.bfloat16), vw, preferred_element_type=jnp.float32)
        o2 = o2 * (1.0 / l)
        o_ref[pl.ds(qoff, GRID_W), cols] = jnp.where(lo, o2[:GRID_W], o2[GRID_W:]).astype(jnp.bfloat16)

    def trip(j, carry):
        chains = [(j * NAT_ROWS_PER_TRIP + q, p) for q in range(NAT_ROWS_PER_TRIP) for p in range(N_HEADS_B // 2)]
        pending = [scores(*c) for c in chains[:NAT_LOOKAHEAD]]
        for idx, (_, p) in enumerate(chains):
            s, koff, qoff = pending.pop(0)
            if idx + NAT_LOOKAHEAD < len(chains):
                pending.append(scores(*chains[idx + NAT_LOOKAHEAD]))
            attend(s, koff, qoff, p)
        return carry

    lax.fori_loop(0, NAT_ROWS_PER_STEP // NAT_ROWS_PER_TRIP, trip, 0)


def _nat_attention(qkv, tb, bsz, t):
    n = bsz * t
    rows_seq = t // GRID_W
    main = NAT_ROWS_PER_STEP * GRID_W
    halo = NAT_HALO_ROWS * GRID_W
    ntile = t // main
    nhalo_seq = t // halo
    per = main // halo

    def main_map(col):
        return lambda b, i: (b * ntile + i, col)

    def prev_map(col):
        return lambda b, i: (b * nhalo_seq + jnp.maximum(i * per - 1, 0), col)

    def next_map(col):
        return lambda b, i: (b * nhalo_seq + jnp.minimum(i * per + per, nhalo_seq - 1), col)

    qb, kb, vb = QB_COL // WIDTH_B, KB_COL // WIDTH_B, VB_COL // WIDTH_B
    return pl.pallas_call(
        functools.partial(_nat_kernel, rows_seq=rows_seq),
        out_shape=jax.ShapeDtypeStruct((n, WIDTH_B), jnp.bfloat16),
        grid=(bsz, ntile),
        in_specs=[
            pl.BlockSpec((main, WIDTH_B), main_map(qb)),
            pl.BlockSpec((halo, WIDTH_B), prev_map(kb)),
            pl.BlockSpec((main, WIDTH_B), main_map(kb)),
            pl.BlockSpec((halo, WIDTH_B), next_map(kb)),
            pl.BlockSpec((halo, WIDTH_B), prev_map(vb)),
            pl.BlockSpec((main, WIDTH_B), main_map(vb)),
            pl.BlockSpec((halo, WIDTH_B), next_map(vb)),
            pl.BlockSpec((N_HEADS_B // 2, NA_ROWS, 2 * GRID_W, NAT_KEYS), lambda b, i: (0, 0, 0, 0)),
        ],
        out_specs=pl.BlockSpec((main, WIDTH_B), main_map(0)),
        scratch_shapes=[pltpu.VMEM((main + 2 * halo, WIDTH_B), jnp.bfloat16),
                        pltpu.VMEM((main + 2 * halo, WIDTH_B), jnp.bfloat16)],
        compiler_params=_cparams("parallel", "parallel"),
        name="nat",
    )(qkv, qkv, qkv, qkv, qkv, qkv, qkv, tb)


EXPERT_ROW0 = N_GROUPS
ROUTE_ROWS = 48
INFO_E1, INFO_E2, INFO_R1, INFO_R2, INFO_W1, INFO_W2 = range(6)
INFO_ROWS = 8
MERGE_SUBTILES = 2


def _route(lt, carry, tri):
    rr, tm = lt.shape
    row = lax.broadcasted_iota(jnp.int32, (rr, tm), 0).astype(jnp.float32)
    none = jnp.float32(rr)

    def first_max(sel):
        m = jnp.max(jnp.where(sel, lt, NEG_BIG), axis=0, keepdims=True)
        idx = jnp.min(jnp.where(sel & (lt == m), row, none), axis=0, keepdims=True)
        return m, idx

    is_group = row < N_GROUPS
    mg, g = first_max(is_group)
    pg_sel = 1.0 / jnp.sum(jnp.where(is_group, jnp.exp(jnp.where(is_group, lt, mg) - mg), 0.0),
                           axis=0, keepdims=True)
    row0 = EXPERT_ROW0 + EXPERTS_PER_GROUP * g
    in_group = (row >= row0) & (row < row0 + EXPERTS_PER_GROUP)
    m1, i1 = first_max(in_group)
    m2, i2 = first_max(in_group & (row != i1))
    e2 = jnp.exp(m2 - m1)
    w1 = pg_sel / (1.0 + e2)
    w2 = pg_sel * e2 / (1.0 + e2)

    oh1 = row == i1
    oh2 = row == i2
    both = (oh1 | oh2).astype(jnp.bfloat16)
    before = jnp.dot(both, tri, preferred_element_type=jnp.float32) + carry
    r1 = jnp.sum(jnp.where(oh1, before, 0.0), axis=0, keepdims=True)
    r2 = jnp.sum(jnp.where(oh2, before, 0.0), axis=0, keepdims=True)
    new_carry = carry + jnp.sum(both.astype(jnp.float32), axis=1, keepdims=True)

    field = lax.broadcasted_iota(jnp.int32, (INFO_ROWS, tm), 0)
    info = jnp.zeros((INFO_ROWS, tm), jnp.float32)
    for k, v in ((INFO_E1, i1 - EXPERT_ROW0), (INFO_E2, i2 - EXPERT_ROW0), (INFO_R1, r1), (INFO_R2, r2),
                 (INFO_W1, w1), (INFO_W2, w2)):
        info = jnp.where(field == k, v, info)
    return info, new_carry


def _merge_kernel(x_ref, oa_ref, ob_ref, lng_ref, lnb_ref, wg_ref, wpa_ref, wpb_ref, wo_ref,
                  l1g_ref, l1b_ref, wr_ref, br_ref, cnt0_ref,
                  h1_ref, h1p_ref, info_ref, cnt_ref, carry_ref, tri_ref):
    tm = x_ref.shape[0]

    @pl.when(pl.program_id(0) == 0)
    def _():
        carry_ref[...] = cnt0_ref[...]
        r = lax.broadcasted_iota(jnp.int32, (tm, tm), 0)
        c = lax.broadcasted_iota(jnp.int32, (tm, tm), 1)
        tri_ref[...] = (r < c).astype(jnp.bfloat16)

    def project(rows):
        h = _layer_norm(x_ref[rows], lng_ref[...], lnb_ref[...])
        gates = jnp.dot(h.astype(jnp.bfloat16), wg_ref[...], preferred_element_type=jnp.float32)
        pa = jnp.dot(oa_ref[rows], wpa_ref[...], preferred_element_type=jnp.float32)
        pb = jnp.dot(ob_ref[rows], wpb_ref[...], preferred_element_type=jnp.float32)
        return h, gates, pa, pb

    def mix(h, gates, pa, pb):
        mixin = jax.nn.sigmoid(gates[:, :D_MODEL]) * pa + jax.nn.sigmoid(gates[:, D_MODEL:]) * pb
        return DEEPNORM_ALPHA * h + jnp.dot(mixin.astype(jnp.bfloat16), wo_ref[...],
                                            preferred_element_type=jnp.float32)

    def norm_and_logits(pre, rows):
        h1 = _layer_norm(pre, l1g_ref[...], l1b_ref[...])
        h1_ref[rows] = h1
        h1p_ref[rows] = _pack_rows(h1)
        hi = h1.astype(jnp.bfloat16)
        lo = (h1 - hi.astype(jnp.float32)).astype(jnp.bfloat16)
        lhs = jnp.concatenate([hi, lo, hi], axis=1)
        return lax.dot_general(wr_ref[...], lhs, (((1,), (1,)), ((), ())), preferred_element_type=jnp.float32)

    sub = tm // MERGE_SUBTILES
    parts = [slice(k * sub, (k + 1) * sub) for k in range(MERGE_SUBTILES)]
    projected = [project(rows) for rows in parts]
    mixed = [mix(*pr) for pr in projected]
    logits_t = jnp.concatenate([norm_and_logits(pre, rows) for pre, rows in zip(mixed, parts)], axis=1)
    logits_t = logits_t + br_ref[...]
    info, carry = _route(logits_t, carry_ref[...], tri_ref[...])
    info_ref[...] = info
    carry_ref[...] = carry
    cnt_ref[...] = carry[:, :LANES]


def _merge(x2, oa, ob, ln_g, ln_b, w_gates, w_pa, w_pb, w_o, l1g, l1b, w_r, b_r, cnt0, tm):
    n = x2.shape[0]

    def const(shape):
        return pl.BlockSpec(shape, lambda i: (0,) * len(shape))

    def rows(width):
        return pl.BlockSpec((tm, width), lambda i: (i, 0))

    return pl.pallas_call(
        _merge_kernel,
        out_shape=(jax.ShapeDtypeStruct((n, D_MODEL), jnp.float32),
                   jax.ShapeDtypeStruct((n, PACKED_WIDTH), jnp.uint32),
                   jax.ShapeDtypeStruct((INFO_ROWS, n), jnp.float32),
                   jax.ShapeDtypeStruct((ROUTE_ROWS, LANES), jnp.float32)),
        grid=(n // tm,),
        in_specs=[
            rows(D_MODEL), rows(WIDTH_A), rows(WIDTH_B),
            const((1, D_MODEL)), const((1, D_MODEL)),
            const((D_MODEL, 2 * D_MODEL)),
            const((WIDTH_A, D_MODEL)), const((WIDTH_B, D_MODEL)),
            const((D_MODEL, D_MODEL)),
            const((1, D_MODEL)), const((1, D_MODEL)),
            const((ROUTE_ROWS, 3 * D_MODEL)), const((ROUTE_ROWS, tm)), const((ROUTE_ROWS, tm)),
        ],
        out_specs=(rows(D_MODEL), rows(PACKED_WIDTH), pl.BlockSpec((INFO_ROWS, tm), lambda i: (0, i)),
                   const((ROUTE_ROWS, LANES))),
        scratch_shapes=[pltpu.VMEM((ROUTE_ROWS, tm), jnp.float32), pltpu.VMEM((tm, tm), jnp.bfloat16)],
        compiler_params=_cparams("arbitrary"),
        name="merge",
    )(x2, oa, ob, ln_g, ln_b, w_gates, w_pa, w_pb, w_o, l1g, l1b, w_r, b_r, cnt0)


def _row_copy(src_ref, src_row, dst_ref, dst_row, sem):
    return pltpu.make_async_copy(src_ref.at[pl.ds(src_row, 1)], dst_ref.at[pl.ds(dst_row, 1)], sem)


ROW_DMA_UNROLL = 8


def _tile_dest(dest_t, tt):
    n = dest_t.shape[1]
    return dest_t.reshape(TOP_K, n // tt, tt).transpose(1, 0, 2).reshape(n // tt, 1, TOP_K * tt)


def _dispatch_kernel(dest_ref, h_ref, xs_ref, sem):
    tt = h_ref.shape[0]

    def issue(t, c):
        for k in range(TOP_K):
            _row_copy(h_ref, t, xs_ref, dest_ref[0, 0, k * tt + t], sem).start()
        return c

    lax.fori_loop(0, tt, issue, 0, unroll=ROW_DMA_UNROLL)
    for k in range(TOP_K):
        pltpu.make_async_copy(h_ref, xs_ref.at[pl.ds(0, tt)], sem).wait()


def _dispatch(h1p, dest_t, tt):
    n, width = h1p.shape
    dest3 = _tile_dest(dest_t, tt)
    return pl.pallas_call(
        _dispatch_kernel,
        out_shape=jax.ShapeDtypeStruct((TOP_K * n, width), h1p.dtype),
        grid=(n // tt,),
        in_specs=[
            pl.BlockSpec((1, 1, TOP_K * tt), lambda i: (i, 0, 0), memory_space=pltpu.SMEM),
            pl.BlockSpec((tt, width), lambda i: (i, 0)),
        ],
        out_specs=pl.BlockSpec(memory_space=pl.ANY),
        scratch_shapes=[pltpu.SemaphoreType.DMA(())],
        compiler_params=_cparams("arbitrary"),
        name="dispatch",
    )(dest3, h1p)


def _moe_plan(counts, na, bm):
    ends = jnp.cumsum(counts)
    starts = ends - counts
    first_blk = starts // bm
    tiles = jnp.where(counts > 0, (ends - 1) // bm - first_blk + 1, 0)
    item_end = jnp.cumsum(tiles)
    item_start = item_end - tiles
    total = item_end[-1]
    wmax = na // bm + N_EXPERTS - 1
    w = jnp.arange(wmax, dtype=jnp.int32)
    wc = jnp.minimum(w, total - 1)
    e = jnp.sum((item_end[None, :] <= wc[:, None]).astype(jnp.int32), axis=1)
    e = jnp.minimum(e, N_EXPERTS - 1)
    blk = (first_blk[e] + (wc - item_start[e])).astype(jnp.int32)
    valid = w < total
    lo = jnp.where(valid, jnp.maximum(starts[e], blk * bm), 0).astype(jnp.int32)
    hi = jnp.where(valid, jnp.minimum(ends[e], (blk + 1) * bm), 0).astype(jnp.int32)
    prev_blk = jnp.concatenate([jnp.full((1,), -1, jnp.int32), blk[:-1]])
    prev_e = jnp.concatenate([jnp.full((1,), -1, jnp.int32), e[:-1]])
    flags = (valid.astype(jnp.int32)
             + 2 * (valid & (blk != prev_blk)).astype(jnp.int32)
             + 4 * (valid & (e != prev_e)).astype(jnp.int32))
    return blk, e, lo, hi, flags


FLAG_VALID, FLAG_NEW_BLOCK, FLAG_NEW_EXPERT = 1, 2, 4


def _expert_kernel(blk_ref, e_ref, lo_ref, hi_ref, flag_ref, x_ref, wg_ref, wu_ref, wd_ref, o_ref,
                   wg_b, wu_b, wd_b):
    w = pl.program_id(0)
    bm = x_ref.shape[0]
    flags = flag_ref[w]

    @pl.when((flags & FLAG_NEW_EXPERT) != 0)
    def _():
        wg_b[...] = wg_ref[0].astype(jnp.bfloat16)
        wu_b[...] = wu_ref[0].astype(jnp.bfloat16)
        wd_b[...] = wd_ref[0].astype(jnp.bfloat16)

    @pl.when((flags & FLAG_VALID) != 0)
    def _():
        x = _unpack_rows(x_ref[...]).astype(jnp.bfloat16)
        g = jnp.dot(x, wg_b[...], preferred_element_type=jnp.float32)
        u = jnp.dot(x, wu_b[...], preferred_element_type=jnp.float32)
        hmid = (jax.nn.silu(g) * u).astype(jnp.bfloat16)
        y = _pack_rows(jnp.dot(hmid, wd_b[...], preferred_element_type=jnp.float32))
        row = blk_ref[w] * bm + lax.broadcasted_iota(jnp.int32, (bm, 1), 0)
        mine = (row >= lo_ref[w]) & (row < hi_ref[w])

        @pl.when((flags & FLAG_NEW_BLOCK) != 0)
        def _():
            o_ref[...] = jnp.where(mine, y, jnp.uint32(0))

        @pl.when((flags & FLAG_NEW_BLOCK) == 0)
        def _():
            o_ref[...] = jnp.where(mine, y, o_ref[...])


def _experts(xs, plan, w_gate, w_up, w_down, bm):
    na = xs.shape[0]
    blk, e, lo, hi, flags = plan
    nitems = blk.shape[0]
    grid_spec = pltpu.PrefetchScalarGridSpec(
        num_scalar_prefetch=5,
        grid=(nitems,),
        in_specs=[
            pl.BlockSpec((bm, PACKED_WIDTH), lambda w, blk, e, *_: (blk[w], 0)),
            pl.BlockSpec((1, D_MODEL, D_EXPERT), lambda w, blk, e, *_: (e[w], 0, 0)),
            pl.BlockSpec((1, D_MODEL, D_EXPERT), lambda w, blk, e, *_: (e[w], 0, 0)),
            pl.BlockSpec((1, D_EXPERT, D_MODEL), lambda w, blk, e, *_: (e[w], 0, 0)),
        ],
        out_specs=pl.BlockSpec((bm, PACKED_WIDTH), lambda w, blk, e, *_: (blk[w], 0)),
        scratch_shapes=[pltpu.VMEM((D_MODEL, D_EXPERT), jnp.bfloat16),
                        pltpu.VMEM((D_MODEL, D_EXPERT), jnp.bfloat16),
                        pltpu.VMEM((D_EXPERT, D_MODEL), jnp.bfloat16)],
    )
    return pl.pallas_call(
        _expert_kernel,
        out_shape=jax.ShapeDtypeStruct((na, PACKED_WIDTH), jnp.uint32),
        grid_spec=grid_spec,
        compiler_params=_cparams("arbitrary"),
        name="experts",
    )(blk, e, lo, hi, flags, xs, w_gate, w_up, w_down)


def _combine_kernel(dest_ref, dest_next_ref, h1_ref, info_ref, g_ref, b_ref, ys_ref, o_ref, ybuf, sem):
    i = pl.program_id(0)
    nsteps = pl.num_programs(0)
    tt = h1_ref.shape[0]
    slot = i % 2

    def gather(d_ref, s):
        def issue(t, c):
            for k in range(TOP_K):
                _row_copy(ys_ref, d_ref[0, 0, k * tt + t], ybuf.at[s, k], t, sem.at[s]).start()
            return c
        lax.fori_loop(0, tt, issue, 0, unroll=ROW_DMA_UNROLL)

    @pl.when(i == 0)
    def _():
        gather(dest_ref, 0)

    @pl.when(i + 1 < nsteps)
    def _():
        gather(dest_next_ref, 1 - slot)

    for k in range(TOP_K):
        pltpu.make_async_copy(ys_ref.at[pl.ds(0, tt)], ybuf.at[slot, k], sem.at[slot]).wait()
    pad = jnp.zeros((LANES - INFO_ROWS, tt), jnp.float32)
    info = jnp.concatenate([info_ref[...], pad], axis=0).T
    moe = (_unpack_rows(ybuf[slot, 0]) * info[:, INFO_W1:INFO_W1 + 1]
           + _unpack_rows(ybuf[slot, 1]) * info[:, INFO_W2:INFO_W2 + 1])
    o_ref[...] = _layer_norm(DEEPNORM_ALPHA * h1_ref[...] + moe, g_ref[...], b_ref[...])


def _combine(h1, info, dest_t, ys, ln_g, ln_b, tt):
    n = h1.shape[0]
    nsteps = n // tt
    dest3 = _tile_dest(dest_t, tt)
    return pl.pallas_call(
        _combine_kernel,
        out_shape=jax.ShapeDtypeStruct((n, D_MODEL), jnp.float32),
        grid=(nsteps,),
        in_specs=[
            pl.BlockSpec((1, 1, TOP_K * tt), lambda i: (i, 0, 0), memory_space=pltpu.SMEM),
            pl.BlockSpec((1, 1, TOP_K * tt), lambda i: (jnp.minimum(i + 1, nsteps - 1), 0, 0),
                         memory_space=pltpu.SMEM),
            pl.BlockSpec((tt, D_MODEL), lambda i: (i, 0)),
            pl.BlockSpec((INFO_ROWS, tt), lambda i: (0, i)),
            pl.BlockSpec((1, D_MODEL), lambda i: (0, 0)),
            pl.BlockSpec((1, D_MODEL), lambda i: (0, 0)),
            pl.BlockSpec(memory_space=pl.ANY),
        ],
        out_specs=pl.BlockSpec((tt, D_MODEL), lambda i: (i, 0)),
        scratch_shapes=[pltpu.VMEM((2, TOP_K, tt, PACKED_WIDTH), jnp.uint32), pltpu.SemaphoreType.DMA((2,))],
        compiler_params=_cparams("arbitrary"),
        name="combine",
    )(dest3, dest3, h1, info, ln_g, ln_b, ys)


TM_QKV = 512
TQ_WIN = 512
TM_MERGE = 512
TT_ROWS = 256
BM_EXPERT = 256


def _prepare_weights(ln_in_g, ln_in_b, w_in, attn_sink, rel_pos_bias, w_proj_a, w_proj_b, w_out,
                     ln1_g, ln1_b, w_route_group, b_route_group, w_route_expert, b_route_expert,
                     ln2_g, ln2_b):
    bf = jnp.bfloat16
    w = w_in[0]
    splits = np.cumsum([WIDTH_A, KV_WIDTH_A, KV_WIDTH_A, WIDTH_B, WIDTH_B, WIDTH_B, D_MODEL])
    wqa, wka, wva, wqb, wkb, wvb, wga, wgb = jnp.split(w, [int(s) for s in splits], axis=1)
    wqa = (wqa.reshape(D_MODEL, N_KV_HEADS_A, GQA_GROUP, HEAD_DIM).transpose(0, 2, 1, 3)
           .reshape(D_MODEL, WIDTH_A))
    w_qkv = jnp.concatenate([wqa, wqb, wkb, wvb, wka, wva], axis=1).astype(bf)
    w_gates = jnp.concatenate([wga, wgb], axis=1).astype(bf)
    w_pa = (w_proj_a[0].reshape(N_KV_HEADS_A, GQA_GROUP, HEAD_DIM, D_MODEL).transpose(1, 0, 2, 3)
            .reshape(WIDTH_A, D_MODEL).astype(bf))
    w_pb = w_proj_b[0].astype(bf)
    w_o = w_out[0].astype(bf)
    pad = ROUTE_ROWS - N_GROUPS - N_EXPERTS
    w_r = jnp.concatenate([w_route_group[0].T, w_route_expert[0].T, jnp.zeros((pad, D_MODEL), jnp.float32)], axis=0)
    w_r_hi = w_r.astype(bf)
    w_r_lo = (w_r - w_r_hi.astype(jnp.float32)).astype(bf)
    w_r3 = jnp.concatenate([w_r_hi, w_r_hi, w_r_lo], axis=1)
    b_r = jnp.concatenate([b_route_group[0], b_route_expert[0], jnp.zeros((pad,), jnp.float32)])
    b_r = jnp.broadcast_to(b_r[:, None], (ROUTE_ROWS, TM_MERGE))
    row = lambda v: v.reshape(1, D_MODEL)
    return dict(
        ln_in_g=row(ln_in_g), ln_in_b=row(ln_in_b), w_qkv=w_qkv, w_gates=w_gates,
        sink=attn_sink[0].astype(jnp.float32), nat_bias=_nat_bias_table(rel_pos_bias[0]),
        w_pa=w_pa, w_pb=w_pb, w_o=w_o, ln1_g=row(ln1_g[0]), ln1_b=row(ln1_b[0]),
        w_r3=w_r3, b_r=b_r, ln2_g=row(ln2_g[0]), ln2_b=row(ln2_b[0]))


def _encode(x, p, w_gate, w_up, w_down):
    bsz, t, _ = x.shape
    n = bsz * t
    x2 = x.reshape(n, D_MODEL)
    qkv = _qkv(x2, p["ln_in_g"], p["ln_in_b"], p["w_qkv"], TM_QKV)
    oa = _win_attention(qkv, p["sink"], bsz, t, TQ_WIN)
    ob = _nat_attention(qkv, p["nat_bias"], bsz, t)
    cnt0 = jnp.zeros((ROUTE_ROWS, TM_MERGE), jnp.float32)
    h1, h1p, info, cnt = _merge(x2, oa, ob, p["ln_in_g"], p["ln_in_b"], p["w_gates"], p["w_pa"], p["w_pb"],
                                p["w_o"], p["ln1_g"], p["ln1_b"], p["w_r3"], p["b_r"], cnt0, TM_MERGE)
    counts = cnt[EXPERT_ROW0:EXPERT_ROW0 + N_EXPERTS, 0].astype(jnp.int32)
    starts = jnp.cumsum(counts) - counts
    eid = info[INFO_E1:INFO_E2 + 1].astype(jnp.int32)
    rank = info[INFO_R1:INFO_R2 + 1].astype(jnp.int32)
    expert = jnp.arange(N_EXPERTS, dtype=jnp.int32)[:, None, None]
    dest_t = rank + jnp.sum(jnp.where(eid[None] == expert, starts[:, None, None], 0), axis=0)
    xs = _dispatch(h1p, dest_t, TT_ROWS)
    plan = _moe_plan(counts, TOP_K * n, BM_EXPERT)
    ys = _experts(xs, plan, w_gate[0], w_up[0], w_down[0], BM_EXPERT)
    out = _combine(h1, info, dest_t, ys, p["ln2_g"], p["ln2_b"], TT_ROWS)
    return out.reshape(bsz, t, D_MODEL)


def kernel(x_prompt, x_sample, ln_in_g, ln_in_b, w_in, attn_sink, rel_pos_bias, w_proj_a, w_proj_b, w_out,
           ln1_g, ln1_b, w_route_group, b_route_group, w_route_expert, b_route_expert,
           w_gate, w_up, w_down, ln2_g, ln2_b):
    p = _prepare_weights(ln_in_g, ln_in_b, w_in, attn_sink, rel_pos_bias, w_proj_a, w_proj_b, w_out,
                         ln1_g, ln1_b, w_route_group, b_route_group, w_route_expert, b_route_expert,
                         ln2_g, ln2_b)
    return (_encode(x_prompt, p, w_gate, w_up, w_down), _encode(x_sample, p, w_gate, w_up, w_down))
```

```python
import functools

import numpy as np
import jax
import jax.numpy as jnp
from jax import lax
from jax.experimental import pallas as pl
from jax.experimental.pallas import tpu as pltpu

D_MODEL = 1024
HEAD_DIM = 64
N_HEADS_A = 8
N_KV_HEADS_A = 2
WINDOW = 128
N_HEADS_B = 8
GRID_W = 64
NA_ROWS = 8
NA_COLS = 16
N_GROUPS = 4
EXPERTS_PER_GROUP = 8
N_EXPERTS = N_GROUPS * EXPERTS_PER_GROUP
TOP_K = 2
D_EXPERT = D_MODEL // 2
LN_EPS = 1e-5
DEPTH = 1
DEEPNORM_ALPHA = (2.0 * DEPTH) ** 0.25
WIDTH_A = N_HEADS_A * HEAD_DIM
KV_WIDTH_A = N_KV_HEADS_A * HEAD_DIM
WIDTH_B = N_HEADS_B * HEAD_DIM
QKV_WIDTH = WIDTH_A + 2 * KV_WIDTH_A + 3 * WIDTH_B

LANES = 128
VMEM_LIMIT_BYTES = 56 * 1024 * 1024

NEG_BIG = -1e30

QA_COL, QB_COL, KB_COL, VB_COL = 0, WIDTH_A, WIDTH_A + WIDTH_B, WIDTH_A + 2 * WIDTH_B
KA_COL = WIDTH_A + 3 * WIDTH_B
VA_COL = KA_COL + KV_WIDTH_A

GQA_GROUP = N_HEADS_A // N_KV_HEADS_A


def _cparams(*sem):
    return pltpu.CompilerParams(dimension_semantics=sem, vmem_limit_bytes=VMEM_LIMIT_BYTES)


def _layer_norm(x, g, b):
    mu = jnp.mean(x, axis=-1, keepdims=True)
    xc = x - mu
    var = jnp.mean(xc * xc, axis=-1, keepdims=True)
    return xc * lax.rsqrt(var + LN_EPS) * g + b


PACKED_WIDTH = D_MODEL // 2


def _pack_rows(x):
    def rne(v):
        return v + jnp.uint32(0x7FFF) + ((v >> 16) & jnp.uint32(1))
    hi = lax.bitcast_convert_type(x[:, :PACKED_WIDTH], jnp.uint32)
    lo = lax.bitcast_convert_type(x[:, PACKED_WIDTH:], jnp.uint32)
    return (rne(hi) & jnp.uint32(0xFFFF0000)) | (rne(lo) >> 16)


def _unpack_rows(w):
    hi = lax.bitcast_convert_type(w & jnp.uint32(0xFFFF0000), jnp.float32)
    lo = lax.bitcast_convert_type(w << 16, jnp.float32)
    return jnp.concatenate([hi, lo], axis=1)


def _qkv_kernel(x_ref, g_ref, b_ref, w_ref, o_ref):
    h = _layer_norm(x_ref[...], g_ref[...], b_ref[...])
    y = jnp.dot(h.astype(jnp.bfloat16), w_ref[...], preferred_element_type=jnp.float32)
    col = lax.broadcasted_iota(jnp.int32, (1, QKV_WIDTH), 1)
    y = y * jnp.where(col < KB_COL, HEAD_DIM ** -0.5, 1.0)
    o_ref[...] = y.astype(jnp.bfloat16)


def _qkv(x2, ln_g, ln_b, w_qkv, tm):
    n = x2.shape[0]
    return pl.pallas_call(
        _qkv_kernel,
        out_shape=jax.ShapeDtypeStruct((n, QKV_WIDTH), jnp.bfloat16),
        grid=(n // tm,),
        in_specs=[
            pl.BlockSpec((tm, D_MODEL), lambda i: (i, 0)),
            pl.BlockSpec((1, D_MODEL), lambda i: (0, 0)),
            pl.BlockSpec((1, D_MODEL), lambda i: (0, 0)),
            pl.BlockSpec((D_MODEL, QKV_WIDTH), lambda i: (0, 0)),
        ],
        out_specs=pl.BlockSpec((tm, QKV_WIDTH), lambda i: (i, 0)),
        compiler_params=_cparams("parallel"),
        name="qkv",
    )(x2, ln_g, ln_b, w_qkv)


WIN_BLK = 128
WIN_LOOKAHEAD = 2


def _win_bias_table():
    qi = np.arange(WIN_BLK)[:, None]
    kj = np.arange(3 * WIN_BLK)[None, :]
    dist = np.abs(kj - WIN_BLK - qi).astype(np.float64)
    slopes = 2.0 ** (-8.0 * np.arange(1, N_HEADS_A + 1) / N_HEADS_A)
    per_head = np.where(dist <= WINDOW, -slopes[:, None, None] * dist[None], NEG_BIG)
    groups = [np.concatenate([per_head[j], per_head[j + 4]], axis=0) for j in range(4)]
    return np.stack(groups).astype(np.float32)


def _win_kernel(sink_ref, q_ref, kp_ref, km_ref, kn_ref, vp_ref, vm_ref, vn_ref, bias_ref, o_ref,
                *, nsub, nblk_seq):
    i = pl.program_id(1)
    kcat = jnp.concatenate([kp_ref[...], km_ref[...], kn_ref[...]], axis=0)
    vcat = jnp.concatenate([vp_ref[...], vm_ref[...], vn_ref[...]], axis=0)
    lo = lax.broadcasted_iota(jnp.int32, (1, LANES), 1) < HEAD_DIM
    col = lax.broadcasted_iota(jnp.int32, (1, 3 * WIN_BLK), 1)
    top = lax.broadcasted_iota(jnp.int32, (2 * WIN_BLK, 1), 0) < WIN_BLK
    zero = jnp.zeros((), jnp.bfloat16)

    def scores(j, g):
        n = i * nsub + j
        off_seq = ((col < WIN_BLK) & (n == 0)) | ((col >= 2 * WIN_BLK) & (n == nblk_seq - 1))
        edge = jnp.where(off_seq, NEG_BIG, 0.0)
        qg = q_ref[WIN_BLK * j:WIN_BLK * (j + 1), LANES * g:LANES * (g + 1)]
        qm = jnp.concatenate([jnp.where(lo, qg, zero), jnp.where(lo, zero, qg)], axis=0)
        kj = kcat[WIN_BLK * j:WIN_BLK * (j + 3)]
        s = lax.dot_general(qm, kj, (((1,), (1,)), ((), ())), preferred_element_type=jnp.float32)
        return s + bias_ref[g] + edge

    def attend(s, j, g):
        vj = vcat[WIN_BLK * j:WIN_BLK * (j + 3)]
        sink = jnp.where(top, sink_ref[g], sink_ref[g + 4])
        m = jnp.maximum(jnp.max(s, axis=-1, keepdims=True), sink)
        p = jnp.exp(s - m)
        l = jnp.sum(p, axis=-1, keepdims=True) + jnp.exp(sink - m)
        o2 = jnp.dot(p.astype(jnp.bfloat16), vj, preferred_element_type=jnp.float32)
        o2 = o2 * (1.0 / l)
        o_ref[WIN_BLK * j:WIN_BLK * (j + 1), LANES * g:LANES * (g + 1)] = (
            jnp.where(lo, o2[:WIN_BLK], o2[WIN_BLK:]).astype(jnp.bfloat16))

    chains = [(j, g) for j in range(nsub) for g in range(4)]
    pending = [scores(*c) for c in chains[:WIN_LOOKAHEAD]]
    for idx, c in enumerate(chains):
        s = pending.pop(0)
        if idx + WIN_LOOKAHEAD < len(chains):
            pending.append(scores(*chains[idx + WIN_LOOKAHEAD]))
        attend(s, *c)


def _win_attention(qkv, sink, bsz, t, tq):
    n = bsz * t
    nsub = tq // WIN_BLK
    nblk_seq = t // WIN_BLK
    ntile = t // tq
    bias = jnp.asarray(_win_bias_table())

    def main_map(col):
        return lambda b, i, *_: (b * ntile + i, col)

    def prev_map(col):
        return lambda b, i, *_: (b * nblk_seq + jnp.maximum(i * nsub - 1, 0), col)

    def next_map(col):
        return lambda b, i, *_: (b * nblk_seq + jnp.minimum(i * nsub + nsub, nblk_seq - 1), col)

    halo = (WIN_BLK, LANES)
    ka, va = KA_COL // LANES, VA_COL // LANES
    grid_spec = pltpu.PrefetchScalarGridSpec(
        num_scalar_prefetch=1,
        grid=(bsz, ntile),
        in_specs=[
            pl.BlockSpec((tq, WIDTH_A), main_map(QA_COL // WIDTH_A)),
            pl.BlockSpec(halo, prev_map(ka)),
            pl.BlockSpec((tq, LANES), main_map(ka)),
            pl.BlockSpec(halo, next_map(ka)),
            pl.BlockSpec(halo, prev_map(va)),
            pl.BlockSpec((tq, LANES), main_map(va)),
            pl.BlockSpec(halo, next_map(va)),
            pl.BlockSpec((4, 2 * WIN_BLK, 3 * WIN_BLK), lambda b, i, *_: (0, 0, 0)),
        ],
        out_specs=pl.BlockSpec((tq, WIDTH_A), main_map(0)),
    )
    return pl.pallas_call(
        functools.partial(_win_kernel, nsub=nsub, nblk_seq=nblk_seq),
        out_shape=jax.ShapeDtypeStruct((n, WIDTH_A), jnp.bfloat16),
        grid_spec=grid_spec,
        compiler_params=_cparams("parallel", "parallel"),
        name="win",
    )(sink, qkv, qkv, qkv, qkv, qkv, qkv, qkv, bias)


NAT_ROWS_PER_STEP = 8
NAT_HALO_ROWS = NA_ROWS // 2
NAT_KEYS = NA_ROWS * GRID_W
NAT_ROWS_PER_TRIP = 8
NAT_LOOKAHEAD = 4


def _nat_bias_table(rpb):
    c = np.arange(GRID_W)
    cs = np.clip(c - NA_COLS // 2, 0, GRID_W - NA_COLS)
    col_mask = (c[None, :] >= cs[:, None]) & (c[None, :] < cs[:, None] + NA_COLS)
    dc = np.clip(c[None, :] - c[:, None] + (NA_COLS - 1), 0, 2 * NA_COLS - 2)
    onehot = jnp.asarray(dc[None] == np.arange(2 * NA_COLS - 1)[:, None, None], jnp.float32)
    picked = jnp.einsum("hdj,jqc->hdqc", rpb, onehot, precision=lax.Precision.HIGHEST)
    t1 = jnp.where(col_mask[None, None], picked, NEG_BIG)
    per_shift = []
    for sh in range(NA_ROWS):
        w = t1[:, sh:sh + NA_ROWS]
        w = jnp.transpose(w, (0, 2, 1, 3)).reshape(N_HEADS_B // 2, 2 * GRID_W, NAT_KEYS)
        per_shift.append(w)
    return jnp.stack(per_shift, axis=1).astype(jnp.float32)


def _nat_kernel(q_ref, kp_ref, km_ref, kn_ref, vp_ref, vm_ref, vn_ref, tb_ref, o_ref, kcat, vcat,
                *, rows_seq):
    i = pl.program_id(1)
    halo = NAT_HALO_ROWS * GRID_W
    main = NAT_ROWS_PER_STEP * GRID_W
    kcat[0:halo] = kp_ref[...]
    kcat[halo:halo + main] = km_ref[...]
    kcat[halo + main:2 * halo + main] = kn_ref[...]
    vcat[0:halo] = vp_ref[...]
    vcat[halo:halo + main] = vm_ref[...]
    vcat[halo + main:2 * halo + main] = vn_ref[...]
    lo = lax.broadcasted_iota(jnp.int32, (1, LANES), 1) < HEAD_DIM
    zero = jnp.zeros((), jnp.bfloat16)
    r0 = i * NAT_ROWS_PER_STEP

    def scores(qr, p):
        r = r0 + qr
        rs = jnp.clip(r - NA_ROWS // 2, 0, rows_seq - NA_ROWS)
        koff = pl.multiple_of((rs - r0 + NAT_HALO_ROWS) * GRID_W, GRID_W)
        sh = rs - r + (NA_ROWS - 1)
        qoff = pl.multiple_of(qr * GRID_W, GRID_W)
        cols = slice(LANES * p, LANES * (p + 1))
        qp = q_ref[pl.ds(qoff, GRID_W), cols]
        qm = jnp.concatenate([jnp.where(lo, qp, zero), jnp.where(lo, zero, qp)], axis=0)
        kw = kcat[pl.ds(koff, NAT_KEYS), cols]
        s = lax.dot_general(qm, kw, (((1,), (1,)), ((), ())), preferred_element_type=jnp.float32)
        return s + tb_ref[p, sh], koff, qoff

    def attend(s, koff, qoff, p):
        cols = slice(LANES * p, LANES * (p + 1))
        vw = vcat[pl.ds(koff, NAT_KEYS), cols]
        m = jnp.max(s, axis=-1, keepdims=True)
        pe = jnp.exp(s - m)
        l = jnp.sum(pe, axis=-1, keepdims=True)
        o2 = jnp.dot(pe.astype(jnp.bfloat16), vw, preferred_element_type=jnp.float32)
        o2 = o2 * (1.0 / l)
        o_ref[pl.ds(qoff, GRID_W), cols] = jnp.where(lo, o2[:GRID_W], o2[GRID_W:]).astype(jnp.bfloat16)

    def trip(j, carry):
        chains = [(j * NAT_ROWS_PER_TRIP + q, p) for q in range(NAT_ROWS_PER_TRIP) for p in range(N_HEADS_B // 2)]
        pending = [scores(*c) for c in chains[:NAT_LOOKAHEAD]]
        for idx, (_, p) in enumerate(chains):
            s, koff, qoff = pending.pop(0)
            if idx + NAT_LOOKAHEAD < len(chains):
                pending.append(scores(*chains[idx + NAT_LOOKAHEAD]))
            attend(s, koff, qoff, p)
        return carry

    lax.fori_loop(0, NAT_ROWS_PER_STEP // NAT_ROWS_PER_TRIP, trip, 0)


def _nat_attention(qkv, tb, bsz, t):
    n = bsz * t
    rows_seq = t // GRID_W
    main = NAT_ROWS_PER_STEP * GRID_W
    halo = NAT_HALO_ROWS * GRID_W
    ntile = t // main
    nhalo_seq = t // halo
    per = main // halo

    def main_map(col):
        return lambda b, i: (b * ntile + i, col)

    def prev_map(col):
        return lambda b, i: (b * nhalo_seq + jnp.maximum(i * per - 1, 0), col)

    def next_map(col):
        return lambda b, i: (b * nhalo_seq + jnp.minimum(i * per + per, nhalo_seq - 1), col)

    qb, kb, vb = QB_COL // WIDTH_B, KB_COL // WIDTH_B, VB_COL // WIDTH_B
    return pl.pallas_call(
        functools.partial(_nat_kernel, rows_seq=rows_seq),
        out_shape=jax.ShapeDtypeStruct((n, WIDTH_B), jnp.bfloat16),
        grid=(bsz, ntile),
        in_specs=[
            pl.BlockSpec((main, WIDTH_B), main_map(qb)),
            pl.BlockSpec((halo, WIDTH_B), prev_map(kb)),
            pl.BlockSpec((main, WIDTH_B), main_map(kb)),
            pl.BlockSpec((halo, WIDTH_B), next_map(kb)),
            pl.BlockSpec((halo, WIDTH_B), prev_map(vb)),
            pl.BlockSpec((main, WIDTH_B), main_map(vb)),
            pl.BlockSpec((halo, WIDTH_B), next_map(vb)),
            pl.BlockSpec((N_HEADS_B // 2, NA_ROWS, 2 * GRID_W, NAT_KEYS), lambda b, i: (0, 0, 0, 0)),
        ],
        out_specs=pl.BlockSpec((main, WIDTH_B), main_map(0)),
        scratch_shapes=[pltpu.VMEM((main + 2 * halo, WIDTH_B), jnp.bfloat16),
                        pltpu.VMEM((main + 2 * halo, WIDTH_B), jnp.bfloat16)],
        compiler_params=_cparams("parallel", "parallel"),
        name="nat",
    )(qkv, qkv, qkv, qkv, qkv, qkv, qkv, tb)


EXPERT_ROW0 = N_GROUPS
ROUTE_ROWS = 48
INFO_E1, INFO_E2, INFO_R1, INFO_R2, INFO_W1, INFO_W2 = range(6)
INFO_ROWS = 8
MERGE_SUBTILES = 2


def _route(lt, carry, tri):
    rr, tm = lt.shape
    row = lax.broadcasted_iota(jnp.int32, (rr, tm), 0).astype(jnp.float32)
    none = jnp.float32(rr)

    def first_max(sel):
        m = jnp.max(jnp.where(sel, lt, NEG_BIG), axis=0, keepdims=True)
        idx = jnp.min(jnp.where(sel & (lt == m), row, none), axis=0, keepdims=True)
        return m, idx

    is_group = row < N_GROUPS
    mg, g = first_max(is_group)
    pg_sel = 1.0 / jnp.sum(jnp.where(is_group, jnp.exp(jnp.where(is_group, lt, mg) - mg), 0.0),
                           axis=0, keepdims=True)
    row0 = EXPERT_ROW0 + EXPERTS_PER_GROUP * g
    in_group = (row >= row0) & (row < row0 + EXPERTS_PER_GROUP)
    m1, i1 = first_max(in_group)
    m2, i2 = first_max(in_group & (row != i1))
    e2 = jnp.exp(m2 - m1)
    w1 = pg_sel / (1.0 + e2)
    w2 = pg_sel * e2 / (1.0 + e2)

    oh1 = row == i1
    oh2 = row == i2
    both = (oh1 | oh2).astype(jnp.bfloat16)
    before = jnp.dot(both, tri, preferred_element_type=jnp.float32) + carry
    r1 = jnp.sum(jnp.where(oh1, before, 0.0), axis=0, keepdims=True)
    r2 = jnp.sum(jnp.where(oh2, before, 0.0), axis=0, keepdims=True)
    new_carry = carry + jnp.sum(both.astype(jnp.float32), axis=1, keepdims=True)

    field = lax.broadcasted_iota(jnp.int32, (INFO_ROWS, tm), 0)
    info = jnp.zeros((INFO_ROWS, tm), jnp.float32)
    for k, v in ((INFO_E1, i1 - EXPERT_ROW0), (INFO_E2, i2 - EXPERT_ROW0), (INFO_R1, r1), (INFO_R2, r2),
                 (INFO_W1, w1), (INFO_W2, w2)):
        info = jnp.where(field == k, v, info)
    return info, new_carry


def _merge_kernel(x_ref, oa_ref, ob_ref, lng_ref, lnb_ref, wg_ref, wpa_ref, wpb_ref, wo_ref,
                  l1g_ref, l1b_ref, wr_ref, br_ref, cnt0_ref,
                  h1_ref, h1p_ref, info_ref, cnt_ref, carry_ref, tri_ref):
    tm = x_ref.shape[0]

    @pl.when(pl.program_id(0) == 0)
    def _():
        carry_ref[...] = cnt0_ref[...]
        r = lax.broadcasted_iota(jnp.int32, (tm, tm), 0)
        c = lax.broadcasted_iota(jnp.int32, (tm, tm), 1)
        tri_ref[...] = (r < c).astype(jnp.bfloat16)

    def project(rows):
        h = _layer_norm(x_ref[rows], lng_ref[...], lnb_ref[...])
        gates = jnp.dot(h.astype(jnp.bfloat16), wg_ref[...], preferred_element_type=jnp.float32)
        pa = jnp.dot(oa_ref[rows], wpa_ref[...], preferred_element_type=jnp.float32)
        pb = jnp.dot(ob_ref[rows], wpb_ref[...], preferred_element_type=jnp.float32)
        return h, gates, pa, pb

    def mix(h, gates, pa, pb):
        mixin = jax.nn.sigmoid(gates[:, :D_MODEL]) * pa + jax.nn.sigmoid(gates[:, D_MODEL:]) * pb
        return DEEPNORM_ALPHA * h + jnp.dot(mixin.astype(jnp.bfloat16), wo_ref[...],
                                            preferred_element_type=jnp.float32)

    def norm_and_logits(pre, rows):
        h1 = _layer_norm(pre, l1g_ref[...], l1b_ref[...])
        h1_ref[rows] = h1
        h1p_ref[rows] = _pack_rows(h1)
        hi = h1.astype(jnp.bfloat16)
        lo = (h1 - hi.astype(jnp.float32)).astype(jnp.bfloat16)
        lhs = jnp.concatenate([hi, lo, hi], axis=1)
        return lax.dot_general(wr_ref[...], lhs, (((1,), (1,)), ((), ())), preferred_element_type=jnp.float32)

    sub = tm // MERGE_SUBTILES
    parts = [slice(k * sub, (k + 1) * sub) for k in range(MERGE_SUBTILES)]
    projected = [project(rows) for rows in parts]
    mixed = [mix(*pr) for pr in projected]
    logits_t = jnp.concatenate([norm_and_logits(pre, rows) for pre, rows in zip(mixed, parts)], axis=1)
    logits_t = logits_t + br_ref[...]
    info, carry = _route(logits_t, carry_ref[...], tri_ref[...])
    info_ref[...] = info
    carry_ref[...] = carry
    cnt_ref[...] = carry[:, :LANES]


def _merge(x2, oa, ob, ln_g, ln_b, w_gates, w_pa, w_pb, w_o, l1g, l1b, w_r, b_r, cnt0, tm):
    n = x2.shape[0]

    def const(shape):
        return pl.BlockSpec(shape, lambda i: (0,) * len(shape))

    def rows(width):
        return pl.BlockSpec((tm, width), lambda i: (i, 0))

    return pl.pallas_call(
        _merge_kernel,
        out_shape=(jax.ShapeDtypeStruct((n, D_MODEL), jnp.float32),
                   jax.ShapeDtypeStruct((n, PACKED_WIDTH), jnp.uint32),
                   jax.ShapeDtypeStruct((INFO_ROWS, n), jnp.float32),
                   jax.ShapeDtypeStruct((ROUTE_ROWS, LANES), jnp.float32)),
        grid=(n // tm,),
        in_specs=[
            rows(D_MODEL), rows(WIDTH_A), rows(WIDTH_B),
            const((1, D_MODEL)), const((1, D_MODEL)),
            const((D_MODEL, 2 * D_MODEL)),
            const((WIDTH_A, D_MODEL)), const((WIDTH_B, D_MODEL)),
            const((D_MODEL, D_MODEL)),
            const((1, D_MODEL)), const((1, D_MODEL)),
            const((ROUTE_ROWS, 3 * D_MODEL)), const((ROUTE_ROWS, tm)), const((ROUTE_ROWS, tm)),
        ],
        out_specs=(rows(D_MODEL), rows(PACKED_WIDTH), pl.BlockSpec((INFO_ROWS, tm), lambda i: (0, i)),
                   const((ROUTE_ROWS, LANES))),
        scratch_shapes=[pltpu.VMEM((ROUTE_ROWS, tm), jnp.float32), pltpu.VMEM((tm, tm), jnp.bfloat16)],
        compiler_params=_cparams("arbitrary"),
        name="merge",
    )(x2, oa, ob, ln_g, ln_b, w_gates, w_pa, w_pb, w_o, l1g, l1b, w_r, b_r, cnt0)


def _row_copy(src_ref, src_row, dst_ref, dst_row, sem):
    return pltpu.make_async_copy(src_ref.at[pl.ds(src_row, 1)], dst_ref.at[pl.ds(dst_row, 1)], sem)


ROW_DMA_UNROLL = 8


def _tile_dest(dest_t, tt):
    n = dest_t.shape[1]
    return dest_t.reshape(TOP_K, n // tt, tt).transpose(1, 0, 2).reshape(n // tt, 1, TOP_K * tt)


def _dispatch_kernel(dest_ref, h_ref, xs_ref, sem):
    tt = h_ref.shape[0]

    def issue(i, c):
        base = pl.multiple_of(i * ROW_DMA_UNROLL, ROW_DMA_UNROLL)
        for j in range(ROW_DMA_UNROLL):
            for k in range(TOP_K):
                _row_copy(h_ref, base + j, xs_ref, dest_ref[0, 0, k * tt + base + j], sem).start()
        return c

    lax.fori_loop(0, tt // ROW_DMA_UNROLL, issue, 0)
    for k in range(TOP_K):
        pltpu.make_async_copy(h_ref, xs_ref.at[pl.ds(0, tt)], sem).wait()


def _dispatch(h1p, dest_t, tt):
    n, width = h1p.shape
    dest3 = _tile_dest(dest_t, tt)
    return pl.pallas_call(
        _dispatch_kernel,
        out_shape=jax.ShapeDtypeStruct((TOP_K * n, width), h1p.dtype),
        grid=(n // tt,),
        in_specs=[
            pl.BlockSpec((1, 1, TOP_K * tt), lambda i: (i, 0, 0), memory_space=pltpu.SMEM),
            pl.BlockSpec((tt, width), lambda i: (i, 0)),
        ],
        out_specs=pl.BlockSpec(memory_space=pl.ANY),
        scratch_shapes=[pltpu.SemaphoreType.DMA(())],
        compiler_params=_cparams("arbitrary"),
        name="dispatch",
    )(dest3, h1p)


def _moe_plan(counts, na, bm):
    ends = jnp.cumsum(counts)
    starts = ends - counts
    first_blk = starts // bm
    tiles = jnp.where(counts > 0, (ends - 1) // bm - first_blk + 1, 0)
    item_end = jnp.cumsum(tiles)
    item_start = item_end - tiles
    total = item_end[-1]
    wmax = na // bm + N_EXPERTS - 1
    w = jnp.arange(wmax, dtype=jnp.int32)
    wc = jnp.minimum(w, total - 1)
    e = jnp.sum((item_end[None, :] <= wc[:, None]).astype(jnp.int32), axis=1)
    e = jnp.minimum(e, N_EXPERTS - 1)
    blk = (first_blk[e] + (wc - item_start[e])).astype(jnp.int32)
    valid = w < total
    lo = jnp.where(valid, jnp.maximum(starts[e], blk * bm), 0).astype(jnp.int32)
    hi = jnp.where(valid, jnp.minimum(ends[e], (blk + 1) * bm), 0).astype(jnp.int32)
    prev_blk = jnp.concatenate([jnp.full((1,), -1, jnp.int32), blk[:-1]])
    prev_e = jnp.concatenate([jnp.full((1,), -1, jnp.int32), e[:-1]])
    flags = (valid.astype(jnp.int32)
             + 2 * (valid & (blk != prev_blk)).astype(jnp.int32)
             + 4 * (valid & (e != prev_e)).astype(jnp.int32))
    return blk, e, lo, hi, flags


FLAG_VALID, FLAG_NEW_BLOCK, FLAG_NEW_EXPERT = 1, 2, 4


def _expert_kernel(blk_ref, e_ref, lo_ref, hi_ref, flag_ref, x_ref, wg_ref, wu_ref, wd_ref, o_ref,
                   wg_b, wu_b, wd_b):
    w = pl.program_id(0)
    bm = x_ref.shape[0]
    flags = flag_ref[w]

    @pl.when((flags & FLAG_NEW_EXPERT) != 0)
    def _():
        wg_b[...] = wg_ref[0].astype(jnp.bfloat16)
        wu_b[...] = wu_ref[0].astype(jnp.bfloat16)
        wd_b[...] = wd_ref[0].astype(jnp.bfloat16)

    @pl.when((flags & FLAG_VALID) != 0)
    def _():
        x = _unpack_rows(x_ref[...]).astype(jnp.bfloat16)
        g = jnp.dot(x, wg_b[...], preferred_element_type=jnp.float32)
        u = jnp.dot(x, wu_b[...], preferred_element_type=jnp.float32)
        hmid = (jax.nn.silu(g) * u).astype(jnp.bfloat16)
        y = _pack_rows(jnp.dot(hmid, wd_b[...], preferred_element_type=jnp.float32))
        row = blk_ref[w] * bm + lax.broadcasted_iota(jnp.int32, (bm, 1), 0)
        mine = (row >= lo_ref[w]) & (row < hi_ref[w])

        @pl.when((flags & FLAG_NEW_BLOCK) != 0)
        def _():
            o_ref[...] = jnp.where(mine, y, jnp.uint32(0))

        @pl.when((flags & FLAG_NEW_BLOCK) == 0)
        def _():
            o_ref[...] = jnp.where(mine, y, o_ref[...])


def _experts(xs, plan, w_gate, w_up, w_down, bm):
    na = xs.shape[0]
    blk, e, lo, hi, flags = plan
    nitems = blk.shape[0]
    grid_spec = pltpu.PrefetchScalarGridSpec(
        num_scalar_prefetch=5,
        grid=(nitems,),
        in_specs=[
            pl.BlockSpec((bm, PACKED_WIDTH), lambda w, blk, e, *_: (blk[w], 0)),
            pl.BlockSpec((1, D_MODEL, D_EXPERT), lambda w, blk, e, *_: (e[w], 0, 0)),
            pl.BlockSpec((1, D_MODEL, D_EXPERT), lambda w, blk, e, *_: (e[w], 0, 0)),
            pl.BlockSpec((1, D_EXPERT, D_MODEL), lambda w, blk, e, *_: (e[w], 0, 0)),
        ],
        out_specs=pl.BlockSpec((bm, PACKED_WIDTH), lambda w, blk, e, *_: (blk[w], 0)),
        scratch_shapes=[pltpu.VMEM((D_MODEL, D_EXPERT), jnp.bfloat16),
                        pltpu.VMEM((D_MODEL, D_EXPERT), jnp.bfloat16),
                        pltpu.VMEM((D_EXPERT, D_MODEL), jnp.bfloat16)],
    )
    return pl.pallas_call(
        _expert_kernel,
        out_shape=jax.ShapeDtypeStruct((na, PACKED_WIDTH), jnp.uint32),
        grid_spec=grid_spec,
        compiler_params=_cparams("arbitrary"),
        name="experts",
    )(blk, e, lo, hi, flags, xs, w_gate, w_up, w_down)


def _combine_kernel(dest_ref, dest_next_ref, h1_ref, info_ref, g_ref, b_ref, ys_ref, o_ref, ybuf, sem):
    i = pl.program_id(0)
    nsteps = pl.num_programs(0)
    tt = h1_ref.shape[0]
    slot = i % 2

    def start_row(d_ref, s, t, k):
        _row_copy(ys_ref, d_ref[0, 0, k * tt + t], ybuf.at[s, k], t, sem.at[s]).start()

    def normalise():
        for k in range(TOP_K):
            pltpu.make_async_copy(ys_ref.at[pl.ds(0, tt)], ybuf.at[slot, k], sem.at[slot]).wait()
        pad = jnp.zeros((LANES - INFO_ROWS, tt), jnp.float32)
        info = jnp.concatenate([info_ref[...], pad], axis=0).T
        moe = (_unpack_rows(ybuf[slot, 0]) * info[:, INFO_W1:INFO_W1 + 1]
               + _unpack_rows(ybuf[slot, 1]) * info[:, INFO_W2:INFO_W2 + 1])
        o_ref[...] = _layer_norm(DEEPNORM_ALPHA * h1_ref[...] + moe, g_ref[...], b_ref[...])

    @pl.when(i == 0)
    def _():
        def issue(t, c):
            for k in range(TOP_K):
                start_row(dest_ref, 0, t, k)
            return c
        lax.fori_loop(0, tt, issue, 0, unroll=ROW_DMA_UNROLL)

    @pl.when(i + 1 < nsteps)
    def _():
        for t in range(tt):
            for k in range(TOP_K):
                start_row(dest_next_ref, 1 - slot, t, k)
        normalise()

    @pl.when(i + 1 >= nsteps)
    def _():
        normalise()


def _combine(h1, info, dest_t, ys, ln_g, ln_b, tt):
    n = h1.shape[0]
    nsteps = n // tt
    dest3 = _tile_dest(dest_t, tt)
    return pl.pallas_call(
        _combine_kernel,
        out_shape=jax.ShapeDtypeStruct((n, D_MODEL), jnp.float32),
        grid=(nsteps,),
        in_specs=[
            pl.BlockSpec((1, 1, TOP_K * tt), lambda i: (i, 0, 0), memory_space=pltpu.SMEM),
            pl.BlockSpec((1, 1, TOP_K * tt), lambda i: (jnp.minimum(i + 1, nsteps - 1), 0, 0),
                         memory_space=pltpu.SMEM),
            pl.BlockSpec((tt, D_MODEL), lambda i: (i, 0)),
            pl.BlockSpec((INFO_ROWS, tt), lambda i: (0, i)),
            pl.BlockSpec((1, D_MODEL), lambda i: (0, 0)),
            pl.BlockSpec((1, D_MODEL), lambda i: (0, 0)),
            pl.BlockSpec(memory_space=pl.ANY),
        ],
        out_specs=pl.BlockSpec((tt, D_MODEL), lambda i: (i, 0)),
        scratch_shapes=[pltpu.VMEM((2, TOP_K, tt, PACKED_WIDTH), jnp.uint32), pltpu.SemaphoreType.DMA((2,))],
        compiler_params=_cparams("arbitrary"),
        name="combine",
    )(dest3, dest3, h1, info, ln_g, ln_b, ys)


TM_QKV = 512
TQ_WIN = 512
TM_MERGE = 512
TT_ROWS = 256
BM_EXPERT = 256


def _prepare_weights(ln_in_g, ln_in_b, w_in, attn_sink, rel_pos_bias, w_proj_a, w_proj_b, w_out,
                     ln1_g, ln1_b, w_route_group, b_route_group, w_route_expert, b_route_expert,
                     ln2_g, ln2_b):
    bf = jnp.bfloat16
    w = w_in[0]
    splits = np.cumsum([WIDTH_A, KV_WIDTH_A, KV_WIDTH_A, WIDTH_B, WIDTH_B, WIDTH_B, D_MODEL])
    wqa, wka, wva, wqb, wkb, wvb, wga, wgb = jnp.split(w, [int(s) for s in splits], axis=1)
    wqa = (wqa.reshape(D_MODEL, N_KV_HEADS_A, GQA_GROUP, HEAD_DIM).transpose(0, 2, 1, 3)
           .reshape(D_MODEL, WIDTH_A))
    w_qkv = jnp.concatenate([wqa, wqb, wkb, wvb, wka, wva], axis=1).astype(bf)
    w_gates = jnp.concatenate([wga, wgb], axis=1).astype(bf)
    w_pa = (w_proj_a[0].reshape(N_KV_HEADS_A, GQA_GROUP, HEAD_DIM, D_MODEL).transpose(1, 0, 2, 3)
            .reshape(WIDTH_A, D_MODEL).astype(bf))
    w_pb = w_proj_b[0].astype(bf)
    w_o = w_out[0].astype(bf)
    pad = ROUTE_ROWS - N_GROUPS - N_EXPERTS
    w_r = jnp.concatenate([w_route_group[0].T, w_route_expert[0].T, jnp.zeros((pad, D_MODEL), jnp.float32)], axis=0)
    w_r_hi = w_r.astype(bf)
    w_r_lo = (w_r - w_r_hi.astype(jnp.float32)).astype(bf)
    w_r3 = jnp.concatenate([w_r_hi, w_r_hi, w_r_lo], axis=1)
    b_r = jnp.concatenate([b_route_group[0], b_route_expert[0], jnp.zeros((pad,), jnp.float32)])
    b_r = jnp.broadcast_to(b_r[:, None], (ROUTE_ROWS, TM_MERGE))
    row = lambda v: v.reshape(1, D_MODEL)
    return dict(
        ln_in_g=row(ln_in_g), ln_in_b=row(ln_in_b), w_qkv=w_qkv, w_gates=w_gates,
        sink=attn_sink[0].astype(jnp.float32), nat_bias=_nat_bias_table(rel_pos_bias[0]),
        w_pa=w_pa, w_pb=w_pb, w_o=w_o, ln1_g=row(ln1_g[0]), ln1_b=row(ln1_b[0]),
        w_r3=w_r3, b_r=b_r, ln2_g=row(ln2_g[0]), ln2_b=row(ln2_b[0]))


def _encode(x, p, w_gate, w_up, w_down):
    bsz, t, _ = x.shape
    n = bsz * t
    x2 = x.reshape(n, D_MODEL)
    qkv = _qkv(x2, p["ln_in_g"], p["ln_in_b"], p["w_qkv"], TM_QKV)
    oa = _win_attention(qkv, p["sink"], bsz, t, TQ_WIN)
    ob = _nat_attention(qkv, p["nat_bias"], bsz, t)
    cnt0 = jnp.zeros((ROUTE_ROWS, TM_MERGE), jnp.float32)
    h1, h1p, info, cnt = _merge(x2, oa, ob, p["ln_in_g"], p["ln_in_b"], p["w_gates"], p["w_pa"], p["w_pb"],
                                p["w_o"], p["ln1_g"], p["ln1_b"], p["w_r3"], p["b_r"], cnt0, TM_MERGE)
    counts = cnt[EXPERT_ROW0:EXPERT_ROW0 + N_EXPERTS, 0].astype(jnp.int32)
    starts = jnp.cumsum(counts) - counts
    eid = info[INFO_E1:INFO_E2 + 1].astype(jnp.int32)
    rank = info[INFO_R1:INFO_R2 + 1].astype(jnp.int32)
    expert = jnp.arange(N_EXPERTS, dtype=jnp.int32)[:, None, None]
    dest_t = rank + jnp.sum(jnp.where(eid[None] == expert, starts[:, None, None], 0), axis=0)
    xs = _dispatch(h1p, dest_t, TT_ROWS)
    plan = _moe_plan(counts, TOP_K * n, BM_EXPERT)
    ys = _experts(xs, plan, w_gate[0], w_up[0], w_down[0], BM_EXPERT)
    out = _combine(h1, info, dest_t, ys, p["ln2_g"], p["ln2_b"], TT_ROWS)
    return out.reshape(bsz, t, D_MODEL)


def kernel(x_prompt, x_sample, ln_in_g, ln_in_b, w_in, attn_sink, rel_pos_bias, w_proj_a, w_proj_b, w_out,
           ln1_g, ln1_b, w_route_group, b_route_group, w_route_expert, b_route_expert,
           w_gate, w_up, w_down, ln2_g, ln2_b):
    p = _prepare_weights(ln_in_g, ln_in_b, w_in, attn_sink, rel_pos_bias, w_proj_a, w_proj_b, w_out,
                         ln1_g, ln1_b, w_route_group, b_route_group, w_route_expert, b_route_expert,
                         ln2_g, ln2_b)
    return (_encode(x_prompt, p, w_gate, w_up, w_down), _encode(x_sample, p, w_gate, w_up, w_down))
```

```python
import functools

import numpy as np
import jax
import jax.numpy as jnp
from jax import lax
from jax.experimental import pallas as pl
from jax.experimental.pallas import tpu as pltpu
from jax.experimental.pallas import tpu_sc as plsc

D_MODEL = 1024
HEAD_DIM = 64
N_HEADS_A = 8
N_KV_HEADS_A = 2
WINDOW = 128
N_HEADS_B = 8
GRID_W = 64
NA_ROWS = 8
NA_COLS = 16
N_GROUPS = 4
EXPERTS_PER_GROUP = 8
N_EXPERTS = N_GROUPS * EXPERTS_PER_GROUP
TOP_K = 2
D_EXPERT = D_MODEL // 2
LN_EPS = 1e-5
DEPTH = 1
DEEPNORM_ALPHA = (2.0 * DEPTH) ** 0.25
WIDTH_A = N_HEADS_A * HEAD_DIM
KV_WIDTH_A = N_KV_HEADS_A * HEAD_DIM
WIDTH_B = N_HEADS_B * HEAD_DIM
QKV_WIDTH = WIDTH_A + 2 * KV_WIDTH_A + 3 * WIDTH_B

LANES = 128
VMEM_LIMIT_BYTES = 56 * 1024 * 1024

NEG_BIG = -1e30

QA_COL, QB_COL, KB_COL, VB_COL = 0, WIDTH_A, WIDTH_A + WIDTH_B, WIDTH_A + 2 * WIDTH_B
KA_COL = WIDTH_A + 3 * WIDTH_B
VA_COL = KA_COL + KV_WIDTH_A

GQA_GROUP = N_HEADS_A // N_KV_HEADS_A


def _cparams(*sem):
    return pltpu.CompilerParams(dimension_semantics=sem, vmem_limit_bytes=VMEM_LIMIT_BYTES)


def _layer_norm(x, g, b):
    mu = jnp.mean(x, axis=-1, keepdims=True)
    xc = x - mu
    var = jnp.mean(xc * xc, axis=-1, keepdims=True)
    return xc * lax.rsqrt(var + LN_EPS) * g + b


PACKED_WIDTH = D_MODEL // 2


def _pack_rows(x):
    def rne(v):
        return v + jnp.uint32(0x7FFF) + ((v >> 16) & jnp.uint32(1))
    hi = lax.bitcast_convert_type(x[:, :PACKED_WIDTH], jnp.uint32)
    lo = lax.bitcast_convert_type(x[:, PACKED_WIDTH:], jnp.uint32)
    return (rne(hi) & jnp.uint32(0xFFFF0000)) | (rne(lo) >> 16)


def _unpack_rows(w):
    hi = lax.bitcast_convert_type(w & jnp.uint32(0xFFFF0000), jnp.float32)
    lo = lax.bitcast_convert_type(w << 16, jnp.float32)
    return jnp.concatenate([hi, lo], axis=1)


def _qkv_kernel(x_ref, g_ref, b_ref, w_ref, o_ref):
    h = _layer_norm(x_ref[...], g_ref[...], b_ref[...])
    y = jnp.dot(h.astype(jnp.bfloat16), w_ref[...], preferred_element_type=jnp.float32)
    col = lax.broadcasted_iota(jnp.int32, (1, QKV_WIDTH), 1)
    y = y * jnp.where(col < KB_COL, HEAD_DIM ** -0.5, 1.0)
    o_ref[...] = y.astype(jnp.bfloat16)


def _qkv(x2, ln_g, ln_b, w_qkv, tm):
    n = x2.shape[0]
    return pl.pallas_call(
        _qkv_kernel,
        out_shape=jax.ShapeDtypeStruct((n, QKV_WIDTH), jnp.bfloat16),
        grid=(n // tm,),
        in_specs=[
            pl.BlockSpec((tm, D_MODEL), lambda i: (i, 0)),
            pl.BlockSpec((1, D_MODEL), lambda i: (0, 0)),
            pl.BlockSpec((1, D_MODEL), lambda i: (0, 0)),
            pl.BlockSpec((D_MODEL, QKV_WIDTH), lambda i: (0, 0)),
        ],
        out_specs=pl.BlockSpec((tm, QKV_WIDTH), lambda i: (i, 0)),
        compiler_params=_cparams("parallel"),
        name="qkv",
    )(x2, ln_g, ln_b, w_qkv)


WIN_BLK = 128
WIN_LOOKAHEAD = 2


def _win_bias_table():
    qi = np.arange(WIN_BLK)[:, None]
    kj = np.arange(3 * WIN_BLK)[None, :]
    dist = np.abs(kj - WIN_BLK - qi).astype(np.float64)
    slopes = 2.0 ** (-8.0 * np.arange(1, N_HEADS_A + 1) / N_HEADS_A)
    per_head = np.where(dist <= WINDOW, -slopes[:, None, None] * dist[None], NEG_BIG)
    groups = [np.concatenate([per_head[j], per_head[j + 4]], axis=0) for j in range(4)]
    return np.stack(groups).astype(np.float32)


def _win_kernel(sink_ref, q_ref, kp_ref, km_ref, kn_ref, vp_ref, vm_ref, vn_ref, bias_ref, o_ref,
                *, nsub, nblk_seq):
    i = pl.program_id(1)
    kcat = jnp.concatenate([kp_ref[...], km_ref[...], kn_ref[...]], axis=0)
    vcat = jnp.concatenate([vp_ref[...], vm_ref[...], vn_ref[...]], axis=0)
    lo = lax.broadcasted_iota(jnp.int32, (1, LANES), 1) < HEAD_DIM
    col = lax.broadcasted_iota(jnp.int32, (1, 3 * WIN_BLK), 1)
    top = lax.broadcasted_iota(jnp.int32, (2 * WIN_BLK, 1), 0) < WIN_BLK
    zero = jnp.zeros((), jnp.bfloat16)

    def scores(j, g):
        n = i * nsub + j
        off_seq = ((col < WIN_BLK) & (n == 0)) | ((col >= 2 * WIN_BLK) & (n == nblk_seq - 1))
        edge = jnp.where(off_seq, NEG_BIG, 0.0)
        qg = q_ref[WIN_BLK * j:WIN_BLK * (j + 1), LANES * g:LANES * (g + 1)]
        qm = jnp.concatenate([jnp.where(lo, qg, zero), jnp.where(lo, zero, qg)], axis=0)
        kj = kcat[WIN_BLK * j:WIN_BLK * (j + 3)]
        s = lax.dot_general(qm, kj, (((1,), (1,)), ((), ())), preferred_element_type=jnp.float32)
        return s + bias_ref[g] + edge

    def attend(s, j, g):
        vj = vcat[WIN_BLK * j:WIN_BLK * (j + 3)]
        sink = jnp.where(top, sink_ref[g], sink_ref[g + 4])
        m = jnp.maximum(jnp.max(s, axis=-1, keepdims=True), sink)
        p = jnp.exp(s - m)
        l = jnp.sum(p, axis=-1, keepdims=True) + jnp.exp(sink - m)
        o2 = jnp.dot(p.astype(jnp.bfloat16), vj, preferred_element_type=jnp.float32)
        o2 = o2 * (1.0 / l)
        o_ref[WIN_BLK * j:WIN_BLK * (j + 1), LANES * g:LANES * (g + 1)] = (
            jnp.where(lo, o2[:WIN_BLK], o2[WIN_BLK:]).astype(jnp.bfloat16))

    chains = [(j, g) for j in range(nsub) for g in range(4)]
    pending = [scores(*c) for c in chains[:WIN_LOOKAHEAD]]
    for idx, c in enumerate(chains):
        s = pending.pop(0)
        if idx + WIN_LOOKAHEAD < len(chains):
            pending.append(scores(*chains[idx + WIN_LOOKAHEAD]))
        attend(s, *c)


def _win_attention(qkv, sink, bsz, t, tq):
    n = bsz * t
    nsub = tq // WIN_BLK
    nblk_seq = t // WIN_BLK
    ntile = t // tq
    bias = jnp.asarray(_win_bias_table())

    def main_map(col):
        return lambda b, i, *_: (b * ntile + i, col)

    def prev_map(col):
        return lambda b, i, *_: (b * nblk_seq + jnp.maximum(i * nsub - 1, 0), col)

    def next_map(col):
        return lambda b, i, *_: (b * nblk_seq + jnp.minimum(i * nsub + nsub, nblk_seq - 1), col)

    halo = (WIN_BLK, LANES)
    ka, va = KA_COL // LANES, VA_COL // LANES
    grid_spec = pltpu.PrefetchScalarGridSpec(
        num_scalar_prefetch=1,
        grid=(bsz, ntile),
        in_specs=[
            pl.BlockSpec((tq, WIDTH_A), main_map(QA_COL // WIDTH_A)),
            pl.BlockSpec(halo, prev_map(ka)),
            pl.BlockSpec((tq, LANES), main_map(ka)),
            pl.BlockSpec(halo, next_map(ka)),
            pl.BlockSpec(halo, prev_map(va)),
            pl.BlockSpec((tq, LANES), main_map(va)),
            pl.BlockSpec(halo, next_map(va)),
            pl.BlockSpec((4, 2 * WIN_BLK, 3 * WIN_BLK), lambda b, i, *_: (0, 0, 0)),
        ],
        out_specs=pl.BlockSpec((tq, WIDTH_A), main_map(0)),
    )
    return pl.pallas_call(
        functools.partial(_win_kernel, nsub=nsub, nblk_seq=nblk_seq),
        out_shape=jax.ShapeDtypeStruct((n, WIDTH_A), jnp.bfloat16),
        grid_spec=grid_spec,
        compiler_params=_cparams("parallel", "parallel"),
        name="win",
    )(sink, qkv, qkv, qkv, qkv, qkv, qkv, qkv, bias)


NAT_ROWS_PER_STEP = 8
NAT_HALO_ROWS = NA_ROWS // 2
NAT_KEYS = NA_ROWS * GRID_W
NAT_ROWS_PER_TRIP = 8
NAT_LOOKAHEAD = 4


def _nat_bias_table(rpb):
    c = np.arange(GRID_W)
    cs = np.clip(c - NA_COLS // 2, 0, GRID_W - NA_COLS)
    col_mask = (c[None, :] >= cs[:, None]) & (c[None, :] < cs[:, None] + NA_COLS)
    dc = np.clip(c[None, :] - c[:, None] + (NA_COLS - 1), 0, 2 * NA_COLS - 2)
    onehot = jnp.asarray(dc[None] == np.arange(2 * NA_COLS - 1)[:, None, None], jnp.float32)
    picked = jnp.einsum("hdj,jqc->hdqc", rpb, onehot, precision=lax.Precision.HIGHEST)
    t1 = jnp.where(col_mask[None, None], picked, NEG_BIG)
    per_shift = []
    for sh in range(NA_ROWS):
        w = t1[:, sh:sh + NA_ROWS]
        w = jnp.transpose(w, (0, 2, 1, 3)).reshape(N_HEADS_B // 2, 2 * GRID_W, NAT_KEYS)
        per_shift.append(w)
    return jnp.stack(per_shift, axis=1).astype(jnp.float32)


def _nat_kernel(q_ref, kp_ref, km_ref, kn_ref, vp_ref, vm_ref, vn_ref, tb_ref, o_ref, kcat, vcat,
                *, rows_seq):
    i = pl.program_id(1)
    halo = NAT_HALO_ROWS * GRID_W
    main = NAT_ROWS_PER_STEP * GRID_W
    kcat[0:halo] = kp_ref[...]
    kcat[halo:halo + main] = km_ref[...]
    kcat[halo + main:2 * halo + main] = kn_ref[...]
    vcat[0:halo] = vp_ref[...]
    vcat[halo:halo + main] = vm_ref[...]
    vcat[halo + main:2 * halo + main] = vn_ref[...]
    lo = lax.broadcasted_iota(jnp.int32, (1, LANES), 1) < HEAD_DIM
    zero = jnp.zeros((), jnp.bfloat16)
    r0 = i * NAT_ROWS_PER_STEP

    def scores(qr, p):
        r = r0 + qr
        rs = jnp.clip(r - NA_ROWS // 2, 0, rows_seq - NA_ROWS)
        koff = pl.multiple_of((rs - r0 + NAT_HALO_ROWS) * GRID_W, GRID_W)
        sh = rs - r + (NA_ROWS - 1)
        qoff = pl.multiple_of(qr * GRID_W, GRID_W)
        cols = slice(LANES * p, LANES * (p + 1))
        qp = q_ref[pl.ds(qoff, GRID_W), cols]
        qm = jnp.concatenate([jnp.where(lo, qp, zero), jnp.where(lo, zero, qp)], axis=0)
        kw = kcat[pl.ds(koff, NAT_KEYS), cols]
        s = lax.dot_general(qm, kw, (((1,), (1,)), ((), ())), preferred_element_type=jnp.float32)
        return s + tb_ref[p, sh], koff, qoff

    def attend(s, koff, qoff, p):
        cols = slice(LANES * p, LANES * (p + 1))
        vw = vcat[pl.ds(koff, NAT_KEYS), cols]
        m = jnp.max(s, axis=-1, keepdims=True)
        pe = jnp.exp(s - m)
        l = jnp.sum(pe, axis=-1, keepdims=True)
        o2 = jnp.dot(pe.astype(jnp.bfloat16), vw, preferred_element_type=jnp.float32)
        o2 = o2 * (1.0 / l)
        o_ref[pl.ds(qoff, GRID_W), cols] = jnp.where(lo, o2[:GRID_W], o2[GRID_W:]).astype(jnp.bfloat16)

    def trip(j, carry):
        chains = [(j * NAT_ROWS_PER_TRIP + q, p) for q in range(NAT_ROWS_PER_TRIP) for p in range(N_HEADS_B // 2)]
        pending = [scores(*c) for c in chains[:NAT_LOOKAHEAD]]
        for idx, (_, p) in enumerate(chains):
            s, koff, qoff = pending.pop(0)
            if idx + NAT_LOOKAHEAD < len(chains):
                pending.append(scores(*chains[idx + NAT_LOOKAHEAD]))
            attend(s, koff, qoff, p)
        return carry

    lax.fori_loop(0, NAT_ROWS_PER_STEP // NAT_ROWS_PER_TRIP, trip, 0)


def _nat_attention(qkv, tb, bsz, t):
    n = bsz * t
    rows_seq = t // GRID_W
    main = NAT_ROWS_PER_STEP * GRID_W
    halo = NAT_HALO_ROWS * GRID_W
    ntile = t // main
    nhalo_seq = t // halo
    per = main // halo

    def main_map(col):
        return lambda b, i: (b * ntile + i, col)

    def prev_map(col):
        return lambda b, i: (b * nhalo_seq + jnp.maximum(i * per - 1, 0), col)

    def next_map(col):
        return lambda b, i: (b * nhalo_seq + jnp.minimum(i * per + per, nhalo_seq - 1), col)

    qb, kb, vb = QB_COL // WIDTH_B, KB_COL // WIDTH_B, VB_COL // WIDTH_B
    return pl.pallas_call(
        functools.partial(_nat_kernel, rows_seq=rows_seq),
        out_shape=jax.ShapeDtypeStruct((n, WIDTH_B), jnp.bfloat16),
        grid=(bsz, ntile),
        in_specs=[
            pl.BlockSpec((main, WIDTH_B), main_map(qb)),
            pl.BlockSpec((halo, WIDTH_B), prev_map(kb)),
            pl.BlockSpec((main, WIDTH_B), main_map(kb)),
            pl.BlockSpec((halo, WIDTH_B), next_map(kb)),
            pl.BlockSpec((halo, WIDTH_B), prev_map(vb)),
            pl.BlockSpec((main, WIDTH_B), main_map(vb)),
            pl.BlockSpec((halo, WIDTH_B), next_map(vb)),
            pl.BlockSpec((N_HEADS_B // 2, NA_ROWS, 2 * GRID_W, NAT_KEYS), lambda b, i: (0, 0, 0, 0)),
        ],
        out_specs=pl.BlockSpec((main, WIDTH_B), main_map(0)),
        scratch_shapes=[pltpu.VMEM((main + 2 * halo, WIDTH_B), jnp.bfloat16),
                        pltpu.VMEM((main + 2 * halo, WIDTH_B), jnp.bfloat16)],
        compiler_params=_cparams("parallel", "parallel"),
        name="nat",
    )(qkv, qkv, qkv, qkv, qkv, qkv, qkv, tb)


EXPERT_ROW0 = N_GROUPS
ROUTE_ROWS = 48
INFO_E1, INFO_E2, INFO_R1, INFO_R2, INFO_W1, INFO_W2 = range(6)
INFO_ROWS = 8
MERGE_SUBTILES = 2


def _route(lt, carry, tri):
    rr, tm = lt.shape
    row = lax.broadcasted_iota(jnp.int32, (rr, tm), 0).astype(jnp.float32)
    none = jnp.float32(rr)

    def first_max(sel):
        m = jnp.max(jnp.where(sel, lt, NEG_BIG), axis=0, keepdims=True)
        idx = jnp.min(jnp.where(sel & (lt == m), row, none), axis=0, keepdims=True)
        return m, idx

    is_group = row < N_GROUPS
    mg, g = first_max(is_group)
    pg_sel = 1.0 / jnp.sum(jnp.where(is_group, jnp.exp(jnp.where(is_group, lt, mg) - mg), 0.0),
                           axis=0, keepdims=True)
    row0 = EXPERT_ROW0 + EXPERTS_PER_GROUP * g
    in_group = (row >= row0) & (row < row0 + EXPERTS_PER_GROUP)
    m1, i1 = first_max(in_group)
    m2, i2 = first_max(in_group & (row != i1))
    e2 = jnp.exp(m2 - m1)
    w1 = pg_sel / (1.0 + e2)
    w2 = pg_sel * e2 / (1.0 + e2)

    oh1 = row == i1
    oh2 = row == i2
    both = (oh1 | oh2).astype(jnp.bfloat16)
    before = jnp.dot(both, tri, preferred_element_type=jnp.float32) + carry
    r1 = jnp.sum(jnp.where(oh1, before, 0.0), axis=0, keepdims=True)
    r2 = jnp.sum(jnp.where(oh2, before, 0.0), axis=0, keepdims=True)
    new_carry = carry + jnp.sum(both.astype(jnp.float32), axis=1, keepdims=True)

    field = lax.broadcasted_iota(jnp.int32, (INFO_ROWS, tm), 0)
    info = jnp.zeros((INFO_ROWS, tm), jnp.float32)
    for k, v in ((INFO_E1, i1 - EXPERT_ROW0), (INFO_E2, i2 - EXPERT_ROW0), (INFO_R1, r1), (INFO_R2, r2),
                 (INFO_W1, w1), (INFO_W2, w2)):
        info = jnp.where(field == k, v, info)
    return info, new_carry


def _merge_kernel(x_ref, oa_ref, ob_ref, lng_ref, lnb_ref, wg_ref, wpa_ref, wpb_ref, wo_ref,
                  l1g_ref, l1b_ref, wr_ref, br_ref, cnt0_ref,
                  h1_ref, h1p_ref, info_ref, cnt_ref, carry_ref, tri_ref):
    tm = x_ref.shape[0]

    @pl.when(pl.program_id(0) == 0)
    def _():
        carry_ref[...] = cnt0_ref[...]
        r = lax.broadcasted_iota(jnp.int32, (tm, tm), 0)
        c = lax.broadcasted_iota(jnp.int32, (tm, tm), 1)
        tri_ref[...] = (r < c).astype(jnp.bfloat16)

    def project(rows):
        h = _layer_norm(x_ref[rows], lng_ref[...], lnb_ref[...])
        gates = jnp.dot(h.astype(jnp.bfloat16), wg_ref[...], preferred_element_type=jnp.float32)
        pa = jnp.dot(oa_ref[rows], wpa_ref[...], preferred_element_type=jnp.float32)
        pb = jnp.dot(ob_ref[rows], wpb_ref[...], preferred_element_type=jnp.float32)
        return h, gates, pa, pb

    def mix(h, gates, pa, pb):
        mixin = jax.nn.sigmoid(gates[:, :D_MODEL]) * pa + jax.nn.sigmoid(gates[:, D_MODEL:]) * pb
        return DEEPNORM_ALPHA * h + jnp.dot(mixin.astype(jnp.bfloat16), wo_ref[...],
                                            preferred_element_type=jnp.float32)

    def norm_and_logits(pre, rows):
        h1 = _layer_norm(pre, l1g_ref[...], l1b_ref[...])
        h1_ref[rows] = h1
        h1p_ref[rows] = _pack_rows(h1)
        hi = h1.astype(jnp.bfloat16)
        lo = (h1 - hi.astype(jnp.float32)).astype(jnp.bfloat16)
        lhs = jnp.concatenate([hi, lo, hi], axis=1)
        return lax.dot_general(wr_ref[...], lhs, (((1,), (1,)), ((), ())), preferred_element_type=jnp.float32)

    sub = tm // MERGE_SUBTILES
    parts = [slice(k * sub, (k + 1) * sub) for k in range(MERGE_SUBTILES)]
    projected = [project(rows) for rows in parts]
    mixed = [mix(*pr) for pr in projected]
    logits_t = jnp.concatenate([norm_and_logits(pre, rows) for pre, rows in zip(mixed, parts)], axis=1)
    logits_t = logits_t + br_ref[...]
    info, carry = _route(logits_t, carry_ref[...], tri_ref[...])
    info_ref[...] = info
    carry_ref[...] = carry
    cnt_ref[...] = carry[:, :LANES]


def _merge(x2, oa, ob, ln_g, ln_b, w_gates, w_pa, w_pb, w_o, l1g, l1b, w_r, b_r, cnt0, tm):
    n = x2.shape[0]

    def const(shape):
        return pl.BlockSpec(shape, lambda i: (0,) * len(shape))

    def rows(width):
        return pl.BlockSpec((tm, width), lambda i: (i, 0))

    return pl.pallas_call(
        _merge_kernel,
        out_shape=(jax.ShapeDtypeStruct((n, D_MODEL), jnp.float32),
                   jax.ShapeDtypeStruct((n, PACKED_WIDTH), jnp.uint32),
                   jax.ShapeDtypeStruct((INFO_ROWS, n), jnp.float32),
                   jax.ShapeDtypeStruct((ROUTE_ROWS, LANES), jnp.float32)),
        grid=(n // tm,),
        in_specs=[
            rows(D_MODEL), rows(WIDTH_A), rows(WIDTH_B),
            const((1, D_MODEL)), const((1, D_MODEL)),
            const((D_MODEL, 2 * D_MODEL)),
            const((WIDTH_A, D_MODEL)), const((WIDTH_B, D_MODEL)),
            const((D_MODEL, D_MODEL)),
            const((1, D_MODEL)), const((1, D_MODEL)),
            const((ROUTE_ROWS, 3 * D_MODEL)), const((ROUTE_ROWS, tm)), const((ROUTE_ROWS, tm)),
        ],
        out_specs=(rows(D_MODEL), rows(PACKED_WIDTH), pl.BlockSpec((INFO_ROWS, tm), lambda i: (0, i)),
                   const((ROUTE_ROWS, LANES))),
        scratch_shapes=[pltpu.VMEM((ROUTE_ROWS, tm), jnp.float32), pltpu.VMEM((tm, tm), jnp.bfloat16)],
        compiler_params=_cparams("arbitrary"),
        name="merge",
    )(x2, oa, ob, ln_g, ln_b, w_gates, w_pa, w_pb, w_o, l1g, l1b, w_r, b_r, cnt0)


def _row_copy(src_ref, src_row, dst_ref, dst_row, sem):
    return pltpu.make_async_copy(src_ref.at[pl.ds(src_row, 1)], dst_ref.at[pl.ds(dst_row, 1)], sem)


ROW_DMA_UNROLL = 8


def _tile_dest(dest_t, tt):
    n = dest_t.shape[1]
    return dest_t.reshape(TOP_K, n // tt, tt).transpose(1, 0, 2).reshape(n // tt, 1, TOP_K * tt)


def _dispatch_kernel(dest_ref, h_ref, xs_ref, sem):
    tt = h_ref.shape[0]

    def issue(i, c):
        base = pl.multiple_of(i * ROW_DMA_UNROLL, ROW_DMA_UNROLL)
        for j in range(ROW_DMA_UNROLL):
            for k in range(TOP_K):
                _row_copy(h_ref, base + j, xs_ref, dest_ref[0, 0, k * tt + base + j], sem).start()
        return c

    lax.fori_loop(0, tt // ROW_DMA_UNROLL, issue, 0)
    for k in range(TOP_K):
        pltpu.make_async_copy(h_ref, xs_ref.at[pl.ds(0, tt)], sem).wait()


def _dispatch(h1p, dest_t, tt):
    n, width = h1p.shape
    dest3 = _tile_dest(dest_t, tt)
    return pl.pallas_call(
        _dispatch_kernel,
        out_shape=jax.ShapeDtypeStruct((TOP_K * n, width), h1p.dtype),
        grid=(n // tt,),
        in_specs=[
            pl.BlockSpec((1, 1, TOP_K * tt), lambda i: (i, 0, 0), memory_space=pltpu.SMEM),
            pl.BlockSpec((tt, width), lambda i: (i, 0)),
        ],
        out_specs=pl.BlockSpec(memory_space=pl.ANY),
        scratch_shapes=[pltpu.SemaphoreType.DMA(())],
        compiler_params=_cparams("arbitrary"),
        name="dispatch",
    )(dest3, h1p)


def _moe_plan(counts, na, bm):
    ends = jnp.cumsum(counts)
    starts = ends - counts
    first_blk = starts // bm
    tiles = jnp.where(counts > 0, (ends - 1) // bm - first_blk + 1, 0)
    item_end = jnp.cumsum(tiles)
    item_start = item_end - tiles
    total = item_end[-1]
    wmax = na // bm + N_EXPERTS - 1
    w = jnp.arange(wmax, dtype=jnp.int32)
    wc = jnp.minimum(w, total - 1)
    e = jnp.sum((item_end[None, :] <= wc[:, None]).astype(jnp.int32), axis=1)
    e = jnp.minimum(e, N_EXPERTS - 1)
    blk = (first_blk[e] + (wc - item_start[e])).astype(jnp.int32)
    valid = w < total
    lo = jnp.where(valid, jnp.maximum(starts[e], blk * bm), 0).astype(jnp.int32)
    hi = jnp.where(valid, jnp.minimum(ends[e], (blk + 1) * bm), 0).astype(jnp.int32)
    prev_blk = jnp.concatenate([jnp.full((1,), -1, jnp.int32), blk[:-1]])
    prev_e = jnp.concatenate([jnp.full((1,), -1, jnp.int32), e[:-1]])
    flags = (valid.astype(jnp.int32)
             + 2 * (valid & (blk != prev_blk)).astype(jnp.int32)
             + 4 * (valid & (e != prev_e)).astype(jnp.int32))
    return blk, e, lo, hi, flags


FLAG_VALID, FLAG_NEW_BLOCK, FLAG_NEW_EXPERT = 1, 2, 4


def _expert_kernel(blk_ref, e_ref, lo_ref, hi_ref, flag_ref, x_ref, wg_ref, wu_ref, wd_ref, o_ref,
                   wg_b, wu_b, wd_b):
    w = pl.program_id(0)
    bm = x_ref.shape[0]
    flags = flag_ref[w]

    @pl.when((flags & FLAG_NEW_EXPERT) != 0)
    def _():
        wg_b[...] = wg_ref[0].astype(jnp.bfloat16)
        wu_b[...] = wu_ref[0].astype(jnp.bfloat16)
        wd_b[...] = wd_ref[0].astype(jnp.bfloat16)

    @pl.when((flags & FLAG_VALID) != 0)
    def _():
        x = _unpack_rows(x_ref[...]).astype(jnp.bfloat16)
        g = jnp.dot(x, wg_b[...], preferred_element_type=jnp.float32)
        u = jnp.dot(x, wu_b[...], preferred_element_type=jnp.float32)
        hmid = (jax.nn.silu(g) * u).astype(jnp.bfloat16)
        y = _pack_rows(jnp.dot(hmid, wd_b[...], preferred_element_type=jnp.float32))
        row = blk_ref[w] * bm + lax.broadcasted_iota(jnp.int32, (bm, 1), 0)
        mine = (row >= lo_ref[w]) & (row < hi_ref[w])

        @pl.when((flags & FLAG_NEW_BLOCK) != 0)
        def _():
            o_ref[...] = jnp.where(mine, y, jnp.uint32(0))

        @pl.when((flags & FLAG_NEW_BLOCK) == 0)
        def _():
            o_ref[...] = jnp.where(mine, y, o_ref[...])


def _experts(xs, plan, w_gate, w_up, w_down, bm):
    na = xs.shape[0]
    blk, e, lo, hi, flags = plan
    nitems = blk.shape[0]
    grid_spec = pltpu.PrefetchScalarGridSpec(
        num_scalar_prefetch=5,
        grid=(nitems,),
        in_specs=[
            pl.BlockSpec((bm, PACKED_WIDTH), lambda w, blk, e, *_: (blk[w], 0)),
            pl.BlockSpec((1, D_MODEL, D_EXPERT), lambda w, blk, e, *_: (e[w], 0, 0)),
            pl.BlockSpec((1, D_MODEL, D_EXPERT), lambda w, blk, e, *_: (e[w], 0, 0)),
            pl.BlockSpec((1, D_EXPERT, D_MODEL), lambda w, blk, e, *_: (e[w], 0, 0)),
        ],
        out_specs=pl.BlockSpec((bm, PACKED_WIDTH), lambda w, blk, e, *_: (blk[w], 0)),
        scratch_shapes=[pltpu.VMEM((D_MODEL, D_EXPERT), jnp.bfloat16),
                        pltpu.VMEM((D_MODEL, D_EXPERT), jnp.bfloat16),
                        pltpu.VMEM((D_EXPERT, D_MODEL), jnp.bfloat16)],
    )
    return pl.pallas_call(
        _expert_kernel,
        out_shape=jax.ShapeDtypeStruct((na, PACKED_WIDTH), jnp.uint32),
        grid_spec=grid_spec,
        compiler_params=_cparams("arbitrary"),
        name="experts",
    )(blk, e, lo, hi, flags, xs, w_gate, w_up, w_down)


SC_CORES = 2
SC_SUBCORES = 16
SC_WORKERS = SC_CORES * SC_SUBCORES
SC_ROWS_PER_GATHER = 64


def _sc_gather_rows(table, idx):
    nrows = idx.shape[0]
    width = table.shape[1]
    per_worker = nrows // SC_WORKERS
    nchunks = per_worker // SC_ROWS_PER_GATHER
    assert nchunks * SC_ROWS_PER_GATHER * SC_WORKERS == nrows and nchunks % 2 == 0
    mesh = plsc.VectorSubcoreMesh(core_axis_name="c", subcore_axis_name="s")

    @functools.partial(
        pl.kernel, out_type=jax.ShapeDtypeStruct((nrows, width), table.dtype), mesh=mesh,
        scratch_types=[pltpu.VMEM((per_worker,), jnp.int32),
                       pltpu.VMEM((2, SC_ROWS_PER_GATHER, width), table.dtype),
                       pltpu.SemaphoreType.DMA((2,)), pltpu.SemaphoreType.DMA((2,))],
        name="sc_gather")
    def gather_kernel(table_hbm, idx_hbm, out_hbm, idx_v, rows_v, gsem, wsem):
        base = (lax.axis_index("s") * SC_CORES + lax.axis_index("c")) * per_worker
        pltpu.sync_copy(idx_hbm.at[pl.ds(base, per_worker)], idx_v)

        def gather(j, slot):
            rows = idx_v.at[pl.ds(j * SC_ROWS_PER_GATHER, SC_ROWS_PER_GATHER)]
            return pltpu.make_async_copy(table_hbm.at[rows], rows_v.at[slot], gsem.at[slot])

        def write(j, slot):
            dst = out_hbm.at[pl.ds(base + j * SC_ROWS_PER_GATHER, SC_ROWS_PER_GATHER)]
            return pltpu.make_async_copy(rows_v.at[slot], dst, wsem.at[slot])

        gather(0, 0).start()

        @pl.loop(0, nchunks, step=2)
        def _(j0):
            for slot in range(2):
                j = j0 + slot
                gather(j, slot).wait()

                @pl.when(j + 1 < nchunks)
                def _():
                    @pl.when(j >= 1)
                    def _():
                        write(j - 1, 1 - slot).wait()
                    gather(j + 1, 1 - slot).start()

                write(j, slot).start()

        write(nchunks - 2, 0).wait()
        write(nchunks - 1, 1).wait()

    return gather_kernel(table, idx)


def _finalize_kernel(h1_ref, y1_ref, y2_ref, info_ref, g_ref, b_ref, o_ref):
    tt = h1_ref.shape[0]
    pad = jnp.zeros((LANES - INFO_ROWS, tt), jnp.float32)
    info = jnp.concatenate([info_ref[...], pad], axis=0).T
    moe = (_unpack_rows(y1_ref[...]) * info[:, INFO_W1:INFO_W1 + 1]
           + _unpack_rows(y2_ref[...]) * info[:, INFO_W2:INFO_W2 + 1])
    o_ref[...] = _layer_norm(DEEPNORM_ALPHA * h1_ref[...] + moe, g_ref[...], b_ref[...])


def _combine(h1, info, dest_t, ys, ln_g, ln_b, tt):
    n = h1.shape[0]
    nsteps = n // tt
    yg = _sc_gather_rows(ys, dest_t.reshape(TOP_K * n))
    return pl.pallas_call(
        _finalize_kernel,
        out_shape=jax.ShapeDtypeStruct((n, D_MODEL), jnp.float32),
        grid=(nsteps,),
        in_specs=[
            pl.BlockSpec((tt, D_MODEL), lambda i: (i, 0)),
            pl.BlockSpec((tt, PACKED_WIDTH), lambda i: (i, 0)),
            pl.BlockSpec((tt, PACKED_WIDTH), lambda i: (nsteps + i, 0)),
            pl.BlockSpec((INFO_ROWS, tt), lambda i: (0, i)),
            pl.BlockSpec((1, D_MODEL), lambda i: (0, 0)),
            pl.BlockSpec((1, D_MODEL), lambda i: (0, 0)),
        ],
        out_specs=pl.BlockSpec((tt, D_MODEL), lambda i: (i, 0)),
        compiler_params=_cparams("parallel"),
        name="finalize",
    )(h1, yg, yg, info, ln_g, ln_b)


TM_QKV = 512
TQ_WIN = 512
TM_MERGE = 512
TT_ROWS = 256
BM_EXPERT = 256


def _prepare_weights(ln_in_g, ln_in_b, w_in, attn_sink, rel_pos_bias, w_proj_a, w_proj_b, w_out,
                     ln1_g, ln1_b, w_route_group, b_route_group, w_route_expert, b_route_expert,
                     ln2_g, ln2_b):
    bf = jnp.bfloat16
    w = w_in[0]
    splits = np.cumsum([WIDTH_A, KV_WIDTH_A, KV_WIDTH_A, WIDTH_B, WIDTH_B, WIDTH_B, D_MODEL])
    wqa, wka, wva, wqb, wkb, wvb, wga, wgb = jnp.split(w, [int(s) for s in splits], axis=1)
    wqa = (wqa.reshape(D_MODEL, N_KV_HEADS_A, GQA_GROUP, HEAD_DIM).transpose(0, 2, 1, 3)
           .reshape(D_MODEL, WIDTH_A))
    w_qkv = jnp.concatenate([wqa, wqb, wkb, wvb, wka, wva], axis=1).astype(bf)
    w_gates = jnp.concatenate([wga, wgb], axis=1).astype(bf)
    w_pa = (w_proj_a[0].reshape(N_KV_HEADS_A, GQA_GROUP, HEAD_DIM, D_MODEL).transpose(1, 0, 2, 3)
            .reshape(WIDTH_A, D_MODEL).astype(bf))
    w_pb = w_proj_b[0].astype(bf)
    w_o = w_out[0].astype(bf)
    pad = ROUTE_ROWS - N_GROUPS - N_EXPERTS
    w_r = jnp.concatenate([w_route_group[0].T, w_route_expert[0].T, jnp.zeros((pad, D_MODEL), jnp.float32)], axis=0)
    w_r_hi = w_r.astype(bf)
    w_r_lo = (w_r - w_r_hi.astype(jnp.float32)).astype(bf)
    w_r3 = jnp.concatenate([w_r_hi, w_r_hi, w_r_lo], axis=1)
    b_r = jnp.concatenate([b_route_group[0], b_route_expert[0], jnp.zeros((pad,), jnp.float32)])
    b_r = jnp.broadcast_to(b_r[:, None], (ROUTE_ROWS, TM_MERGE))
    row = lambda v: v.reshape(1, D_MODEL)
    return dict(
        ln_in_g=row(ln_in_g), ln_in_b=row(ln_in_b), w_qkv=w_qkv, w_gates=w_gates,
        sink=attn_sink[0].astype(jnp.float32), nat_bias=_nat_bias_table(rel_pos_bias[0]),
        w_pa=w_pa, w_pb=w_pb, w_o=w_o, ln1_g=row(ln1_g[0]), ln1_b=row(ln1_b[0]),
        w_r3=w_r3, b_r=b_r, ln2_g=row(ln2_g[0]), ln2_b=row(ln2_b[0]))


def _encode(x, p, w_gate, w_up, w_down):
    bsz, t, _ = x.shape
    n = bsz * t
    x2 = x.reshape(n, D_MODEL)
    qkv = _qkv(x2, p["ln_in_g"], p["ln_in_b"], p["w_qkv"], TM_QKV)
    oa = _win_attention(qkv, p["sink"], bsz, t, TQ_WIN)
    ob = _nat_attention(qkv, p["nat_bias"], bsz, t)
    cnt0 = jnp.zeros((ROUTE_ROWS, TM_MERGE), jnp.float32)
    h1, h1p, info, cnt = _merge(x2, oa, ob, p["ln_in_g"], p["ln_in_b"], p["w_gates"], p["w_pa"], p["w_pb"],
                                p["w_o"], p["ln1_g"], p["ln1_b"], p["w_r3"], p["b_r"], cnt0, TM_MERGE)
    counts = cnt[EXPERT_ROW0:EXPERT_ROW0 + N_EXPERTS, 0].astype(jnp.int32)
    starts = jnp.cumsum(counts) - counts
    eid = info[INFO_E1:INFO_E2 + 1].astype(jnp.int32)
    rank = info[INFO_R1:INFO_R2 + 1].astype(jnp.int32)
    expert = jnp.arange(N_EXPERTS, dtype=jnp.int32)[:, None, None]
    dest_t = rank + jnp.sum(jnp.where(eid[None] == expert, starts[:, None, None], 0), axis=0)
    xs = _dispatch(h1p, dest_t, TT_ROWS)
    plan = _moe_plan(counts, TOP_K * n, BM_EXPERT)
    ys = _experts(xs, plan, w_gate[0], w_up[0], w_down[0], BM_EXPERT)
    out = _combine(h1, info, dest_t, ys, p["ln2_g"], p["ln2_b"], TT_ROWS)
    return out.reshape(bsz, t, D_MODEL)


def kernel(x_prompt, x_sample, ln_in_g, ln_in_b, w_in, attn_sink, rel_pos_bias, w_proj_a, w_proj_b, w_out,
           ln1_g, ln1_b, w_route_group, b_route_group, w_route_expert, b_route_expert,
           w_gate, w_up, w_down, ln2_g, ln2_b):
    p = _prepare_weights(ln_in_g, ln_in_b, w_in, attn_sink, rel_pos_bias, w_proj_a, w_proj_b, w_out,
                         ln1_g, ln1_b, w_route_group, b_route_group, w_route_expert, b_route_expert,
                         ln2_g, ln2_b)
    return (_encode(x_prompt, p, w_gate, w_up, w_down), _encode(x_sample, p, w_gate, w_up, w_down))
```

```python
import functools

import numpy as np
import jax
import jax.numpy as jnp
from jax import lax
from jax.experimental import pallas as pl
from jax.experimental.pallas import tpu as pltpu
from jax.experimental.pallas import tpu_sc as plsc

D_MODEL = 1024
HEAD_DIM = 64
N_HEADS_A = 8
N_KV_HEADS_A = 2
WINDOW = 128
N_HEADS_B = 8
GRID_W = 64
NA_ROWS = 8
NA_COLS = 16
N_GROUPS = 4
EXPERTS_PER_GROUP = 8
N_EXPERTS = N_GROUPS * EXPERTS_PER_GROUP
TOP_K = 2
D_EXPERT = D_MODEL // 2
LN_EPS = 1e-5
DEPTH = 1
DEEPNORM_ALPHA = (2.0 * DEPTH) ** 0.25
WIDTH_A = N_HEADS_A * HEAD_DIM
KV_WIDTH_A = N_KV_HEADS_A * HEAD_DIM
WIDTH_B = N_HEADS_B * HEAD_DIM
QKV_WIDTH = WIDTH_A + 2 * KV_WIDTH_A + 3 * WIDTH_B

LANES = 128
VMEM_LIMIT_BYTES = 56 * 1024 * 1024

NEG_BIG = -1e30

QA_COL, QB_COL, KB_COL, VB_COL = 0, WIDTH_A, WIDTH_A + WIDTH_B, WIDTH_A + 2 * WIDTH_B
KA_COL = WIDTH_A + 3 * WIDTH_B
VA_COL = KA_COL + KV_WIDTH_A

GQA_GROUP = N_HEADS_A // N_KV_HEADS_A


def _cparams(*sem):
    return pltpu.CompilerParams(dimension_semantics=sem, vmem_limit_bytes=VMEM_LIMIT_BYTES)


def _layer_norm(x, g, b):
    mu = jnp.mean(x, axis=-1, keepdims=True)
    xc = x - mu
    var = jnp.mean(xc * xc, axis=-1, keepdims=True)
    return xc * lax.rsqrt(var + LN_EPS) * g + b


PACKED_WIDTH = D_MODEL // 2


def _pack_rows(x):
    def rne(v):
        return v + jnp.uint32(0x7FFF) + ((v >> 16) & jnp.uint32(1))
    hi = lax.bitcast_convert_type(x[:, :PACKED_WIDTH], jnp.uint32)
    lo = lax.bitcast_convert_type(x[:, PACKED_WIDTH:], jnp.uint32)
    return (rne(hi) & jnp.uint32(0xFFFF0000)) | (rne(lo) >> 16)


def _unpack_rows(w):
    hi = lax.bitcast_convert_type(w & jnp.uint32(0xFFFF0000), jnp.float32)
    lo = lax.bitcast_convert_type(w << 16, jnp.float32)
    return jnp.concatenate([hi, lo], axis=1)


def _qkv_kernel(x_ref, g_ref, b_ref, w_ref, o_ref):
    h = _layer_norm(x_ref[...], g_ref[...], b_ref[...])
    y = jnp.dot(h.astype(jnp.bfloat16), w_ref[...], preferred_element_type=jnp.float32)
    col = lax.broadcasted_iota(jnp.int32, (1, QKV_WIDTH), 1)
    y = y * jnp.where(col < KB_COL, HEAD_DIM ** -0.5, 1.0)
    o_ref[...] = y.astype(jnp.bfloat16)


def _qkv(x2, ln_g, ln_b, w_qkv, tm):
    n = x2.shape[0]
    return pl.pallas_call(
        _qkv_kernel,
        out_shape=jax.ShapeDtypeStruct((n, QKV_WIDTH), jnp.bfloat16),
        grid=(n // tm,),
        in_specs=[
            pl.BlockSpec((tm, D_MODEL), lambda i: (i, 0)),
            pl.BlockSpec((1, D_MODEL), lambda i: (0, 0)),
            pl.BlockSpec((1, D_MODEL), lambda i: (0, 0)),
            pl.BlockSpec((D_MODEL, QKV_WIDTH), lambda i: (0, 0)),
        ],
        out_specs=pl.BlockSpec((tm, QKV_WIDTH), lambda i: (i, 0)),
        compiler_params=_cparams("parallel"),
        name="qkv",
    )(x2, ln_g, ln_b, w_qkv)


WIN_BLK = 128
WIN_LOOKAHEAD = 2


def _win_bias_table():
    qi = np.arange(WIN_BLK)[:, None]
    kj = np.arange(3 * WIN_BLK)[None, :]
    dist = np.abs(kj - WIN_BLK - qi).astype(np.float64)
    slopes = 2.0 ** (-8.0 * np.arange(1, N_HEADS_A + 1) / N_HEADS_A)
    per_head = np.where(dist <= WINDOW, -slopes[:, None, None] * dist[None], NEG_BIG)
    groups = [np.concatenate([per_head[j], per_head[j + 4]], axis=0) for j in range(4)]
    return np.stack(groups).astype(np.float32)


def _win_kernel(sink_ref, q_ref, kp_ref, km_ref, kn_ref, vp_ref, vm_ref, vn_ref, bias_ref, o_ref,
                *, nsub, nblk_seq):
    i = pl.program_id(1)
    kcat = jnp.concatenate([kp_ref[...], km_ref[...], kn_ref[...]], axis=0)
    vcat = jnp.concatenate([vp_ref[...], vm_ref[...], vn_ref[...]], axis=0)
    lo = lax.broadcasted_iota(jnp.int32, (1, LANES), 1) < HEAD_DIM
    col = lax.broadcasted_iota(jnp.int32, (1, 3 * WIN_BLK), 1)
    top = lax.broadcasted_iota(jnp.int32, (2 * WIN_BLK, 1), 0) < WIN_BLK
    zero = jnp.zeros((), jnp.bfloat16)

    def scores(j, g):
        n = i * nsub + j
        off_seq = ((col < WIN_BLK) & (n == 0)) | ((col >= 2 * WIN_BLK) & (n == nblk_seq - 1))
        edge = jnp.where(off_seq, NEG_BIG, 0.0)
        qg = q_ref[WIN_BLK * j:WIN_BLK * (j + 1), LANES * g:LANES * (g + 1)]
        qm = jnp.concatenate([jnp.where(lo, qg, zero), jnp.where(lo, zero, qg)], axis=0)
        kj = kcat[WIN_BLK * j:WIN_BLK * (j + 3)]
        s = lax.dot_general(qm, kj, (((1,), (1,)), ((), ())), preferred_element_type=jnp.float32)
        return s + bias_ref[g] + edge

    def attend(s, j, g):
        vj = vcat[WIN_BLK * j:WIN_BLK * (j + 3)]
        sink = jnp.where(top, sink_ref[g], sink_ref[g + 4])
        m = jnp.maximum(jnp.max(s, axis=-1, keepdims=True), sink)
        p = jnp.exp(s - m)
        l = jnp.sum(p, axis=-1, keepdims=True) + jnp.exp(sink - m)
        o2 = jnp.dot(p.astype(jnp.bfloat16), vj, preferred_element_type=jnp.float32)
        o2 = o2 * (1.0 / l)
        o_ref[WIN_BLK * j:WIN_BLK * (j + 1), LANES * g:LANES * (g + 1)] = (
            jnp.where(lo, o2[:WIN_BLK], o2[WIN_BLK:]).astype(jnp.bfloat16))

    chains = [(j, g) for j in range(nsub) for g in range(4)]
    pending = [scores(*c) for c in chains[:WIN_LOOKAHEAD]]
    for idx, c in enumerate(chains):
        s = pending.pop(0)
        if idx + WIN_LOOKAHEAD < len(chains):
            pending.append(scores(*chains[idx + WIN_LOOKAHEAD]))
        attend(s, *c)


def _win_attention(qkv, sink, bsz, t, tq):
    n = bsz * t
    nsub = tq // WIN_BLK
    nblk_seq = t // WIN_BLK
    ntile = t // tq
    bias = jnp.asarray(_win_bias_table())

    def main_map(col):
        return lambda b, i, *_: (b * ntile + i, col)

    def prev_map(col):
        return lambda b, i, *_: (b * nblk_seq + jnp.maximum(i * nsub - 1, 0), col)

    def next_map(col):
        return lambda b, i, *_: (b * nblk_seq + jnp.minimum(i * nsub + nsub, nblk_seq - 1), col)

    halo = (WIN_BLK, LANES)
    ka, va = KA_COL // LANES, VA_COL // LANES
    grid_spec = pltpu.PrefetchScalarGridSpec(
        num_scalar_prefetch=1,
        grid=(bsz, ntile),
        in_specs=[
            pl.BlockSpec((tq, WIDTH_A), main_map(QA_COL // WIDTH_A)),
            pl.BlockSpec(halo, prev_map(ka)),
            pl.BlockSpec((tq, LANES), main_map(ka)),
            pl.BlockSpec(halo, next_map(ka)),
            pl.BlockSpec(halo, prev_map(va)),
            pl.BlockSpec((tq, LANES), main_map(va)),
            pl.BlockSpec(halo, next_map(va)),
            pl.BlockSpec((4, 2 * WIN_BLK, 3 * WIN_BLK), lambda b, i, *_: (0, 0, 0)),
        ],
        out_specs=pl.BlockSpec((tq, WIDTH_A), main_map(0)),
    )
    return pl.pallas_call(
        functools.partial(_win_kernel, nsub=nsub, nblk_seq=nblk_seq),
        out_shape=jax.ShapeDtypeStruct((n, WIDTH_A), jnp.bfloat16),
        grid_spec=grid_spec,
        compiler_params=_cparams("parallel", "parallel"),
        name="win",
    )(sink, qkv, qkv, qkv, qkv, qkv, qkv, qkv, bias)


NAT_ROWS_PER_STEP = 8
NAT_HALO_ROWS = NA_ROWS // 2
NAT_KEYS = NA_ROWS * GRID_W
NAT_ROWS_PER_TRIP = 8
NAT_LOOKAHEAD = 4


def _nat_bias_table(rpb):
    c = np.arange(GRID_W)
    cs = np.clip(c - NA_COLS // 2, 0, GRID_W - NA_COLS)
    col_mask = (c[None, :] >= cs[:, None]) & (c[None, :] < cs[:, None] + NA_COLS)
    dc = np.clip(c[None, :] - c[:, None] + (NA_COLS - 1), 0, 2 * NA_COLS - 2)
    onehot = jnp.asarray(dc[None] == np.arange(2 * NA_COLS - 1)[:, None, None], jnp.float32)
    picked = jnp.einsum("hdj,jqc->hdqc", rpb, onehot, precision=lax.Precision.HIGHEST)
    t1 = jnp.where(col_mask[None, None], picked, NEG_BIG)
    per_shift = []
    for sh in range(NA_ROWS):
        w = t1[:, sh:sh + NA_ROWS]
        w = jnp.transpose(w, (0, 2, 1, 3)).reshape(N_HEADS_B // 2, 2 * GRID_W, NAT_KEYS)
        per_shift.append(w)
    return jnp.stack(per_shift, axis=1).astype(jnp.float32)


def _nat_kernel(q_ref, kp_ref, km_ref, kn_ref, vp_ref, vm_ref, vn_ref, tb_ref, o_ref, kcat, vcat,
                *, rows_seq):
    i = pl.program_id(1)
    halo = NAT_HALO_ROWS * GRID_W
    main = NAT_ROWS_PER_STEP * GRID_W
    kcat[0:halo] = kp_ref[...]
    kcat[halo:halo + main] = km_ref[...]
    kcat[halo + main:2 * halo + main] = kn_ref[...]
    vcat[0:halo] = vp_ref[...]
    vcat[halo:halo + main] = vm_ref[...]
    vcat[halo + main:2 * halo + main] = vn_ref[...]
    lo = lax.broadcasted_iota(jnp.int32, (1, LANES), 1) < HEAD_DIM
    zero = jnp.zeros((), jnp.bfloat16)
    r0 = i * NAT_ROWS_PER_STEP

    def scores(qr, p):
        r = r0 + qr
        rs = jnp.clip(r - NA_ROWS // 2, 0, rows_seq - NA_ROWS)
        koff = pl.multiple_of((rs - r0 + NAT_HALO_ROWS) * GRID_W, GRID_W)
        sh = rs - r + (NA_ROWS - 1)
        qoff = pl.multiple_of(qr * GRID_W, GRID_W)
        cols = slice(LANES * p, LANES * (p + 1))
        qp = q_ref[pl.ds(qoff, GRID_W), cols]
        qm = jnp.concatenate([jnp.where(lo, qp, zero), jnp.where(lo, zero, qp)], axis=0)
        kw = kcat[pl.ds(koff, NAT_KEYS), cols]
        s = lax.dot_general(qm, kw, (((1,), (1,)), ((), ())), preferred_element_type=jnp.float32)
        return s + tb_ref[p, sh], koff, qoff

    def attend(s, koff, qoff, p):
        cols = slice(LANES * p, LANES * (p + 1))
        vw = vcat[pl.ds(koff, NAT_KEYS), cols]
        m = jnp.max(s, axis=-1, keepdims=True)
        pe = jnp.exp(s - m)
        l = jnp.sum(pe, axis=-1, keepdims=True)
        o2 = jnp.dot(pe.astype(jnp.bfloat16), vw, preferred_element_type=jnp.float32)
        o2 = o2 * (1.0 / l)
        o_ref[pl.ds(qoff, GRID_W), cols] = jnp.where(lo, o2[:GRID_W], o2[GRID_W:]).astype(jnp.bfloat16)

    def trip(j, carry):
        chains = [(j * NAT_ROWS_PER_TRIP + q, p) for q in range(NAT_ROWS_PER_TRIP) for p in range(N_HEADS_B // 2)]
        pending = [scores(*c) for c in chains[:NAT_LOOKAHEAD]]
        for idx, (_, p) in enumerate(chains):
            s, koff, qoff = pending.pop(0)
            if idx + NAT_LOOKAHEAD < len(chains):
                pending.append(scores(*chains[idx + NAT_LOOKAHEAD]))
            attend(s, koff, qoff, p)
        return carry

    lax.fori_loop(0, NAT_ROWS_PER_STEP // NAT_ROWS_PER_TRIP, trip, 0)


def _nat_attention(qkv, tb, bsz, t):
    n = bsz * t
    rows_seq = t // GRID_W
    main = NAT_ROWS_PER_STEP * GRID_W
    halo = NAT_HALO_ROWS * GRID_W
    ntile = t // main
    nhalo_seq = t // halo
    per = main // halo

    def main_map(col):
        return lambda b, i: (b * ntile + i, col)

    def prev_map(col):
        return lambda b, i: (b * nhalo_seq + jnp.maximum(i * per - 1, 0), col)

    def next_map(col):
        return lambda b, i: (b * nhalo_seq + jnp.minimum(i * per + per, nhalo_seq - 1), col)

    qb, kb, vb = QB_COL // WIDTH_B, KB_COL // WIDTH_B, VB_COL // WIDTH_B
    return pl.pallas_call(
        functools.partial(_nat_kernel, rows_seq=rows_seq),
        out_shape=jax.ShapeDtypeStruct((n, WIDTH_B), jnp.bfloat16),
        grid=(bsz, ntile),
        in_specs=[
            pl.BlockSpec((main, WIDTH_B), main_map(qb)),
            pl.BlockSpec((halo, WIDTH_B), prev_map(kb)),
            pl.BlockSpec((main, WIDTH_B), main_map(kb)),
            pl.BlockSpec((halo, WIDTH_B), next_map(kb)),
            pl.BlockSpec((halo, WIDTH_B), prev_map(vb)),
            pl.BlockSpec((main, WIDTH_B), main_map(vb)),
            pl.BlockSpec((halo, WIDTH_B), next_map(vb)),
            pl.BlockSpec((N_HEADS_B // 2, NA_ROWS, 2 * GRID_W, NAT_KEYS), lambda b, i: (0, 0, 0, 0)),
        ],
        out_specs=pl.BlockSpec((main, WIDTH_B), main_map(0)),
        scratch_shapes=[pltpu.VMEM((main + 2 * halo, WIDTH_B), jnp.bfloat16),
                        pltpu.VMEM((main + 2 * halo, WIDTH_B), jnp.bfloat16)],
        compiler_params=_cparams("parallel", "parallel"),
        name="nat",
    )(qkv, qkv, qkv, qkv, qkv, qkv, qkv, tb)


EXPERT_ROW0 = N_GROUPS
ROUTE_ROWS = 48
INFO_E1, INFO_E2, INFO_R1, INFO_R2, INFO_W1, INFO_W2 = range(6)
INFO_ROWS = 8
MERGE_SUBTILES = 2


def _route(lt, carry, tri):
    rr, tm = lt.shape
    row = lax.broadcasted_iota(jnp.int32, (rr, tm), 0).astype(jnp.float32)
    none = jnp.float32(rr)

    def first_max(sel):
        m = jnp.max(jnp.where(sel, lt, NEG_BIG), axis=0, keepdims=True)
        idx = jnp.min(jnp.where(sel & (lt == m), row, none), axis=0, keepdims=True)
        return m, idx

    is_group = row < N_GROUPS
    mg, g = first_max(is_group)
    pg_sel = 1.0 / jnp.sum(jnp.where(is_group, jnp.exp(jnp.where(is_group, lt, mg) - mg), 0.0),
                           axis=0, keepdims=True)
    row0 = EXPERT_ROW0 + EXPERTS_PER_GROUP * g
    in_group = (row >= row0) & (row < row0 + EXPERTS_PER_GROUP)
    m1, i1 = first_max(in_group)
    m2, i2 = first_max(in_group & (row != i1))
    e2 = jnp.exp(m2 - m1)
    w1 = pg_sel / (1.0 + e2)
    w2 = pg_sel * e2 / (1.0 + e2)

    oh1 = row == i1
    oh2 = row == i2
    both = (oh1 | oh2).astype(jnp.bfloat16)
    before = jnp.dot(both, tri, preferred_element_type=jnp.float32) + carry
    r1 = jnp.sum(jnp.where(oh1, before, 0.0), axis=0, keepdims=True)
    r2 = jnp.sum(jnp.where(oh2, before, 0.0), axis=0, keepdims=True)
    new_carry = carry + jnp.sum(both.astype(jnp.float32), axis=1, keepdims=True)

    field = lax.broadcasted_iota(jnp.int32, (INFO_ROWS, tm), 0)
    info = jnp.zeros((INFO_ROWS, tm), jnp.float32)
    for k, v in ((INFO_E1, i1 - EXPERT_ROW0), (INFO_E2, i2 - EXPERT_ROW0), (INFO_R1, r1), (INFO_R2, r2),
                 (INFO_W1, w1), (INFO_W2, w2)):
        info = jnp.where(field == k, v, info)
    return info, new_carry


def _merge_kernel(x_ref, oa_ref, ob_ref, lng_ref, lnb_ref, wg_ref, wpa_ref, wpb_ref, wo_ref,
                  l1g_ref, l1b_ref, wr_ref, br_ref, cnt0_ref,
                  h1_ref, h1p_ref, info_ref, cnt_ref, carry_ref, tri_ref):
    tm = x_ref.shape[0]

    @pl.when(pl.program_id(0) == 0)
    def _():
        carry_ref[...] = cnt0_ref[...]
        r = lax.broadcasted_iota(jnp.int32, (tm, tm), 0)
        c = lax.broadcasted_iota(jnp.int32, (tm, tm), 1)
        tri_ref[...] = (r < c).astype(jnp.bfloat16)

    def project(rows):
        h = _layer_norm(x_ref[rows], lng_ref[...], lnb_ref[...])
        gates = jnp.dot(h.astype(jnp.bfloat16), wg_ref[...], preferred_element_type=jnp.float32)
        pa = jnp.dot(oa_ref[rows], wpa_ref[...], preferred_element_type=jnp.float32)
        pb = jnp.dot(ob_ref[rows], wpb_ref[...], preferred_element_type=jnp.float32)
        return h, gates, pa, pb

    def mix(h, gates, pa, pb):
        mixin = jax.nn.sigmoid(gates[:, :D_MODEL]) * pa + jax.nn.sigmoid(gates[:, D_MODEL:]) * pb
        return DEEPNORM_ALPHA * h + jnp.dot(mixin.astype(jnp.bfloat16), wo_ref[...],
                                            preferred_element_type=jnp.float32)

    def norm_and_logits(pre, rows):
        h1 = _layer_norm(pre, l1g_ref[...], l1b_ref[...])
        h1_ref[rows] = h1
        h1p_ref[rows] = _pack_rows(h1)
        hi = h1.astype(jnp.bfloat16)
        lo = (h1 - hi.astype(jnp.float32)).astype(jnp.bfloat16)
        lhs = jnp.concatenate([hi, lo, hi], axis=1)
        return lax.dot_general(wr_ref[...], lhs, (((1,), (1,)), ((), ())), preferred_element_type=jnp.float32)

    sub = tm // MERGE_SUBTILES
    parts = [slice(k * sub, (k + 1) * sub) for k in range(MERGE_SUBTILES)]
    projected = [project(rows) for rows in parts]
    mixed = [mix(*pr) for pr in projected]
    logits_t = jnp.concatenate([norm_and_logits(pre, rows) for pre, rows in zip(mixed, parts)], axis=1)
    logits_t = logits_t + br_ref[...]
    info, carry = _route(logits_t, carry_ref[...], tri_ref[...])
    info_ref[...] = info
    carry_ref[...] = carry
    cnt_ref[...] = carry[:, :LANES]


def _merge(x2, oa, ob, ln_g, ln_b, w_gates, w_pa, w_pb, w_o, l1g, l1b, w_r, b_r, cnt0, tm):
    n = x2.shape[0]

    def const(shape):
        return pl.BlockSpec(shape, lambda i: (0,) * len(shape))

    def rows(width):
        return pl.BlockSpec((tm, width), lambda i: (i, 0))

    return pl.pallas_call(
        _merge_kernel,
        out_shape=(jax.ShapeDtypeStruct((n, D_MODEL), jnp.float32),
                   jax.ShapeDtypeStruct((n, PACKED_WIDTH), jnp.uint32),
                   jax.ShapeDtypeStruct((INFO_ROWS, n), jnp.float32),
                   jax.ShapeDtypeStruct((ROUTE_ROWS, LANES), jnp.float32)),
        grid=(n // tm,),
        in_specs=[
            rows(D_MODEL), rows(WIDTH_A), rows(WIDTH_B),
            const((1, D_MODEL)), const((1, D_MODEL)),
            const((D_MODEL, 2 * D_MODEL)),
            const((WIDTH_A, D_MODEL)), const((WIDTH_B, D_MODEL)),
            const((D_MODEL, D_MODEL)),
            const((1, D_MODEL)), const((1, D_MODEL)),
            const((ROUTE_ROWS, 3 * D_MODEL)), const((ROUTE_ROWS, tm)), const((ROUTE_ROWS, tm)),
        ],
        out_specs=(rows(D_MODEL), rows(PACKED_WIDTH), pl.BlockSpec((INFO_ROWS, tm), lambda i: (0, i)),
                   const((ROUTE_ROWS, LANES))),
        scratch_shapes=[pltpu.VMEM((ROUTE_ROWS, tm), jnp.float32), pltpu.VMEM((tm, tm), jnp.bfloat16)],
        compiler_params=_cparams("arbitrary"),
        name="merge",
    )(x2, oa, ob, ln_g, ln_b, w_gates, w_pa, w_pb, w_o, l1g, l1b, w_r, b_r, cnt0)


SC_CORES = 2
SC_SUBCORES = 16
SC_WORKERS = SC_CORES * SC_SUBCORES
SC_ROWS_PER_STREAM = 64


def _sc_worker():
    return lax.axis_index("s") * SC_CORES + lax.axis_index("c")


def _dispatch(h1p, dest_t):
    n, width = h1p.shape
    per_worker = n // SC_WORKERS
    nchunks = per_worker // SC_ROWS_PER_STREAM
    assert nchunks * SC_ROWS_PER_STREAM * SC_WORKERS == n and nchunks % 2 == 0
    idx = dest_t.reshape(TOP_K, SC_WORKERS, nchunks, SC_ROWS_PER_STREAM)
    mesh = plsc.VectorSubcoreMesh(core_axis_name="c", subcore_axis_name="s")

    @functools.partial(
        pl.kernel, out_type=jax.ShapeDtypeStruct((TOP_K * n, width), h1p.dtype), mesh=mesh,
        scratch_types=[pltpu.VMEM((TOP_K, nchunks, SC_ROWS_PER_STREAM), jnp.int32),
                       pltpu.VMEM((2, SC_ROWS_PER_STREAM, width), h1p.dtype),
                       pltpu.SemaphoreType.DMA((2,)), pltpu.SemaphoreType.DMA((2,))],
        name="sc_dispatch")
    def scatter_kernel(src_hbm, idx_hbm, out_hbm, idx_v, rows_v, rsem, ssem):
        wid = _sc_worker()
        base = wid * per_worker
        for k in range(TOP_K):
            pltpu.sync_copy(idx_hbm.at[k, wid], idx_v.at[k])

        def read(j, slot):
            src = src_hbm.at[pl.ds(base + j * SC_ROWS_PER_STREAM, SC_ROWS_PER_STREAM)]
            return pltpu.make_async_copy(src, rows_v.at[slot], rsem.at[slot])

        def scatter(j, slot, k):
            return pltpu.make_async_copy(rows_v.at[slot], out_hbm.at[idx_v.at[k, j]], ssem.at[slot])

        read(0, 0).start()

        @pl.loop(0, nchunks, step=2)
        def _(j0):
            for slot in range(2):
                j = j0 + slot
                read(j, slot).wait()

                @pl.when(j + 1 < nchunks)
                def _():
                    @pl.when(j >= 1)
                    def _():
                        for k in range(TOP_K):
                            scatter(j - 1, 1 - slot, k).wait()
                    read(j + 1, 1 - slot).start()

                for k in range(TOP_K):
                    scatter(j, slot, k).start()

        for k in range(TOP_K):
            scatter(nchunks - 2, 0, k).wait()
            scatter(nchunks - 1, 1, k).wait()

    return scatter_kernel(h1p, idx)


def _moe_plan(counts, na, bm):
    ends = jnp.cumsum(counts)
    starts = ends - counts
    first_blk = starts // bm
    tiles = jnp.where(counts > 0, (ends - 1) // bm - first_blk + 1, 0)
    item_end = jnp.cumsum(tiles)
    item_start = item_end - tiles
    total = item_end[-1]
    wmax = na // bm + N_EXPERTS - 1
    w = jnp.arange(wmax, dtype=jnp.int32)
    wc = jnp.minimum(w, total - 1)
    e = jnp.sum((item_end[None, :] <= wc[:, None]).astype(jnp.int32), axis=1)
    e = jnp.minimum(e, N_EXPERTS - 1)
    blk = (first_blk[e] + (wc - item_start[e])).astype(jnp.int32)
    valid = w < total
    lo = jnp.where(valid, jnp.maximum(starts[e], blk * bm), 0).astype(jnp.int32)
    hi = jnp.where(valid, jnp.minimum(ends[e], (blk + 1) * bm), 0).astype(jnp.int32)
    prev_blk = jnp.concatenate([jnp.full((1,), -1, jnp.int32), blk[:-1]])
    prev_e = jnp.concatenate([jnp.full((1,), -1, jnp.int32), e[:-1]])
    flags = (valid.astype(jnp.int32)
             + 2 * (valid & (blk != prev_blk)).astype(jnp.int32)
             + 4 * (valid & (e != prev_e)).astype(jnp.int32))
    return blk, e, lo, hi, flags


FLAG_VALID, FLAG_NEW_BLOCK, FLAG_NEW_EXPERT = 1, 2, 4


def _expert_kernel(blk_ref, e_ref, lo_ref, hi_ref, flag_ref, x_ref, wg_ref, wu_ref, wd_ref, o_ref,
                   wg_b, wu_b, wd_b):
    w = pl.program_id(0)
    bm = x_ref.shape[0]
    flags = flag_ref[w]

    @pl.when((flags & FLAG_NEW_EXPERT) != 0)
    def _():
        wg_b[...] = wg_ref[0].astype(jnp.bfloat16)
        wu_b[...] = wu_ref[0].astype(jnp.bfloat16)
        wd_b[...] = wd_ref[0].astype(jnp.bfloat16)

    @pl.when((flags & FLAG_VALID) != 0)
    def _():
        x = _unpack_rows(x_ref[...]).astype(jnp.bfloat16)
        g = jnp.dot(x, wg_b[...], preferred_element_type=jnp.float32)
        u = jnp.dot(x, wu_b[...], preferred_element_type=jnp.float32)
        hmid = (jax.nn.silu(g) * u).astype(jnp.bfloat16)
        y = _pack_rows(jnp.dot(hmid, wd_b[...], preferred_element_type=jnp.float32))
        row = blk_ref[w] * bm + lax.broadcasted_iota(jnp.int32, (bm, 1), 0)
        mine = (row >= lo_ref[w]) & (row < hi_ref[w])

        @pl.when((flags & FLAG_NEW_BLOCK) != 0)
        def _():
            o_ref[...] = jnp.where(mine, y, jnp.uint32(0))

        @pl.when((flags & FLAG_NEW_BLOCK) == 0)
        def _():
            o_ref[...] = jnp.where(mine, y, o_ref[...])


def _experts(xs, plan, w_gate, w_up, w_down, bm):
    na = xs.shape[0]
    blk, e, lo, hi, flags = plan
    nitems = blk.shape[0]
    grid_spec = pltpu.PrefetchScalarGridSpec(
        num_scalar_prefetch=5,
        grid=(nitems,),
        in_specs=[
            pl.BlockSpec((bm, PACKED_WIDTH), lambda w, blk, e, *_: (blk[w], 0)),
            pl.BlockSpec((1, D_MODEL, D_EXPERT), lambda w, blk, e, *_: (e[w], 0, 0)),
            pl.BlockSpec((1, D_MODEL, D_EXPERT), lambda w, blk, e, *_: (e[w], 0, 0)),
            pl.BlockSpec((1, D_EXPERT, D_MODEL), lambda w, blk, e, *_: (e[w], 0, 0)),
        ],
        out_specs=pl.BlockSpec((bm, PACKED_WIDTH), lambda w, blk, e, *_: (blk[w], 0)),
        scratch_shapes=[pltpu.VMEM((D_MODEL, D_EXPERT), jnp.bfloat16),
                        pltpu.VMEM((D_MODEL, D_EXPERT), jnp.bfloat16),
                        pltpu.VMEM((D_EXPERT, D_MODEL), jnp.bfloat16)],
    )
    return pl.pallas_call(
        _expert_kernel,
        out_shape=jax.ShapeDtypeStruct((na, PACKED_WIDTH), jnp.uint32),
        grid_spec=grid_spec,
        compiler_params=_cparams("arbitrary"),
        name="experts",
    )(blk, e, lo, hi, flags, xs, w_gate, w_up, w_down)


def _sc_gather_rows(table, idx):
    nrows = idx.shape[0]
    width = table.shape[1]
    per_worker = nrows // SC_WORKERS
    nchunks = per_worker // SC_ROWS_PER_STREAM
    assert nchunks * SC_ROWS_PER_STREAM * SC_WORKERS == nrows and nchunks % 2 == 0
    mesh = plsc.VectorSubcoreMesh(core_axis_name="c", subcore_axis_name="s")

    @functools.partial(
        pl.kernel, out_type=jax.ShapeDtypeStruct((nrows, width), table.dtype), mesh=mesh,
        scratch_types=[pltpu.VMEM((per_worker,), jnp.int32),
                       pltpu.VMEM((2, SC_ROWS_PER_STREAM, width), table.dtype),
                       pltpu.SemaphoreType.DMA((2,)), pltpu.SemaphoreType.DMA((2,))],
        name="sc_gather")
    def gather_kernel(table_hbm, idx_hbm, out_hbm, idx_v, rows_v, gsem, wsem):
        base = _sc_worker() * per_worker
        pltpu.sync_copy(idx_hbm.at[pl.ds(base, per_worker)], idx_v)

        def gather(j, slot):
            rows = idx_v.at[pl.ds(j * SC_ROWS_PER_STREAM, SC_ROWS_PER_STREAM)]
            return pltpu.make_async_copy(table_hbm.at[rows], rows_v.at[slot], gsem.at[slot])

        def write(j, slot):
            dst = out_hbm.at[pl.ds(base + j * SC_ROWS_PER_STREAM, SC_ROWS_PER_STREAM)]
            return pltpu.make_async_copy(rows_v.at[slot], dst, wsem.at[slot])

        gather(0, 0).start()

        @pl.loop(0, nchunks, step=2)
        def _(j0):
            for slot in range(2):
                j = j0 + slot
                gather(j, slot).wait()

                @pl.when(j + 1 < nchunks)
                def _():
                    @pl.when(j >= 1)
                    def _():
                        write(j - 1, 1 - slot).wait()
                    gather(j + 1, 1 - slot).start()

                write(j, slot).start()

        write(nchunks - 2, 0).wait()
        write(nchunks - 1, 1).wait()

    return gather_kernel(table, idx)


def _finalize_kernel(h1_ref, y1_ref, y2_ref, info_ref, g_ref, b_ref, o_ref):
    tt = h1_ref.shape[0]
    pad = jnp.zeros((LANES - INFO_ROWS, tt), jnp.float32)
    info = jnp.concatenate([info_ref[...], pad], axis=0).T
    moe = (_unpack_rows(y1_ref[...]) * info[:, INFO_W1:INFO_W1 + 1]
           + _unpack_rows(y2_ref[...]) * info[:, INFO_W2:INFO_W2 + 1])
    o_ref[...] = _layer_norm(DEEPNORM_ALPHA * h1_ref[...] + moe, g_ref[...], b_ref[...])


def _combine(h1, info, dest_t, ys, ln_g, ln_b, tt):
    n = h1.shape[0]
    nsteps = n // tt
    yg = _sc_gather_rows(ys, dest_t.reshape(TOP_K * n))
    return pl.pallas_call(
        _finalize_kernel,
        out_shape=jax.ShapeDtypeStruct((n, D_MODEL), jnp.float32),
        grid=(nsteps,),
        in_specs=[
            pl.BlockSpec((tt, D_MODEL), lambda i: (i, 0)),
            pl.BlockSpec((tt, PACKED_WIDTH), lambda i: (i, 0)),
            pl.BlockSpec((tt, PACKED_WIDTH), lambda i: (nsteps + i, 0)),
            pl.BlockSpec((INFO_ROWS, tt), lambda i: (0, i)),
            pl.BlockSpec((1, D_MODEL), lambda i: (0, 0)),
            pl.BlockSpec((1, D_MODEL), lambda i: (0, 0)),
        ],
        out_specs=pl.BlockSpec((tt, D_MODEL), lambda i: (i, 0)),
        compiler_params=_cparams("parallel"),
        name="finalize",
    )(h1, yg, yg, info, ln_g, ln_b)


TM_QKV = 512
TQ_WIN = 512
TM_MERGE = 512
TT_ROWS = 256
BM_EXPERT = 256


def _prepare_weights(ln_in_g, ln_in_b, w_in, attn_sink, rel_pos_bias, w_proj_a, w_proj_b, w_out,
                     ln1_g, ln1_b, w_route_group, b_route_group, w_route_expert, b_route_expert,
                     ln2_g, ln2_b):
    bf = jnp.bfloat16
    w = w_in[0]
    splits = np.cumsum([WIDTH_A, KV_WIDTH_A, KV_WIDTH_A, WIDTH_B, WIDTH_B, WIDTH_B, D_MODEL])
    wqa, wka, wva, wqb, wkb, wvb, wga, wgb = jnp.split(w, [int(s) for s in splits], axis=1)
    wqa = (wqa.reshape(D_MODEL, N_KV_HEADS_A, GQA_GROUP, HEAD_DIM).transpose(0, 2, 1, 3)
           .reshape(D_MODEL, WIDTH_A))
    w_qkv = jnp.concatenate([wqa, wqb, wkb, wvb, wka, wva], axis=1).astype(bf)
    w_gates = jnp.concatenate([wga, wgb], axis=1).astype(bf)
    w_pa = (w_proj_a[0].reshape(N_KV_HEADS_A, GQA_GROUP, HEAD_DIM, D_MODEL).transpose(1, 0, 2, 3)
            .reshape(WIDTH_A, D_MODEL).astype(bf))
    w_pb = w_proj_b[0].astype(bf)
    w_o = w_out[0].astype(bf)
    pad = ROUTE_ROWS - N_GROUPS - N_EXPERTS
    w_r = jnp.concatenate([w_route_group[0].T, w_route_expert[0].T, jnp.zeros((pad, D_MODEL), jnp.float32)], axis=0)
    w_r_hi = w_r.astype(bf)
    w_r_lo = (w_r - w_r_hi.astype(jnp.float32)).astype(bf)
    w_r3 = jnp.concatenate([w_r_hi, w_r_hi, w_r_lo], axis=1)
    b_r = jnp.concatenate([b_route_group[0], b_route_expert[0], jnp.zeros((pad,), jnp.float32)])
    b_r = jnp.broadcast_to(b_r[:, None], (ROUTE_ROWS, TM_MERGE))
    row = lambda v: v.reshape(1, D_MODEL)
    return dict(
        ln_in_g=row(ln_in_g), ln_in_b=row(ln_in_b), w_qkv=w_qkv, w_gates=w_gates,
        sink=attn_sink[0].astype(jnp.float32), nat_bias=_nat_bias_table(rel_pos_bias[0]),
        w_pa=w_pa, w_pb=w_pb, w_o=w_o, ln1_g=row(ln1_g[0]), ln1_b=row(ln1_b[0]),
        w_r3=w_r3, b_r=b_r, ln2_g=row(ln2_g[0]), ln2_b=row(ln2_b[0]))


def _encode(x, p, w_gate, w_up, w_down):
    bsz, t, _ = x.shape
    n = bsz * t
    x2 = x.reshape(n, D_MODEL)
    qkv = _qkv(x2, p["ln_in_g"], p["ln_in_b"], p["w_qkv"], TM_QKV)
    oa = _win_attention(qkv, p["sink"], bsz, t, TQ_WIN)
    ob = _nat_attention(qkv, p["nat_bias"], bsz, t)
    cnt0 = jnp.zeros((ROUTE_ROWS, TM_MERGE), jnp.float32)
    h1, h1p, info, cnt = _merge(x2, oa, ob, p["ln_in_g"], p["ln_in_b"], p["w_gates"], p["w_pa"], p["w_pb"],
                                p["w_o"], p["ln1_g"], p["ln1_b"], p["w_r3"], p["b_r"], cnt0, TM_MERGE)
    counts = cnt[EXPERT_ROW0:EXPERT_ROW0 + N_EXPERTS, 0].astype(jnp.int32)
    starts = jnp.cumsum(counts) - counts
    eid = info[INFO_E1:INFO_E2 + 1].astype(jnp.int32)
    rank = info[INFO_R1:INFO_R2 + 1].astype(jnp.int32)
    expert = jnp.arange(N_EXPERTS, dtype=jnp.int32)[:, None, None]
    dest_t = rank + jnp.sum(jnp.where(eid[None] == expert, starts[:, None, None], 0), axis=0)
    xs = _dispatch(h1p, dest_t)
    plan = _moe_plan(counts, TOP_K * n, BM_EXPERT)
    ys = _experts(xs, plan, w_gate[0], w_up[0], w_down[0], BM_EXPERT)
    out = _combine(h1, info, dest_t, ys, p["ln2_g"], p["ln2_b"], TT_ROWS)
    return out.reshape(bsz, t, D_MODEL)


def kernel(x_prompt, x_sample, ln_in_g, ln_in_b, w_in, attn_sink, rel_pos_bias, w_proj_a, w_proj_b, w_out,
           ln1_g, ln1_b, w_route_group, b_route_group, w_route_expert, b_route_expert,
           w_gate, w_up, w_down, ln2_g, ln2_b):
    p = _prepare_weights(ln_in_g, ln_in_b, w_in, attn_sink, rel_pos_bias, w_proj_a, w_proj_b, w_out,
                         ln1_g, ln1_b, w_route_group, b_route_group, w_route_expert, b_route_expert,
                         ln2_g, ln2_b)
    return (_encode(x_prompt, p, w_gate, w_up, w_down), _encode(x_sample, p, w_gate, w_up, w_down))
```

```python
import functools

import numpy as np
import jax
import jax.numpy as jnp
from jax import lax
from jax.experimental import pallas as pl
from jax.experimental.pallas import tpu as pltpu
from jax.experimental.pallas import tpu_sc as plsc

D_MODEL = 1024
HEAD_DIM = 64
N_HEADS_A = 8
N_KV_HEADS_A = 2
WINDOW = 128
N_HEADS_B = 8
GRID_W = 64
NA_ROWS = 8
NA_COLS = 16
N_GROUPS = 4
EXPERTS_PER_GROUP = 8
N_EXPERTS = N_GROUPS * EXPERTS_PER_GROUP
TOP_K = 2
D_EXPERT = D_MODEL // 2
LN_EPS = 1e-5
DEPTH = 1
DEEPNORM_ALPHA = (2.0 * DEPTH) ** 0.25
WIDTH_A = N_HEADS_A * HEAD_DIM
KV_WIDTH_A = N_KV_HEADS_A * HEAD_DIM
WIDTH_B = N_HEADS_B * HEAD_DIM
QKV_WIDTH = WIDTH_A + 2 * KV_WIDTH_A + 3 * WIDTH_B

LANES = 128
VMEM_LIMIT_BYTES = 56 * 1024 * 1024

NEG_BIG = -1e30

QA_COL, QB_COL, KB_COL, VB_COL = 0, WIDTH_A, WIDTH_A + WIDTH_B, WIDTH_A + 2 * WIDTH_B
KA_COL = WIDTH_A + 3 * WIDTH_B
VA_COL = KA_COL + KV_WIDTH_A

GQA_GROUP = N_HEADS_A // N_KV_HEADS_A


def _cparams(*sem):
    return pltpu.CompilerParams(dimension_semantics=sem, vmem_limit_bytes=VMEM_LIMIT_BYTES)


def _layer_norm(x, g, b):
    mu = jnp.mean(x, axis=-1, keepdims=True)
    xc = x - mu
    var = jnp.mean(xc * xc, axis=-1, keepdims=True)
    return xc * lax.rsqrt(var + LN_EPS) * g + b


PACKED_WIDTH = D_MODEL // 2


def _pack_rows(x):
    def rne(v):
        return v + jnp.uint32(0x7FFF) + ((v >> 16) & jnp.uint32(1))
    hi = lax.bitcast_convert_type(x[:, :PACKED_WIDTH], jnp.uint32)
    lo = lax.bitcast_convert_type(x[:, PACKED_WIDTH:], jnp.uint32)
    return (rne(hi) & jnp.uint32(0xFFFF0000)) | (rne(lo) >> 16)


def _unpack_rows(w):
    hi = lax.bitcast_convert_type(w & jnp.uint32(0xFFFF0000), jnp.float32)
    lo = lax.bitcast_convert_type(w << 16, jnp.float32)
    return jnp.concatenate([hi, lo], axis=1)


def _qkv_kernel(x_ref, g_ref, b_ref, w_ref, o_ref):
    h = _layer_norm(x_ref[...], g_ref[...], b_ref[...])
    y = jnp.dot(h.astype(jnp.bfloat16), w_ref[...], preferred_element_type=jnp.float32)
    col = lax.broadcasted_iota(jnp.int32, (1, QKV_WIDTH), 1)
    y = y * jnp.where(col < KB_COL, HEAD_DIM ** -0.5, 1.0)
    o_ref[...] = y.astype(jnp.bfloat16)


def _qkv(x2, ln_g, ln_b, w_qkv, tm):
    n = x2.shape[0]
    return pl.pallas_call(
        _qkv_kernel,
        out_shape=jax.ShapeDtypeStruct((n, QKV_WIDTH), jnp.bfloat16),
        grid=(n // tm,),
        in_specs=[
            pl.BlockSpec((tm, D_MODEL), lambda i: (i, 0)),
            pl.BlockSpec((1, D_MODEL), lambda i: (0, 0)),
            pl.BlockSpec((1, D_MODEL), lambda i: (0, 0)),
            pl.BlockSpec((D_MODEL, QKV_WIDTH), lambda i: (0, 0)),
        ],
        out_specs=pl.BlockSpec((tm, QKV_WIDTH), lambda i: (i, 0)),
        compiler_params=_cparams("parallel"),
        name="qkv",
    )(x2, ln_g, ln_b, w_qkv)


WIN_BLK = 128
WIN_LOOKAHEAD = 2


def _win_bias_table():
    qi = np.arange(WIN_BLK)[:, None]
    kj = np.arange(3 * WIN_BLK)[None, :]
    dist = np.abs(kj - WIN_BLK - qi).astype(np.float64)
    slopes = 2.0 ** (-8.0 * np.arange(1, N_HEADS_A + 1) / N_HEADS_A)
    per_head = np.where(dist <= WINDOW, -slopes[:, None, None] * dist[None], NEG_BIG)
    groups = [np.concatenate([per_head[j], per_head[j + 4]], axis=0) for j in range(4)]
    return np.stack(groups).astype(np.float32)


def _win_kernel(sink_ref, q_ref, kp_ref, km_ref, kn_ref, vp_ref, vm_ref, vn_ref, bias_ref, o_ref,
                *, nsub, nblk_seq):
    i = pl.program_id(1)
    kcat = jnp.concatenate([kp_ref[...], km_ref[...], kn_ref[...]], axis=0)
    vcat = jnp.concatenate([vp_ref[...], vm_ref[...], vn_ref[...]], axis=0)
    lo = lax.broadcasted_iota(jnp.int32, (1, LANES), 1) < HEAD_DIM
    col = lax.broadcasted_iota(jnp.int32, (1, 3 * WIN_BLK), 1)
    top = lax.broadcasted_iota(jnp.int32, (2 * WIN_BLK, 1), 0) < WIN_BLK
    zero = jnp.zeros((), jnp.bfloat16)

    def scores(j, g):
        n = i * nsub + j
        off_seq = ((col < WIN_BLK) & (n == 0)) | ((col >= 2 * WIN_BLK) & (n == nblk_seq - 1))
        edge = jnp.where(off_seq, NEG_BIG, 0.0)
        qg = q_ref[WIN_BLK * j:WIN_BLK * (j + 1), LANES * g:LANES * (g + 1)]
        qm = jnp.concatenate([jnp.where(lo, qg, zero), jnp.where(lo, zero, qg)], axis=0)
        kj = kcat[WIN_BLK * j:WIN_BLK * (j + 3)]
        s = lax.dot_general(qm, kj, (((1,), (1,)), ((), ())), preferred_element_type=jnp.float32)
        return s + bias_ref[g] + edge

    def attend(s, j, g):
        vj = vcat[WIN_BLK * j:WIN_BLK * (j + 3)]
        sink = jnp.where(top, sink_ref[g], sink_ref[g + 4])
        m = jnp.maximum(jnp.max(s, axis=-1, keepdims=True), sink)
        p = jnp.exp(s - m)
        l = jnp.sum(p, axis=-1, keepdims=True) + jnp.exp(sink - m)
        o2 = jnp.dot(p.astype(jnp.bfloat16), vj, preferred_element_type=jnp.float32)
        o2 = o2 * (1.0 / l)
        o_ref[WIN_BLK * j:WIN_BLK * (j + 1), LANES * g:LANES * (g + 1)] = (
            jnp.where(lo, o2[:WIN_BLK], o2[WIN_BLK:]).astype(jnp.bfloat16))

    chains = [(j, g) for j in range(nsub) for g in range(4)]
    pending = [scores(*c) for c in chains[:WIN_LOOKAHEAD]]
    for idx, c in enumerate(chains):
        s = pending.pop(0)
        if idx + WIN_LOOKAHEAD < len(chains):
            pending.append(scores(*chains[idx + WIN_LOOKAHEAD]))
        attend(s, *c)


def _win_attention(qkv, sink, bsz, t, tq):
    n = bsz * t
    nsub = tq // WIN_BLK
    nblk_seq = t // WIN_BLK
    ntile = t // tq
    bias = jnp.asarray(_win_bias_table())

    def main_map(col):
        return lambda b, i, *_: (b * ntile + i, col)

    def prev_map(col):
        return lambda b, i, *_: (b * nblk_seq + jnp.maximum(i * nsub - 1, 0), col)

    def next_map(col):
        return lambda b, i, *_: (b * nblk_seq + jnp.minimum(i * nsub + nsub, nblk_seq - 1), col)

    halo = (WIN_BLK, LANES)
    ka, va = KA_COL // LANES, VA_COL // LANES
    grid_spec = pltpu.PrefetchScalarGridSpec(
        num_scalar_prefetch=1,
        grid=(bsz, ntile),
        in_specs=[
            pl.BlockSpec((tq, WIDTH_A), main_map(QA_COL // WIDTH_A)),
            pl.BlockSpec(halo, prev_map(ka)),
            pl.BlockSpec((tq, LANES), main_map(ka)),
            pl.BlockSpec(halo, next_map(ka)),
            pl.BlockSpec(halo, prev_map(va)),
            pl.BlockSpec((tq, LANES), main_map(va)),
            pl.BlockSpec(halo, next_map(va)),
            pl.BlockSpec((4, 2 * WIN_BLK, 3 * WIN_BLK), lambda b, i, *_: (0, 0, 0)),
        ],
        out_specs=pl.BlockSpec((tq, WIDTH_A), main_map(0)),
    )
    return pl.pallas_call(
        functools.partial(_win_kernel, nsub=nsub, nblk_seq=nblk_seq),
        out_shape=jax.ShapeDtypeStruct((n, WIDTH_A), jnp.bfloat16),
        grid_spec=grid_spec,
        compiler_params=_cparams("parallel", "parallel"),
        name="win",
    )(sink, qkv, qkv, qkv, qkv, qkv, qkv, qkv, bias)


NAT_ROWS_PER_STEP = 8
NAT_HALO_ROWS = NA_ROWS // 2
NAT_KEYS = NA_ROWS * GRID_W
NAT_ROWS_PER_TRIP = 8
NAT_LOOKAHEAD = 4


def _nat_bias_table(rpb):
    c = np.arange(GRID_W)
    cs = np.clip(c - NA_COLS // 2, 0, GRID_W - NA_COLS)
    col_mask = (c[None, :] >= cs[:, None]) & (c[None, :] < cs[:, None] + NA_COLS)
    dc = np.clip(c[None, :] - c[:, None] + (NA_COLS - 1), 0, 2 * NA_COLS - 2)
    onehot = jnp.asarray(dc[None] == np.arange(2 * NA_COLS - 1)[:, None, None], jnp.float32)
    picked = jnp.einsum("hdj,jqc->hdqc", rpb, onehot, precision=lax.Precision.HIGHEST)
    t1 = jnp.where(col_mask[None, None], picked, NEG_BIG)
    per_shift = []
    for sh in range(NA_ROWS):
        w = t1[:, sh:sh + NA_ROWS]
        w = jnp.transpose(w, (0, 2, 1, 3)).reshape(N_HEADS_B // 2, 2 * GRID_W, NAT_KEYS)
        per_shift.append(w)
    return jnp.stack(per_shift, axis=1).astype(jnp.float32)


def _nat_kernel(q_ref, kp_ref, km_ref, kn_ref, vp_ref, vm_ref, vn_ref, tb_ref, o_ref, kcat, vcat,
                *, rows_seq):
    i = pl.program_id(1)
    halo = NAT_HALO_ROWS * GRID_W
    main = NAT_ROWS_PER_STEP * GRID_W
    kcat[0:halo] = kp_ref[...]
    kcat[halo:halo + main] = km_ref[...]
    kcat[halo + main:2 * halo + main] = kn_ref[...]
    vcat[0:halo] = vp_ref[...]
    vcat[halo:halo + main] = vm_ref[...]
    vcat[halo + main:2 * halo + main] = vn_ref[...]
    lo = lax.broadcasted_iota(jnp.int32, (1, LANES), 1) < HEAD_DIM
    zero = jnp.zeros((), jnp.bfloat16)
    r0 = i * NAT_ROWS_PER_STEP

    def scores(qr, p):
        r = r0 + qr
        rs = jnp.clip(r - NA_ROWS // 2, 0, rows_seq - NA_ROWS)
        koff = pl.multiple_of((rs - r0 + NAT_HALO_ROWS) * GRID_W, GRID_W)
        sh = rs - r + (NA_ROWS - 1)
        qoff = pl.multiple_of(qr * GRID_W, GRID_W)
        cols = slice(LANES * p, LANES * (p + 1))
        qp = q_ref[pl.ds(qoff, GRID_W), cols]
        qm = jnp.concatenate([jnp.where(lo, qp, zero), jnp.where(lo, zero, qp)], axis=0)
        kw = kcat[pl.ds(koff, NAT_KEYS), cols]
        s = lax.dot_general(qm, kw, (((1,), (1,)), ((), ())), preferred_element_type=jnp.float32)
        return s + tb_ref[p, sh], koff, qoff

    def attend(s, koff, qoff, p):
        cols = slice(LANES * p, LANES * (p + 1))
        vw = vcat[pl.ds(koff, NAT_KEYS), cols]
        m = jnp.max(s, axis=-1, keepdims=True)
        pe = jnp.exp(s - m)
        l = jnp.sum(pe, axis=-1, keepdims=True)
        o2 = jnp.dot(pe.astype(jnp.bfloat16), vw, preferred_element_type=jnp.float32)
        o2 = o2 * (1.0 / l)
        o_ref[pl.ds(qoff, GRID_W), cols] = jnp.where(lo, o2[:GRID_W], o2[GRID_W:]).astype(jnp.bfloat16)

    def trip(j, carry):
        chains = [(j * NAT_ROWS_PER_TRIP + q, p) for q in range(NAT_ROWS_PER_TRIP) for p in range(N_HEADS_B // 2)]
        pending = [scores(*c) for c in chains[:NAT_LOOKAHEAD]]
        for idx, (_, p) in enumerate(chains):
            s, koff, qoff = pending.pop(0)
            if idx + NAT_LOOKAHEAD < len(chains):
                pending.append(scores(*chains[idx + NAT_LOOKAHEAD]))
            attend(s, koff, qoff, p)
        return carry

    lax.fori_loop(0, NAT_ROWS_PER_STEP // NAT_ROWS_PER_TRIP, trip, 0)


def _nat_attention(qkv, tb, bsz, t):
    n = bsz * t
    rows_seq = t // GRID_W
    main = NAT_ROWS_PER_STEP * GRID_W
    halo = NAT_HALO_ROWS * GRID_W
    ntile = t // main
    nhalo_seq = t // halo
    per = main // halo

    def main_map(col):
        return lambda b, i: (b * ntile + i, col)

    def prev_map(col):
        return lambda b, i: (b * nhalo_seq + jnp.maximum(i * per - 1, 0), col)

    def next_map(col):
        return lambda b, i: (b * nhalo_seq + jnp.minimum(i * per + per, nhalo_seq - 1), col)

    qb, kb, vb = QB_COL // WIDTH_B, KB_COL // WIDTH_B, VB_COL // WIDTH_B
    return pl.pallas_call(
        functools.partial(_nat_kernel, rows_seq=rows_seq),
        out_shape=jax.ShapeDtypeStruct((n, WIDTH_B), jnp.bfloat16),
        grid=(bsz, ntile),
        in_specs=[
            pl.BlockSpec((main, WIDTH_B), main_map(qb)),
            pl.BlockSpec((halo, WIDTH_B), prev_map(kb)),
            pl.BlockSpec((main, WIDTH_B), main_map(kb)),
            pl.BlockSpec((halo, WIDTH_B), next_map(kb)),
            pl.BlockSpec((halo, WIDTH_B), prev_map(vb)),
            pl.BlockSpec((main, WIDTH_B), main_map(vb)),
            pl.BlockSpec((halo, WIDTH_B), next_map(vb)),
            pl.BlockSpec((N_HEADS_B // 2, NA_ROWS, 2 * GRID_W, NAT_KEYS), lambda b, i: (0, 0, 0, 0)),
        ],
        out_specs=pl.BlockSpec((main, WIDTH_B), main_map(0)),
        scratch_shapes=[pltpu.VMEM((main + 2 * halo, WIDTH_B), jnp.bfloat16),
                        pltpu.VMEM((main + 2 * halo, WIDTH_B), jnp.bfloat16)],
        compiler_params=_cparams("parallel", "parallel"),
        name="nat",
    )(qkv, qkv, qkv, qkv, qkv, qkv, qkv, tb)


EXPERT_ROW0 = N_GROUPS
ROUTE_ROWS = 48
INFO_E1, INFO_E2, INFO_R1, INFO_R2, INFO_W1, INFO_W2 = range(6)
INFO_ROWS = 8
MERGE_SUBTILES = 2


def _route(lt, carry, tri):
    rr, tm = lt.shape
    row = lax.broadcasted_iota(jnp.int32, (rr, tm), 0).astype(jnp.float32)
    none = jnp.float32(rr)

    def first_max(sel):
        m = jnp.max(jnp.where(sel, lt, NEG_BIG), axis=0, keepdims=True)
        idx = jnp.min(jnp.where(sel & (lt == m), row, none), axis=0, keepdims=True)
        return m, idx

    is_group = row < N_GROUPS
    mg, g = first_max(is_group)
    pg_sel = 1.0 / jnp.sum(jnp.where(is_group, jnp.exp(jnp.where(is_group, lt, mg) - mg), 0.0),
                           axis=0, keepdims=True)
    row0 = EXPERT_ROW0 + EXPERTS_PER_GROUP * g
    in_group = (row >= row0) & (row < row0 + EXPERTS_PER_GROUP)
    m1, i1 = first_max(in_group)
    m2, i2 = first_max(in_group & (row != i1))
    e2 = jnp.exp(m2 - m1)
    w1 = pg_sel / (1.0 + e2)
    w2 = pg_sel * e2 / (1.0 + e2)

    oh1 = row == i1
    oh2 = row == i2
    both = (oh1 | oh2).astype(jnp.bfloat16)
    before = jnp.dot(both, tri, preferred_element_type=jnp.float32) + carry
    r1 = jnp.sum(jnp.where(oh1, before, 0.0), axis=0, keepdims=True)
    r2 = jnp.sum(jnp.where(oh2, before, 0.0), axis=0, keepdims=True)
    new_carry = carry + jnp.sum(both.astype(jnp.float32), axis=1, keepdims=True)

    field = lax.broadcasted_iota(jnp.int32, (INFO_ROWS, tm), 0)
    info = jnp.zeros((INFO_ROWS, tm), jnp.float32)
    for k, v in ((INFO_E1, i1 - EXPERT_ROW0), (INFO_E2, i2 - EXPERT_ROW0), (INFO_R1, r1), (INFO_R2, r2),
                 (INFO_W1, w1), (INFO_W2, w2)):
        info = jnp.where(field == k, v, info)
    return info, new_carry


def _merge_kernel(x_ref, oa_ref, ob_ref, lng_ref, lnb_ref, wg_ref, wpa_ref, wpb_ref, wo_ref,
                  l1g_ref, l1b_ref, wr_ref, br_ref, cnt0_ref,
                  h1_ref, h1p_ref, info_ref, cnt_ref, carry_ref, tri_ref):
    tm = x_ref.shape[0]

    @pl.when(pl.program_id(0) == 0)
    def _():
        carry_ref[...] = cnt0_ref[...]
        r = lax.broadcasted_iota(jnp.int32, (tm, tm), 0)
        c = lax.broadcasted_iota(jnp.int32, (tm, tm), 1)
        tri_ref[...] = (r < c).astype(jnp.bfloat16)

    def project(rows):
        h = _layer_norm(x_ref[rows], lng_ref[...], lnb_ref[...])
        gates = jnp.dot(h.astype(jnp.bfloat16), wg_ref[...], preferred_element_type=jnp.float32)
        pa = jnp.dot(oa_ref[rows], wpa_ref[...], preferred_element_type=jnp.float32)
        pb = jnp.dot(ob_ref[rows], wpb_ref[...], preferred_element_type=jnp.float32)
        return h, gates, pa, pb

    def mix(h, gates, pa, pb):
        mixin = jax.nn.sigmoid(gates[:, :D_MODEL]) * pa + jax.nn.sigmoid(gates[:, D_MODEL:]) * pb
        return DEEPNORM_ALPHA * h + jnp.dot(mixin.astype(jnp.bfloat16), wo_ref[...],
                                            preferred_element_type=jnp.float32)

    def norm_and_logits(pre, rows):
        h1 = _layer_norm(pre, l1g_ref[...], l1b_ref[...])
        h1_ref[rows] = h1
        h1p_ref[rows] = _pack_rows(h1)
        hi = h1.astype(jnp.bfloat16)
        lo = (h1 - hi.astype(jnp.float32)).astype(jnp.bfloat16)
        lhs = jnp.concatenate([hi, lo, hi], axis=1)
        return lax.dot_general(wr_ref[...], lhs, (((1,), (1,)), ((), ())), preferred_element_type=jnp.float32)

    sub = tm // MERGE_SUBTILES
    parts = [slice(k * sub, (k + 1) * sub) for k in range(MERGE_SUBTILES)]
    projected = [project(rows) for rows in parts]
    mixed = [mix(*pr) for pr in projected]
    logits_t = jnp.concatenate([norm_and_logits(pre, rows) for pre, rows in zip(mixed, parts)], axis=1)
    logits_t = logits_t + br_ref[...]
    info, carry = _route(logits_t, carry_ref[...], tri_ref[...])
    info_ref[...] = info
    carry_ref[...] = carry
    cnt_ref[...] = carry[:, :LANES]


def _merge(x2, oa, ob, ln_g, ln_b, w_gates, w_pa, w_pb, w_o, l1g, l1b, w_r, b_r, cnt0, tm):
    n = x2.shape[0]

    def const(shape):
        return pl.BlockSpec(shape, lambda i: (0,) * len(shape))

    def rows(width):
        return pl.BlockSpec((tm, width), lambda i: (i, 0))

    return pl.pallas_call(
        _merge_kernel,
        out_shape=(jax.ShapeDtypeStruct((n, D_MODEL), jnp.float32),
                   jax.ShapeDtypeStruct((n, PACKED_WIDTH), jnp.uint32),
                   jax.ShapeDtypeStruct((INFO_ROWS, n), jnp.float32),
                   jax.ShapeDtypeStruct((ROUTE_ROWS, LANES), jnp.float32)),
        grid=(n // tm,),
        in_specs=[
            rows(D_MODEL), rows(WIDTH_A), rows(WIDTH_B),
            const((1, D_MODEL)), const((1, D_MODEL)),
            const((D_MODEL, 2 * D_MODEL)),
            const((WIDTH_A, D_MODEL)), const((WIDTH_B, D_MODEL)),
            const((D_MODEL, D_MODEL)),
            const((1, D_MODEL)), const((1, D_MODEL)),
            const((ROUTE_ROWS, 3 * D_MODEL)), const((ROUTE_ROWS, tm)), const((ROUTE_ROWS, tm)),
        ],
        out_specs=(rows(D_MODEL), rows(PACKED_WIDTH), pl.BlockSpec((INFO_ROWS, tm), lambda i: (0, i)),
                   const((ROUTE_ROWS, LANES))),
        scratch_shapes=[pltpu.VMEM((ROUTE_ROWS, tm), jnp.float32), pltpu.VMEM((tm, tm), jnp.bfloat16)],
        compiler_params=_cparams("arbitrary"),
        name="merge",
    )(x2, oa, ob, ln_g, ln_b, w_gates, w_pa, w_pb, w_o, l1g, l1b, w_r, b_r, cnt0)


SC_CORES = 2
SC_SUBCORES = 16
SC_WORKERS = SC_CORES * SC_SUBCORES
SC_ROWS_PER_STREAM = 64


def _sc_worker():
    return lax.axis_index("s") * SC_CORES + lax.axis_index("c")


def _dispatch(h1p, dest_t):
    n, width = h1p.shape
    per_worker = n // SC_WORKERS
    nchunks = per_worker // SC_ROWS_PER_STREAM
    assert nchunks * SC_ROWS_PER_STREAM * SC_WORKERS == n and nchunks % 2 == 0
    idx = dest_t.reshape(TOP_K, SC_WORKERS, nchunks, SC_ROWS_PER_STREAM)
    mesh = plsc.VectorSubcoreMesh(core_axis_name="c", subcore_axis_name="s")

    @functools.partial(
        pl.kernel, out_type=jax.ShapeDtypeStruct((TOP_K * n, width), h1p.dtype), mesh=mesh,
        scratch_types=[pltpu.VMEM((TOP_K, nchunks, SC_ROWS_PER_STREAM), jnp.int32),
                       pltpu.VMEM((2, SC_ROWS_PER_STREAM, width), h1p.dtype),
                       pltpu.SemaphoreType.DMA((2,)), pltpu.SemaphoreType.DMA((2,))],
        name="sc_dispatch")
    def scatter_kernel(src_hbm, idx_hbm, out_hbm, idx_v, rows_v, rsem, ssem):
        wid = _sc_worker()
        base = wid * per_worker
        for k in range(TOP_K):
            pltpu.sync_copy(idx_hbm.at[k, wid], idx_v.at[k])

        def read(j, slot):
            src = src_hbm.at[pl.ds(base + j * SC_ROWS_PER_STREAM, SC_ROWS_PER_STREAM)]
            return pltpu.make_async_copy(src, rows_v.at[slot], rsem.at[slot])

        def scatter(j, slot, k):
            return pltpu.make_async_copy(rows_v.at[slot], out_hbm.at[idx_v.at[k, j]], ssem.at[slot])

        read(0, 0).start()

        @pl.loop(0, nchunks, step=2)
        def _(j0):
            for slot in range(2):
                j = j0 + slot
                read(j, slot).wait()

                @pl.when(j + 1 < nchunks)
                def _():
                    @pl.when(j >= 1)
                    def _():
                        for k in range(TOP_K):
                            scatter(j - 1, 1 - slot, k).wait()
                    read(j + 1, 1 - slot).start()

                for k in range(TOP_K):
                    scatter(j, slot, k).start()

        for k in range(TOP_K):
            scatter(nchunks - 2, 0, k).wait()
            scatter(nchunks - 1, 1, k).wait()

    return scatter_kernel(h1p, idx)


def _moe_plan(counts, na, bm):
    expert = jnp.arange(N_EXPERTS, dtype=jnp.int32)
    upto = expert[None, :] <= expert[:, None]

    def running_sum(v):
        return jnp.sum(jnp.where(upto, v[None, :], 0), axis=1)

    ends = running_sum(counts)
    starts = ends - counts
    first_blk = starts // bm
    tiles = jnp.where(counts > 0, (ends - 1) // bm - first_blk + 1, 0)
    item_end = running_sum(tiles)
    item_start = item_end - tiles
    total = jnp.sum(tiles)
    wmax = na // bm + N_EXPERTS - 1
    w = jnp.arange(wmax, dtype=jnp.int32)
    wc = jnp.minimum(w, total - 1)
    e = jnp.sum((item_end[None, :] <= wc[:, None]).astype(jnp.int32), axis=1)
    e = jnp.minimum(e, N_EXPERTS - 1)
    owner = e[:, None] == expert[None, :]

    def of_owner(table):
        return jnp.sum(jnp.where(owner, table[None, :], 0), axis=1)

    blk = (of_owner(first_blk) + (wc - of_owner(item_start))).astype(jnp.int32)
    valid = w < total
    lo = jnp.where(valid, jnp.maximum(of_owner(starts), blk * bm), 0).astype(jnp.int32)
    hi = jnp.where(valid, jnp.minimum(of_owner(ends), (blk + 1) * bm), 0).astype(jnp.int32)
    prev_blk = jnp.concatenate([jnp.full((1,), -1, jnp.int32), blk[:-1]])
    prev_e = jnp.concatenate([jnp.full((1,), -1, jnp.int32), e[:-1]])
    flags = (valid.astype(jnp.int32)
             + 2 * (valid & (blk != prev_blk)).astype(jnp.int32)
             + 4 * (valid & (e != prev_e)).astype(jnp.int32))
    return blk, e, lo, hi, flags


FLAG_VALID, FLAG_NEW_BLOCK, FLAG_NEW_EXPERT = 1, 2, 4


def _expert_kernel(blk_ref, e_ref, lo_ref, hi_ref, flag_ref, x_ref, wg_ref, wu_ref, wd_ref, o_ref,
                   wg_b, wu_b, wd_b):
    w = pl.program_id(0)
    bm = x_ref.shape[0]
    flags = flag_ref[w]

    @pl.when((flags & FLAG_NEW_EXPERT) != 0)
    def _():
        wg_b[...] = wg_ref[0].astype(jnp.bfloat16)
        wu_b[...] = wu_ref[0].astype(jnp.bfloat16)
        wd_b[...] = wd_ref[0].astype(jnp.bfloat16)

    @pl.when((flags & FLAG_VALID) != 0)
    def _():
        x = _unpack_rows(x_ref[...]).astype(jnp.bfloat16)
        g = jnp.dot(x, wg_b[...], preferred_element_type=jnp.float32)
        u = jnp.dot(x, wu_b[...], preferred_element_type=jnp.float32)
        hmid = (jax.nn.silu(g) * u).astype(jnp.bfloat16)
        y = _pack_rows(jnp.dot(hmid, wd_b[...], preferred_element_type=jnp.float32))
        row = blk_ref[w] * bm + lax.broadcasted_iota(jnp.int32, (bm, 1), 0)
        mine = (row >= lo_ref[w]) & (row < hi_ref[w])

        @pl.when((flags & FLAG_NEW_BLOCK) != 0)
        def _():
            o_ref[...] = jnp.where(mine, y, jnp.uint32(0))

        @pl.when((flags & FLAG_NEW_BLOCK) == 0)
        def _():
            o_ref[...] = jnp.where(mine, y, o_ref[...])


def _experts(xs, plan, w_gate, w_up, w_down, bm):
    na = xs.shape[0]
    blk, e, lo, hi, flags = plan
    nitems = blk.shape[0]
    grid_spec = pltpu.PrefetchScalarGridSpec(
        num_scalar_prefetch=5,
        grid=(nitems,),
        in_specs=[
            pl.BlockSpec((bm, PACKED_WIDTH), lambda w, blk, e, *_: (blk[w], 0)),
            pl.BlockSpec((1, D_MODEL, D_EXPERT), lambda w, blk, e, *_: (e[w], 0, 0)),
            pl.BlockSpec((1, D_MODEL, D_EXPERT), lambda w, blk, e, *_: (e[w], 0, 0)),
            pl.BlockSpec((1, D_EXPERT, D_MODEL), lambda w, blk, e, *_: (e[w], 0, 0)),
        ],
        out_specs=pl.BlockSpec((bm, PACKED_WIDTH), lambda w, blk, e, *_: (blk[w], 0)),
        scratch_shapes=[pltpu.VMEM((D_MODEL, D_EXPERT), jnp.bfloat16),
                        pltpu.VMEM((D_MODEL, D_EXPERT), jnp.bfloat16),
                        pltpu.VMEM((D_EXPERT, D_MODEL), jnp.bfloat16)],
    )
    return pl.pallas_call(
        _expert_kernel,
        out_shape=jax.ShapeDtypeStruct((na, PACKED_WIDTH), jnp.uint32),
        grid_spec=grid_spec,
        compiler_params=_cparams("arbitrary"),
        name="experts",
    )(blk, e, lo, hi, flags, xs, w_gate, w_up, w_down)


def _sc_gather_rows(table, idx):
    nrows = idx.shape[0]
    width = table.shape[1]
    per_worker = nrows // SC_WORKERS
    nchunks = per_worker // SC_ROWS_PER_STREAM
    assert nchunks * SC_ROWS_PER_STREAM * SC_WORKERS == nrows and nchunks % 2 == 0
    mesh = plsc.VectorSubcoreMesh(core_axis_name="c", subcore_axis_name="s")

    @functools.partial(
        pl.kernel, out_type=jax.ShapeDtypeStruct((nrows, width), table.dtype), mesh=mesh,
        scratch_types=[pltpu.VMEM((per_worker,), jnp.int32),
                       pltpu.VMEM((2, SC_ROWS_PER_STREAM, width), table.dtype),
                       pltpu.SemaphoreType.DMA((2,)), pltpu.SemaphoreType.DMA((2,))],
        name="sc_gather")
    def gather_kernel(table_hbm, idx_hbm, out_hbm, idx_v, rows_v, gsem, wsem):
        base = _sc_worker() * per_worker
        pltpu.sync_copy(idx_hbm.at[pl.ds(base, per_worker)], idx_v)

        def gather(j, slot):
            rows = idx_v.at[pl.ds(j * SC_ROWS_PER_STREAM, SC_ROWS_PER_STREAM)]
            return pltpu.make_async_copy(table_hbm.at[rows], rows_v.at[slot], gsem.at[slot])

        def write(j, slot):
            dst = out_hbm.at[pl.ds(base + j * SC_ROWS_PER_STREAM, SC_ROWS_PER_STREAM)]
            return pltpu.make_async_copy(rows_v.at[slot], dst, wsem.at[slot])

        gather(0, 0).start()

        @pl.loop(0, nchunks, step=2)
        def _(j0):
            for slot in range(2):
                j = j0 + slot
                gather(j, slot).wait()

                @pl.when(j + 1 < nchunks)
                def _():
                    @pl.when(j >= 1)
                    def _():
                        write(j - 1, 1 - slot).wait()
                    gather(j + 1, 1 - slot).start()

                write(j, slot).start()

        write(nchunks - 2, 0).wait()
        write(nchunks - 1, 1).wait()

    return gather_kernel(table, idx)


def _finalize_kernel(h1_ref, y1_ref, y2_ref, info_ref, g_ref, b_ref, o_ref):
    tt = h1_ref.shape[0]
    pad = jnp.zeros((LANES - INFO_ROWS, tt), jnp.float32)
    info = jnp.concatenate([info_ref[...], pad], axis=0).T
    moe = (_unpack_rows(y1_ref[...]) * info[:, INFO_W1:INFO_W1 + 1]
           + _unpack_rows(y2_ref[...]) * info[:, INFO_W2:INFO_W2 + 1])
    o_ref[...] = _layer_norm(DEEPNORM_ALPHA * h1_ref[...] + moe, g_ref[...], b_ref[...])


def _combine(h1, info, dest_t, ys, ln_g, ln_b, tt):
    n = h1.shape[0]
    nsteps = n // tt
    yg = _sc_gather_rows(ys, dest_t.reshape(TOP_K * n))
    return pl.pallas_call(
        _finalize_kernel,
        out_shape=jax.ShapeDtypeStruct((n, D_MODEL), jnp.float32),
        grid=(nsteps,),
        in_specs=[
            pl.BlockSpec((tt, D_MODEL), lambda i: (i, 0)),
            pl.BlockSpec((tt, PACKED_WIDTH), lambda i: (i, 0)),
            pl.BlockSpec((tt, PACKED_WIDTH), lambda i: (nsteps + i, 0)),
            pl.BlockSpec((INFO_ROWS, tt), lambda i: (0, i)),
            pl.BlockSpec((1, D_MODEL), lambda i: (0, 0)),
            pl.BlockSpec((1, D_MODEL), lambda i: (0, 0)),
        ],
        out_specs=pl.BlockSpec((tt, D_MODEL), lambda i: (i, 0)),
        compiler_params=_cparams("parallel"),
        name="finalize",
    )(h1, yg, yg, info, ln_g, ln_b)


TM_QKV = 1024
TQ_WIN = 512
TM_MERGE = 1024
TT_ROWS = 256
BM_EXPERT = 256


def _prepare_weights(ln_in_g, ln_in_b, w_in, attn_sink, rel_pos_bias, w_proj_a, w_proj_b, w_out,
                     ln1_g, ln1_b, w_route_group, b_route_group, w_route_expert, b_route_expert,
                     ln2_g, ln2_b):
    bf = jnp.bfloat16
    w = w_in[0]
    splits = np.cumsum([WIDTH_A, KV_WIDTH_A, KV_WIDTH_A, WIDTH_B, WIDTH_B, WIDTH_B, D_MODEL])
    wqa, wka, wva, wqb, wkb, wvb, wga, wgb = jnp.split(w, [int(s) for s in splits], axis=1)
    wqa = (wqa.reshape(D_MODEL, N_KV_HEADS_A, GQA_GROUP, HEAD_DIM).transpose(0, 2, 1, 3)
           .reshape(D_MODEL, WIDTH_A))
    w_qkv = jnp.concatenate([wqa, wqb, wkb, wvb, wka, wva], axis=1).astype(bf)
    w_gates = jnp.concatenate([wga, wgb], axis=1).astype(bf)
    w_pa = (w_proj_a[0].reshape(N_KV_HEADS_A, GQA_GROUP, HEAD_DIM, D_MODEL).transpose(1, 0, 2, 3)
            .reshape(WIDTH_A, D_MODEL).astype(bf))
    w_pb = w_proj_b[0].astype(bf)
    w_o = w_out[0].astype(bf)
    pad = ROUTE_ROWS - N_GROUPS - N_EXPERTS
    w_r = jnp.concatenate([w_route_group[0].T, w_route_expert[0].T, jnp.zeros((pad, D_MODEL), jnp.float32)], axis=0)
    w_r_hi = w_r.astype(bf)
    w_r_lo = (w_r - w_r_hi.astype(jnp.float32)).astype(bf)
    w_r3 = jnp.concatenate([w_r_hi, w_r_hi, w_r_lo], axis=1)
    b_r = jnp.concatenate([b_route_group[0], b_route_expert[0], jnp.zeros((pad,), jnp.float32)])
    b_r = jnp.broadcast_to(b_r[:, None], (ROUTE_ROWS, TM_MERGE))
    row = lambda v: v.reshape(1, D_MODEL)
    return dict(
        ln_in_g=row(ln_in_g), ln_in_b=row(ln_in_b), w_qkv=w_qkv, w_gates=w_gates,
        sink=attn_sink[0].astype(jnp.float32), nat_bias=_nat_bias_table(rel_pos_bias[0]),
        w_pa=w_pa, w_pb=w_pb, w_o=w_o, ln1_g=row(ln1_g[0]), ln1_b=row(ln1_b[0]),
        w_r3=w_r3, b_r=b_r, ln2_g=row(ln2_g[0]), ln2_b=row(ln2_b[0]))


def _after(value, other):
    if other is None:
        return value
    return value + (other.astype(jnp.float32) * 0.0).astype(value.dtype)


def _attend_and_route(x, p, after=None):
    bsz, t, _ = x.shape
    n = bsz * t
    x2 = x.reshape(n, D_MODEL)
    qkv = _qkv(x2, p["ln_in_g"], p["ln_in_b"], p["w_qkv"], TM_QKV)
    oa = _win_attention(qkv, p["sink"], bsz, t, TQ_WIN)
    ob = _nat_attention(qkv, p["nat_bias"], bsz, t)
    cnt0 = _after(jnp.zeros((ROUTE_ROWS, TM_MERGE), jnp.float32), after)
    h1, h1p, info, cnt = _merge(x2, oa, ob, p["ln_in_g"], p["ln_in_b"], p["w_gates"], p["w_pa"], p["w_pb"],
                                p["w_o"], p["ln1_g"], p["ln1_b"], p["w_r3"], p["b_r"], cnt0, TM_MERGE)
    counts = cnt[EXPERT_ROW0:EXPERT_ROW0 + N_EXPERTS, 0].astype(jnp.int32)
    expert = jnp.arange(N_EXPERTS, dtype=jnp.int32)
    starts = jnp.sum(jnp.where(expert[None, :] < expert[:, None], counts[None, :], 0), axis=1)
    eid = info[INFO_E1:INFO_E2 + 1].astype(jnp.int32)
    rank = info[INFO_R1:INFO_R2 + 1].astype(jnp.int32)
    dest_t = rank + jnp.sum(jnp.where(eid[None] == expert[:, None, None], starts[:, None, None], 0), axis=0)
    return dict(shape=x.shape, h1=h1, h1p=h1p, info=info, counts=counts, dest_t=dest_t)


def _run_experts(r, w_gate, w_up, w_down, after=None):
    n = r["h1"].shape[0]
    xs = _dispatch(r["h1p"], r["dest_t"])
    blk, e, lo, hi, flags = _moe_plan(r["counts"], TOP_K * n, BM_EXPERT)
    plan = (blk, e, lo, hi, _after(flags, after))
    return _experts(xs, plan, w_gate[0], w_up[0], w_down[0], BM_EXPERT)


def _finish(r, ys, p):
    out = _combine(r["h1"], r["info"], r["dest_t"], ys, p["ln2_g"], p["ln2_b"], TT_ROWS)
    return out.reshape(r["shape"])


def kernel(x_prompt, x_sample, ln_in_g, ln_in_b, w_in, attn_sink, rel_pos_bias, w_proj_a, w_proj_b, w_out,
           ln1_g, ln1_b, w_route_group, b_route_group, w_route_expert, b_route_expert,
           w_gate, w_up, w_down, ln2_g, ln2_b):
    p = _prepare_weights(ln_in_g, ln_in_b, w_in, attn_sink, rel_pos_bias, w_proj_a, w_proj_b, w_out,
                         ln1_g, ln1_b, w_route_group, b_route_group, w_route_expert, b_route_expert,
                         ln2_g, ln2_b)
    rp = _attend_and_route(x_prompt, p)
    rs = _attend_and_route(x_sample, p, after=rp["counts"][0])
    ys_p = _run_experts(rp, w_gate, w_up, w_down)
    ys_s = _run_experts(rs, w_gate, w_up, w_down, after=ys_p[0, 0])
    return (_finish(rp, ys_p, p), _finish(rs, ys_s, p))
```

```python
import functools

import numpy as np
import jax
import jax.numpy as jnp
from jax import lax
from jax.experimental import pallas as pl
from jax.experimental.pallas import tpu as pltpu
from jax.experimental.pallas import tpu_sc as plsc

D_MODEL = 1024
HEAD_DIM = 64
N_HEADS_A = 8
N_KV_HEADS_A = 2
WINDOW = 128
N_HEADS_B = 8
GRID_W = 64
NA_ROWS = 8
NA_COLS = 16
N_GROUPS = 4
EXPERTS_PER_GROUP = 8
N_EXPERTS = N_GROUPS * EXPERTS_PER_GROUP
TOP_K = 2
D_EXPERT = D_MODEL // 2
LN_EPS = 1e-5
DEPTH = 1
DEEPNORM_ALPHA = (2.0 * DEPTH) ** 0.25
WIDTH_A = N_HEADS_A * HEAD_DIM
KV_WIDTH_A = N_KV_HEADS_A * HEAD_DIM
WIDTH_B = N_HEADS_B * HEAD_DIM
QKV_WIDTH = WIDTH_A + 2 * KV_WIDTH_A + 3 * WIDTH_B

LANES = 128
VMEM_LIMIT_BYTES = 56 * 1024 * 1024

NEG_BIG = -1e30

QA_COL, QB_COL, KB_COL, VB_COL = 0, WIDTH_A, WIDTH_A + WIDTH_B, WIDTH_A + 2 * WIDTH_B
KA_COL = WIDTH_A + 3 * WIDTH_B
VA_COL = KA_COL + KV_WIDTH_A

GQA_GROUP = N_HEADS_A // N_KV_HEADS_A


def _cparams(*sem):
    return pltpu.CompilerParams(dimension_semantics=sem, vmem_limit_bytes=VMEM_LIMIT_BYTES)


def _layer_norm(x, g, b):
    mu = jnp.mean(x, axis=-1, keepdims=True)
    xc = x - mu
    var = jnp.mean(xc * xc, axis=-1, keepdims=True)
    return xc * lax.rsqrt(var + LN_EPS) * g + b


PACKED_WIDTH = D_MODEL // 2


def _pack_rows(x):
    def rne(v):
        return v + jnp.uint32(0x7FFF) + ((v >> 16) & jnp.uint32(1))
    hi = lax.bitcast_convert_type(x[:, :PACKED_WIDTH], jnp.uint32)
    lo = lax.bitcast_convert_type(x[:, PACKED_WIDTH:], jnp.uint32)
    return (rne(hi) & jnp.uint32(0xFFFF0000)) | (rne(lo) >> 16)


def _unpack_rows(w):
    hi = lax.bitcast_convert_type(w & jnp.uint32(0xFFFF0000), jnp.float32)
    lo = lax.bitcast_convert_type(w << 16, jnp.float32)
    return jnp.concatenate([hi, lo], axis=1)


def _qkv_kernel(x_ref, g_ref, b_ref, w_ref, o_ref):
    h = _layer_norm(x_ref[...], g_ref[...], b_ref[...])
    y = jnp.dot(h.astype(jnp.bfloat16), w_ref[...], preferred_element_type=jnp.float32)
    col = lax.broadcasted_iota(jnp.int32, (1, QKV_WIDTH), 1)
    y = y * jnp.where(col < KB_COL, HEAD_DIM ** -0.5, 1.0)
    o_ref[...] = y.astype(jnp.bfloat16)


def _qkv(x2, ln_g, ln_b, w_qkv, tm):
    n = x2.shape[0]
    return pl.pallas_call(
        _qkv_kernel,
        out_shape=jax.ShapeDtypeStruct((n, QKV_WIDTH), jnp.bfloat16),
        grid=(n // tm,),
        in_specs=[
            pl.BlockSpec((tm, D_MODEL), lambda i: (i, 0)),
            pl.BlockSpec((1, D_MODEL), lambda i: (0, 0)),
            pl.BlockSpec((1, D_MODEL), lambda i: (0, 0)),
            pl.BlockSpec((D_MODEL, QKV_WIDTH), lambda i: (0, 0)),
        ],
        out_specs=pl.BlockSpec((tm, QKV_WIDTH), lambda i: (i, 0)),
        compiler_params=_cparams("parallel"),
        name="qkv",
    )(x2, ln_g, ln_b, w_qkv)


WIN_BLK = 128
WIN_LOOKAHEAD = 2


def _win_bias_table():
    qi = np.arange(WIN_BLK)[:, None]
    kj = np.arange(3 * WIN_BLK)[None, :]
    dist = np.abs(kj - WIN_BLK - qi).astype(np.float64)
    slopes = 2.0 ** (-8.0 * np.arange(1, N_HEADS_A + 1) / N_HEADS_A)
    per_head = np.where(dist <= WINDOW, -slopes[:, None, None] * dist[None], NEG_BIG)
    groups = [np.concatenate([per_head[j], per_head[j + 4]], axis=0) for j in range(4)]
    return np.stack(groups).astype(np.float32)


def _win_kernel(sink_ref, q_ref, kp_ref, km_ref, kn_ref, vp_ref, vm_ref, vn_ref, bias_ref, o_ref,
                *, nsub, nblk_seq):
    i = pl.program_id(1)
    kcat = jnp.concatenate([kp_ref[...], km_ref[...], kn_ref[...]], axis=0)
    vcat = jnp.concatenate([vp_ref[...], vm_ref[...], vn_ref[...]], axis=0)
    lo = lax.broadcasted_iota(jnp.int32, (1, LANES), 1) < HEAD_DIM
    col = lax.broadcasted_iota(jnp.int32, (1, 3 * WIN_BLK), 1)
    top = lax.broadcasted_iota(jnp.int32, (2 * WIN_BLK, 1), 0) < WIN_BLK
    zero = jnp.zeros((), jnp.bfloat16)

    def scores(j, g):
        n = i * nsub + j
        off_seq = ((col < WIN_BLK) & (n == 0)) | ((col >= 2 * WIN_BLK) & (n == nblk_seq - 1))
        edge = jnp.where(off_seq, NEG_BIG, 0.0)
        qg = q_ref[WIN_BLK * j:WIN_BLK * (j + 1), LANES * g:LANES * (g + 1)]
        qm = jnp.concatenate([jnp.where(lo, qg, zero), jnp.where(lo, zero, qg)], axis=0)
        kj = kcat[WIN_BLK * j:WIN_BLK * (j + 3)]
        s = lax.dot_general(qm, kj, (((1,), (1,)), ((), ())), preferred_element_type=jnp.float32)
        return s + bias_ref[g] + edge

    def attend(s, j, g):
        vj = vcat[WIN_BLK * j:WIN_BLK * (j + 3)]
        sink = jnp.where(top, sink_ref[g], sink_ref[g + 4])
        m = jnp.maximum(jnp.max(s, axis=-1, keepdims=True), sink)
        p = jnp.exp(s - m)
        l = jnp.sum(p, axis=-1, keepdims=True) + jnp.exp(sink - m)
        o2 = jnp.dot(p.astype(jnp.bfloat16), vj, preferred_element_type=jnp.float32)
        o2 = o2 * (1.0 / l)
        o_ref[WIN_BLK * j:WIN_BLK * (j + 1), LANES * g:LANES * (g + 1)] = (
            jnp.where(lo, o2[:WIN_BLK], o2[WIN_BLK:]).astype(jnp.bfloat16))

    chains = [(j, g) for j in range(nsub) for g in range(4)]
    pending = [scores(*c) for c in chains[:WIN_LOOKAHEAD]]
    for idx, c in enumerate(chains):
        s = pending.pop(0)
        if idx + WIN_LOOKAHEAD < len(chains):
            pending.append(scores(*chains[idx + WIN_LOOKAHEAD]))
        attend(s, *c)


def _win_attention(qkv, sink, bsz, t, tq):
    n = bsz * t
    nsub = tq // WIN_BLK
    nblk_seq = t // WIN_BLK
    ntile = t // tq
    bias = jnp.asarray(_win_bias_table())

    def main_map(col):
        return lambda b, i, *_: (b * ntile + i, col)

    def prev_map(col):
        return lambda b, i, *_: (b * nblk_seq + jnp.maximum(i * nsub - 1, 0), col)

    def next_map(col):
        return lambda b, i, *_: (b * nblk_seq + jnp.minimum(i * nsub + nsub, nblk_seq - 1), col)

    halo = (WIN_BLK, LANES)
    ka, va = KA_COL // LANES, VA_COL // LANES
    grid_spec = pltpu.PrefetchScalarGridSpec(
        num_scalar_prefetch=1,
        grid=(bsz, ntile),
        in_specs=[
            pl.BlockSpec((tq, WIDTH_A), main_map(QA_COL // WIDTH_A)),
            pl.BlockSpec(halo, prev_map(ka)),
            pl.BlockSpec((tq, LANES), main_map(ka)),
            pl.BlockSpec(halo, next_map(ka)),
            pl.BlockSpec(halo, prev_map(va)),
            pl.BlockSpec((tq, LANES), main_map(va)),
            pl.BlockSpec(halo, next_map(va)),
            pl.BlockSpec((4, 2 * WIN_BLK, 3 * WIN_BLK), lambda b, i, *_: (0, 0, 0)),
        ],
        out_specs=pl.BlockSpec((tq, WIDTH_A), main_map(0)),
    )
    return pl.pallas_call(
        functools.partial(_win_kernel, nsub=nsub, nblk_seq=nblk_seq),
        out_shape=jax.ShapeDtypeStruct((n, WIDTH_A), jnp.bfloat16),
        grid_spec=grid_spec,
        compiler_params=_cparams("parallel", "parallel"),
        name="win",
    )(sink, qkv, qkv, qkv, qkv, qkv, qkv, qkv, bias)


NAT_ROWS_PER_STEP = 8
NAT_HALO_ROWS = NA_ROWS // 2
NAT_KEYS = NA_ROWS * GRID_W
NAT_ROWS_PER_TRIP = 8
NAT_LOOKAHEAD = 4


def _nat_bias_table(rpb):
    c = np.arange(GRID_W)
    cs = np.clip(c - NA_COLS // 2, 0, GRID_W - NA_COLS)
    col_mask = (c[None, :] >= cs[:, None]) & (c[None, :] < cs[:, None] + NA_COLS)
    dc = np.clip(c[None, :] - c[:, None] + (NA_COLS - 1), 0, 2 * NA_COLS - 2)
    onehot = jnp.asarray(dc[None] == np.arange(2 * NA_COLS - 1)[:, None, None], jnp.float32)
    picked = jnp.einsum("hdj,jqc->hdqc", rpb, onehot, precision=lax.Precision.HIGHEST)
    t1 = jnp.where(col_mask[None, None], picked, NEG_BIG)
    per_shift = []
    for sh in range(NA_ROWS):
        w = t1[:, sh:sh + NA_ROWS]
        w = jnp.transpose(w, (0, 2, 1, 3)).reshape(N_HEADS_B // 2, 2 * GRID_W, NAT_KEYS)
        per_shift.append(w)
    return jnp.stack(per_shift, axis=1).astype(jnp.float32)


def _nat_kernel(q_ref, kp_ref, km_ref, kn_ref, vp_ref, vm_ref, vn_ref, tb_ref, o_ref, kcat, vcat,
                *, rows_seq):
    i = pl.program_id(1)
    halo = NAT_HALO_ROWS * GRID_W
    main = NAT_ROWS_PER_STEP * GRID_W
    kcat[0:halo] = kp_ref[...]
    kcat[halo:halo + main] = km_ref[...]
    kcat[halo + main:2 * halo + main] = kn_ref[...]
    vcat[0:halo] = vp_ref[...]
    vcat[halo:halo + main] = vm_ref[...]
    vcat[halo + main:2 * halo + main] = vn_ref[...]
    lo = lax.broadcasted_iota(jnp.int32, (1, LANES), 1) < HEAD_DIM
    zero = jnp.zeros((), jnp.bfloat16)
    r0 = i * NAT_ROWS_PER_STEP

    def scores(qr, p):
        r = r0 + qr
        rs = jnp.clip(r - NA_ROWS // 2, 0, rows_seq - NA_ROWS)
        koff = pl.multiple_of((rs - r0 + NAT_HALO_ROWS) * GRID_W, GRID_W)
        sh = rs - r + (NA_ROWS - 1)
        qoff = pl.multiple_of(qr * GRID_W, GRID_W)
        cols = slice(LANES * p, LANES * (p + 1))
        qp = q_ref[pl.ds(qoff, GRID_W), cols]
        qm = jnp.concatenate([jnp.where(lo, qp, zero), jnp.where(lo, zero, qp)], axis=0)
        kw = kcat[pl.ds(koff, NAT_KEYS), cols]
        s = lax.dot_general(qm, kw, (((1,), (1,)), ((), ())), preferred_element_type=jnp.float32)
        return s + tb_ref[p, sh], koff, qoff

    def attend(s, koff, qoff, p):
        cols = slice(LANES * p, LANES * (p + 1))
        vw = vcat[pl.ds(koff, NAT_KEYS), cols]
        m = jnp.max(s, axis=-1, keepdims=True)
        pe = jnp.exp(s - m)
        l = jnp.sum(pe, axis=-1, keepdims=True)
        o2 = jnp.dot(pe.astype(jnp.bfloat16), vw, preferred_element_type=jnp.float32)
        o2 = o2 * (1.0 / l)
        o_ref[pl.ds(qoff, GRID_W), cols] = jnp.where(lo, o2[:GRID_W], o2[GRID_W:]).astype(jnp.bfloat16)

    def trip(j, carry):
        chains = [(j * NAT_ROWS_PER_TRIP + q, p) for q in range(NAT_ROWS_PER_TRIP) for p in range(N_HEADS_B // 2)]
        pending = [scores(*c) for c in chains[:NAT_LOOKAHEAD]]
        for idx, (_, p) in enumerate(chains):
            s, koff, qoff = pending.pop(0)
            if idx + NAT_LOOKAHEAD < len(chains):
                pending.append(scores(*chains[idx + NAT_LOOKAHEAD]))
            attend(s, koff, qoff, p)
        return carry

    lax.fori_loop(0, NAT_ROWS_PER_STEP // NAT_ROWS_PER_TRIP, trip, 0)


def _nat_attention(qkv, tb, bsz, t):
    n = bsz * t
    rows_seq = t // GRID_W
    main = NAT_ROWS_PER_STEP * GRID_W
    halo = NAT_HALO_ROWS * GRID_W
    ntile = t // main
    nhalo_seq = t // halo
    per = main // halo

    def main_map(col):
        return lambda b, i: (b * ntile + i, col)

    def prev_map(col):
        return lambda b, i: (b * nhalo_seq + jnp.maximum(i * per - 1, 0), col)

    def next_map(col):
        return lambda b, i: (b * nhalo_seq + jnp.minimum(i * per + per, nhalo_seq - 1), col)

    qb, kb, vb = QB_COL // WIDTH_B, KB_COL // WIDTH_B, VB_COL // WIDTH_B
    return pl.pallas_call(
        functools.partial(_nat_kernel, rows_seq=rows_seq),
        out_shape=jax.ShapeDtypeStruct((n, WIDTH_B), jnp.bfloat16),
        grid=(bsz, ntile),
        in_specs=[
            pl.BlockSpec((main, WIDTH_B), main_map(qb)),
            pl.BlockSpec((halo, WIDTH_B), prev_map(kb)),
            pl.BlockSpec((main, WIDTH_B), main_map(kb)),
            pl.BlockSpec((halo, WIDTH_B), next_map(kb)),
            pl.BlockSpec((halo, WIDTH_B), prev_map(vb)),
            pl.BlockSpec((main, WIDTH_B), main_map(vb)),
            pl.BlockSpec((halo, WIDTH_B), next_map(vb)),
            pl.BlockSpec((N_HEADS_B // 2, NA_ROWS, 2 * GRID_W, NAT_KEYS), lambda b, i: (0, 0, 0, 0)),
        ],
        out_specs=pl.BlockSpec((main, WIDTH_B), main_map(0)),
        scratch_shapes=[pltpu.VMEM((main + 2 * halo, WIDTH_B), jnp.bfloat16),
                        pltpu.VMEM((main + 2 * halo, WIDTH_B), jnp.bfloat16)],
        compiler_params=_cparams("parallel", "parallel"),
        name="nat",
    )(qkv, qkv, qkv, qkv, qkv, qkv, qkv, tb)


EXPERT_ROW0 = N_GROUPS
ROUTE_ROWS = 48
INFO_E1, INFO_E2, INFO_R1, INFO_R2, INFO_W1, INFO_W2 = range(6)
INFO_ROWS = 8
MERGE_SUBTILES = 2


def _route(lt, carry, tri):
    rr, tm = lt.shape
    row = lax.broadcasted_iota(jnp.int32, (rr, tm), 0).astype(jnp.float32)
    none = jnp.float32(rr)

    def first_max(sel):
        m = jnp.max(jnp.where(sel, lt, NEG_BIG), axis=0, keepdims=True)
        idx = jnp.min(jnp.where(sel & (lt == m), row, none), axis=0, keepdims=True)
        return m, idx

    is_group = row < N_GROUPS
    mg, g = first_max(is_group)
    pg_sel = 1.0 / jnp.sum(jnp.where(is_group, jnp.exp(jnp.where(is_group, lt, mg) - mg), 0.0),
                           axis=0, keepdims=True)
    row0 = EXPERT_ROW0 + EXPERTS_PER_GROUP * g
    in_group = (row >= row0) & (row < row0 + EXPERTS_PER_GROUP)
    m1, i1 = first_max(in_group)
    m2, i2 = first_max(in_group & (row != i1))
    e2 = jnp.exp(m2 - m1)
    w1 = pg_sel / (1.0 + e2)
    w2 = pg_sel * e2 / (1.0 + e2)

    oh1 = row == i1
    oh2 = row == i2
    both = (oh1 | oh2).astype(jnp.bfloat16)
    before = jnp.dot(both, tri, preferred_element_type=jnp.float32) + carry
    r1 = jnp.sum(jnp.where(oh1, before, 0.0), axis=0, keepdims=True)
    r2 = jnp.sum(jnp.where(oh2, before, 0.0), axis=0, keepdims=True)
    new_carry = carry + jnp.sum(both.astype(jnp.float32), axis=1, keepdims=True)

    field = lax.broadcasted_iota(jnp.int32, (INFO_ROWS, tm), 0)
    info = jnp.zeros((INFO_ROWS, tm), jnp.float32)
    for k, v in ((INFO_E1, i1 - EXPERT_ROW0), (INFO_E2, i2 - EXPERT_ROW0), (INFO_R1, r1), (INFO_R2, r2),
                 (INFO_W1, w1), (INFO_W2, w2)):
        info = jnp.where(field == k, v, info)
    return info, new_carry


def _merge_kernel(x_ref, oa_ref, ob_ref, lng_ref, lnb_ref, wg_ref, wpa_ref, wpb_ref, wo_ref,
                  l1g_ref, l1b_ref, wr_ref, br_ref, cnt0_ref,
                  h1_ref, h1p_ref, info_ref, cnt_ref, carry_ref, tri_ref):
    tm = x_ref.shape[0]

    @pl.when(pl.program_id(0) == 0)
    def _():
        carry_ref[...] = cnt0_ref[...]
        r = lax.broadcasted_iota(jnp.int32, (tm, tm), 0)
        c = lax.broadcasted_iota(jnp.int32, (tm, tm), 1)
        tri_ref[...] = (r < c).astype(jnp.bfloat16)

    def project(rows):
        h = _layer_norm(x_ref[rows], lng_ref[...], lnb_ref[...])
        gates = jnp.dot(h.astype(jnp.bfloat16), wg_ref[...], preferred_element_type=jnp.float32)
        pa = jnp.dot(oa_ref[rows], wpa_ref[...], preferred_element_type=jnp.float32)
        pb = jnp.dot(ob_ref[rows], wpb_ref[...], preferred_element_type=jnp.float32)
        return h, gates, pa, pb

    def mix(h, gates, pa, pb):
        mixin = jax.nn.sigmoid(gates[:, :D_MODEL]) * pa + jax.nn.sigmoid(gates[:, D_MODEL:]) * pb
        return DEEPNORM_ALPHA * h + jnp.dot(mixin.astype(jnp.bfloat16), wo_ref[...],
                                            preferred_element_type=jnp.float32)

    def norm_and_logits(pre, rows):
        h1 = _layer_norm(pre, l1g_ref[...], l1b_ref[...])
        h1_ref[rows] = h1
        h1p_ref[rows] = _pack_rows(h1)
        hi = h1.astype(jnp.bfloat16)
        lo = (h1 - hi.astype(jnp.float32)).astype(jnp.bfloat16)
        lhs = jnp.concatenate([hi, lo, hi], axis=1)
        return lax.dot_general(wr_ref[...], lhs, (((1,), (1,)), ((), ())), preferred_element_type=jnp.float32)

    sub = tm // MERGE_SUBTILES
    parts = [slice(k * sub, (k + 1) * sub) for k in range(MERGE_SUBTILES)]
    projected = [project(rows) for rows in parts]
    mixed = [mix(*pr) for pr in projected]
    logits_t = jnp.concatenate([norm_and_logits(pre, rows) for pre, rows in zip(mixed, parts)], axis=1)
    logits_t = logits_t + br_ref[...]
    info, carry = _route(logits_t, carry_ref[...], tri_ref[...])
    info_ref[...] = info
    carry_ref[...] = carry
    cnt_ref[...] = carry[:, :LANES]


def _merge(x2, oa, ob, ln_g, ln_b, w_gates, w_pa, w_pb, w_o, l1g, l1b, w_r, b_r, cnt0, tm):
    n = x2.shape[0]

    def const(shape):
        return pl.BlockSpec(shape, lambda i: (0,) * len(shape))

    def rows(width):
        return pl.BlockSpec((tm, width), lambda i: (i, 0))

    return pl.pallas_call(
        _merge_kernel,
        out_shape=(jax.ShapeDtypeStruct((n, D_MODEL), jnp.float32),
                   jax.ShapeDtypeStruct((n, PACKED_WIDTH), jnp.uint32),
                   jax.ShapeDtypeStruct((INFO_ROWS, n), jnp.float32),
                   jax.ShapeDtypeStruct((ROUTE_ROWS, LANES), jnp.float32)),
        grid=(n // tm,),
        in_specs=[
            rows(D_MODEL), rows(WIDTH_A), rows(WIDTH_B),
            const((1, D_MODEL)), const((1, D_MODEL)),
            const((D_MODEL, 2 * D_MODEL)),
            const((WIDTH_A, D_MODEL)), const((WIDTH_B, D_MODEL)),
            const((D_MODEL, D_MODEL)),
            const((1, D_MODEL)), const((1, D_MODEL)),
            const((ROUTE_ROWS, 3 * D_MODEL)), const((ROUTE_ROWS, tm)), const((ROUTE_ROWS, tm)),
        ],
        out_specs=(rows(D_MODEL), rows(PACKED_WIDTH), pl.BlockSpec((INFO_ROWS, tm), lambda i: (0, i)),
                   const((ROUTE_ROWS, LANES))),
        scratch_shapes=[pltpu.VMEM((ROUTE_ROWS, tm), jnp.float32), pltpu.VMEM((tm, tm), jnp.bfloat16)],
        compiler_params=_cparams("arbitrary"),
        name="merge",
    )(x2, oa, ob, ln_g, ln_b, w_gates, w_pa, w_pb, w_o, l1g, l1b, w_r, b_r, cnt0)


SC_CORES = 2
SC_SUBCORES = 16
SC_WORKERS = SC_CORES * SC_SUBCORES
SC_ROWS_PER_STREAM = 64


def _sc_worker():
    return lax.axis_index("s") * SC_CORES + lax.axis_index("c")


def _dispatch(h1p, dest_t):
    n, width = h1p.shape
    per_worker = n // SC_WORKERS
    nchunks = per_worker // SC_ROWS_PER_STREAM
    assert nchunks * SC_ROWS_PER_STREAM * SC_WORKERS == n and nchunks % 2 == 0
    idx = dest_t.reshape(TOP_K, SC_WORKERS, nchunks, SC_ROWS_PER_STREAM)
    mesh = plsc.VectorSubcoreMesh(core_axis_name="c", subcore_axis_name="s")

    @functools.partial(
        pl.kernel, out_type=jax.ShapeDtypeStruct((TOP_K * n, width), h1p.dtype), mesh=mesh,
        scratch_types=[pltpu.VMEM((TOP_K, nchunks, SC_ROWS_PER_STREAM), jnp.int32),
                       pltpu.VMEM((2, SC_ROWS_PER_STREAM, width), h1p.dtype),
                       pltpu.SemaphoreType.DMA((2,)), pltpu.SemaphoreType.DMA((2,))],
        name="sc_dispatch")
    def scatter_kernel(src_hbm, idx_hbm, out_hbm, idx_v, rows_v, rsem, ssem):
        wid = _sc_worker()
        base = wid * per_worker
        for k in range(TOP_K):
            pltpu.sync_copy(idx_hbm.at[k, wid], idx_v.at[k])

        def read(j, slot):
            src = src_hbm.at[pl.ds(base + j * SC_ROWS_PER_STREAM, SC_ROWS_PER_STREAM)]
            return pltpu.make_async_copy(src, rows_v.at[slot], rsem.at[slot])

        def scatter(j, slot, k):
            return pltpu.make_async_copy(rows_v.at[slot], out_hbm.at[idx_v.at[k, j]], ssem.at[slot])

        read(0, 0).start()

        @pl.loop(0, nchunks, step=2)
        def _(j0):
            for slot in range(2):
                j = j0 + slot
                read(j, slot).wait()

                @pl.when(j + 1 < nchunks)
                def _():
                    @pl.when(j >= 1)
                    def _():
                        for k in range(TOP_K):
                            scatter(j - 1, 1 - slot, k).wait()
                    read(j + 1, 1 - slot).start()

                for k in range(TOP_K):
                    scatter(j, slot, k).start()

        for k in range(TOP_K):
            scatter(nchunks - 2, 0, k).wait()
            scatter(nchunks - 1, 1, k).wait()

    return scatter_kernel(h1p, idx)


def _moe_plan(counts, na, bm):
    expert = jnp.arange(N_EXPERTS, dtype=jnp.int32)
    upto = expert[None, :] <= expert[:, None]

    def running_sum(v):
        return jnp.sum(jnp.where(upto, v[None, :], 0), axis=1)

    ends = running_sum(counts)
    starts = ends - counts
    first_blk = starts // bm
    tiles = jnp.where(counts > 0, (ends - 1) // bm - first_blk + 1, 0)
    item_end = running_sum(tiles)
    item_start = item_end - tiles
    total = jnp.sum(tiles)
    wmax = na // bm + N_EXPERTS - 1
    w = jnp.arange(wmax, dtype=jnp.int32)
    wc = jnp.minimum(w, total - 1)
    e = jnp.sum((item_end[None, :] <= wc[:, None]).astype(jnp.int32), axis=1)
    e = jnp.minimum(e, N_EXPERTS - 1)
    owner = e[:, None] == expert[None, :]

    def of_owner(table):
        return jnp.sum(jnp.where(owner, table[None, :], 0), axis=1)

    blk = (of_owner(first_blk) + (wc - of_owner(item_start))).astype(jnp.int32)
    valid = w < total
    lo = jnp.where(valid, jnp.maximum(of_owner(starts), blk * bm), 0).astype(jnp.int32)
    hi = jnp.where(valid, jnp.minimum(of_owner(ends), (blk + 1) * bm), 0).astype(jnp.int32)
    prev_blk = jnp.concatenate([jnp.full((1,), -1, jnp.int32), blk[:-1]])
    prev_e = jnp.concatenate([jnp.full((1,), -1, jnp.int32), e[:-1]])
    flags = (valid.astype(jnp.int32)
             + 2 * (valid & (blk != prev_blk)).astype(jnp.int32)
             + 4 * (valid & (e != prev_e)).astype(jnp.int32))
    return blk, e, lo, hi, flags


FLAG_VALID, FLAG_NEW_BLOCK, FLAG_NEW_EXPERT = 1, 2, 4


def _expert_kernel(blk_ref, e_ref, lo_ref, hi_ref, flag_ref, x_ref, wg_ref, wu_ref, wd_ref, o_ref,
                   wg_b, wu_b, wd_b):
    w = pl.program_id(0)
    bm = x_ref.shape[0]
    flags = flag_ref[w]

    @pl.when((flags & FLAG_NEW_EXPERT) != 0)
    def _():
        wg_b[...] = wg_ref[0].astype(jnp.bfloat16)
        wu_b[...] = wu_ref[0].astype(jnp.bfloat16)
        wd_b[...] = wd_ref[0].astype(jnp.bfloat16)

    @pl.when((flags & FLAG_VALID) != 0)
    def _():
        x = _unpack_rows(x_ref[...]).astype(jnp.bfloat16)
        g = jnp.dot(x, wg_b[...], preferred_element_type=jnp.float32)
        u = jnp.dot(x, wu_b[...], preferred_element_type=jnp.float32)
        hmid = (jax.nn.silu(g) * u).astype(jnp.bfloat16)
        y = _pack_rows(jnp.dot(hmid, wd_b[...], preferred_element_type=jnp.float32))
        row = blk_ref[w] * bm + lax.broadcasted_iota(jnp.int32, (bm, 1), 0)
        mine = (row >= lo_ref[w]) & (row < hi_ref[w])

        @pl.when((flags & FLAG_NEW_BLOCK) != 0)
        def _():
            o_ref[...] = jnp.where(mine, y, jnp.uint32(0))

        @pl.when((flags & FLAG_NEW_BLOCK) == 0)
        def _():
            o_ref[...] = jnp.where(mine, y, o_ref[...])


def _experts(xs, plan, w_gate, w_up, w_down, bm):
    na = xs.shape[0]
    blk, e, lo, hi, flags = plan
    nitems = blk.shape[0]
    grid_spec = pltpu.PrefetchScalarGridSpec(
        num_scalar_prefetch=5,
        grid=(nitems,),
        in_specs=[
            pl.BlockSpec((bm, PACKED_WIDTH), lambda w, blk, e, *_: (blk[w], 0)),
            pl.BlockSpec((1, D_MODEL, D_EXPERT), lambda w, blk, e, *_: (e[w], 0, 0)),
            pl.BlockSpec((1, D_MODEL, D_EXPERT), lambda w, blk, e, *_: (e[w], 0, 0)),
            pl.BlockSpec((1, D_EXPERT, D_MODEL), lambda w, blk, e, *_: (e[w], 0, 0)),
        ],
        out_specs=pl.BlockSpec((bm, PACKED_WIDTH), lambda w, blk, e, *_: (blk[w], 0)),
        scratch_shapes=[pltpu.VMEM((D_MODEL, D_EXPERT), jnp.bfloat16),
                        pltpu.VMEM((D_MODEL, D_EXPERT), jnp.bfloat16),
                        pltpu.VMEM((D_EXPERT, D_MODEL), jnp.bfloat16)],
    )
    return pl.pallas_call(
        _expert_kernel,
        out_shape=jax.ShapeDtypeStruct((na, PACKED_WIDTH), jnp.uint32),
        grid_spec=grid_spec,
        compiler_params=_cparams("arbitrary"),
        name="experts",
    )(blk, e, lo, hi, flags, xs, w_gate, w_up, w_down)


def _sc_gather_rows(table, idx):
    nrows = idx.shape[0]
    width = table.shape[1]
    per_worker = nrows // SC_WORKERS
    nchunks = per_worker // SC_ROWS_PER_STREAM
    assert nchunks * SC_ROWS_PER_STREAM * SC_WORKERS == nrows and nchunks % 2 == 0
    mesh = plsc.VectorSubcoreMesh(core_axis_name="c", subcore_axis_name="s")

    @functools.partial(
        pl.kernel, out_type=jax.ShapeDtypeStruct((nrows, width), table.dtype), mesh=mesh,
        scratch_types=[pltpu.VMEM((per_worker,), jnp.int32),
                       pltpu.VMEM((2, SC_ROWS_PER_STREAM, width), table.dtype),
                       pltpu.SemaphoreType.DMA((2,)), pltpu.SemaphoreType.DMA((2,))],
        name="sc_gather")
    def gather_kernel(table_hbm, idx_hbm, out_hbm, idx_v, rows_v, gsem, wsem):
        base = _sc_worker() * per_worker
        pltpu.sync_copy(idx_hbm.at[pl.ds(base, per_worker)], idx_v)

        def gather(j, slot):
            rows = idx_v.at[pl.ds(j * SC_ROWS_PER_STREAM, SC_ROWS_PER_STREAM)]
            return pltpu.make_async_copy(table_hbm.at[rows], rows_v.at[slot], gsem.at[slot])

        def write(j, slot):
            dst = out_hbm.at[pl.ds(base + j * SC_ROWS_PER_STREAM, SC_ROWS_PER_STREAM)]
            return pltpu.make_async_copy(rows_v.at[slot], dst, wsem.at[slot])

        gather(0, 0).start()

        @pl.loop(0, nchunks, step=2)
        def _(j0):
            for slot in range(2):
                j = j0 + slot
                gather(j, slot).wait()

                @pl.when(j + 1 < nchunks)
                def _():
                    @pl.when(j >= 1)
                    def _():
                        write(j - 1, 1 - slot).wait()
                    gather(j + 1, 1 - slot).start()

                write(j, slot).start()

        write(nchunks - 2, 0).wait()
        write(nchunks - 1, 1).wait()

    return gather_kernel(table, idx)


def _finalize_kernel(h1_ref, y1_ref, y2_ref, info_ref, g_ref, b_ref, o_ref):
    tt = h1_ref.shape[0]
    pad = jnp.zeros((LANES - INFO_ROWS, tt), jnp.float32)
    info = jnp.concatenate([info_ref[...], pad], axis=0).T
    moe = (_unpack_rows(y1_ref[...]) * info[:, INFO_W1:INFO_W1 + 1]
           + _unpack_rows(y2_ref[...]) * info[:, INFO_W2:INFO_W2 + 1])
    o_ref[...] = _layer_norm(DEEPNORM_ALPHA * h1_ref[...] + moe, g_ref[...], b_ref[...])


def _combine(h1, info, dest_t, ys, ln_g, ln_b, tt):
    n = h1.shape[0]
    nsteps = n // tt
    yg = _sc_gather_rows(ys, dest_t.reshape(TOP_K * n))
    return pl.pallas_call(
        _finalize_kernel,
        out_shape=jax.ShapeDtypeStruct((n, D_MODEL), jnp.float32),
        grid=(nsteps,),
        in_specs=[
            pl.BlockSpec((tt, D_MODEL), lambda i: (i, 0)),
            pl.BlockSpec((tt, PACKED_WIDTH), lambda i: (i, 0)),
            pl.BlockSpec((tt, PACKED_WIDTH), lambda i: (nsteps + i, 0)),
            pl.BlockSpec((INFO_ROWS, tt), lambda i: (0, i)),
            pl.BlockSpec((1, D_MODEL), lambda i: (0, 0)),
            pl.BlockSpec((1, D_MODEL), lambda i: (0, 0)),
        ],
        out_specs=pl.BlockSpec((tt, D_MODEL), lambda i: (i, 0)),
        compiler_params=_cparams("parallel"),
        name="finalize",
    )(h1, yg, yg, info, ln_g, ln_b)


TM_QKV = 1024
TQ_WIN = 512
TM_MERGE = 1024
TT_ROWS = 1024
BM_EXPERT = 512


def _prepare_weights(ln_in_g, ln_in_b, w_in, attn_sink, rel_pos_bias, w_proj_a, w_proj_b, w_out,
                     ln1_g, ln1_b, w_route_group, b_route_group, w_route_expert, b_route_expert,
                     ln2_g, ln2_b):
    bf = jnp.bfloat16
    w = w_in[0]
    splits = np.cumsum([WIDTH_A, KV_WIDTH_A, KV_WIDTH_A, WIDTH_B, WIDTH_B, WIDTH_B, D_MODEL])
    wqa, wka, wva, wqb, wkb, wvb, wga, wgb = jnp.split(w, [int(s) for s in splits], axis=1)
    wqa = (wqa.reshape(D_MODEL, N_KV_HEADS_A, GQA_GROUP, HEAD_DIM).transpose(0, 2, 1, 3)
           .reshape(D_MODEL, WIDTH_A))
    w_qkv = jnp.concatenate([wqa, wqb, wkb, wvb, wka, wva], axis=1).astype(bf)
    w_gates = jnp.concatenate([wga, wgb], axis=1).astype(bf)
    w_pa = (w_proj_a[0].reshape(N_KV_HEADS_A, GQA_GROUP, HEAD_DIM, D_MODEL).transpose(1, 0, 2, 3)
            .reshape(WIDTH_A, D_MODEL).astype(bf))
    w_pb = w_proj_b[0].astype(bf)
    w_o = w_out[0].astype(bf)
    pad = ROUTE_ROWS - N_GROUPS - N_EXPERTS
    w_r = jnp.concatenate([w_route_group[0].T, w_route_expert[0].T, jnp.zeros((pad, D_MODEL), jnp.float32)], axis=0)
    w_r_hi = w_r.astype(bf)
    w_r_lo = (w_r - w_r_hi.astype(jnp.float32)).astype(bf)
    w_r3 = jnp.concatenate([w_r_hi, w_r_hi, w_r_lo], axis=1)
    b_r = jnp.concatenate([b_route_group[0], b_route_expert[0], jnp.zeros((pad,), jnp.float32)])
    b_r = jnp.broadcast_to(b_r[:, None], (ROUTE_ROWS, TM_MERGE))
    row = lambda v: v.reshape(1, D_MODEL)
    return dict(
        ln_in_g=row(ln_in_g), ln_in_b=row(ln_in_b), w_qkv=w_qkv, w_gates=w_gates,
        sink=attn_sink[0].astype(jnp.float32), nat_bias=_nat_bias_table(rel_pos_bias[0]),
        w_pa=w_pa, w_pb=w_pb, w_o=w_o, ln1_g=row(ln1_g[0]), ln1_b=row(ln1_b[0]),
        w_r3=w_r3, b_r=b_r, ln2_g=row(ln2_g[0]), ln2_b=row(ln2_b[0]))


def _after(value, other):
    if other is None:
        return value
    other = other.astype(jnp.float32)
    zero = jnp.where(jnp.isfinite(other), other, 0.0) * 0.0
    return value + zero.astype(value.dtype)


def _attend_and_route(x, p, after=None):
    bsz, t, _ = x.shape
    n = bsz * t
    x2 = x.reshape(n, D_MODEL)
    qkv = _qkv(x2, p["ln_in_g"], p["ln_in_b"], p["w_qkv"], TM_QKV)
    oa = _win_attention(qkv, p["sink"], bsz, t, TQ_WIN)
    ob = _nat_attention(qkv, p["nat_bias"], bsz, t)
    cnt0 = _after(jnp.zeros((ROUTE_ROWS, TM_MERGE), jnp.float32), after)
    h1, h1p, info, cnt = _merge(x2, oa, ob, p["ln_in_g"], p["ln_in_b"], p["w_gates"], p["w_pa"], p["w_pb"],
                                p["w_o"], p["ln1_g"], p["ln1_b"], p["w_r3"], p["b_r"], cnt0, TM_MERGE)
    counts = cnt[EXPERT_ROW0:EXPERT_ROW0 + N_EXPERTS, 0].astype(jnp.int32)
    expert = jnp.arange(N_EXPERTS, dtype=jnp.int32)
    starts = jnp.sum(jnp.where(expert[None, :] < expert[:, None], counts[None, :], 0), axis=1)
    eid = info[INFO_E1:INFO_E2 + 1].astype(jnp.int32)
    rank = info[INFO_R1:INFO_R2 + 1].astype(jnp.int32)
    dest_t = rank + jnp.sum(jnp.where(eid[None] == expert[:, None, None], starts[:, None, None], 0), axis=0)
    return dict(shape=x.shape, h1=h1, h1p=h1p, info=info, counts=counts, dest_t=dest_t)


def _run_experts(r, w_gate, w_up, w_down, after=None):
    n = r["h1"].shape[0]
    xs = _dispatch(r["h1p"], r["dest_t"])
    blk, e, lo, hi, flags = _moe_plan(r["counts"], TOP_K * n, BM_EXPERT)
    plan = (blk, e, lo, hi, _after(flags, after))
    return _experts(xs, plan, w_gate[0], w_up[0], w_down[0], BM_EXPERT)


def _finish(r, ys, p, after=None):
    out = _combine(r["h1"], r["info"], r["dest_t"], ys, _after(p["ln2_g"], after), p["ln2_b"], TT_ROWS)
    return out.reshape(r["shape"])


def kernel(x_prompt, x_sample, ln_in_g, ln_in_b, w_in, attn_sink, rel_pos_bias, w_proj_a, w_proj_b, w_out,
           ln1_g, ln1_b, w_route_group, b_route_group, w_route_expert, b_route_expert,
           w_gate, w_up, w_down, ln2_g, ln2_b):
    p = _prepare_weights(ln_in_g, ln_in_b, w_in, attn_sink, rel_pos_bias, w_proj_a, w_proj_b, w_out,
                         ln1_g, ln1_b, w_route_group, b_route_group, w_route_expert, b_route_expert,
                         ln2_g, ln2_b)
    rp = _attend_and_route(x_prompt, p)
    rs = _attend_and_route(x_sample, p, after=rp["counts"][0])
    ys_p = _run_experts(rp, w_gate, w_up, w_down)
    ys_s = _run_experts(rs, w_gate, w_up, w_down, after=ys_p[0, 0])
    y_prompt = _finish(rp, ys_p, p)
    y_sample = _finish(rs, ys_s, p, after=y_prompt[0, 0, 0])
    return (y_prompt, y_sample)
```

```python
import functools

import numpy as np
import jax
import jax.numpy as jnp
from jax import lax
from jax.experimental import pallas as pl
from jax.experimental.pallas import tpu as pltpu
from jax.experimental.pallas import tpu_sc as plsc

D_MODEL = 1024
HEAD_DIM = 64
N_HEADS_A = 8
N_KV_HEADS_A = 2
WINDOW = 128
N_HEADS_B = 8
GRID_W = 64
NA_ROWS = 8
NA_COLS = 16
N_GROUPS = 4
EXPERTS_PER_GROUP = 8
N_EXPERTS = N_GROUPS * EXPERTS_PER_GROUP
TOP_K = 2
D_EXPERT = D_MODEL // 2
LN_EPS = 1e-5
DEPTH = 1
DEEPNORM_ALPHA = (2.0 * DEPTH) ** 0.25
WIDTH_A = N_HEADS_A * HEAD_DIM
KV_WIDTH_A = N_KV_HEADS_A * HEAD_DIM
WIDTH_B = N_HEADS_B * HEAD_DIM
QKV_WIDTH = WIDTH_A + 2 * KV_WIDTH_A + 3 * WIDTH_B

LANES = 128
VMEM_LIMIT_BYTES = 56 * 1024 * 1024

NEG_BIG = -1e30

QA_COL, QB_COL, KB_COL, VB_COL = 0, WIDTH_A, WIDTH_A + WIDTH_B, WIDTH_A + 2 * WIDTH_B
KA_COL = WIDTH_A + 3 * WIDTH_B
VA_COL = KA_COL + KV_WIDTH_A

GQA_GROUP = N_HEADS_A // N_KV_HEADS_A


def _cparams(*sem):
    return pltpu.CompilerParams(dimension_semantics=sem, vmem_limit_bytes=VMEM_LIMIT_BYTES)


def _layer_norm(x, g, b):
    mu = jnp.mean(x, axis=-1, keepdims=True)
    xc = x - mu
    var = jnp.mean(xc * xc, axis=-1, keepdims=True)
    return xc * lax.rsqrt(var + LN_EPS) * g + b


PACKED_WIDTH = D_MODEL // 2


def _pack_rows(x):
    def rne(v):
        return v + jnp.uint32(0x7FFF) + ((v >> 16) & jnp.uint32(1))
    hi = lax.bitcast_convert_type(x[:, :PACKED_WIDTH], jnp.uint32)
    lo = lax.bitcast_convert_type(x[:, PACKED_WIDTH:], jnp.uint32)
    return (rne(hi) & jnp.uint32(0xFFFF0000)) | (rne(lo) >> 16)


def _unpack_rows(w):
    hi = lax.bitcast_convert_type(w & jnp.uint32(0xFFFF0000), jnp.float32)
    lo = lax.bitcast_convert_type(w << 16, jnp.float32)
    return jnp.concatenate([hi, lo], axis=1)


def _qkv_kernel(x_ref, g_ref, b_ref, w_ref, o_ref):
    h = _layer_norm(x_ref[...], g_ref[...], b_ref[...])
    y = jnp.dot(h.astype(jnp.bfloat16), w_ref[...], preferred_element_type=jnp.float32)
    col = lax.broadcasted_iota(jnp.int32, (1, QKV_WIDTH), 1)
    y = y * jnp.where(col < KB_COL, HEAD_DIM ** -0.5, 1.0)
    o_ref[...] = y.astype(jnp.bfloat16)


def _qkv(x2, ln_g, ln_b, w_qkv, tm):
    n = x2.shape[0]
    return pl.pallas_call(
        _qkv_kernel,
        out_shape=jax.ShapeDtypeStruct((n, QKV_WIDTH), jnp.bfloat16),
        grid=(n // tm,),
        in_specs=[
            pl.BlockSpec((tm, D_MODEL), lambda i: (i, 0)),
            pl.BlockSpec((1, D_MODEL), lambda i: (0, 0)),
            pl.BlockSpec((1, D_MODEL), lambda i: (0, 0)),
            pl.BlockSpec((D_MODEL, QKV_WIDTH), lambda i: (0, 0)),
        ],
        out_specs=pl.BlockSpec((tm, QKV_WIDTH), lambda i: (i, 0)),
        compiler_params=_cparams("parallel"),
        name="qkv",
    )(x2, ln_g, ln_b, w_qkv)


WIN_BLK = 128
WIN_LOOKAHEAD = 2


def _win_bias_table():
    qi = np.arange(WIN_BLK)[:, None]
    kj = np.arange(3 * WIN_BLK)[None, :]
    dist = np.abs(kj - WIN_BLK - qi).astype(np.float64)
    slopes = 2.0 ** (-8.0 * np.arange(1, N_HEADS_A + 1) / N_HEADS_A)
    per_head = np.where(dist <= WINDOW, -slopes[:, None, None] * dist[None], NEG_BIG)
    groups = [np.concatenate([per_head[j], per_head[j + 4]], axis=0) for j in range(4)]
    return np.stack(groups).astype(np.float32)


def _win_kernel(sink_ref, q_ref, kp_ref, km_ref, kn_ref, vp_ref, vm_ref, vn_ref, bias_ref, o_ref,
                *, nsub, nblk_seq):
    i = pl.program_id(1)
    kcat = jnp.concatenate([kp_ref[...], km_ref[...], kn_ref[...]], axis=0)
    vcat = jnp.concatenate([vp_ref[...], vm_ref[...], vn_ref[...]], axis=0)
    lo = lax.broadcasted_iota(jnp.int32, (1, LANES), 1) < HEAD_DIM
    col = lax.broadcasted_iota(jnp.int32, (1, 3 * WIN_BLK), 1)
    top = lax.broadcasted_iota(jnp.int32, (2 * WIN_BLK, 1), 0) < WIN_BLK
    zero = jnp.zeros((), jnp.bfloat16)

    def scores(j, g):
        n = i * nsub + j
        off_seq = ((col < WIN_BLK) & (n == 0)) | ((col >= 2 * WIN_BLK) & (n == nblk_seq - 1))
        edge = jnp.where(off_seq, NEG_BIG, 0.0)
        qg = q_ref[WIN_BLK * j:WIN_BLK * (j + 1), LANES * g:LANES * (g + 1)]
        qm = jnp.concatenate([jnp.where(lo, qg, zero), jnp.where(lo, zero, qg)], axis=0)
        kj = kcat[WIN_BLK * j:WIN_BLK * (j + 3)]
        s = lax.dot_general(qm, kj, (((1,), (1,)), ((), ())), preferred_element_type=jnp.float32)
        return s + bias_ref[g] + edge

    def attend(s, j, g):
        vj = vcat[WIN_BLK * j:WIN_BLK * (j + 3)]
        sink = jnp.where(top, sink_ref[g], sink_ref[g + 4])
        m = jnp.maximum(jnp.max(s, axis=-1, keepdims=True), sink)
        p = jnp.exp(s - m)
        l = jnp.sum(p, axis=-1, keepdims=True) + jnp.exp(sink - m)
        o2 = jnp.dot(p.astype(jnp.bfloat16), vj, preferred_element_type=jnp.float32)
        o2 = o2 * (1.0 / l)
        o_ref[WIN_BLK * j:WIN_BLK * (j + 1), LANES * g:LANES * (g + 1)] = (
            jnp.where(lo, o2[:WIN_BLK], o2[WIN_BLK:]).astype(jnp.bfloat16))

    chains = [(j, g) for j in range(nsub) for g in range(4)]
    pending = [scores(*c) for c in chains[:WIN_LOOKAHEAD]]
    for idx, c in enumerate(chains):
        s = pending.pop(0)
        if idx + WIN_LOOKAHEAD < len(chains):
            pending.append(scores(*chains[idx + WIN_LOOKAHEAD]))
        attend(s, *c)


def _win_attention(qkv, sink, bsz, t, tq):
    n = bsz * t
    nsub = tq // WIN_BLK
    nblk_seq = t // WIN_BLK
    ntile = t // tq
    bias = jnp.asarray(_win_bias_table())

    def main_map(col):
        return lambda b, i, *_: (b * ntile + i, col)

    def prev_map(col):
        return lambda b, i, *_: (b * nblk_seq + jnp.maximum(i * nsub - 1, 0), col)

    def next_map(col):
        return lambda b, i, *_: (b * nblk_seq + jnp.minimum(i * nsub + nsub, nblk_seq - 1), col)

    halo = (WIN_BLK, LANES)
    ka, va = KA_COL // LANES, VA_COL // LANES
    grid_spec = pltpu.PrefetchScalarGridSpec(
        num_scalar_prefetch=1,
        grid=(bsz, ntile),
        in_specs=[
            pl.BlockSpec((tq, WIDTH_A), main_map(QA_COL // WIDTH_A)),
            pl.BlockSpec(halo, prev_map(ka)),
            pl.BlockSpec((tq, LANES), main_map(ka)),
            pl.BlockSpec(halo, next_map(ka)),
            pl.BlockSpec(halo, prev_map(va)),
            pl.BlockSpec((tq, LANES), main_map(va)),
            pl.BlockSpec(halo, next_map(va)),
            pl.BlockSpec((4, 2 * WIN_BLK, 3 * WIN_BLK), lambda b, i, *_: (0, 0, 0)),
        ],
        out_specs=pl.BlockSpec((tq, WIDTH_A), main_map(0)),
    )
    return pl.pallas_call(
        functools.partial(_win_kernel, nsub=nsub, nblk_seq=nblk_seq),
        out_shape=jax.ShapeDtypeStruct((n, WIDTH_A), jnp.bfloat16),
        grid_spec=grid_spec,
        compiler_params=_cparams("parallel", "parallel"),
        name="win",
    )(sink, qkv, qkv, qkv, qkv, qkv, qkv, qkv, bias)


NAT_ROWS_PER_STEP = 8
NAT_HALO_ROWS = NA_ROWS // 2
NAT_KEYS = NA_ROWS * GRID_W
NAT_ROWS_PER_TRIP = 8
NAT_LOOKAHEAD = 4


def _nat_bias_table(rpb):
    c = np.arange(GRID_W)
    cs = np.clip(c - NA_COLS // 2, 0, GRID_W - NA_COLS)
    col_mask = (c[None, :] >= cs[:, None]) & (c[None, :] < cs[:, None] + NA_COLS)
    dc = np.clip(c[None, :] - c[:, None] + (NA_COLS - 1), 0, 2 * NA_COLS - 2)
    onehot = jnp.asarray(dc[None] == np.arange(2 * NA_COLS - 1)[:, None, None], jnp.float32)
    picked = jnp.einsum("hdj,jqc->hdqc", rpb, onehot, precision=lax.Precision.HIGHEST)
    t1 = jnp.where(col_mask[None, None], picked, NEG_BIG)
    per_shift = []
    for sh in range(NA_ROWS):
        w = t1[:, sh:sh + NA_ROWS]
        w = jnp.transpose(w, (0, 2, 1, 3)).reshape(N_HEADS_B // 2, 2 * GRID_W, NAT_KEYS)
        per_shift.append(w)
    return jnp.stack(per_shift, axis=1).astype(jnp.float32)


def _nat_kernel(q_ref, kp_ref, km_ref, kn_ref, vp_ref, vm_ref, vn_ref, tb_ref, o_ref, kcat, vcat,
                *, rows_seq):
    i = pl.program_id(1)
    halo = NAT_HALO_ROWS * GRID_W
    main = NAT_ROWS_PER_STEP * GRID_W
    kcat[0:halo] = kp_ref[...]
    kcat[halo:halo + main] = km_ref[...]
    kcat[halo + main:2 * halo + main] = kn_ref[...]
    vcat[0:halo] = vp_ref[...]
    vcat[halo:halo + main] = vm_ref[...]
    vcat[halo + main:2 * halo + main] = vn_ref[...]
    lo = lax.broadcasted_iota(jnp.int32, (1, LANES), 1) < HEAD_DIM
    zero = jnp.zeros((), jnp.bfloat16)
    r0 = i * NAT_ROWS_PER_STEP

    def scores(qr, p):
        r = r0 + qr
        rs = jnp.clip(r - NA_ROWS // 2, 0, rows_seq - NA_ROWS)
        koff = pl.multiple_of((rs - r0 + NAT_HALO_ROWS) * GRID_W, GRID_W)
        sh = rs - r + (NA_ROWS - 1)
        qoff = pl.multiple_of(qr * GRID_W, GRID_W)
        cols = slice(LANES * p, LANES * (p + 1))
        qp = q_ref[pl.ds(qoff, GRID_W), cols]
        qm = jnp.concatenate([jnp.where(lo, qp, zero), jnp.where(lo, zero, qp)], axis=0)
        kw = kcat[pl.ds(koff, NAT_KEYS), cols]
        s = lax.dot_general(qm, kw, (((1,), (1,)), ((), ())), preferred_element_type=jnp.float32)
        return s + tb_ref[p, sh], koff, qoff

    def attend(s, koff, qoff, p):
        cols = slice(LANES * p, LANES * (p + 1))
        vw = vcat[pl.ds(koff, NAT_KEYS), cols]
        m = jnp.max(s, axis=-1, keepdims=True)
        pe = jnp.exp(s - m)
        l = jnp.sum(pe, axis=-1, keepdims=True)
        o2 = jnp.dot(pe.astype(jnp.bfloat16), vw, preferred_element_type=jnp.float32)
        o2 = o2 * (1.0 / l)
        o_ref[pl.ds(qoff, GRID_W), cols] = jnp.where(lo, o2[:GRID_W], o2[GRID_W:]).astype(jnp.bfloat16)

    def trip(j, carry):
        chains = [(j * NAT_ROWS_PER_TRIP + q, p) for q in range(NAT_ROWS_PER_TRIP) for p in range(N_HEADS_B // 2)]
        pending = [scores(*c) for c in chains[:NAT_LOOKAHEAD]]
        for idx, (_, p) in enumerate(chains):
            s, koff, qoff = pending.pop(0)
            if idx + NAT_LOOKAHEAD < len(chains):
                pending.append(scores(*chains[idx + NAT_LOOKAHEAD]))
            attend(s, koff, qoff, p)
        return carry

    lax.fori_loop(0, NAT_ROWS_PER_STEP // NAT_ROWS_PER_TRIP, trip, 0)


def _nat_attention(qkv, tb, bsz, t):
    n = bsz * t
    rows_seq = t // GRID_W
    main = NAT_ROWS_PER_STEP * GRID_W
    halo = NAT_HALO_ROWS * GRID_W
    ntile = t // main
    nhalo_seq = t // halo
    per = main // halo

    def main_map(col):
        return lambda b, i: (b * ntile + i, col)

    def prev_map(col):
        return lambda b, i: (b * nhalo_seq + jnp.maximum(i * per - 1, 0), col)

    def next_map(col):
        return lambda b, i: (b * nhalo_seq + jnp.minimum(i * per + per, nhalo_seq - 1), col)

    qb, kb, vb = QB_COL // WIDTH_B, KB_COL // WIDTH_B, VB_COL // WIDTH_B
    return pl.pallas_call(
        functools.partial(_nat_kernel, rows_seq=rows_seq),
        out_shape=jax.ShapeDtypeStruct((n, WIDTH_B), jnp.bfloat16),
        grid=(bsz, ntile),
        in_specs=[
            pl.BlockSpec((main, WIDTH_B), main_map(qb)),
            pl.BlockSpec((halo, WIDTH_B), prev_map(kb)),
            pl.BlockSpec((main, WIDTH_B), main_map(kb)),
            pl.BlockSpec((halo, WIDTH_B), next_map(kb)),
            pl.BlockSpec((halo, WIDTH_B), prev_map(vb)),
            pl.BlockSpec((main, WIDTH_B), main_map(vb)),
            pl.BlockSpec((halo, WIDTH_B), next_map(vb)),
            pl.BlockSpec((N_HEADS_B // 2, NA_ROWS, 2 * GRID_W, NAT_KEYS), lambda b, i: (0, 0, 0, 0)),
        ],
        out_specs=pl.BlockSpec((main, WIDTH_B), main_map(0)),
        scratch_shapes=[pltpu.VMEM((main + 2 * halo, WIDTH_B), jnp.bfloat16),
                        pltpu.VMEM((main + 2 * halo, WIDTH_B), jnp.bfloat16)],
        compiler_params=_cparams("parallel", "parallel"),
        name="nat",
    )(qkv, qkv, qkv, qkv, qkv, qkv, qkv, tb)


EXPERT_ROW0 = N_GROUPS
ROUTE_ROWS = 48
INFO_E1, INFO_E2, INFO_R1, INFO_R2, INFO_W1, INFO_W2 = range(6)
INFO_ROWS = 8
MERGE_SUBTILES = 2


def _route(lt, carry, tri):
    rr, tm = lt.shape
    row = lax.broadcasted_iota(jnp.int32, (rr, tm), 0).astype(jnp.float32)
    none = jnp.float32(rr)

    def first_max(sel):
        m = jnp.max(jnp.where(sel, lt, NEG_BIG), axis=0, keepdims=True)
        idx = jnp.min(jnp.where(sel & (lt == m), row, none), axis=0, keepdims=True)
        return m, idx

    is_group = row < N_GROUPS
    mg, g = first_max(is_group)
    pg_sel = 1.0 / jnp.sum(jnp.where(is_group, jnp.exp(jnp.where(is_group, lt, mg) - mg), 0.0),
                           axis=0, keepdims=True)
    row0 = EXPERT_ROW0 + EXPERTS_PER_GROUP * g
    in_group = (row >= row0) & (row < row0 + EXPERTS_PER_GROUP)
    m1, i1 = first_max(in_group)
    m2, i2 = first_max(in_group & (row != i1))
    e2 = jnp.exp(m2 - m1)
    w1 = pg_sel / (1.0 + e2)
    w2 = pg_sel * e2 / (1.0 + e2)

    oh1 = row == i1
    oh2 = row == i2
    both = (oh1 | oh2).astype(jnp.bfloat16)
    before = jnp.dot(both, tri, preferred_element_type=jnp.float32) + carry
    r1 = jnp.sum(jnp.where(oh1, before, 0.0), axis=0, keepdims=True)
    r2 = jnp.sum(jnp.where(oh2, before, 0.0), axis=0, keepdims=True)
    new_carry = carry + jnp.sum(both.astype(jnp.float32), axis=1, keepdims=True)

    field = lax.broadcasted_iota(jnp.int32, (INFO_ROWS, tm), 0)
    info = jnp.zeros((INFO_ROWS, tm), jnp.float32)
    for k, v in ((INFO_E1, i1 - EXPERT_ROW0), (INFO_E2, i2 - EXPERT_ROW0), (INFO_R1, r1), (INFO_R2, r2),
                 (INFO_W1, w1), (INFO_W2, w2)):
        info = jnp.where(field == k, v, info)
    return info, new_carry


def _merge_kernel(x_ref, oa_ref, ob_ref, lng_ref, lnb_ref, wg_ref, wpa_ref, wpb_ref, wo_ref,
                  l1g_ref, l1b_ref, wr_ref, br_ref, cnt0_ref,
                  h1_ref, h1p_ref, info_ref, cnt_ref, carry_ref, tri_ref):
    tm = x_ref.shape[0]

    @pl.when(pl.program_id(0) == 0)
    def _():
        carry_ref[...] = cnt0_ref[...]
        r = lax.broadcasted_iota(jnp.int32, (tm, tm), 0)
        c = lax.broadcasted_iota(jnp.int32, (tm, tm), 1)
        tri_ref[...] = (r < c).astype(jnp.bfloat16)

    def project(rows):
        h = _layer_norm(x_ref[rows], lng_ref[...], lnb_ref[...])
        gates = jnp.dot(h.astype(jnp.bfloat16), wg_ref[...], preferred_element_type=jnp.float32)
        pa = jnp.dot(oa_ref[rows], wpa_ref[...], preferred_element_type=jnp.float32)
        pb = jnp.dot(ob_ref[rows], wpb_ref[...], preferred_element_type=jnp.float32)
        return h, gates, pa, pb

    def mix(h, gates, pa, pb):
        mixin = jax.nn.sigmoid(gates[:, :D_MODEL]) * pa + jax.nn.sigmoid(gates[:, D_MODEL:]) * pb
        return DEEPNORM_ALPHA * h + jnp.dot(mixin.astype(jnp.bfloat16), wo_ref[...],
                                            preferred_element_type=jnp.float32)

    def norm_and_logits(pre, rows):
        h1 = _layer_norm(pre, l1g_ref[...], l1b_ref[...])
        h1_ref[rows] = h1
        h1p_ref[rows] = _pack_rows(h1)
        hi = h1.astype(jnp.bfloat16)
        lo = (h1 - hi.astype(jnp.float32)).astype(jnp.bfloat16)
        lhs = jnp.concatenate([hi, lo, hi], axis=1)
        return lax.dot_general(wr_ref[...], lhs, (((1,), (1,)), ((), ())), preferred_element_type=jnp.float32)

    sub = tm // MERGE_SUBTILES
    parts = [slice(k * sub, (k + 1) * sub) for k in range(MERGE_SUBTILES)]
    projected = [project(rows) for rows in parts]
    mixed = [mix(*pr) for pr in projected]
    logits_t = jnp.concatenate([norm_and_logits(pre, rows) for pre, rows in zip(mixed, parts)], axis=1)
    logits_t = logits_t + br_ref[...]
    info, carry = _route(logits_t, carry_ref[...], tri_ref[...])
    info_ref[...] = info
    carry_ref[...] = carry
    cnt_ref[...] = carry[:, :LANES]


def _merge(x2, oa, ob, ln_g, ln_b, w_gates, w_pa, w_pb, w_o, l1g, l1b, w_r, b_r, cnt0, tm):
    n = x2.shape[0]

    def const(shape):
        return pl.BlockSpec(shape, lambda i: (0,) * len(shape))

    def rows(width):
        return pl.BlockSpec((tm, width), lambda i: (i, 0))

    return pl.pallas_call(
        _merge_kernel,
        out_shape=(jax.ShapeDtypeStruct((n, D_MODEL), jnp.float32),
                   jax.ShapeDtypeStruct((n, PACKED_WIDTH), jnp.uint32),
                   jax.ShapeDtypeStruct((INFO_ROWS, n), jnp.float32),
                   jax.ShapeDtypeStruct((ROUTE_ROWS, LANES), jnp.float32)),
        grid=(n // tm,),
        in_specs=[
            rows(D_MODEL), rows(WIDTH_A), rows(WIDTH_B),
            const((1, D_MODEL)), const((1, D_MODEL)),
            const((D_MODEL, 2 * D_MODEL)),
            const((WIDTH_A, D_MODEL)), const((WIDTH_B, D_MODEL)),
            const((D_MODEL, D_MODEL)),
            const((1, D_MODEL)), const((1, D_MODEL)),
            const((ROUTE_ROWS, 3 * D_MODEL)), const((ROUTE_ROWS, tm)), const((ROUTE_ROWS, tm)),
        ],
        out_specs=(rows(D_MODEL), rows(PACKED_WIDTH), pl.BlockSpec((INFO_ROWS, tm), lambda i: (0, i)),
                   const((ROUTE_ROWS, LANES))),
        scratch_shapes=[pltpu.VMEM((ROUTE_ROWS, tm), jnp.float32), pltpu.VMEM((tm, tm), jnp.bfloat16)],
        compiler_params=_cparams("arbitrary"),
        name="merge",
    )(x2, oa, ob, ln_g, ln_b, w_gates, w_pa, w_pb, w_o, l1g, l1b, w_r, b_r, cnt0)


SC_CORES = 2
SC_SUBCORES = 16
SC_WORKERS = SC_CORES * SC_SUBCORES
SC_ROWS_PER_STREAM = 64


def _sc_worker():
    return lax.axis_index("s") * SC_CORES + lax.axis_index("c")


def _dispatch(h1p, dest_t):
    n, width = h1p.shape
    per_worker = n // SC_WORKERS
    nchunks = per_worker // SC_ROWS_PER_STREAM
    assert nchunks * SC_ROWS_PER_STREAM * SC_WORKERS == n and nchunks % 2 == 0
    idx = dest_t.reshape(TOP_K, SC_WORKERS, nchunks, SC_ROWS_PER_STREAM)
    mesh = plsc.VectorSubcoreMesh(core_axis_name="c", subcore_axis_name="s")

    @functools.partial(
        pl.kernel, out_type=jax.ShapeDtypeStruct((TOP_K * n, width), h1p.dtype), mesh=mesh,
        scratch_types=[pltpu.VMEM((TOP_K, nchunks, SC_ROWS_PER_STREAM), jnp.int32),
                       pltpu.VMEM((2, SC_ROWS_PER_STREAM, width), h1p.dtype),
                       pltpu.SemaphoreType.DMA((2,)), pltpu.SemaphoreType.DMA((2,))],
        name="sc_dispatch")
    def scatter_kernel(src_hbm, idx_hbm, out_hbm, idx_v, rows_v, rsem, ssem):
        wid = _sc_worker()
        base = wid * per_worker
        for k in range(TOP_K):
            pltpu.sync_copy(idx_hbm.at[k, wid], idx_v.at[k])

        def read(j, slot):
            src = src_hbm.at[pl.ds(base + j * SC_ROWS_PER_STREAM, SC_ROWS_PER_STREAM)]
            return pltpu.make_async_copy(src, rows_v.at[slot], rsem.at[slot])

        def scatter(j, slot, k):
            return pltpu.make_async_copy(rows_v.at[slot], out_hbm.at[idx_v.at[k, j]], ssem.at[slot])

        read(0, 0).start()

        @pl.loop(0, nchunks, step=2)
        def _(j0):
            for slot in range(2):
                j = j0 + slot
                read(j, slot).wait()

                @pl.when(j + 1 < nchunks)
                def _():
                    @pl.when(j >= 1)
                    def _():
                        for k in range(TOP_K):
                            scatter(j - 1, 1 - slot, k).wait()
                    read(j + 1, 1 - slot).start()

                for k in range(TOP_K):
                    scatter(j, slot, k).start()

        for k in range(TOP_K):
            scatter(nchunks - 2, 0, k).wait()
            scatter(nchunks - 1, 1, k).wait()

    return scatter_kernel(h1p, idx)


def _moe_plan(counts, na, bm):
    expert = jnp.arange(N_EXPERTS, dtype=jnp.int32)
    upto = expert[None, :] <= expert[:, None]

    def running_sum(v):
        return jnp.sum(jnp.where(upto, v[None, :], 0), axis=1)

    ends = running_sum(counts)
    starts = ends - counts
    first_blk = starts // bm
    tiles = jnp.where(counts > 0, (ends - 1) // bm - first_blk + 1, 0)
    item_end = running_sum(tiles)
    item_start = item_end - tiles
    total = jnp.sum(tiles)
    wmax = na // bm + N_EXPERTS - 1
    w = jnp.arange(wmax, dtype=jnp.int32)
    wc = jnp.minimum(w, total - 1)
    e = jnp.sum((item_end[None, :] <= wc[:, None]).astype(jnp.int32), axis=1)
    e = jnp.minimum(e, N_EXPERTS - 1)
    owner = e[:, None] == expert[None, :]

    def of_owner(table):
        return jnp.sum(jnp.where(owner, table[None, :], 0), axis=1)

    blk = (of_owner(first_blk) + (wc - of_owner(item_start))).astype(jnp.int32)
    valid = w < total
    lo = jnp.where(valid, jnp.maximum(of_owner(starts), blk * bm), 0).astype(jnp.int32)
    hi = jnp.where(valid, jnp.minimum(of_owner(ends), (blk + 1) * bm), 0).astype(jnp.int32)
    prev_blk = jnp.concatenate([jnp.full((1,), -1, jnp.int32), blk[:-1]])
    prev_e = jnp.concatenate([jnp.full((1,), -1, jnp.int32), e[:-1]])
    new_expert = valid & (e != prev_e)
    flags = (valid.astype(jnp.int32)
             + FLAG_NEW_BLOCK * (valid & (blk != prev_blk)).astype(jnp.int32)
             + FLAG_NEW_EXPERT * new_expert.astype(jnp.int32))
    earlier = w[None, :] <= w[:, None]
    slot = (jnp.sum(jnp.where(earlier, new_expert[None, :].astype(jnp.int32), 0), axis=1) + 1) % 2
    later = valid[None, :] & (e[None, :] > e[:, None])
    nxt = jnp.min(jnp.where(later, e[None, :], N_EXPERTS), axis=1)
    nxt = jnp.where(valid & (nxt < N_EXPERTS), nxt, -1)
    return blk, e, lo, hi, flags, slot.astype(jnp.int32), nxt.astype(jnp.int32)


FLAG_VALID, FLAG_NEW_BLOCK, FLAG_NEW_EXPERT = 1, 2, 4


def _expert_kernel(blk_ref, e_ref, lo_ref, hi_ref, flag_ref, slot_ref, nxt_ref, x_ref, wg_hbm, wu_hbm, wd_hbm, o_ref,
                   wg_f, wu_f, wd_f, wg_b, wu_b, wd_b, wsem):
    w = pl.program_id(0)
    bm = x_ref.shape[0]
    flags = flag_ref[w]

    def weight_copies(expert, slot):
        return [pltpu.make_async_copy(hbm.at[expert], buf.at[slot], wsem.at[slot, m])
                for m, (hbm, buf) in enumerate(((wg_hbm, wg_f), (wu_hbm, wu_f), (wd_hbm, wd_f)))]

    @pl.when((flags & FLAG_NEW_EXPERT) != 0)
    def _():
        slot = slot_ref[w]

        @pl.when(w == 0)
        def _():
            for c in weight_copies(e_ref[w], slot):
                c.start()

        for c in weight_copies(e_ref[w], slot):
            c.wait()
        wg_b[...] = wg_f[slot].astype(jnp.bfloat16)
        wu_b[...] = wu_f[slot].astype(jnp.bfloat16)
        wd_b[...] = wd_f[slot].astype(jnp.bfloat16)

        @pl.when(nxt_ref[w] >= 0)
        def _():
            for c in weight_copies(nxt_ref[w], 1 - slot):
                c.start()

    @pl.when((flags & FLAG_VALID) != 0)
    def _():
        x = _unpack_rows(x_ref[...]).astype(jnp.bfloat16)
        g = jnp.dot(x, wg_b[...], preferred_element_type=jnp.float32)
        u = jnp.dot(x, wu_b[...], preferred_element_type=jnp.float32)
        hmid = (jax.nn.silu(g) * u).astype(jnp.bfloat16)
        y = _pack_rows(jnp.dot(hmid, wd_b[...], preferred_element_type=jnp.float32))
        row = blk_ref[w] * bm + lax.broadcasted_iota(jnp.int32, (bm, 1), 0)
        mine = (row >= lo_ref[w]) & (row < hi_ref[w])

        @pl.when((flags & FLAG_NEW_BLOCK) != 0)
        def _():
            o_ref[...] = jnp.where(mine, y, jnp.uint32(0))

        @pl.when((flags & FLAG_NEW_BLOCK) == 0)
        def _():
            o_ref[...] = jnp.where(mine, y, o_ref[...])


def _experts(xs, plan, w_gate, w_up, w_down, bm):
    na = xs.shape[0]
    nitems = plan[0].shape[0]
    up_shape, down_shape = (D_MODEL, D_EXPERT), (D_EXPERT, D_MODEL)
    grid_spec = pltpu.PrefetchScalarGridSpec(
        num_scalar_prefetch=len(plan),
        grid=(nitems,),
        in_specs=[
            pl.BlockSpec((bm, PACKED_WIDTH), lambda w, blk, *_: (blk[w], 0)),
            pl.BlockSpec(memory_space=pl.ANY),
            pl.BlockSpec(memory_space=pl.ANY),
            pl.BlockSpec(memory_space=pl.ANY),
        ],
        out_specs=pl.BlockSpec((bm, PACKED_WIDTH), lambda w, blk, *_: (blk[w], 0)),
        scratch_shapes=[pltpu.VMEM((2,) + up_shape, jnp.float32),
                        pltpu.VMEM((2,) + up_shape, jnp.float32),
                        pltpu.VMEM((2,) + down_shape, jnp.float32),
                        pltpu.VMEM(up_shape, jnp.bfloat16),
                        pltpu.VMEM(up_shape, jnp.bfloat16),
                        pltpu.VMEM(down_shape, jnp.bfloat16),
                        pltpu.SemaphoreType.DMA((2, 3))],
    )
    return pl.pallas_call(
        _expert_kernel,
        out_shape=jax.ShapeDtypeStruct((na, PACKED_WIDTH), jnp.uint32),
        grid_spec=grid_spec,
        compiler_params=_cparams("arbitrary"),
        name="experts",
    )(*plan, xs, w_gate, w_up, w_down)


def _sc_gather_rows(table, idx):
    nrows = idx.shape[0]
    width = table.shape[1]
    per_worker = nrows // SC_WORKERS
    nchunks = per_worker // SC_ROWS_PER_STREAM
    assert nchunks * SC_ROWS_PER_STREAM * SC_WORKERS == nrows and nchunks % 2 == 0
    mesh = plsc.VectorSubcoreMesh(core_axis_name="c", subcore_axis_name="s")

    @functools.partial(
        pl.kernel, out_type=jax.ShapeDtypeStruct((nrows, width), table.dtype), mesh=mesh,
        scratch_types=[pltpu.VMEM((per_worker,), jnp.int32),
                       pltpu.VMEM((2, SC_ROWS_PER_STREAM, width), table.dtype),
                       pltpu.SemaphoreType.DMA((2,)), pltpu.SemaphoreType.DMA((2,))],
        name="sc_gather")
    def gather_kernel(table_hbm, idx_hbm, out_hbm, idx_v, rows_v, gsem, wsem):
        base = _sc_worker() * per_worker
        pltpu.sync_copy(idx_hbm.at[pl.ds(base, per_worker)], idx_v)

        def gather(j, slot):
            rows = idx_v.at[pl.ds(j * SC_ROWS_PER_STREAM, SC_ROWS_PER_STREAM)]
            return pltpu.make_async_copy(table_hbm.at[rows], rows_v.at[slot], gsem.at[slot])

        def write(j, slot):
            dst = out_hbm.at[pl.ds(base + j * SC_ROWS_PER_STREAM, SC_ROWS_PER_STREAM)]
            return pltpu.make_async_copy(rows_v.at[slot], dst, wsem.at[slot])

        gather(0, 0).start()

        @pl.loop(0, nchunks, step=2)
        def _(j0):
            for slot in range(2):
                j = j0 + slot
                gather(j, slot).wait()

                @pl.when(j + 1 < nchunks)
                def _():
                    @pl.when(j >= 1)
                    def _():
                        write(j - 1, 1 - slot).wait()
                    gather(j + 1, 1 - slot).start()

                write(j, slot).start()

        write(nchunks - 2, 0).wait()
        write(nchunks - 1, 1).wait()

    return gather_kernel(table, idx)


def _finalize_kernel(h1_ref, y1_ref, y2_ref, info_ref, g_ref, b_ref, o_ref):
    tt = h1_ref.shape[0]
    pad = jnp.zeros((LANES - INFO_ROWS, tt), jnp.float32)
    info = jnp.concatenate([info_ref[...], pad], axis=0).T
    moe = (_unpack_rows(y1_ref[...]) * info[:, INFO_W1:INFO_W1 + 1]
           + _unpack_rows(y2_ref[...]) * info[:, INFO_W2:INFO_W2 + 1])
    o_ref[...] = _layer_norm(DEEPNORM_ALPHA * h1_ref[...] + moe, g_ref[...], b_ref[...])


def _combine(h1, info, dest_t, ys, ln_g, ln_b, tt):
    n = h1.shape[0]
    nsteps = n // tt
    yg = _sc_gather_rows(ys, dest_t.reshape(TOP_K * n))
    return pl.pallas_call(
        _finalize_kernel,
        out_shape=jax.ShapeDtypeStruct((n, D_MODEL), jnp.float32),
        grid=(nsteps,),
        in_specs=[
            pl.BlockSpec((tt, D_MODEL), lambda i: (i, 0)),
            pl.BlockSpec((tt, PACKED_WIDTH), lambda i: (i, 0)),
            pl.BlockSpec((tt, PACKED_WIDTH), lambda i: (nsteps + i, 0)),
            pl.BlockSpec((INFO_ROWS, tt), lambda i: (0, i)),
            pl.BlockSpec((1, D_MODEL), lambda i: (0, 0)),
            pl.BlockSpec((1, D_MODEL), lambda i: (0, 0)),
        ],
        out_specs=pl.BlockSpec((tt, D_MODEL), lambda i: (i, 0)),
        compiler_params=_cparams("parallel"),
        name="finalize",
    )(h1, yg, yg, info, ln_g, ln_b)


TM_QKV = 1024
TQ_WIN = 512
TM_MERGE = 1024
TT_ROWS = 1024
BM_EXPERT = 512


def _prepare_weights(ln_in_g, ln_in_b, w_in, attn_sink, rel_pos_bias, w_proj_a, w_proj_b, w_out,
                     ln1_g, ln1_b, w_route_group, b_route_group, w_route_expert, b_route_expert,
                     ln2_g, ln2_b):
    bf = jnp.bfloat16
    w = w_in[0]
    splits = np.cumsum([WIDTH_A, KV_WIDTH_A, KV_WIDTH_A, WIDTH_B, WIDTH_B, WIDTH_B, D_MODEL])
    wqa, wka, wva, wqb, wkb, wvb, wga, wgb = jnp.split(w, [int(s) for s in splits], axis=1)
    wqa = (wqa.reshape(D_MODEL, N_KV_HEADS_A, GQA_GROUP, HEAD_DIM).transpose(0, 2, 1, 3)
           .reshape(D_MODEL, WIDTH_A))
    w_qkv = jnp.concatenate([wqa, wqb, wkb, wvb, wka, wva], axis=1).astype(bf)
    w_gates = jnp.concatenate([wga, wgb], axis=1).astype(bf)
    w_pa = (w_proj_a[0].reshape(N_KV_HEADS_A, GQA_GROUP, HEAD_DIM, D_MODEL).transpose(1, 0, 2, 3)
            .reshape(WIDTH_A, D_MODEL).astype(bf))
    w_pb = w_proj_b[0].astype(bf)
    w_o = w_out[0].astype(bf)
    pad = ROUTE_ROWS - N_GROUPS - N_EXPERTS
    w_r = jnp.concatenate([w_route_group[0].T, w_route_expert[0].T, jnp.zeros((pad, D_MODEL), jnp.float32)], axis=0)
    w_r_hi = w_r.astype(bf)
    w_r_lo = (w_r - w_r_hi.astype(jnp.float32)).astype(bf)
    w_r3 = jnp.concatenate([w_r_hi, w_r_hi, w_r_lo], axis=1)
    b_r = jnp.concatenate([b_route_group[0], b_route_expert[0], jnp.zeros((pad,), jnp.float32)])
    b_r = jnp.broadcast_to(b_r[:, None], (ROUTE_ROWS, TM_MERGE))
    row = lambda v: v.reshape(1, D_MODEL)
    return dict(
        ln_in_g=row(ln_in_g), ln_in_b=row(ln_in_b), w_qkv=w_qkv, w_gates=w_gates,
        sink=attn_sink[0].astype(jnp.float32), nat_bias=_nat_bias_table(rel_pos_bias[0]),
        w_pa=w_pa, w_pb=w_pb, w_o=w_o, ln1_g=row(ln1_g[0]), ln1_b=row(ln1_b[0]),
        w_r3=w_r3, b_r=b_r, ln2_g=row(ln2_g[0]), ln2_b=row(ln2_b[0]))


def _after(value, other):
    if other is None:
        return value
    other = other.astype(jnp.float32)
    zero = jnp.where(jnp.isfinite(other), other, 0.0) * 0.0
    return value + zero.astype(value.dtype)


def _attend_and_route(x, p, after=None):
    bsz, t, _ = x.shape
    n = bsz * t
    x2 = x.reshape(n, D_MODEL)
    qkv = _qkv(x2, p["ln_in_g"], p["ln_in_b"], p["w_qkv"], TM_QKV)
    oa = _win_attention(qkv, p["sink"], bsz, t, TQ_WIN)
    ob = _nat_attention(qkv, p["nat_bias"], bsz, t)
    cnt0 = _after(jnp.zeros((ROUTE_ROWS, TM_MERGE), jnp.float32), after)
    h1, h1p, info, cnt = _merge(x2, oa, ob, p["ln_in_g"], p["ln_in_b"], p["w_gates"], p["w_pa"], p["w_pb"],
                                p["w_o"], p["ln1_g"], p["ln1_b"], p["w_r3"], p["b_r"], cnt0, TM_MERGE)
    counts = cnt[EXPERT_ROW0:EXPERT_ROW0 + N_EXPERTS, 0].astype(jnp.int32)
    expert = jnp.arange(N_EXPERTS, dtype=jnp.int32)
    starts = jnp.sum(jnp.where(expert[None, :] < expert[:, None], counts[None, :], 0), axis=1)
    eid = info[INFO_E1:INFO_E2 + 1].astype(jnp.int32)
    rank = info[INFO_R1:INFO_R2 + 1].astype(jnp.int32)
    dest_t = rank + jnp.sum(jnp.where(eid[None] == expert[:, None, None], starts[:, None, None], 0), axis=0)
    return dict(shape=x.shape, h1=h1, h1p=h1p, info=info, counts=counts, dest_t=dest_t)


def _run_experts(r, w_gate, w_up, w_down, after=None):
    n = r["h1"].shape[0]
    xs = _dispatch(r["h1p"], r["dest_t"])
    blk, e, lo, hi, flags, slot, nxt = _moe_plan(r["counts"], TOP_K * n, BM_EXPERT)
    plan = (blk, e, lo, hi, _after(flags, after), slot, nxt)
    return _experts(xs, plan, w_gate[0], w_up[0], w_down[0], BM_EXPERT)


def _finish(r, ys, p, after=None):
    out = _combine(r["h1"], r["info"], r["dest_t"], ys, _after(p["ln2_g"], after), p["ln2_b"], TT_ROWS)
    return out.reshape(r["shape"])


def kernel(x_prompt, x_sample, ln_in_g, ln_in_b, w_in, attn_sink, rel_pos_bias, w_proj_a, w_proj_b, w_out,
           ln1_g, ln1_b, w_route_group, b_route_group, w_route_expert, b_route_expert,
           w_gate, w_up, w_down, ln2_g, ln2_b):
    p = _prepare_weights(ln_in_g, ln_in_b, w_in, attn_sink, rel_pos_bias, w_proj_a, w_proj_b, w_out,
                         ln1_g, ln1_b, w_route_group, b_route_group, w_route_expert, b_route_expert,
                         ln2_g, ln2_b)
    rp = _attend_and_route(x_prompt, p)
    rs = _attend_and_route(x_sample, p, after=rp["counts"][0])
    ys_p = _run_experts(rp, w_gate, w_up, w_down)
    ys_s = _run_experts(rs, w_gate, w_up, w_down, after=ys_p[0, 0])
    y_prompt = _finish(rp, ys_p, p)
    y_sample = _finish(rs, ys_s, p, after=y_prompt[0, 0, 0])
    return (y_prompt, y_sample)
```

```python
import functools

import numpy as np
import jax
import jax.numpy as jnp
from jax import lax
from jax.experimental import pallas as pl
from jax.experimental.pallas import tpu as pltpu
from jax.experimental.pallas import tpu_sc as plsc

D_MODEL = 1024
HEAD_DIM = 64
N_HEADS_A = 8
N_KV_HEADS_A = 2
WINDOW = 128
N_HEADS_B = 8
GRID_W = 64
NA_ROWS = 8
NA_COLS = 16
N_GROUPS = 4
EXPERTS_PER_GROUP = 8
N_EXPERTS = N_GROUPS * EXPERTS_PER_GROUP
TOP_K = 2
D_EXPERT = D_MODEL // 2
LN_EPS = 1e-5
DEPTH = 1
DEEPNORM_ALPHA = (2.0 * DEPTH) ** 0.25
WIDTH_A = N_HEADS_A * HEAD_DIM
KV_WIDTH_A = N_KV_HEADS_A * HEAD_DIM
WIDTH_B = N_HEADS_B * HEAD_DIM
QKV_WIDTH = WIDTH_A + 2 * KV_WIDTH_A + 3 * WIDTH_B

LANES = 128
VMEM_LIMIT_BYTES = 56 * 1024 * 1024

NEG_BIG = -1e30

QA_COL, QB_COL, KB_COL, VB_COL = 0, WIDTH_A, WIDTH_A + WIDTH_B, WIDTH_A + 2 * WIDTH_B
KA_COL = WIDTH_A + 3 * WIDTH_B
VA_COL = KA_COL + KV_WIDTH_A

GQA_GROUP = N_HEADS_A // N_KV_HEADS_A


def _cparams(*sem):
    return pltpu.CompilerParams(dimension_semantics=sem, vmem_limit_bytes=VMEM_LIMIT_BYTES)


def _layer_norm(x, g, b):
    mu = jnp.mean(x, axis=-1, keepdims=True)
    xc = x - mu
    var = jnp.mean(xc * xc, axis=-1, keepdims=True)
    return xc * lax.rsqrt(var + LN_EPS) * g + b


PACKED_WIDTH = D_MODEL // 2


def _pack_rows(x):
    def rne(v):
        return v + jnp.uint32(0x7FFF) + ((v >> 16) & jnp.uint32(1))
    hi = lax.bitcast_convert_type(x[:, :PACKED_WIDTH], jnp.uint32)
    lo = lax.bitcast_convert_type(x[:, PACKED_WIDTH:], jnp.uint32)
    return (rne(hi) & jnp.uint32(0xFFFF0000)) | (rne(lo) >> 16)


def _unpack_rows(w):
    hi = lax.bitcast_convert_type(w & jnp.uint32(0xFFFF0000), jnp.float32)
    lo = lax.bitcast_convert_type(w << 16, jnp.float32)
    return jnp.concatenate([hi, lo], axis=1)


def _qkv_kernel(x_ref, g_ref, b_ref, w_ref, o_ref):
    h = _layer_norm(x_ref[...], g_ref[...], b_ref[...])
    y = jnp.dot(h.astype(jnp.bfloat16), w_ref[...], preferred_element_type=jnp.float32)
    col = lax.broadcasted_iota(jnp.int32, (1, QKV_WIDTH), 1)
    y = y * jnp.where(col < KB_COL, HEAD_DIM ** -0.5, 1.0)
    o_ref[...] = y.astype(jnp.bfloat16)


def _qkv(x2, ln_g, ln_b, w_qkv, tm):
    n = x2.shape[0]
    return pl.pallas_call(
        _qkv_kernel,
        out_shape=jax.ShapeDtypeStruct((n, QKV_WIDTH), jnp.bfloat16),
        grid=(n // tm,),
        in_specs=[
            pl.BlockSpec((tm, D_MODEL), lambda i: (i, 0)),
            pl.BlockSpec((1, D_MODEL), lambda i: (0, 0)),
            pl.BlockSpec((1, D_MODEL), lambda i: (0, 0)),
            pl.BlockSpec((D_MODEL, QKV_WIDTH), lambda i: (0, 0)),
        ],
        out_specs=pl.BlockSpec((tm, QKV_WIDTH), lambda i: (i, 0)),
        compiler_params=_cparams("parallel"),
        name="qkv",
    )(x2, ln_g, ln_b, w_qkv)


WIN_BLK = 128
WIN_LOOKAHEAD = 2


def _win_bias_table():
    qi = np.arange(WIN_BLK)[:, None]
    kj = np.arange(3 * WIN_BLK)[None, :]
    dist = np.abs(kj - WIN_BLK - qi).astype(np.float64)
    slopes = 2.0 ** (-8.0 * np.arange(1, N_HEADS_A + 1) / N_HEADS_A)
    per_head = np.where(dist <= WINDOW, -slopes[:, None, None] * dist[None], NEG_BIG)
    groups = [np.concatenate([per_head[j], per_head[j + 4]], axis=0) for j in range(4)]
    return np.stack(groups).astype(np.float32)


def _win_kernel(sink_ref, q_ref, kp_ref, km_ref, kn_ref, vp_ref, vm_ref, vn_ref, bias_ref, o_ref,
                *, nsub, nblk_seq):
    i = pl.program_id(1)
    kcat = jnp.concatenate([kp_ref[...], km_ref[...], kn_ref[...]], axis=0)
    vcat = jnp.concatenate([vp_ref[...], vm_ref[...], vn_ref[...]], axis=0)
    lo = lax.broadcasted_iota(jnp.int32, (1, LANES), 1) < HEAD_DIM
    col = lax.broadcasted_iota(jnp.int32, (1, 3 * WIN_BLK), 1)
    top = lax.broadcasted_iota(jnp.int32, (2 * WIN_BLK, 1), 0) < WIN_BLK
    zero = jnp.zeros((), jnp.bfloat16)

    def scores(j, g):
        n = i * nsub + j
        off_seq = ((col < WIN_BLK) & (n == 0)) | ((col >= 2 * WIN_BLK) & (n == nblk_seq - 1))
        edge = jnp.where(off_seq, NEG_BIG, 0.0)
        qg = q_ref[WIN_BLK * j:WIN_BLK * (j + 1), LANES * g:LANES * (g + 1)]
        qm = jnp.concatenate([jnp.where(lo, qg, zero), jnp.where(lo, zero, qg)], axis=0)
        kj = kcat[WIN_BLK * j:WIN_BLK * (j + 3)]
        s = lax.dot_general(qm, kj, (((1,), (1,)), ((), ())), preferred_element_type=jnp.float32)
        return s + bias_ref[g] + edge

    def attend(s, j, g):
        vj = vcat[WIN_BLK * j:WIN_BLK * (j + 3)]
        sink = jnp.where(top, sink_ref[g], sink_ref[g + 4])
        m = jnp.maximum(jnp.max(s, axis=-1, keepdims=True), sink)
        p = jnp.exp(s - m)
        l = jnp.sum(p, axis=-1, keepdims=True) + jnp.exp(sink - m)
        o2 = jnp.dot(p.astype(jnp.bfloat16), vj, preferred_element_type=jnp.float32)
        o2 = o2 * (1.0 / l)
        o_ref[WIN_BLK * j:WIN_BLK * (j + 1), LANES * g:LANES * (g + 1)] = (
            jnp.where(lo, o2[:WIN_BLK], o2[WIN_BLK:]).astype(jnp.bfloat16))

    chains = [(j, g) for j in range(nsub) for g in range(4)]
    pending = [scores(*c) for c in chains[:WIN_LOOKAHEAD]]
    for idx, c in enumerate(chains):
        s = pending.pop(0)
        if idx + WIN_LOOKAHEAD < len(chains):
            pending.append(scores(*chains[idx + WIN_LOOKAHEAD]))
        attend(s, *c)


def _win_attention(qkv, sink, bsz, t, tq):
    n = bsz * t
    nsub = tq // WIN_BLK
    nblk_seq = t // WIN_BLK
    ntile = t // tq
    bias = jnp.asarray(_win_bias_table())

    def main_map(col):
        return lambda b, i, *_: (b * ntile + i, col)

    def prev_map(col):
        return lambda b, i, *_: (b * nblk_seq + jnp.maximum(i * nsub - 1, 0), col)

    def next_map(col):
        return lambda b, i, *_: (b * nblk_seq + jnp.minimum(i * nsub + nsub, nblk_seq - 1), col)

    halo = (WIN_BLK, LANES)
    ka, va = KA_COL // LANES, VA_COL // LANES
    grid_spec = pltpu.PrefetchScalarGridSpec(
        num_scalar_prefetch=1,
        grid=(bsz, ntile),
        in_specs=[
            pl.BlockSpec((tq, WIDTH_A), main_map(QA_COL // WIDTH_A)),
            pl.BlockSpec(halo, prev_map(ka)),
            pl.BlockSpec((tq, LANES), main_map(ka)),
            pl.BlockSpec(halo, next_map(ka)),
            pl.BlockSpec(halo, prev_map(va)),
            pl.BlockSpec((tq, LANES), main_map(va)),
            pl.BlockSpec(halo, next_map(va)),
            pl.BlockSpec((4, 2 * WIN_BLK, 3 * WIN_BLK), lambda b, i, *_: (0, 0, 0)),
        ],
        out_specs=pl.BlockSpec((tq, WIDTH_A), main_map(0)),
    )
    return pl.pallas_call(
        functools.partial(_win_kernel, nsub=nsub, nblk_seq=nblk_seq),
        out_shape=jax.ShapeDtypeStruct((n, WIDTH_A), jnp.bfloat16),
        grid_spec=grid_spec,
        compiler_params=_cparams("parallel", "parallel"),
        name="win",
    )(sink, qkv, qkv, qkv, qkv, qkv, qkv, qkv, bias)


NAT_ROWS_PER_STEP = 8
NAT_HALO_ROWS = NA_ROWS // 2
NAT_KEYS = NA_ROWS * GRID_W
NAT_ROWS_PER_TRIP = 8
NAT_HEADS_PER_CHAIN = 2
NAT_LOOKAHEAD = 4


def _nat_bias_table(rpb):
    c = np.arange(GRID_W)
    cs = np.clip(c - NA_COLS // 2, 0, GRID_W - NA_COLS)
    col_mask = (c[None, :] >= cs[:, None]) & (c[None, :] < cs[:, None] + NA_COLS)
    dc = np.clip(c[None, :] - c[:, None] + (NA_COLS - 1), 0, 2 * NA_COLS - 2)
    onehot = jnp.asarray(dc[None] == np.arange(2 * NA_COLS - 1)[:, None, None], jnp.float32)
    picked = jnp.einsum("hdj,jqc->hdqc", rpb, onehot, precision=lax.Precision.HIGHEST)
    t1 = jnp.where(col_mask[None, None], picked, NEG_BIG)
    per_shift = []
    for sh in range(NA_ROWS):
        w = t1[:, sh:sh + NA_ROWS]
        w = jnp.transpose(w, (0, 2, 1, 3)).reshape(N_HEADS_B // 2, 2 * GRID_W, NAT_KEYS)
        per_shift.append(w)
    return jnp.stack(per_shift, axis=1).astype(jnp.float32)


def _nat_kernel(q_ref, kp_ref, km_ref, kn_ref, vp_ref, vm_ref, vn_ref, tb_ref, o_ref, kcat, vcat,
                *, rows_seq):
    i = pl.program_id(1)
    halo = NAT_HALO_ROWS * GRID_W
    main = NAT_ROWS_PER_STEP * GRID_W
    kcat[0:halo] = kp_ref[...]
    kcat[halo:halo + main] = km_ref[...]
    kcat[halo + main:2 * halo + main] = kn_ref[...]
    vcat[0:halo] = vp_ref[...]
    vcat[halo:halo + main] = vm_ref[...]
    vcat[halo + main:2 * halo + main] = vn_ref[...]
    width = NAT_HEADS_PER_CHAIN * HEAD_DIM
    head_of_lane = lax.broadcasted_iota(jnp.int32, (1, width), 1) // HEAD_DIM
    zero = jnp.zeros((), jnp.bfloat16)
    r0 = i * NAT_ROWS_PER_STEP

    def scores(qr, c):
        r = r0 + qr
        rs = jnp.clip(r - NA_ROWS // 2, 0, rows_seq - NA_ROWS)
        koff = pl.multiple_of((rs - r0 + NAT_HALO_ROWS) * GRID_W, GRID_W)
        sh = rs - r + (NA_ROWS - 1)
        qoff = pl.multiple_of(qr * GRID_W, GRID_W)
        cols = slice(width * c, width * (c + 1))
        qc = q_ref[pl.ds(qoff, GRID_W), cols]
        qm = jnp.concatenate([jnp.where(head_of_lane == h, qc, zero) for h in range(NAT_HEADS_PER_CHAIN)], axis=0)
        kw = kcat[pl.ds(koff, NAT_KEYS), cols]
        s = lax.dot_general(qm, kw, (((1,), (1,)), ((), ())), preferred_element_type=jnp.float32)
        pairs = NAT_HEADS_PER_CHAIN // 2
        bias = jnp.concatenate([tb_ref[pairs * c + k, sh] for k in range(pairs)], axis=0)
        return s + bias, koff, qoff

    def attend(s, koff, qoff, c):
        cols = slice(width * c, width * (c + 1))
        vw = vcat[pl.ds(koff, NAT_KEYS), cols]
        m = jnp.max(s, axis=-1, keepdims=True)
        pe = jnp.exp(s - m)
        l = jnp.sum(pe, axis=-1, keepdims=True)
        o2 = jnp.dot(pe.astype(jnp.bfloat16), vw, preferred_element_type=jnp.float32)
        o2 = o2 * (1.0 / l)
        out = o2[:GRID_W]
        for h in range(1, NAT_HEADS_PER_CHAIN):
            out = jnp.where(head_of_lane == h, o2[GRID_W * h:GRID_W * (h + 1)], out)
        o_ref[pl.ds(qoff, GRID_W), cols] = out.astype(jnp.bfloat16)

    def trip(j, carry):
        chains = [(j * NAT_ROWS_PER_TRIP + q, c) for q in range(NAT_ROWS_PER_TRIP)
                  for c in range(N_HEADS_B // NAT_HEADS_PER_CHAIN)]
        pending = [scores(*c) for c in chains[:NAT_LOOKAHEAD]]
        for idx, (_, p) in enumerate(chains):
            s, koff, qoff = pending.pop(0)
            if idx + NAT_LOOKAHEAD < len(chains):
                pending.append(scores(*chains[idx + NAT_LOOKAHEAD]))
            attend(s, koff, qoff, p)
        return carry

    lax.fori_loop(0, NAT_ROWS_PER_STEP // NAT_ROWS_PER_TRIP, trip, 0)


def _nat_attention(qkv, tb, bsz, t):
    n = bsz * t
    rows_seq = t // GRID_W
    main = NAT_ROWS_PER_STEP * GRID_W
    halo = NAT_HALO_ROWS * GRID_W
    ntile = t // main
    nhalo_seq = t // halo
    per = main // halo

    def main_map(col):
        return lambda b, i: (b * ntile + i, col)

    def prev_map(col):
        return lambda b, i: (b * nhalo_seq + jnp.maximum(i * per - 1, 0), col)

    def next_map(col):
        return lambda b, i: (b * nhalo_seq + jnp.minimum(i * per + per, nhalo_seq - 1), col)

    qb, kb, vb = QB_COL // WIDTH_B, KB_COL // WIDTH_B, VB_COL // WIDTH_B
    return pl.pallas_call(
        functools.partial(_nat_kernel, rows_seq=rows_seq),
        out_shape=jax.ShapeDtypeStruct((n, WIDTH_B), jnp.bfloat16),
        grid=(bsz, ntile),
        in_specs=[
            pl.BlockSpec((main, WIDTH_B), main_map(qb)),
            pl.BlockSpec((halo, WIDTH_B), prev_map(kb)),
            pl.BlockSpec((main, WIDTH_B), main_map(kb)),
            pl.BlockSpec((halo, WIDTH_B), next_map(kb)),
            pl.BlockSpec((halo, WIDTH_B), prev_map(vb)),
            pl.BlockSpec((main, WIDTH_B), main_map(vb)),
            pl.BlockSpec((halo, WIDTH_B), next_map(vb)),
            pl.BlockSpec((N_HEADS_B // 2, NA_ROWS, 2 * GRID_W, NAT_KEYS), lambda b, i: (0, 0, 0, 0)),
        ],
        out_specs=pl.BlockSpec((main, WIDTH_B), main_map(0)),
        scratch_shapes=[pltpu.VMEM((main + 2 * halo, WIDTH_B), jnp.bfloat16),
                        pltpu.VMEM((main + 2 * halo, WIDTH_B), jnp.bfloat16)],
        compiler_params=_cparams("parallel", "parallel"),
        name="nat",
    )(qkv, qkv, qkv, qkv, qkv, qkv, qkv, tb)


EXPERT_ROW0 = N_GROUPS
ROUTE_ROWS = 48
INFO_E1, INFO_E2, INFO_R1, INFO_R2, INFO_W1, INFO_W2 = range(6)
INFO_ROWS = 8
MERGE_SUBTILES = 2


def _route(lt, carry, tri):
    rr, tm = lt.shape
    row = lax.broadcasted_iota(jnp.int32, (rr, tm), 0).astype(jnp.float32)
    none = jnp.float32(rr)

    def first_max(sel):
        m = jnp.max(jnp.where(sel, lt, NEG_BIG), axis=0, keepdims=True)
        idx = jnp.min(jnp.where(sel & (lt == m), row, none), axis=0, keepdims=True)
        return m, idx

    is_group = row < N_GROUPS
    mg, g = first_max(is_group)
    pg_sel = 1.0 / jnp.sum(jnp.where(is_group, jnp.exp(jnp.where(is_group, lt, mg) - mg), 0.0),
                           axis=0, keepdims=True)
    row0 = EXPERT_ROW0 + EXPERTS_PER_GROUP * g
    in_group = (row >= row0) & (row < row0 + EXPERTS_PER_GROUP)
    m1, i1 = first_max(in_group)
    m2, i2 = first_max(in_group & (row != i1))
    e2 = jnp.exp(m2 - m1)
    w1 = pg_sel / (1.0 + e2)
    w2 = pg_sel * e2 / (1.0 + e2)

    oh1 = row == i1
    oh2 = row == i2
    both = (oh1 | oh2).astype(jnp.bfloat16)
    before = jnp.dot(both, tri, preferred_element_type=jnp.float32) + carry
    r1 = jnp.sum(jnp.where(oh1, before, 0.0), axis=0, keepdims=True)
    r2 = jnp.sum(jnp.where(oh2, before, 0.0), axis=0, keepdims=True)
    new_carry = carry + jnp.sum(both.astype(jnp.float32), axis=1, keepdims=True)

    field = lax.broadcasted_iota(jnp.int32, (INFO_ROWS, tm), 0)
    info = jnp.zeros((INFO_ROWS, tm), jnp.float32)
    for k, v in ((INFO_E1, i1 - EXPERT_ROW0), (INFO_E2, i2 - EXPERT_ROW0), (INFO_R1, r1), (INFO_R2, r2),
                 (INFO_W1, w1), (INFO_W2, w2)):
        info = jnp.where(field == k, v, info)
    return info, new_carry


def _merge_kernel(x_ref, oa_ref, ob_ref, lng_ref, lnb_ref, wg_ref, wpa_ref, wpb_ref, wo_ref,
                  l1g_ref, l1b_ref, wr_ref, br_ref, cnt0_ref,
                  h1_ref, h1p_ref, info_ref, cnt_ref, carry_ref, tri_ref):
    tm = x_ref.shape[0]

    @pl.when(pl.program_id(0) == 0)
    def _():
        carry_ref[...] = cnt0_ref[...]
        r = lax.broadcasted_iota(jnp.int32, (tm, tm), 0)
        c = lax.broadcasted_iota(jnp.int32, (tm, tm), 1)
        tri_ref[...] = (r < c).astype(jnp.bfloat16)

    def project(rows):
        h = _layer_norm(x_ref[rows], lng_ref[...], lnb_ref[...])
        gates = jnp.dot(h.astype(jnp.bfloat16), wg_ref[...], preferred_element_type=jnp.float32)
        pa = jnp.dot(oa_ref[rows], wpa_ref[...], preferred_element_type=jnp.float32)
        pb = jnp.dot(ob_ref[rows], wpb_ref[...], preferred_element_type=jnp.float32)
        return h, gates, pa, pb

    def mix(h, gates, pa, pb):
        mixin = jax.nn.sigmoid(gates[:, :D_MODEL]) * pa + jax.nn.sigmoid(gates[:, D_MODEL:]) * pb
        return DEEPNORM_ALPHA * h + jnp.dot(mixin.astype(jnp.bfloat16), wo_ref[...],
                                            preferred_element_type=jnp.float32)

    def norm_and_logits(pre, rows):
        h1 = _layer_norm(pre, l1g_ref[...], l1b_ref[...])
        h1_ref[rows] = h1
        h1p_ref[rows] = _pack_rows(h1)
        return lax.dot_general(wr_ref[...], h1.astype(jnp.bfloat16), (((1,), (1,)), ((), ())),
                               preferred_element_type=jnp.float32)

    sub = tm // MERGE_SUBTILES
    parts = [slice(k * sub, (k + 1) * sub) for k in range(MERGE_SUBTILES)]
    projected = [project(rows) for rows in parts]
    mixed = [mix(*pr) for pr in projected]
    logits_t = jnp.concatenate([norm_and_logits(pre, rows) for pre, rows in zip(mixed, parts)], axis=1)
    logits_t = logits_t + br_ref[...]
    info, carry = _route(logits_t, carry_ref[...], tri_ref[...])
    info_ref[...] = info
    carry_ref[...] = carry
    cnt_ref[...] = carry[:, :LANES]


def _merge(x2, oa, ob, ln_g, ln_b, w_gates, w_pa, w_pb, w_o, l1g, l1b, w_r, b_r, cnt0, tm):
    n = x2.shape[0]

    def const(shape):
        return pl.BlockSpec(shape, lambda i: (0,) * len(shape))

    def rows(width):
        return pl.BlockSpec((tm, width), lambda i: (i, 0))

    return pl.pallas_call(
        _merge_kernel,
        out_shape=(jax.ShapeDtypeStruct((n, D_MODEL), jnp.float32),
                   jax.ShapeDtypeStruct((n, PACKED_WIDTH), jnp.uint32),
                   jax.ShapeDtypeStruct((INFO_ROWS, n), jnp.float32),
                   jax.ShapeDtypeStruct((ROUTE_ROWS, LANES), jnp.float32)),
        grid=(n // tm,),
        in_specs=[
            rows(D_MODEL), rows(WIDTH_A), rows(WIDTH_B),
            const((1, D_MODEL)), const((1, D_MODEL)),
            const((D_MODEL, 2 * D_MODEL)),
            const((WIDTH_A, D_MODEL)), const((WIDTH_B, D_MODEL)),
            const((D_MODEL, D_MODEL)),
            const((1, D_MODEL)), const((1, D_MODEL)),
            const((ROUTE_ROWS, D_MODEL)), const((ROUTE_ROWS, tm)), const((ROUTE_ROWS, tm)),
        ],
        out_specs=(rows(D_MODEL), rows(PACKED_WIDTH), pl.BlockSpec((INFO_ROWS, tm), lambda i: (0, i)),
                   const((ROUTE_ROWS, LANES))),
        scratch_shapes=[pltpu.VMEM((ROUTE_ROWS, tm), jnp.float32), pltpu.VMEM((tm, tm), jnp.bfloat16)],
        compiler_params=_cparams("arbitrary"),
        name="merge",
    )(x2, oa, ob, ln_g, ln_b, w_gates, w_pa, w_pb, w_o, l1g, l1b, w_r, b_r, cnt0)


SC_CORES = 2
SC_SUBCORES = 16
SC_WORKERS = SC_CORES * SC_SUBCORES
SC_ROWS_PER_STREAM = 64


def _sc_worker():
    return lax.axis_index("s") * SC_CORES + lax.axis_index("c")


def _dispatch(h1p, dest_t):
    n, width = h1p.shape
    per_worker = n // SC_WORKERS
    nchunks = per_worker // SC_ROWS_PER_STREAM
    assert nchunks * SC_ROWS_PER_STREAM * SC_WORKERS == n and nchunks % 2 == 0
    idx = dest_t.reshape(TOP_K, SC_WORKERS, nchunks, SC_ROWS_PER_STREAM)
    mesh = plsc.VectorSubcoreMesh(core_axis_name="c", subcore_axis_name="s")

    @functools.partial(
        pl.kernel, out_type=jax.ShapeDtypeStruct((TOP_K * n, width), h1p.dtype), mesh=mesh,
        scratch_types=[pltpu.VMEM((TOP_K, nchunks, SC_ROWS_PER_STREAM), jnp.int32),
                       pltpu.VMEM((2, SC_ROWS_PER_STREAM, width), h1p.dtype),
                       pltpu.SemaphoreType.DMA((2,)), pltpu.SemaphoreType.DMA((2,))],
        name="sc_dispatch")
    def scatter_kernel(src_hbm, idx_hbm, out_hbm, idx_v, rows_v, rsem, ssem):
        wid = _sc_worker()
        base = wid * per_worker
        for k in range(TOP_K):
            pltpu.sync_copy(idx_hbm.at[k, wid], idx_v.at[k])

        def read(j, slot):
            src = src_hbm.at[pl.ds(base + j * SC_ROWS_PER_STREAM, SC_ROWS_PER_STREAM)]
            return pltpu.make_async_copy(src, rows_v.at[slot], rsem.at[slot])

        def scatter(j, slot, k):
            return pltpu.make_async_copy(rows_v.at[slot], out_hbm.at[idx_v.at[k, j]], ssem.at[slot])

        read(0, 0).start()

        @pl.loop(0, nchunks, step=2)
        def _(j0):
            for slot in range(2):
                j = j0 + slot
                read(j, slot).wait()

                @pl.when(j + 1 < nchunks)
                def _():
                    @pl.when(j >= 1)
                    def _():
                        for k in range(TOP_K):
                            scatter(j - 1, 1 - slot, k).wait()
                    read(j + 1, 1 - slot).start()

                for k in range(TOP_K):
                    scatter(j, slot, k).start()

        for k in range(TOP_K):
            scatter(nchunks - 2, 0, k).wait()
            scatter(nchunks - 1, 1, k).wait()

    return scatter_kernel(h1p, idx)


def _moe_plan(counts, na, bm):
    expert = jnp.arange(N_EXPERTS, dtype=jnp.int32)
    upto = expert[None, :] <= expert[:, None]

    def running_sum(v):
        return jnp.sum(jnp.where(upto, v[None, :], 0), axis=1)

    ends = running_sum(counts)
    starts = ends - counts
    first_blk = starts // bm
    tiles = jnp.where(counts > 0, (ends - 1) // bm - first_blk + 1, 0)
    item_end = running_sum(tiles)
    item_start = item_end - tiles
    total = jnp.sum(tiles)
    wmax = na // bm + N_EXPERTS - 1
    w = jnp.arange(wmax, dtype=jnp.int32)
    wc = jnp.minimum(w, total - 1)
    e = jnp.sum((item_end[None, :] <= wc[:, None]).astype(jnp.int32), axis=1)
    e = jnp.minimum(e, N_EXPERTS - 1)
    owner = e[:, None] == expert[None, :]

    def of_owner(table):
        return jnp.sum(jnp.where(owner, table[None, :], 0), axis=1)

    blk = (of_owner(first_blk) + (wc - of_owner(item_start))).astype(jnp.int32)
    valid = w < total
    lo = jnp.where(valid, jnp.maximum(of_owner(starts), blk * bm), 0).astype(jnp.int32)
    hi = jnp.where(valid, jnp.minimum(of_owner(ends), (blk + 1) * bm), 0).astype(jnp.int32)
    prev_blk = jnp.concatenate([jnp.full((1,), -1, jnp.int32), blk[:-1]])
    prev_e = jnp.concatenate([jnp.full((1,), -1, jnp.int32), e[:-1]])
    new_expert = valid & (e != prev_e)
    flags = (valid.astype(jnp.int32)
             + FLAG_NEW_BLOCK * (valid & (blk != prev_blk)).astype(jnp.int32)
             + FLAG_NEW_EXPERT * new_expert.astype(jnp.int32))
    earlier = w[None, :] <= w[:, None]
    slot = (jnp.sum(jnp.where(earlier, new_expert[None, :].astype(jnp.int32), 0), axis=1) + 1) % 2
    later = valid[None, :] & (e[None, :] > e[:, None])
    nxt = jnp.min(jnp.where(later, e[None, :], N_EXPERTS), axis=1)
    nxt = jnp.where(valid & (nxt < N_EXPERTS), nxt, -1)
    return blk, e, lo, hi, flags, slot.astype(jnp.int32), nxt.astype(jnp.int32)


FLAG_VALID, FLAG_NEW_BLOCK, FLAG_NEW_EXPERT = 1, 2, 4


def _expert_kernel(blk_ref, e_ref, lo_ref, hi_ref, flag_ref, slot_ref, nxt_ref, x_ref, wg_hbm, wu_hbm, wd_hbm, o_ref,
                   wg_f, wu_f, wd_f, wg_b, wu_b, wd_b, wsem):
    w = pl.program_id(0)
    bm = x_ref.shape[0]
    flags = flag_ref[w]

    def weight_copies(expert, slot):
        return [pltpu.make_async_copy(hbm.at[expert], buf.at[slot], wsem.at[slot, m])
                for m, (hbm, buf) in enumerate(((wg_hbm, wg_f), (wu_hbm, wu_f), (wd_hbm, wd_f)))]

    @pl.when((flags & FLAG_NEW_EXPERT) != 0)
    def _():
        slot = slot_ref[w]

        @pl.when(w == 0)
        def _():
            for c in weight_copies(e_ref[w], slot):
                c.start()

        for c in weight_copies(e_ref[w], slot):
            c.wait()
        wg_b[...] = wg_f[slot].astype(jnp.bfloat16)
        wu_b[...] = wu_f[slot].astype(jnp.bfloat16)
        wd_b[...] = wd_f[slot].astype(jnp.bfloat16)

        @pl.when(nxt_ref[w] >= 0)
        def _():
            for c in weight_copies(nxt_ref[w], 1 - slot):
                c.start()

    @pl.when((flags & FLAG_VALID) != 0)
    def _():
        x = _unpack_rows(x_ref[...]).astype(jnp.bfloat16)
        g = jnp.dot(x, wg_b[...], preferred_element_type=jnp.float32)
        u = jnp.dot(x, wu_b[...], preferred_element_type=jnp.float32)
        hmid = (jax.nn.silu(g) * u).astype(jnp.bfloat16)
        y = _pack_rows(jnp.dot(hmid, wd_b[...], preferred_element_type=jnp.float32))
        row = blk_ref[w] * bm + lax.broadcasted_iota(jnp.int32, (bm, 1), 0)
        mine = (row >= lo_ref[w]) & (row < hi_ref[w])

        @pl.when((flags & FLAG_NEW_BLOCK) != 0)
        def _():
            o_ref[...] = jnp.where(mine, y, jnp.uint32(0))

        @pl.when((flags & FLAG_NEW_BLOCK) == 0)
        def _():
            o_ref[...] = jnp.where(mine, y, o_ref[...])


def _experts(xs, plan, w_gate, w_up, w_down, bm):
    na = xs.shape[0]
    nitems = plan[0].shape[0]
    up_shape, down_shape = (D_MODEL, D_EXPERT), (D_EXPERT, D_MODEL)
    grid_spec = pltpu.PrefetchScalarGridSpec(
        num_scalar_prefetch=len(plan),
        grid=(nitems,),
        in_specs=[
            pl.BlockSpec((bm, PACKED_WIDTH), lambda w, blk, *_: (blk[w], 0)),
            pl.BlockSpec(memory_space=pl.ANY),
            pl.BlockSpec(memory_space=pl.ANY),
            pl.BlockSpec(memory_space=pl.ANY),
        ],
        out_specs=pl.BlockSpec((bm, PACKED_WIDTH), lambda w, blk, *_: (blk[w], 0)),
        scratch_shapes=[pltpu.VMEM((2,) + up_shape, jnp.float32),
                        pltpu.VMEM((2,) + up_shape, jnp.float32),
                        pltpu.VMEM((2,) + down_shape, jnp.float32),
                        pltpu.VMEM(up_shape, jnp.bfloat16),
                        pltpu.VMEM(up_shape, jnp.bfloat16),
                        pltpu.VMEM(down_shape, jnp.bfloat16),
                        pltpu.SemaphoreType.DMA((2, 3))],
    )
    return pl.pallas_call(
        _expert_kernel,
        out_shape=jax.ShapeDtypeStruct((na, PACKED_WIDTH), jnp.uint32),
        grid_spec=grid_spec,
        compiler_params=_cparams("arbitrary"),
        name="experts",
    )(*plan, xs, w_gate, w_up, w_down)


def _sc_gather_rows(table, idx):
    nrows = idx.shape[0]
    width = table.shape[1]
    per_worker = nrows // SC_WORKERS
    nchunks = per_worker // SC_ROWS_PER_STREAM
    assert nchunks * SC_ROWS_PER_STREAM * SC_WORKERS == nrows and nchunks % 2 == 0
    mesh = plsc.VectorSubcoreMesh(core_axis_name="c", subcore_axis_name="s")

    @functools.partial(
        pl.kernel, out_type=jax.ShapeDtypeStruct((nrows, width), table.dtype), mesh=mesh,
        scratch_types=[pltpu.VMEM((per_worker,), jnp.int32),
                       pltpu.VMEM((2, SC_ROWS_PER_STREAM, width), table.dtype),
                       pltpu.SemaphoreType.DMA((2,)), pltpu.SemaphoreType.DMA((2,))],
        name="sc_gather")
    def gather_kernel(table_hbm, idx_hbm, out_hbm, idx_v, rows_v, gsem, wsem):
        base = _sc_worker() * per_worker
        pltpu.sync_copy(idx_hbm.at[pl.ds(base, per_worker)], idx_v)

        def gather(j, slot):
            rows = idx_v.at[pl.ds(j * SC_ROWS_PER_STREAM, SC_ROWS_PER_STREAM)]
            return pltpu.make_async_copy(table_hbm.at[rows], rows_v.at[slot], gsem.at[slot])

        def write(j, slot):
            dst = out_hbm.at[pl.ds(base + j * SC_ROWS_PER_STREAM, SC_ROWS_PER_STREAM)]
            return pltpu.make_async_copy(rows_v.at[slot], dst, wsem.at[slot])

        gather(0, 0).start()

        @pl.loop(0, nchunks, step=2)
        def _(j0):
            for slot in range(2):
                j = j0 + slot
                gather(j, slot).wait()

                @pl.when(j + 1 < nchunks)
                def _():
                    @pl.when(j >= 1)
                    def _():
                        write(j - 1, 1 - slot).wait()
                    gather(j + 1, 1 - slot).start()

                write(j, slot).start()

        write(nchunks - 2, 0).wait()
        write(nchunks - 1, 1).wait()

    return gather_kernel(table, idx)


def _finalize_kernel(h1_ref, y1_ref, y2_ref, info_ref, g_ref, b_ref, o_ref):
    tt = h1_ref.shape[0]
    pad = jnp.zeros((LANES - INFO_ROWS, tt), jnp.float32)
    info = jnp.concatenate([info_ref[...], pad], axis=0).T
    moe = (_unpack_rows(y1_ref[...]) * info[:, INFO_W1:INFO_W1 + 1]
           + _unpack_rows(y2_ref[...]) * info[:, INFO_W2:INFO_W2 + 1])
    o_ref[...] = _layer_norm(DEEPNORM_ALPHA * h1_ref[...] + moe, g_ref[...], b_ref[...])


def _combine(h1, info, dest_t, ys, ln_g, ln_b, tt):
    n = h1.shape[0]
    nsteps = n // tt
    yg = _sc_gather_rows(ys, dest_t.reshape(TOP_K * n))
    return pl.pallas_call(
        _finalize_kernel,
        out_shape=jax.ShapeDtypeStruct((n, D_MODEL), jnp.float32),
        grid=(nsteps,),
        in_specs=[
            pl.BlockSpec((tt, D_MODEL), lambda i: (i, 0)),
            pl.BlockSpec((tt, PACKED_WIDTH), lambda i: (i, 0)),
            pl.BlockSpec((tt, PACKED_WIDTH), lambda i: (nsteps + i, 0)),
            pl.BlockSpec((INFO_ROWS, tt), lambda i: (0, i)),
            pl.BlockSpec((1, D_MODEL), lambda i: (0, 0)),
            pl.BlockSpec((1, D_MODEL), lambda i: (0, 0)),
        ],
        out_specs=pl.BlockSpec((tt, D_MODEL), lambda i: (i, 0)),
        compiler_params=_cparams("parallel"),
        name="finalize",
    )(h1, yg, yg, info, ln_g, ln_b)


TM_QKV = 1024
TQ_WIN = 512
TM_MERGE = 1024
TT_ROWS = 1024
BM_EXPERT = 512


def _prepare_weights(ln_in_g, ln_in_b, w_in, attn_sink, rel_pos_bias, w_proj_a, w_proj_b, w_out,
                     ln1_g, ln1_b, w_route_group, b_route_group, w_route_expert, b_route_expert,
                     ln2_g, ln2_b):
    bf = jnp.bfloat16
    w = w_in[0]
    splits = np.cumsum([WIDTH_A, KV_WIDTH_A, KV_WIDTH_A, WIDTH_B, WIDTH_B, WIDTH_B, D_MODEL])
    wqa, wka, wva, wqb, wkb, wvb, wga, wgb = jnp.split(w, [int(s) for s in splits], axis=1)
    wqa = (wqa.reshape(D_MODEL, N_KV_HEADS_A, GQA_GROUP, HEAD_DIM).transpose(0, 2, 1, 3)
           .reshape(D_MODEL, WIDTH_A))
    w_qkv = jnp.concatenate([wqa, wqb, wkb, wvb, wka, wva], axis=1).astype(bf)
    w_gates = jnp.concatenate([wga, wgb], axis=1).astype(bf)
    w_pa = (w_proj_a[0].reshape(N_KV_HEADS_A, GQA_GROUP, HEAD_DIM, D_MODEL).transpose(1, 0, 2, 3)
            .reshape(WIDTH_A, D_MODEL).astype(bf))
    w_pb = w_proj_b[0].astype(bf)
    w_o = w_out[0].astype(bf)
    pad = ROUTE_ROWS - N_GROUPS - N_EXPERTS
    w_r = jnp.concatenate([w_route_group[0].T, w_route_expert[0].T, jnp.zeros((pad, D_MODEL), jnp.float32)], axis=0)
    w_route = w_r.astype(bf)
    b_r = jnp.concatenate([b_route_group[0], b_route_expert[0], jnp.zeros((pad,), jnp.float32)])
    b_r = jnp.broadcast_to(b_r[:, None], (ROUTE_ROWS, TM_MERGE))
    row = lambda v: v.reshape(1, D_MODEL)
    return dict(
        ln_in_g=row(ln_in_g), ln_in_b=row(ln_in_b), w_qkv=w_qkv, w_gates=w_gates,
        sink=attn_sink[0].astype(jnp.float32), nat_bias=_nat_bias_table(rel_pos_bias[0]),
        w_pa=w_pa, w_pb=w_pb, w_o=w_o, ln1_g=row(ln1_g[0]), ln1_b=row(ln1_b[0]),
        w_route=w_route, b_r=b_r, ln2_g=row(ln2_g[0]), ln2_b=row(ln2_b[0]))


def _after(value, other):
    if other is None:
        return value
    other = other.astype(jnp.float32)
    zero = jnp.where(jnp.isfinite(other), other, 0.0) * 0.0
    return value + zero.astype(value.dtype)


def _attend_and_route(x, p, after=None):
    bsz, t, _ = x.shape
    n = bsz * t
    x2 = x.reshape(n, D_MODEL)
    qkv = _qkv(x2, p["ln_in_g"], p["ln_in_b"], p["w_qkv"], TM_QKV)
    oa = _win_attention(qkv, p["sink"], bsz, t, TQ_WIN)
    ob = _nat_attention(qkv, p["nat_bias"], bsz, t)
    cnt0 = _after(jnp.zeros((ROUTE_ROWS, TM_MERGE), jnp.float32), after)
    h1, h1p, info, cnt = _merge(x2, oa, ob, p["ln_in_g"], p["ln_in_b"], p["w_gates"], p["w_pa"], p["w_pb"],
                                p["w_o"], p["ln1_g"], p["ln1_b"], p["w_route"], p["b_r"], cnt0, TM_MERGE)
    counts = cnt[EXPERT_ROW0:EXPERT_ROW0 + N_EXPERTS, 0].astype(jnp.int32)
    expert = jnp.arange(N_EXPERTS, dtype=jnp.int32)
    starts = jnp.sum(jnp.where(expert[None, :] < expert[:, None], counts[None, :], 0), axis=1)
    eid = info[INFO_E1:INFO_E2 + 1].astype(jnp.int32)
    rank = info[INFO_R1:INFO_R2 + 1].astype(jnp.int32)
    dest_t = rank + jnp.sum(jnp.where(eid[None] == expert[:, None, None], starts[:, None, None], 0), axis=0)
    return dict(shape=x.shape, h1=h1, h1p=h1p, info=info, counts=counts, dest_t=dest_t)


def _run_experts(r, w_gate, w_up, w_down, after=None):
    n = r["h1"].shape[0]
    xs = _dispatch(r["h1p"], r["dest_t"])
    bm = BM_EXPERT if TOP_K * n // BM_EXPERT >= 4 * N_EXPERTS else BM_EXPERT // 2
    blk, e, lo, hi, flags, slot, nxt = _moe_plan(r["counts"], TOP_K * n, bm)
    plan = (blk, e, lo, hi, _after(flags, after), slot, nxt)
    return _experts(xs, plan, w_gate[0], w_up[0], w_down[0], bm)


def _finish(r, ys, p, after=None):
    out = _combine(r["h1"], r["info"], r["dest_t"], ys, _after(p["ln2_g"], after), p["ln2_b"], TT_ROWS)
    return out.reshape(r["shape"])


def kernel(x_prompt, x_sample, ln_in_g, ln_in_b, w_in, attn_sink, rel_pos_bias, w_proj_a, w_proj_b, w_out,
           ln1_g, ln1_b, w_route_group, b_route_group, w_route_expert, b_route_expert,
           w_gate, w_up, w_down, ln2_g, ln2_b):
    p = _prepare_weights(ln_in_g, ln_in_b, w_in, attn_sink, rel_pos_bias, w_proj_a, w_proj_b, w_out,
                         ln1_g, ln1_b, w_route_group, b_route_group, w_route_expert, b_route_expert,
                         ln2_g, ln2_b)
    rp = _attend_and_route(x_prompt, p)
    rs = _attend_and_route(x_sample, p, after=rp["counts"][0])
    ys_p = _run_experts(rp, w_gate, w_up, w_down)
    ys_s = _run_experts(rs, w_gate, w_up, w_down, after=ys_p[0, 0])
    y_prompt = _finish(rp, ys_p, p)
    y_sample = _finish(rs, ys_s, p, after=y_prompt[0, 0, 0])
    return (y_prompt, y_sample)
```

```python
import functools

import numpy as np
import jax
import jax.numpy as jnp
from jax import lax
from jax.experimental import pallas as pl
from jax.experimental.pallas import tpu as pltpu
from jax.experimental.pallas import tpu_sc as plsc

D_MODEL = 1024
HEAD_DIM = 64
N_HEADS_A = 8
N_KV_HEADS_A = 2
WINDOW = 128
N_HEADS_B = 8
GRID_W = 64
NA_ROWS = 8
NA_COLS = 16
N_GROUPS = 4
EXPERTS_PER_GROUP = 8
N_EXPERTS = N_GROUPS * EXPERTS_PER_GROUP
TOP_K = 2
D_EXPERT = D_MODEL // 2
LN_EPS = 1e-5
DEPTH = 1
DEEPNORM_ALPHA = (2.0 * DEPTH) ** 0.25
WIDTH_A = N_HEADS_A * HEAD_DIM
KV_WIDTH_A = N_KV_HEADS_A * HEAD_DIM
WIDTH_B = N_HEADS_B * HEAD_DIM
QKV_WIDTH = WIDTH_A + 2 * KV_WIDTH_A + 3 * WIDTH_B

LANES = 128
VMEM_LIMIT_BYTES = 56 * 1024 * 1024

NEG_BIG = -1e30
LOG2E = float(np.log2(np.e))

QA_COL, QB_COL, KB_COL, VB_COL = 0, WIDTH_A, WIDTH_A + WIDTH_B, WIDTH_A + 2 * WIDTH_B
KA_COL = WIDTH_A + 3 * WIDTH_B
VA_COL = KA_COL + KV_WIDTH_A

GQA_GROUP = N_HEADS_A // N_KV_HEADS_A


def _cparams(*sem):
    return pltpu.CompilerParams(dimension_semantics=sem, vmem_limit_bytes=VMEM_LIMIT_BYTES)


def _layer_norm(x, g, b):
    mu = jnp.mean(x, axis=-1, keepdims=True)
    xc = x - mu
    var = jnp.mean(xc * xc, axis=-1, keepdims=True)
    return xc * lax.rsqrt(var + LN_EPS) * g + b


PACKED_WIDTH = D_MODEL // 2


def _pack_rows(x):
    def rne(v):
        return v + jnp.uint32(0x7FFF) + ((v >> 16) & jnp.uint32(1))
    hi = lax.bitcast_convert_type(x[:, :PACKED_WIDTH], jnp.uint32)
    lo = lax.bitcast_convert_type(x[:, PACKED_WIDTH:], jnp.uint32)
    return (rne(hi) & jnp.uint32(0xFFFF0000)) | (rne(lo) >> 16)


def _unpack_rows(w):
    hi = lax.bitcast_convert_type(w & jnp.uint32(0xFFFF0000), jnp.float32)
    lo = lax.bitcast_convert_type(w << 16, jnp.float32)
    return jnp.concatenate([hi, lo], axis=1)


def _qkv_kernel(x_ref, g_ref, b_ref, w_ref, o_ref):
    h = _layer_norm(x_ref[...], g_ref[...], b_ref[...])
    y = jnp.dot(h.astype(jnp.bfloat16), w_ref[...], preferred_element_type=jnp.float32)
    col = lax.broadcasted_iota(jnp.int32, (1, QKV_WIDTH), 1)
    y = y * jnp.where(col < KB_COL, HEAD_DIM ** -0.5 * LOG2E, 1.0)
    o_ref[...] = y.astype(jnp.bfloat16)


def _qkv(x2, ln_g, ln_b, w_qkv, tm):
    n = x2.shape[0]
    return pl.pallas_call(
        _qkv_kernel,
        out_shape=jax.ShapeDtypeStruct((n, QKV_WIDTH), jnp.bfloat16),
        grid=(n // tm,),
        in_specs=[
            pl.BlockSpec((tm, D_MODEL), lambda i: (i, 0)),
            pl.BlockSpec((1, D_MODEL), lambda i: (0, 0)),
            pl.BlockSpec((1, D_MODEL), lambda i: (0, 0)),
            pl.BlockSpec((D_MODEL, QKV_WIDTH), lambda i: (0, 0)),
        ],
        out_specs=pl.BlockSpec((tm, QKV_WIDTH), lambda i: (i, 0)),
        compiler_params=_cparams("parallel"),
        name="qkv",
    )(x2, ln_g, ln_b, w_qkv)


WIN_BLK = 128
WIN_LOOKAHEAD = 2


def _win_bias_table():
    qi = np.arange(WIN_BLK)[:, None]
    kj = np.arange(3 * WIN_BLK)[None, :]
    dist = np.abs(kj - WIN_BLK - qi).astype(np.float64)
    slopes = 2.0 ** (-8.0 * np.arange(1, N_HEADS_A + 1) / N_HEADS_A)
    per_head = np.where(dist <= WINDOW, -slopes[:, None, None] * dist[None] * LOG2E, NEG_BIG)
    groups = [np.concatenate([per_head[j], per_head[j + 4]], axis=0) for j in range(4)]
    return np.stack(groups).astype(np.float32)


def _win_kernel(sink_ref, q_ref, kp_ref, km_ref, kn_ref, vp_ref, vm_ref, vn_ref, bias_ref, o_ref,
                *, nsub, nblk_seq):
    i = pl.program_id(1)
    kcat = jnp.concatenate([kp_ref[...], km_ref[...], kn_ref[...]], axis=0)
    vcat = jnp.concatenate([vp_ref[...], vm_ref[...], vn_ref[...]], axis=0)
    lo = lax.broadcasted_iota(jnp.int32, (1, LANES), 1) < HEAD_DIM
    col = lax.broadcasted_iota(jnp.int32, (1, 3 * WIN_BLK), 1)
    top = lax.broadcasted_iota(jnp.int32, (2 * WIN_BLK, 1), 0) < WIN_BLK
    zero = jnp.zeros((), jnp.bfloat16)

    def scores(j, g):
        n = i * nsub + j
        off_seq = ((col < WIN_BLK) & (n == 0)) | ((col >= 2 * WIN_BLK) & (n == nblk_seq - 1))
        edge = jnp.where(off_seq, NEG_BIG, 0.0)
        qg = q_ref[WIN_BLK * j:WIN_BLK * (j + 1), LANES * g:LANES * (g + 1)]
        qm = jnp.concatenate([jnp.where(lo, qg, zero), jnp.where(lo, zero, qg)], axis=0)
        kj = kcat[WIN_BLK * j:WIN_BLK * (j + 3)]
        s = lax.dot_general(qm, kj, (((1,), (1,)), ((), ())), preferred_element_type=jnp.float32)
        return s + bias_ref[g] + edge

    def attend(s, j, g):
        vj = vcat[WIN_BLK * j:WIN_BLK * (j + 3)]
        sink = jnp.where(top, sink_ref[g], sink_ref[g + 4]) * LOG2E
        m = jnp.maximum(jnp.max(s, axis=-1, keepdims=True), sink)
        p = jnp.exp2(s - m)
        l = jnp.sum(p, axis=-1, keepdims=True) + jnp.exp2(sink - m)
        o2 = jnp.dot(p.astype(jnp.bfloat16), vj, preferred_element_type=jnp.float32)
        o2 = o2 * (1.0 / l)
        o_ref[WIN_BLK * j:WIN_BLK * (j + 1), LANES * g:LANES * (g + 1)] = (
            jnp.where(lo, o2[:WIN_BLK], o2[WIN_BLK:]).astype(jnp.bfloat16))

    chains = [(j, g) for j in range(nsub) for g in range(4)]
    pending = [scores(*c) for c in chains[:WIN_LOOKAHEAD]]
    for idx, c in enumerate(chains):
        s = pending.pop(0)
        if idx + WIN_LOOKAHEAD < len(chains):
            pending.append(scores(*chains[idx + WIN_LOOKAHEAD]))
        attend(s, *c)


def _win_attention(qkv, sink, bsz, t, tq):
    n = bsz * t
    nsub = tq // WIN_BLK
    nblk_seq = t // WIN_BLK
    ntile = t // tq
    bias = jnp.asarray(_win_bias_table())

    def main_map(col):
        return lambda b, i, *_: (b * ntile + i, col)

    def prev_map(col):
        return lambda b, i, *_: (b * nblk_seq + jnp.maximum(i * nsub - 1, 0), col)

    def next_map(col):
        return lambda b, i, *_: (b * nblk_seq + jnp.minimum(i * nsub + nsub, nblk_seq - 1), col)

    halo = (WIN_BLK, LANES)
    ka, va = KA_COL // LANES, VA_COL // LANES
    grid_spec = pltpu.PrefetchScalarGridSpec(
        num_scalar_prefetch=1,
        grid=(bsz, ntile),
        in_specs=[
            pl.BlockSpec((tq, WIDTH_A), main_map(QA_COL // WIDTH_A)),
            pl.BlockSpec(halo, prev_map(ka)),
            pl.BlockSpec((tq, LANES), main_map(ka)),
            pl.BlockSpec(halo, next_map(ka)),
            pl.BlockSpec(halo, prev_map(va)),
            pl.BlockSpec((tq, LANES), main_map(va)),
            pl.BlockSpec(halo, next_map(va)),
            pl.BlockSpec((4, 2 * WIN_BLK, 3 * WIN_BLK), lambda b, i, *_: (0, 0, 0)),
        ],
        out_specs=pl.BlockSpec((tq, WIDTH_A), main_map(0)),
    )
    return pl.pallas_call(
        functools.partial(_win_kernel, nsub=nsub, nblk_seq=nblk_seq),
        out_shape=jax.ShapeDtypeStruct((n, WIDTH_A), jnp.bfloat16),
        grid_spec=grid_spec,
        compiler_params=_cparams("parallel", "parallel"),
        name="win",
    )(sink, qkv, qkv, qkv, qkv, qkv, qkv, qkv, bias)


NAT_ROWS_PER_STEP = 8
NAT_HALO_ROWS = NA_ROWS // 2
NAT_KEYS = NA_ROWS * GRID_W
NAT_ROWS_PER_TRIP = 8
NAT_HEADS_PER_CHAIN = 2
NAT_LOOKAHEAD = 4


def _nat_bias_table(rpb):
    c = np.arange(GRID_W)
    cs = np.clip(c - NA_COLS // 2, 0, GRID_W - NA_COLS)
    col_mask = (c[None, :] >= cs[:, None]) & (c[None, :] < cs[:, None] + NA_COLS)
    dc = np.clip(c[None, :] - c[:, None] + (NA_COLS - 1), 0, 2 * NA_COLS - 2)
    onehot = jnp.asarray(dc[None] == np.arange(2 * NA_COLS - 1)[:, None, None], jnp.float32)
    picked = jnp.einsum("hdj,jqc->hdqc", rpb, onehot, precision=lax.Precision.HIGHEST)
    t1 = jnp.where(col_mask[None, None], picked * LOG2E, NEG_BIG)
    per_shift = []
    for sh in range(NA_ROWS):
        w = t1[:, sh:sh + NA_ROWS]
        w = jnp.transpose(w, (0, 2, 1, 3)).reshape(N_HEADS_B // 2, 2 * GRID_W, NAT_KEYS)
        per_shift.append(w)
    return jnp.stack(per_shift, axis=1).astype(jnp.float32)


def _nat_kernel(q_ref, kp_ref, km_ref, kn_ref, vp_ref, vm_ref, vn_ref, tb_ref, o_ref, kcat, vcat,
                *, rows_seq):
    i = pl.program_id(1)
    halo = NAT_HALO_ROWS * GRID_W
    main = NAT_ROWS_PER_STEP * GRID_W
    kcat[0:halo] = kp_ref[...]
    kcat[halo:halo + main] = km_ref[...]
    kcat[halo + main:2 * halo + main] = kn_ref[...]
    vcat[0:halo] = vp_ref[...]
    vcat[halo:halo + main] = vm_ref[...]
    vcat[halo + main:2 * halo + main] = vn_ref[...]
    width = NAT_HEADS_PER_CHAIN * HEAD_DIM
    head_of_lane = lax.broadcasted_iota(jnp.int32, (1, width), 1) // HEAD_DIM
    zero = jnp.zeros((), jnp.bfloat16)
    r0 = i * NAT_ROWS_PER_STEP

    def scores(qr, c):
        r = r0 + qr
        rs = jnp.clip(r - NA_ROWS // 2, 0, rows_seq - NA_ROWS)
        koff = pl.multiple_of((rs - r0 + NAT_HALO_ROWS) * GRID_W, GRID_W)
        sh = rs - r + (NA_ROWS - 1)
        qoff = pl.multiple_of(qr * GRID_W, GRID_W)
        cols = slice(width * c, width * (c + 1))
        qc = q_ref[pl.ds(qoff, GRID_W), cols]
        qm = jnp.concatenate([jnp.where(head_of_lane == h, qc, zero) for h in range(NAT_HEADS_PER_CHAIN)], axis=0)
        kw = kcat[pl.ds(koff, NAT_KEYS), cols]
        s = lax.dot_general(qm, kw, (((1,), (1,)), ((), ())), preferred_element_type=jnp.float32)
        pairs = NAT_HEADS_PER_CHAIN // 2
        bias = jnp.concatenate([tb_ref[pairs * c + k, sh] for k in range(pairs)], axis=0)
        return s + bias, koff, qoff

    def attend(s, koff, qoff, c):
        cols = slice(width * c, width * (c + 1))
        vw = vcat[pl.ds(koff, NAT_KEYS), cols]
        m = jnp.max(s, axis=-1, keepdims=True)
        pe = jnp.exp2(s - m)
        l = jnp.sum(pe, axis=-1, keepdims=True)
        o2 = jnp.dot(pe.astype(jnp.bfloat16), vw, preferred_element_type=jnp.float32)
        o2 = o2 * (1.0 / l)
        out = o2[:GRID_W]
        for h in range(1, NAT_HEADS_PER_CHAIN):
            out = jnp.where(head_of_lane == h, o2[GRID_W * h:GRID_W * (h + 1)], out)
        o_ref[pl.ds(qoff, GRID_W), cols] = out.astype(jnp.bfloat16)

    def trip(j, carry):
        chains = [(j * NAT_ROWS_PER_TRIP + q, c) for q in range(NAT_ROWS_PER_TRIP)
                  for c in range(N_HEADS_B // NAT_HEADS_PER_CHAIN)]
        pending = [scores(*c) for c in chains[:NAT_LOOKAHEAD]]
        for idx, (_, p) in enumerate(chains):
            s, koff, qoff = pending.pop(0)
            if idx + NAT_LOOKAHEAD < len(chains):
                pending.append(scores(*chains[idx + NAT_LOOKAHEAD]))
            attend(s, koff, qoff, p)
        return carry

    lax.fori_loop(0, NAT_ROWS_PER_STEP // NAT_ROWS_PER_TRIP, trip, 0)


def _nat_attention(qkv, tb, bsz, t):
    n = bsz * t
    rows_seq = t // GRID_W
    main = NAT_ROWS_PER_STEP * GRID_W
    halo = NAT_HALO_ROWS * GRID_W
    ntile = t // main
    nhalo_seq = t // halo
    per = main // halo

    def main_map(col):
        return lambda b, i: (b * ntile + i, col)

    def prev_map(col):
        return lambda b, i: (b * nhalo_seq + jnp.maximum(i * per - 1, 0), col)

    def next_map(col):
        return lambda b, i: (b * nhalo_seq + jnp.minimum(i * per + per, nhalo_seq - 1), col)

    qb, kb, vb = QB_COL // WIDTH_B, KB_COL // WIDTH_B, VB_COL // WIDTH_B
    return pl.pallas_call(
        functools.partial(_nat_kernel, rows_seq=rows_seq),
        out_shape=jax.ShapeDtypeStruct((n, WIDTH_B), jnp.bfloat16),
        grid=(bsz, ntile),
        in_specs=[
            pl.BlockSpec((main, WIDTH_B), main_map(qb)),
            pl.BlockSpec((halo, WIDTH_B), prev_map(kb)),
            pl.BlockSpec((main, WIDTH_B), main_map(kb)),
            pl.BlockSpec((halo, WIDTH_B), next_map(kb)),
            pl.BlockSpec((halo, WIDTH_B), prev_map(vb)),
            pl.BlockSpec((main, WIDTH_B), main_map(vb)),
            pl.BlockSpec((halo, WIDTH_B), next_map(vb)),
            pl.BlockSpec((N_HEADS_B // 2, NA_ROWS, 2 * GRID_W, NAT_KEYS), lambda b, i: (0, 0, 0, 0)),
        ],
        out_specs=pl.BlockSpec((main, WIDTH_B), main_map(0)),
        scratch_shapes=[pltpu.VMEM((main + 2 * halo, WIDTH_B), jnp.bfloat16),
                        pltpu.VMEM((main + 2 * halo, WIDTH_B), jnp.bfloat16)],
        compiler_params=_cparams("parallel", "parallel"),
        name="nat",
    )(qkv, qkv, qkv, qkv, qkv, qkv, qkv, tb)


EXPERT_ROW0 = N_GROUPS
ROUTE_ROWS = 48
INFO_E1, INFO_E2, INFO_R1, INFO_R2, INFO_W1, INFO_W2 = range(6)
INFO_ROWS = 8
MERGE_SUBTILES = 2


def _route(lt, carry, tri):
    rr, tm = lt.shape
    row = lax.broadcasted_iota(jnp.int32, (rr, tm), 0).astype(jnp.float32)
    none = jnp.float32(rr)

    def first_max(sel):
        m = jnp.max(jnp.where(sel, lt, NEG_BIG), axis=0, keepdims=True)
        idx = jnp.min(jnp.where(sel & (lt == m), row, none), axis=0, keepdims=True)
        return m, idx

    is_group = row < N_GROUPS
    mg, g = first_max(is_group)
    pg_sel = 1.0 / jnp.sum(jnp.where(is_group, jnp.exp(jnp.where(is_group, lt, mg) - mg), 0.0),
                           axis=0, keepdims=True)
    row0 = EXPERT_ROW0 + EXPERTS_PER_GROUP * g
    in_group = (row >= row0) & (row < row0 + EXPERTS_PER_GROUP)
    m1, i1 = first_max(in_group)
    m2, i2 = first_max(in_group & (row != i1))
    e2 = jnp.exp(m2 - m1)
    w1 = pg_sel / (1.0 + e2)
    w2 = pg_sel * e2 / (1.0 + e2)

    oh1 = row == i1
    oh2 = row == i2
    both = (oh1 | oh2).astype(jnp.bfloat16)
    before = jnp.dot(both, tri, preferred_element_type=jnp.float32) + carry
    r1 = jnp.sum(jnp.where(oh1, before, 0.0), axis=0, keepdims=True)
    r2 = jnp.sum(jnp.where(oh2, before, 0.0), axis=0, keepdims=True)
    new_carry = carry + jnp.sum(both.astype(jnp.float32), axis=1, keepdims=True)

    field = lax.broadcasted_iota(jnp.int32, (INFO_ROWS, tm), 0)
    info = jnp.zeros((INFO_ROWS, tm), jnp.float32)
    for k, v in ((INFO_E1, i1 - EXPERT_ROW0), (INFO_E2, i2 - EXPERT_ROW0), (INFO_R1, r1), (INFO_R2, r2),
                 (INFO_W1, w1), (INFO_W2, w2)):
        info = jnp.where(field == k, v, info)
    return info, new_carry


def _merge_kernel(x_ref, oa_ref, ob_ref, lng_ref, lnb_ref, wg_ref, wpa_ref, wpb_ref, wo_ref,
                  l1g_ref, l1b_ref, wr_ref, br_ref, cnt0_ref,
                  h1_ref, h1p_ref, info_ref, cnt_ref, carry_ref, tri_ref):
    tm = x_ref.shape[0]

    @pl.when(pl.program_id(0) == 0)
    def _():
        carry_ref[...] = cnt0_ref[...]
        r = lax.broadcasted_iota(jnp.int32, (tm, tm), 0)
        c = lax.broadcasted_iota(jnp.int32, (tm, tm), 1)
        tri_ref[...] = (r < c).astype(jnp.bfloat16)

    def project(rows):
        h = _layer_norm(x_ref[rows], lng_ref[...], lnb_ref[...])
        gates = jnp.dot(h.astype(jnp.bfloat16), wg_ref[...], preferred_element_type=jnp.float32)
        pa = jnp.dot(oa_ref[rows], wpa_ref[...], preferred_element_type=jnp.float32)
        pb = jnp.dot(ob_ref[rows], wpb_ref[...], preferred_element_type=jnp.float32)
        return h, gates, pa, pb

    def mix(h, gates, pa, pb):
        mixin = jax.nn.sigmoid(gates[:, :D_MODEL]) * pa + jax.nn.sigmoid(gates[:, D_MODEL:]) * pb
        return DEEPNORM_ALPHA * h + jnp.dot(mixin.astype(jnp.bfloat16), wo_ref[...],
                                            preferred_element_type=jnp.float32)

    def norm_and_logits(pre, rows):
        h1 = _layer_norm(pre, l1g_ref[...], l1b_ref[...])
        h1_ref[rows] = h1
        h1p_ref[rows] = _pack_rows(h1)
        return lax.dot_general(wr_ref[...], h1.astype(jnp.bfloat16), (((1,), (1,)), ((), ())),
                               preferred_element_type=jnp.float32)

    sub = tm // MERGE_SUBTILES
    parts = [slice(k * sub, (k + 1) * sub) for k in range(MERGE_SUBTILES)]
    projected = [project(rows) for rows in parts]
    mixed = [mix(*pr) for pr in projected]
    logits_t = jnp.concatenate([norm_and_logits(pre, rows) for pre, rows in zip(mixed, parts)], axis=1)
    logits_t = logits_t + br_ref[...]
    info, carry = _route(logits_t, carry_ref[...], tri_ref[...])
    info_ref[...] = info
    carry_ref[...] = carry
    cnt_ref[...] = carry[:, :LANES]


def _merge(x2, oa, ob, ln_g, ln_b, w_gates, w_pa, w_pb, w_o, l1g, l1b, w_r, b_r, cnt0, tm):
    n = x2.shape[0]

    def const(shape):
        return pl.BlockSpec(shape, lambda i: (0,) * len(shape))

    def rows(width):
        return pl.BlockSpec((tm, width), lambda i: (i, 0))

    return pl.pallas_call(
        _merge_kernel,
        out_shape=(jax.ShapeDtypeStruct((n, D_MODEL), jnp.float32),
                   jax.ShapeDtypeStruct((n, PACKED_WIDTH), jnp.uint32),
                   jax.ShapeDtypeStruct((INFO_ROWS, n), jnp.float32),
                   jax.ShapeDtypeStruct((ROUTE_ROWS, LANES), jnp.float32)),
        grid=(n // tm,),
        in_specs=[
            rows(D_MODEL), rows(WIDTH_A), rows(WIDTH_B),
            const((1, D_MODEL)), const((1, D_MODEL)),
            const((D_MODEL, 2 * D_MODEL)),
            const((WIDTH_A, D_MODEL)), const((WIDTH_B, D_MODEL)),
            const((D_MODEL, D_MODEL)),
            const((1, D_MODEL)), const((1, D_MODEL)),
            const((ROUTE_ROWS, D_MODEL)), const((ROUTE_ROWS, tm)), const((ROUTE_ROWS, tm)),
        ],
        out_specs=(rows(D_MODEL), rows(PACKED_WIDTH), pl.BlockSpec((INFO_ROWS, tm), lambda i: (0, i)),
                   const((ROUTE_ROWS, LANES))),
        scratch_shapes=[pltpu.VMEM((ROUTE_ROWS, tm), jnp.float32), pltpu.VMEM((tm, tm), jnp.bfloat16)],
        compiler_params=_cparams("arbitrary"),
        name="merge",
    )(x2, oa, ob, ln_g, ln_b, w_gates, w_pa, w_pb, w_o, l1g, l1b, w_r, b_r, cnt0)


SC_CORES = 2
SC_SUBCORES = 16
SC_WORKERS = SC_CORES * SC_SUBCORES
SC_ROWS_PER_STREAM = 64


def _sc_worker():
    return lax.axis_index("s") * SC_CORES + lax.axis_index("c")


def _dispatch(h1p, dest_t):
    n, width = h1p.shape
    per_worker = n // SC_WORKERS
    nchunks = per_worker // SC_ROWS_PER_STREAM
    assert nchunks * SC_ROWS_PER_STREAM * SC_WORKERS == n and nchunks % 2 == 0
    idx = dest_t.reshape(TOP_K, SC_WORKERS, nchunks, SC_ROWS_PER_STREAM)
    mesh = plsc.VectorSubcoreMesh(core_axis_name="c", subcore_axis_name="s")

    @functools.partial(
        pl.kernel, out_type=jax.ShapeDtypeStruct((TOP_K * n, width), h1p.dtype), mesh=mesh,
        scratch_types=[pltpu.VMEM((TOP_K, nchunks, SC_ROWS_PER_STREAM), jnp.int32),
                       pltpu.VMEM((2, SC_ROWS_PER_STREAM, width), h1p.dtype),
                       pltpu.SemaphoreType.DMA((2,)), pltpu.SemaphoreType.DMA((2,))],
        name="sc_dispatch")
    def scatter_kernel(src_hbm, idx_hbm, out_hbm, idx_v, rows_v, rsem, ssem):
        wid = _sc_worker()
        base = wid * per_worker
        for k in range(TOP_K):
            pltpu.sync_copy(idx_hbm.at[k, wid], idx_v.at[k])

        def read(j, slot):
            src = src_hbm.at[pl.ds(base + j * SC_ROWS_PER_STREAM, SC_ROWS_PER_STREAM)]
            return pltpu.make_async_copy(src, rows_v.at[slot], rsem.at[slot])

        def scatter(j, slot, k):
            return pltpu.make_async_copy(rows_v.at[slot], out_hbm.at[idx_v.at[k, j]], ssem.at[slot])

        read(0, 0).start()

        @pl.loop(0, nchunks, step=2)
        def _(j0):
            for slot in range(2):
                j = j0 + slot
                read(j, slot).wait()

                @pl.when(j + 1 < nchunks)
                def _():
                    @pl.when(j >= 1)
                    def _():
                        for k in range(TOP_K):
                            scatter(j - 1, 1 - slot, k).wait()
                    read(j + 1, 1 - slot).start()

                for k in range(TOP_K):
                    scatter(j, slot, k).start()

        for k in range(TOP_K):
            scatter(nchunks - 2, 0, k).wait()
            scatter(nchunks - 1, 1, k).wait()

    return scatter_kernel(h1p, idx)


def _moe_plan(counts, na, bm):
    expert = jnp.arange(N_EXPERTS, dtype=jnp.int32)
    upto = expert[None, :] <= expert[:, None]

    def running_sum(v):
        return jnp.sum(jnp.where(upto, v[None, :], 0), axis=1)

    ends = running_sum(counts)
    starts = ends - counts
    first_blk = starts // bm
    tiles = jnp.where(counts > 0, (ends - 1) // bm - first_blk + 1, 0)
    item_end = running_sum(tiles)
    item_start = item_end - tiles
    total = jnp.sum(tiles)
    wmax = na // bm + N_EXPERTS - 1
    w = jnp.arange(wmax, dtype=jnp.int32)
    wc = jnp.minimum(w, total - 1)
    e = jnp.sum((item_end[None, :] <= wc[:, None]).astype(jnp.int32), axis=1)
    e = jnp.minimum(e, N_EXPERTS - 1)
    owner = e[:, None] == expert[None, :]

    def of_owner(table):
        return jnp.sum(jnp.where(owner, table[None, :], 0), axis=1)

    blk = (of_owner(first_blk) + (wc - of_owner(item_start))).astype(jnp.int32)
    valid = w < total
    lo = jnp.where(valid, jnp.maximum(of_owner(starts), blk * bm), 0).astype(jnp.int32)
    hi = jnp.where(valid, jnp.minimum(of_owner(ends), (blk + 1) * bm), 0).astype(jnp.int32)
    prev_blk = jnp.concatenate([jnp.full((1,), -1, jnp.int32), blk[:-1]])
    prev_e = jnp.concatenate([jnp.full((1,), -1, jnp.int32), e[:-1]])
    new_expert = valid & (e != prev_e)
    flags = (valid.astype(jnp.int32)
             + FLAG_NEW_BLOCK * (valid & (blk != prev_blk)).astype(jnp.int32)
             + FLAG_NEW_EXPERT * new_expert.astype(jnp.int32))
    earlier = w[None, :] <= w[:, None]
    slot = (jnp.sum(jnp.where(earlier, new_expert[None, :].astype(jnp.int32), 0), axis=1) + 1) % 2
    later = valid[None, :] & (e[None, :] > e[:, None])
    nxt = jnp.min(jnp.where(later, e[None, :], N_EXPERTS), axis=1)
    nxt = jnp.where(valid & (nxt < N_EXPERTS), nxt, -1)
    return blk, e, lo, hi, flags, slot.astype(jnp.int32), nxt.astype(jnp.int32)


FLAG_VALID, FLAG_NEW_BLOCK, FLAG_NEW_EXPERT = 1, 2, 4


def _expert_kernel(blk_ref, e_ref, lo_ref, hi_ref, flag_ref, slot_ref, nxt_ref, x_ref, wg_hbm, wu_hbm, wd_hbm, o_ref,
                   wg_f, wu_f, wd_f, wg_b, wu_b, wd_b, wsem):
    w = pl.program_id(0)
    bm = x_ref.shape[0]
    flags = flag_ref[w]

    def weight_copies(expert, slot):
        return [pltpu.make_async_copy(hbm.at[expert], buf.at[slot], wsem.at[slot, m])
                for m, (hbm, buf) in enumerate(((wg_hbm, wg_f), (wu_hbm, wu_f), (wd_hbm, wd_f)))]

    @pl.when((flags & FLAG_NEW_EXPERT) != 0)
    def _():
        slot = slot_ref[w]

        @pl.when(w == 0)
        def _():
            for c in weight_copies(e_ref[w], slot):
                c.start()

        for c in weight_copies(e_ref[w], slot):
            c.wait()
        wg_b[...] = wg_f[slot].astype(jnp.bfloat16)
        wu_b[...] = wu_f[slot].astype(jnp.bfloat16)
        wd_b[...] = wd_f[slot].astype(jnp.bfloat16)

        @pl.when(nxt_ref[w] >= 0)
        def _():
            for c in weight_copies(nxt_ref[w], 1 - slot):
                c.start()

    @pl.when((flags & FLAG_VALID) != 0)
    def _():
        x = _unpack_rows(x_ref[...]).astype(jnp.bfloat16)
        g = jnp.dot(x, wg_b[...], preferred_element_type=jnp.float32)
        u = jnp.dot(x, wu_b[...], preferred_element_type=jnp.float32)
        hmid = (jax.nn.silu(g) * u).astype(jnp.bfloat16)
        y = _pack_rows(jnp.dot(hmid, wd_b[...], preferred_element_type=jnp.float32))
        row = blk_ref[w] * bm + lax.broadcasted_iota(jnp.int32, (bm, 1), 0)
        mine = (row >= lo_ref[w]) & (row < hi_ref[w])

        @pl.when((flags & FLAG_NEW_BLOCK) != 0)
        def _():
            o_ref[...] = jnp.where(mine, y, jnp.uint32(0))

        @pl.when((flags & FLAG_NEW_BLOCK) == 0)
        def _():
            o_ref[...] = jnp.where(mine, y, o_ref[...])


def _experts(xs, plan, w_gate, w_up, w_down, bm):
    na = xs.shape[0]
    nitems = plan[0].shape[0]
    up_shape, down_shape = (D_MODEL, D_EXPERT), (D_EXPERT, D_MODEL)
    grid_spec = pltpu.PrefetchScalarGridSpec(
        num_scalar_prefetch=len(plan),
        grid=(nitems,),
        in_specs=[
            pl.BlockSpec((bm, PACKED_WIDTH), lambda w, blk, *_: (blk[w], 0)),
            pl.BlockSpec(memory_space=pl.ANY),
            pl.BlockSpec(memory_space=pl.ANY),
            pl.BlockSpec(memory_space=pl.ANY),
        ],
        out_specs=pl.BlockSpec((bm, PACKED_WIDTH), lambda w, blk, *_: (blk[w], 0)),
        scratch_shapes=[pltpu.VMEM((2,) + up_shape, jnp.float32),
                        pltpu.VMEM((2,) + up_shape, jnp.float32),
                        pltpu.VMEM((2,) + down_shape, jnp.float32),
                        pltpu.VMEM(up_shape, jnp.bfloat16),
                        pltpu.VMEM(up_shape, jnp.bfloat16),
                        pltpu.VMEM(down_shape, jnp.bfloat16),
                        pltpu.SemaphoreType.DMA((2, 3))],
    )
    return pl.pallas_call(
        _expert_kernel,
        out_shape=jax.ShapeDtypeStruct((na, PACKED_WIDTH), jnp.uint32),
        grid_spec=grid_spec,
        compiler_params=_cparams("arbitrary"),
        name="experts",
    )(*plan, xs, w_gate, w_up, w_down)


def _sc_gather_rows(table, idx):
    nrows = idx.shape[0]
    width = table.shape[1]
    per_worker = nrows // SC_WORKERS
    nchunks = per_worker // SC_ROWS_PER_STREAM
    assert nchunks * SC_ROWS_PER_STREAM * SC_WORKERS == nrows and nchunks % 2 == 0
    mesh = plsc.VectorSubcoreMesh(core_axis_name="c", subcore_axis_name="s")

    @functools.partial(
        pl.kernel, out_type=jax.ShapeDtypeStruct((nrows, width), table.dtype), mesh=mesh,
        scratch_types=[pltpu.VMEM((per_worker,), jnp.int32),
                       pltpu.VMEM((2, SC_ROWS_PER_STREAM, width), table.dtype),
                       pltpu.SemaphoreType.DMA((2,)), pltpu.SemaphoreType.DMA((2,))],
        name="sc_gather")
    def gather_kernel(table_hbm, idx_hbm, out_hbm, idx_v, rows_v, gsem, wsem):
        base = _sc_worker() * per_worker
        pltpu.sync_copy(idx_hbm.at[pl.ds(base, per_worker)], idx_v)

        def gather(j, slot):
            rows = idx_v.at[pl.ds(j * SC_ROWS_PER_STREAM, SC_ROWS_PER_STREAM)]
            return pltpu.make_async_copy(table_hbm.at[rows], rows_v.at[slot], gsem.at[slot])

        def write(j, slot):
            dst = out_hbm.at[pl.ds(base + j * SC_ROWS_PER_STREAM, SC_ROWS_PER_STREAM)]
            return pltpu.make_async_copy(rows_v.at[slot], dst, wsem.at[slot])

        gather(0, 0).start()

        @pl.loop(0, nchunks, step=2)
        def _(j0):
            for slot in range(2):
                j = j0 + slot
                gather(j, slot).wait()

                @pl.when(j + 1 < nchunks)
                def _():
                    @pl.when(j >= 1)
                    def _():
                        write(j - 1, 1 - slot).wait()
                    gather(j + 1, 1 - slot).start()

                write(j, slot).start()

        write(nchunks - 2, 0).wait()
        write(nchunks - 1, 1).wait()

    return gather_kernel(table, idx)


def _finalize_kernel(h1_ref, y1_ref, y2_ref, info_ref, g_ref, b_ref, o_ref):
    tt = h1_ref.shape[0]
    pad = jnp.zeros((LANES - INFO_ROWS, tt), jnp.float32)
    info = jnp.concatenate([info_ref[...], pad], axis=0).T
    moe = (_unpack_rows(y1_ref[...]) * info[:, INFO_W1:INFO_W1 + 1]
           + _unpack_rows(y2_ref[...]) * info[:, INFO_W2:INFO_W2 + 1])
    o_ref[...] = _layer_norm(DEEPNORM_ALPHA * h1_ref[...] + moe, g_ref[...], b_ref[...])


def _combine(h1, info, dest_t, ys, ln_g, ln_b, tt):
    n = h1.shape[0]
    nsteps = n // tt
    yg = _sc_gather_rows(ys, dest_t.reshape(TOP_K * n))
    return pl.pallas_call(
        _finalize_kernel,
        out_shape=jax.ShapeDtypeStruct((n, D_MODEL), jnp.float32),
        grid=(nsteps,),
        in_specs=[
            pl.BlockSpec((tt, D_MODEL), lambda i: (i, 0)),
            pl.BlockSpec((tt, PACKED_WIDTH), lambda i: (i, 0)),
            pl.BlockSpec((tt, PACKED_WIDTH), lambda i: (nsteps + i, 0)),
            pl.BlockSpec((INFO_ROWS, tt), lambda i: (0, i)),
            pl.BlockSpec((1, D_MODEL), lambda i: (0, 0)),
            pl.BlockSpec((1, D_MODEL), lambda i: (0, 0)),
        ],
        out_specs=pl.BlockSpec((tt, D_MODEL), lambda i: (i, 0)),
        compiler_params=_cparams("parallel"),
        name="finalize",
    )(h1, yg, yg, info, ln_g, ln_b)


TM_QKV = 1024
TQ_WIN = 512
TM_MERGE = 1024
TT_ROWS = 1024
BM_EXPERT = 512


def _prepare_weights(ln_in_g, ln_in_b, w_in, attn_sink, rel_pos_bias, w_proj_a, w_proj_b, w_out,
                     ln1_g, ln1_b, w_route_group, b_route_group, w_route_expert, b_route_expert,
                     ln2_g, ln2_b):
    bf = jnp.bfloat16
    w = w_in[0]
    splits = np.cumsum([WIDTH_A, KV_WIDTH_A, KV_WIDTH_A, WIDTH_B, WIDTH_B, WIDTH_B, D_MODEL])
    wqa, wka, wva, wqb, wkb, wvb, wga, wgb = jnp.split(w, [int(s) for s in splits], axis=1)
    wqa = (wqa.reshape(D_MODEL, N_KV_HEADS_A, GQA_GROUP, HEAD_DIM).transpose(0, 2, 1, 3)
           .reshape(D_MODEL, WIDTH_A))
    w_qkv = jnp.concatenate([wqa, wqb, wkb, wvb, wka, wva], axis=1).astype(bf)
    w_gates = jnp.concatenate([wga, wgb], axis=1).astype(bf)
    w_pa = (w_proj_a[0].reshape(N_KV_HEADS_A, GQA_GROUP, HEAD_DIM, D_MODEL).transpose(1, 0, 2, 3)
            .reshape(WIDTH_A, D_MODEL).astype(bf))
    w_pb = w_proj_b[0].astype(bf)
    w_o = w_out[0].astype(bf)
    pad = ROUTE_ROWS - N_GROUPS - N_EXPERTS
    w_r = jnp.concatenate([w_route_group[0].T, w_route_expert[0].T, jnp.zeros((pad, D_MODEL), jnp.float32)], axis=0)
    w_route = w_r.astype(bf)
    b_r = jnp.concatenate([b_route_group[0], b_route_expert[0], jnp.zeros((pad,), jnp.float32)])
    b_r = jnp.broadcast_to(b_r[:, None], (ROUTE_ROWS, TM_MERGE))
    row = lambda v: v.reshape(1, D_MODEL)
    return dict(
        ln_in_g=row(ln_in_g), ln_in_b=row(ln_in_b), w_qkv=w_qkv, w_gates=w_gates,
        sink=attn_sink[0].astype(jnp.float32), nat_bias=_nat_bias_table(rel_pos_bias[0]),
        w_pa=w_pa, w_pb=w_pb, w_o=w_o, ln1_g=row(ln1_g[0]), ln1_b=row(ln1_b[0]),
        w_route=w_route, b_r=b_r, ln2_g=row(ln2_g[0]), ln2_b=row(ln2_b[0]))


def _after(value, other):
    if other is None:
        return value
    other = other.astype(jnp.float32)
    zero = jnp.where(jnp.isfinite(other), other, 0.0) * 0.0
    return value + zero.astype(value.dtype)


def _attend_and_route(x, p, after=None):
    bsz, t, _ = x.shape
    n = bsz * t
    x2 = x.reshape(n, D_MODEL)
    qkv = _qkv(x2, p["ln_in_g"], p["ln_in_b"], p["w_qkv"], TM_QKV)
    oa = _win_attention(qkv, p["sink"], bsz, t, TQ_WIN)
    ob = _nat_attention(qkv, p["nat_bias"], bsz, t)
    cnt0 = _after(jnp.zeros((ROUTE_ROWS, TM_MERGE), jnp.float32), after)
    h1, h1p, info, cnt = _merge(x2, oa, ob, p["ln_in_g"], p["ln_in_b"], p["w_gates"], p["w_pa"], p["w_pb"],
                                p["w_o"], p["ln1_g"], p["ln1_b"], p["w_route"], p["b_r"], cnt0, TM_MERGE)
    counts = cnt[EXPERT_ROW0:EXPERT_ROW0 + N_EXPERTS, 0].astype(jnp.int32)
    expert = jnp.arange(N_EXPERTS, dtype=jnp.int32)
    starts = jnp.sum(jnp.where(expert[None, :] < expert[:, None], counts[None, :], 0), axis=1)
    eid = info[INFO_E1:INFO_E2 + 1].astype(jnp.int32)
    rank = info[INFO_R1:INFO_R2 + 1].astype(jnp.int32)
    dest_t = rank + jnp.sum(jnp.where(eid[None] == expert[:, None, None], starts[:, None, None], 0), axis=0)
    return dict(shape=x.shape, h1=h1, h1p=h1p, info=info, counts=counts, dest_t=dest_t)


def _run_experts(r, w_gate, w_up, w_down, after=None):
    n = r["h1"].shape[0]
    xs = _dispatch(r["h1p"], r["dest_t"])
    blk, e, lo, hi, flags, slot, nxt = _moe_plan(r["counts"], TOP_K * n, BM_EXPERT)
    plan = (blk, e, lo, hi, _after(flags, after), slot, nxt)
    return _experts(xs, plan, w_gate[0], w_up[0], w_down[0], BM_EXPERT)


def _finish(r, ys, p, after=None):
    out = _combine(r["h1"], r["info"], r["dest_t"], ys, _after(p["ln2_g"], after), p["ln2_b"], TT_ROWS)
    return out.reshape(r["shape"])


def kernel(x_prompt, x_sample, ln_in_g, ln_in_b, w_in, attn_sink, rel_pos_bias, w_proj_a, w_proj_b, w_out,
           ln1_g, ln1_b, w_route_group, b_route_group, w_route_expert, b_route_expert,
           w_gate, w_up, w_down, ln2_g, ln2_b):
    p = _prepare_weights(ln_in_g, ln_in_b, w_in, attn_sink, rel_pos_bias, w_proj_a, w_proj_b, w_out,
                         ln1_g, ln1_b, w_route_group, b_route_group, w_route_expert, b_route_expert,
                         ln2_g, ln2_b)
    rp = _attend_and_route(x_prompt, p)
    rs = _attend_and_route(x_sample, p, after=rp["counts"][0])
    ys_p = _run_experts(rp, w_gate, w_up, w_down)
    ys_s = _run_experts(rs, w_gate, w_up, w_down, after=ys_p[0, 0])
    y_prompt = _finish(rp, ys_p, p)
    y_sample = _finish(rs, ys_s, p, after=y_prompt[0, 0, 0])
    return (y_prompt, y_sample)
```

```python
import functools

import numpy as np
import jax
import jax.numpy as jnp
from jax import lax
from jax.experimental import pallas as pl
from jax.experimental.pallas import tpu as pltpu
from jax.experimental.pallas import tpu_sc as plsc

D_MODEL = 1024
HEAD_DIM = 64
N_HEADS_A = 8
N_KV_HEADS_A = 2
WINDOW = 128
N_HEADS_B = 8
GRID_W = 64
NA_ROWS = 8
NA_COLS = 16
N_GROUPS = 4
EXPERTS_PER_GROUP = 8
N_EXPERTS = N_GROUPS * EXPERTS_PER_GROUP
TOP_K = 2
D_EXPERT = D_MODEL // 2
LN_EPS = 1e-5
DEPTH = 1
DEEPNORM_ALPHA = (2.0 * DEPTH) ** 0.25
WIDTH_A = N_HEADS_A * HEAD_DIM
KV_WIDTH_A = N_KV_HEADS_A * HEAD_DIM
WIDTH_B = N_HEADS_B * HEAD_DIM
QKV_WIDTH = WIDTH_A + 2 * KV_WIDTH_A + 3 * WIDTH_B

LANES = 128
VMEM_LIMIT_BYTES = 56 * 1024 * 1024

NEG_BIG = -1e30
LOG2E = float(np.log2(np.e))

QA_COL, QB_COL, KB_COL, VB_COL = 0, WIDTH_A, WIDTH_A + WIDTH_B, WIDTH_A + 2 * WIDTH_B
KA_COL = WIDTH_A + 3 * WIDTH_B
VA_COL = KA_COL + KV_WIDTH_A

GQA_GROUP = N_HEADS_A // N_KV_HEADS_A


def _cparams(*sem):
    return pltpu.CompilerParams(dimension_semantics=sem, vmem_limit_bytes=VMEM_LIMIT_BYTES)


def _layer_norm(x, g, b):
    mu = jnp.mean(x, axis=-1, keepdims=True)
    xc = x - mu
    var = jnp.mean(xc * xc, axis=-1, keepdims=True)
    return xc * lax.rsqrt(var + LN_EPS) * g + b


PACKED_WIDTH = D_MODEL // 2


def _pack_rows(x):
    def rne(v):
        return v + jnp.uint32(0x7FFF) + ((v >> 16) & jnp.uint32(1))
    hi = lax.bitcast_convert_type(x[:, :PACKED_WIDTH], jnp.uint32)
    lo = lax.bitcast_convert_type(x[:, PACKED_WIDTH:], jnp.uint32)
    return (rne(hi) & jnp.uint32(0xFFFF0000)) | (rne(lo) >> 16)


def _unpack_rows(w):
    hi = lax.bitcast_convert_type(w & jnp.uint32(0xFFFF0000), jnp.float32)
    lo = lax.bitcast_convert_type(w << 16, jnp.float32)
    return jnp.concatenate([hi, lo], axis=1)


def _qkv_kernel(x_ref, g_ref, b_ref, w_ref, *rest):
    o_ref = rest[len(rest) // 2]
    h = _layer_norm(x_ref[...], g_ref[...], b_ref[...])
    y = jnp.dot(h.astype(jnp.bfloat16), w_ref[...], preferred_element_type=jnp.float32)
    col = lax.broadcasted_iota(jnp.int32, (1, QKV_WIDTH), 1)
    y = y * jnp.where(col < KB_COL, HEAD_DIM ** -0.5 * LOG2E, 1.0)
    o_ref[...] = y.astype(jnp.bfloat16)
    ncast = len(rest) // 2
    for src_ref, dst_ref in zip(rest[:ncast], rest[ncast + 1:]):
        dst_ref[...] = src_ref[...].astype(jnp.bfloat16)


def _qkv(x2, ln_g, ln_b, w_qkv, tm, cast=()):
    n = x2.shape[0]
    steps = n // tm
    per_step = -(-N_EXPERTS // steps)
    assert all(c.shape[0] == N_EXPERTS for c in cast) and (steps * per_step) % N_EXPERTS == 0
    revisit = steps * per_step // N_EXPERTS

    def expert_block(c):
        return pl.BlockSpec((per_step,) + c.shape[1:], lambda i: (i // revisit, 0, 0))

    outs = pl.pallas_call(
        _qkv_kernel,
        out_shape=[jax.ShapeDtypeStruct((n, QKV_WIDTH), jnp.bfloat16)]
        + [jax.ShapeDtypeStruct(c.shape, jnp.bfloat16) for c in cast],
        grid=(steps,),
        in_specs=[
            pl.BlockSpec((tm, D_MODEL), lambda i: (i, 0)),
            pl.BlockSpec((1, D_MODEL), lambda i: (0, 0)),
            pl.BlockSpec((1, D_MODEL), lambda i: (0, 0)),
            pl.BlockSpec((D_MODEL, QKV_WIDTH), lambda i: (0, 0)),
        ] + [expert_block(c) for c in cast],
        out_specs=[pl.BlockSpec((tm, QKV_WIDTH), lambda i: (i, 0))] + [expert_block(c) for c in cast],
        compiler_params=_cparams("arbitrary" if cast else "parallel"),
        name="qkv",
    )(x2, ln_g, ln_b, w_qkv, *cast)
    return outs[0], tuple(outs[1:])


WIN_BLK = 128
WIN_LOOKAHEAD = 2


def _win_bias_table():
    qi = np.arange(WIN_BLK)[:, None]
    kj = np.arange(3 * WIN_BLK)[None, :]
    dist = np.abs(kj - WIN_BLK - qi).astype(np.float64)
    slopes = 2.0 ** (-8.0 * np.arange(1, N_HEADS_A + 1) / N_HEADS_A)
    per_head = np.where(dist <= WINDOW, -slopes[:, None, None] * dist[None] * LOG2E, NEG_BIG)
    groups = [np.concatenate([per_head[j], per_head[j + 4]], axis=0) for j in range(4)]
    return np.stack(groups).astype(np.float32)


def _win_kernel(sink_ref, q_ref, kp_ref, km_ref, kn_ref, vp_ref, vm_ref, vn_ref, bias_ref, o_ref,
                *, nsub, nblk_seq):
    i = pl.program_id(1)
    kcat = jnp.concatenate([kp_ref[...], km_ref[...], kn_ref[...]], axis=0)
    vcat = jnp.concatenate([vp_ref[...], vm_ref[...], vn_ref[...]], axis=0)
    lo = lax.broadcasted_iota(jnp.int32, (1, LANES), 1) < HEAD_DIM
    col = lax.broadcasted_iota(jnp.int32, (1, 3 * WIN_BLK), 1)
    top = lax.broadcasted_iota(jnp.int32, (2 * WIN_BLK, 1), 0) < WIN_BLK
    zero = jnp.zeros((), jnp.bfloat16)

    def scores(j, g):
        n = i * nsub + j
        off_seq = ((col < WIN_BLK) & (n == 0)) | ((col >= 2 * WIN_BLK) & (n == nblk_seq - 1))
        edge = jnp.where(off_seq, NEG_BIG, 0.0)
        qg = q_ref[WIN_BLK * j:WIN_BLK * (j + 1), LANES * g:LANES * (g + 1)]
        qm = jnp.concatenate([jnp.where(lo, qg, zero), jnp.where(lo, zero, qg)], axis=0)
        kj = kcat[WIN_BLK * j:WIN_BLK * (j + 3)]
        s = lax.dot_general(qm, kj, (((1,), (1,)), ((), ())), preferred_element_type=jnp.float32)
        return s + bias_ref[g] + edge

    def attend(s, j, g):
        vj = vcat[WIN_BLK * j:WIN_BLK * (j + 3)]
        sink = jnp.where(top, sink_ref[g], sink_ref[g + 4]) * LOG2E
        m = jnp.maximum(jnp.max(s, axis=-1, keepdims=True), sink)
        p = jnp.exp2(s - m)
        l = jnp.sum(p, axis=-1, keepdims=True) + jnp.exp2(sink - m)
        o2 = jnp.dot(p.astype(jnp.bfloat16), vj, preferred_element_type=jnp.float32)
        o2 = o2 * (1.0 / l)
        o_ref[WIN_BLK * j:WIN_BLK * (j + 1), LANES * g:LANES * (g + 1)] = (
            jnp.where(lo, o2[:WIN_BLK], o2[WIN_BLK:]).astype(jnp.bfloat16))

    chains = [(j, g) for j in range(nsub) for g in range(4)]
    pending = [scores(*c) for c in chains[:WIN_LOOKAHEAD]]
    for idx, c in enumerate(chains):
        s = pending.pop(0)
        if idx + WIN_LOOKAHEAD < len(chains):
            pending.append(scores(*chains[idx + WIN_LOOKAHEAD]))
        attend(s, *c)


def _win_attention(qkv, sink, bsz, t, tq):
    n = bsz * t
    nsub = tq // WIN_BLK
    nblk_seq = t // WIN_BLK
    ntile = t // tq
    bias = jnp.asarray(_win_bias_table())

    def main_map(col):
        return lambda b, i, *_: (b * ntile + i, col)

    def prev_map(col):
        return lambda b, i, *_: (b * nblk_seq + jnp.maximum(i * nsub - 1, 0), col)

    def next_map(col):
        return lambda b, i, *_: (b * nblk_seq + jnp.minimum(i * nsub + nsub, nblk_seq - 1), col)

    halo = (WIN_BLK, LANES)
    ka, va = KA_COL // LANES, VA_COL // LANES
    grid_spec = pltpu.PrefetchScalarGridSpec(
        num_scalar_prefetch=1,
        grid=(bsz, ntile),
        in_specs=[
            pl.BlockSpec((tq, WIDTH_A), main_map(QA_COL // WIDTH_A)),
            pl.BlockSpec(halo, prev_map(ka)),
            pl.BlockSpec((tq, LANES), main_map(ka)),
            pl.BlockSpec(halo, next_map(ka)),
            pl.BlockSpec(halo, prev_map(va)),
            pl.BlockSpec((tq, LANES), main_map(va)),
            pl.BlockSpec(halo, next_map(va)),
            pl.BlockSpec((4, 2 * WIN_BLK, 3 * WIN_BLK), lambda b, i, *_: (0, 0, 0)),
        ],
        out_specs=pl.BlockSpec((tq, WIDTH_A), main_map(0)),
    )
    return pl.pallas_call(
        functools.partial(_win_kernel, nsub=nsub, nblk_seq=nblk_seq),
        out_shape=jax.ShapeDtypeStruct((n, WIDTH_A), jnp.bfloat16),
        grid_spec=grid_spec,
        compiler_params=_cparams("parallel", "parallel"),
        name="win",
    )(sink, qkv, qkv, qkv, qkv, qkv, qkv, qkv, bias)


NAT_ROWS_PER_STEP = 8
NAT_HALO_ROWS = NA_ROWS // 2
NAT_KEYS = NA_ROWS * GRID_W
NAT_ROWS_PER_TRIP = 8
NAT_HEADS_PER_CHAIN = 2
NAT_LOOKAHEAD = 4


def _nat_bias_table(rpb):
    c = np.arange(GRID_W)
    cs = np.clip(c - NA_COLS // 2, 0, GRID_W - NA_COLS)
    col_mask = (c[None, :] >= cs[:, None]) & (c[None, :] < cs[:, None] + NA_COLS)
    dc = np.clip(c[None, :] - c[:, None] + (NA_COLS - 1), 0, 2 * NA_COLS - 2)
    onehot = jnp.asarray(dc[None] == np.arange(2 * NA_COLS - 1)[:, None, None], jnp.float32)
    picked = jnp.einsum("hdj,jqc->hdqc", rpb, onehot, precision=lax.Precision.HIGHEST)
    t1 = jnp.where(col_mask[None, None], picked * LOG2E, NEG_BIG)
    per_shift = []
    for sh in range(NA_ROWS):
        w = t1[:, sh:sh + NA_ROWS]
        w = jnp.transpose(w, (0, 2, 1, 3)).reshape(N_HEADS_B // 2, 2 * GRID_W, NAT_KEYS)
        per_shift.append(w)
    return jnp.stack(per_shift, axis=1).astype(jnp.float32)


def _nat_kernel(q_ref, kp_ref, km_ref, kn_ref, vp_ref, vm_ref, vn_ref, tb_ref, o_ref, kcat, vcat,
                *, rows_seq):
    i = pl.program_id(1)
    halo = NAT_HALO_ROWS * GRID_W
    main = NAT_ROWS_PER_STEP * GRID_W
    kcat[0:halo] = kp_ref[...]
    kcat[halo:halo + main] = km_ref[...]
    kcat[halo + main:2 * halo + main] = kn_ref[...]
    vcat[0:halo] = vp_ref[...]
    vcat[halo:halo + main] = vm_ref[...]
    vcat[halo + main:2 * halo + main] = vn_ref[...]
    width = NAT_HEADS_PER_CHAIN * HEAD_DIM
    head_of_lane = lax.broadcasted_iota(jnp.int32, (1, width), 1) // HEAD_DIM
    zero = jnp.zeros((), jnp.bfloat16)
    r0 = i * NAT_ROWS_PER_STEP

    def scores(qr, c):
        r = r0 + qr
        rs = jnp.clip(r - NA_ROWS // 2, 0, rows_seq - NA_ROWS)
        koff = pl.multiple_of((rs - r0 + NAT_HALO_ROWS) * GRID_W, GRID_W)
        sh = rs - r + (NA_ROWS - 1)
        qoff = pl.multiple_of(qr * GRID_W, GRID_W)
        cols = slice(width * c, width * (c + 1))
        qc = q_ref[pl.ds(qoff, GRID_W), cols]
        qm = jnp.concatenate([jnp.where(head_of_lane == h, qc, zero) for h in range(NAT_HEADS_PER_CHAIN)], axis=0)
        kw = kcat[pl.ds(koff, NAT_KEYS), cols]
        s = lax.dot_general(qm, kw, (((1,), (1,)), ((), ())), preferred_element_type=jnp.float32)
        pairs = NAT_HEADS_PER_CHAIN // 2
        bias = jnp.concatenate([tb_ref[pairs * c + k, sh] for k in range(pairs)], axis=0)
        return s + bias, koff, qoff

    def attend(s, koff, qoff, c):
        cols = slice(width * c, width * (c + 1))
        vw = vcat[pl.ds(koff, NAT_KEYS), cols]
        m = jnp.max(s, axis=-1, keepdims=True)
        pe = jnp.exp2(s - m)
        l = jnp.sum(pe, axis=-1, keepdims=True)
        o2 = jnp.dot(pe.astype(jnp.bfloat16), vw, preferred_element_type=jnp.float32)
        o2 = o2 * (1.0 / l)
        out = o2[:GRID_W]
        for h in range(1, NAT_HEADS_PER_CHAIN):
            out = jnp.where(head_of_lane == h, o2[GRID_W * h:GRID_W * (h + 1)], out)
        o_ref[pl.ds(qoff, GRID_W), cols] = out.astype(jnp.bfloat16)

    def trip(j, carry):
        chains = [(j * NAT_ROWS_PER_TRIP + q, c) for q in range(NAT_ROWS_PER_TRIP)
                  for c in range(N_HEADS_B // NAT_HEADS_PER_CHAIN)]
        pending = [scores(*c) for c in chains[:NAT_LOOKAHEAD]]
        for idx, (_, p) in enumerate(chains):
            s, koff, qoff = pending.pop(0)
            if idx + NAT_LOOKAHEAD < len(chains):
                pending.append(scores(*chains[idx + NAT_LOOKAHEAD]))
            attend(s, koff, qoff, p)
        return carry

    lax.fori_loop(0, NAT_ROWS_PER_STEP // NAT_ROWS_PER_TRIP, trip, 0)


def _nat_attention(qkv, tb, bsz, t):
    n = bsz * t
    rows_seq = t // GRID_W
    main = NAT_ROWS_PER_STEP * GRID_W
    halo = NAT_HALO_ROWS * GRID_W
    ntile = t // main
    nhalo_seq = t // halo
    per = main // halo

    def main_map(col):
        return lambda b, i: (b * ntile + i, col)

    def prev_map(col):
        return lambda b, i: (b * nhalo_seq + jnp.maximum(i * per - 1, 0), col)

    def next_map(col):
        return lambda b, i: (b * nhalo_seq + jnp.minimum(i * per + per, nhalo_seq - 1), col)

    qb, kb, vb = QB_COL // WIDTH_B, KB_COL // WIDTH_B, VB_COL // WIDTH_B
    return pl.pallas_call(
        functools.partial(_nat_kernel, rows_seq=rows_seq),
        out_shape=jax.ShapeDtypeStruct((n, WIDTH_B), jnp.bfloat16),
        grid=(bsz, ntile),
        in_specs=[
            pl.BlockSpec((main, WIDTH_B), main_map(qb)),
            pl.BlockSpec((halo, WIDTH_B), prev_map(kb)),
            pl.BlockSpec((main, WIDTH_B), main_map(kb)),
            pl.BlockSpec((halo, WIDTH_B), next_map(kb)),
            pl.BlockSpec((halo, WIDTH_B), prev_map(vb)),
            pl.BlockSpec((main, WIDTH_B), main_map(vb)),
            pl.BlockSpec((halo, WIDTH_B), next_map(vb)),
            pl.BlockSpec((N_HEADS_B // 2, NA_ROWS, 2 * GRID_W, NAT_KEYS), lambda b, i: (0, 0, 0, 0)),
        ],
        out_specs=pl.BlockSpec((main, WIDTH_B), main_map(0)),
        scratch_shapes=[pltpu.VMEM((main + 2 * halo, WIDTH_B), jnp.bfloat16),
                        pltpu.VMEM((main + 2 * halo, WIDTH_B), jnp.bfloat16)],
        compiler_params=_cparams("parallel", "parallel"),
        name="nat",
    )(qkv, qkv, qkv, qkv, qkv, qkv, qkv, tb)


EXPERT_ROW0 = N_GROUPS
ROUTE_ROWS = 48
INFO_E1, INFO_E2, INFO_R1, INFO_R2, INFO_W1, INFO_W2 = range(6)
INFO_ROWS = 8
MERGE_SUBTILES = 2


def _route(lt, carry, tri):
    rr, tm = lt.shape
    row = lax.broadcasted_iota(jnp.int32, (rr, tm), 0).astype(jnp.float32)
    none = jnp.float32(rr)

    def first_max(sel):
        m = jnp.max(jnp.where(sel, lt, NEG_BIG), axis=0, keepdims=True)
        idx = jnp.min(jnp.where(sel & (lt == m), row, none), axis=0, keepdims=True)
        return m, idx

    is_group = row < N_GROUPS
    mg, g = first_max(is_group)
    pg_sel = 1.0 / jnp.sum(jnp.where(is_group, jnp.exp(jnp.where(is_group, lt, mg) - mg), 0.0),
                           axis=0, keepdims=True)
    row0 = EXPERT_ROW0 + EXPERTS_PER_GROUP * g
    in_group = (row >= row0) & (row < row0 + EXPERTS_PER_GROUP)
    m1, i1 = first_max(in_group)
    m2, i2 = first_max(in_group & (row != i1))
    e2 = jnp.exp(m2 - m1)
    w1 = pg_sel / (1.0 + e2)
    w2 = pg_sel * e2 / (1.0 + e2)

    oh1 = row == i1
    oh2 = row == i2
    both = (oh1 | oh2).astype(jnp.bfloat16)
    before = jnp.dot(both, tri, preferred_element_type=jnp.float32) + carry
    r1 = jnp.sum(jnp.where(oh1, before, 0.0), axis=0, keepdims=True)
    r2 = jnp.sum(jnp.where(oh2, before, 0.0), axis=0, keepdims=True)
    new_carry = carry + jnp.sum(both.astype(jnp.float32), axis=1, keepdims=True)

    field = lax.broadcasted_iota(jnp.int32, (INFO_ROWS, tm), 0)
    info = jnp.zeros((INFO_ROWS, tm), jnp.float32)
    for k, v in ((INFO_E1, i1 - EXPERT_ROW0), (INFO_E2, i2 - EXPERT_ROW0), (INFO_R1, r1), (INFO_R2, r2),
                 (INFO_W1, w1), (INFO_W2, w2)):
        info = jnp.where(field == k, v, info)
    return info, new_carry


def _merge_kernel(x_ref, oa_ref, ob_ref, lng_ref, lnb_ref, wg_ref, wpa_ref, wpb_ref, wo_ref,
                  l1g_ref, l1b_ref, wr_ref, br_ref, cnt0_ref,
                  h1_ref, h1p_ref, info_ref, cnt_ref, carry_ref, tri_ref):
    tm = x_ref.shape[0]

    @pl.when(pl.program_id(0) == 0)
    def _():
        carry_ref[...] = cnt0_ref[...]
        r = lax.broadcasted_iota(jnp.int32, (tm, tm), 0)
        c = lax.broadcasted_iota(jnp.int32, (tm, tm), 1)
        tri_ref[...] = (r < c).astype(jnp.bfloat16)

    def project(rows):
        h = _layer_norm(x_ref[rows], lng_ref[...], lnb_ref[...])
        gates = jnp.dot(h.astype(jnp.bfloat16), wg_ref[...], preferred_element_type=jnp.float32)
        pa = jnp.dot(oa_ref[rows], wpa_ref[...], preferred_element_type=jnp.float32)
        pb = jnp.dot(ob_ref[rows], wpb_ref[...], preferred_element_type=jnp.float32)
        return h, gates, pa, pb

    def mix(h, gates, pa, pb):
        mixin = jax.nn.sigmoid(gates[:, :D_MODEL]) * pa + jax.nn.sigmoid(gates[:, D_MODEL:]) * pb
        return DEEPNORM_ALPHA * h + jnp.dot(mixin.astype(jnp.bfloat16), wo_ref[...],
                                            preferred_element_type=jnp.float32)

    def norm_and_logits(pre, rows):
        h1 = _layer_norm(pre, l1g_ref[...], l1b_ref[...])
        h1_ref[rows] = h1
        h1p_ref[rows] = _pack_rows(h1)
        return lax.dot_general(wr_ref[...], h1.astype(jnp.bfloat16), (((1,), (1,)), ((), ())),
                               preferred_element_type=jnp.float32)

    sub = tm // MERGE_SUBTILES
    parts = [slice(k * sub, (k + 1) * sub) for k in range(MERGE_SUBTILES)]
    projected = [project(rows) for rows in parts]
    mixed = [mix(*pr) for pr in projected]
    logits_t = jnp.concatenate([norm_and_logits(pre, rows) for pre, rows in zip(mixed, parts)], axis=1)
    logits_t = logits_t + br_ref[...]
    info, carry = _route(logits_t, carry_ref[...], tri_ref[...])
    info_ref[...] = info
    carry_ref[...] = carry
    cnt_ref[...] = carry[:, :LANES]


def _merge(x2, oa, ob, ln_g, ln_b, w_gates, w_pa, w_pb, w_o, l1g, l1b, w_r, b_r, cnt0, tm):
    n = x2.shape[0]

    def const(shape):
        return pl.BlockSpec(shape, lambda i: (0,) * len(shape))

    def rows(width):
        return pl.BlockSpec((tm, width), lambda i: (i, 0))

    return pl.pallas_call(
        _merge_kernel,
        out_shape=(jax.ShapeDtypeStruct((n, D_MODEL), jnp.float32),
                   jax.ShapeDtypeStruct((n, PACKED_WIDTH), jnp.uint32),
                   jax.ShapeDtypeStruct((INFO_ROWS, n), jnp.float32),
                   jax.ShapeDtypeStruct((ROUTE_ROWS, LANES), jnp.float32)),
        grid=(n // tm,),
        in_specs=[
            rows(D_MODEL), rows(WIDTH_A), rows(WIDTH_B),
            const((1, D_MODEL)), const((1, D_MODEL)),
            const((D_MODEL, 2 * D_MODEL)),
            const((WIDTH_A, D_MODEL)), const((WIDTH_B, D_MODEL)),
            const((D_MODEL, D_MODEL)),
            const((1, D_MODEL)), const((1, D_MODEL)),
            const((ROUTE_ROWS, D_MODEL)), const((ROUTE_ROWS, tm)), const((ROUTE_ROWS, tm)),
        ],
        out_specs=(rows(D_MODEL), rows(PACKED_WIDTH), pl.BlockSpec((INFO_ROWS, tm), lambda i: (0, i)),
                   const((ROUTE_ROWS, LANES))),
        scratch_shapes=[pltpu.VMEM((ROUTE_ROWS, tm), jnp.float32), pltpu.VMEM((tm, tm), jnp.bfloat16)],
        compiler_params=_cparams("arbitrary"),
        name="merge",
    )(x2, oa, ob, ln_g, ln_b, w_gates, w_pa, w_pb, w_o, l1g, l1b, w_r, b_r, cnt0)


SC_CORES = 2
SC_SUBCORES = 16
SC_WORKERS = SC_CORES * SC_SUBCORES
SC_ROWS_PER_STREAM = 64


def _sc_worker():
    return lax.axis_index("s") * SC_CORES + lax.axis_index("c")


def _dispatch(h1p, dest_t):
    n, width = h1p.shape
    per_worker = n // SC_WORKERS
    nchunks = per_worker // SC_ROWS_PER_STREAM
    assert nchunks * SC_ROWS_PER_STREAM * SC_WORKERS == n and nchunks % 2 == 0
    idx = dest_t.reshape(TOP_K, SC_WORKERS, nchunks, SC_ROWS_PER_STREAM)
    mesh = plsc.VectorSubcoreMesh(core_axis_name="c", subcore_axis_name="s")

    @functools.partial(
        pl.kernel, out_type=jax.ShapeDtypeStruct((TOP_K * n, width), h1p.dtype), mesh=mesh,
        scratch_types=[pltpu.VMEM((TOP_K, nchunks, SC_ROWS_PER_STREAM), jnp.int32),
                       pltpu.VMEM((2, SC_ROWS_PER_STREAM, width), h1p.dtype),
                       pltpu.SemaphoreType.DMA((2,)), pltpu.SemaphoreType.DMA((2,))],
        name="sc_dispatch")
    def scatter_kernel(src_hbm, idx_hbm, out_hbm, idx_v, rows_v, rsem, ssem):
        wid = _sc_worker()
        base = wid * per_worker
        for k in range(TOP_K):
            pltpu.sync_copy(idx_hbm.at[k, wid], idx_v.at[k])

        def read(j, slot):
            src = src_hbm.at[pl.ds(base + j * SC_ROWS_PER_STREAM, SC_ROWS_PER_STREAM)]
            return pltpu.make_async_copy(src, rows_v.at[slot], rsem.at[slot])

        def scatter(j, slot, k):
            return pltpu.make_async_copy(rows_v.at[slot], out_hbm.at[idx_v.at[k, j]], ssem.at[slot])

        read(0, 0).start()

        @pl.loop(0, nchunks, step=2)
        def _(j0):
            for slot in range(2):
                j = j0 + slot
                read(j, slot).wait()

                @pl.when(j + 1 < nchunks)
                def _():
                    @pl.when(j >= 1)
                    def _():
                        for k in range(TOP_K):
                            scatter(j - 1, 1 - slot, k).wait()
                    read(j + 1, 1 - slot).start()

                for k in range(TOP_K):
                    scatter(j, slot, k).start()

        for k in range(TOP_K):
            scatter(nchunks - 2, 0, k).wait()
            scatter(nchunks - 1, 1, k).wait()

    return scatter_kernel(h1p, idx)


def _moe_plan(counts, na, bm):
    expert = jnp.arange(N_EXPERTS, dtype=jnp.int32)
    upto = expert[None, :] <= expert[:, None]

    def running_sum(v):
        return jnp.sum(jnp.where(upto, v[None, :], 0), axis=1)

    ends = running_sum(counts)
    starts = ends - counts
    first_blk = starts // bm
    tiles = jnp.where(counts > 0, (ends - 1) // bm - first_blk + 1, 0)
    item_end = running_sum(tiles)
    item_start = item_end - tiles
    total = jnp.sum(tiles)
    wmax = na // bm + N_EXPERTS - 1
    w = jnp.arange(wmax, dtype=jnp.int32)
    wc = jnp.minimum(w, total - 1)
    e = jnp.sum((item_end[None, :] <= wc[:, None]).astype(jnp.int32), axis=1)
    e = jnp.minimum(e, N_EXPERTS - 1)
    owner = e[:, None] == expert[None, :]

    def of_owner(table):
        return jnp.sum(jnp.where(owner, table[None, :], 0), axis=1)

    blk = (of_owner(first_blk) + (wc - of_owner(item_start))).astype(jnp.int32)
    valid = w < total
    lo = jnp.where(valid, jnp.maximum(of_owner(starts), blk * bm), 0).astype(jnp.int32)
    hi = jnp.where(valid, jnp.minimum(of_owner(ends), (blk + 1) * bm), 0).astype(jnp.int32)
    prev_blk = jnp.concatenate([jnp.full((1,), -1, jnp.int32), blk[:-1]])
    prev_e = jnp.concatenate([jnp.full((1,), -1, jnp.int32), e[:-1]])
    flags = (valid.astype(jnp.int32)
             + FLAG_NEW_BLOCK * (valid & (blk != prev_blk)).astype(jnp.int32))
    return blk, e, lo, hi, flags


FLAG_VALID, FLAG_NEW_BLOCK = 1, 2


def _expert_kernel(blk_ref, e_ref, lo_ref, hi_ref, flag_ref, x_ref, wg_ref, wu_ref, wd_ref, o_ref):
    w = pl.program_id(0)
    bm = x_ref.shape[0]
    flags = flag_ref[w]

    @pl.when((flags & FLAG_VALID) != 0)
    def _():
        x = _unpack_rows(x_ref[...]).astype(jnp.bfloat16)
        g = jnp.dot(x, wg_ref[0], preferred_element_type=jnp.float32)
        u = jnp.dot(x, wu_ref[0], preferred_element_type=jnp.float32)
        hmid = (jax.nn.silu(g) * u).astype(jnp.bfloat16)
        y = _pack_rows(jnp.dot(hmid, wd_ref[0], preferred_element_type=jnp.float32))
        row = blk_ref[w] * bm + lax.broadcasted_iota(jnp.int32, (bm, 1), 0)
        mine = (row >= lo_ref[w]) & (row < hi_ref[w])

        @pl.when((flags & FLAG_NEW_BLOCK) != 0)
        def _():
            o_ref[...] = jnp.where(mine, y, jnp.uint32(0))

        @pl.when((flags & FLAG_NEW_BLOCK) == 0)
        def _():
            o_ref[...] = jnp.where(mine, y, o_ref[...])


def _experts(xs, plan, w_gate, w_up, w_down, bm):
    na = xs.shape[0]
    nitems = plan[0].shape[0]

    def weights(shape):
        return pl.BlockSpec((1,) + shape, lambda w, blk, e, *_: (e[w], 0, 0))

    grid_spec = pltpu.PrefetchScalarGridSpec(
        num_scalar_prefetch=len(plan),
        grid=(nitems,),
        in_specs=[
            pl.BlockSpec((bm, PACKED_WIDTH), lambda w, blk, *_: (blk[w], 0)),
            weights((D_MODEL, D_EXPERT)), weights((D_MODEL, D_EXPERT)), weights((D_EXPERT, D_MODEL)),
        ],
        out_specs=pl.BlockSpec((bm, PACKED_WIDTH), lambda w, blk, *_: (blk[w], 0)),
    )
    return pl.pallas_call(
        _expert_kernel,
        out_shape=jax.ShapeDtypeStruct((na, PACKED_WIDTH), jnp.uint32),
        grid_spec=grid_spec,
        compiler_params=_cparams("arbitrary"),
        name="experts",
    )(*plan, xs, w_gate, w_up, w_down)


def _sc_gather_rows(table, idx):
    nrows = idx.shape[0]
    width = table.shape[1]
    per_worker = nrows // SC_WORKERS
    nchunks = per_worker // SC_ROWS_PER_STREAM
    assert nchunks * SC_ROWS_PER_STREAM * SC_WORKERS == nrows and nchunks % 2 == 0
    mesh = plsc.VectorSubcoreMesh(core_axis_name="c", subcore_axis_name="s")

    @functools.partial(
        pl.kernel, out_type=jax.ShapeDtypeStruct((nrows, width), table.dtype), mesh=mesh,
        scratch_types=[pltpu.VMEM((per_worker,), jnp.int32),
                       pltpu.VMEM((2, SC_ROWS_PER_STREAM, width), table.dtype),
                       pltpu.SemaphoreType.DMA((2,)), pltpu.SemaphoreType.DMA((2,))],
        name="sc_gather")
    def gather_kernel(table_hbm, idx_hbm, out_hbm, idx_v, rows_v, gsem, wsem):
        base = _sc_worker() * per_worker
        pltpu.sync_copy(idx_hbm.at[pl.ds(base, per_worker)], idx_v)

        def gather(j, slot):
            rows = idx_v.at[pl.ds(j * SC_ROWS_PER_STREAM, SC_ROWS_PER_STREAM)]
            return pltpu.make_async_copy(table_hbm.at[rows], rows_v.at[slot], gsem.at[slot])

        def write(j, slot):
            dst = out_hbm.at[pl.ds(base + j * SC_ROWS_PER_STREAM, SC_ROWS_PER_STREAM)]
            return pltpu.make_async_copy(rows_v.at[slot], dst, wsem.at[slot])

        gather(0, 0).start()

        @pl.loop(0, nchunks, step=2)
        def _(j0):
            for slot in range(2):
                j = j0 + slot
                gather(j, slot).wait()

                @pl.when(j + 1 < nchunks)
                def _():
                    @pl.when(j >= 1)
                    def _():
                        write(j - 1, 1 - slot).wait()
                    gather(j + 1, 1 - slot).start()

                write(j, slot).start()

        write(nchunks - 2, 0).wait()
        write(nchunks - 1, 1).wait()

    return gather_kernel(table, idx)


def _finalize_kernel(h1_ref, y1_ref, y2_ref, info_ref, g_ref, b_ref, o_ref):
    tt = h1_ref.shape[0]
    pad = jnp.zeros((LANES - INFO_ROWS, tt), jnp.float32)
    info = jnp.concatenate([info_ref[...], pad], axis=0).T
    moe = (_unpack_rows(y1_ref[...]) * info[:, INFO_W1:INFO_W1 + 1]
           + _unpack_rows(y2_ref[...]) * info[:, INFO_W2:INFO_W2 + 1])
    o_ref[...] = _layer_norm(DEEPNORM_ALPHA * h1_ref[...] + moe, g_ref[...], b_ref[...])


def _combine(h1, info, dest_t, ys, ln_g, ln_b, tt):
    n = h1.shape[0]
    nsteps = n // tt
    yg = _sc_gather_rows(ys, dest_t.reshape(TOP_K * n))
    return pl.pallas_call(
        _finalize_kernel,
        out_shape=jax.ShapeDtypeStruct((n, D_MODEL), jnp.float32),
        grid=(nsteps,),
        in_specs=[
            pl.BlockSpec((tt, D_MODEL), lambda i: (i, 0)),
            pl.BlockSpec((tt, PACKED_WIDTH), lambda i: (i, 0)),
            pl.BlockSpec((tt, PACKED_WIDTH), lambda i: (nsteps + i, 0)),
            pl.BlockSpec((INFO_ROWS, tt), lambda i: (0, i)),
            pl.BlockSpec((1, D_MODEL), lambda i: (0, 0)),
            pl.BlockSpec((1, D_MODEL), lambda i: (0, 0)),
        ],
        out_specs=pl.BlockSpec((tt, D_MODEL), lambda i: (i, 0)),
        compiler_params=_cparams("parallel"),
        name="finalize",
    )(h1, yg, yg, info, ln_g, ln_b)


TM_QKV = 1024
TQ_WIN = 512
TM_MERGE = 1024
TT_ROWS = 1024
BM_EXPERT = 512


def _prepare_weights(ln_in_g, ln_in_b, w_in, attn_sink, rel_pos_bias, w_proj_a, w_proj_b, w_out,
                     ln1_g, ln1_b, w_route_group, b_route_group, w_route_expert, b_route_expert,
                     ln2_g, ln2_b):
    bf = jnp.bfloat16
    w = w_in[0]
    splits = np.cumsum([WIDTH_A, KV_WIDTH_A, KV_WIDTH_A, WIDTH_B, WIDTH_B, WIDTH_B, D_MODEL])
    wqa, wka, wva, wqb, wkb, wvb, wga, wgb = jnp.split(w, [int(s) for s in splits], axis=1)
    wqa = (wqa.reshape(D_MODEL, N_KV_HEADS_A, GQA_GROUP, HEAD_DIM).transpose(0, 2, 1, 3)
           .reshape(D_MODEL, WIDTH_A))
    w_qkv = jnp.concatenate([wqa, wqb, wkb, wvb, wka, wva], axis=1).astype(bf)
    w_gates = jnp.concatenate([wga, wgb], axis=1).astype(bf)
    w_pa = (w_proj_a[0].reshape(N_KV_HEADS_A, GQA_GROUP, HEAD_DIM, D_MODEL).transpose(1, 0, 2, 3)
            .reshape(WIDTH_A, D_MODEL).astype(bf))
    w_pb = w_proj_b[0].astype(bf)
    w_o = w_out[0].astype(bf)
    pad = ROUTE_ROWS - N_GROUPS - N_EXPERTS
    w_r = jnp.concatenate([w_route_group[0].T, w_route_expert[0].T, jnp.zeros((pad, D_MODEL), jnp.float32)], axis=0)
    w_route = w_r.astype(bf)
    b_r = jnp.concatenate([b_route_group[0], b_route_expert[0], jnp.zeros((pad,), jnp.float32)])
    b_r = jnp.broadcast_to(b_r[:, None], (ROUTE_ROWS, TM_MERGE))
    row = lambda v: v.reshape(1, D_MODEL)
    return dict(
        ln_in_g=row(ln_in_g), ln_in_b=row(ln_in_b), w_qkv=w_qkv, w_gates=w_gates,
        sink=attn_sink[0].astype(jnp.float32), nat_bias=_nat_bias_table(rel_pos_bias[0]),
        w_pa=w_pa, w_pb=w_pb, w_o=w_o, ln1_g=row(ln1_g[0]), ln1_b=row(ln1_b[0]),
        w_route=w_route, b_r=b_r, ln2_g=row(ln2_g[0]), ln2_b=row(ln2_b[0]))


def _after(value, other):
    if other is None:
        return value
    other = other.astype(jnp.float32)
    zero = jnp.where(jnp.isfinite(other), other, 0.0) * 0.0
    return value + zero.astype(value.dtype)


def _attend_and_route(x, p, after=None, cast=()):
    bsz, t, _ = x.shape
    n = bsz * t
    x2 = x.reshape(n, D_MODEL)
    qkv, cast_bf16 = _qkv(x2, p["ln_in_g"], p["ln_in_b"], p["w_qkv"], TM_QKV, cast)
    oa = _win_attention(qkv, p["sink"], bsz, t, TQ_WIN)
    ob = _nat_attention(qkv, p["nat_bias"], bsz, t)
    cnt0 = _after(jnp.zeros((ROUTE_ROWS, TM_MERGE), jnp.float32), after)
    h1, h1p, info, cnt = _merge(x2, oa, ob, p["ln_in_g"], p["ln_in_b"], p["w_gates"], p["w_pa"], p["w_pb"],
                                p["w_o"], p["ln1_g"], p["ln1_b"], p["w_route"], p["b_r"], cnt0, TM_MERGE)
    counts = cnt[EXPERT_ROW0:EXPERT_ROW0 + N_EXPERTS, 0].astype(jnp.int32)
    expert = jnp.arange(N_EXPERTS, dtype=jnp.int32)
    starts = jnp.sum(jnp.where(expert[None, :] < expert[:, None], counts[None, :], 0), axis=1)
    eid = info[INFO_E1:INFO_E2 + 1].astype(jnp.int32)
    rank = info[INFO_R1:INFO_R2 + 1].astype(jnp.int32)
    dest_t = rank + jnp.sum(jnp.where(eid[None] == expert[:, None, None], starts[:, None, None], 0), axis=0)
    return dict(shape=x.shape, h1=h1, h1p=h1p, info=info, counts=counts, dest_t=dest_t, cast=cast_bf16)


def _run_experts(r, expert_weights, after=None):
    n = r["h1"].shape[0]
    xs = _dispatch(r["h1p"], r["dest_t"])
    blk, e, lo, hi, flags = _moe_plan(r["counts"], TOP_K * n, BM_EXPERT)
    plan = (blk, e, lo, hi, _after(flags, after))
    return _experts(xs, plan, *expert_weights, BM_EXPERT)


def _finish(r, ys, p, after=None):
    out = _combine(r["h1"], r["info"], r["dest_t"], ys, _after(p["ln2_g"], after), p["ln2_b"], TT_ROWS)
    return out.reshape(r["shape"])


def kernel(x_prompt, x_sample, ln_in_g, ln_in_b, w_in, attn_sink, rel_pos_bias, w_proj_a, w_proj_b, w_out,
           ln1_g, ln1_b, w_route_group, b_route_group, w_route_expert, b_route_expert,
           w_gate, w_up, w_down, ln2_g, ln2_b):
    p = _prepare_weights(ln_in_g, ln_in_b, w_in, attn_sink, rel_pos_bias, w_proj_a, w_proj_b, w_out,
                         ln1_g, ln1_b, w_route_group, b_route_group, w_route_expert, b_route_expert,
                         ln2_g, ln2_b)
    rp = _attend_and_route(x_prompt, p, cast=(w_gate[0], w_up[0], w_down[0]))
    rs = _attend_and_route(x_sample, p, after=rp["counts"][0])
    ys_p = _run_experts(rp, rp["cast"])
    ys_s = _run_experts(rs, rp["cast"], after=ys_p[0, 0])
    y_prompt = _finish(rp, ys_p, p)
    y_sample = _finish(rs, ys_s, p, after=y_prompt[0, 0, 0])
    return (y_prompt, y_sample)
```

```python
import functools

import numpy as np
import jax
import jax.numpy as jnp
from jax import lax
from jax.experimental import pallas as pl
from jax.experimental.pallas import tpu as pltpu
from jax.experimental.pallas import tpu_sc as plsc

D_MODEL = 1024
HEAD_DIM = 64
N_HEADS_A = 8
N_KV_HEADS_A = 2
WINDOW = 128
N_HEADS_B = 8
GRID_W = 64
NA_ROWS = 8
NA_COLS = 16
N_GROUPS = 4
EXPERTS_PER_GROUP = 8
N_EXPERTS = N_GROUPS * EXPERTS_PER_GROUP
TOP_K = 2
D_EXPERT = D_MODEL // 2
LN_EPS = 1e-5
DEPTH = 1
DEEPNORM_ALPHA = (2.0 * DEPTH) ** 0.25
WIDTH_A = N_HEADS_A * HEAD_DIM
KV_WIDTH_A = N_KV_HEADS_A * HEAD_DIM
WIDTH_B = N_HEADS_B * HEAD_DIM
QKV_WIDTH = WIDTH_A + 2 * KV_WIDTH_A + 3 * WIDTH_B

LANES = 128
VMEM_LIMIT_BYTES = 56 * 1024 * 1024

NEG_BIG = -1e30
LOG2E = float(np.log2(np.e))

QA_COL, QB_COL, KB_COL, VB_COL = 0, WIDTH_A, WIDTH_A + WIDTH_B, WIDTH_A + 2 * WIDTH_B
KA_COL = WIDTH_A + 3 * WIDTH_B
VA_COL = KA_COL + KV_WIDTH_A

GQA_GROUP = N_HEADS_A // N_KV_HEADS_A


def _cparams(*sem):
    return pltpu.CompilerParams(dimension_semantics=sem, vmem_limit_bytes=VMEM_LIMIT_BYTES)


def _layer_norm(x, g, b):
    mu = jnp.mean(x, axis=-1, keepdims=True)
    xc = x - mu
    var = jnp.mean(xc * xc, axis=-1, keepdims=True)
    return xc * lax.rsqrt(var + LN_EPS) * g + b


PACKED_WIDTH = D_MODEL // 2


def _pack_rows(x):
    def rne(v):
        return v + jnp.uint32(0x7FFF) + ((v >> 16) & jnp.uint32(1))
    hi = lax.bitcast_convert_type(x[:, :PACKED_WIDTH], jnp.uint32)
    lo = lax.bitcast_convert_type(x[:, PACKED_WIDTH:], jnp.uint32)
    return (rne(hi) & jnp.uint32(0xFFFF0000)) | (rne(lo) >> 16)


def _unpack_rows(w):
    hi = lax.bitcast_convert_type(w & jnp.uint32(0xFFFF0000), jnp.float32)
    lo = lax.bitcast_convert_type(w << 16, jnp.float32)
    return jnp.concatenate([hi, lo], axis=1)


def _qkv_kernel(x_ref, g_ref, b_ref, w_ref, *rest):
    o_ref = rest[len(rest) // 2]
    h = _layer_norm(x_ref[...], g_ref[...], b_ref[...])
    y = jnp.dot(h.astype(jnp.bfloat16), w_ref[...], preferred_element_type=jnp.float32)
    col = lax.broadcasted_iota(jnp.int32, (1, QKV_WIDTH), 1)
    y = y * jnp.where(col < KB_COL, HEAD_DIM ** -0.5 * LOG2E, 1.0)
    o_ref[...] = y.astype(jnp.bfloat16)
    ncast = len(rest) // 2
    for src_ref, dst_ref in zip(rest[:ncast], rest[ncast + 1:]):
        dst_ref[...] = src_ref[...].astype(jnp.bfloat16)


def _qkv(x2, ln_g, ln_b, w_qkv, tm, cast=()):
    n = x2.shape[0]
    steps = n // tm
    per_step = -(-N_EXPERTS // steps)
    assert all(c.shape[0] == N_EXPERTS for c in cast) and (steps * per_step) % N_EXPERTS == 0
    revisit = steps * per_step // N_EXPERTS

    def expert_block(c):
        return pl.BlockSpec((per_step,) + c.shape[1:], lambda i: (i // revisit, 0, 0))

    outs = pl.pallas_call(
        _qkv_kernel,
        out_shape=[jax.ShapeDtypeStruct((n, QKV_WIDTH), jnp.bfloat16)]
        + [jax.ShapeDtypeStruct(c.shape, jnp.bfloat16) for c in cast],
        grid=(steps,),
        in_specs=[
            pl.BlockSpec((tm, D_MODEL), lambda i: (i, 0)),
            pl.BlockSpec((1, D_MODEL), lambda i: (0, 0)),
            pl.BlockSpec((1, D_MODEL), lambda i: (0, 0)),
            pl.BlockSpec((D_MODEL, QKV_WIDTH), lambda i: (0, 0)),
        ] + [expert_block(c) for c in cast],
        out_specs=[pl.BlockSpec((tm, QKV_WIDTH), lambda i: (i, 0))] + [expert_block(c) for c in cast],
        compiler_params=_cparams("arbitrary" if cast else "parallel"),
        name="qkv",
    )(x2, ln_g, ln_b, w_qkv, *cast)
    return outs[0], tuple(outs[1:])


WIN_BLK = 128
WIN_LOOKAHEAD = 2


def _win_bias_table():
    qi = np.arange(WIN_BLK)[:, None]
    kj = np.arange(3 * WIN_BLK)[None, :]
    dist = np.abs(kj - WIN_BLK - qi).astype(np.float64)
    slopes = 2.0 ** (-8.0 * np.arange(1, N_HEADS_A + 1) / N_HEADS_A)
    per_head = np.where(dist <= WINDOW, -slopes[:, None, None] * dist[None] * LOG2E, NEG_BIG)
    groups = [np.concatenate([per_head[j], per_head[j + 4]], axis=0) for j in range(4)]
    return np.stack(groups).astype(np.float32)


def _win_kernel(sink_ref, q_ref, kp_ref, km_ref, kn_ref, vp_ref, vm_ref, vn_ref, bias_ref, o_ref,
                *, nsub, nblk_seq):
    i = pl.program_id(1)
    kcat = jnp.concatenate([kp_ref[...], km_ref[...], kn_ref[...]], axis=0)
    vcat = jnp.concatenate([vp_ref[...], vm_ref[...], vn_ref[...]], axis=0)
    lo = lax.broadcasted_iota(jnp.int32, (1, LANES), 1) < HEAD_DIM
    col = lax.broadcasted_iota(jnp.int32, (1, 3 * WIN_BLK), 1)
    top = lax.broadcasted_iota(jnp.int32, (2 * WIN_BLK, 1), 0) < WIN_BLK
    zero = jnp.zeros((), jnp.bfloat16)

    def scores(j, g):
        n = i * nsub + j
        off_seq = ((col < WIN_BLK) & (n == 0)) | ((col >= 2 * WIN_BLK) & (n == nblk_seq - 1))
        edge = jnp.where(off_seq, NEG_BIG, 0.0)
        qg = q_ref[WIN_BLK * j:WIN_BLK * (j + 1), LANES * g:LANES * (g + 1)]
        qm = jnp.concatenate([jnp.where(lo, qg, zero), jnp.where(lo, zero, qg)], axis=0)
        kj = kcat[WIN_BLK * j:WIN_BLK * (j + 3)]
        s = lax.dot_general(qm, kj, (((1,), (1,)), ((), ())), preferred_element_type=jnp.float32)
        return s + bias_ref[g] + edge

    def attend(s, j, g):
        vj = vcat[WIN_BLK * j:WIN_BLK * (j + 3)]
        sink = jnp.where(top, sink_ref[g], sink_ref[g + 4]) * LOG2E
        m = jnp.maximum(jnp.max(s, axis=-1, keepdims=True), sink)
        p = jnp.exp2(s - m)
        l = jnp.sum(p, axis=-1, keepdims=True) + jnp.exp2(sink - m)
        o2 = jnp.dot(p.astype(jnp.bfloat16), vj, preferred_element_type=jnp.float32)
        o2 = o2 * (1.0 / l)
        o_ref[WIN_BLK * j:WIN_BLK * (j + 1), LANES * g:LANES * (g + 1)] = (
            jnp.where(lo, o2[:WIN_BLK], o2[WIN_BLK:]).astype(jnp.bfloat16))

    chains = [(j, g) for j in range(nsub) for g in range(4)]
    pending = [scores(*c) for c in chains[:WIN_LOOKAHEAD]]
    for idx, c in enumerate(chains):
        s = pending.pop(0)
        if idx + WIN_LOOKAHEAD < len(chains):
            pending.append(scores(*chains[idx + WIN_LOOKAHEAD]))
        attend(s, *c)


def _win_attention(qkv, sink, bsz, t, tq):
    n = bsz * t
    nsub = tq // WIN_BLK
    nblk_seq = t // WIN_BLK
    ntile = t // tq
    bias = jnp.asarray(_win_bias_table())

    def main_map(col):
        return lambda b, i, *_: (b * ntile + i, col)

    def prev_map(col):
        return lambda b, i, *_: (b * nblk_seq + jnp.maximum(i * nsub - 1, 0), col)

    def next_map(col):
        return lambda b, i, *_: (b * nblk_seq + jnp.minimum(i * nsub + nsub, nblk_seq - 1), col)

    halo = (WIN_BLK, LANES)
    ka, va = KA_COL // LANES, VA_COL // LANES
    grid_spec = pltpu.PrefetchScalarGridSpec(
        num_scalar_prefetch=1,
        grid=(bsz, ntile),
        in_specs=[
            pl.BlockSpec((tq, WIDTH_A), main_map(QA_COL // WIDTH_A)),
            pl.BlockSpec(halo, prev_map(ka)),
            pl.BlockSpec((tq, LANES), main_map(ka)),
            pl.BlockSpec(halo, next_map(ka)),
            pl.BlockSpec(halo, prev_map(va)),
            pl.BlockSpec((tq, LANES), main_map(va)),
            pl.BlockSpec(halo, next_map(va)),
            pl.BlockSpec((4, 2 * WIN_BLK, 3 * WIN_BLK), lambda b, i, *_: (0, 0, 0)),
        ],
        out_specs=pl.BlockSpec((tq, WIDTH_A), main_map(0)),
    )
    return pl.pallas_call(
        functools.partial(_win_kernel, nsub=nsub, nblk_seq=nblk_seq),
        out_shape=jax.ShapeDtypeStruct((n, WIDTH_A), jnp.bfloat16),
        grid_spec=grid_spec,
        compiler_params=_cparams("parallel", "parallel"),
        name="win",
    )(sink, qkv, qkv, qkv, qkv, qkv, qkv, qkv, bias)


NAT_ROWS_PER_STEP = 16
NAT_HALO_ROWS = NA_ROWS // 2
NAT_KEYS = NA_ROWS * GRID_W
NAT_ROWS_PER_TRIP = 8
NAT_HEADS_PER_CHAIN = 2
NAT_LOOKAHEAD = 4


def _nat_bias_table(rpb):
    c = np.arange(GRID_W)
    cs = np.clip(c - NA_COLS // 2, 0, GRID_W - NA_COLS)
    col_mask = (c[None, :] >= cs[:, None]) & (c[None, :] < cs[:, None] + NA_COLS)
    dc = np.clip(c[None, :] - c[:, None] + (NA_COLS - 1), 0, 2 * NA_COLS - 2)
    onehot = jnp.asarray(dc[None] == np.arange(2 * NA_COLS - 1)[:, None, None], jnp.float32)
    picked = jnp.einsum("hdj,jqc->hdqc", rpb, onehot, precision=lax.Precision.HIGHEST)
    t1 = jnp.where(col_mask[None, None], picked * LOG2E, NEG_BIG)
    per_shift = []
    for sh in range(NA_ROWS):
        w = t1[:, sh:sh + NA_ROWS]
        w = jnp.transpose(w, (0, 2, 1, 3)).reshape(N_HEADS_B // 2, 2 * GRID_W, NAT_KEYS)
        per_shift.append(w)
    return jnp.stack(per_shift, axis=1).astype(jnp.float32)


def _nat_kernel(q_ref, kp_ref, km_ref, kn_ref, vp_ref, vm_ref, vn_ref, tb_ref, o_ref, kcat, vcat,
                *, rows_seq):
    i = pl.program_id(1)
    halo = NAT_HALO_ROWS * GRID_W
    main = NAT_ROWS_PER_STEP * GRID_W
    kcat[0:halo] = kp_ref[...]
    kcat[halo:halo + main] = km_ref[...]
    kcat[halo + main:2 * halo + main] = kn_ref[...]
    vcat[0:halo] = vp_ref[...]
    vcat[halo:halo + main] = vm_ref[...]
    vcat[halo + main:2 * halo + main] = vn_ref[...]
    width = NAT_HEADS_PER_CHAIN * HEAD_DIM
    head_of_lane = lax.broadcasted_iota(jnp.int32, (1, width), 1) // HEAD_DIM
    zero = jnp.zeros((), jnp.bfloat16)
    r0 = i * NAT_ROWS_PER_STEP

    def scores(qr, c):
        r = r0 + qr
        rs = jnp.clip(r - NA_ROWS // 2, 0, rows_seq - NA_ROWS)
        koff = pl.multiple_of((rs - r0 + NAT_HALO_ROWS) * GRID_W, GRID_W)
        sh = rs - r + (NA_ROWS - 1)
        qoff = pl.multiple_of(qr * GRID_W, GRID_W)
        cols = slice(width * c, width * (c + 1))
        qc = q_ref[pl.ds(qoff, GRID_W), cols]
        qm = jnp.concatenate([jnp.where(head_of_lane == h, qc, zero) for h in range(NAT_HEADS_PER_CHAIN)], axis=0)
        kw = kcat[pl.ds(koff, NAT_KEYS), cols]
        s = lax.dot_general(qm, kw, (((1,), (1,)), ((), ())), preferred_element_type=jnp.float32)
        pairs = NAT_HEADS_PER_CHAIN // 2
        bias = jnp.concatenate([tb_ref[pairs * c + k, sh] for k in range(pairs)], axis=0)
        return s + bias, koff, qoff

    def attend(s, koff, qoff, c):
        cols = slice(width * c, width * (c + 1))
        vw = vcat[pl.ds(koff, NAT_KEYS), cols]
        m = jnp.max(s, axis=-1, keepdims=True)
        pe = jnp.exp2(s - m)
        l = jnp.sum(pe, axis=-1, keepdims=True)
        o2 = jnp.dot(pe.astype(jnp.bfloat16), vw, preferred_element_type=jnp.float32)
        o2 = o2 * (1.0 / l)
        out = o2[:GRID_W]
        for h in range(1, NAT_HEADS_PER_CHAIN):
            out = jnp.where(head_of_lane == h, o2[GRID_W * h:GRID_W * (h + 1)], out)
        o_ref[pl.ds(qoff, GRID_W), cols] = out.astype(jnp.bfloat16)

    def trip(j, carry):
        chains = [(j * NAT_ROWS_PER_TRIP + q, c) for q in range(NAT_ROWS_PER_TRIP)
                  for c in range(N_HEADS_B // NAT_HEADS_PER_CHAIN)]
        pending = [scores(*c) for c in chains[:NAT_LOOKAHEAD]]
        for idx, (_, p) in enumerate(chains):
            s, koff, qoff = pending.pop(0)
            if idx + NAT_LOOKAHEAD < len(chains):
                pending.append(scores(*chains[idx + NAT_LOOKAHEAD]))
            attend(s, koff, qoff, p)
        return carry

    lax.fori_loop(0, NAT_ROWS_PER_STEP // NAT_ROWS_PER_TRIP, trip, 0)


def _nat_attention(qkv, tb, bsz, t):
    n = bsz * t
    rows_seq = t // GRID_W
    main = NAT_ROWS_PER_STEP * GRID_W
    halo = NAT_HALO_ROWS * GRID_W
    ntile = t // main
    nhalo_seq = t // halo
    per = main // halo

    def main_map(col):
        return lambda b, i: (b * ntile + i, col)

    def prev_map(col):
        return lambda b, i: (b * nhalo_seq + jnp.maximum(i * per - 1, 0), col)

    def next_map(col):
        return lambda b, i: (b * nhalo_seq + jnp.minimum(i * per + per, nhalo_seq - 1), col)

    qb, kb, vb = QB_COL // WIDTH_B, KB_COL // WIDTH_B, VB_COL // WIDTH_B
    return pl.pallas_call(
        functools.partial(_nat_kernel, rows_seq=rows_seq),
        out_shape=jax.ShapeDtypeStruct((n, WIDTH_B), jnp.bfloat16),
        grid=(bsz, ntile),
        in_specs=[
            pl.BlockSpec((main, WIDTH_B), main_map(qb)),
            pl.BlockSpec((halo, WIDTH_B), prev_map(kb)),
            pl.BlockSpec((main, WIDTH_B), main_map(kb)),
            pl.BlockSpec((halo, WIDTH_B), next_map(kb)),
            pl.BlockSpec((halo, WIDTH_B), prev_map(vb)),
            pl.BlockSpec((main, WIDTH_B), main_map(vb)),
            pl.BlockSpec((halo, WIDTH_B), next_map(vb)),
            pl.BlockSpec((N_HEADS_B // 2, NA_ROWS, 2 * GRID_W, NAT_KEYS), lambda b, i: (0, 0, 0, 0)),
        ],
        out_specs=pl.BlockSpec((main, WIDTH_B), main_map(0)),
        scratch_shapes=[pltpu.VMEM((main + 2 * halo, WIDTH_B), jnp.bfloat16),
                        pltpu.VMEM((main + 2 * halo, WIDTH_B), jnp.bfloat16)],
        compiler_params=_cparams("parallel", "parallel"),
        name="nat",
    )(qkv, qkv, qkv, qkv, qkv, qkv, qkv, tb)


EXPERT_ROW0 = N_GROUPS
ROUTE_ROWS = 48
INFO_E1, INFO_E2, INFO_R1, INFO_R2, INFO_W1, INFO_W2 = range(6)
INFO_ROWS = 8
MERGE_SUBTILES = 4


def _route(lt, carry, tri):
    rr, tm = lt.shape
    row = lax.broadcasted_iota(jnp.int32, (rr, tm), 0).astype(jnp.float32)
    none = jnp.float32(rr)

    def first_max(sel):
        m = jnp.max(jnp.where(sel, lt, NEG_BIG), axis=0, keepdims=True)
        idx = jnp.min(jnp.where(sel & (lt == m), row, none), axis=0, keepdims=True)
        return m, idx

    is_group = row < N_GROUPS
    mg, g = first_max(is_group)
    pg_sel = 1.0 / jnp.sum(jnp.where(is_group, jnp.exp(jnp.where(is_group, lt, mg) - mg), 0.0),
                           axis=0, keepdims=True)
    row0 = EXPERT_ROW0 + EXPERTS_PER_GROUP * g
    in_group = (row >= row0) & (row < row0 + EXPERTS_PER_GROUP)
    m1, i1 = first_max(in_group)
    m2, i2 = first_max(in_group & (row != i1))
    e2 = jnp.exp(m2 - m1)
    w1 = pg_sel / (1.0 + e2)
    w2 = pg_sel * e2 / (1.0 + e2)

    oh1 = row == i1
    oh2 = row == i2
    both = (oh1 | oh2).astype(jnp.bfloat16)
    before = jnp.dot(both, tri, preferred_element_type=jnp.float32) + carry
    r1 = jnp.sum(jnp.where(oh1, before, 0.0), axis=0, keepdims=True)
    r2 = jnp.sum(jnp.where(oh2, before, 0.0), axis=0, keepdims=True)
    new_carry = carry + jnp.sum(both.astype(jnp.float32), axis=1, keepdims=True)

    field = lax.broadcasted_iota(jnp.int32, (INFO_ROWS, tm), 0)
    info = jnp.zeros((INFO_ROWS, tm), jnp.float32)
    for k, v in ((INFO_E1, i1 - EXPERT_ROW0), (INFO_E2, i2 - EXPERT_ROW0), (INFO_R1, r1), (INFO_R2, r2),
                 (INFO_W1, w1), (INFO_W2, w2)):
        info = jnp.where(field == k, v, info)
    return info, new_carry


def _merge_kernel(x_ref, oa_ref, ob_ref, lng_ref, lnb_ref, wg_ref, wpa_ref, wpb_ref, wo_ref,
                  l1g_ref, l1b_ref, wr_ref, br_ref, cnt0_ref,
                  h1_ref, h1p_ref, info_ref, cnt_ref, carry_ref, tri_ref):
    tm = x_ref.shape[0]

    @pl.when(pl.program_id(0) == 0)
    def _():
        carry_ref[...] = cnt0_ref[...]
        r = lax.broadcasted_iota(jnp.int32, (tm, tm), 0)
        c = lax.broadcasted_iota(jnp.int32, (tm, tm), 1)
        tri_ref[...] = (r < c).astype(jnp.bfloat16)

    def project(rows):
        h = _layer_norm(x_ref[rows], lng_ref[...], lnb_ref[...])
        gates = jnp.dot(h.astype(jnp.bfloat16), wg_ref[...], preferred_element_type=jnp.float32)
        pa = jnp.dot(oa_ref[rows], wpa_ref[...], preferred_element_type=jnp.float32)
        pb = jnp.dot(ob_ref[rows], wpb_ref[...], preferred_element_type=jnp.float32)
        return h, gates, pa, pb

    def mix(h, gates, pa, pb):
        mixin = jax.nn.sigmoid(gates[:, :D_MODEL]) * pa + jax.nn.sigmoid(gates[:, D_MODEL:]) * pb
        return DEEPNORM_ALPHA * h + jnp.dot(mixin.astype(jnp.bfloat16), wo_ref[...],
                                            preferred_element_type=jnp.float32)

    def norm_and_logits(pre, rows):
        h1 = _layer_norm(pre, l1g_ref[...], l1b_ref[...])
        h1_ref[rows] = h1
        h1p_ref[rows] = _pack_rows(h1)
        return lax.dot_general(wr_ref[...], h1.astype(jnp.bfloat16), (((1,), (1,)), ((), ())),
                               preferred_element_type=jnp.float32)

    sub = tm // MERGE_SUBTILES
    parts = [slice(k * sub, (k + 1) * sub) for k in range(MERGE_SUBTILES)]
    projected = [project(rows) for rows in parts]
    mixed = [mix(*pr) for pr in projected]
    logits_t = jnp.concatenate([norm_and_logits(pre, rows) for pre, rows in zip(mixed, parts)], axis=1)
    logits_t = logits_t + br_ref[...]
    info, carry = _route(logits_t, carry_ref[...], tri_ref[...])
    info_ref[...] = info
    carry_ref[...] = carry
    cnt_ref[...] = carry[:, :LANES]


def _merge(x2, oa, ob, ln_g, ln_b, w_gates, w_pa, w_pb, w_o, l1g, l1b, w_r, b_r, cnt0, tm):
    n = x2.shape[0]

    def const(shape):
        return pl.BlockSpec(shape, lambda i: (0,) * len(shape))

    def rows(width):
        return pl.BlockSpec((tm, width), lambda i: (i, 0))

    return pl.pallas_call(
        _merge_kernel,
        out_shape=(jax.ShapeDtypeStruct((n, D_MODEL), jnp.float32),
                   jax.ShapeDtypeStruct((n, PACKED_WIDTH), jnp.uint32),
                   jax.ShapeDtypeStruct((INFO_ROWS, n), jnp.float32),
                   jax.ShapeDtypeStruct((ROUTE_ROWS, LANES), jnp.float32)),
        grid=(n // tm,),
        in_specs=[
            rows(D_MODEL), rows(WIDTH_A), rows(WIDTH_B),
            const((1, D_MODEL)), const((1, D_MODEL)),
            const((D_MODEL, 2 * D_MODEL)),
            const((WIDTH_A, D_MODEL)), const((WIDTH_B, D_MODEL)),
            const((D_MODEL, D_MODEL)),
            const((1, D_MODEL)), const((1, D_MODEL)),
            const((ROUTE_ROWS, D_MODEL)), const((ROUTE_ROWS, tm)), const((ROUTE_ROWS, tm)),
        ],
        out_specs=(rows(D_MODEL), rows(PACKED_WIDTH), pl.BlockSpec((INFO_ROWS, tm), lambda i: (0, i)),
                   const((ROUTE_ROWS, LANES))),
        scratch_shapes=[pltpu.VMEM((ROUTE_ROWS, tm), jnp.float32), pltpu.VMEM((tm, tm), jnp.bfloat16)],
        compiler_params=_cparams("arbitrary"),
        name="merge",
    )(x2, oa, ob, ln_g, ln_b, w_gates, w_pa, w_pb, w_o, l1g, l1b, w_r, b_r, cnt0)


SC_CORES = 2
SC_SUBCORES = 16
SC_WORKERS = SC_CORES * SC_SUBCORES
SC_ROWS_PER_STREAM = 64


def _sc_worker():
    return lax.axis_index("s") * SC_CORES + lax.axis_index("c")


def _dispatch(h1p, dest_t):
    n, width = h1p.shape
    per_worker = n // SC_WORKERS
    nchunks = per_worker // SC_ROWS_PER_STREAM
    assert nchunks * SC_ROWS_PER_STREAM * SC_WORKERS == n and nchunks % 2 == 0
    idx = dest_t.reshape(TOP_K, SC_WORKERS, nchunks, SC_ROWS_PER_STREAM)
    mesh = plsc.VectorSubcoreMesh(core_axis_name="c", subcore_axis_name="s")

    @functools.partial(
        pl.kernel, out_type=jax.ShapeDtypeStruct((TOP_K * n, width), h1p.dtype), mesh=mesh,
        scratch_types=[pltpu.VMEM((TOP_K, nchunks, SC_ROWS_PER_STREAM), jnp.int32),
                       pltpu.VMEM((2, SC_ROWS_PER_STREAM, width), h1p.dtype),
                       pltpu.SemaphoreType.DMA((2,)), pltpu.SemaphoreType.DMA((2,))],
        name="sc_dispatch")
    def scatter_kernel(src_hbm, idx_hbm, out_hbm, idx_v, rows_v, rsem, ssem):
        wid = _sc_worker()
        base = wid * per_worker
        for k in range(TOP_K):
            pltpu.sync_copy(idx_hbm.at[k, wid], idx_v.at[k])

        def read(j, slot):
            src = src_hbm.at[pl.ds(base + j * SC_ROWS_PER_STREAM, SC_ROWS_PER_STREAM)]
            return pltpu.make_async_copy(src, rows_v.at[slot], rsem.at[slot])

        def scatter(j, slot, k):
            return pltpu.make_async_copy(rows_v.at[slot], out_hbm.at[idx_v.at[k, j]], ssem.at[slot])

        read(0, 0).start()

        @pl.loop(0, nchunks, step=2)
        def _(j0):
            for slot in range(2):
                j = j0 + slot
                read(j, slot).wait()

                @pl.when(j + 1 < nchunks)
                def _():
                    @pl.when(j >= 1)
                    def _():
                        for k in range(TOP_K):
                            scatter(j - 1, 1 - slot, k).wait()
                    read(j + 1, 1 - slot).start()

                for k in range(TOP_K):
                    scatter(j, slot, k).start()

        for k in range(TOP_K):
            scatter(nchunks - 2, 0, k).wait()
            scatter(nchunks - 1, 1, k).wait()

    return scatter_kernel(h1p, idx)


def _moe_plan(counts, na, bm):
    expert = jnp.arange(N_EXPERTS, dtype=jnp.int32)
    upto = expert[None, :] <= expert[:, None]

    def running_sum(v):
        return jnp.sum(jnp.where(upto, v[None, :], 0), axis=1)

    ends = running_sum(counts)
    starts = ends - counts
    first_blk = starts // bm
    tiles = jnp.where(counts > 0, (ends - 1) // bm - first_blk + 1, 0)
    item_end = running_sum(tiles)
    item_start = item_end - tiles
    total = jnp.sum(tiles)
    wmax = na // bm + N_EXPERTS - 1
    w = jnp.arange(wmax, dtype=jnp.int32)
    wc = jnp.minimum(w, total - 1)
    e = jnp.sum((item_end[None, :] <= wc[:, None]).astype(jnp.int32), axis=1)
    e = jnp.minimum(e, N_EXPERTS - 1)
    owner = e[:, None] == expert[None, :]

    def of_owner(table):
        return jnp.sum(jnp.where(owner, table[None, :], 0), axis=1)

    blk = (of_owner(first_blk) + (wc - of_owner(item_start))).astype(jnp.int32)
    valid = w < total
    lo = jnp.where(valid, jnp.maximum(of_owner(starts), blk * bm), 0).astype(jnp.int32)
    hi = jnp.where(valid, jnp.minimum(of_owner(ends), (blk + 1) * bm), 0).astype(jnp.int32)
    prev_blk = jnp.concatenate([jnp.full((1,), -1, jnp.int32), blk[:-1]])
    prev_e = jnp.concatenate([jnp.full((1,), -1, jnp.int32), e[:-1]])
    flags = (valid.astype(jnp.int32)
             + FLAG_NEW_BLOCK * (valid & (blk != prev_blk)).astype(jnp.int32))
    return blk, e, lo, hi, flags


FLAG_VALID, FLAG_NEW_BLOCK = 1, 2


def _expert_kernel(blk_ref, e_ref, lo_ref, hi_ref, flag_ref, x_ref, wg_ref, wu_ref, wd_ref, o_ref):
    w = pl.program_id(0)
    bm = x_ref.shape[0]
    flags = flag_ref[w]

    @pl.when((flags & FLAG_VALID) != 0)
    def _():
        x = _unpack_rows(x_ref[...]).astype(jnp.bfloat16)
        g = jnp.dot(x, wg_ref[0], preferred_element_type=jnp.float32)
        u = jnp.dot(x, wu_ref[0], preferred_element_type=jnp.float32)
        hmid = (jax.nn.silu(g) * u).astype(jnp.bfloat16)
        y = _pack_rows(jnp.dot(hmid, wd_ref[0], preferred_element_type=jnp.float32))
        row = blk_ref[w] * bm + lax.broadcasted_iota(jnp.int32, (bm, 1), 0)
        mine = (row >= lo_ref[w]) & (row < hi_ref[w])

        @pl.when((flags & FLAG_NEW_BLOCK) != 0)
        def _():
            o_ref[...] = jnp.where(mine, y, jnp.uint32(0))

        @pl.when((flags & FLAG_NEW_BLOCK) == 0)
        def _():
            o_ref[...] = jnp.where(mine, y, o_ref[...])


def _experts(xs, plan, w_gate, w_up, w_down, bm):
    na = xs.shape[0]
    nitems = plan[0].shape[0]

    def weights(shape):
        return pl.BlockSpec((1,) + shape, lambda w, blk, e, *_: (e[w], 0, 0))

    grid_spec = pltpu.PrefetchScalarGridSpec(
        num_scalar_prefetch=len(plan),
        grid=(nitems,),
        in_specs=[
            pl.BlockSpec((bm, PACKED_WIDTH), lambda w, blk, *_: (blk[w], 0)),
            weights((D_MODEL, D_EXPERT)), weights((D_MODEL, D_EXPERT)), weights((D_EXPERT, D_MODEL)),
        ],
        out_specs=pl.BlockSpec((bm, PACKED_WIDTH), lambda w, blk, *_: (blk[w], 0)),
    )
    return pl.pallas_call(
        _expert_kernel,
        out_shape=jax.ShapeDtypeStruct((na, PACKED_WIDTH), jnp.uint32),
        grid_spec=grid_spec,
        compiler_params=_cparams("arbitrary"),
        name="experts",
    )(*plan, xs, w_gate, w_up, w_down)


def _sc_gather_rows(table, idx):
    nrows = idx.shape[0]
    width = table.shape[1]
    per_worker = nrows // SC_WORKERS
    nchunks = per_worker // SC_ROWS_PER_STREAM
    assert nchunks * SC_ROWS_PER_STREAM * SC_WORKERS == nrows and nchunks % 2 == 0
    mesh = plsc.VectorSubcoreMesh(core_axis_name="c", subcore_axis_name="s")

    @functools.partial(
        pl.kernel, out_type=jax.ShapeDtypeStruct((nrows, width), table.dtype), mesh=mesh,
        scratch_types=[pltpu.VMEM((per_worker,), jnp.int32),
                       pltpu.VMEM((2, SC_ROWS_PER_STREAM, width), table.dtype),
                       pltpu.SemaphoreType.DMA((2,)), pltpu.SemaphoreType.DMA((2,))],
        name="sc_gather")
    def gather_kernel(table_hbm, idx_hbm, out_hbm, idx_v, rows_v, gsem, wsem):
        base = _sc_worker() * per_worker
        pltpu.sync_copy(idx_hbm.at[pl.ds(base, per_worker)], idx_v)

        def gather(j, slot):
            rows = idx_v.at[pl.ds(j * SC_ROWS_PER_STREAM, SC_ROWS_PER_STREAM)]
            return pltpu.make_async_copy(table_hbm.at[rows], rows_v.at[slot], gsem.at[slot])

        def write(j, slot):
            dst = out_hbm.at[pl.ds(base + j * SC_ROWS_PER_STREAM, SC_ROWS_PER_STREAM)]
            return pltpu.make_async_copy(rows_v.at[slot], dst, wsem.at[slot])

        gather(0, 0).start()

        @pl.loop(0, nchunks, step=2)
        def _(j0):
            for slot in range(2):
                j = j0 + slot
                gather(j, slot).wait()

                @pl.when(j + 1 < nchunks)
                def _():
                    @pl.when(j >= 1)
                    def _():
                        write(j - 1, 1 - slot).wait()
                    gather(j + 1, 1 - slot).start()

                write(j, slot).start()

        write(nchunks - 2, 0).wait()
        write(nchunks - 1, 1).wait()

    return gather_kernel(table, idx)


def _finalize_kernel(h1_ref, y1_ref, y2_ref, info_ref, g_ref, b_ref, o_ref):
    tt = h1_ref.shape[0]
    pad = jnp.zeros((LANES - INFO_ROWS, tt), jnp.float32)
    info = jnp.concatenate([info_ref[...], pad], axis=0).T
    moe = (_unpack_rows(y1_ref[...]) * info[:, INFO_W1:INFO_W1 + 1]
           + _unpack_rows(y2_ref[...]) * info[:, INFO_W2:INFO_W2 + 1])
    o_ref[...] = _layer_norm(DEEPNORM_ALPHA * h1_ref[...] + moe, g_ref[...], b_ref[...])


def _combine(h1, info, dest_t, ys, ln_g, ln_b, tt):
    n = h1.shape[0]
    nsteps = n // tt
    yg = _sc_gather_rows(ys, dest_t.reshape(TOP_K * n))
    return pl.pallas_call(
        _finalize_kernel,
        out_shape=jax.ShapeDtypeStruct((n, D_MODEL), jnp.float32),
        grid=(nsteps,),
        in_specs=[
            pl.BlockSpec((tt, D_MODEL), lambda i: (i, 0)),
            pl.BlockSpec((tt, PACKED_WIDTH), lambda i: (i, 0)),
            pl.BlockSpec((tt, PACKED_WIDTH), lambda i: (nsteps + i, 0)),
            pl.BlockSpec((INFO_ROWS, tt), lambda i: (0, i)),
            pl.BlockSpec((1, D_MODEL), lambda i: (0, 0)),
            pl.BlockSpec((1, D_MODEL), lambda i: (0, 0)),
        ],
        out_specs=pl.BlockSpec((tt, D_MODEL), lambda i: (i, 0)),
        compiler_params=_cparams("parallel"),
        name="finalize",
    )(h1, yg, yg, info, ln_g, ln_b)


TM_QKV = 1024
TQ_WIN = 1024
TM_MERGE = 1024
TT_ROWS = 1024
BM_EXPERT = 512


def _prepare_weights(ln_in_g, ln_in_b, w_in, attn_sink, rel_pos_bias, w_proj_a, w_proj_b, w_out,
                     ln1_g, ln1_b, w_route_group, b_route_group, w_route_expert, b_route_expert,
                     ln2_g, ln2_b):
    bf = jnp.bfloat16
    w = w_in[0]
    splits = np.cumsum([WIDTH_A, KV_WIDTH_A, KV_WIDTH_A, WIDTH_B, WIDTH_B, WIDTH_B, D_MODEL])
    wqa, wka, wva, wqb, wkb, wvb, wga, wgb = jnp.split(w, [int(s) for s in splits], axis=1)
    wqa = (wqa.reshape(D_MODEL, N_KV_HEADS_A, GQA_GROUP, HEAD_DIM).transpose(0, 2, 1, 3)
           .reshape(D_MODEL, WIDTH_A))
    w_qkv = jnp.concatenate([wqa, wqb, wkb, wvb, wka, wva], axis=1).astype(bf)
    w_gates = jnp.concatenate([wga, wgb], axis=1).astype(bf)
    w_pa = (w_proj_a[0].reshape(N_KV_HEADS_A, GQA_GROUP, HEAD_DIM, D_MODEL).transpose(1, 0, 2, 3)
            .reshape(WIDTH_A, D_MODEL).astype(bf))
    w_pb = w_proj_b[0].astype(bf)
    w_o = w_out[0].astype(bf)
    pad = ROUTE_ROWS - N_GROUPS - N_EXPERTS
    w_r = jnp.concatenate([w_route_group[0].T, w_route_expert[0].T, jnp.zeros((pad, D_MODEL), jnp.float32)], axis=0)
    w_route = w_r.astype(bf)
    b_r = jnp.concatenate([b_route_group[0], b_route_expert[0], jnp.zeros((pad,), jnp.float32)])
    b_r = jnp.broadcast_to(b_r[:, None], (ROUTE_ROWS, TM_MERGE))
    row = lambda v: v.reshape(1, D_MODEL)
    return dict(
        ln_in_g=row(ln_in_g), ln_in_b=row(ln_in_b), w_qkv=w_qkv, w_gates=w_gates,
        sink=attn_sink[0].astype(jnp.float32), nat_bias=_nat_bias_table(rel_pos_bias[0]),
        w_pa=w_pa, w_pb=w_pb, w_o=w_o, ln1_g=row(ln1_g[0]), ln1_b=row(ln1_b[0]),
        w_route=w_route, b_r=b_r, ln2_g=row(ln2_g[0]), ln2_b=row(ln2_b[0]))


def _after(value, other):
    if other is None:
        return value
    other = other.astype(jnp.float32)
    zero = jnp.where(jnp.isfinite(other), other, 0.0) * 0.0
    return value + zero.astype(value.dtype)


def _attend_and_route(x, p, after=None, cast=()):
    bsz, t, _ = x.shape
    n = bsz * t
    x2 = x.reshape(n, D_MODEL)
    qkv, cast_bf16 = _qkv(x2, p["ln_in_g"], p["ln_in_b"], p["w_qkv"], TM_QKV, cast)
    oa = _win_attention(qkv, p["sink"], bsz, t, TQ_WIN)
    ob = _nat_attention(qkv, p["nat_bias"], bsz, t)
    cnt0 = _after(jnp.zeros((ROUTE_ROWS, TM_MERGE), jnp.float32), after)
    h1, h1p, info, cnt = _merge(x2, oa, ob, p["ln_in_g"], p["ln_in_b"], p["w_gates"], p["w_pa"], p["w_pb"],
                                p["w_o"], p["ln1_g"], p["ln1_b"], p["w_route"], p["b_r"], cnt0, TM_MERGE)
    counts = cnt[EXPERT_ROW0:EXPERT_ROW0 + N_EXPERTS, 0].astype(jnp.int32)
    expert = jnp.arange(N_EXPERTS, dtype=jnp.int32)
    starts = jnp.sum(jnp.where(expert[None, :] < expert[:, None], counts[None, :], 0), axis=1)
    eid = info[INFO_E1:INFO_E2 + 1].astype(jnp.int32)
    rank = info[INFO_R1:INFO_R2 + 1].astype(jnp.int32)
    dest_t = rank + jnp.sum(jnp.where(eid[None] == expert[:, None, None], starts[:, None, None], 0), axis=0)
    return dict(shape=x.shape, h1=h1, h1p=h1p, info=info, counts=counts, dest_t=dest_t, cast=cast_bf16)


def _run_experts(r, expert_weights, after=None):
    n = r["h1"].shape[0]
    xs = _dispatch(r["h1p"], r["dest_t"])
    blk, e, lo, hi, flags = _moe_plan(r["counts"], TOP_K * n, BM_EXPERT)
    plan = (blk, e, lo, hi, _after(flags, after))
    return _experts(xs, plan, *expert_weights, BM_EXPERT)


def _finish(r, ys, p, after=None):
    out = _combine(r["h1"], r["info"], r["dest_t"], ys, _after(p["ln2_g"], after), p["ln2_b"], TT_ROWS)
    return out.reshape(r["shape"])


def kernel(x_prompt, x_sample, ln_in_g, ln_in_b, w_in, attn_sink, rel_pos_bias, w_proj_a, w_proj_b, w_out,
           ln1_g, ln1_b, w_route_group, b_route_group, w_route_expert, b_route_expert,
           w_gate, w_up, w_down, ln2_g, ln2_b):
    p = _prepare_weights(ln_in_g, ln_in_b, w_in, attn_sink, rel_pos_bias, w_proj_a, w_proj_b, w_out,
                         ln1_g, ln1_b, w_route_group, b_route_group, w_route_expert, b_route_expert,
                         ln2_g, ln2_b)
    rp = _attend_and_route(x_prompt, p, cast=(w_gate[0], w_up[0], w_down[0]))
    rs = _attend_and_route(x_sample, p, after=rp["counts"][0])
    ys_p = _run_experts(rp, rp["cast"])
    ys_s = _run_experts(rs, rp["cast"], after=ys_p[0, 0])
    y_prompt = _finish(rp, ys_p, p)
    y_sample = _finish(rs, ys_s, p, after=y_prompt[0, 0, 0])
    return (y_prompt, y_sample)
```

```python
import functools

import numpy as np
import jax
import jax.numpy as jnp
from jax import lax
from jax.experimental import pallas as pl
from jax.experimental.pallas import tpu as pltpu
from jax.experimental.pallas import tpu_sc as plsc

D_MODEL = 1024
HEAD_DIM = 64
N_HEADS_A = 8
N_KV_HEADS_A = 2
WINDOW = 128
N_HEADS_B = 8
GRID_W = 64
NA_ROWS = 8
NA_COLS = 16
N_GROUPS = 4
EXPERTS_PER_GROUP = 8
N_EXPERTS = N_GROUPS * EXPERTS_PER_GROUP
TOP_K = 2
D_EXPERT = D_MODEL // 2
LN_EPS = 1e-5
DEPTH = 1
DEEPNORM_ALPHA = (2.0 * DEPTH) ** 0.25
WIDTH_A = N_HEADS_A * HEAD_DIM
KV_WIDTH_A = N_KV_HEADS_A * HEAD_DIM
WIDTH_B = N_HEADS_B * HEAD_DIM
QKV_WIDTH = WIDTH_A + 2 * KV_WIDTH_A + 3 * WIDTH_B

LANES = 128
VMEM_LIMIT_BYTES = 56 * 1024 * 1024

NEG_BIG = -1e30
LOG2E = float(np.log2(np.e))

QA_COL, QB_COL, KB_COL, VB_COL = 0, WIDTH_A, WIDTH_A + WIDTH_B, WIDTH_A + 2 * WIDTH_B
KA_COL = WIDTH_A + 3 * WIDTH_B
VA_COL = KA_COL + KV_WIDTH_A

GQA_GROUP = N_HEADS_A // N_KV_HEADS_A


def _cparams(*sem):
    return pltpu.CompilerParams(dimension_semantics=sem, vmem_limit_bytes=VMEM_LIMIT_BYTES)


def _layer_norm(x, g, b):
    mu = jnp.mean(x, axis=-1, keepdims=True)
    xc = x - mu
    var = jnp.mean(xc * xc, axis=-1, keepdims=True)
    return xc * lax.rsqrt(var + LN_EPS) * g + b


PACKED_WIDTH = D_MODEL // 2


def _pack_rows(x):
    def rne(v):
        return v + jnp.uint32(0x7FFF) + ((v >> 16) & jnp.uint32(1))
    hi = lax.bitcast_convert_type(x[:, :PACKED_WIDTH], jnp.uint32)
    lo = lax.bitcast_convert_type(x[:, PACKED_WIDTH:], jnp.uint32)
    return (rne(hi) & jnp.uint32(0xFFFF0000)) | (rne(lo) >> 16)


def _unpack_rows(w):
    hi = lax.bitcast_convert_type(w & jnp.uint32(0xFFFF0000), jnp.float32)
    lo = lax.bitcast_convert_type(w << 16, jnp.float32)
    return jnp.concatenate([hi, lo], axis=1)


def _qkv_kernel(x_ref, g_ref, b_ref, w_ref, *rest):
    o_ref = rest[len(rest) // 2]
    h = _layer_norm(x_ref[...], g_ref[...], b_ref[...])
    y = jnp.dot(h.astype(jnp.bfloat16), w_ref[...], preferred_element_type=jnp.float32)
    col = lax.broadcasted_iota(jnp.int32, (1, QKV_WIDTH), 1)
    y = y * jnp.where(col < KB_COL, HEAD_DIM ** -0.5 * LOG2E, 1.0)
    o_ref[...] = y.astype(jnp.bfloat16)
    ncast = len(rest) // 2
    for src_ref, dst_ref in zip(rest[:ncast], rest[ncast + 1:]):
        dst_ref[...] = src_ref[...].astype(jnp.bfloat16)


def _qkv(x2, ln_g, ln_b, w_qkv, tm, cast=()):
    n = x2.shape[0]
    steps = n // tm
    per_step = -(-N_EXPERTS // steps)
    assert all(c.shape[0] == N_EXPERTS for c in cast) and (steps * per_step) % N_EXPERTS == 0
    revisit = steps * per_step // N_EXPERTS

    def expert_block(c):
        return pl.BlockSpec((per_step,) + c.shape[1:], lambda i: (i // revisit, 0, 0))

    outs = pl.pallas_call(
        _qkv_kernel,
        out_shape=[jax.ShapeDtypeStruct((n, QKV_WIDTH), jnp.bfloat16)]
        + [jax.ShapeDtypeStruct(c.shape, jnp.bfloat16) for c in cast],
        grid=(steps,),
        in_specs=[
            pl.BlockSpec((tm, D_MODEL), lambda i: (i, 0)),
            pl.BlockSpec((1, D_MODEL), lambda i: (0, 0)),
            pl.BlockSpec((1, D_MODEL), lambda i: (0, 0)),
            pl.BlockSpec((D_MODEL, QKV_WIDTH), lambda i: (0, 0)),
        ] + [expert_block(c) for c in cast],
        out_specs=[pl.BlockSpec((tm, QKV_WIDTH), lambda i: (i, 0))] + [expert_block(c) for c in cast],
        compiler_params=_cparams("arbitrary" if cast else "parallel"),
        name="qkv",
    )(x2, ln_g, ln_b, w_qkv, *cast)
    return outs[0], tuple(outs[1:])


WIN_BLK = 128
WIN_LOOKAHEAD = 2


def _win_bias_table():
    qi = np.arange(WIN_BLK)[:, None]
    kj = np.arange(3 * WIN_BLK)[None, :]
    dist = np.abs(kj - WIN_BLK - qi).astype(np.float64)
    slopes = 2.0 ** (-8.0 * np.arange(1, N_HEADS_A + 1) / N_HEADS_A)
    per_head = np.where(dist <= WINDOW, -slopes[:, None, None] * dist[None] * LOG2E, NEG_BIG)
    groups = [np.concatenate([per_head[j], per_head[j + 4]], axis=0) for j in range(4)]
    return np.stack(groups).astype(np.float32)


def _win_kernel(sink_ref, q_ref, kp_ref, km_ref, kn_ref, vp_ref, vm_ref, vn_ref, bias_ref, o_ref,
                *, nsub, nblk_seq):
    i = pl.program_id(1)
    kcat = jnp.concatenate([kp_ref[...], km_ref[...], kn_ref[...]], axis=0)
    vcat = jnp.concatenate([vp_ref[...], vm_ref[...], vn_ref[...]], axis=0)
    lo = lax.broadcasted_iota(jnp.int32, (1, LANES), 1) < HEAD_DIM
    col = lax.broadcasted_iota(jnp.int32, (1, 3 * WIN_BLK), 1)
    top = lax.broadcasted_iota(jnp.int32, (2 * WIN_BLK, 1), 0) < WIN_BLK
    zero = jnp.zeros((), jnp.bfloat16)

    def scores(j, g):
        n = i * nsub + j
        off_seq = ((col < WIN_BLK) & (n == 0)) | ((col >= 2 * WIN_BLK) & (n == nblk_seq - 1))
        edge = jnp.where(off_seq, NEG_BIG, 0.0)
        qg = q_ref[WIN_BLK * j:WIN_BLK * (j + 1), LANES * g:LANES * (g + 1)]
        qm = jnp.concatenate([jnp.where(lo, qg, zero), jnp.where(lo, zero, qg)], axis=0)
        kj = kcat[WIN_BLK * j:WIN_BLK * (j + 3)]
        s = lax.dot_general(qm, kj, (((1,), (1,)), ((), ())), preferred_element_type=jnp.float32)
        return s + bias_ref[g] + edge

    def attend(s, j, g):
        vj = vcat[WIN_BLK * j:WIN_BLK * (j + 3)]
        sink = jnp.where(top, sink_ref[g], sink_ref[g + 4]) * LOG2E
        m = jnp.maximum(jnp.max(s, axis=-1, keepdims=True), sink)
        p = jnp.exp2(s - m)
        l = jnp.sum(p, axis=-1, keepdims=True) + jnp.exp2(sink - m)
        o2 = jnp.dot(p.astype(jnp.bfloat16), vj, preferred_element_type=jnp.float32)
        o2 = o2 * (1.0 / l)
        o_ref[WIN_BLK * j:WIN_BLK * (j + 1), LANES * g:LANES * (g + 1)] = (
            jnp.where(lo, o2[:WIN_BLK], o2[WIN_BLK:]).astype(jnp.bfloat16))

    chains = [(j, g) for j in range(nsub) for g in range(4)]
    pending = [scores(*c) for c in chains[:WIN_LOOKAHEAD]]
    for idx, c in enumerate(chains):
        s = pending.pop(0)
        if idx + WIN_LOOKAHEAD < len(chains):
            pending.append(scores(*chains[idx + WIN_LOOKAHEAD]))
        attend(s, *c)


def _win_attention(qkv, sink, bsz, t, tq):
    n = bsz * t
    nsub = tq // WIN_BLK
    nblk_seq = t // WIN_BLK
    ntile = t // tq
    bias = jnp.asarray(_win_bias_table())

    def main_map(col):
        return lambda b, i, *_: (b * ntile + i, col)

    def prev_map(col):
        return lambda b, i, *_: (b * nblk_seq + jnp.maximum(i * nsub - 1, 0), col)

    def next_map(col):
        return lambda b, i, *_: (b * nblk_seq + jnp.minimum(i * nsub + nsub, nblk_seq - 1), col)

    halo = (WIN_BLK, LANES)
    ka, va = KA_COL // LANES, VA_COL // LANES
    grid_spec = pltpu.PrefetchScalarGridSpec(
        num_scalar_prefetch=1,
        grid=(bsz, ntile),
        in_specs=[
            pl.BlockSpec((tq, WIDTH_A), main_map(QA_COL // WIDTH_A)),
            pl.BlockSpec(halo, prev_map(ka)),
            pl.BlockSpec((tq, LANES), main_map(ka)),
            pl.BlockSpec(halo, next_map(ka)),
            pl.BlockSpec(halo, prev_map(va)),
            pl.BlockSpec((tq, LANES), main_map(va)),
            pl.BlockSpec(halo, next_map(va)),
            pl.BlockSpec((4, 2 * WIN_BLK, 3 * WIN_BLK), lambda b, i, *_: (0, 0, 0)),
        ],
        out_specs=pl.BlockSpec((tq, WIDTH_A), main_map(0)),
    )
    return pl.pallas_call(
        functools.partial(_win_kernel, nsub=nsub, nblk_seq=nblk_seq),
        out_shape=jax.ShapeDtypeStruct((n, WIDTH_A), jnp.bfloat16),
        grid_spec=grid_spec,
        compiler_params=_cparams("parallel", "parallel"),
        name="win",
    )(sink, qkv, qkv, qkv, qkv, qkv, qkv, qkv, bias)


NAT_ROWS_PER_STEP = 16
NAT_HALO_ROWS = NA_ROWS // 2
NAT_KEYS = NA_ROWS * GRID_W
NAT_ROWS_PER_TRIP = 8
NAT_HEADS_PER_CHAIN = 2
NAT_LOOKAHEAD = 4


def _nat_bias_table(rpb):
    c = np.arange(GRID_W)
    cs = np.clip(c - NA_COLS // 2, 0, GRID_W - NA_COLS)
    col_mask = (c[None, :] >= cs[:, None]) & (c[None, :] < cs[:, None] + NA_COLS)
    dc = np.clip(c[None, :] - c[:, None] + (NA_COLS - 1), 0, 2 * NA_COLS - 2)
    onehot = jnp.asarray(dc[None] == np.arange(2 * NA_COLS - 1)[:, None, None], jnp.float32)
    picked = jnp.einsum("hdj,jqc->hqdc", rpb, onehot, precision=lax.Precision.HIGHEST)
    t1 = jnp.where(col_mask[None, :, None, :], picked * LOG2E, NEG_BIG)
    flat = t1.reshape(N_HEADS_B, GRID_W, (2 * NA_ROWS - 1) * GRID_W)
    shifts = jnp.stack([flat[:, :, sh * GRID_W:sh * GRID_W + NAT_KEYS] for sh in range(NA_ROWS)], axis=1)
    tb = shifts.reshape(N_HEADS_B // 2, 2, NA_ROWS, GRID_W, NAT_KEYS).transpose(0, 2, 1, 3, 4)
    return tb.reshape(N_HEADS_B // 2, NA_ROWS, 2 * GRID_W, NAT_KEYS)


def _nat_kernel(q_ref, kp_ref, km_ref, kn_ref, vp_ref, vm_ref, vn_ref, tb_ref, o_ref, kcat, vcat,
                *, rows_seq):
    i = pl.program_id(1)
    halo = NAT_HALO_ROWS * GRID_W
    main = NAT_ROWS_PER_STEP * GRID_W
    kcat[0:halo] = kp_ref[...]
    kcat[halo:halo + main] = km_ref[...]
    kcat[halo + main:2 * halo + main] = kn_ref[...]
    vcat[0:halo] = vp_ref[...]
    vcat[halo:halo + main] = vm_ref[...]
    vcat[halo + main:2 * halo + main] = vn_ref[...]
    width = NAT_HEADS_PER_CHAIN * HEAD_DIM
    head_of_lane = lax.broadcasted_iota(jnp.int32, (1, width), 1) // HEAD_DIM
    zero = jnp.zeros((), jnp.bfloat16)
    r0 = i * NAT_ROWS_PER_STEP

    def scores(qr, c):
        r = r0 + qr
        rs = jnp.clip(r - NA_ROWS // 2, 0, rows_seq - NA_ROWS)
        koff = pl.multiple_of((rs - r0 + NAT_HALO_ROWS) * GRID_W, GRID_W)
        sh = rs - r + (NA_ROWS - 1)
        qoff = pl.multiple_of(qr * GRID_W, GRID_W)
        cols = slice(width * c, width * (c + 1))
        qc = q_ref[pl.ds(qoff, GRID_W), cols]
        qm = jnp.concatenate([jnp.where(head_of_lane == h, qc, zero) for h in range(NAT_HEADS_PER_CHAIN)], axis=0)
        kw = kcat[pl.ds(koff, NAT_KEYS), cols]
        s = lax.dot_general(qm, kw, (((1,), (1,)), ((), ())), preferred_element_type=jnp.float32)
        pairs = NAT_HEADS_PER_CHAIN // 2
        bias = jnp.concatenate([tb_ref[pairs * c + k, sh] for k in range(pairs)], axis=0)
        return s + bias, koff, qoff

    def attend(s, koff, qoff, c):
        cols = slice(width * c, width * (c + 1))
        vw = vcat[pl.ds(koff, NAT_KEYS), cols]
        m = jnp.max(s, axis=-1, keepdims=True)
        pe = jnp.exp2(s - m)
        l = jnp.sum(pe, axis=-1, keepdims=True)
        o2 = jnp.dot(pe.astype(jnp.bfloat16), vw, preferred_element_type=jnp.float32)
        o2 = o2 * (1.0 / l)
        out = o2[:GRID_W]
        for h in range(1, NAT_HEADS_PER_CHAIN):
            out = jnp.where(head_of_lane == h, o2[GRID_W * h:GRID_W * (h + 1)], out)
        o_ref[pl.ds(qoff, GRID_W), cols] = out.astype(jnp.bfloat16)

    def trip(j, carry):
        chains = [(j * NAT_ROWS_PER_TRIP + q, c) for q in range(NAT_ROWS_PER_TRIP)
                  for c in range(N_HEADS_B // NAT_HEADS_PER_CHAIN)]
        pending = [scores(*c) for c in chains[:NAT_LOOKAHEAD]]
        for idx, (_, p) in enumerate(chains):
            s, koff, qoff = pending.pop(0)
            if idx + NAT_LOOKAHEAD < len(chains):
                pending.append(scores(*chains[idx + NAT_LOOKAHEAD]))
            attend(s, koff, qoff, p)
        return carry

    lax.fori_loop(0, NAT_ROWS_PER_STEP // NAT_ROWS_PER_TRIP, trip, 0)


def _nat_attention(qkv, tb, bsz, t):
    n = bsz * t
    rows_seq = t // GRID_W
    main = NAT_ROWS_PER_STEP * GRID_W
    halo = NAT_HALO_ROWS * GRID_W
    ntile = t // main
    nhalo_seq = t // halo
    per = main // halo

    def main_map(col):
        return lambda b, i: (b * ntile + i, col)

    def prev_map(col):
        return lambda b, i: (b * nhalo_seq + jnp.maximum(i * per - 1, 0), col)

    def next_map(col):
        return lambda b, i: (b * nhalo_seq + jnp.minimum(i * per + per, nhalo_seq - 1), col)

    qb, kb, vb = QB_COL // WIDTH_B, KB_COL // WIDTH_B, VB_COL // WIDTH_B
    return pl.pallas_call(
        functools.partial(_nat_kernel, rows_seq=rows_seq),
        out_shape=jax.ShapeDtypeStruct((n, WIDTH_B), jnp.bfloat16),
        grid=(bsz, ntile),
        in_specs=[
            pl.BlockSpec((main, WIDTH_B), main_map(qb)),
            pl.BlockSpec((halo, WIDTH_B), prev_map(kb)),
            pl.BlockSpec((main, WIDTH_B), main_map(kb)),
            pl.BlockSpec((halo, WIDTH_B), next_map(kb)),
            pl.BlockSpec((halo, WIDTH_B), prev_map(vb)),
            pl.BlockSpec((main, WIDTH_B), main_map(vb)),
            pl.BlockSpec((halo, WIDTH_B), next_map(vb)),
            pl.BlockSpec((N_HEADS_B // 2, NA_ROWS, 2 * GRID_W, NAT_KEYS), lambda b, i: (0, 0, 0, 0)),
        ],
        out_specs=pl.BlockSpec((main, WIDTH_B), main_map(0)),
        scratch_shapes=[pltpu.VMEM((main + 2 * halo, WIDTH_B), jnp.bfloat16),
                        pltpu.VMEM((main + 2 * halo, WIDTH_B), jnp.bfloat16)],
        compiler_params=_cparams("parallel", "parallel"),
        name="nat",
    )(qkv, qkv, qkv, qkv, qkv, qkv, qkv, tb)


EXPERT_ROW0 = N_GROUPS
ROUTE_ROWS = 48
INFO_E1, INFO_E2, INFO_R1, INFO_R2, INFO_W1, INFO_W2 = range(6)
INFO_ROWS = 8
MERGE_SUBTILES = 4


def _route(lt, carry, tri):
    rr, tm = lt.shape
    row = lax.broadcasted_iota(jnp.int32, (rr, tm), 0).astype(jnp.float32)
    none = jnp.float32(rr)

    def first_max(sel):
        m = jnp.max(jnp.where(sel, lt, NEG_BIG), axis=0, keepdims=True)
        idx = jnp.min(jnp.where(sel & (lt == m), row, none), axis=0, keepdims=True)
        return m, idx

    is_group = row < N_GROUPS
    mg, g = first_max(is_group)
    pg_sel = 1.0 / jnp.sum(jnp.where(is_group, jnp.exp(jnp.where(is_group, lt, mg) - mg), 0.0),
                           axis=0, keepdims=True)
    row0 = EXPERT_ROW0 + EXPERTS_PER_GROUP * g
    in_group = (row >= row0) & (row < row0 + EXPERTS_PER_GROUP)
    m1, i1 = first_max(in_group)
    m2, i2 = first_max(in_group & (row != i1))
    e2 = jnp.exp(m2 - m1)
    w1 = pg_sel / (1.0 + e2)
    w2 = pg_sel * e2 / (1.0 + e2)

    oh1 = row == i1
    oh2 = row == i2
    both = (oh1 | oh2).astype(jnp.bfloat16)
    before = jnp.dot(both, tri, preferred_element_type=jnp.float32) + carry
    r1 = jnp.sum(jnp.where(oh1, before, 0.0), axis=0, keepdims=True)
    r2 = jnp.sum(jnp.where(oh2, before, 0.0), axis=0, keepdims=True)
    new_carry = carry + jnp.sum(both.astype(jnp.float32), axis=1, keepdims=True)

    field = lax.broadcasted_iota(jnp.int32, (INFO_ROWS, tm), 0)
    info = jnp.zeros((INFO_ROWS, tm), jnp.float32)
    for k, v in ((INFO_E1, i1 - EXPERT_ROW0), (INFO_E2, i2 - EXPERT_ROW0), (INFO_R1, r1), (INFO_R2, r2),
                 (INFO_W1, w1), (INFO_W2, w2)):
        info = jnp.where(field == k, v, info)
    return info, new_carry


def _merge_kernel(x_ref, oa_ref, ob_ref, lng_ref, lnb_ref, wg_ref, wpa_ref, wpb_ref, wo_ref,
                  l1g_ref, l1b_ref, wr_ref, br_ref, cnt0_ref,
                  h1_ref, h1p_ref, info_ref, cnt_ref, carry_ref, tri_ref):
    tm = x_ref.shape[0]

    @pl.when(pl.program_id(0) == 0)
    def _():
        carry_ref[...] = cnt0_ref[...]
        r = lax.broadcasted_iota(jnp.int32, (tm, tm), 0)
        c = lax.broadcasted_iota(jnp.int32, (tm, tm), 1)
        tri_ref[...] = (r < c).astype(jnp.bfloat16)

    def project(rows):
        h = _layer_norm(x_ref[rows], lng_ref[...], lnb_ref[...])
        gates = jnp.dot(h.astype(jnp.bfloat16), wg_ref[...], preferred_element_type=jnp.float32)
        pa = jnp.dot(oa_ref[rows], wpa_ref[...], preferred_element_type=jnp.float32)
        pb = jnp.dot(ob_ref[rows], wpb_ref[...], preferred_element_type=jnp.float32)
        return h, gates, pa, pb

    def mix(h, gates, pa, pb):
        mixin = jax.nn.sigmoid(gates[:, :D_MODEL]) * pa + jax.nn.sigmoid(gates[:, D_MODEL:]) * pb
        return DEEPNORM_ALPHA * h + jnp.dot(mixin.astype(jnp.bfloat16), wo_ref[...],
                                            preferred_element_type=jnp.float32)

    def norm_and_logits(pre, rows):
        h1 = _layer_norm(pre, l1g_ref[...], l1b_ref[...])
        h1_ref[rows] = h1
        h1p_ref[rows] = _pack_rows(h1)
        return lax.dot_general(wr_ref[...], h1.astype(jnp.bfloat16), (((1,), (1,)), ((), ())),
                               preferred_element_type=jnp.float32)

    sub = tm // MERGE_SUBTILES
    parts = [slice(k * sub, (k + 1) * sub) for k in range(MERGE_SUBTILES)]
    projected = [project(rows) for rows in parts]
    mixed = [mix(*pr) for pr in projected]
    logits_t = jnp.concatenate([norm_and_logits(pre, rows) for pre, rows in zip(mixed, parts)], axis=1)
    logits_t = logits_t + br_ref[...]
    info, carry = _route(logits_t, carry_ref[...], tri_ref[...])
    info_ref[...] = info
    carry_ref[...] = carry
    cnt_ref[...] = carry[:, :LANES]


def _merge(x2, oa, ob, ln_g, ln_b, w_gates, w_pa, w_pb, w_o, l1g, l1b, w_r, b_r, cnt0, tm):
    n = x2.shape[0]

    def const(shape):
        return pl.BlockSpec(shape, lambda i: (0,) * len(shape))

    def rows(width):
        return pl.BlockSpec((tm, width), lambda i: (i, 0))

    return pl.pallas_call(
        _merge_kernel,
        out_shape=(jax.ShapeDtypeStruct((n, D_MODEL), jnp.float32),
                   jax.ShapeDtypeStruct((n, PACKED_WIDTH), jnp.uint32),
                   jax.ShapeDtypeStruct((INFO_ROWS, n), jnp.float32),
                   jax.ShapeDtypeStruct((ROUTE_ROWS, LANES), jnp.float32)),
        grid=(n // tm,),
        in_specs=[
            rows(D_MODEL), rows(WIDTH_A), rows(WIDTH_B),
            const((1, D_MODEL)), const((1, D_MODEL)),
            const((D_MODEL, 2 * D_MODEL)),
            const((WIDTH_A, D_MODEL)), const((WIDTH_B, D_MODEL)),
            const((D_MODEL, D_MODEL)),
            const((1, D_MODEL)), const((1, D_MODEL)),
            const((ROUTE_ROWS, D_MODEL)), const((ROUTE_ROWS, tm)), const((ROUTE_ROWS, tm)),
        ],
        out_specs=(rows(D_MODEL), rows(PACKED_WIDTH), pl.BlockSpec((INFO_ROWS, tm), lambda i: (0, i)),
                   const((ROUTE_ROWS, LANES))),
        scratch_shapes=[pltpu.VMEM((ROUTE_ROWS, tm), jnp.float32), pltpu.VMEM((tm, tm), jnp.bfloat16)],
        compiler_params=_cparams("arbitrary"),
        name="merge",
    )(x2, oa, ob, ln_g, ln_b, w_gates, w_pa, w_pb, w_o, l1g, l1b, w_r, b_r, cnt0)


SC_CORES = 2
SC_SUBCORES = 16
SC_WORKERS = SC_CORES * SC_SUBCORES
SC_ROWS_PER_STREAM = 64


def _sc_worker():
    return lax.axis_index("s") * SC_CORES + lax.axis_index("c")


def _dispatch(h1p, dest_t):
    n, width = h1p.shape
    per_worker = n // SC_WORKERS
    nchunks = per_worker // SC_ROWS_PER_STREAM
    assert nchunks * SC_ROWS_PER_STREAM * SC_WORKERS == n and nchunks % 2 == 0
    idx = dest_t.reshape(TOP_K, SC_WORKERS, nchunks, SC_ROWS_PER_STREAM)
    mesh = plsc.VectorSubcoreMesh(core_axis_name="c", subcore_axis_name="s")

    @functools.partial(
        pl.kernel, out_type=jax.ShapeDtypeStruct((TOP_K * n, width), h1p.dtype), mesh=mesh,
        scratch_types=[pltpu.VMEM((TOP_K, nchunks, SC_ROWS_PER_STREAM), jnp.int32),
                       pltpu.VMEM((2, SC_ROWS_PER_STREAM, width), h1p.dtype),
                       pltpu.SemaphoreType.DMA((2,)), pltpu.SemaphoreType.DMA((2,))],
        name="sc_dispatch")
    def scatter_kernel(src_hbm, idx_hbm, out_hbm, idx_v, rows_v, rsem, ssem):
        wid = _sc_worker()
        base = wid * per_worker
        for k in range(TOP_K):
            pltpu.sync_copy(idx_hbm.at[k, wid], idx_v.at[k])

        def read(j, slot):
            src = src_hbm.at[pl.ds(base + j * SC_ROWS_PER_STREAM, SC_ROWS_PER_STREAM)]
            return pltpu.make_async_copy(src, rows_v.at[slot], rsem.at[slot])

        def scatter(j, slot, k):
            return pltpu.make_async_copy(rows_v.at[slot], out_hbm.at[idx_v.at[k, j]], ssem.at[slot])

        read(0, 0).start()

        @pl.loop(0, nchunks, step=2)
        def _(j0):
            for slot in range(2):
                j = j0 + slot
                read(j, slot).wait()

                @pl.when(j + 1 < nchunks)
                def _():
                    @pl.when(j >= 1)
                    def _():
                        for k in range(TOP_K):
                            scatter(j - 1, 1 - slot, k).wait()
                    read(j + 1, 1 - slot).start()

                for k in range(TOP_K):
                    scatter(j, slot, k).start()

        for k in range(TOP_K):
            scatter(nchunks - 2, 0, k).wait()
            scatter(nchunks - 1, 1, k).wait()

    return scatter_kernel(h1p, idx)


def _moe_plan(counts, na, bm):
    expert = jnp.arange(N_EXPERTS, dtype=jnp.int32)
    upto = expert[None, :] <= expert[:, None]

    def running_sum(v):
        return jnp.sum(jnp.where(upto, v[None, :], 0), axis=1)

    ends = running_sum(counts)
    starts = ends - counts
    first_blk = starts // bm
    tiles = jnp.where(counts > 0, (ends - 1) // bm - first_blk + 1, 0)
    item_end = running_sum(tiles)
    item_start = item_end - tiles
    total = jnp.sum(tiles)
    wmax = na // bm + N_EXPERTS - 1
    w = jnp.arange(wmax, dtype=jnp.int32)
    wc = jnp.minimum(w, total - 1)
    e = jnp.sum((item_end[None, :] <= wc[:, None]).astype(jnp.int32), axis=1)
    e = jnp.minimum(e, N_EXPERTS - 1)
    owner = e[:, None] == expert[None, :]

    def of_owner(table):
        return jnp.sum(jnp.where(owner, table[None, :], 0), axis=1)

    blk = (of_owner(first_blk) + (wc - of_owner(item_start))).astype(jnp.int32)
    valid = w < total
    lo = jnp.where(valid, jnp.maximum(of_owner(starts), blk * bm), 0).astype(jnp.int32)
    hi = jnp.where(valid, jnp.minimum(of_owner(ends), (blk + 1) * bm), 0).astype(jnp.int32)
    prev_blk = jnp.concatenate([jnp.full((1,), -1, jnp.int32), blk[:-1]])
    prev_e = jnp.concatenate([jnp.full((1,), -1, jnp.int32), e[:-1]])
    flags = (valid.astype(jnp.int32)
             + FLAG_NEW_BLOCK * (valid & (blk != prev_blk)).astype(jnp.int32))
    return blk, e, lo, hi, flags


FLAG_VALID, FLAG_NEW_BLOCK = 1, 2


def _expert_kernel(blk_ref, e_ref, lo_ref, hi_ref, flag_ref, x_ref, wg_ref, wu_ref, wd_ref, o_ref):
    w = pl.program_id(0)
    bm = x_ref.shape[0]
    flags = flag_ref[w]

    @pl.when((flags & FLAG_VALID) != 0)
    def _():
        x = _unpack_rows(x_ref[...]).astype(jnp.bfloat16)
        g = jnp.dot(x, wg_ref[0], preferred_element_type=jnp.float32)
        u = jnp.dot(x, wu_ref[0], preferred_element_type=jnp.float32)
        hmid = (jax.nn.silu(g) * u).astype(jnp.bfloat16)
        y = _pack_rows(jnp.dot(hmid, wd_ref[0], preferred_element_type=jnp.float32))
        row = blk_ref[w] * bm + lax.broadcasted_iota(jnp.int32, (bm, 1), 0)
        mine = (row >= lo_ref[w]) & (row < hi_ref[w])

        @pl.when((flags & FLAG_NEW_BLOCK) != 0)
        def _():
            o_ref[...] = jnp.where(mine, y, jnp.uint32(0))

        @pl.when((flags & FLAG_NEW_BLOCK) == 0)
        def _():
            o_ref[...] = jnp.where(mine, y, o_ref[...])


def _experts(xs, plan, w_gate, w_up, w_down, bm):
    na = xs.shape[0]
    nitems = plan[0].shape[0]

    def weights(shape):
        return pl.BlockSpec((1,) + shape, lambda w, blk, e, *_: (e[w], 0, 0))

    grid_spec = pltpu.PrefetchScalarGridSpec(
        num_scalar_prefetch=len(plan),
        grid=(nitems,),
        in_specs=[
            pl.BlockSpec((bm, PACKED_WIDTH), lambda w, blk, *_: (blk[w], 0)),
            weights((D_MODEL, D_EXPERT)), weights((D_MODEL, D_EXPERT)), weights((D_EXPERT, D_MODEL)),
        ],
        out_specs=pl.BlockSpec((bm, PACKED_WIDTH), lambda w, blk, *_: (blk[w], 0)),
    )
    return pl.pallas_call(
        _expert_kernel,
        out_shape=jax.ShapeDtypeStruct((na, PACKED_WIDTH), jnp.uint32),
        grid_spec=grid_spec,
        compiler_params=_cparams("arbitrary"),
        name="experts",
    )(*plan, xs, w_gate, w_up, w_down)


def _sc_gather_rows(table, idx):
    nrows = idx.shape[0]
    width = table.shape[1]
    per_worker = nrows // SC_WORKERS
    nchunks = per_worker // SC_ROWS_PER_STREAM
    assert nchunks * SC_ROWS_PER_STREAM * SC_WORKERS == nrows and nchunks % 2 == 0
    mesh = plsc.VectorSubcoreMesh(core_axis_name="c", subcore_axis_name="s")

    @functools.partial(
        pl.kernel, out_type=jax.ShapeDtypeStruct((nrows, width), table.dtype), mesh=mesh,
        scratch_types=[pltpu.VMEM((per_worker,), jnp.int32),
                       pltpu.VMEM((2, SC_ROWS_PER_STREAM, width), table.dtype),
                       pltpu.SemaphoreType.DMA((2,)), pltpu.SemaphoreType.DMA((2,))],
        name="sc_gather")
    def gather_kernel(table_hbm, idx_hbm, out_hbm, idx_v, rows_v, gsem, wsem):
        base = _sc_worker() * per_worker
        pltpu.sync_copy(idx_hbm.at[pl.ds(base, per_worker)], idx_v)

        def gather(j, slot):
            rows = idx_v.at[pl.ds(j * SC_ROWS_PER_STREAM, SC_ROWS_PER_STREAM)]
            return pltpu.make_async_copy(table_hbm.at[rows], rows_v.at[slot], gsem.at[slot])

        def write(j, slot):
            dst = out_hbm.at[pl.ds(base + j * SC_ROWS_PER_STREAM, SC_ROWS_PER_STREAM)]
            return pltpu.make_async_copy(rows_v.at[slot], dst, wsem.at[slot])

        gather(0, 0).start()

        @pl.loop(0, nchunks, step=2)
        def _(j0):
            for slot in range(2):
                j = j0 + slot
                gather(j, slot).wait()

                @pl.when(j + 1 < nchunks)
                def _():
                    @pl.when(j >= 1)
                    def _():
                        write(j - 1, 1 - slot).wait()
                    gather(j + 1, 1 - slot).start()

                write(j, slot).start()

        write(nchunks - 2, 0).wait()
        write(nchunks - 1, 1).wait()

    return gather_kernel(table, idx)


def _finalize_kernel(h1_ref, y1_ref, y2_ref, info_ref, g_ref, b_ref, o_ref):
    tt = h1_ref.shape[0]
    pad = jnp.zeros((LANES - INFO_ROWS, tt), jnp.float32)
    info = jnp.concatenate([info_ref[...], pad], axis=0).T
    moe = (_unpack_rows(y1_ref[...]) * info[:, INFO_W1:INFO_W1 + 1]
           + _unpack_rows(y2_ref[...]) * info[:, INFO_W2:INFO_W2 + 1])
    o_ref[...] = _layer_norm(DEEPNORM_ALPHA * h1_ref[...] + moe, g_ref[...], b_ref[...])


def _combine(h1, info, dest_t, ys, ln_g, ln_b, tt):
    n = h1.shape[0]
    nsteps = n // tt
    yg = _sc_gather_rows(ys, dest_t.reshape(TOP_K * n))
    return pl.pallas_call(
        _finalize_kernel,
        out_shape=jax.ShapeDtypeStruct((n, D_MODEL), jnp.float32),
        grid=(nsteps,),
        in_specs=[
            pl.BlockSpec((tt, D_MODEL), lambda i: (i, 0)),
            pl.BlockSpec((tt, PACKED_WIDTH), lambda i: (i, 0)),
            pl.BlockSpec((tt, PACKED_WIDTH), lambda i: (nsteps + i, 0)),
            pl.BlockSpec((INFO_ROWS, tt), lambda i: (0, i)),
            pl.BlockSpec((1, D_MODEL), lambda i: (0, 0)),
            pl.BlockSpec((1, D_MODEL), lambda i: (0, 0)),
        ],
        out_specs=pl.BlockSpec((tt, D_MODEL), lambda i: (i, 0)),
        compiler_params=_cparams("parallel"),
        name="finalize",
    )(h1, yg, yg, info, ln_g, ln_b)


TM_QKV = 1024
TQ_WIN = 1024
TM_MERGE = 1024
TT_ROWS = 1024
BM_EXPERT = 512


def _prepare_weights(ln_in_g, ln_in_b, w_in, attn_sink, rel_pos_bias, w_proj_a, w_proj_b, w_out,
                     ln1_g, ln1_b, w_route_group, b_route_group, w_route_expert, b_route_expert,
                     ln2_g, ln2_b):
    bf = jnp.bfloat16
    w = w_in[0]
    splits = np.cumsum([WIDTH_A, KV_WIDTH_A, KV_WIDTH_A, WIDTH_B, WIDTH_B, WIDTH_B, D_MODEL])
    wqa, wka, wva, wqb, wkb, wvb, wga, wgb = jnp.split(w, [int(s) for s in splits], axis=1)
    wqa = (wqa.reshape(D_MODEL, N_KV_HEADS_A, GQA_GROUP, HEAD_DIM).transpose(0, 2, 1, 3)
           .reshape(D_MODEL, WIDTH_A))
    w_qkv = jnp.concatenate([wqa, wqb, wkb, wvb, wka, wva], axis=1).astype(bf)
    w_gates = jnp.concatenate([wga, wgb], axis=1).astype(bf)
    w_pa = (w_proj_a[0].reshape(N_KV_HEADS_A, GQA_GROUP, HEAD_DIM, D_MODEL).transpose(1, 0, 2, 3)
            .reshape(WIDTH_A, D_MODEL).astype(bf))
    w_pb = w_proj_b[0].astype(bf)
    w_o = w_out[0].astype(bf)
    pad = ROUTE_ROWS - N_GROUPS - N_EXPERTS
    w_r = jnp.concatenate([w_route_group[0].T, w_route_expert[0].T, jnp.zeros((pad, D_MODEL), jnp.float32)], axis=0)
    w_route = w_r.astype(bf)
    b_r = jnp.concatenate([b_route_group[0], b_route_expert[0], jnp.zeros((pad,), jnp.float32)])
    b_r = jnp.broadcast_to(b_r[:, None], (ROUTE_ROWS, TM_MERGE))
    row = lambda v: v.reshape(1, D_MODEL)
    return dict(
        ln_in_g=row(ln_in_g), ln_in_b=row(ln_in_b), w_qkv=w_qkv, w_gates=w_gates,
        sink=attn_sink[0].astype(jnp.float32), nat_bias=_nat_bias_table(rel_pos_bias[0]),
        w_pa=w_pa, w_pb=w_pb, w_o=w_o, ln1_g=row(ln1_g[0]), ln1_b=row(ln1_b[0]),
        w_route=w_route, b_r=b_r, ln2_g=row(ln2_g[0]), ln2_b=row(ln2_b[0]))


def _after(value, other):
    if other is None:
        return value
    other = other.astype(jnp.float32)
    zero = jnp.where(jnp.isfinite(other), other, 0.0) * 0.0
    return value + zero.astype(value.dtype)


def _attend_and_route(x, p, after=None, cast=()):
    bsz, t, _ = x.shape
    n = bsz * t
    x2 = x.reshape(n, D_MODEL)
    qkv, cast_bf16 = _qkv(x2, p["ln_in_g"], p["ln_in_b"], p["w_qkv"], TM_QKV, cast)
    oa = _win_attention(qkv, p["sink"], bsz, t, TQ_WIN)
    ob = _nat_attention(qkv, p["nat_bias"], bsz, t)
    cnt0 = _after(jnp.zeros((ROUTE_ROWS, TM_MERGE), jnp.float32), after)
    h1, h1p, info, cnt = _merge(x2, oa, ob, p["ln_in_g"], p["ln_in_b"], p["w_gates"], p["w_pa"], p["w_pb"],
                                p["w_o"], p["ln1_g"], p["ln1_b"], p["w_route"], p["b_r"], cnt0, TM_MERGE)
    counts = cnt[EXPERT_ROW0:EXPERT_ROW0 + N_EXPERTS, 0].astype(jnp.int32)
    expert = jnp.arange(N_EXPERTS, dtype=jnp.int32)
    starts = jnp.sum(jnp.where(expert[None, :] < expert[:, None], counts[None, :], 0), axis=1)
    eid = info[INFO_E1:INFO_E2 + 1].astype(jnp.int32)
    rank = info[INFO_R1:INFO_R2 + 1].astype(jnp.int32)
    dest_t = rank + jnp.sum(jnp.where(eid[None] == expert[:, None, None], starts[:, None, None], 0), axis=0)
    return dict(shape=x.shape, h1=h1, h1p=h1p, info=info, counts=counts, dest_t=dest_t, cast=cast_bf16)


def _run_experts(r, expert_weights, after=None):
    n = r["h1"].shape[0]
    xs = _dispatch(r["h1p"], r["dest_t"])
    blk, e, lo, hi, flags = _moe_plan(r["counts"], TOP_K * n, BM_EXPERT)
    plan = (blk, e, lo, hi, _after(flags, after))
    return _experts(xs, plan, *expert_weights, BM_EXPERT)


def _finish(r, ys, p, after=None):
    out = _combine(r["h1"], r["info"], r["dest_t"], ys, _after(p["ln2_g"], after), p["ln2_b"], TT_ROWS)
    return out.reshape(r["shape"])


def kernel(x_prompt, x_sample, ln_in_g, ln_in_b, w_in, attn_sink, rel_pos_bias, w_proj_a, w_proj_b, w_out,
           ln1_g, ln1_b, w_route_group, b_route_group, w_route_expert, b_route_expert,
           w_gate, w_up, w_down, ln2_g, ln2_b):
    p = _prepare_weights(ln_in_g, ln_in_b, w_in, attn_sink, rel_pos_bias, w_proj_a, w_proj_b, w_out,
                         ln1_g, ln1_b, w_route_group, b_route_group, w_route_expert, b_route_expert,
                         ln2_g, ln2_b)
    rp = _attend_and_route(x_prompt, p, cast=(w_gate[0], w_up[0], w_down[0]))
    rs = _attend_and_route(x_sample, p, after=rp["counts"][0])
    ys_p = _run_experts(rp, rp["cast"])
    ys_s = _run_experts(rs, rp["cast"], after=ys_p[0, 0])
    y_prompt = _finish(rp, ys_p, p)
    y_sample = _finish(rs, ys_s, p, after=y_prompt[0, 0, 0])
    return (y_prompt, y_sample)
```

```python
import functools

import numpy as np
import jax
import jax.numpy as jnp
from jax import lax
from jax.experimental import pallas as pl
from jax.experimental.pallas import tpu as pltpu
from jax.experimental.pallas import tpu_sc as plsc

D_MODEL = 1024
HEAD_DIM = 64
N_HEADS_A = 8
N_KV_HEADS_A = 2
WINDOW = 128
N_HEADS_B = 8
GRID_W = 64
NA_ROWS = 8
NA_COLS = 16
N_GROUPS = 4
EXPERTS_PER_GROUP = 8
N_EXPERTS = N_GROUPS * EXPERTS_PER_GROUP
TOP_K = 2
D_EXPERT = D_MODEL // 2
LN_EPS = 1e-5
DEPTH = 1
DEEPNORM_ALPHA = (2.0 * DEPTH) ** 0.25
WIDTH_A = N_HEADS_A * HEAD_DIM
KV_WIDTH_A = N_KV_HEADS_A * HEAD_DIM
WIDTH_B = N_HEADS_B * HEAD_DIM
QKV_WIDTH = WIDTH_A + 2 * KV_WIDTH_A + 3 * WIDTH_B

LANES = 128
VMEM_LIMIT_BYTES = 56 * 1024 * 1024

NEG_BIG = -1e30
LOG2E = float(np.log2(np.e))

QA_COL, QB_COL, KB_COL, VB_COL = 0, WIDTH_A, WIDTH_A + WIDTH_B, WIDTH_A + 2 * WIDTH_B
KA_COL = WIDTH_A + 3 * WIDTH_B
VA_COL = KA_COL + KV_WIDTH_A

GQA_GROUP = N_HEADS_A // N_KV_HEADS_A


def _cparams(*sem):
    return pltpu.CompilerParams(dimension_semantics=sem, vmem_limit_bytes=VMEM_LIMIT_BYTES)


def _layer_norm(x, g, b):
    mu = jnp.mean(x, axis=-1, keepdims=True)
    xc = x - mu
    var = jnp.mean(xc * xc, axis=-1, keepdims=True)
    return xc * lax.rsqrt(var + LN_EPS) * g + b


PACKED_WIDTH = D_MODEL // 2


def _pack_rows(x):
    def rne(v):
        return v + jnp.uint32(0x7FFF) + ((v >> 16) & jnp.uint32(1))
    hi = lax.bitcast_convert_type(x[:, :PACKED_WIDTH], jnp.uint32)
    lo = lax.bitcast_convert_type(x[:, PACKED_WIDTH:], jnp.uint32)
    return (rne(hi) & jnp.uint32(0xFFFF0000)) | (rne(lo) >> 16)


def _unpack_rows(w):
    hi = lax.bitcast_convert_type(w & jnp.uint32(0xFFFF0000), jnp.float32)
    lo = lax.bitcast_convert_type(w << 16, jnp.float32)
    return jnp.concatenate([hi, lo], axis=1)


def _qkv_kernel(x_ref, g_ref, b_ref, w_ref, *rest):
    o_ref = rest[len(rest) // 2]
    h = _layer_norm(x_ref[...], g_ref[...], b_ref[...])
    y = jnp.dot(h.astype(jnp.bfloat16), w_ref[...], preferred_element_type=jnp.float32)
    col = lax.broadcasted_iota(jnp.int32, (1, QKV_WIDTH), 1)
    y = y * jnp.where(col < KB_COL, HEAD_DIM ** -0.5 * LOG2E, 1.0)
    o_ref[...] = y.astype(jnp.bfloat16)
    ncast = len(rest) // 2
    for src_ref, dst_ref in zip(rest[:ncast], rest[ncast + 1:]):
        dst_ref[...] = src_ref[...].astype(jnp.bfloat16)


def _qkv(x2, ln_g, ln_b, w_qkv, tm, cast=()):
    n = x2.shape[0]
    steps = n // tm
    per_step = -(-N_EXPERTS // steps)
    assert all(c.shape[0] == N_EXPERTS for c in cast) and (steps * per_step) % N_EXPERTS == 0
    revisit = steps * per_step // N_EXPERTS

    def expert_block(c):
        return pl.BlockSpec((per_step,) + c.shape[1:], lambda i: (i // revisit, 0, 0))

    outs = pl.pallas_call(
        _qkv_kernel,
        out_shape=[jax.ShapeDtypeStruct((n, QKV_WIDTH), jnp.bfloat16)]
        + [jax.ShapeDtypeStruct(c.shape, jnp.bfloat16) for c in cast],
        grid=(steps,),
        in_specs=[
            pl.BlockSpec((tm, D_MODEL), lambda i: (i, 0)),
            pl.BlockSpec((1, D_MODEL), lambda i: (0, 0)),
            pl.BlockSpec((1, D_MODEL), lambda i: (0, 0)),
            pl.BlockSpec((D_MODEL, QKV_WIDTH), lambda i: (0, 0)),
        ] + [expert_block(c) for c in cast],
        out_specs=[pl.BlockSpec((tm, QKV_WIDTH), lambda i: (i, 0))] + [expert_block(c) for c in cast],
        compiler_params=_cparams("arbitrary" if cast else "parallel"),
        name="qkv",
    )(x2, ln_g, ln_b, w_qkv, *cast)
    return outs[0], tuple(outs[1:])


WIN_BLK = 128
WIN_LOOKAHEAD = 2


def _win_bias_table():
    qi = np.arange(WIN_BLK)[:, None]
    kj = np.arange(3 * WIN_BLK)[None, :]
    dist = np.abs(kj - WIN_BLK - qi).astype(np.float64)
    slopes = 2.0 ** (-8.0 * np.arange(1, N_HEADS_A + 1) / N_HEADS_A)
    per_head = np.where(dist <= WINDOW, -slopes[:, None, None] * dist[None] * LOG2E, NEG_BIG)
    groups = [np.concatenate([per_head[j], per_head[j + 4]], axis=0) for j in range(4)]
    return np.stack(groups).astype(np.float32)


def _win_kernel(sink_ref, q_ref, kp_ref, km_ref, kn_ref, vp_ref, vm_ref, vn_ref, bias_ref, o_ref,
                *, nsub, nblk_seq):
    i = pl.program_id(1)
    kcat = jnp.concatenate([kp_ref[...], km_ref[...], kn_ref[...]], axis=0)
    vcat = jnp.concatenate([vp_ref[...], vm_ref[...], vn_ref[...]], axis=0)
    lo = lax.broadcasted_iota(jnp.int32, (1, LANES), 1) < HEAD_DIM
    col = lax.broadcasted_iota(jnp.int32, (1, 3 * WIN_BLK), 1)
    top = lax.broadcasted_iota(jnp.int32, (2 * WIN_BLK, 1), 0) < WIN_BLK
    zero = jnp.zeros((), jnp.bfloat16)

    def scores(j, g):
        n = i * nsub + j
        off_seq = ((col < WIN_BLK) & (n == 0)) | ((col >= 2 * WIN_BLK) & (n == nblk_seq - 1))
        edge = jnp.where(off_seq, NEG_BIG, 0.0)
        qg = q_ref[WIN_BLK * j:WIN_BLK * (j + 1), LANES * g:LANES * (g + 1)]
        qm = jnp.concatenate([jnp.where(lo, qg, zero), jnp.where(lo, zero, qg)], axis=0)
        kj = kcat[WIN_BLK * j:WIN_BLK * (j + 3)]
        s = lax.dot_general(qm, kj, (((1,), (1,)), ((), ())), preferred_element_type=jnp.float32)
        return s + bias_ref[g] + edge

    def attend(s, j, g):
        vj = vcat[WIN_BLK * j:WIN_BLK * (j + 3)]
        sink = jnp.where(top, sink_ref[g], sink_ref[g + 4]) * LOG2E
        m = jnp.maximum(jnp.max(s, axis=-1, keepdims=True), sink)
        p = jnp.exp2(s - m)
        l = jnp.sum(p, axis=-1, keepdims=True) + jnp.exp2(sink - m)
        o2 = jnp.dot(p.astype(jnp.bfloat16), vj, preferred_element_type=jnp.float32)
        o2 = o2 * (1.0 / l)
        o_ref[WIN_BLK * j:WIN_BLK * (j + 1), LANES * g:LANES * (g + 1)] = (
            jnp.where(lo, o2[:WIN_BLK], o2[WIN_BLK:]).astype(jnp.bfloat16))

    chains = [(j, g) for j in range(nsub) for g in range(4)]
    pending = [scores(*c) for c in chains[:WIN_LOOKAHEAD]]
    for idx, c in enumerate(chains):
        s = pending.pop(0)
        if idx + WIN_LOOKAHEAD < len(chains):
            pending.append(scores(*chains[idx + WIN_LOOKAHEAD]))
        attend(s, *c)


def _win_attention(qkv, sink, bsz, t, tq):
    n = bsz * t
    nsub = tq // WIN_BLK
    nblk_seq = t // WIN_BLK
    ntile = t // tq
    bias = jnp.asarray(_win_bias_table())

    def main_map(col):
        return lambda b, i, *_: (b * ntile + i, col)

    def prev_map(col):
        return lambda b, i, *_: (b * nblk_seq + jnp.maximum(i * nsub - 1, 0), col)

    def next_map(col):
        return lambda b, i, *_: (b * nblk_seq + jnp.minimum(i * nsub + nsub, nblk_seq - 1), col)

    halo = (WIN_BLK, LANES)
    ka, va = KA_COL // LANES, VA_COL // LANES
    grid_spec = pltpu.PrefetchScalarGridSpec(
        num_scalar_prefetch=1,
        grid=(bsz, ntile),
        in_specs=[
            pl.BlockSpec((tq, WIDTH_A), main_map(QA_COL // WIDTH_A)),
            pl.BlockSpec(halo, prev_map(ka)),
            pl.BlockSpec((tq, LANES), main_map(ka)),
            pl.BlockSpec(halo, next_map(ka)),
            pl.BlockSpec(halo, prev_map(va)),
            pl.BlockSpec((tq, LANES), main_map(va)),
            pl.BlockSpec(halo, next_map(va)),
            pl.BlockSpec((4, 2 * WIN_BLK, 3 * WIN_BLK), lambda b, i, *_: (0, 0, 0)),
        ],
        out_specs=pl.BlockSpec((tq, WIDTH_A), main_map(0)),
    )
    return pl.pallas_call(
        functools.partial(_win_kernel, nsub=nsub, nblk_seq=nblk_seq),
        out_shape=jax.ShapeDtypeStruct((n, WIDTH_A), jnp.bfloat16),
        grid_spec=grid_spec,
        compiler_params=_cparams("parallel", "parallel"),
        name="win",
    )(sink, qkv, qkv, qkv, qkv, qkv, qkv, qkv, bias)


NAT_ROWS_PER_STEP = 16
NAT_HALO_ROWS = NA_ROWS // 2
NAT_KEYS = NA_ROWS * GRID_W
NAT_ROWS_PER_TRIP = 8
NAT_HEADS_PER_CHAIN = 2
NAT_LOOKAHEAD = 4


def _nat_bias_table(rpb):
    c = np.arange(GRID_W)
    cs = np.clip(c - NA_COLS // 2, 0, GRID_W - NA_COLS)
    col_mask = (c[None, :] >= cs[:, None]) & (c[None, :] < cs[:, None] + NA_COLS)
    dc = np.clip(c[None, :] - c[:, None] + (NA_COLS - 1), 0, 2 * NA_COLS - 2)
    onehot = jnp.asarray(dc[None] == np.arange(2 * NA_COLS - 1)[:, None, None], jnp.float32)
    picked = jnp.einsum("hdj,jqc->hqdc", rpb, onehot, precision=lax.Precision.HIGHEST)
    t1 = jnp.where(col_mask[None, :, None, :], picked * LOG2E, NEG_BIG)
    flat = t1.reshape(N_HEADS_B, GRID_W, (2 * NA_ROWS - 1) * GRID_W)
    shifts = jnp.stack([flat[:, :, sh * GRID_W:sh * GRID_W + NAT_KEYS] for sh in range(NA_ROWS)], axis=1)
    tb = shifts.reshape(N_HEADS_B // 2, 2, NA_ROWS, GRID_W, NAT_KEYS).transpose(0, 2, 1, 3, 4)
    return tb.reshape(N_HEADS_B // 2, NA_ROWS, 2 * GRID_W, NAT_KEYS)


def _nat_kernel(q_ref, kp_ref, km_ref, kn_ref, vp_ref, vm_ref, vn_ref, tb_ref, o_ref, kcat, vcat,
                *, rows_seq):
    i = pl.program_id(1)
    halo = NAT_HALO_ROWS * GRID_W
    main = NAT_ROWS_PER_STEP * GRID_W
    kcat[0:halo] = kp_ref[...]
    kcat[halo:halo + main] = km_ref[...]
    kcat[halo + main:2 * halo + main] = kn_ref[...]
    vcat[0:halo] = vp_ref[...]
    vcat[halo:halo + main] = vm_ref[...]
    vcat[halo + main:2 * halo + main] = vn_ref[...]
    width = NAT_HEADS_PER_CHAIN * HEAD_DIM
    head_of_lane = lax.broadcasted_iota(jnp.int32, (1, width), 1) // HEAD_DIM
    zero = jnp.zeros((), jnp.bfloat16)
    r0 = i * NAT_ROWS_PER_STEP

    def scores(qr, c):
        r = r0 + qr
        rs = jnp.clip(r - NA_ROWS // 2, 0, rows_seq - NA_ROWS)
        koff = pl.multiple_of((rs - r0 + NAT_HALO_ROWS) * GRID_W, GRID_W)
        sh = rs - r + (NA_ROWS - 1)
        qoff = pl.multiple_of(qr * GRID_W, GRID_W)
        cols = slice(width * c, width * (c + 1))
        qc = q_ref[pl.ds(qoff, GRID_W), cols]
        qm = jnp.concatenate([jnp.where(head_of_lane == h, qc, zero) for h in range(NAT_HEADS_PER_CHAIN)], axis=0)
        kw = kcat[pl.ds(koff, NAT_KEYS), cols]
        s = lax.dot_general(qm, kw, (((1,), (1,)), ((), ())), preferred_element_type=jnp.float32)
        pairs = NAT_HEADS_PER_CHAIN // 2
        bias = jnp.concatenate([tb_ref[pairs * c + k, sh] for k in range(pairs)], axis=0)
        return s + bias, koff, qoff

    def attend(s, koff, qoff, c):
        cols = slice(width * c, width * (c + 1))
        vw = vcat[pl.ds(koff, NAT_KEYS), cols]
        m = jnp.max(s, axis=-1, keepdims=True)
        pe = jnp.exp2(s - m)
        l = jnp.sum(pe, axis=-1, keepdims=True)
        o2 = jnp.dot(pe.astype(jnp.bfloat16), vw, preferred_element_type=jnp.float32)
        o2 = o2 * (1.0 / l)
        out = o2[:GRID_W]
        for h in range(1, NAT_HEADS_PER_CHAIN):
            out = jnp.where(head_of_lane == h, o2[GRID_W * h:GRID_W * (h + 1)], out)
        o_ref[pl.ds(qoff, GRID_W), cols] = out.astype(jnp.bfloat16)

    def trip(j, carry):
        chains = [(j * NAT_ROWS_PER_TRIP + q, c) for q in range(NAT_ROWS_PER_TRIP)
                  for c in range(N_HEADS_B // NAT_HEADS_PER_CHAIN)]
        pending = [scores(*c) for c in chains[:NAT_LOOKAHEAD]]
        for idx, (_, p) in enumerate(chains):
            s, koff, qoff = pending.pop(0)
            if idx + NAT_LOOKAHEAD < len(chains):
                pending.append(scores(*chains[idx + NAT_LOOKAHEAD]))
            attend(s, koff, qoff, p)
        return carry

    lax.fori_loop(0, NAT_ROWS_PER_STEP // NAT_ROWS_PER_TRIP, trip, 0)


def _nat_attention(qkv, tb, bsz, t):
    n = bsz * t
    rows_seq = t // GRID_W
    main = NAT_ROWS_PER_STEP * GRID_W
    halo = NAT_HALO_ROWS * GRID_W
    ntile = t // main
    nhalo_seq = t // halo
    per = main // halo

    def main_map(col):
        return lambda b, i: (b * ntile + i, col)

    def prev_map(col):
        return lambda b, i: (b * nhalo_seq + jnp.maximum(i * per - 1, 0), col)

    def next_map(col):
        return lambda b, i: (b * nhalo_seq + jnp.minimum(i * per + per, nhalo_seq - 1), col)

    qb, kb, vb = QB_COL // WIDTH_B, KB_COL // WIDTH_B, VB_COL // WIDTH_B
    return pl.pallas_call(
        functools.partial(_nat_kernel, rows_seq=rows_seq),
        out_shape=jax.ShapeDtypeStruct((n, WIDTH_B), jnp.bfloat16),
        grid=(bsz, ntile),
        in_specs=[
            pl.BlockSpec((main, WIDTH_B), main_map(qb)),
            pl.BlockSpec((halo, WIDTH_B), prev_map(kb)),
            pl.BlockSpec((main, WIDTH_B), main_map(kb)),
            pl.BlockSpec((halo, WIDTH_B), next_map(kb)),
            pl.BlockSpec((halo, WIDTH_B), prev_map(vb)),
            pl.BlockSpec((main, WIDTH_B), main_map(vb)),
            pl.BlockSpec((halo, WIDTH_B), next_map(vb)),
            pl.BlockSpec((N_HEADS_B // 2, NA_ROWS, 2 * GRID_W, NAT_KEYS), lambda b, i: (0, 0, 0, 0)),
        ],
        out_specs=pl.BlockSpec((main, WIDTH_B), main_map(0)),
        scratch_shapes=[pltpu.VMEM((main + 2 * halo, WIDTH_B), jnp.bfloat16),
                        pltpu.VMEM((main + 2 * halo, WIDTH_B), jnp.bfloat16)],
        compiler_params=_cparams("parallel", "parallel"),
        name="nat",
    )(qkv, qkv, qkv, qkv, qkv, qkv, qkv, tb)


EXPERT_ROW0 = N_GROUPS
ROUTE_ROWS = 48
INFO_E1, INFO_E2, INFO_R1, INFO_R2, INFO_W1, INFO_W2 = range(6)
INFO_ROWS = 8
MERGE_SUBTILES = 4


def _route(lt, carry, tri):
    rr, tm = lt.shape
    row = lax.broadcasted_iota(jnp.int32, (rr, tm), 0).astype(jnp.float32)
    none = jnp.float32(rr)

    def first_max(sel):
        m = jnp.max(jnp.where(sel, lt, NEG_BIG), axis=0, keepdims=True)
        idx = jnp.min(jnp.where(sel & (lt == m), row, none), axis=0, keepdims=True)
        return m, idx

    is_group = row < N_GROUPS
    mg, g = first_max(is_group)
    pg_sel = 1.0 / jnp.sum(jnp.where(is_group, jnp.exp(jnp.where(is_group, lt, mg) - mg), 0.0),
                           axis=0, keepdims=True)
    row0 = EXPERT_ROW0 + EXPERTS_PER_GROUP * g
    in_group = (row >= row0) & (row < row0 + EXPERTS_PER_GROUP)
    m1, i1 = first_max(in_group)
    m2, i2 = first_max(in_group & (row != i1))
    e2 = jnp.exp(m2 - m1)
    w1 = pg_sel / (1.0 + e2)
    w2 = pg_sel * e2 / (1.0 + e2)

    oh1 = row == i1
    oh2 = row == i2
    both = (oh1 | oh2).astype(jnp.bfloat16)
    before = jnp.dot(both, tri, preferred_element_type=jnp.float32) + carry
    r1 = jnp.sum(jnp.where(oh1, before, 0.0), axis=0, keepdims=True)
    r2 = jnp.sum(jnp.where(oh2, before, 0.0), axis=0, keepdims=True)
    new_carry = carry + jnp.sum(both.astype(jnp.float32), axis=1, keepdims=True)

    field = lax.broadcasted_iota(jnp.int32, (INFO_ROWS, tm), 0)
    info = jnp.zeros((INFO_ROWS, tm), jnp.float32)
    for k, v in ((INFO_E1, i1 - EXPERT_ROW0), (INFO_E2, i2 - EXPERT_ROW0), (INFO_R1, r1), (INFO_R2, r2),
                 (INFO_W1, w1), (INFO_W2, w2)):
        info = jnp.where(field == k, v, info)
    return info, new_carry


def _merge_kernel(x_ref, oa_ref, ob_ref, lng_ref, lnb_ref, wg_ref, wpa_ref, wpb_ref, wo_ref,
                  l1g_ref, l1b_ref, wr_ref, br_ref, cnt0_ref,
                  h1_ref, h1p_ref, info_ref, cnt_ref, carry_ref, tri_ref):
    tm = x_ref.shape[0]

    @pl.when(pl.program_id(0) == 0)
    def _():
        carry_ref[...] = cnt0_ref[...]
        r = lax.broadcasted_iota(jnp.int32, (tm, tm), 0)
        c = lax.broadcasted_iota(jnp.int32, (tm, tm), 1)
        tri_ref[...] = (r < c).astype(jnp.bfloat16)

    def project(rows):
        h = _layer_norm(x_ref[rows], lng_ref[...], lnb_ref[...])
        gates = jnp.dot(h.astype(jnp.bfloat16), wg_ref[...], preferred_element_type=jnp.float32)
        pa = jnp.dot(oa_ref[rows], wpa_ref[...], preferred_element_type=jnp.float32)
        pb = jnp.dot(ob_ref[rows], wpb_ref[...], preferred_element_type=jnp.float32)
        return h, gates, pa, pb

    def mix(h, gates, pa, pb):
        mixin = jax.nn.sigmoid(gates[:, :D_MODEL]) * pa + jax.nn.sigmoid(gates[:, D_MODEL:]) * pb
        return DEEPNORM_ALPHA * h + jnp.dot(mixin.astype(jnp.bfloat16), wo_ref[...],
                                            preferred_element_type=jnp.float32)

    def norm_and_logits(pre, rows):
        h1 = _layer_norm(pre, l1g_ref[...], l1b_ref[...])
        h1_ref[rows] = h1
        h1p_ref[rows] = _pack_rows(h1)
        return lax.dot_general(wr_ref[...], h1.astype(jnp.bfloat16), (((1,), (1,)), ((), ())),
                               preferred_element_type=jnp.float32)

    sub = tm // MERGE_SUBTILES
    parts = [slice(k * sub, (k + 1) * sub) for k in range(MERGE_SUBTILES)]
    projected = [project(rows) for rows in parts]
    mixed = [mix(*pr) for pr in projected]
    logits_t = jnp.concatenate([norm_and_logits(pre, rows) for pre, rows in zip(mixed, parts)], axis=1)
    logits_t = logits_t + br_ref[...]
    info, carry = _route(logits_t, carry_ref[...], tri_ref[...])
    info_ref[...] = info
    carry_ref[...] = carry
    cnt_ref[...] = carry[:, :LANES]


def _merge(x2, oa, ob, ln_g, ln_b, w_gates, w_pa, w_pb, w_o, l1g, l1b, w_r, b_r, cnt0, tm):
    n = x2.shape[0]

    def const(shape):
        return pl.BlockSpec(shape, lambda i: (0,) * len(shape))

    def rows(width):
        return pl.BlockSpec((tm, width), lambda i: (i, 0))

    return pl.pallas_call(
        _merge_kernel,
        out_shape=(jax.ShapeDtypeStruct((n, D_MODEL), jnp.float32),
                   jax.ShapeDtypeStruct((n, PACKED_WIDTH), jnp.uint32),
                   jax.ShapeDtypeStruct((INFO_ROWS, n), jnp.float32),
                   jax.ShapeDtypeStruct((ROUTE_ROWS, LANES), jnp.float32)),
        grid=(n // tm,),
        in_specs=[
            rows(D_MODEL), rows(WIDTH_A), rows(WIDTH_B),
            const((1, D_MODEL)), const((1, D_MODEL)),
            const((D_MODEL, 2 * D_MODEL)),
            const((WIDTH_A, D_MODEL)), const((WIDTH_B, D_MODEL)),
            const((D_MODEL, D_MODEL)),
            const((1, D_MODEL)), const((1, D_MODEL)),
            const((ROUTE_ROWS, D_MODEL)), const((ROUTE_ROWS, tm)), const((ROUTE_ROWS, tm)),
        ],
        out_specs=(rows(D_MODEL), rows(PACKED_WIDTH), pl.BlockSpec((INFO_ROWS, tm), lambda i: (0, i)),
                   const((ROUTE_ROWS, LANES))),
        scratch_shapes=[pltpu.VMEM((ROUTE_ROWS, tm), jnp.float32), pltpu.VMEM((tm, tm), jnp.bfloat16)],
        compiler_params=_cparams("arbitrary"),
        name="merge",
    )(x2, oa, ob, ln_g, ln_b, w_gates, w_pa, w_pb, w_o, l1g, l1b, w_r, b_r, cnt0)


SC_CORES = 2
SC_SUBCORES = 16
SC_WORKERS = SC_CORES * SC_SUBCORES
SC_ROWS_PER_STREAM = 64


def _sc_worker():
    return lax.axis_index("s") * SC_CORES + lax.axis_index("c")


def _dispatch(h1p, dest_t, pad_rows, nrows_out):
    n, width = h1p.shape
    per_worker = n // SC_WORKERS
    nchunks = per_worker // SC_ROWS_PER_STREAM
    assert nchunks * SC_ROWS_PER_STREAM * SC_WORKERS == n and nchunks % 2 == 0
    idx = dest_t.reshape(TOP_K, SC_WORKERS, nchunks, SC_ROWS_PER_STREAM)
    npad = pad_rows.size // (SC_WORKERS * SC_ROWS_PER_STREAM)
    assert npad * SC_WORKERS * SC_ROWS_PER_STREAM == pad_rows.size
    pad_idx = pad_rows.reshape(SC_WORKERS, npad, SC_ROWS_PER_STREAM)
    zeros = jnp.zeros((SC_ROWS_PER_STREAM, width), h1p.dtype)
    mesh = plsc.VectorSubcoreMesh(core_axis_name="c", subcore_axis_name="s")

    @functools.partial(
        pl.kernel, out_type=jax.ShapeDtypeStruct((nrows_out, width), h1p.dtype), mesh=mesh,
        scratch_types=[pltpu.VMEM((TOP_K, nchunks, SC_ROWS_PER_STREAM), jnp.int32),
                       pltpu.VMEM((npad, SC_ROWS_PER_STREAM), jnp.int32),
                       pltpu.VMEM((2, SC_ROWS_PER_STREAM, width), h1p.dtype),
                       pltpu.SemaphoreType.DMA((2,)), pltpu.SemaphoreType.DMA((2,))],
        name="sc_dispatch")
    def scatter_kernel(src_hbm, idx_hbm, pad_hbm, zeros_hbm, out_hbm, idx_v, pad_v, rows_v, rsem, ssem):
        wid = _sc_worker()
        base = wid * per_worker
        for k in range(TOP_K):
            pltpu.sync_copy(idx_hbm.at[k, wid], idx_v.at[k])
        pltpu.sync_copy(pad_hbm.at[wid], pad_v)

        def read(j, slot):
            src = src_hbm.at[pl.ds(base + j * SC_ROWS_PER_STREAM, SC_ROWS_PER_STREAM)]
            return pltpu.make_async_copy(src, rows_v.at[slot], rsem.at[slot])

        def scatter(j, slot, k):
            return pltpu.make_async_copy(rows_v.at[slot], out_hbm.at[idx_v.at[k, j]], ssem.at[slot])

        read(0, 0).start()

        @pl.loop(0, nchunks, step=2)
        def _(j0):
            for slot in range(2):
                j = j0 + slot
                read(j, slot).wait()

                @pl.when(j + 1 < nchunks)
                def _():
                    @pl.when(j >= 1)
                    def _():
                        for k in range(TOP_K):
                            scatter(j - 1, 1 - slot, k).wait()
                    read(j + 1, 1 - slot).start()

                for k in range(TOP_K):
                    scatter(j, slot, k).start()

        for k in range(TOP_K):
            scatter(nchunks - 2, 0, k).wait()
            scatter(nchunks - 1, 1, k).wait()

        pltpu.sync_copy(zeros_hbm, rows_v.at[0])
        fills = [pltpu.make_async_copy(rows_v.at[0], out_hbm.at[pad_v.at[c]], ssem.at[0]) for c in range(npad)]
        for f in fills:
            f.start()
        for f in fills:
            f.wait()

    return scatter_kernel(h1p, idx, pad_idx, zeros)


def _sorted_layout(counts, eid, rank, bm):
    n = eid.shape[1]
    nblocks = TOP_K * n // bm + N_EXPERTS
    expert = jnp.arange(N_EXPERTS, dtype=jnp.int32)
    before = expert[None, :] < expert[:, None]
    blocks = (counts + bm - 1) // bm
    first_blk = jnp.sum(jnp.where(before, blocks[None, :], 0), axis=1)
    starts = first_blk * bm
    dest = rank + jnp.sum(jnp.where(eid[None] == expert[:, None, None], starts[:, None, None], 0), axis=0)
    j = jnp.arange(bm, dtype=jnp.int32)[None, :]
    npad = (blocks * bm - counts)[:, None]
    pad_rows = jnp.where(j < npad, (starts + counts)[:, None] + j, nblocks * bm - 1).astype(jnp.int32)
    total = jnp.sum(blocks)
    w = jnp.arange(nblocks, dtype=jnp.int32)
    src = jnp.minimum(w, total - 1)
    blk_end = first_blk + blocks
    owner = jnp.minimum(jnp.sum((blk_end[None, :] <= src[:, None]).astype(jnp.int32), axis=1), N_EXPERTS - 1)
    valid = (w < total).astype(jnp.int32)
    return dest.astype(jnp.int32), pad_rows, (src.astype(jnp.int32), owner.astype(jnp.int32), valid)


def _expert_kernel(src_ref, e_ref, valid_ref, x_ref, wg_ref, wu_ref, wd_ref, o_ref):
    w = pl.program_id(0)

    @pl.when(valid_ref[w] != 0)
    def _():
        x = _unpack_rows(x_ref[...]).astype(jnp.bfloat16)
        g = jnp.dot(x, wg_ref[0], preferred_element_type=jnp.float32)
        u = jnp.dot(x, wu_ref[0], preferred_element_type=jnp.float32)
        hmid = (jax.nn.silu(g) * u).astype(jnp.bfloat16)
        o_ref[...] = _pack_rows(jnp.dot(hmid, wd_ref[0], preferred_element_type=jnp.float32))

    @pl.when(valid_ref[w] == 0)
    def _():
        o_ref[...] = jnp.zeros(o_ref.shape, o_ref.dtype)


def _experts(xs, plan, w_gate, w_up, w_down, bm):
    nblocks = plan[0].shape[0]
    assert xs.shape[0] == nblocks * bm

    def weights(shape):
        return pl.BlockSpec((1,) + shape, lambda w, src, e, *_: (e[w], 0, 0))

    grid_spec = pltpu.PrefetchScalarGridSpec(
        num_scalar_prefetch=len(plan),
        grid=(nblocks,),
        in_specs=[
            pl.BlockSpec((bm, PACKED_WIDTH), lambda w, src, *_: (src[w], 0)),
            weights((D_MODEL, D_EXPERT)), weights((D_MODEL, D_EXPERT)), weights((D_EXPERT, D_MODEL)),
        ],
        out_specs=pl.BlockSpec((bm, PACKED_WIDTH), lambda w, *_: (w, 0)),
    )
    return pl.pallas_call(
        _expert_kernel,
        out_shape=jax.ShapeDtypeStruct(xs.shape, jnp.uint32),
        grid_spec=grid_spec,
        compiler_params=_cparams("parallel"),
        name="experts",
    )(*plan, xs, w_gate, w_up, w_down)


def _sc_gather_rows(table, idx):
    nrows = idx.shape[0]
    width = table.shape[1]
    per_worker = nrows // SC_WORKERS
    nchunks = per_worker // SC_ROWS_PER_STREAM
    assert nchunks * SC_ROWS_PER_STREAM * SC_WORKERS == nrows and nchunks % 2 == 0
    mesh = plsc.VectorSubcoreMesh(core_axis_name="c", subcore_axis_name="s")

    @functools.partial(
        pl.kernel, out_type=jax.ShapeDtypeStruct((nrows, width), table.dtype), mesh=mesh,
        scratch_types=[pltpu.VMEM((per_worker,), jnp.int32),
                       pltpu.VMEM((2, SC_ROWS_PER_STREAM, width), table.dtype),
                       pltpu.SemaphoreType.DMA((2,)), pltpu.SemaphoreType.DMA((2,))],
        name="sc_gather")
    def gather_kernel(table_hbm, idx_hbm, out_hbm, idx_v, rows_v, gsem, wsem):
        base = _sc_worker() * per_worker
        pltpu.sync_copy(idx_hbm.at[pl.ds(base, per_worker)], idx_v)

        def gather(j, slot):
            rows = idx_v.at[pl.ds(j * SC_ROWS_PER_STREAM, SC_ROWS_PER_STREAM)]
            return pltpu.make_async_copy(table_hbm.at[rows], rows_v.at[slot], gsem.at[slot])

        def write(j, slot):
            dst = out_hbm.at[pl.ds(base + j * SC_ROWS_PER_STREAM, SC_ROWS_PER_STREAM)]
            return pltpu.make_async_copy(rows_v.at[slot], dst, wsem.at[slot])

        gather(0, 0).start()

        @pl.loop(0, nchunks, step=2)
        def _(j0):
            for slot in range(2):
                j = j0 + slot
                gather(j, slot).wait()

                @pl.when(j + 1 < nchunks)
                def _():
                    @pl.when(j >= 1)
                    def _():
                        write(j - 1, 1 - slot).wait()
                    gather(j + 1, 1 - slot).start()

                write(j, slot).start()

        write(nchunks - 2, 0).wait()
        write(nchunks - 1, 1).wait()

    return gather_kernel(table, idx)


def _finalize_kernel(h1_ref, y1_ref, y2_ref, info_ref, g_ref, b_ref, o_ref):
    tt = h1_ref.shape[0]
    pad = jnp.zeros((LANES - INFO_ROWS, tt), jnp.float32)
    info = jnp.concatenate([info_ref[...], pad], axis=0).T
    moe = (_unpack_rows(y1_ref[...]) * info[:, INFO_W1:INFO_W1 + 1]
           + _unpack_rows(y2_ref[...]) * info[:, INFO_W2:INFO_W2 + 1])
    o_ref[...] = _layer_norm(DEEPNORM_ALPHA * h1_ref[...] + moe, g_ref[...], b_ref[...])


def _combine(h1, info, dest_t, ys, ln_g, ln_b, tt):
    n = h1.shape[0]
    nsteps = n // tt
    yg = _sc_gather_rows(ys, dest_t.reshape(TOP_K * n))
    return pl.pallas_call(
        _finalize_kernel,
        out_shape=jax.ShapeDtypeStruct((n, D_MODEL), jnp.float32),
        grid=(nsteps,),
        in_specs=[
            pl.BlockSpec((tt, D_MODEL), lambda i: (i, 0)),
            pl.BlockSpec((tt, PACKED_WIDTH), lambda i: (i, 0)),
            pl.BlockSpec((tt, PACKED_WIDTH), lambda i: (nsteps + i, 0)),
            pl.BlockSpec((INFO_ROWS, tt), lambda i: (0, i)),
            pl.BlockSpec((1, D_MODEL), lambda i: (0, 0)),
            pl.BlockSpec((1, D_MODEL), lambda i: (0, 0)),
        ],
        out_specs=pl.BlockSpec((tt, D_MODEL), lambda i: (i, 0)),
        compiler_params=_cparams("parallel"),
        name="finalize",
    )(h1, yg, yg, info, ln_g, ln_b)


TM_QKV = 1024
TQ_WIN = 1024
TM_MERGE = 1024
TT_ROWS = 1024
BM_EXPERT = 512


def _prepare_weights(ln_in_g, ln_in_b, w_in, attn_sink, rel_pos_bias, w_proj_a, w_proj_b, w_out,
                     ln1_g, ln1_b, w_route_group, b_route_group, w_route_expert, b_route_expert,
                     ln2_g, ln2_b):
    bf = jnp.bfloat16
    w = w_in[0]
    splits = np.cumsum([WIDTH_A, KV_WIDTH_A, KV_WIDTH_A, WIDTH_B, WIDTH_B, WIDTH_B, D_MODEL])
    wqa, wka, wva, wqb, wkb, wvb, wga, wgb = jnp.split(w, [int(s) for s in splits], axis=1)
    wqa = (wqa.reshape(D_MODEL, N_KV_HEADS_A, GQA_GROUP, HEAD_DIM).transpose(0, 2, 1, 3)
           .reshape(D_MODEL, WIDTH_A))
    w_qkv = jnp.concatenate([wqa, wqb, wkb, wvb, wka, wva], axis=1).astype(bf)
    w_gates = jnp.concatenate([wga, wgb], axis=1).astype(bf)
    w_pa = (w_proj_a[0].reshape(N_KV_HEADS_A, GQA_GROUP, HEAD_DIM, D_MODEL).transpose(1, 0, 2, 3)
            .reshape(WIDTH_A, D_MODEL).astype(bf))
    w_pb = w_proj_b[0].astype(bf)
    w_o = w_out[0].astype(bf)
    pad = ROUTE_ROWS - N_GROUPS - N_EXPERTS
    w_r = jnp.concatenate([w_route_group[0].T, w_route_expert[0].T, jnp.zeros((pad, D_MODEL), jnp.float32)], axis=0)
    w_route = w_r.astype(bf)
    b_r = jnp.concatenate([b_route_group[0], b_route_expert[0], jnp.zeros((pad,), jnp.float32)])
    b_r = jnp.broadcast_to(b_r[:, None], (ROUTE_ROWS, TM_MERGE))
    row = lambda v: v.reshape(1, D_MODEL)
    return dict(
        ln_in_g=row(ln_in_g), ln_in_b=row(ln_in_b), w_qkv=w_qkv, w_gates=w_gates,
        sink=attn_sink[0].astype(jnp.float32), nat_bias=_nat_bias_table(rel_pos_bias[0]),
        w_pa=w_pa, w_pb=w_pb, w_o=w_o, ln1_g=row(ln1_g[0]), ln1_b=row(ln1_b[0]),
        w_route=w_route, b_r=b_r, ln2_g=row(ln2_g[0]), ln2_b=row(ln2_b[0]))


def _after(value, other):
    if other is None:
        return value
    other = other.astype(jnp.float32)
    zero = jnp.where(jnp.isfinite(other), other, 0.0) * 0.0
    return value + zero.astype(value.dtype)


def _attend_and_route(x, p, after=None, cast=()):
    bsz, t, _ = x.shape
    n = bsz * t
    x2 = x.reshape(n, D_MODEL)
    qkv, cast_bf16 = _qkv(x2, p["ln_in_g"], p["ln_in_b"], p["w_qkv"], TM_QKV, cast)
    oa = _win_attention(qkv, p["sink"], bsz, t, TQ_WIN)
    ob = _nat_attention(qkv, p["nat_bias"], bsz, t)
    cnt0 = _after(jnp.zeros((ROUTE_ROWS, TM_MERGE), jnp.float32), after)
    h1, h1p, info, cnt = _merge(x2, oa, ob, p["ln_in_g"], p["ln_in_b"], p["w_gates"], p["w_pa"], p["w_pb"],
                                p["w_o"], p["ln1_g"], p["ln1_b"], p["w_route"], p["b_r"], cnt0, TM_MERGE)
    counts = cnt[EXPERT_ROW0:EXPERT_ROW0 + N_EXPERTS, 0].astype(jnp.int32)
    eid = info[INFO_E1:INFO_E2 + 1].astype(jnp.int32)
    rank = info[INFO_R1:INFO_R2 + 1].astype(jnp.int32)
    dest_t, pad_rows, plan = _sorted_layout(counts, eid, rank, BM_EXPERT)
    return dict(shape=x.shape, h1=h1, h1p=h1p, info=info, counts=counts, dest_t=dest_t, pad_rows=pad_rows,
                plan=plan, cast=cast_bf16)


def _run_experts(r, expert_weights, after=None):
    src, owner, valid = r["plan"]
    xs = _dispatch(r["h1p"], r["dest_t"], r["pad_rows"], src.shape[0] * BM_EXPERT)
    return _experts(xs, (src, owner, _after(valid, after)), *expert_weights, BM_EXPERT)


def _finish(r, ys, p, after=None):
    out = _combine(r["h1"], r["info"], r["dest_t"], ys, _after(p["ln2_g"], after), p["ln2_b"], TT_ROWS)
    return out.reshape(r["shape"])


def kernel(x_prompt, x_sample, ln_in_g, ln_in_b, w_in, attn_sink, rel_pos_bias, w_proj_a, w_proj_b, w_out,
           ln1_g, ln1_b, w_route_group, b_route_group, w_route_expert, b_route_expert,
           w_gate, w_up, w_down, ln2_g, ln2_b):
    p = _prepare_weights(ln_in_g, ln_in_b, w_in, attn_sink, rel_pos_bias, w_proj_a, w_proj_b, w_out,
                         ln1_g, ln1_b, w_route_group, b_route_group, w_route_expert, b_route_expert,
                         ln2_g, ln2_b)
    rp = _attend_and_route(x_prompt, p, cast=(w_gate[0], w_up[0], w_down[0]))
    rs = _attend_and_route(x_sample, p, after=rp["counts"][0])
    ys_p = _run_experts(rp, rp["cast"])
    ys_s = _run_experts(rs, rp["cast"], after=ys_p[0, 0])
    y_prompt = _finish(rp, ys_p, p)
    y_sample = _finish(rs, ys_s, p, after=y_prompt[0, 0, 0])
    return (y_prompt, y_sample)
```

```python
import functools

import numpy as np
import jax
import jax.numpy as jnp
from jax import lax
from jax.experimental import pallas as pl
from jax.experimental.pallas import tpu as pltpu
from jax.experimental.pallas import tpu_sc as plsc

D_MODEL = 1024
HEAD_DIM = 64
N_HEADS_A = 8
N_KV_HEADS_A = 2
WINDOW = 128
N_HEADS_B = 8
GRID_W = 64
NA_ROWS = 8
NA_COLS = 16
N_GROUPS = 4
EXPERTS_PER_GROUP = 8
N_EXPERTS = N_GROUPS * EXPERTS_PER_GROUP
TOP_K = 2
D_EXPERT = D_MODEL // 2
LN_EPS = 1e-5
DEPTH = 1
DEEPNORM_ALPHA = (2.0 * DEPTH) ** 0.25
WIDTH_A = N_HEADS_A * HEAD_DIM
KV_WIDTH_A = N_KV_HEADS_A * HEAD_DIM
WIDTH_B = N_HEADS_B * HEAD_DIM
QKV_WIDTH = WIDTH_A + 2 * KV_WIDTH_A + 3 * WIDTH_B

LANES = 128
VMEM_LIMIT_BYTES = 56 * 1024 * 1024

NEG_BIG = -1e30
LOG2E = float(np.log2(np.e))

QA_COL, QB_COL, KB_COL, VB_COL = 0, WIDTH_A, WIDTH_A + WIDTH_B, WIDTH_A + 2 * WIDTH_B
KA_COL = WIDTH_A + 3 * WIDTH_B
VA_COL = KA_COL + KV_WIDTH_A

GQA_GROUP = N_HEADS_A // N_KV_HEADS_A


def _cparams(*sem):
    return pltpu.CompilerParams(dimension_semantics=sem, vmem_limit_bytes=VMEM_LIMIT_BYTES)


def _layer_norm(x, g, b):
    mu = jnp.mean(x, axis=-1, keepdims=True)
    xc = x - mu
    var = jnp.mean(xc * xc, axis=-1, keepdims=True)
    return xc * lax.rsqrt(var + LN_EPS) * g + b


PACKED_WIDTH = D_MODEL // 2


def _pack_rows(x):
    def rne(v):
        return v + jnp.uint32(0x7FFF) + ((v >> 16) & jnp.uint32(1))
    hi = lax.bitcast_convert_type(x[:, :PACKED_WIDTH], jnp.uint32)
    lo = lax.bitcast_convert_type(x[:, PACKED_WIDTH:], jnp.uint32)
    return (rne(hi) & jnp.uint32(0xFFFF0000)) | (rne(lo) >> 16)


def _unpack_rows(w):
    hi = lax.bitcast_convert_type(w & jnp.uint32(0xFFFF0000), jnp.float32)
    lo = lax.bitcast_convert_type(w << 16, jnp.float32)
    return jnp.concatenate([hi, lo], axis=1)


def _qkv_kernel(x_ref, g_ref, b_ref, w_ref, *rest):
    o_ref = rest[len(rest) // 2]
    h = _layer_norm(x_ref[...], g_ref[...], b_ref[...])
    y = jnp.dot(h.astype(jnp.bfloat16), w_ref[...], preferred_element_type=jnp.float32)
    col = lax.broadcasted_iota(jnp.int32, (1, QKV_WIDTH), 1)
    y = y * jnp.where(col < KB_COL, HEAD_DIM ** -0.5 * LOG2E, 1.0)
    o_ref[...] = y.astype(jnp.bfloat16)
    ncast = len(rest) // 2
    for src_ref, dst_ref in zip(rest[:ncast], rest[ncast + 1:]):
        dst_ref[...] = src_ref[...].astype(jnp.bfloat16)


def _qkv(x2, ln_g, ln_b, w_qkv, tm, cast=()):
    n = x2.shape[0]
    steps = n // tm
    per_step = -(-N_EXPERTS // steps)
    assert all(c.shape[0] == N_EXPERTS for c in cast) and (steps * per_step) % N_EXPERTS == 0
    revisit = steps * per_step // N_EXPERTS

    def expert_block(c):
        return pl.BlockSpec((per_step,) + c.shape[1:], lambda i: (i // revisit, 0, 0))

    outs = pl.pallas_call(
        _qkv_kernel,
        out_shape=[jax.ShapeDtypeStruct((n, QKV_WIDTH), jnp.bfloat16)]
        + [jax.ShapeDtypeStruct(c.shape, jnp.bfloat16) for c in cast],
        grid=(steps,),
        in_specs=[
            pl.BlockSpec((tm, D_MODEL), lambda i: (i, 0)),
            pl.BlockSpec((1, D_MODEL), lambda i: (0, 0)),
            pl.BlockSpec((1, D_MODEL), lambda i: (0, 0)),
            pl.BlockSpec((D_MODEL, QKV_WIDTH), lambda i: (0, 0)),
        ] + [expert_block(c) for c in cast],
        out_specs=[pl.BlockSpec((tm, QKV_WIDTH), lambda i: (i, 0))] + [expert_block(c) for c in cast],
        compiler_params=_cparams("arbitrary" if cast else "parallel"),
        name="qkv",
    )(x2, ln_g, ln_b, w_qkv, *cast)
    return outs[0], tuple(outs[1:])


WIN_BLK = 128
WIN_LOOKAHEAD = 2


def _win_bias_table():
    qi = np.arange(WIN_BLK)[:, None]
    kj = np.arange(3 * WIN_BLK)[None, :]
    dist = np.abs(kj - WIN_BLK - qi).astype(np.float64)
    slopes = 2.0 ** (-8.0 * np.arange(1, N_HEADS_A + 1) / N_HEADS_A)
    per_head = np.where(dist <= WINDOW, -slopes[:, None, None] * dist[None] * LOG2E, NEG_BIG)
    groups = [np.concatenate([per_head[j], per_head[j + 4]], axis=0) for j in range(4)]
    return np.stack(groups).astype(np.float32)


def _win_kernel(sink_ref, q_ref, kp_ref, km_ref, kn_ref, vp_ref, vm_ref, vn_ref, bias_ref, o_ref,
                *, nsub, nblk_seq):
    i = pl.program_id(1)
    kcat = jnp.concatenate([kp_ref[...], km_ref[...], kn_ref[...]], axis=0)
    vcat = jnp.concatenate([vp_ref[...], vm_ref[...], vn_ref[...]], axis=0)
    lo = lax.broadcasted_iota(jnp.int32, (1, LANES), 1) < HEAD_DIM
    col = lax.broadcasted_iota(jnp.int32, (1, 3 * WIN_BLK), 1)
    top = lax.broadcasted_iota(jnp.int32, (2 * WIN_BLK, 1), 0) < WIN_BLK
    zero = jnp.zeros((), jnp.bfloat16)

    def scores(j, g):
        n = i * nsub + j
        off_seq = ((col < WIN_BLK) & (n == 0)) | ((col >= 2 * WIN_BLK) & (n == nblk_seq - 1))
        edge = jnp.where(off_seq, NEG_BIG, 0.0)
        qg = q_ref[WIN_BLK * j:WIN_BLK * (j + 1), LANES * g:LANES * (g + 1)]
        qm = jnp.concatenate([jnp.where(lo, qg, zero), jnp.where(lo, zero, qg)], axis=0)
        kj = kcat[WIN_BLK * j:WIN_BLK * (j + 3)]
        s = lax.dot_general(qm, kj, (((1,), (1,)), ((), ())), preferred_element_type=jnp.float32)
        return s + bias_ref[g] + edge

    def attend(s, j, g):
        vj = vcat[WIN_BLK * j:WIN_BLK * (j + 3)]
        sink = jnp.where(top, sink_ref[g], sink_ref[g + 4]) * LOG2E
        m = jnp.maximum(jnp.max(s, axis=-1, keepdims=True), sink)
        p = jnp.exp2(s - m)
        l = jnp.sum(p, axis=-1, keepdims=True) + jnp.exp2(sink - m)
        o2 = jnp.dot(p.astype(jnp.bfloat16), vj, preferred_element_type=jnp.float32)
        o2 = o2 * (1.0 / l)
        o_ref[WIN_BLK * j:WIN_BLK * (j + 1), LANES * g:LANES * (g + 1)] = (
            jnp.where(lo, o2[:WIN_BLK], o2[WIN_BLK:]).astype(jnp.bfloat16))

    chains = [(j, g) for j in range(nsub) for g in range(4)]
    pending = [scores(*c) for c in chains[:WIN_LOOKAHEAD]]
    for idx, c in enumerate(chains):
        s = pending.pop(0)
        if idx + WIN_LOOKAHEAD < len(chains):
            pending.append(scores(*chains[idx + WIN_LOOKAHEAD]))
        attend(s, *c)


def _win_attention(qkv, sink, bsz, t, tq):
    n = bsz * t
    nsub = tq // WIN_BLK
    nblk_seq = t // WIN_BLK
    ntile = t // tq
    bias = jnp.asarray(_win_bias_table())

    def main_map(col):
        return lambda b, i, *_: (b * ntile + i, col)

    def prev_map(col):
        return lambda b, i, *_: (b * nblk_seq + jnp.maximum(i * nsub - 1, 0), col)

    def next_map(col):
        return lambda b, i, *_: (b * nblk_seq + jnp.minimum(i * nsub + nsub, nblk_seq - 1), col)

    halo = (WIN_BLK, LANES)
    ka, va = KA_COL // LANES, VA_COL // LANES
    grid_spec = pltpu.PrefetchScalarGridSpec(
        num_scalar_prefetch=1,
        grid=(bsz, ntile),
        in_specs=[
            pl.BlockSpec((tq, WIDTH_A), main_map(QA_COL // WIDTH_A)),
            pl.BlockSpec(halo, prev_map(ka)),
            pl.BlockSpec((tq, LANES), main_map(ka)),
            pl.BlockSpec(halo, next_map(ka)),
            pl.BlockSpec(halo, prev_map(va)),
            pl.BlockSpec((tq, LANES), main_map(va)),
            pl.BlockSpec(halo, next_map(va)),
            pl.BlockSpec((4, 2 * WIN_BLK, 3 * WIN_BLK), lambda b, i, *_: (0, 0, 0)),
        ],
        out_specs=pl.BlockSpec((tq, WIDTH_A), main_map(0)),
    )
    return pl.pallas_call(
        functools.partial(_win_kernel, nsub=nsub, nblk_seq=nblk_seq),
        out_shape=jax.ShapeDtypeStruct((n, WIDTH_A), jnp.bfloat16),
        grid_spec=grid_spec,
        compiler_params=_cparams("parallel", "parallel"),
        name="win",
    )(sink, qkv, qkv, qkv, qkv, qkv, qkv, qkv, bias)


NAT_ROWS_PER_STEP = 16
NAT_HALO_ROWS = NA_ROWS // 2
NAT_KEYS = NA_ROWS * GRID_W
NAT_ROWS_PER_TRIP = 8
NAT_HEADS_PER_CHAIN = 2
NAT_LOOKAHEAD = 4


def _nat_bias_table(rpb):
    c = np.arange(GRID_W)
    cs = np.clip(c - NA_COLS // 2, 0, GRID_W - NA_COLS)
    col_mask = (c[None, :] >= cs[:, None]) & (c[None, :] < cs[:, None] + NA_COLS)
    dc = np.clip(c[None, :] - c[:, None] + (NA_COLS - 1), 0, 2 * NA_COLS - 2)
    onehot = jnp.asarray(dc[None] == np.arange(2 * NA_COLS - 1)[:, None, None], jnp.float32)
    picked = jnp.einsum("hdj,jqc->hqdc", rpb, onehot, precision=lax.Precision.HIGHEST)
    t1 = jnp.where(col_mask[None, :, None, :], picked * LOG2E, NEG_BIG)
    flat = t1.reshape(N_HEADS_B, GRID_W, (2 * NA_ROWS - 1) * GRID_W)
    shifts = jnp.stack([flat[:, :, sh * GRID_W:sh * GRID_W + NAT_KEYS] for sh in range(NA_ROWS)], axis=1)
    tb = shifts.reshape(N_HEADS_B // 2, 2, NA_ROWS, GRID_W, NAT_KEYS).transpose(0, 2, 1, 3, 4)
    return tb.reshape(N_HEADS_B // 2, NA_ROWS, 2 * GRID_W, NAT_KEYS)


def _nat_kernel(q_ref, kp_ref, km_ref, kn_ref, vp_ref, vm_ref, vn_ref, tb_ref, o_ref, kcat, vcat,
                *, rows_seq):
    i = pl.program_id(1)
    halo = NAT_HALO_ROWS * GRID_W
    main = NAT_ROWS_PER_STEP * GRID_W
    kcat[0:halo] = kp_ref[...]
    kcat[halo:halo + main] = km_ref[...]
    kcat[halo + main:2 * halo + main] = kn_ref[...]
    vcat[0:halo] = vp_ref[...]
    vcat[halo:halo + main] = vm_ref[...]
    vcat[halo + main:2 * halo + main] = vn_ref[...]
    width = NAT_HEADS_PER_CHAIN * HEAD_DIM
    head_of_lane = lax.broadcasted_iota(jnp.int32, (1, width), 1) // HEAD_DIM
    zero = jnp.zeros((), jnp.bfloat16)
    r0 = i * NAT_ROWS_PER_STEP

    def scores(qr, c):
        r = r0 + qr
        rs = jnp.clip(r - NA_ROWS // 2, 0, rows_seq - NA_ROWS)
        koff = pl.multiple_of((rs - r0 + NAT_HALO_ROWS) * GRID_W, GRID_W)
        sh = rs - r + (NA_ROWS - 1)
        qoff = pl.multiple_of(qr * GRID_W, GRID_W)
        cols = slice(width * c, width * (c + 1))
        qc = q_ref[pl.ds(qoff, GRID_W), cols]
        qm = jnp.concatenate([jnp.where(head_of_lane == h, qc, zero) for h in range(NAT_HEADS_PER_CHAIN)], axis=0)
        kw = kcat[pl.ds(koff, NAT_KEYS), cols]
        s = lax.dot_general(qm, kw, (((1,), (1,)), ((), ())), preferred_element_type=jnp.float32)
        pairs = NAT_HEADS_PER_CHAIN // 2
        bias = jnp.concatenate([tb_ref[pairs * c + k, sh] for k in range(pairs)], axis=0)
        return s + bias, koff, qoff

    def attend(s, koff, qoff, c):
        cols = slice(width * c, width * (c + 1))
        vw = vcat[pl.ds(koff, NAT_KEYS), cols]
        m = jnp.max(s, axis=-1, keepdims=True)
        pe = jnp.exp2(s - m)
        l = jnp.sum(pe, axis=-1, keepdims=True)
        o2 = jnp.dot(pe.astype(jnp.bfloat16), vw, preferred_element_type=jnp.float32)
        o2 = o2 * (1.0 / l)
        out = o2[:GRID_W]
        for h in range(1, NAT_HEADS_PER_CHAIN):
            out = jnp.where(head_of_lane == h, o2[GRID_W * h:GRID_W * (h + 1)], out)
        o_ref[pl.ds(qoff, GRID_W), cols] = out.astype(jnp.bfloat16)

    def trip(j, carry):
        chains = [(j * NAT_ROWS_PER_TRIP + q, c) for q in range(NAT_ROWS_PER_TRIP)
                  for c in range(N_HEADS_B // NAT_HEADS_PER_CHAIN)]
        pending = [scores(*c) for c in chains[:NAT_LOOKAHEAD]]
        for idx, (_, p) in enumerate(chains):
            s, koff, qoff = pending.pop(0)
            if idx + NAT_LOOKAHEAD < len(chains):
                pending.append(scores(*chains[idx + NAT_LOOKAHEAD]))
            attend(s, koff, qoff, p)
        return carry

    lax.fori_loop(0, NAT_ROWS_PER_STEP // NAT_ROWS_PER_TRIP, trip, 0)


def _nat_attention(qkv, tb, bsz, t):
    n = bsz * t
    rows_seq = t // GRID_W
    main = NAT_ROWS_PER_STEP * GRID_W
    halo = NAT_HALO_ROWS * GRID_W
    ntile = t // main
    nhalo_seq = t // halo
    per = main // halo

    def main_map(col):
        return lambda b, i: (b * ntile + i, col)

    def prev_map(col):
        return lambda b, i: (b * nhalo_seq + jnp.maximum(i * per - 1, 0), col)

    def next_map(col):
        return lambda b, i: (b * nhalo_seq + jnp.minimum(i * per + per, nhalo_seq - 1), col)

    qb, kb, vb = QB_COL // WIDTH_B, KB_COL // WIDTH_B, VB_COL // WIDTH_B
    return pl.pallas_call(
        functools.partial(_nat_kernel, rows_seq=rows_seq),
        out_shape=jax.ShapeDtypeStruct((n, WIDTH_B), jnp.bfloat16),
        grid=(bsz, ntile),
        in_specs=[
            pl.BlockSpec((main, WIDTH_B), main_map(qb)),
            pl.BlockSpec((halo, WIDTH_B), prev_map(kb)),
            pl.BlockSpec((main, WIDTH_B), main_map(kb)),
            pl.BlockSpec((halo, WIDTH_B), next_map(kb)),
            pl.BlockSpec((halo, WIDTH_B), prev_map(vb)),
            pl.BlockSpec((main, WIDTH_B), main_map(vb)),
            pl.BlockSpec((halo, WIDTH_B), next_map(vb)),
            pl.BlockSpec((N_HEADS_B // 2, NA_ROWS, 2 * GRID_W, NAT_KEYS), lambda b, i: (0, 0, 0, 0)),
        ],
        out_specs=pl.BlockSpec((main, WIDTH_B), main_map(0)),
        scratch_shapes=[pltpu.VMEM((main + 2 * halo, WIDTH_B), jnp.bfloat16),
                        pltpu.VMEM((main + 2 * halo, WIDTH_B), jnp.bfloat16)],
        compiler_params=_cparams("parallel", "parallel"),
        name="nat",
    )(qkv, qkv, qkv, qkv, qkv, qkv, qkv, tb)


EXPERT_ROW0 = N_GROUPS
ROUTE_ROWS = 48
INFO_E1, INFO_E2, INFO_R1, INFO_R2, INFO_W1, INFO_W2 = range(6)
INFO_ROWS = 8
MERGE_SUBTILES = 4


def _route(lt, carry, tri):
    rr, tm = lt.shape
    row = lax.broadcasted_iota(jnp.int32, (rr, tm), 0).astype(jnp.float32)
    none = jnp.float32(rr)

    def first_max(sel):
        m = jnp.max(jnp.where(sel, lt, NEG_BIG), axis=0, keepdims=True)
        idx = jnp.min(jnp.where(sel & (lt == m), row, none), axis=0, keepdims=True)
        return m, idx

    is_group = row < N_GROUPS
    mg, g = first_max(is_group)
    pg_sel = 1.0 / jnp.sum(jnp.where(is_group, jnp.exp(jnp.where(is_group, lt, mg) - mg), 0.0),
                           axis=0, keepdims=True)
    row0 = EXPERT_ROW0 + EXPERTS_PER_GROUP * g
    in_group = (row >= row0) & (row < row0 + EXPERTS_PER_GROUP)
    m1, i1 = first_max(in_group)
    m2, i2 = first_max(in_group & (row != i1))
    e2 = jnp.exp(m2 - m1)
    w1 = pg_sel / (1.0 + e2)
    w2 = pg_sel * e2 / (1.0 + e2)

    oh1 = row == i1
    oh2 = row == i2
    both = (oh1 | oh2).astype(jnp.bfloat16)
    before = jnp.dot(both, tri, preferred_element_type=jnp.float32) + carry
    r1 = jnp.sum(jnp.where(oh1, before, 0.0), axis=0, keepdims=True)
    r2 = jnp.sum(jnp.where(oh2, before, 0.0), axis=0, keepdims=True)
    new_carry = carry + jnp.sum(both.astype(jnp.float32), axis=1, keepdims=True)

    field = lax.broadcasted_iota(jnp.int32, (INFO_ROWS, tm), 0)
    info = jnp.zeros((INFO_ROWS, tm), jnp.float32)
    for k, v in ((INFO_E1, i1 - EXPERT_ROW0), (INFO_E2, i2 - EXPERT_ROW0), (INFO_R1, r1), (INFO_R2, r2),
                 (INFO_W1, w1), (INFO_W2, w2)):
        info = jnp.where(field == k, v, info)
    return info, new_carry


def _merge_kernel(x_ref, oa_ref, ob_ref, lng_ref, lnb_ref, wg_ref, wpa_ref, wpb_ref, wo_ref,
                  l1g_ref, l1b_ref, wr_ref, br_ref, cnt0_ref,
                  h1_ref, h1p_ref, info_ref, cnt_ref, carry_ref, tri_ref):
    tm = x_ref.shape[0]

    @pl.when(pl.program_id(0) == 0)
    def _():
        carry_ref[...] = cnt0_ref[...]
        r = lax.broadcasted_iota(jnp.int32, (tm, tm), 0)
        c = lax.broadcasted_iota(jnp.int32, (tm, tm), 1)
        tri_ref[...] = (r < c).astype(jnp.bfloat16)

    def project(rows):
        h = _layer_norm(x_ref[rows], lng_ref[...], lnb_ref[...])
        gates = jnp.dot(h.astype(jnp.bfloat16), wg_ref[...], preferred_element_type=jnp.float32)
        pa = jnp.dot(oa_ref[rows], wpa_ref[...], preferred_element_type=jnp.float32)
        pb = jnp.dot(ob_ref[rows], wpb_ref[...], preferred_element_type=jnp.float32)
        return h, gates, pa, pb

    def mix(h, gates, pa, pb):
        mixin = jax.nn.sigmoid(gates[:, :D_MODEL]) * pa + jax.nn.sigmoid(gates[:, D_MODEL:]) * pb
        return DEEPNORM_ALPHA * h + jnp.dot(mixin.astype(jnp.bfloat16), wo_ref[...],
                                            preferred_element_type=jnp.float32)

    def norm_and_logits(pre, rows):
        h1 = _layer_norm(pre, l1g_ref[...], l1b_ref[...])
        h1_ref[rows] = h1
        h1p_ref[rows] = _pack_rows(h1)
        return lax.dot_general(wr_ref[...], h1.astype(jnp.bfloat16), (((1,), (1,)), ((), ())),
                               preferred_element_type=jnp.float32)

    sub = tm // MERGE_SUBTILES
    parts = [slice(k * sub, (k + 1) * sub) for k in range(MERGE_SUBTILES)]
    projected = [project(rows) for rows in parts]
    mixed = [mix(*pr) for pr in projected]
    logits_t = jnp.concatenate([norm_and_logits(pre, rows) for pre, rows in zip(mixed, parts)], axis=1)
    logits_t = logits_t + br_ref[...]
    info, carry = _route(logits_t, carry_ref[...], tri_ref[...])
    info_ref[...] = info
    carry_ref[...] = carry
    cnt_ref[...] = carry[:, :LANES]


def _merge(x2, oa, ob, ln_g, ln_b, w_gates, w_pa, w_pb, w_o, l1g, l1b, w_r, b_r, cnt0, tm):
    n = x2.shape[0]

    def const(shape):
        return pl.BlockSpec(shape, lambda i: (0,) * len(shape))

    def rows(width):
        return pl.BlockSpec((tm, width), lambda i: (i, 0))

    return pl.pallas_call(
        _merge_kernel,
        out_shape=(jax.ShapeDtypeStruct((n, D_MODEL), jnp.float32),
                   jax.ShapeDtypeStruct((n, PACKED_WIDTH), jnp.uint32),
                   jax.ShapeDtypeStruct((INFO_ROWS, n), jnp.float32),
                   jax.ShapeDtypeStruct((ROUTE_ROWS, LANES), jnp.float32)),
        grid=(n // tm,),
        in_specs=[
            rows(D_MODEL), rows(WIDTH_A), rows(WIDTH_B),
            const((1, D_MODEL)), const((1, D_MODEL)),
            const((D_MODEL, 2 * D_MODEL)),
            const((WIDTH_A, D_MODEL)), const((WIDTH_B, D_MODEL)),
            const((D_MODEL, D_MODEL)),
            const((1, D_MODEL)), const((1, D_MODEL)),
            const((ROUTE_ROWS, D_MODEL)), const((ROUTE_ROWS, tm)), const((ROUTE_ROWS, tm)),
        ],
        out_specs=(rows(D_MODEL), rows(PACKED_WIDTH), pl.BlockSpec((INFO_ROWS, tm), lambda i: (0, i)),
                   const((ROUTE_ROWS, LANES))),
        scratch_shapes=[pltpu.VMEM((ROUTE_ROWS, tm), jnp.float32), pltpu.VMEM((tm, tm), jnp.bfloat16)],
        compiler_params=_cparams("arbitrary"),
        name="merge",
    )(x2, oa, ob, ln_g, ln_b, w_gates, w_pa, w_pb, w_o, l1g, l1b, w_r, b_r, cnt0)


SC_CORES = 2
SC_SUBCORES = 16
SC_WORKERS = SC_CORES * SC_SUBCORES
SC_ROWS_PER_STREAM = 64


def _sc_worker():
    return lax.axis_index("s") * SC_CORES + lax.axis_index("c")


def _dispatch(h1p, dest_t, pad_rows, nrows_out):
    n, width = h1p.shape
    per_worker = n // SC_WORKERS
    nchunks = per_worker // SC_ROWS_PER_STREAM
    assert nchunks * SC_ROWS_PER_STREAM * SC_WORKERS == n and nchunks % 2 == 0
    idx = dest_t.reshape(TOP_K, SC_WORKERS, nchunks, SC_ROWS_PER_STREAM)
    npad = pad_rows.size // (SC_WORKERS * SC_ROWS_PER_STREAM)
    assert npad * SC_WORKERS * SC_ROWS_PER_STREAM == pad_rows.size
    pad_idx = pad_rows.reshape(SC_WORKERS, npad, SC_ROWS_PER_STREAM)
    zeros = jnp.zeros((SC_ROWS_PER_STREAM, width), h1p.dtype)
    mesh = plsc.VectorSubcoreMesh(core_axis_name="c", subcore_axis_name="s")

    @functools.partial(
        pl.kernel, out_type=jax.ShapeDtypeStruct((nrows_out, width), h1p.dtype), mesh=mesh,
        scratch_types=[pltpu.VMEM((TOP_K, nchunks, SC_ROWS_PER_STREAM), jnp.int32),
                       pltpu.VMEM((npad, SC_ROWS_PER_STREAM), jnp.int32),
                       pltpu.VMEM((2, SC_ROWS_PER_STREAM, width), h1p.dtype),
                       pltpu.SemaphoreType.DMA((2,)), pltpu.SemaphoreType.DMA((2,))],
        name="sc_dispatch")
    def scatter_kernel(src_hbm, idx_hbm, pad_hbm, zeros_hbm, out_hbm, idx_v, pad_v, rows_v, rsem, ssem):
        wid = _sc_worker()
        base = wid * per_worker
        for k in range(TOP_K):
            pltpu.sync_copy(idx_hbm.at[k, wid], idx_v.at[k])
        pltpu.sync_copy(pad_hbm.at[wid], pad_v)

        def read(j, slot):
            src = src_hbm.at[pl.ds(base + j * SC_ROWS_PER_STREAM, SC_ROWS_PER_STREAM)]
            return pltpu.make_async_copy(src, rows_v.at[slot], rsem.at[slot])

        def scatter(j, slot, k):
            return pltpu.make_async_copy(rows_v.at[slot], out_hbm.at[idx_v.at[k, j]], ssem.at[slot])

        read(0, 0).start()

        @pl.loop(0, nchunks, step=2)
        def _(j0):
            for slot in range(2):
                j = j0 + slot
                read(j, slot).wait()

                @pl.when(j + 1 < nchunks)
                def _():
                    @pl.when(j >= 1)
                    def _():
                        for k in range(TOP_K):
                            scatter(j - 1, 1 - slot, k).wait()
                    read(j + 1, 1 - slot).start()

                for k in range(TOP_K):
                    scatter(j, slot, k).start()

        for k in range(TOP_K):
            scatter(nchunks - 2, 0, k).wait()
            scatter(nchunks - 1, 1, k).wait()

        pltpu.sync_copy(zeros_hbm, rows_v.at[0])
        fills = [pltpu.make_async_copy(rows_v.at[0], out_hbm.at[pad_v.at[c]], ssem.at[0]) for c in range(npad)]
        for f in fills:
            f.start()
        for f in fills:
            f.wait()

    return scatter_kernel(h1p, idx, pad_idx, zeros)


def _sorted_layout(counts, eid, rank, bm):
    n = eid.shape[1]
    nblocks = TOP_K * n // bm + N_EXPERTS
    expert = jnp.arange(N_EXPERTS, dtype=jnp.int32)
    before = expert[None, :] < expert[:, None]
    blocks = (counts + bm - 1) // bm
    first_blk = jnp.sum(jnp.where(before, blocks[None, :], 0), axis=1)
    starts = first_blk * bm
    dest = rank + jnp.sum(jnp.where(eid[None] == expert[:, None, None], starts[:, None, None], 0), axis=0)
    total = jnp.sum(blocks)
    j = jnp.arange(bm, dtype=jnp.int32)[None, :]
    npad = blocks * bm - counts
    spare_before = jnp.sum(jnp.where(before, (bm - npad)[None, :], 0), axis=1)
    pad_rows = jnp.where(j < npad[:, None], (starts + counts)[:, None] + j,
                         (total * bm + spare_before - npad)[:, None] + j).astype(jnp.int32)
    w = jnp.arange(nblocks, dtype=jnp.int32)
    src = jnp.minimum(w, total - 1)
    blk_end = first_blk + blocks
    owner = jnp.minimum(jnp.sum((blk_end[None, :] <= src[:, None]).astype(jnp.int32), axis=1), N_EXPERTS - 1)
    valid = (w < total).astype(jnp.int32)
    return dest.astype(jnp.int32), pad_rows, (src.astype(jnp.int32), owner.astype(jnp.int32), valid)


def _expert_kernel(src_ref, e_ref, valid_ref, x_ref, wg_ref, wu_ref, wd_ref, o_ref):
    w = pl.program_id(0)

    @pl.when(valid_ref[w] != 0)
    def _():
        x = _unpack_rows(x_ref[...]).astype(jnp.bfloat16)
        g = jnp.dot(x, wg_ref[0], preferred_element_type=jnp.float32)
        u = jnp.dot(x, wu_ref[0], preferred_element_type=jnp.float32)
        hmid = (jax.nn.silu(g) * u).astype(jnp.bfloat16)
        o_ref[...] = _pack_rows(jnp.dot(hmid, wd_ref[0], preferred_element_type=jnp.float32))

    @pl.when(valid_ref[w] == 0)
    def _():
        o_ref[...] = jnp.zeros(o_ref.shape, o_ref.dtype)


def _experts(xs, plan, w_gate, w_up, w_down, bm):
    nblocks = plan[0].shape[0]
    assert xs.shape[0] == nblocks * bm

    def weights(shape):
        return pl.BlockSpec((1,) + shape, lambda w, src, e, *_: (e[w], 0, 0))

    grid_spec = pltpu.PrefetchScalarGridSpec(
        num_scalar_prefetch=len(plan),
        grid=(nblocks,),
        in_specs=[
            pl.BlockSpec((bm, PACKED_WIDTH), lambda w, src, *_: (src[w], 0)),
            weights((D_MODEL, D_EXPERT)), weights((D_MODEL, D_EXPERT)), weights((D_EXPERT, D_MODEL)),
        ],
        out_specs=pl.BlockSpec((bm, PACKED_WIDTH), lambda w, *_: (w, 0)),
    )
    return pl.pallas_call(
        _expert_kernel,
        out_shape=jax.ShapeDtypeStruct(xs.shape, jnp.uint32),
        grid_spec=grid_spec,
        compiler_params=_cparams("parallel"),
        name="experts",
    )(*plan, xs, w_gate, w_up, w_down)


def _sc_gather_rows(table, idx):
    nrows = idx.shape[0]
    width = table.shape[1]
    per_worker = nrows // SC_WORKERS
    nchunks = per_worker // SC_ROWS_PER_STREAM
    assert nchunks * SC_ROWS_PER_STREAM * SC_WORKERS == nrows and nchunks % 2 == 0
    mesh = plsc.VectorSubcoreMesh(core_axis_name="c", subcore_axis_name="s")

    @functools.partial(
        pl.kernel, out_type=jax.ShapeDtypeStruct((nrows, width), table.dtype), mesh=mesh,
        scratch_types=[pltpu.VMEM((per_worker,), jnp.int32),
                       pltpu.VMEM((2, SC_ROWS_PER_STREAM, width), table.dtype),
                       pltpu.SemaphoreType.DMA((2,)), pltpu.SemaphoreType.DMA((2,))],
        name="sc_gather")
    def gather_kernel(table_hbm, idx_hbm, out_hbm, idx_v, rows_v, gsem, wsem):
        base = _sc_worker() * per_worker
        pltpu.sync_copy(idx_hbm.at[pl.ds(base, per_worker)], idx_v)

        def gather(j, slot):
            rows = idx_v.at[pl.ds(j * SC_ROWS_PER_STREAM, SC_ROWS_PER_STREAM)]
            return pltpu.make_async_copy(table_hbm.at[rows], rows_v.at[slot], gsem.at[slot])

        def write(j, slot):
            dst = out_hbm.at[pl.ds(base + j * SC_ROWS_PER_STREAM, SC_ROWS_PER_STREAM)]
            return pltpu.make_async_copy(rows_v.at[slot], dst, wsem.at[slot])

        gather(0, 0).start()

        @pl.loop(0, nchunks, step=2)
        def _(j0):
            for slot in range(2):
                j = j0 + slot
                gather(j, slot).wait()

                @pl.when(j + 1 < nchunks)
                def _():
                    @pl.when(j >= 1)
                    def _():
                        write(j - 1, 1 - slot).wait()
                    gather(j + 1, 1 - slot).start()

                write(j, slot).start()

        write(nchunks - 2, 0).wait()
        write(nchunks - 1, 1).wait()

    return gather_kernel(table, idx)


def _finalize_kernel(h1_ref, y1_ref, y2_ref, info_ref, g_ref, b_ref, o_ref):
    tt = h1_ref.shape[0]
    pad = jnp.zeros((LANES - INFO_ROWS, tt), jnp.float32)
    info = jnp.concatenate([info_ref[...], pad], axis=0).T
    moe = (_unpack_rows(y1_ref[...]) * info[:, INFO_W1:INFO_W1 + 1]
           + _unpack_rows(y2_ref[...]) * info[:, INFO_W2:INFO_W2 + 1])
    o_ref[...] = _layer_norm(DEEPNORM_ALPHA * h1_ref[...] + moe, g_ref[...], b_ref[...])


def _combine(h1, info, dest_t, ys, ln_g, ln_b, tt):
    n = h1.shape[0]
    nsteps = n // tt
    yg = _sc_gather_rows(ys, dest_t.reshape(TOP_K * n))
    return pl.pallas_call(
        _finalize_kernel,
        out_shape=jax.ShapeDtypeStruct((n, D_MODEL), jnp.float32),
        grid=(nsteps,),
        in_specs=[
            pl.BlockSpec((tt, D_MODEL), lambda i: (i, 0)),
            pl.BlockSpec((tt, PACKED_WIDTH), lambda i: (i, 0)),
            pl.BlockSpec((tt, PACKED_WIDTH), lambda i: (nsteps + i, 0)),
            pl.BlockSpec((INFO_ROWS, tt), lambda i: (0, i)),
            pl.BlockSpec((1, D_MODEL), lambda i: (0, 0)),
            pl.BlockSpec((1, D_MODEL), lambda i: (0, 0)),
        ],
        out_specs=pl.BlockSpec((tt, D_MODEL), lambda i: (i, 0)),
        compiler_params=_cparams("parallel"),
        name="finalize",
    )(h1, yg, yg, info, ln_g, ln_b)


TM_QKV = 1024
TQ_WIN = 1024
TM_MERGE = 1024
TT_ROWS = 1024
BM_EXPERT = 512


def _prepare_weights(ln_in_g, ln_in_b, w_in, attn_sink, rel_pos_bias, w_proj_a, w_proj_b, w_out,
                     ln1_g, ln1_b, w_route_group, b_route_group, w_route_expert, b_route_expert,
                     ln2_g, ln2_b):
    bf = jnp.bfloat16
    w = w_in[0]
    splits = np.cumsum([WIDTH_A, KV_WIDTH_A, KV_WIDTH_A, WIDTH_B, WIDTH_B, WIDTH_B, D_MODEL])
    wqa, wka, wva, wqb, wkb, wvb, wga, wgb = jnp.split(w, [int(s) for s in splits], axis=1)
    wqa = (wqa.reshape(D_MODEL, N_KV_HEADS_A, GQA_GROUP, HEAD_DIM).transpose(0, 2, 1, 3)
           .reshape(D_MODEL, WIDTH_A))
    w_qkv = jnp.concatenate([wqa, wqb, wkb, wvb, wka, wva], axis=1).astype(bf)
    w_gates = jnp.concatenate([wga, wgb], axis=1).astype(bf)
    w_pa = (w_proj_a[0].reshape(N_KV_HEADS_A, GQA_GROUP, HEAD_DIM, D_MODEL).transpose(1, 0, 2, 3)
            .reshape(WIDTH_A, D_MODEL).astype(bf))
    w_pb = w_proj_b[0].astype(bf)
    w_o = w_out[0].astype(bf)
    pad = ROUTE_ROWS - N_GROUPS - N_EXPERTS
    w_r = jnp.concatenate([w_route_group[0].T, w_route_expert[0].T, jnp.zeros((pad, D_MODEL), jnp.float32)], axis=0)
    w_route = w_r.astype(bf)
    b_r = jnp.concatenate([b_route_group[0], b_route_expert[0], jnp.zeros((pad,), jnp.float32)])
    b_r = jnp.broadcast_to(b_r[:, None], (ROUTE_ROWS, TM_MERGE))
    row = lambda v: v.reshape(1, D_MODEL)
    return dict(
        ln_in_g=row(ln_in_g), ln_in_b=row(ln_in_b), w_qkv=w_qkv, w_gates=w_gates,
        sink=attn_sink[0].astype(jnp.float32), nat_bias=_nat_bias_table(rel_pos_bias[0]),
        w_pa=w_pa, w_pb=w_pb, w_o=w_o, ln1_g=row(ln1_g[0]), ln1_b=row(ln1_b[0]),
        w_route=w_route, b_r=b_r, ln2_g=row(ln2_g[0]), ln2_b=row(ln2_b[0]))


def _after(value, other):
    if other is None:
        return value
    other = other.astype(jnp.float32)
    zero = jnp.where(jnp.isfinite(other), other, 0.0) * 0.0
    return value + zero.astype(value.dtype)


def _attend_and_route(x, p, after=None, cast=()):
    bsz, t, _ = x.shape
    n = bsz * t
    x2 = x.reshape(n, D_MODEL)
    qkv, cast_bf16 = _qkv(x2, p["ln_in_g"], p["ln_in_b"], p["w_qkv"], TM_QKV, cast)
    oa = _win_attention(qkv, p["sink"], bsz, t, TQ_WIN)
    ob = _nat_attention(qkv, p["nat_bias"], bsz, t)
    cnt0 = _after(jnp.zeros((ROUTE_ROWS, TM_MERGE), jnp.float32), after)
    h1, h1p, info, cnt = _merge(x2, oa, ob, p["ln_in_g"], p["ln_in_b"], p["w_gates"], p["w_pa"], p["w_pb"],
                                p["w_o"], p["ln1_g"], p["ln1_b"], p["w_route"], p["b_r"], cnt0, TM_MERGE)
    counts = cnt[EXPERT_ROW0:EXPERT_ROW0 + N_EXPERTS, 0].astype(jnp.int32)
    eid = info[INFO_E1:INFO_E2 + 1].astype(jnp.int32)
    rank = info[INFO_R1:INFO_R2 + 1].astype(jnp.int32)
    dest_t, pad_rows, plan = _sorted_layout(counts, eid, rank, BM_EXPERT)
    return dict(shape=x.shape, h1=h1, h1p=h1p, info=info, counts=counts, dest_t=dest_t, pad_rows=pad_rows,
                plan=plan, cast=cast_bf16)


def _run_experts(r, expert_weights, after=None):
    src, owner, valid = r["plan"]
    xs = _dispatch(r["h1p"], r["dest_t"], r["pad_rows"], src.shape[0] * BM_EXPERT)
    return _experts(xs, (src, owner, _after(valid, after)), *expert_weights, BM_EXPERT)


def _finish(r, ys, p, after=None):
    out = _combine(r["h1"], r["info"], r["dest_t"], ys, _after(p["ln2_g"], after), p["ln2_b"], TT_ROWS)
    return out.reshape(r["shape"])


def kernel(x_prompt, x_sample, ln_in_g, ln_in_b, w_in, attn_sink, rel_pos_bias, w_proj_a, w_proj_b, w_out,
           ln1_g, ln1_b, w_route_group, b_route_group, w_route_expert, b_route_expert,
           w_gate, w_up, w_down, ln2_g, ln2_b):
    p = _prepare_weights(ln_in_g, ln_in_b, w_in, attn_sink, rel_pos_bias, w_proj_a, w_proj_b, w_out,
                         ln1_g, ln1_b, w_route_group, b_route_group, w_route_expert, b_route_expert,
                         ln2_g, ln2_b)
    rp = _attend_and_route(x_prompt, p, cast=(w_gate[0], w_up[0], w_down[0]))
    rs = _attend_and_route(x_sample, p, after=rp["counts"][0])
    ys_p = _run_experts(rp, rp["cast"])
    ys_s = _run_experts(rs, rp["cast"], after=ys_p[0, 0])
    y_prompt = _finish(rp, ys_p, p)
    y_sample = _finish(rs, ys_s, p, after=y_prompt[0, 0, 0])
    return (y_prompt, y_sample)
```

```python
import functools

import numpy as np
import jax
import jax.numpy as jnp
from jax import lax
from jax.experimental import pallas as pl
from jax.experimental.pallas import tpu as pltpu
from jax.experimental.pallas import tpu_sc as plsc

D_MODEL = 1024
HEAD_DIM = 64
N_HEADS_A = 8
N_KV_HEADS_A = 2
WINDOW = 128
N_HEADS_B = 8
GRID_W = 64
NA_ROWS = 8
NA_COLS = 16
N_GROUPS = 4
EXPERTS_PER_GROUP = 8
N_EXPERTS = N_GROUPS * EXPERTS_PER_GROUP
TOP_K = 2
D_EXPERT = D_MODEL // 2
LN_EPS = 1e-5
DEPTH = 1
DEEPNORM_ALPHA = (2.0 * DEPTH) ** 0.25
WIDTH_A = N_HEADS_A * HEAD_DIM
KV_WIDTH_A = N_KV_HEADS_A * HEAD_DIM
WIDTH_B = N_HEADS_B * HEAD_DIM
QKV_WIDTH = WIDTH_A + 2 * KV_WIDTH_A + 3 * WIDTH_B

LANES = 128
VMEM_LIMIT_BYTES = 56 * 1024 * 1024

NEG_BIG = -1e30
LOG2E = float(np.log2(np.e))

QA_COL, QB_COL, KB_COL, VB_COL = 0, WIDTH_A, WIDTH_A + WIDTH_B, WIDTH_A + 2 * WIDTH_B
KA_COL = WIDTH_A + 3 * WIDTH_B
VA_COL = KA_COL + KV_WIDTH_A

GQA_GROUP = N_HEADS_A // N_KV_HEADS_A


def _cparams(*sem):
    return pltpu.CompilerParams(dimension_semantics=sem, vmem_limit_bytes=VMEM_LIMIT_BYTES)


def _layer_norm(x, g, b):
    mu = jnp.mean(x, axis=-1, keepdims=True)
    xc = x - mu
    var = jnp.mean(xc * xc, axis=-1, keepdims=True)
    return xc * lax.rsqrt(var + LN_EPS) * g + b


PACKED_WIDTH = D_MODEL // 2


def _pack_rows(x):
    def rne(v):
        return v + jnp.uint32(0x7FFF) + ((v >> 16) & jnp.uint32(1))
    hi = lax.bitcast_convert_type(x[:, :PACKED_WIDTH], jnp.uint32)
    lo = lax.bitcast_convert_type(x[:, PACKED_WIDTH:], jnp.uint32)
    return (rne(hi) & jnp.uint32(0xFFFF0000)) | (rne(lo) >> 16)


def _unpack_rows(w):
    hi = lax.bitcast_convert_type(w & jnp.uint32(0xFFFF0000), jnp.float32)
    lo = lax.bitcast_convert_type(w << 16, jnp.float32)
    return jnp.concatenate([hi, lo], axis=1)


def _qkv_kernel(x_ref, g_ref, b_ref, w_ref, *rest):
    o_ref = rest[len(rest) // 2]
    h = _layer_norm(x_ref[...], g_ref[...], b_ref[...])
    y = jnp.dot(h.astype(jnp.bfloat16), w_ref[...], preferred_element_type=jnp.float32)
    col = lax.broadcasted_iota(jnp.int32, (1, QKV_WIDTH), 1)
    y = y * jnp.where(col < KB_COL, HEAD_DIM ** -0.5 * LOG2E, 1.0)
    o_ref[...] = y.astype(jnp.bfloat16)
    ncast = len(rest) // 2
    for src_ref, dst_ref in zip(rest[:ncast], rest[ncast + 1:]):
        dst_ref[...] = src_ref[...].astype(jnp.bfloat16)


def _qkv(x2, ln_g, ln_b, w_qkv, tm, cast=()):
    n = x2.shape[0]
    steps = n // tm
    per_step = -(-N_EXPERTS // steps)
    assert all(c.shape[0] == N_EXPERTS for c in cast) and (steps * per_step) % N_EXPERTS == 0
    revisit = steps * per_step // N_EXPERTS

    def expert_block(c):
        return pl.BlockSpec((per_step,) + c.shape[1:], lambda i: (i // revisit, 0, 0))

    outs = pl.pallas_call(
        _qkv_kernel,
        out_shape=[jax.ShapeDtypeStruct((n, QKV_WIDTH), jnp.bfloat16)]
        + [jax.ShapeDtypeStruct(c.shape, jnp.bfloat16) for c in cast],
        grid=(steps,),
        in_specs=[
            pl.BlockSpec((tm, D_MODEL), lambda i: (i, 0)),
            pl.BlockSpec((1, D_MODEL), lambda i: (0, 0)),
            pl.BlockSpec((1, D_MODEL), lambda i: (0, 0)),
            pl.BlockSpec((D_MODEL, QKV_WIDTH), lambda i: (0, 0)),
        ] + [expert_block(c) for c in cast],
        out_specs=[pl.BlockSpec((tm, QKV_WIDTH), lambda i: (i, 0))] + [expert_block(c) for c in cast],
        compiler_params=_cparams("arbitrary" if cast else "parallel"),
        name="qkv",
    )(x2, ln_g, ln_b, w_qkv, *cast)
    return outs[0], tuple(outs[1:])


WIN_BLK = 128
WIN_LOOKAHEAD = 2


def _win_bias_table():
    qi = np.arange(WIN_BLK)[:, None]
    kj = np.arange(3 * WIN_BLK)[None, :]
    dist = np.abs(kj - WIN_BLK - qi).astype(np.float64)
    slopes = 2.0 ** (-8.0 * np.arange(1, N_HEADS_A + 1) / N_HEADS_A)
    per_head = np.where(dist <= WINDOW, -slopes[:, None, None] * dist[None] * LOG2E, NEG_BIG)
    groups = [np.concatenate([per_head[j], per_head[j + 4]], axis=0) for j in range(4)]
    return np.stack(groups).astype(np.float32)


def _win_kernel(sink_ref, q_ref, kp_ref, km_ref, kn_ref, vp_ref, vm_ref, vn_ref, bias_ref, o_ref,
                *, nsub, nblk_seq):
    i = pl.program_id(1)
    kcat = jnp.concatenate([kp_ref[...], km_ref[...], kn_ref[...]], axis=0)
    vcat = jnp.concatenate([vp_ref[...], vm_ref[...], vn_ref[...]], axis=0)
    lo = lax.broadcasted_iota(jnp.int32, (1, LANES), 1) < HEAD_DIM
    col = lax.broadcasted_iota(jnp.int32, (1, 3 * WIN_BLK), 1)
    top = lax.broadcasted_iota(jnp.int32, (2 * WIN_BLK, 1), 0) < WIN_BLK
    zero = jnp.zeros((), jnp.bfloat16)

    def scores(j, g):
        n = i * nsub + j
        off_seq = ((col < WIN_BLK) & (n == 0)) | ((col >= 2 * WIN_BLK) & (n == nblk_seq - 1))
        edge = jnp.where(off_seq, NEG_BIG, 0.0)
        qg = q_ref[WIN_BLK * j:WIN_BLK * (j + 1), LANES * g:LANES * (g + 1)]
        qm = jnp.concatenate([jnp.where(lo, qg, zero), jnp.where(lo, zero, qg)], axis=0)
        kj = kcat[WIN_BLK * j:WIN_BLK * (j + 3)]
        s = lax.dot_general(qm, kj, (((1,), (1,)), ((), ())), preferred_element_type=jnp.float32)
        return s + bias_ref[g] + edge

    def attend(s, j, g):
        vj = vcat[WIN_BLK * j:WIN_BLK * (j + 3)]
        sink = jnp.where(top, sink_ref[g], sink_ref[g + 4]) * LOG2E
        m = jnp.maximum(jnp.max(s, axis=-1, keepdims=True), sink)
        p = jnp.exp2(s - m)
        l = jnp.sum(p, axis=-1, keepdims=True) + jnp.exp2(sink - m)
        o2 = jnp.dot(p.astype(jnp.bfloat16), vj, preferred_element_type=jnp.float32)
        o2 = o2 * (1.0 / l)
        o_ref[WIN_BLK * j:WIN_BLK * (j + 1), LANES * g:LANES * (g + 1)] = (
            jnp.where(lo, o2[:WIN_BLK], o2[WIN_BLK:]).astype(jnp.bfloat16))

    chains = [(j, g) for j in range(nsub) for g in range(4)]
    pending = [scores(*c) for c in chains[:WIN_LOOKAHEAD]]
    for idx, c in enumerate(chains):
        s = pending.pop(0)
        if idx + WIN_LOOKAHEAD < len(chains):
            pending.append(scores(*chains[idx + WIN_LOOKAHEAD]))
        attend(s, *c)


def _win_attention(qkv, sink, bsz, t, tq):
    n = bsz * t
    nsub = tq // WIN_BLK
    nblk_seq = t // WIN_BLK
    ntile = t // tq
    bias = jnp.asarray(_win_bias_table())

    def main_map(col):
        return lambda b, i, *_: (b * ntile + i, col)

    def prev_map(col):
        return lambda b, i, *_: (b * nblk_seq + jnp.maximum(i * nsub - 1, 0), col)

    def next_map(col):
        return lambda b, i, *_: (b * nblk_seq + jnp.minimum(i * nsub + nsub, nblk_seq - 1), col)

    halo = (WIN_BLK, LANES)
    ka, va = KA_COL // LANES, VA_COL // LANES
    grid_spec = pltpu.PrefetchScalarGridSpec(
        num_scalar_prefetch=1,
        grid=(bsz, ntile),
        in_specs=[
            pl.BlockSpec((tq, WIDTH_A), main_map(QA_COL // WIDTH_A)),
            pl.BlockSpec(halo, prev_map(ka)),
            pl.BlockSpec((tq, LANES), main_map(ka)),
            pl.BlockSpec(halo, next_map(ka)),
            pl.BlockSpec(halo, prev_map(va)),
            pl.BlockSpec((tq, LANES), main_map(va)),
            pl.BlockSpec(halo, next_map(va)),
            pl.BlockSpec((4, 2 * WIN_BLK, 3 * WIN_BLK), lambda b, i, *_: (0, 0, 0)),
        ],
        out_specs=pl.BlockSpec((tq, WIDTH_A), main_map(0)),
    )
    return pl.pallas_call(
        functools.partial(_win_kernel, nsub=nsub, nblk_seq=nblk_seq),
        out_shape=jax.ShapeDtypeStruct((n, WIDTH_A), jnp.bfloat16),
        grid_spec=grid_spec,
        compiler_params=_cparams("parallel", "parallel"),
        name="win",
    )(sink, qkv, qkv, qkv, qkv, qkv, qkv, qkv, bias)


NAT_ROWS_PER_STEP = 16
NAT_HALO_ROWS = NA_ROWS // 2
NAT_KEYS = NA_ROWS * GRID_W
NAT_ROWS_PER_TRIP = 8
NAT_HEADS_PER_CHAIN = 2
NAT_LOOKAHEAD = 4


def _nat_bias_table(rpb):
    c = np.arange(GRID_W)
    cs = np.clip(c - NA_COLS // 2, 0, GRID_W - NA_COLS)
    col_mask = (c[None, :] >= cs[:, None]) & (c[None, :] < cs[:, None] + NA_COLS)
    dc = np.clip(c[None, :] - c[:, None] + (NA_COLS - 1), 0, 2 * NA_COLS - 2)
    onehot = jnp.asarray(dc[None] == np.arange(2 * NA_COLS - 1)[:, None, None], jnp.float32)
    picked = jnp.einsum("hdj,jqc->hqdc", rpb, onehot, precision=lax.Precision.HIGHEST)
    t1 = jnp.where(col_mask[None, :, None, :], picked * LOG2E, NEG_BIG)
    flat = t1.reshape(N_HEADS_B, GRID_W, (2 * NA_ROWS - 1) * GRID_W)
    shifts = jnp.stack([flat[:, :, sh * GRID_W:sh * GRID_W + NAT_KEYS] for sh in range(NA_ROWS)], axis=1)
    tb = shifts.reshape(N_HEADS_B // 2, 2, NA_ROWS, GRID_W, NAT_KEYS).transpose(0, 2, 1, 3, 4)
    return tb.reshape(N_HEADS_B // 2, NA_ROWS, 2 * GRID_W, NAT_KEYS)


def _nat_kernel(q_ref, kp_ref, km_ref, kn_ref, vp_ref, vm_ref, vn_ref, tb_ref, o_ref, kcat, vcat,
                *, rows_seq):
    i = pl.program_id(1)
    halo = NAT_HALO_ROWS * GRID_W
    main = NAT_ROWS_PER_STEP * GRID_W
    kcat[0:halo] = kp_ref[...]
    kcat[halo:halo + main] = km_ref[...]
    kcat[halo + main:2 * halo + main] = kn_ref[...]
    vcat[0:halo] = vp_ref[...]
    vcat[halo:halo + main] = vm_ref[...]
    vcat[halo + main:2 * halo + main] = vn_ref[...]
    width = NAT_HEADS_PER_CHAIN * HEAD_DIM
    head_of_lane = lax.broadcasted_iota(jnp.int32, (1, width), 1) // HEAD_DIM
    zero = jnp.zeros((), jnp.bfloat16)
    r0 = i * NAT_ROWS_PER_STEP

    def scores(qr, c):
        r = r0 + qr
        rs = jnp.clip(r - NA_ROWS // 2, 0, rows_seq - NA_ROWS)
        koff = pl.multiple_of((rs - r0 + NAT_HALO_ROWS) * GRID_W, GRID_W)
        sh = rs - r + (NA_ROWS - 1)
        qoff = pl.multiple_of(qr * GRID_W, GRID_W)
        cols = slice(width * c, width * (c + 1))
        qc = q_ref[pl.ds(qoff, GRID_W), cols]
        qm = jnp.concatenate([jnp.where(head_of_lane == h, qc, zero) for h in range(NAT_HEADS_PER_CHAIN)], axis=0)
        kw = kcat[pl.ds(koff, NAT_KEYS), cols]
        s = lax.dot_general(qm, kw, (((1,), (1,)), ((), ())), preferred_element_type=jnp.float32)
        pairs = NAT_HEADS_PER_CHAIN // 2
        bias = jnp.concatenate([tb_ref[pairs * c + k, sh] for k in range(pairs)], axis=0)
        return s + bias, koff, qoff

    def attend(s, koff, qoff, c):
        cols = slice(width * c, width * (c + 1))
        vw = vcat[pl.ds(koff, NAT_KEYS), cols]
        m = jnp.max(s, axis=-1, keepdims=True)
        pe = jnp.exp2(s - m)
        l = jnp.sum(pe, axis=-1, keepdims=True)
        o2 = jnp.dot(pe.astype(jnp.bfloat16), vw, preferred_element_type=jnp.float32)
        o2 = o2 * (1.0 / l)
        out = o2[:GRID_W]
        for h in range(1, NAT_HEADS_PER_CHAIN):
            out = jnp.where(head_of_lane == h, o2[GRID_W * h:GRID_W * (h + 1)], out)
        o_ref[pl.ds(qoff, GRID_W), cols] = out.astype(jnp.bfloat16)

    def trip(j, carry):
        chains = [(j * NAT_ROWS_PER_TRIP + q, c) for q in range(NAT_ROWS_PER_TRIP)
                  for c in range(N_HEADS_B // NAT_HEADS_PER_CHAIN)]
        pending = [scores(*c) for c in chains[:NAT_LOOKAHEAD]]
        for idx, (_, p) in enumerate(chains):
            s, koff, qoff = pending.pop(0)
            if idx + NAT_LOOKAHEAD < len(chains):
                pending.append(scores(*chains[idx + NAT_LOOKAHEAD]))
            attend(s, koff, qoff, p)
        return carry

    lax.fori_loop(0, NAT_ROWS_PER_STEP // NAT_ROWS_PER_TRIP, trip, 0)


def _nat_attention(qkv, tb, bsz, t):
    n = bsz * t
    rows_seq = t // GRID_W
    main = NAT_ROWS_PER_STEP * GRID_W
    halo = NAT_HALO_ROWS * GRID_W
    ntile = t // main
    nhalo_seq = t // halo
    per = main // halo

    def main_map(col):
        return lambda b, i: (b * ntile + i, col)

    def prev_map(col):
        return lambda b, i: (b * nhalo_seq + jnp.maximum(i * per - 1, 0), col)

    def next_map(col):
        return lambda b, i: (b * nhalo_seq + jnp.minimum(i * per + per, nhalo_seq - 1), col)

    qb, kb, vb = QB_COL // WIDTH_B, KB_COL // WIDTH_B, VB_COL // WIDTH_B
    return pl.pallas_call(
        functools.partial(_nat_kernel, rows_seq=rows_seq),
        out_shape=jax.ShapeDtypeStruct((n, WIDTH_B), jnp.bfloat16),
        grid=(bsz, ntile),
        in_specs=[
            pl.BlockSpec((main, WIDTH_B), main_map(qb)),
            pl.BlockSpec((halo, WIDTH_B), prev_map(kb)),
            pl.BlockSpec((main, WIDTH_B), main_map(kb)),
            pl.BlockSpec((halo, WIDTH_B), next_map(kb)),
            pl.BlockSpec((halo, WIDTH_B), prev_map(vb)),
            pl.BlockSpec((main, WIDTH_B), main_map(vb)),
            pl.BlockSpec((halo, WIDTH_B), next_map(vb)),
            pl.BlockSpec((N_HEADS_B // 2, NA_ROWS, 2 * GRID_W, NAT_KEYS), lambda b, i: (0, 0, 0, 0)),
        ],
        out_specs=pl.BlockSpec((main, WIDTH_B), main_map(0)),
        scratch_shapes=[pltpu.VMEM((main + 2 * halo, WIDTH_B), jnp.bfloat16),
                        pltpu.VMEM((main + 2 * halo, WIDTH_B), jnp.bfloat16)],
        compiler_params=_cparams("parallel", "parallel"),
        name="nat",
    )(qkv, qkv, qkv, qkv, qkv, qkv, qkv, tb)


EXPERT_ROW0 = N_GROUPS
ROUTE_ROWS = 48
INFO_E1, INFO_E2, INFO_R1, INFO_R2, INFO_W1, INFO_W2 = range(6)
INFO_ROWS = 8
MERGE_SUBTILES = 4


def _route(lt, carry, tri):
    rr, tm = lt.shape
    row = lax.broadcasted_iota(jnp.int32, (rr, tm), 0).astype(jnp.float32)
    none = jnp.float32(rr)

    def first_max(sel):
        m = jnp.max(jnp.where(sel, lt, NEG_BIG), axis=0, keepdims=True)
        idx = jnp.min(jnp.where(sel & (lt == m), row, none), axis=0, keepdims=True)
        return m, idx

    is_group = row < N_GROUPS
    mg, g = first_max(is_group)
    pg_sel = 1.0 / jnp.sum(jnp.where(is_group, jnp.exp(jnp.where(is_group, lt, mg) - mg), 0.0),
                           axis=0, keepdims=True)
    row0 = EXPERT_ROW0 + EXPERTS_PER_GROUP * g
    in_group = (row >= row0) & (row < row0 + EXPERTS_PER_GROUP)
    m1, i1 = first_max(in_group)
    m2, i2 = first_max(in_group & (row != i1))
    e2 = jnp.exp(m2 - m1)
    w1 = pg_sel / (1.0 + e2)
    w2 = pg_sel * e2 / (1.0 + e2)

    oh1 = row == i1
    oh2 = row == i2
    both = (oh1 | oh2).astype(jnp.bfloat16)
    before = jnp.dot(both, tri, preferred_element_type=jnp.float32) + carry
    r1 = jnp.sum(jnp.where(oh1, before, 0.0), axis=0, keepdims=True)
    r2 = jnp.sum(jnp.where(oh2, before, 0.0), axis=0, keepdims=True)
    new_carry = carry + jnp.sum(both.astype(jnp.float32), axis=1, keepdims=True)

    field = lax.broadcasted_iota(jnp.int32, (INFO_ROWS, tm), 0)
    info = jnp.zeros((INFO_ROWS, tm), jnp.float32)
    for k, v in ((INFO_E1, i1 - EXPERT_ROW0), (INFO_E2, i2 - EXPERT_ROW0), (INFO_R1, r1), (INFO_R2, r2),
                 (INFO_W1, w1), (INFO_W2, w2)):
        info = jnp.where(field == k, v, info)
    return info, new_carry


def _merge_kernel(x_ref, oa_ref, ob_ref, lng_ref, lnb_ref, wg_ref, wpa_ref, wpb_ref, wo_ref,
                  l1g_ref, l1b_ref, wr_ref, br_ref, cnt0_ref,
                  h1_ref, h1p_ref, info_ref, cnt_ref, carry_ref, tri_ref):
    tm = x_ref.shape[0]

    @pl.when(pl.program_id(0) == 0)
    def _():
        carry_ref[...] = cnt0_ref[...]
        r = lax.broadcasted_iota(jnp.int32, (tm, tm), 0)
        c = lax.broadcasted_iota(jnp.int32, (tm, tm), 1)
        tri_ref[...] = (r < c).astype(jnp.bfloat16)

    def project(rows):
        h = _layer_norm(x_ref[rows], lng_ref[...], lnb_ref[...])
        gates = jnp.dot(h.astype(jnp.bfloat16), wg_ref[...], preferred_element_type=jnp.float32)
        pa = jnp.dot(oa_ref[rows], wpa_ref[...], preferred_element_type=jnp.float32)
        pb = jnp.dot(ob_ref[rows], wpb_ref[...], preferred_element_type=jnp.float32)
        return h, gates, pa, pb

    def mix(h, gates, pa, pb):
        mixin = jax.nn.sigmoid(gates[:, :D_MODEL]) * pa + jax.nn.sigmoid(gates[:, D_MODEL:]) * pb
        return DEEPNORM_ALPHA * h + jnp.dot(mixin.astype(jnp.bfloat16), wo_ref[...],
                                            preferred_element_type=jnp.float32)

    def norm_and_logits(pre, rows):
        h1 = _layer_norm(pre, l1g_ref[...], l1b_ref[...])
        h1_ref[rows] = h1
        h1p_ref[rows] = _pack_rows(h1)
        return lax.dot_general(wr_ref[...], h1.astype(jnp.bfloat16), (((1,), (1,)), ((), ())),
                               preferred_element_type=jnp.float32)

    sub = tm // MERGE_SUBTILES
    parts = [slice(k * sub, (k + 1) * sub) for k in range(MERGE_SUBTILES)]
    projected = [project(rows) for rows in parts]
    mixed = [mix(*pr) for pr in projected]
    logits_t = jnp.concatenate([norm_and_logits(pre, rows) for pre, rows in zip(mixed, parts)], axis=1)
    logits_t = logits_t + br_ref[...]
    info, carry = _route(logits_t, carry_ref[...], tri_ref[...])
    info_ref[...] = info
    carry_ref[...] = carry
    cnt_ref[...] = carry[:, :LANES]


def _merge(x2, oa, ob, ln_g, ln_b, w_gates, w_pa, w_pb, w_o, l1g, l1b, w_r, b_r, cnt0, tm):
    n = x2.shape[0]

    def const(shape):
        return pl.BlockSpec(shape, lambda i: (0,) * len(shape))

    def rows(width):
        return pl.BlockSpec((tm, width), lambda i: (i, 0))

    return pl.pallas_call(
        _merge_kernel,
        out_shape=(jax.ShapeDtypeStruct((n, D_MODEL), jnp.float32),
                   jax.ShapeDtypeStruct((n, PACKED_WIDTH), jnp.uint32),
                   jax.ShapeDtypeStruct((INFO_ROWS, n), jnp.float32),
                   jax.ShapeDtypeStruct((ROUTE_ROWS, LANES), jnp.float32)),
        grid=(n // tm,),
        in_specs=[
            rows(D_MODEL), rows(WIDTH_A), rows(WIDTH_B),
            const((1, D_MODEL)), const((1, D_MODEL)),
            const((D_MODEL, 2 * D_MODEL)),
            const((WIDTH_A, D_MODEL)), const((WIDTH_B, D_MODEL)),
            const((D_MODEL, D_MODEL)),
            const((1, D_MODEL)), const((1, D_MODEL)),
            const((ROUTE_ROWS, D_MODEL)), const((ROUTE_ROWS, tm)), const((ROUTE_ROWS, tm)),
        ],
        out_specs=(rows(D_MODEL), rows(PACKED_WIDTH), pl.BlockSpec((INFO_ROWS, tm), lambda i: (0, i)),
                   const((ROUTE_ROWS, LANES))),
        scratch_shapes=[pltpu.VMEM((ROUTE_ROWS, tm), jnp.float32), pltpu.VMEM((tm, tm), jnp.bfloat16)],
        compiler_params=_cparams("arbitrary"),
        name="merge",
    )(x2, oa, ob, ln_g, ln_b, w_gates, w_pa, w_pb, w_o, l1g, l1b, w_r, b_r, cnt0)


SC_CORES = 2
SC_SUBCORES = 16
SC_WORKERS = SC_CORES * SC_SUBCORES
SC_ROWS_PER_STREAM = 64


def _sc_worker():
    return lax.axis_index("s") * SC_CORES + lax.axis_index("c")


def _row_move_cost(rows_moved, width, dtype):
    return pl.CostEstimate(flops=0, transcendentals=0, bytes_accessed=rows_moved * width * jnp.dtype(dtype).itemsize)


def _dispatch(h1p, dest_t, pad_rows, nrows_out):
    n, width = h1p.shape
    per_worker = n // SC_WORKERS
    nchunks = per_worker // SC_ROWS_PER_STREAM
    assert nchunks * SC_ROWS_PER_STREAM * SC_WORKERS == n and nchunks % 2 == 0
    idx = dest_t.reshape(TOP_K, SC_WORKERS, nchunks, SC_ROWS_PER_STREAM)
    npad = pad_rows.size // (SC_WORKERS * SC_ROWS_PER_STREAM)
    assert npad * SC_WORKERS * SC_ROWS_PER_STREAM == pad_rows.size
    pad_idx = pad_rows.reshape(SC_WORKERS, npad, SC_ROWS_PER_STREAM)
    zeros = jnp.zeros((SC_ROWS_PER_STREAM, width), h1p.dtype)
    mesh = plsc.VectorSubcoreMesh(core_axis_name="c", subcore_axis_name="s")

    @functools.partial(
        pl.kernel, out_type=jax.ShapeDtypeStruct((nrows_out, width), h1p.dtype), mesh=mesh,
        scratch_types=[pltpu.VMEM((TOP_K, nchunks, SC_ROWS_PER_STREAM), jnp.int32),
                       pltpu.VMEM((npad, SC_ROWS_PER_STREAM), jnp.int32),
                       pltpu.VMEM((2, SC_ROWS_PER_STREAM, width), h1p.dtype),
                       pltpu.SemaphoreType.DMA((2,)), pltpu.SemaphoreType.DMA((2,))],
        cost_estimate=_row_move_cost(n + nrows_out, width, h1p.dtype),
        name="sc_dispatch")
    def scatter_kernel(src_hbm, idx_hbm, pad_hbm, zeros_hbm, out_hbm, idx_v, pad_v, rows_v, rsem, ssem):
        wid = _sc_worker()
        base = wid * per_worker
        for k in range(TOP_K):
            pltpu.sync_copy(idx_hbm.at[k, wid], idx_v.at[k])
        pltpu.sync_copy(pad_hbm.at[wid], pad_v)

        def read(j, slot):
            src = src_hbm.at[pl.ds(base + j * SC_ROWS_PER_STREAM, SC_ROWS_PER_STREAM)]
            return pltpu.make_async_copy(src, rows_v.at[slot], rsem.at[slot])

        def scatter(j, slot, k):
            return pltpu.make_async_copy(rows_v.at[slot], out_hbm.at[idx_v.at[k, j]], ssem.at[slot])

        read(0, 0).start()

        @pl.loop(0, nchunks, step=2)
        def _(j0):
            for slot in range(2):
                j = j0 + slot
                read(j, slot).wait()

                @pl.when(j + 1 < nchunks)
                def _():
                    @pl.when(j >= 1)
                    def _():
                        for k in range(TOP_K):
                            scatter(j - 1, 1 - slot, k).wait()
                    read(j + 1, 1 - slot).start()

                for k in range(TOP_K):
                    scatter(j, slot, k).start()

        for k in range(TOP_K):
            scatter(nchunks - 2, 0, k).wait()
            scatter(nchunks - 1, 1, k).wait()

        pltpu.sync_copy(zeros_hbm, rows_v.at[0])
        fills = [pltpu.make_async_copy(rows_v.at[0], out_hbm.at[pad_v.at[c]], ssem.at[0]) for c in range(npad)]
        for f in fills:
            f.start()
        for f in fills:
            f.wait()

    return scatter_kernel(h1p, idx, pad_idx, zeros)


def _sorted_layout(counts, eid, rank, bm):
    n = eid.shape[1]
    nblocks = TOP_K * n // bm + N_EXPERTS
    expert = jnp.arange(N_EXPERTS, dtype=jnp.int32)
    before = expert[None, :] < expert[:, None]
    blocks = (counts + bm - 1) // bm
    first_blk = jnp.sum(jnp.where(before, blocks[None, :], 0), axis=1)
    starts = first_blk * bm
    dest = rank + jnp.sum(jnp.where(eid[None] == expert[:, None, None], starts[:, None, None], 0), axis=0)
    total = jnp.sum(blocks)
    j = jnp.arange(bm, dtype=jnp.int32)[None, :]
    npad = blocks * bm - counts
    spare_before = jnp.sum(jnp.where(before, (bm - npad)[None, :], 0), axis=1)
    pad_rows = jnp.where(j < npad[:, None], (starts + counts)[:, None] + j,
                         (total * bm + spare_before - npad)[:, None] + j).astype(jnp.int32)
    w = jnp.arange(nblocks, dtype=jnp.int32)
    src = jnp.minimum(w, total - 1)
    blk_end = first_blk + blocks
    owner = jnp.minimum(jnp.sum((blk_end[None, :] <= src[:, None]).astype(jnp.int32), axis=1), N_EXPERTS - 1)
    valid = (w < total).astype(jnp.int32)
    return dest.astype(jnp.int32), pad_rows, (src.astype(jnp.int32), owner.astype(jnp.int32), valid)


def _expert_kernel(src_ref, e_ref, valid_ref, x_ref, wg_ref, wu_ref, wd_ref, o_ref):
    w = pl.program_id(0)

    @pl.when(valid_ref[w] != 0)
    def _():
        x = _unpack_rows(x_ref[...]).astype(jnp.bfloat16)
        g = jnp.dot(x, wg_ref[0], preferred_element_type=jnp.float32)
        u = jnp.dot(x, wu_ref[0], preferred_element_type=jnp.float32)
        hmid = (jax.nn.silu(g) * u).astype(jnp.bfloat16)
        o_ref[...] = _pack_rows(jnp.dot(hmid, wd_ref[0], preferred_element_type=jnp.float32))

    @pl.when(valid_ref[w] == 0)
    def _():
        o_ref[...] = jnp.zeros(o_ref.shape, o_ref.dtype)


def _experts(xs, plan, w_gate, w_up, w_down, bm):
    nblocks = plan[0].shape[0]
    assert xs.shape[0] == nblocks * bm

    def weights(shape):
        return pl.BlockSpec((1,) + shape, lambda w, src, e, *_: (e[w], 0, 0))

    grid_spec = pltpu.PrefetchScalarGridSpec(
        num_scalar_prefetch=len(plan),
        grid=(nblocks,),
        in_specs=[
            pl.BlockSpec((bm, PACKED_WIDTH), lambda w, src, *_: (src[w], 0)),
            weights((D_MODEL, D_EXPERT)), weights((D_MODEL, D_EXPERT)), weights((D_EXPERT, D_MODEL)),
        ],
        out_specs=pl.BlockSpec((bm, PACKED_WIDTH), lambda w, *_: (w, 0)),
    )
    return pl.pallas_call(
        _expert_kernel,
        out_shape=jax.ShapeDtypeStruct(xs.shape, jnp.uint32),
        grid_spec=grid_spec,
        compiler_params=_cparams("parallel"),
        name="experts",
    )(*plan, xs, w_gate, w_up, w_down)


def _sc_gather_rows(table, idx):
    nrows = idx.shape[0]
    width = table.shape[1]
    per_worker = nrows // SC_WORKERS
    nchunks = per_worker // SC_ROWS_PER_STREAM
    assert nchunks * SC_ROWS_PER_STREAM * SC_WORKERS == nrows and nchunks % 2 == 0
    mesh = plsc.VectorSubcoreMesh(core_axis_name="c", subcore_axis_name="s")

    @functools.partial(
        pl.kernel, out_type=jax.ShapeDtypeStruct((nrows, width), table.dtype), mesh=mesh,
        scratch_types=[pltpu.VMEM((per_worker,), jnp.int32),
                       pltpu.VMEM((2, SC_ROWS_PER_STREAM, width), table.dtype),
                       pltpu.SemaphoreType.DMA((2,)), pltpu.SemaphoreType.DMA((2,))],
        cost_estimate=_row_move_cost(2 * nrows, width, table.dtype),
        name="sc_gather")
    def gather_kernel(table_hbm, idx_hbm, out_hbm, idx_v, rows_v, gsem, wsem):
        base = _sc_worker() * per_worker
        pltpu.sync_copy(idx_hbm.at[pl.ds(base, per_worker)], idx_v)

        def gather(j, slot):
            rows = idx_v.at[pl.ds(j * SC_ROWS_PER_STREAM, SC_ROWS_PER_STREAM)]
            return pltpu.make_async_copy(table_hbm.at[rows], rows_v.at[slot], gsem.at[slot])

        def write(j, slot):
            dst = out_hbm.at[pl.ds(base + j * SC_ROWS_PER_STREAM, SC_ROWS_PER_STREAM)]
            return pltpu.make_async_copy(rows_v.at[slot], dst, wsem.at[slot])

        gather(0, 0).start()

        @pl.loop(0, nchunks, step=2)
        def _(j0):
            for slot in range(2):
                j = j0 + slot
                gather(j, slot).wait()

                @pl.when(j + 1 < nchunks)
                def _():
                    @pl.when(j >= 1)
                    def _():
                        write(j - 1, 1 - slot).wait()
                    gather(j + 1, 1 - slot).start()

                write(j, slot).start()

        write(nchunks - 2, 0).wait()
        write(nchunks - 1, 1).wait()

    return gather_kernel(table, idx)


def _finalize_kernel(h1_ref, y1_ref, y2_ref, info_ref, g_ref, b_ref, o_ref):
    tt = h1_ref.shape[0]
    pad = jnp.zeros((LANES - INFO_ROWS, tt), jnp.float32)
    info = jnp.concatenate([info_ref[...], pad], axis=0).T
    moe = (_unpack_rows(y1_ref[...]) * info[:, INFO_W1:INFO_W1 + 1]
           + _unpack_rows(y2_ref[...]) * info[:, INFO_W2:INFO_W2 + 1])
    o_ref[...] = _layer_norm(DEEPNORM_ALPHA * h1_ref[...] + moe, g_ref[...], b_ref[...])


def _combine(h1, info, dest_t, ys, ln_g, ln_b, tt):
    n = h1.shape[0]
    nsteps = n // tt
    yg = _sc_gather_rows(ys, dest_t.reshape(TOP_K * n))
    return pl.pallas_call(
        _finalize_kernel,
        out_shape=jax.ShapeDtypeStruct((n, D_MODEL), jnp.float32),
        grid=(nsteps,),
        in_specs=[
            pl.BlockSpec((tt, D_MODEL), lambda i: (i, 0)),
            pl.BlockSpec((tt, PACKED_WIDTH), lambda i: (i, 0)),
            pl.BlockSpec((tt, PACKED_WIDTH), lambda i: (nsteps + i, 0)),
            pl.BlockSpec((INFO_ROWS, tt), lambda i: (0, i)),
            pl.BlockSpec((1, D_MODEL), lambda i: (0, 0)),
            pl.BlockSpec((1, D_MODEL), lambda i: (0, 0)),
        ],
        out_specs=pl.BlockSpec((tt, D_MODEL), lambda i: (i, 0)),
        compiler_params=_cparams("parallel"),
        name="finalize",
    )(h1, yg, yg, info, ln_g, ln_b)


TM_QKV = 1024
TQ_WIN = 1024
TM_MERGE = 1024
TT_ROWS = 1024
BM_EXPERT = 512


def _prepare_weights(ln_in_g, ln_in_b, w_in, attn_sink, rel_pos_bias, w_proj_a, w_proj_b, w_out,
                     ln1_g, ln1_b, w_route_group, b_route_group, w_route_expert, b_route_expert,
                     ln2_g, ln2_b):
    bf = jnp.bfloat16
    w = w_in[0]
    splits = np.cumsum([WIDTH_A, KV_WIDTH_A, KV_WIDTH_A, WIDTH_B, WIDTH_B, WIDTH_B, D_MODEL])
    wqa, wka, wva, wqb, wkb, wvb, wga, wgb = jnp.split(w, [int(s) for s in splits], axis=1)
    wqa = (wqa.reshape(D_MODEL, N_KV_HEADS_A, GQA_GROUP, HEAD_DIM).transpose(0, 2, 1, 3)
           .reshape(D_MODEL, WIDTH_A))
    w_qkv = jnp.concatenate([wqa, wqb, wkb, wvb, wka, wva], axis=1).astype(bf)
    w_gates = jnp.concatenate([wga, wgb], axis=1).astype(bf)
    w_pa = (w_proj_a[0].reshape(N_KV_HEADS_A, GQA_GROUP, HEAD_DIM, D_MODEL).transpose(1, 0, 2, 3)
            .reshape(WIDTH_A, D_MODEL).astype(bf))
    w_pb = w_proj_b[0].astype(bf)
    w_o = w_out[0].astype(bf)
    pad = ROUTE_ROWS - N_GROUPS - N_EXPERTS
    w_r = jnp.concatenate([w_route_group[0].T, w_route_expert[0].T, jnp.zeros((pad, D_MODEL), jnp.float32)], axis=0)
    w_route = w_r.astype(bf)
    b_r = jnp.concatenate([b_route_group[0], b_route_expert[0], jnp.zeros((pad,), jnp.float32)])
    b_r = jnp.broadcast_to(b_r[:, None], (ROUTE_ROWS, TM_MERGE))
    row = lambda v: v.reshape(1, D_MODEL)
    return dict(
        ln_in_g=row(ln_in_g), ln_in_b=row(ln_in_b), w_qkv=w_qkv, w_gates=w_gates,
        sink=attn_sink[0].astype(jnp.float32), nat_bias=_nat_bias_table(rel_pos_bias[0]),
        w_pa=w_pa, w_pb=w_pb, w_o=w_o, ln1_g=row(ln1_g[0]), ln1_b=row(ln1_b[0]),
        w_route=w_route, b_r=b_r, ln2_g=row(ln2_g[0]), ln2_b=row(ln2_b[0]))


def _after(value, other):
    if other is None:
        return value
    other = other.astype(jnp.float32)
    zero = jnp.where(jnp.isfinite(other), other, 0.0) * 0.0
    return value + zero.astype(value.dtype)


def _attend_and_route(x, p, after=None, cast=()):
    bsz, t, _ = x.shape
    n = bsz * t
    x2 = x.reshape(n, D_MODEL)
    qkv, cast_bf16 = _qkv(x2, _after(p["ln_in_g"], after), p["ln_in_b"], p["w_qkv"], TM_QKV, cast)
    oa = _win_attention(qkv, p["sink"], bsz, t, TQ_WIN)
    ob = _nat_attention(qkv, p["nat_bias"], bsz, t)
    cnt0 = jnp.zeros((ROUTE_ROWS, TM_MERGE), jnp.float32)
    h1, h1p, info, cnt = _merge(x2, oa, ob, p["ln_in_g"], p["ln_in_b"], p["w_gates"], p["w_pa"], p["w_pb"],
                                p["w_o"], p["ln1_g"], p["ln1_b"], p["w_route"], p["b_r"], cnt0, TM_MERGE)
    counts = cnt[EXPERT_ROW0:EXPERT_ROW0 + N_EXPERTS, 0].astype(jnp.int32)
    eid = info[INFO_E1:INFO_E2 + 1].astype(jnp.int32)
    rank = info[INFO_R1:INFO_R2 + 1].astype(jnp.int32)
    dest_t, pad_rows, plan = _sorted_layout(counts, eid, rank, BM_EXPERT)
    return dict(shape=x.shape, h1=h1, h1p=h1p, info=info, counts=counts, dest_t=dest_t, pad_rows=pad_rows,
                plan=plan, cast=cast_bf16)


def _run_experts(r, expert_weights, after=None):
    src, owner, valid = r["plan"]
    xs = _dispatch(r["h1p"], r["dest_t"], r["pad_rows"], src.shape[0] * BM_EXPERT)
    return _experts(xs, (src, owner, _after(valid, after)), *expert_weights, BM_EXPERT)


def _finish(r, ys, p, after=None):
    out = _combine(r["h1"], r["info"], r["dest_t"], ys, _after(p["ln2_g"], after), p["ln2_b"], TT_ROWS)
    return out.reshape(r["shape"])


def kernel(x_prompt, x_sample, ln_in_g, ln_in_b, w_in, attn_sink, rel_pos_bias, w_proj_a, w_proj_b, w_out,
           ln1_g, ln1_b, w_route_group, b_route_group, w_route_expert, b_route_expert,
           w_gate, w_up, w_down, ln2_g, ln2_b):
    p = _prepare_weights(ln_in_g, ln_in_b, w_in, attn_sink, rel_pos_bias, w_proj_a, w_proj_b, w_out,
                         ln1_g, ln1_b, w_route_group, b_route_group, w_route_expert, b_route_expert,
                         ln2_g, ln2_b)
    rp = _attend_and_route(x_prompt, p, cast=(w_gate[0], w_up[0], w_down[0]))
    glue_done = rp["dest_t"][0, 0] + rp["pad_rows"][0, 0] + sum(a[0] for a in rp["plan"])
    rs = _attend_and_route(x_sample, p, after=glue_done)
    ys_p = _run_experts(rp, rp["cast"])
    ys_s = _run_experts(rs, rp["cast"], after=ys_p[0, 0])
    y_prompt = _finish(rp, ys_p, p)
    y_sample = _finish(rs, ys_s, p, after=y_prompt[0, 0, 0])
    return (y_prompt, y_sample)
```

```python
import functools

import numpy as np
import jax
import jax.numpy as jnp
from jax import lax
from jax.experimental import pallas as pl
from jax.experimental.pallas import tpu as pltpu
from jax.experimental.pallas import tpu_sc as plsc

D_MODEL = 1024
HEAD_DIM = 64
N_HEADS_A = 8
N_KV_HEADS_A = 2
WINDOW = 128
N_HEADS_B = 8
GRID_W = 64
NA_ROWS = 8
NA_COLS = 16
N_GROUPS = 4
EXPERTS_PER_GROUP = 8
N_EXPERTS = N_GROUPS * EXPERTS_PER_GROUP
TOP_K = 2
D_EXPERT = D_MODEL // 2
LN_EPS = 1e-5
DEPTH = 1
DEEPNORM_ALPHA = (2.0 * DEPTH) ** 0.25
WIDTH_A = N_HEADS_A * HEAD_DIM
KV_WIDTH_A = N_KV_HEADS_A * HEAD_DIM
WIDTH_B = N_HEADS_B * HEAD_DIM
QKV_WIDTH = WIDTH_A + 2 * KV_WIDTH_A + 3 * WIDTH_B

LANES = 128
VMEM_LIMIT_BYTES = 56 * 1024 * 1024

NEG_BIG = -1e30
LOG2E = float(np.log2(np.e))

QA_COL, QB_COL, KB_COL, VB_COL = 0, WIDTH_A, WIDTH_A + WIDTH_B, WIDTH_A + 2 * WIDTH_B
KA_COL = WIDTH_A + 3 * WIDTH_B
VA_COL = KA_COL + KV_WIDTH_A

GQA_GROUP = N_HEADS_A // N_KV_HEADS_A


def _cparams(*sem):
    return pltpu.CompilerParams(dimension_semantics=sem, vmem_limit_bytes=VMEM_LIMIT_BYTES)


def _cost(flops, bytes_accessed, transcendentals=0):
    return pl.CostEstimate(flops=int(flops), transcendentals=int(transcendentals), bytes_accessed=int(bytes_accessed))


def _layer_norm(x, g, b):
    mu = jnp.mean(x, axis=-1, keepdims=True)
    xc = x - mu
    var = jnp.mean(xc * xc, axis=-1, keepdims=True)
    return xc * lax.rsqrt(var + LN_EPS) * g + b


PACKED_WIDTH = D_MODEL // 2


def _pack_rows(x):
    def rne(v):
        return v + jnp.uint32(0x7FFF) + ((v >> 16) & jnp.uint32(1))
    hi = lax.bitcast_convert_type(x[:, :PACKED_WIDTH], jnp.uint32)
    lo = lax.bitcast_convert_type(x[:, PACKED_WIDTH:], jnp.uint32)
    return (rne(hi) & jnp.uint32(0xFFFF0000)) | (rne(lo) >> 16)


def _unpack_rows(w):
    hi = lax.bitcast_convert_type(w & jnp.uint32(0xFFFF0000), jnp.float32)
    lo = lax.bitcast_convert_type(w << 16, jnp.float32)
    return jnp.concatenate([hi, lo], axis=1)


def _qkv_kernel(x_ref, g_ref, b_ref, w_ref, *rest):
    o_ref = rest[len(rest) // 2]
    h = _layer_norm(x_ref[...], g_ref[...], b_ref[...])
    y = jnp.dot(h.astype(jnp.bfloat16), w_ref[...], preferred_element_type=jnp.float32)
    col = lax.broadcasted_iota(jnp.int32, (1, QKV_WIDTH), 1)
    y = y * jnp.where(col < KB_COL, HEAD_DIM ** -0.5 * LOG2E, 1.0)
    o_ref[...] = y.astype(jnp.bfloat16)
    ncast = len(rest) // 2
    for src_ref, dst_ref in zip(rest[:ncast], rest[ncast + 1:]):
        dst_ref[...] = src_ref[...].astype(jnp.bfloat16)


def _qkv(x2, ln_g, ln_b, w_qkv, tm, cast=()):
    n = x2.shape[0]
    steps = n // tm
    per_step = -(-N_EXPERTS // steps)
    assert all(c.shape[0] == N_EXPERTS for c in cast) and (steps * per_step) % N_EXPERTS == 0
    revisit = steps * per_step // N_EXPERTS

    def expert_block(c):
        return pl.BlockSpec((per_step,) + c.shape[1:], lambda i: (i // revisit, 0, 0))

    outs = pl.pallas_call(
        _qkv_kernel,
        out_shape=[jax.ShapeDtypeStruct((n, QKV_WIDTH), jnp.bfloat16)]
        + [jax.ShapeDtypeStruct(c.shape, jnp.bfloat16) for c in cast],
        grid=(steps,),
        in_specs=[
            pl.BlockSpec((tm, D_MODEL), lambda i: (i, 0)),
            pl.BlockSpec((1, D_MODEL), lambda i: (0, 0)),
            pl.BlockSpec((1, D_MODEL), lambda i: (0, 0)),
            pl.BlockSpec((D_MODEL, QKV_WIDTH), lambda i: (0, 0)),
        ] + [expert_block(c) for c in cast],
        out_specs=[pl.BlockSpec((tm, QKV_WIDTH), lambda i: (i, 0))] + [expert_block(c) for c in cast],
        compiler_params=_cparams("arbitrary" if cast else "parallel"),
        cost_estimate=_cost(2 * n * D_MODEL * QKV_WIDTH, n * (4 * D_MODEL + 2 * QKV_WIDTH)),
        name="qkv",
    )(x2, ln_g, ln_b, w_qkv, *cast)
    return outs[0], tuple(outs[1:])


WIN_BLK = 128
WIN_LOOKAHEAD = 2


def _win_bias_table():
    qi = np.arange(WIN_BLK)[:, None]
    kj = np.arange(3 * WIN_BLK)[None, :]
    dist = np.abs(kj - WIN_BLK - qi).astype(np.float64)
    slopes = 2.0 ** (-8.0 * np.arange(1, N_HEADS_A + 1) / N_HEADS_A)
    per_head = np.where(dist <= WINDOW, -slopes[:, None, None] * dist[None] * LOG2E, NEG_BIG)
    groups = [np.concatenate([per_head[j], per_head[j + 4]], axis=0) for j in range(4)]
    return np.stack(groups).astype(np.float32)


def _win_kernel(sink_ref, q_ref, kp_ref, km_ref, kn_ref, vp_ref, vm_ref, vn_ref, bias_ref, o_ref,
                *, nsub, nblk_seq):
    i = pl.program_id(1)
    kcat = jnp.concatenate([kp_ref[...], km_ref[...], kn_ref[...]], axis=0)
    vcat = jnp.concatenate([vp_ref[...], vm_ref[...], vn_ref[...]], axis=0)
    lo = lax.broadcasted_iota(jnp.int32, (1, LANES), 1) < HEAD_DIM
    col = lax.broadcasted_iota(jnp.int32, (1, 3 * WIN_BLK), 1)
    top = lax.broadcasted_iota(jnp.int32, (2 * WIN_BLK, 1), 0) < WIN_BLK
    zero = jnp.zeros((), jnp.bfloat16)

    def scores(j, g):
        n = i * nsub + j
        off_seq = ((col < WIN_BLK) & (n == 0)) | ((col >= 2 * WIN_BLK) & (n == nblk_seq - 1))
        edge = jnp.where(off_seq, NEG_BIG, 0.0)
        qg = q_ref[WIN_BLK * j:WIN_BLK * (j + 1), LANES * g:LANES * (g + 1)]
        qm = jnp.concatenate([jnp.where(lo, qg, zero), jnp.where(lo, zero, qg)], axis=0)
        kj = kcat[WIN_BLK * j:WIN_BLK * (j + 3)]
        s = lax.dot_general(qm, kj, (((1,), (1,)), ((), ())), preferred_element_type=jnp.float32)
        return s + bias_ref[g] + edge

    def attend(s, j, g):
        vj = vcat[WIN_BLK * j:WIN_BLK * (j + 3)]
        sink = jnp.where(top, sink_ref[g], sink_ref[g + 4]) * LOG2E
        m = jnp.maximum(jnp.max(s, axis=-1, keepdims=True), sink)
        p = jnp.exp2(s - m)
        l = jnp.sum(p, axis=-1, keepdims=True) + jnp.exp2(sink - m)
        o2 = jnp.dot(p.astype(jnp.bfloat16), vj, preferred_element_type=jnp.float32)
        o2 = o2 * (1.0 / l)
        o_ref[WIN_BLK * j:WIN_BLK * (j + 1), LANES * g:LANES * (g + 1)] = (
            jnp.where(lo, o2[:WIN_BLK], o2[WIN_BLK:]).astype(jnp.bfloat16))

    chains = [(j, g) for j in range(nsub) for g in range(4)]
    pending = [scores(*c) for c in chains[:WIN_LOOKAHEAD]]
    for idx, c in enumerate(chains):
        s = pending.pop(0)
        if idx + WIN_LOOKAHEAD < len(chains):
            pending.append(scores(*chains[idx + WIN_LOOKAHEAD]))
        attend(s, *c)


def _win_attention(qkv, sink, bsz, t, tq):
    n = bsz * t
    nsub = tq // WIN_BLK
    nblk_seq = t // WIN_BLK
    ntile = t // tq
    bias = jnp.asarray(_win_bias_table())

    def main_map(col):
        return lambda b, i, *_: (b * ntile + i, col)

    def prev_map(col):
        return lambda b, i, *_: (b * nblk_seq + jnp.maximum(i * nsub - 1, 0), col)

    def next_map(col):
        return lambda b, i, *_: (b * nblk_seq + jnp.minimum(i * nsub + nsub, nblk_seq - 1), col)

    halo = (WIN_BLK, LANES)
    ka, va = KA_COL // LANES, VA_COL // LANES
    grid_spec = pltpu.PrefetchScalarGridSpec(
        num_scalar_prefetch=1,
        grid=(bsz, ntile),
        in_specs=[
            pl.BlockSpec((tq, WIDTH_A), main_map(QA_COL // WIDTH_A)),
            pl.BlockSpec(halo, prev_map(ka)),
            pl.BlockSpec((tq, LANES), main_map(ka)),
            pl.BlockSpec(halo, next_map(ka)),
            pl.BlockSpec(halo, prev_map(va)),
            pl.BlockSpec((tq, LANES), main_map(va)),
            pl.BlockSpec(halo, next_map(va)),
            pl.BlockSpec((4, 2 * WIN_BLK, 3 * WIN_BLK), lambda b, i, *_: (0, 0, 0)),
        ],
        out_specs=pl.BlockSpec((tq, WIDTH_A), main_map(0)),
    )
    return pl.pallas_call(
        functools.partial(_win_kernel, nsub=nsub, nblk_seq=nblk_seq),
        out_shape=jax.ShapeDtypeStruct((n, WIDTH_A), jnp.bfloat16),
        grid_spec=grid_spec,
        compiler_params=_cparams("parallel", "parallel"),
        cost_estimate=_cost(n * N_HEADS_A * 3 * WIN_BLK * 4 * LANES, n * 2 * (2 * WIDTH_A + 6 * KV_WIDTH_A),
                            n * N_HEADS_A * 3 * WIN_BLK),
        name="win",
    )(sink, qkv, qkv, qkv, qkv, qkv, qkv, qkv, bias)


NAT_ROWS_PER_STEP = 16
NAT_HALO_ROWS = NA_ROWS // 2
NAT_KEYS = NA_ROWS * GRID_W
NAT_ROWS_PER_TRIP = 8
NAT_HEADS_PER_CHAIN = 2
NAT_LOOKAHEAD = 4


def _nat_bias_table(rpb):
    c = np.arange(GRID_W)
    cs = np.clip(c - NA_COLS // 2, 0, GRID_W - NA_COLS)
    col_mask = (c[None, :] >= cs[:, None]) & (c[None, :] < cs[:, None] + NA_COLS)
    dc = np.clip(c[None, :] - c[:, None] + (NA_COLS - 1), 0, 2 * NA_COLS - 2)
    onehot = jnp.asarray(dc[None] == np.arange(2 * NA_COLS - 1)[:, None, None], jnp.float32)
    picked = jnp.einsum("hdj,jqc->hqdc", rpb, onehot, precision=lax.Precision.HIGHEST)
    t1 = jnp.where(col_mask[None, :, None, :], picked * LOG2E, NEG_BIG)
    flat = t1.reshape(N_HEADS_B, GRID_W, (2 * NA_ROWS - 1) * GRID_W)
    shifts = jnp.stack([flat[:, :, sh * GRID_W:sh * GRID_W + NAT_KEYS] for sh in range(NA_ROWS)], axis=1)
    tb = shifts.reshape(N_HEADS_B // 2, 2, NA_ROWS, GRID_W, NAT_KEYS).transpose(0, 2, 1, 3, 4)
    return tb.reshape(N_HEADS_B // 2, NA_ROWS, 2 * GRID_W, NAT_KEYS)


def _nat_kernel(q_ref, kp_ref, km_ref, kn_ref, vp_ref, vm_ref, vn_ref, tb_ref, o_ref, kcat, vcat,
                *, rows_seq):
    i = pl.program_id(1)
    halo = NAT_HALO_ROWS * GRID_W
    main = NAT_ROWS_PER_STEP * GRID_W
    kcat[0:halo] = kp_ref[...]
    kcat[halo:halo + main] = km_ref[...]
    kcat[halo + main:2 * halo + main] = kn_ref[...]
    vcat[0:halo] = vp_ref[...]
    vcat[halo:halo + main] = vm_ref[...]
    vcat[halo + main:2 * halo + main] = vn_ref[...]
    width = NAT_HEADS_PER_CHAIN * HEAD_DIM
    head_of_lane = lax.broadcasted_iota(jnp.int32, (1, width), 1) // HEAD_DIM
    zero = jnp.zeros((), jnp.bfloat16)
    r0 = i * NAT_ROWS_PER_STEP

    def scores(qr, c):
        r = r0 + qr
        rs = jnp.clip(r - NA_ROWS // 2, 0, rows_seq - NA_ROWS)
        koff = pl.multiple_of((rs - r0 + NAT_HALO_ROWS) * GRID_W, GRID_W)
        sh = rs - r + (NA_ROWS - 1)
        qoff = pl.multiple_of(qr * GRID_W, GRID_W)
        cols = slice(width * c, width * (c + 1))
        qc = q_ref[pl.ds(qoff, GRID_W), cols]
        qm = jnp.concatenate([jnp.where(head_of_lane == h, qc, zero) for h in range(NAT_HEADS_PER_CHAIN)], axis=0)
        kw = kcat[pl.ds(koff, NAT_KEYS), cols]
        s = lax.dot_general(qm, kw, (((1,), (1,)), ((), ())), preferred_element_type=jnp.float32)
        pairs = NAT_HEADS_PER_CHAIN // 2
        bias = jnp.concatenate([tb_ref[pairs * c + k, sh] for k in range(pairs)], axis=0)
        return s + bias, koff, qoff

    def attend(s, koff, qoff, c):
        cols = slice(width * c, width * (c + 1))
        vw = vcat[pl.ds(koff, NAT_KEYS), cols]
        m = jnp.max(s, axis=-1, keepdims=True)
        pe = jnp.exp2(s - m)
        l = jnp.sum(pe, axis=-1, keepdims=True)
        o2 = jnp.dot(pe.astype(jnp.bfloat16), vw, preferred_element_type=jnp.float32)
        o2 = o2 * (1.0 / l)
        out = o2[:GRID_W]
        for h in range(1, NAT_HEADS_PER_CHAIN):
            out = jnp.where(head_of_lane == h, o2[GRID_W * h:GRID_W * (h + 1)], out)
        o_ref[pl.ds(qoff, GRID_W), cols] = out.astype(jnp.bfloat16)

    def trip(j, carry):
        chains = [(j * NAT_ROWS_PER_TRIP + q, c) for q in range(NAT_ROWS_PER_TRIP)
                  for c in range(N_HEADS_B // NAT_HEADS_PER_CHAIN)]
        pending = [scores(*c) for c in chains[:NAT_LOOKAHEAD]]
        for idx, (_, p) in enumerate(chains):
            s, koff, qoff = pending.pop(0)
            if idx + NAT_LOOKAHEAD < len(chains):
                pending.append(scores(*chains[idx + NAT_LOOKAHEAD]))
            attend(s, koff, qoff, p)
        return carry

    lax.fori_loop(0, NAT_ROWS_PER_STEP // NAT_ROWS_PER_TRIP, trip, 0)


def _nat_attention(qkv, tb, bsz, t):
    n = bsz * t
    rows_seq = t // GRID_W
    main = NAT_ROWS_PER_STEP * GRID_W
    halo = NAT_HALO_ROWS * GRID_W
    ntile = t // main
    nhalo_seq = t // halo
    per = main // halo

    def main_map(col):
        return lambda b, i: (b * ntile + i, col)

    def prev_map(col):
        return lambda b, i: (b * nhalo_seq + jnp.maximum(i * per - 1, 0), col)

    def next_map(col):
        return lambda b, i: (b * nhalo_seq + jnp.minimum(i * per + per, nhalo_seq - 1), col)

    qb, kb, vb = QB_COL // WIDTH_B, KB_COL // WIDTH_B, VB_COL // WIDTH_B
    return pl.pallas_call(
        functools.partial(_nat_kernel, rows_seq=rows_seq),
        out_shape=jax.ShapeDtypeStruct((n, WIDTH_B), jnp.bfloat16),
        grid=(bsz, ntile),
        in_specs=[
            pl.BlockSpec((main, WIDTH_B), main_map(qb)),
            pl.BlockSpec((halo, WIDTH_B), prev_map(kb)),
            pl.BlockSpec((main, WIDTH_B), main_map(kb)),
            pl.BlockSpec((halo, WIDTH_B), next_map(kb)),
            pl.BlockSpec((halo, WIDTH_B), prev_map(vb)),
            pl.BlockSpec((main, WIDTH_B), main_map(vb)),
            pl.BlockSpec((halo, WIDTH_B), next_map(vb)),
            pl.BlockSpec((N_HEADS_B // 2, NA_ROWS, 2 * GRID_W, NAT_KEYS), lambda b, i: (0, 0, 0, 0)),
        ],
        out_specs=pl.BlockSpec((main, WIDTH_B), main_map(0)),
        scratch_shapes=[pltpu.VMEM((main + 2 * halo, WIDTH_B), jnp.bfloat16),
                        pltpu.VMEM((main + 2 * halo, WIDTH_B), jnp.bfloat16)],
        compiler_params=_cparams("parallel", "parallel"),
        cost_estimate=_cost(n * N_HEADS_B * NAT_KEYS * 4 * LANES, n * 2 * 6 * WIDTH_B, n * N_HEADS_B * NAT_KEYS),
        name="nat",
    )(qkv, qkv, qkv, qkv, qkv, qkv, qkv, tb)


EXPERT_ROW0 = N_GROUPS
ROUTE_ROWS = 48
INFO_E1, INFO_E2, INFO_R1, INFO_R2, INFO_W1, INFO_W2 = range(6)
INFO_ROWS = 8
MERGE_SUBTILES = 4


def _route(lt, carry, tri):
    rr, tm = lt.shape
    row = lax.broadcasted_iota(jnp.int32, (rr, tm), 0).astype(jnp.float32)
    none = jnp.float32(rr)

    def first_max(sel):
        m = jnp.max(jnp.where(sel, lt, NEG_BIG), axis=0, keepdims=True)
        idx = jnp.min(jnp.where(sel & (lt == m), row, none), axis=0, keepdims=True)
        return m, idx

    is_group = row < N_GROUPS
    mg, g = first_max(is_group)
    pg_sel = 1.0 / jnp.sum(jnp.where(is_group, jnp.exp(jnp.where(is_group, lt, mg) - mg), 0.0),
                           axis=0, keepdims=True)
    row0 = EXPERT_ROW0 + EXPERTS_PER_GROUP * g
    in_group = (row >= row0) & (row < row0 + EXPERTS_PER_GROUP)
    m1, i1 = first_max(in_group)
    m2, i2 = first_max(in_group & (row != i1))
    e2 = jnp.exp(m2 - m1)
    w1 = pg_sel / (1.0 + e2)
    w2 = pg_sel * e2 / (1.0 + e2)

    oh1 = row == i1
    oh2 = row == i2
    both = (oh1 | oh2).astype(jnp.bfloat16)
    before = jnp.dot(both, tri, preferred_element_type=jnp.float32) + carry
    r1 = jnp.sum(jnp.where(oh1, before, 0.0), axis=0, keepdims=True)
    r2 = jnp.sum(jnp.where(oh2, before, 0.0), axis=0, keepdims=True)
    new_carry = carry + jnp.sum(both.astype(jnp.float32), axis=1, keepdims=True)

    field = lax.broadcasted_iota(jnp.int32, (INFO_ROWS, tm), 0)
    info = jnp.zeros((INFO_ROWS, tm), jnp.float32)
    for k, v in ((INFO_E1, i1 - EXPERT_ROW0), (INFO_E2, i2 - EXPERT_ROW0), (INFO_R1, r1), (INFO_R2, r2),
                 (INFO_W1, w1), (INFO_W2, w2)):
        info = jnp.where(field == k, v, info)
    return info, new_carry


def _merge_kernel(x_ref, oa_ref, ob_ref, lng_ref, lnb_ref, wg_ref, wpa_ref, wpb_ref, wo_ref,
                  l1g_ref, l1b_ref, wr_ref, br_ref, cnt0_ref,
                  h1_ref, h1p_ref, info_ref, cnt_ref, carry_ref, tri_ref):
    tm = x_ref.shape[0]

    @pl.when(pl.program_id(0) == 0)
    def _():
        carry_ref[...] = cnt0_ref[...]
        r = lax.broadcasted_iota(jnp.int32, (tm, tm), 0)
        c = lax.broadcasted_iota(jnp.int32, (tm, tm), 1)
        tri_ref[...] = (r < c).astype(jnp.bfloat16)

    def project(rows):
        h = _layer_norm(x_ref[rows], lng_ref[...], lnb_ref[...])
        gates = jnp.dot(h.astype(jnp.bfloat16), wg_ref[...], preferred_element_type=jnp.float32)
        pa = jnp.dot(oa_ref[rows], wpa_ref[...], preferred_element_type=jnp.float32)
        pb = jnp.dot(ob_ref[rows], wpb_ref[...], preferred_element_type=jnp.float32)
        return h, gates, pa, pb

    def mix(h, gates, pa, pb):
        mixin = jax.nn.sigmoid(gates[:, :D_MODEL]) * pa + jax.nn.sigmoid(gates[:, D_MODEL:]) * pb
        return DEEPNORM_ALPHA * h + jnp.dot(mixin.astype(jnp.bfloat16), wo_ref[...],
                                            preferred_element_type=jnp.float32)

    def norm_and_logits(pre, rows):
        h1 = _layer_norm(pre, l1g_ref[...], l1b_ref[...])
        h1_ref[rows] = h1
        h1p_ref[rows] = _pack_rows(h1)
        return lax.dot_general(wr_ref[...], h1.astype(jnp.bfloat16), (((1,), (1,)), ((), ())),
                               preferred_element_type=jnp.float32)

    sub = tm // MERGE_SUBTILES
    parts = [slice(k * sub, (k + 1) * sub) for k in range(MERGE_SUBTILES)]
    projected = [project(rows) for rows in parts]
    mixed = [mix(*pr) for pr in projected]
    logits_t = jnp.concatenate([norm_and_logits(pre, rows) for pre, rows in zip(mixed, parts)], axis=1)
    logits_t = logits_t + br_ref[...]
    info, carry = _route(logits_t, carry_ref[...], tri_ref[...])
    info_ref[...] = info
    carry_ref[...] = carry
    cnt_ref[...] = carry[:, :LANES]


def _merge(x2, oa, ob, ln_g, ln_b, w_gates, w_pa, w_pb, w_o, l1g, l1b, w_r, b_r, cnt0, tm):
    n = x2.shape[0]

    def const(shape):
        return pl.BlockSpec(shape, lambda i: (0,) * len(shape))

    def rows(width):
        return pl.BlockSpec((tm, width), lambda i: (i, 0))

    return pl.pallas_call(
        _merge_kernel,
        out_shape=(jax.ShapeDtypeStruct((n, D_MODEL), jnp.float32),
                   jax.ShapeDtypeStruct((n, PACKED_WIDTH), jnp.uint32),
                   jax.ShapeDtypeStruct((INFO_ROWS, n), jnp.float32),
                   jax.ShapeDtypeStruct((ROUTE_ROWS, LANES), jnp.float32)),
        grid=(n // tm,),
        in_specs=[
            rows(D_MODEL), rows(WIDTH_A), rows(WIDTH_B),
            const((1, D_MODEL)), const((1, D_MODEL)),
            const((D_MODEL, 2 * D_MODEL)),
            const((WIDTH_A, D_MODEL)), const((WIDTH_B, D_MODEL)),
            const((D_MODEL, D_MODEL)),
            const((1, D_MODEL)), const((1, D_MODEL)),
            const((ROUTE_ROWS, D_MODEL)), const((ROUTE_ROWS, tm)), const((ROUTE_ROWS, tm)),
        ],
        out_specs=(rows(D_MODEL), rows(PACKED_WIDTH), pl.BlockSpec((INFO_ROWS, tm), lambda i: (0, i)),
                   const((ROUTE_ROWS, LANES))),
        scratch_shapes=[pltpu.VMEM((ROUTE_ROWS, tm), jnp.float32), pltpu.VMEM((tm, tm), jnp.bfloat16)],
        compiler_params=_cparams("arbitrary"),
        cost_estimate=_cost(2 * n * D_MODEL * (3 * D_MODEL + WIDTH_A + WIDTH_B + ROUTE_ROWS),
                            n * (4 * D_MODEL + 2 * (WIDTH_A + WIDTH_B) + 4 * D_MODEL + 4 * PACKED_WIDTH),
                            n * 2 * D_MODEL),
        name="merge",
    )(x2, oa, ob, ln_g, ln_b, w_gates, w_pa, w_pb, w_o, l1g, l1b, w_r, b_r, cnt0)


SC_CORES = 2
SC_SUBCORES = 16
SC_WORKERS = SC_CORES * SC_SUBCORES
SC_ROWS_PER_STREAM = 64


def _sc_worker():
    return lax.axis_index("s") * SC_CORES + lax.axis_index("c")


def _row_move_cost(rows_moved, width, dtype):
    return pl.CostEstimate(flops=0, transcendentals=0, bytes_accessed=rows_moved * width * jnp.dtype(dtype).itemsize)


def _dispatch(h1p, dest_t, pad_rows, nrows_out):
    n, width = h1p.shape
    per_worker = n // SC_WORKERS
    nchunks = per_worker // SC_ROWS_PER_STREAM
    assert nchunks * SC_ROWS_PER_STREAM * SC_WORKERS == n and nchunks % 2 == 0
    idx = dest_t.reshape(TOP_K, SC_WORKERS, nchunks, SC_ROWS_PER_STREAM)
    npad = pad_rows.size // (SC_WORKERS * SC_ROWS_PER_STREAM)
    assert npad * SC_WORKERS * SC_ROWS_PER_STREAM == pad_rows.size
    pad_idx = pad_rows.reshape(SC_WORKERS, npad, SC_ROWS_PER_STREAM)
    zeros = jnp.zeros((SC_ROWS_PER_STREAM, width), h1p.dtype)
    mesh = plsc.VectorSubcoreMesh(core_axis_name="c", subcore_axis_name="s")

    @functools.partial(
        pl.kernel, out_type=jax.ShapeDtypeStruct((nrows_out, width), h1p.dtype), mesh=mesh,
        scratch_types=[pltpu.VMEM((TOP_K, nchunks, SC_ROWS_PER_STREAM), jnp.int32),
                       pltpu.VMEM((npad, SC_ROWS_PER_STREAM), jnp.int32),
                       pltpu.VMEM((2, SC_ROWS_PER_STREAM, width), h1p.dtype),
                       pltpu.SemaphoreType.DMA((2,)), pltpu.SemaphoreType.DMA((2,))],
        cost_estimate=_row_move_cost(n + nrows_out, width, h1p.dtype),
        name="sc_dispatch")
    def scatter_kernel(src_hbm, idx_hbm, pad_hbm, zeros_hbm, out_hbm, idx_v, pad_v, rows_v, rsem, ssem):
        wid = _sc_worker()
        base = wid * per_worker
        for k in range(TOP_K):
            pltpu.sync_copy(idx_hbm.at[k, wid], idx_v.at[k])
        pltpu.sync_copy(pad_hbm.at[wid], pad_v)

        def read(j, slot):
            src = src_hbm.at[pl.ds(base + j * SC_ROWS_PER_STREAM, SC_ROWS_PER_STREAM)]
            return pltpu.make_async_copy(src, rows_v.at[slot], rsem.at[slot])

        def scatter(j, slot, k):
            return pltpu.make_async_copy(rows_v.at[slot], out_hbm.at[idx_v.at[k, j]], ssem.at[slot])

        read(0, 0).start()

        @pl.loop(0, nchunks, step=2)
        def _(j0):
            for slot in range(2):
                j = j0 + slot
                read(j, slot).wait()

                @pl.when(j + 1 < nchunks)
                def _():
                    @pl.when(j >= 1)
                    def _():
                        for k in range(TOP_K):
                            scatter(j - 1, 1 - slot, k).wait()
                    read(j + 1, 1 - slot).start()

                for k in range(TOP_K):
                    scatter(j, slot, k).start()

        for k in range(TOP_K):
            scatter(nchunks - 2, 0, k).wait()
            scatter(nchunks - 1, 1, k).wait()

        pltpu.sync_copy(zeros_hbm, rows_v.at[0])
        fills = [pltpu.make_async_copy(rows_v.at[0], out_hbm.at[pad_v.at[c]], ssem.at[0]) for c in range(npad)]
        for f in fills:
            f.start()
        for f in fills:
            f.wait()

    return scatter_kernel(h1p, idx, pad_idx, zeros)


def _sorted_layout(counts, eid, rank, bm):
    n = eid.shape[1]
    nblocks = TOP_K * n // bm + N_EXPERTS
    expert = jnp.arange(N_EXPERTS, dtype=jnp.int32)
    before = expert[None, :] < expert[:, None]
    blocks = (counts + bm - 1) // bm
    first_blk = jnp.sum(jnp.where(before, blocks[None, :], 0), axis=1)
    starts = first_blk * bm
    dest = rank + jnp.sum(jnp.where(eid[None] == expert[:, None, None], starts[:, None, None], 0), axis=0)
    total = jnp.sum(blocks)
    j = jnp.arange(bm, dtype=jnp.int32)[None, :]
    npad = blocks * bm - counts
    spare_before = jnp.sum(jnp.where(before, (bm - npad)[None, :], 0), axis=1)
    pad_rows = jnp.where(j < npad[:, None], (starts + counts)[:, None] + j,
                         (total * bm + spare_before - npad)[:, None] + j).astype(jnp.int32)
    w = jnp.arange(nblocks, dtype=jnp.int32)
    src = jnp.minimum(w, total - 1)
    blk_end = first_blk + blocks
    owner = jnp.minimum(jnp.sum((blk_end[None, :] <= src[:, None]).astype(jnp.int32), axis=1), N_EXPERTS - 1)
    valid = (w < total).astype(jnp.int32)
    return dest.astype(jnp.int32), pad_rows, (src.astype(jnp.int32), owner.astype(jnp.int32), valid)


def _expert_kernel(src_ref, e_ref, valid_ref, x_ref, wg_ref, wu_ref, wd_ref, o_ref):
    w = pl.program_id(0)

    @pl.when(valid_ref[w] != 0)
    def _():
        x = _unpack_rows(x_ref[...]).astype(jnp.bfloat16)
        g = jnp.dot(x, wg_ref[0], preferred_element_type=jnp.float32)
        u = jnp.dot(x, wu_ref[0], preferred_element_type=jnp.float32)
        hmid = (jax.nn.silu(g) * u).astype(jnp.bfloat16)
        o_ref[...] = _pack_rows(jnp.dot(hmid, wd_ref[0], preferred_element_type=jnp.float32))

    @pl.when(valid_ref[w] == 0)
    def _():
        o_ref[...] = jnp.zeros(o_ref.shape, o_ref.dtype)


def _experts(xs, plan, w_gate, w_up, w_down, bm):
    nblocks = plan[0].shape[0]
    assert xs.shape[0] == nblocks * bm

    def weights(shape):
        return pl.BlockSpec((1,) + shape, lambda w, src, e, *_: (e[w], 0, 0))

    grid_spec = pltpu.PrefetchScalarGridSpec(
        num_scalar_prefetch=len(plan),
        grid=(nblocks,),
        in_specs=[
            pl.BlockSpec((bm, PACKED_WIDTH), lambda w, src, *_: (src[w], 0)),
            weights((D_MODEL, D_EXPERT)), weights((D_MODEL, D_EXPERT)), weights((D_EXPERT, D_MODEL)),
        ],
        out_specs=pl.BlockSpec((bm, PACKED_WIDTH), lambda w, *_: (w, 0)),
    )
    return pl.pallas_call(
        _expert_kernel,
        out_shape=jax.ShapeDtypeStruct(xs.shape, jnp.uint32),
        grid_spec=grid_spec,
        compiler_params=_cparams("parallel"),
        cost_estimate=_cost(2 * xs.shape[0] * 3 * D_MODEL * D_EXPERT,
                            2 * xs.size * 4 + 2 * 3 * N_EXPERTS * D_MODEL * D_EXPERT, xs.shape[0] * D_EXPERT),
        name="experts",
    )(*plan, xs, w_gate, w_up, w_down)


def _sc_gather_rows(table, idx):
    nrows = idx.shape[0]
    width = table.shape[1]
    per_worker = nrows // SC_WORKERS
    nchunks = per_worker // SC_ROWS_PER_STREAM
    assert nchunks * SC_ROWS_PER_STREAM * SC_WORKERS == nrows and nchunks % 2 == 0
    mesh = plsc.VectorSubcoreMesh(core_axis_name="c", subcore_axis_name="s")

    @functools.partial(
        pl.kernel, out_type=jax.ShapeDtypeStruct((nrows, width), table.dtype), mesh=mesh,
        scratch_types=[pltpu.VMEM((per_worker,), jnp.int32),
                       pltpu.VMEM((2, SC_ROWS_PER_STREAM, width), table.dtype),
                       pltpu.SemaphoreType.DMA((2,)), pltpu.SemaphoreType.DMA((2,))],
        cost_estimate=_row_move_cost(2 * nrows, width, table.dtype),
        name="sc_gather")
    def gather_kernel(table_hbm, idx_hbm, out_hbm, idx_v, rows_v, gsem, wsem):
        base = _sc_worker() * per_worker
        pltpu.sync_copy(idx_hbm.at[pl.ds(base, per_worker)], idx_v)

        def gather(j, slot):
            rows = idx_v.at[pl.ds(j * SC_ROWS_PER_STREAM, SC_ROWS_PER_STREAM)]
            return pltpu.make_async_copy(table_hbm.at[rows], rows_v.at[slot], gsem.at[slot])

        def write(j, slot):
            dst = out_hbm.at[pl.ds(base + j * SC_ROWS_PER_STREAM, SC_ROWS_PER_STREAM)]
            return pltpu.make_async_copy(rows_v.at[slot], dst, wsem.at[slot])

        gather(0, 0).start()

        @pl.loop(0, nchunks, step=2)
        def _(j0):
            for slot in range(2):
                j = j0 + slot
                gather(j, slot).wait()

                @pl.when(j + 1 < nchunks)
                def _():
                    @pl.when(j >= 1)
                    def _():
                        write(j - 1, 1 - slot).wait()
                    gather(j + 1, 1 - slot).start()

                write(j, slot).start()

        write(nchunks - 2, 0).wait()
        write(nchunks - 1, 1).wait()

    return gather_kernel(table, idx)


def _finalize_kernel(h1_ref, y1_ref, y2_ref, info_ref, g_ref, b_ref, o_ref):
    tt = h1_ref.shape[0]
    pad = jnp.zeros((LANES - INFO_ROWS, tt), jnp.float32)
    info = jnp.concatenate([info_ref[...], pad], axis=0).T
    moe = (_unpack_rows(y1_ref[...]) * info[:, INFO_W1:INFO_W1 + 1]
           + _unpack_rows(y2_ref[...]) * info[:, INFO_W2:INFO_W2 + 1])
    o_ref[...] = _layer_norm(DEEPNORM_ALPHA * h1_ref[...] + moe, g_ref[...], b_ref[...])


def _combine(h1, info, dest_t, ys, ln_g, ln_b, tt):
    n = h1.shape[0]
    nsteps = n // tt
    yg = _sc_gather_rows(ys, dest_t.reshape(TOP_K * n))
    return pl.pallas_call(
        _finalize_kernel,
        out_shape=jax.ShapeDtypeStruct((n, D_MODEL), jnp.float32),
        grid=(nsteps,),
        in_specs=[
            pl.BlockSpec((tt, D_MODEL), lambda i: (i, 0)),
            pl.BlockSpec((tt, PACKED_WIDTH), lambda i: (i, 0)),
            pl.BlockSpec((tt, PACKED_WIDTH), lambda i: (nsteps + i, 0)),
            pl.BlockSpec((INFO_ROWS, tt), lambda i: (0, i)),
            pl.BlockSpec((1, D_MODEL), lambda i: (0, 0)),
            pl.BlockSpec((1, D_MODEL), lambda i: (0, 0)),
        ],
        out_specs=pl.BlockSpec((tt, D_MODEL), lambda i: (i, 0)),
        compiler_params=_cparams("parallel"),
        cost_estimate=_cost(n * D_MODEL * 16, n * (8 * D_MODEL + 8 * PACKED_WIDTH)),
        name="finalize",
    )(h1, yg, yg, info, ln_g, ln_b)


TM_QKV = 1024
TQ_WIN = 1024
TM_MERGE = 1024
TT_ROWS = 1024
BM_EXPERT = 512


def _prepare_weights(ln_in_g, ln_in_b, w_in, attn_sink, rel_pos_bias, w_proj_a, w_proj_b, w_out,
                     ln1_g, ln1_b, w_route_group, b_route_group, w_route_expert, b_route_expert,
                     ln2_g, ln2_b):
    bf = jnp.bfloat16
    w = w_in[0]
    splits = np.cumsum([WIDTH_A, KV_WIDTH_A, KV_WIDTH_A, WIDTH_B, WIDTH_B, WIDTH_B, D_MODEL])
    wqa, wka, wva, wqb, wkb, wvb, wga, wgb = jnp.split(w, [int(s) for s in splits], axis=1)
    wqa = (wqa.reshape(D_MODEL, N_KV_HEADS_A, GQA_GROUP, HEAD_DIM).transpose(0, 2, 1, 3)
           .reshape(D_MODEL, WIDTH_A))
    w_qkv = jnp.concatenate([wqa, wqb, wkb, wvb, wka, wva], axis=1).astype(bf)
    w_gates = jnp.concatenate([wga, wgb], axis=1).astype(bf)
    w_pa = (w_proj_a[0].reshape(N_KV_HEADS_A, GQA_GROUP, HEAD_DIM, D_MODEL).transpose(1, 0, 2, 3)
            .reshape(WIDTH_A, D_MODEL).astype(bf))
    w_pb = w_proj_b[0].astype(bf)
    w_o = w_out[0].astype(bf)
    pad = ROUTE_ROWS - N_GROUPS - N_EXPERTS
    w_r = jnp.concatenate([w_route_group[0].T, w_route_expert[0].T, jnp.zeros((pad, D_MODEL), jnp.float32)], axis=0)
    w_route = w_r.astype(bf)
    b_r = jnp.concatenate([b_route_group[0], b_route_expert[0], jnp.zeros((pad,), jnp.float32)])
    b_r = jnp.broadcast_to(b_r[:, None], (ROUTE_ROWS, TM_MERGE))
    row = lambda v: v.reshape(1, D_MODEL)
    return dict(
        ln_in_g=row(ln_in_g), ln_in_b=row(ln_in_b), w_qkv=w_qkv, w_gates=w_gates,
        sink=attn_sink[0].astype(jnp.float32), nat_bias=_nat_bias_table(rel_pos_bias[0]),
        w_pa=w_pa, w_pb=w_pb, w_o=w_o, ln1_g=row(ln1_g[0]), ln1_b=row(ln1_b[0]),
        w_route=w_route, b_r=b_r, ln2_g=row(ln2_g[0]), ln2_b=row(ln2_b[0]))


def _after(value, other):
    if other is None:
        return value
    other = other.astype(jnp.float32)
    zero = jnp.where(jnp.isfinite(other), other, 0.0) * 0.0
    return value + zero.astype(value.dtype)


def _attend_and_route(x, p, after=None, cast=()):
    bsz, t, _ = x.shape
    n = bsz * t
    x2 = x.reshape(n, D_MODEL)
    qkv, cast_bf16 = _qkv(x2, _after(p["ln_in_g"], after), p["ln_in_b"], p["w_qkv"], TM_QKV, cast)
    oa = _win_attention(qkv, p["sink"], bsz, t, TQ_WIN)
    ob = _nat_attention(qkv, p["nat_bias"], bsz, t)
    cnt0 = jnp.zeros((ROUTE_ROWS, TM_MERGE), jnp.float32)
    h1, h1p, info, cnt = _merge(x2, oa, ob, p["ln_in_g"], p["ln_in_b"], p["w_gates"], p["w_pa"], p["w_pb"],
                                p["w_o"], p["ln1_g"], p["ln1_b"], p["w_route"], p["b_r"], cnt0, TM_MERGE)
    counts = cnt[EXPERT_ROW0:EXPERT_ROW0 + N_EXPERTS, 0].astype(jnp.int32)
    eid = info[INFO_E1:INFO_E2 + 1].astype(jnp.int32)
    rank = info[INFO_R1:INFO_R2 + 1].astype(jnp.int32)
    dest_t, pad_rows, plan = _sorted_layout(counts, eid, rank, BM_EXPERT)
    return dict(shape=x.shape, h1=h1, h1p=h1p, info=info, counts=counts, dest_t=dest_t, pad_rows=pad_rows,
                plan=plan, cast=cast_bf16)


def _run_experts(r, expert_weights, after=None):
    src, owner, valid = r["plan"]
    xs = _dispatch(r["h1p"], r["dest_t"], r["pad_rows"], src.shape[0] * BM_EXPERT)
    return _experts(xs, (src, owner, _after(valid, after)), *expert_weights, BM_EXPERT)


def _finish(r, ys, p, after=None):
    out = _combine(r["h1"], r["info"], r["dest_t"], ys, _after(p["ln2_g"], after), p["ln2_b"], TT_ROWS)
    return out.reshape(r["shape"])


def kernel(x_prompt, x_sample, ln_in_g, ln_in_b, w_in, attn_sink, rel_pos_bias, w_proj_a, w_proj_b, w_out,
           ln1_g, ln1_b, w_route_group, b_route_group, w_route_expert, b_route_expert,
           w_gate, w_up, w_down, ln2_g, ln2_b):
    p = _prepare_weights(ln_in_g, ln_in_b, w_in, attn_sink, rel_pos_bias, w_proj_a, w_proj_b, w_out,
                         ln1_g, ln1_b, w_route_group, b_route_group, w_route_expert, b_route_expert,
                         ln2_g, ln2_b)
    rp = _attend_and_route(x_prompt, p, cast=(w_gate[0], w_up[0], w_down[0]))
    glue_done = rp["dest_t"][0, 0] + rp["pad_rows"][0, 0] + sum(a[0] for a in rp["plan"])
    rs = _attend_and_route(x_sample, p, after=glue_done)
    ys_p = _run_experts(rp, rp["cast"])
    ys_s = _run_experts(rs, rp["cast"], after=ys_p[0, 0])
    y_prompt = _finish(rp, ys_p, p)
    y_sample = _finish(rs, ys_s, p, after=y_prompt[0, 0, 0])
    return (y_prompt, y_sample)
```

```python
import functools

import numpy as np
import jax
import jax.numpy as jnp
from jax import lax
from jax.experimental import pallas as pl
from jax.experimental.pallas import tpu as pltpu
from jax.experimental.pallas import tpu_sc as plsc

D_MODEL = 1024
HEAD_DIM = 64
N_HEADS_A = 8
N_KV_HEADS_A = 2
WINDOW = 128
N_HEADS_B = 8
GRID_W = 64
NA_ROWS = 8
NA_COLS = 16
N_GROUPS = 4
EXPERTS_PER_GROUP = 8
N_EXPERTS = N_GROUPS * EXPERTS_PER_GROUP
TOP_K = 2
D_EXPERT = D_MODEL // 2
LN_EPS = 1e-5
DEPTH = 1
DEEPNORM_ALPHA = (2.0 * DEPTH) ** 0.25
WIDTH_A = N_HEADS_A * HEAD_DIM
KV_WIDTH_A = N_KV_HEADS_A * HEAD_DIM
WIDTH_B = N_HEADS_B * HEAD_DIM
QKV_WIDTH = WIDTH_A + 2 * KV_WIDTH_A + 3 * WIDTH_B

LANES = 128
VMEM_LIMIT_BYTES = 56 * 1024 * 1024

NEG_BIG = -1e30
LOG2E = float(np.log2(np.e))

QA_COL, QB_COL, KB_COL, VB_COL = 0, WIDTH_A, WIDTH_A + WIDTH_B, WIDTH_A + 2 * WIDTH_B
KA_COL = WIDTH_A + 3 * WIDTH_B
VA_COL = KA_COL + KV_WIDTH_A

GQA_GROUP = N_HEADS_A // N_KV_HEADS_A


def _cparams(*sem):
    return pltpu.CompilerParams(dimension_semantics=sem, vmem_limit_bytes=VMEM_LIMIT_BYTES)


def _cost(flops, bytes_accessed, transcendentals=0):
    return pl.CostEstimate(flops=int(flops), transcendentals=int(transcendentals), bytes_accessed=int(bytes_accessed))


def _layer_norm(x, g, b):
    mu = jnp.mean(x, axis=-1, keepdims=True)
    xc = x - mu
    var = jnp.mean(xc * xc, axis=-1, keepdims=True)
    return xc * lax.rsqrt(var + LN_EPS) * g + b


PACKED_WIDTH = D_MODEL // 2


def _pack_rows(x):
    def rne(v):
        return v + jnp.uint32(0x7FFF) + ((v >> 16) & jnp.uint32(1))
    hi = lax.bitcast_convert_type(x[:, :PACKED_WIDTH], jnp.uint32)
    lo = lax.bitcast_convert_type(x[:, PACKED_WIDTH:], jnp.uint32)
    return (rne(hi) & jnp.uint32(0xFFFF0000)) | (rne(lo) >> 16)


def _unpack_rows(w):
    hi = lax.bitcast_convert_type(w & jnp.uint32(0xFFFF0000), jnp.float32)
    lo = lax.bitcast_convert_type(w << 16, jnp.float32)
    return jnp.concatenate([hi, lo], axis=1)


def _qkv_kernel(x_ref, g_ref, b_ref, w_ref, *rest):
    o_ref = rest[len(rest) // 2]
    h = _layer_norm(x_ref[...], g_ref[...], b_ref[...])
    y = jnp.dot(h.astype(jnp.bfloat16), w_ref[...], preferred_element_type=jnp.float32)
    col = lax.broadcasted_iota(jnp.int32, (1, QKV_WIDTH), 1)
    y = y * jnp.where(col < KB_COL, HEAD_DIM ** -0.5 * LOG2E, 1.0)
    o_ref[...] = y.astype(jnp.bfloat16)
    ncast = len(rest) // 2
    for src_ref, dst_ref in zip(rest[:ncast], rest[ncast + 1:]):
        dst_ref[...] = src_ref[...].astype(jnp.bfloat16)


def _qkv(x2, ln_g, ln_b, w_qkv, tm, cast=()):
    n = x2.shape[0]
    steps = n // tm
    per_step = -(-N_EXPERTS // steps)
    assert all(c.shape[0] == N_EXPERTS for c in cast) and (steps * per_step) % N_EXPERTS == 0
    revisit = steps * per_step // N_EXPERTS

    def expert_block(c):
        return pl.BlockSpec((per_step,) + c.shape[1:], lambda i: (i // revisit, 0, 0))

    outs = pl.pallas_call(
        _qkv_kernel,
        out_shape=[jax.ShapeDtypeStruct((n, QKV_WIDTH), jnp.bfloat16)]
        + [jax.ShapeDtypeStruct(c.shape, jnp.bfloat16) for c in cast],
        grid=(steps,),
        in_specs=[
            pl.BlockSpec((tm, D_MODEL), lambda i: (i, 0)),
            pl.BlockSpec((1, D_MODEL), lambda i: (0, 0)),
            pl.BlockSpec((1, D_MODEL), lambda i: (0, 0)),
            pl.BlockSpec((D_MODEL, QKV_WIDTH), lambda i: (0, 0)),
        ] + [expert_block(c) for c in cast],
        out_specs=[pl.BlockSpec((tm, QKV_WIDTH), lambda i: (i, 0))] + [expert_block(c) for c in cast],
        compiler_params=_cparams("arbitrary" if cast else "parallel"),
        cost_estimate=_cost(2 * n * D_MODEL * QKV_WIDTH, n * (4 * D_MODEL + 2 * QKV_WIDTH)),
        name="qkv",
    )(x2, ln_g, ln_b, w_qkv, *cast)
    return outs[0], tuple(outs[1:])


WIN_BLK = 128
WIN_LOOKAHEAD = 2


def _win_bias_table():
    qi = np.arange(WIN_BLK)[:, None]
    kj = np.arange(3 * WIN_BLK)[None, :]
    dist = np.abs(kj - WIN_BLK - qi).astype(np.float64)
    slopes = 2.0 ** (-8.0 * np.arange(1, N_HEADS_A + 1) / N_HEADS_A)
    per_head = np.where(dist <= WINDOW, -slopes[:, None, None] * dist[None] * LOG2E, NEG_BIG)
    groups = [np.concatenate([per_head[j], per_head[j + 4]], axis=0) for j in range(4)]
    return np.stack(groups).astype(np.float32)


def _win_kernel(sink_ref, q_ref, kp_ref, km_ref, kn_ref, vp_ref, vm_ref, vn_ref, bias_ref, o_ref,
                *, nsub, nblk_seq):
    i = pl.program_id(1)
    kcat = jnp.concatenate([kp_ref[...], km_ref[...], kn_ref[...]], axis=0)
    vcat = jnp.concatenate([vp_ref[...], vm_ref[...], vn_ref[...]], axis=0)
    lo = lax.broadcasted_iota(jnp.int32, (1, LANES), 1) < HEAD_DIM
    col = lax.broadcasted_iota(jnp.int32, (1, 3 * WIN_BLK), 1)
    top = lax.broadcasted_iota(jnp.int32, (2 * WIN_BLK, 1), 0) < WIN_BLK
    zero = jnp.zeros((), jnp.bfloat16)

    def scores(j, g):
        n = i * nsub + j
        off_seq = ((col < WIN_BLK) & (n == 0)) | ((col >= 2 * WIN_BLK) & (n == nblk_seq - 1))
        edge = jnp.where(off_seq, NEG_BIG, 0.0)
        qg = q_ref[WIN_BLK * j:WIN_BLK * (j + 1), LANES * g:LANES * (g + 1)]
        qm = jnp.concatenate([jnp.where(lo, qg, zero), jnp.where(lo, zero, qg)], axis=0)
        kj = kcat[WIN_BLK * j:WIN_BLK * (j + 3)]
        s = lax.dot_general(qm, kj, (((1,), (1,)), ((), ())), preferred_element_type=jnp.float32)
        return s + bias_ref[g] + edge

    def attend(s, j, g):
        vj = vcat[WIN_BLK * j:WIN_BLK * (j + 3)]
        sink = jnp.where(top, sink_ref[g], sink_ref[g + 4]) * LOG2E
        m = jnp.maximum(jnp.max(s, axis=-1, keepdims=True), sink)
        p = jnp.exp2(s - m)
        l = jnp.sum(p, axis=-1, keepdims=True) + jnp.exp2(sink - m)
        o2 = jnp.dot(p.astype(jnp.bfloat16), vj, preferred_element_type=jnp.float32)
        o2 = o2 * (1.0 / l)
        o_ref[WIN_BLK * j:WIN_BLK * (j + 1), LANES * g:LANES * (g + 1)] = (
            jnp.where(lo, o2[:WIN_BLK], o2[WIN_BLK:]).astype(jnp.bfloat16))

    chains = [(j, g) for j in range(nsub) for g in range(4)]
    pending = [scores(*c) for c in chains[:WIN_LOOKAHEAD]]
    for idx, c in enumerate(chains):
        s = pending.pop(0)
        if idx + WIN_LOOKAHEAD < len(chains):
            pending.append(scores(*chains[idx + WIN_LOOKAHEAD]))
        attend(s, *c)


def _win_attention(qkv, sink, bsz, t, tq):
    n = bsz * t
    nsub = tq // WIN_BLK
    nblk_seq = t // WIN_BLK
    ntile = t // tq
    bias = jnp.asarray(_win_bias_table())

    def main_map(col):
        return lambda b, i, *_: (b * ntile + i, col)

    def prev_map(col):
        return lambda b, i, *_: (b * nblk_seq + jnp.maximum(i * nsub - 1, 0), col)

    def next_map(col):
        return lambda b, i, *_: (b * nblk_seq + jnp.minimum(i * nsub + nsub, nblk_seq - 1), col)

    halo = (WIN_BLK, LANES)
    ka, va = KA_COL // LANES, VA_COL // LANES
    grid_spec = pltpu.PrefetchScalarGridSpec(
        num_scalar_prefetch=1,
        grid=(bsz, ntile),
        in_specs=[
            pl.BlockSpec((tq, WIDTH_A), main_map(QA_COL // WIDTH_A)),
            pl.BlockSpec(halo, prev_map(ka)),
            pl.BlockSpec((tq, LANES), main_map(ka)),
            pl.BlockSpec(halo, next_map(ka)),
            pl.BlockSpec(halo, prev_map(va)),
            pl.BlockSpec((tq, LANES), main_map(va)),
            pl.BlockSpec(halo, next_map(va)),
            pl.BlockSpec((4, 2 * WIN_BLK, 3 * WIN_BLK), lambda b, i, *_: (0, 0, 0)),
        ],
        out_specs=pl.BlockSpec((tq, WIDTH_A), main_map(0)),
    )
    return pl.pallas_call(
        functools.partial(_win_kernel, nsub=nsub, nblk_seq=nblk_seq),
        out_shape=jax.ShapeDtypeStruct((n, WIDTH_A), jnp.bfloat16),
        grid_spec=grid_spec,
        compiler_params=_cparams("parallel", "parallel"),
        cost_estimate=_cost(n * N_HEADS_A * 3 * WIN_BLK * 4 * LANES, n * 2 * (2 * WIDTH_A + 6 * KV_WIDTH_A),
                            n * N_HEADS_A * 3 * WIN_BLK),
        name="win",
    )(sink, qkv, qkv, qkv, qkv, qkv, qkv, qkv, bias)


NAT_ROWS_PER_STEP = 16
NAT_HALO_ROWS = NA_ROWS // 2
NAT_KEYS = NA_ROWS * GRID_W
NAT_ROWS_PER_TRIP = 8
NAT_HEADS_PER_CHAIN = 2
NAT_LOOKAHEAD = 4


def _nat_bias_table(rpb):
    c = np.arange(GRID_W)
    cs = np.clip(c - NA_COLS // 2, 0, GRID_W - NA_COLS)
    col_mask = (c[None, :] >= cs[:, None]) & (c[None, :] < cs[:, None] + NA_COLS)
    dc = np.clip(c[None, :] - c[:, None] + (NA_COLS - 1), 0, 2 * NA_COLS - 2)
    onehot = jnp.asarray(dc[None] == np.arange(2 * NA_COLS - 1)[:, None, None], jnp.float32)
    picked = jnp.einsum("hdj,jqc->hqdc", rpb, onehot, precision=lax.Precision.HIGHEST)
    t1 = jnp.where(col_mask[None, :, None, :], picked * LOG2E, NEG_BIG)
    flat = t1.reshape(N_HEADS_B, GRID_W, (2 * NA_ROWS - 1) * GRID_W)
    shifts = jnp.stack([flat[:, :, sh * GRID_W:sh * GRID_W + NAT_KEYS] for sh in range(NA_ROWS)], axis=1)
    tb = shifts.reshape(N_HEADS_B // 2, 2, NA_ROWS, GRID_W, NAT_KEYS).transpose(0, 2, 1, 3, 4)
    return tb.reshape(N_HEADS_B // 2, NA_ROWS, 2 * GRID_W, NAT_KEYS)


def _nat_kernel(q_ref, kp_ref, km_ref, kn_ref, vp_ref, vm_ref, vn_ref, tb_ref, o_ref, kcat, vcat,
                *, rows_seq):
    i = pl.program_id(1)
    halo = NAT_HALO_ROWS * GRID_W
    main = NAT_ROWS_PER_STEP * GRID_W
    kcat[0:halo] = kp_ref[...]
    kcat[halo:halo + main] = km_ref[...]
    kcat[halo + main:2 * halo + main] = kn_ref[...]
    vcat[0:halo] = vp_ref[...]
    vcat[halo:halo + main] = vm_ref[...]
    vcat[halo + main:2 * halo + main] = vn_ref[...]
    width = NAT_HEADS_PER_CHAIN * HEAD_DIM
    head_of_lane = lax.broadcasted_iota(jnp.int32, (1, width), 1) // HEAD_DIM
    zero = jnp.zeros((), jnp.bfloat16)
    r0 = i * NAT_ROWS_PER_STEP

    def scores(qr, c):
        r = r0 + qr
        rs = jnp.clip(r - NA_ROWS // 2, 0, rows_seq - NA_ROWS)
        koff = pl.multiple_of((rs - r0 + NAT_HALO_ROWS) * GRID_W, GRID_W)
        sh = rs - r + (NA_ROWS - 1)
        qoff = pl.multiple_of(qr * GRID_W, GRID_W)
        cols = slice(width * c, width * (c + 1))
        qc = q_ref[pl.ds(qoff, GRID_W), cols]
        qm = jnp.concatenate([jnp.where(head_of_lane == h, qc, zero) for h in range(NAT_HEADS_PER_CHAIN)], axis=0)
        kw = kcat[pl.ds(koff, NAT_KEYS), cols]
        s = lax.dot_general(qm, kw, (((1,), (1,)), ((), ())), preferred_element_type=jnp.float32)
        pairs = NAT_HEADS_PER_CHAIN // 2
        bias = jnp.concatenate([tb_ref[pairs * c + k, sh] for k in range(pairs)], axis=0)
        return s + bias, koff, qoff

    def attend(s, koff, qoff, c):
        cols = slice(width * c, width * (c + 1))
        vw = vcat[pl.ds(koff, NAT_KEYS), cols]
        m = jnp.max(s, axis=-1, keepdims=True)
        pe = jnp.exp2(s - m)
        l = jnp.sum(pe, axis=-1, keepdims=True)
        o2 = jnp.dot(pe.astype(jnp.bfloat16), vw, preferred_element_type=jnp.float32)
        o2 = o2 * (1.0 / l)
        out = o2[:GRID_W]
        for h in range(1, NAT_HEADS_PER_CHAIN):
            out = jnp.where(head_of_lane == h, o2[GRID_W * h:GRID_W * (h + 1)], out)
        o_ref[pl.ds(qoff, GRID_W), cols] = out.astype(jnp.bfloat16)

    def trip(j, carry):
        chains = [(j * NAT_ROWS_PER_TRIP + q, c) for q in range(NAT_ROWS_PER_TRIP)
                  for c in range(N_HEADS_B // NAT_HEADS_PER_CHAIN)]
        pending = [scores(*c) for c in chains[:NAT_LOOKAHEAD]]
        for idx, (_, p) in enumerate(chains):
            s, koff, qoff = pending.pop(0)
            if idx + NAT_LOOKAHEAD < len(chains):
                pending.append(scores(*chains[idx + NAT_LOOKAHEAD]))
            attend(s, koff, qoff, p)
        return carry

    lax.fori_loop(0, NAT_ROWS_PER_STEP // NAT_ROWS_PER_TRIP, trip, 0)


def _nat_attention(qkv, tb, bsz, t):
    n = bsz * t
    rows_seq = t // GRID_W
    main = NAT_ROWS_PER_STEP * GRID_W
    halo = NAT_HALO_ROWS * GRID_W
    ntile = t // main
    nhalo_seq = t // halo
    per = main // halo

    def main_map(col):
        return lambda b, i: (b * ntile + i, col)

    def prev_map(col):
        return lambda b, i: (b * nhalo_seq + jnp.maximum(i * per - 1, 0), col)

    def next_map(col):
        return lambda b, i: (b * nhalo_seq + jnp.minimum(i * per + per, nhalo_seq - 1), col)

    qb, kb, vb = QB_COL // WIDTH_B, KB_COL // WIDTH_B, VB_COL // WIDTH_B
    return pl.pallas_call(
        functools.partial(_nat_kernel, rows_seq=rows_seq),
        out_shape=jax.ShapeDtypeStruct((n, WIDTH_B), jnp.bfloat16),
        grid=(bsz, ntile),
        in_specs=[
            pl.BlockSpec((main, WIDTH_B), main_map(qb)),
            pl.BlockSpec((halo, WIDTH_B), prev_map(kb)),
            pl.BlockSpec((main, WIDTH_B), main_map(kb)),
            pl.BlockSpec((halo, WIDTH_B), next_map(kb)),
            pl.BlockSpec((halo, WIDTH_B), prev_map(vb)),
            pl.BlockSpec((main, WIDTH_B), main_map(vb)),
            pl.BlockSpec((halo, WIDTH_B), next_map(vb)),
            pl.BlockSpec((N_HEADS_B // 2, NA_ROWS, 2 * GRID_W, NAT_KEYS), lambda b, i: (0, 0, 0, 0)),
        ],
        out_specs=pl.BlockSpec((main, WIDTH_B), main_map(0)),
        scratch_shapes=[pltpu.VMEM((main + 2 * halo, WIDTH_B), jnp.bfloat16),
                        pltpu.VMEM((main + 2 * halo, WIDTH_B), jnp.bfloat16)],
        compiler_params=_cparams("parallel", "parallel"),
        cost_estimate=_cost(n * N_HEADS_B * NAT_KEYS * 4 * LANES, n * 2 * 6 * WIDTH_B, n * N_HEADS_B * NAT_KEYS),
        name="nat",
    )(qkv, qkv, qkv, qkv, qkv, qkv, qkv, tb)


EXPERT_ROW0 = N_GROUPS
ROUTE_ROWS = 48
INFO_E1, INFO_E2, INFO_R1, INFO_R2, INFO_W1, INFO_W2 = range(6)
INFO_ROWS = 8
MERGE_SUBTILES = 4


def _route(lt, carry, tri):
    rr, tm = lt.shape
    row = lax.broadcasted_iota(jnp.int32, (rr, tm), 0).astype(jnp.float32)
    none = jnp.float32(rr)

    def first_max(sel):
        m = jnp.max(jnp.where(sel, lt, NEG_BIG), axis=0, keepdims=True)
        idx = jnp.min(jnp.where(sel & (lt == m), row, none), axis=0, keepdims=True)
        return m, idx

    is_group = row < N_GROUPS
    mg, g = first_max(is_group)
    pg_sel = 1.0 / jnp.sum(jnp.where(is_group, jnp.exp(jnp.where(is_group, lt, mg) - mg), 0.0),
                           axis=0, keepdims=True)
    row0 = EXPERT_ROW0 + EXPERTS_PER_GROUP * g
    in_group = (row >= row0) & (row < row0 + EXPERTS_PER_GROUP)
    m1, i1 = first_max(in_group)
    m2, i2 = first_max(in_group & (row != i1))
    e2 = jnp.exp(m2 - m1)
    w1 = pg_sel / (1.0 + e2)
    w2 = pg_sel * e2 / (1.0 + e2)

    oh1 = row == i1
    oh2 = row == i2
    both = (oh1 | oh2).astype(jnp.bfloat16)
    before = jnp.dot(both, tri, preferred_element_type=jnp.float32) + carry
    r1 = jnp.sum(jnp.where(oh1, before, 0.0), axis=0, keepdims=True)
    r2 = jnp.sum(jnp.where(oh2, before, 0.0), axis=0, keepdims=True)
    new_carry = carry + jnp.sum(both.astype(jnp.float32), axis=1, keepdims=True)

    field = lax.broadcasted_iota(jnp.int32, (INFO_ROWS, tm), 0)
    info = jnp.zeros((INFO_ROWS, tm), jnp.float32)
    for k, v in ((INFO_E1, i1 - EXPERT_ROW0), (INFO_E2, i2 - EXPERT_ROW0), (INFO_R1, r1), (INFO_R2, r2),
                 (INFO_W1, w1), (INFO_W2, w2)):
        info = jnp.where(field == k, v, info)
    return info, new_carry


def _merge_kernel(x_ref, oa_ref, ob_ref, lng_ref, lnb_ref, wg_ref, wpa_ref, wpb_ref, wo_ref,
                  l1g_ref, l1b_ref, wr_ref, br_ref, cnt0_ref,
                  h1_ref, h1p_ref, info_ref, cnt_ref, carry_ref, tri_ref):
    tm = x_ref.shape[0]

    @pl.when(pl.program_id(0) == 0)
    def _():
        carry_ref[...] = cnt0_ref[...]
        r = lax.broadcasted_iota(jnp.int32, (tm, tm), 0)
        c = lax.broadcasted_iota(jnp.int32, (tm, tm), 1)
        tri_ref[...] = (r < c).astype(jnp.bfloat16)

    def project(rows):
        h = _layer_norm(x_ref[rows], lng_ref[...], lnb_ref[...])
        gates = jnp.dot(h.astype(jnp.bfloat16), wg_ref[...], preferred_element_type=jnp.float32)
        pa = jnp.dot(oa_ref[rows], wpa_ref[...], preferred_element_type=jnp.float32)
        pb = jnp.dot(ob_ref[rows], wpb_ref[...], preferred_element_type=jnp.float32)
        return h, gates, pa, pb

    def mix(h, gates, pa, pb):
        mixin = jax.nn.sigmoid(gates[:, :D_MODEL]) * pa + jax.nn.sigmoid(gates[:, D_MODEL:]) * pb
        return DEEPNORM_ALPHA * h + jnp.dot(mixin.astype(jnp.bfloat16), wo_ref[...],
                                            preferred_element_type=jnp.float32)

    def norm_and_logits(pre, rows):
        h1 = _layer_norm(pre, l1g_ref[...], l1b_ref[...])
        h1_ref[rows] = h1
        h1p_ref[rows] = _pack_rows(h1)
        return lax.dot_general(wr_ref[...], h1.astype(jnp.bfloat16), (((1,), (1,)), ((), ())),
                               preferred_element_type=jnp.float32)

    sub = tm // MERGE_SUBTILES
    parts = [slice(k * sub, (k + 1) * sub) for k in range(MERGE_SUBTILES)]
    projected = [project(rows) for rows in parts]
    mixed = [mix(*pr) for pr in projected]
    logits_t = jnp.concatenate([norm_and_logits(pre, rows) for pre, rows in zip(mixed, parts)], axis=1)
    logits_t = logits_t + br_ref[...]
    info, carry = _route(logits_t, carry_ref[...], tri_ref[...])
    info_ref[...] = info
    carry_ref[...] = carry
    cnt_ref[...] = carry[:, :LANES]


def _merge(x2, oa, ob, ln_g, ln_b, w_gates, w_pa, w_pb, w_o, l1g, l1b, w_r, b_r, cnt0, tm):
    n = x2.shape[0]

    def const(shape):
        return pl.BlockSpec(shape, lambda i: (0,) * len(shape))

    def rows(width):
        return pl.BlockSpec((tm, width), lambda i: (i, 0))

    return pl.pallas_call(
        _merge_kernel,
        out_shape=(jax.ShapeDtypeStruct((n, D_MODEL), jnp.float32),
                   jax.ShapeDtypeStruct((n, PACKED_WIDTH), jnp.uint32),
                   jax.ShapeDtypeStruct((INFO_ROWS, n), jnp.float32),
                   jax.ShapeDtypeStruct((ROUTE_ROWS, LANES), jnp.float32)),
        grid=(n // tm,),
        in_specs=[
            rows(D_MODEL), rows(WIDTH_A), rows(WIDTH_B),
            const((1, D_MODEL)), const((1, D_MODEL)),
            const((D_MODEL, 2 * D_MODEL)),
            const((WIDTH_A, D_MODEL)), const((WIDTH_B, D_MODEL)),
            const((D_MODEL, D_MODEL)),
            const((1, D_MODEL)), const((1, D_MODEL)),
            const((ROUTE_ROWS, D_MODEL)), const((ROUTE_ROWS, tm)), const((ROUTE_ROWS, tm)),
        ],
        out_specs=(rows(D_MODEL), rows(PACKED_WIDTH), pl.BlockSpec((INFO_ROWS, tm), lambda i: (0, i)),
                   const((ROUTE_ROWS, LANES))),
        scratch_shapes=[pltpu.VMEM((ROUTE_ROWS, tm), jnp.float32), pltpu.VMEM((tm, tm), jnp.bfloat16)],
        compiler_params=_cparams("arbitrary"),
        cost_estimate=_cost(2 * n * D_MODEL * (3 * D_MODEL + WIDTH_A + WIDTH_B + ROUTE_ROWS),
                            n * (4 * D_MODEL + 2 * (WIDTH_A + WIDTH_B) + 4 * D_MODEL + 4 * PACKED_WIDTH),
                            n * 2 * D_MODEL),
        name="merge",
    )(x2, oa, ob, ln_g, ln_b, w_gates, w_pa, w_pb, w_o, l1g, l1b, w_r, b_r, cnt0)


SC_CORES = 2
SC_SUBCORES = 16
SC_WORKERS = SC_CORES * SC_SUBCORES
SC_ROWS_PER_STREAM = 64


def _sc_worker():
    return lax.axis_index("s") * SC_CORES + lax.axis_index("c")


def _row_move_cost(rows_moved, width, dtype):
    return pl.CostEstimate(flops=0, transcendentals=0, bytes_accessed=rows_moved * width * jnp.dtype(dtype).itemsize)


def _scatter_indices(dest_t, pad_rows):
    n = dest_t.shape[1]
    nchunks = n // (SC_WORKERS * SC_ROWS_PER_STREAM)
    npad = pad_rows.size // (SC_WORKERS * SC_ROWS_PER_STREAM)
    assert nchunks * SC_ROWS_PER_STREAM * SC_WORKERS == n and nchunks % 2 == 0
    assert npad * SC_WORKERS * SC_ROWS_PER_STREAM == pad_rows.size
    return (dest_t.reshape(TOP_K, SC_WORKERS, nchunks, SC_ROWS_PER_STREAM),
            pad_rows.reshape(SC_WORKERS, npad, SC_ROWS_PER_STREAM))


def _dispatch(h1p, idx, pad_idx, zeros, nrows_out):
    n, width = h1p.shape
    per_worker = n // SC_WORKERS
    nchunks, npad = idx.shape[2], pad_idx.shape[1]
    assert zeros.shape == (SC_ROWS_PER_STREAM, width) and zeros.dtype == h1p.dtype
    mesh = plsc.VectorSubcoreMesh(core_axis_name="c", subcore_axis_name="s")

    @functools.partial(
        pl.kernel, out_type=jax.ShapeDtypeStruct((nrows_out, width), h1p.dtype), mesh=mesh,
        scratch_types=[pltpu.VMEM((TOP_K, nchunks, SC_ROWS_PER_STREAM), jnp.int32),
                       pltpu.VMEM((npad, SC_ROWS_PER_STREAM), jnp.int32),
                       pltpu.VMEM((2, SC_ROWS_PER_STREAM, width), h1p.dtype),
                       pltpu.SemaphoreType.DMA((2,)), pltpu.SemaphoreType.DMA((2,))],
        cost_estimate=_row_move_cost(n + nrows_out, width, h1p.dtype),
        name="sc_dispatch")
    def scatter_kernel(src_hbm, idx_hbm, pad_hbm, zeros_hbm, out_hbm, idx_v, pad_v, rows_v, rsem, ssem):
        wid = _sc_worker()
        base = wid * per_worker
        for k in range(TOP_K):
            pltpu.sync_copy(idx_hbm.at[k, wid], idx_v.at[k])
        pltpu.sync_copy(pad_hbm.at[wid], pad_v)

        def read(j, slot):
            src = src_hbm.at[pl.ds(base + j * SC_ROWS_PER_STREAM, SC_ROWS_PER_STREAM)]
            return pltpu.make_async_copy(src, rows_v.at[slot], rsem.at[slot])

        def scatter(j, slot, k):
            return pltpu.make_async_copy(rows_v.at[slot], out_hbm.at[idx_v.at[k, j]], ssem.at[slot])

        read(0, 0).start()

        @pl.loop(0, nchunks, step=2)
        def _(j0):
            for slot in range(2):
                j = j0 + slot
                read(j, slot).wait()

                @pl.when(j + 1 < nchunks)
                def _():
                    @pl.when(j >= 1)
                    def _():
                        for k in range(TOP_K):
                            scatter(j - 1, 1 - slot, k).wait()
                    read(j + 1, 1 - slot).start()

                for k in range(TOP_K):
                    scatter(j, slot, k).start()

        for k in range(TOP_K):
            scatter(nchunks - 2, 0, k).wait()
            scatter(nchunks - 1, 1, k).wait()

        pltpu.sync_copy(zeros_hbm, rows_v.at[0])
        fills = [pltpu.make_async_copy(rows_v.at[0], out_hbm.at[pad_v.at[c]], ssem.at[0]) for c in range(npad)]
        for f in fills:
            f.start()
        for f in fills:
            f.wait()

    return scatter_kernel(h1p, idx, pad_idx, zeros)


def _sorted_layout(counts, eid, rank, bm):
    n = eid.shape[1]
    nblocks = TOP_K * n // bm + N_EXPERTS
    expert = jnp.arange(N_EXPERTS, dtype=jnp.int32)
    before = expert[None, :] < expert[:, None]
    blocks = (counts + bm - 1) // bm
    first_blk = jnp.sum(jnp.where(before, blocks[None, :], 0), axis=1)
    starts = first_blk * bm
    dest = rank + jnp.sum(jnp.where(eid[None] == expert[:, None, None], starts[:, None, None], 0), axis=0)
    total = jnp.sum(blocks)
    j = jnp.arange(bm, dtype=jnp.int32)[None, :]
    npad = blocks * bm - counts
    spare_before = jnp.sum(jnp.where(before, (bm - npad)[None, :], 0), axis=1)
    pad_rows = jnp.where(j < npad[:, None], (starts + counts)[:, None] + j,
                         (total * bm + spare_before - npad)[:, None] + j).astype(jnp.int32)
    w = jnp.arange(nblocks, dtype=jnp.int32)
    src = jnp.minimum(w, total - 1)
    blk_end = first_blk + blocks
    owner = jnp.minimum(jnp.sum((blk_end[None, :] <= src[:, None]).astype(jnp.int32), axis=1), N_EXPERTS - 1)
    valid = (w < total).astype(jnp.int32)
    return dest.astype(jnp.int32), pad_rows, (src.astype(jnp.int32), owner.astype(jnp.int32), valid)


def _expert_kernel(src_ref, e_ref, valid_ref, x_ref, wg_ref, wu_ref, wd_ref, o_ref):
    w = pl.program_id(0)

    @pl.when(valid_ref[w] != 0)
    def _():
        x = _unpack_rows(x_ref[...]).astype(jnp.bfloat16)
        g = jnp.dot(x, wg_ref[0], preferred_element_type=jnp.float32)
        u = jnp.dot(x, wu_ref[0], preferred_element_type=jnp.float32)
        hmid = (jax.nn.silu(g) * u).astype(jnp.bfloat16)
        o_ref[...] = _pack_rows(jnp.dot(hmid, wd_ref[0], preferred_element_type=jnp.float32))

    @pl.when(valid_ref[w] == 0)
    def _():
        o_ref[...] = jnp.zeros(o_ref.shape, o_ref.dtype)


def _experts(xs, plan, w_gate, w_up, w_down, bm):
    nblocks = plan[0].shape[0]
    assert xs.shape[0] == nblocks * bm

    def weights(shape):
        return pl.BlockSpec((1,) + shape, lambda w, src, e, *_: (e[w], 0, 0))

    grid_spec = pltpu.PrefetchScalarGridSpec(
        num_scalar_prefetch=len(plan),
        grid=(nblocks,),
        in_specs=[
            pl.BlockSpec((bm, PACKED_WIDTH), lambda w, src, *_: (src[w], 0)),
            weights((D_MODEL, D_EXPERT)), weights((D_MODEL, D_EXPERT)), weights((D_EXPERT, D_MODEL)),
        ],
        out_specs=pl.BlockSpec((bm, PACKED_WIDTH), lambda w, *_: (w, 0)),
    )
    return pl.pallas_call(
        _expert_kernel,
        out_shape=jax.ShapeDtypeStruct(xs.shape, jnp.uint32),
        grid_spec=grid_spec,
        compiler_params=_cparams("parallel"),
        cost_estimate=_cost(2 * xs.shape[0] * 3 * D_MODEL * D_EXPERT,
                            2 * xs.size * 4 + 2 * 3 * N_EXPERTS * D_MODEL * D_EXPERT, xs.shape[0] * D_EXPERT),
        name="experts",
    )(*plan, xs, w_gate, w_up, w_down)


def _sc_gather_rows(table, idx):
    nrows = idx.shape[0]
    width = table.shape[1]
    per_worker = nrows // SC_WORKERS
    nchunks = per_worker // SC_ROWS_PER_STREAM
    assert nchunks * SC_ROWS_PER_STREAM * SC_WORKERS == nrows and nchunks % 2 == 0
    mesh = plsc.VectorSubcoreMesh(core_axis_name="c", subcore_axis_name="s")

    @functools.partial(
        pl.kernel, out_type=jax.ShapeDtypeStruct((nrows, width), table.dtype), mesh=mesh,
        scratch_types=[pltpu.VMEM((per_worker,), jnp.int32),
                       pltpu.VMEM((2, SC_ROWS_PER_STREAM, width), table.dtype),
                       pltpu.SemaphoreType.DMA((2,)), pltpu.SemaphoreType.DMA((2,))],
        cost_estimate=_row_move_cost(2 * nrows, width, table.dtype),
        name="sc_gather")
    def gather_kernel(table_hbm, idx_hbm, out_hbm, idx_v, rows_v, gsem, wsem):
        base = _sc_worker() * per_worker
        pltpu.sync_copy(idx_hbm.at[pl.ds(base, per_worker)], idx_v)

        def gather(j, slot):
            rows = idx_v.at[pl.ds(j * SC_ROWS_PER_STREAM, SC_ROWS_PER_STREAM)]
            return pltpu.make_async_copy(table_hbm.at[rows], rows_v.at[slot], gsem.at[slot])

        def write(j, slot):
            dst = out_hbm.at[pl.ds(base + j * SC_ROWS_PER_STREAM, SC_ROWS_PER_STREAM)]
            return pltpu.make_async_copy(rows_v.at[slot], dst, wsem.at[slot])

        gather(0, 0).start()

        @pl.loop(0, nchunks, step=2)
        def _(j0):
            for slot in range(2):
                j = j0 + slot
                gather(j, slot).wait()

                @pl.when(j + 1 < nchunks)
                def _():
                    @pl.when(j >= 1)
                    def _():
                        write(j - 1, 1 - slot).wait()
                    gather(j + 1, 1 - slot).start()

                write(j, slot).start()

        write(nchunks - 2, 0).wait()
        write(nchunks - 1, 1).wait()

    return gather_kernel(table, idx)


def _finalize_kernel(h1_ref, y1_ref, y2_ref, info_ref, g_ref, b_ref, o_ref):
    tt = h1_ref.shape[0]
    pad = jnp.zeros((LANES - INFO_ROWS, tt), jnp.float32)
    info = jnp.concatenate([info_ref[...], pad], axis=0).T
    moe = (_unpack_rows(y1_ref[...]) * info[:, INFO_W1:INFO_W1 + 1]
           + _unpack_rows(y2_ref[...]) * info[:, INFO_W2:INFO_W2 + 1])
    o_ref[...] = _layer_norm(DEEPNORM_ALPHA * h1_ref[...] + moe, g_ref[...], b_ref[...])


def _combine(h1, info, dest_flat, ys, ln_g, ln_b, tt):
    n = h1.shape[0]
    nsteps = n // tt
    yg = _sc_gather_rows(ys, dest_flat)
    return pl.pallas_call(
        _finalize_kernel,
        out_shape=jax.ShapeDtypeStruct((n, D_MODEL), jnp.float32),
        grid=(nsteps,),
        in_specs=[
            pl.BlockSpec((tt, D_MODEL), lambda i: (i, 0)),
            pl.BlockSpec((tt, PACKED_WIDTH), lambda i: (i, 0)),
            pl.BlockSpec((tt, PACKED_WIDTH), lambda i: (nsteps + i, 0)),
            pl.BlockSpec((INFO_ROWS, tt), lambda i: (0, i)),
            pl.BlockSpec((1, D_MODEL), lambda i: (0, 0)),
            pl.BlockSpec((1, D_MODEL), lambda i: (0, 0)),
        ],
        out_specs=pl.BlockSpec((tt, D_MODEL), lambda i: (i, 0)),
        compiler_params=_cparams("parallel"),
        cost_estimate=_cost(n * D_MODEL * 16, n * (8 * D_MODEL + 8 * PACKED_WIDTH)),
        name="finalize",
    )(h1, yg, yg, info, ln_g, ln_b)


TM_QKV = 1024
TQ_WIN = 1024
TM_MERGE = 1024
TT_ROWS = 1024
BM_EXPERT = 512


def _prepare_weights(ln_in_g, ln_in_b, w_in, attn_sink, rel_pos_bias, w_proj_a, w_proj_b, w_out,
                     ln1_g, ln1_b, w_route_group, b_route_group, w_route_expert, b_route_expert,
                     ln2_g, ln2_b):
    bf = jnp.bfloat16
    w = w_in[0]
    splits = np.cumsum([WIDTH_A, KV_WIDTH_A, KV_WIDTH_A, WIDTH_B, WIDTH_B, WIDTH_B, D_MODEL])
    wqa, wka, wva, wqb, wkb, wvb, wga, wgb = jnp.split(w, [int(s) for s in splits], axis=1)
    wqa = (wqa.reshape(D_MODEL, N_KV_HEADS_A, GQA_GROUP, HEAD_DIM).transpose(0, 2, 1, 3)
           .reshape(D_MODEL, WIDTH_A))
    w_qkv = jnp.concatenate([wqa, wqb, wkb, wvb, wka, wva], axis=1).astype(bf)
    w_gates = jnp.concatenate([wga, wgb], axis=1).astype(bf)
    w_pa = (w_proj_a[0].reshape(N_KV_HEADS_A, GQA_GROUP, HEAD_DIM, D_MODEL).transpose(1, 0, 2, 3)
            .reshape(WIDTH_A, D_MODEL).astype(bf))
    w_pb = w_proj_b[0].astype(bf)
    w_o = w_out[0].astype(bf)
    pad = ROUTE_ROWS - N_GROUPS - N_EXPERTS
    w_r = jnp.concatenate([w_route_group[0].T, w_route_expert[0].T, jnp.zeros((pad, D_MODEL), jnp.float32)], axis=0)
    w_route = w_r.astype(bf)
    b_r = jnp.concatenate([b_route_group[0], b_route_expert[0], jnp.zeros((pad,), jnp.float32)])
    b_r = jnp.broadcast_to(b_r[:, None], (ROUTE_ROWS, TM_MERGE))
    row = lambda v: v.reshape(1, D_MODEL)
    return dict(
        ln_in_g=row(ln_in_g), ln_in_b=row(ln_in_b), w_qkv=w_qkv, w_gates=w_gates,
        sink=attn_sink[0].astype(jnp.float32), nat_bias=_nat_bias_table(rel_pos_bias[0]),
        w_pa=w_pa, w_pb=w_pb, w_o=w_o, ln1_g=row(ln1_g[0]), ln1_b=row(ln1_b[0]),
        w_route=w_route, b_r=b_r, ln2_g=row(ln2_g[0]), ln2_b=row(ln2_b[0]))


def _after(value, other):
    if other is None:
        return value
    other = other.astype(jnp.float32)
    zero = jnp.where(jnp.isfinite(other), other, 0.0) * 0.0
    return value + zero.astype(value.dtype)


def _attend_and_route(x, p, after=None, cast=()):
    bsz, t, _ = x.shape
    n = bsz * t
    x2 = x.reshape(n, D_MODEL)
    qkv, cast_bf16 = _qkv(x2, _after(p["ln_in_g"], after), p["ln_in_b"], p["w_qkv"], TM_QKV, cast)
    oa = _win_attention(qkv, p["sink"], bsz, t, TQ_WIN)
    ob = _nat_attention(qkv, p["nat_bias"], bsz, t)
    cnt0 = jnp.zeros((ROUTE_ROWS, TM_MERGE), jnp.float32)
    h1, h1p, info, cnt = _merge(x2, oa, ob, p["ln_in_g"], p["ln_in_b"], p["w_gates"], p["w_pa"], p["w_pb"],
                                p["w_o"], p["ln1_g"], p["ln1_b"], p["w_route"], p["b_r"], cnt0, TM_MERGE)
    counts = cnt[EXPERT_ROW0:EXPERT_ROW0 + N_EXPERTS, 0].astype(jnp.int32)
    eid = info[INFO_E1:INFO_E2 + 1].astype(jnp.int32)
    rank = info[INFO_R1:INFO_R2 + 1].astype(jnp.int32)
    dest_t, pad_rows, plan = _sorted_layout(counts, eid, rank, BM_EXPERT)
    zero_rows = jnp.zeros((SC_ROWS_PER_STREAM, PACKED_WIDTH), jnp.uint32)
    moves = lax.optimization_barrier(_scatter_indices(dest_t, pad_rows) + (zero_rows, dest_t.reshape(TOP_K * n)))
    return dict(shape=x.shape, h1=h1, h1p=h1p, info=info, counts=counts, moves=moves, plan=plan, cast=cast_bf16)


def _run_experts(r, expert_weights, after=None):
    src, owner, valid = r["plan"]
    scatter_idx, pad_idx, zero_rows, _ = r["moves"]
    xs = _dispatch(r["h1p"], scatter_idx, pad_idx, zero_rows, src.shape[0] * BM_EXPERT)
    return _experts(xs, (src, owner, _after(valid, after)), *expert_weights, BM_EXPERT)


def _finish(r, ys, p, after=None):
    out = _combine(r["h1"], r["info"], r["moves"][3], ys, _after(p["ln2_g"], after), p["ln2_b"], TT_ROWS)
    return out.reshape(r["shape"])


def kernel(x_prompt, x_sample, ln_in_g, ln_in_b, w_in, attn_sink, rel_pos_bias, w_proj_a, w_proj_b, w_out,
           ln1_g, ln1_b, w_route_group, b_route_group, w_route_expert, b_route_expert,
           w_gate, w_up, w_down, ln2_g, ln2_b):
    p = _prepare_weights(ln_in_g, ln_in_b, w_in, attn_sink, rel_pos_bias, w_proj_a, w_proj_b, w_out,
                         ln1_g, ln1_b, w_route_group, b_route_group, w_route_expert, b_route_expert,
                         ln2_g, ln2_b)
    rp = _attend_and_route(x_prompt, p, cast=(w_gate[0], w_up[0], w_down[0]))
    glue_done = sum(a.reshape(-1)[0].astype(jnp.float32) for a in rp["moves"] + rp["plan"])
    rs = _attend_and_route(x_sample, p, after=glue_done)
    ys_p = _run_experts(rp, rp["cast"])
    ys_s = _run_experts(rs, rp["cast"], after=ys_p[0, 0])
    y_prompt = _finish(rp, ys_p, p)
    y_sample = _finish(rs, ys_s, p, after=y_prompt[0, 0, 0])
    return (y_prompt, y_sample)
```

```python
import functools

import numpy as np
import jax
import jax.numpy as jnp
from jax import lax
from jax.experimental import pallas as pl
from jax.experimental.pallas import tpu as pltpu
from jax.experimental.pallas import tpu_sc as plsc

D_MODEL = 1024
HEAD_DIM = 64
N_HEADS_A = 8
N_KV_HEADS_A = 2
WINDOW = 128
N_HEADS_B = 8
GRID_W = 64
NA_ROWS = 8
NA_COLS = 16
N_GROUPS = 4
EXPERTS_PER_GROUP = 8
N_EXPERTS = N_GROUPS * EXPERTS_PER_GROUP
TOP_K = 2
D_EXPERT = D_MODEL // 2
LN_EPS = 1e-5
DEPTH = 1
DEEPNORM_ALPHA = (2.0 * DEPTH) ** 0.25
WIDTH_A = N_HEADS_A * HEAD_DIM
KV_WIDTH_A = N_KV_HEADS_A * HEAD_DIM
WIDTH_B = N_HEADS_B * HEAD_DIM
QKV_WIDTH = WIDTH_A + 2 * KV_WIDTH_A + 3 * WIDTH_B

LANES = 128
VMEM_LIMIT_BYTES = 56 * 1024 * 1024

NEG_BIG = -1e30
LOG2E = float(np.log2(np.e))

QA_COL, QB_COL, KB_COL, VB_COL = 0, WIDTH_A, WIDTH_A + WIDTH_B, WIDTH_A + 2 * WIDTH_B
KA_COL = WIDTH_A + 3 * WIDTH_B
VA_COL = KA_COL + KV_WIDTH_A

GQA_GROUP = N_HEADS_A // N_KV_HEADS_A


def _cparams(*sem):
    return pltpu.CompilerParams(dimension_semantics=sem, vmem_limit_bytes=VMEM_LIMIT_BYTES)


def _layer_norm(x, g, b):
    mu = jnp.mean(x, axis=-1, keepdims=True)
    xc = x - mu
    var = jnp.mean(xc * xc, axis=-1, keepdims=True)
    return xc * lax.rsqrt(var + LN_EPS) * g + b


PACKED_WIDTH = D_MODEL // 2


def _pack_rows(x):
    def rne(v):
        return v + jnp.uint32(0x7FFF) + ((v >> 16) & jnp.uint32(1))
    hi = lax.bitcast_convert_type(x[:, :PACKED_WIDTH], jnp.uint32)
    lo = lax.bitcast_convert_type(x[:, PACKED_WIDTH:], jnp.uint32)
    return (rne(hi) & jnp.uint32(0xFFFF0000)) | (rne(lo) >> 16)


def _unpack_rows(w):
    hi = lax.bitcast_convert_type(w & jnp.uint32(0xFFFF0000), jnp.float32)
    lo = lax.bitcast_convert_type(w << 16, jnp.float32)
    return jnp.concatenate([hi, lo], axis=1)


def _qkv_kernel(x_ref, g_ref, b_ref, w_ref, *rest):
    o_ref = rest[len(rest) // 2]
    h = _layer_norm(x_ref[...], g_ref[...], b_ref[...])
    y = jnp.dot(h.astype(jnp.bfloat16), w_ref[...], preferred_element_type=jnp.float32)
    col = lax.broadcasted_iota(jnp.int32, (1, QKV_WIDTH), 1)
    y = y * jnp.where(col < KB_COL, HEAD_DIM ** -0.5 * LOG2E, 1.0)
    o_ref[...] = y.astype(jnp.bfloat16)
    ncast = len(rest) // 2
    for src_ref, dst_ref in zip(rest[:ncast], rest[ncast + 1:]):
        dst_ref[...] = src_ref[...].astype(jnp.bfloat16)


def _qkv(x2, ln_g, ln_b, w_qkv, tm, cast=()):
    n = x2.shape[0]
    steps = n // tm
    per_step = -(-N_EXPERTS // steps)
    assert all(c.shape[0] == N_EXPERTS for c in cast) and (steps * per_step) % N_EXPERTS == 0
    revisit = steps * per_step // N_EXPERTS

    def expert_block(c):
        return pl.BlockSpec((per_step,) + c.shape[1:], lambda i: (i // revisit, 0, 0))

    outs = pl.pallas_call(
        _qkv_kernel,
        out_shape=[jax.ShapeDtypeStruct((n, QKV_WIDTH), jnp.bfloat16)]
        + [jax.ShapeDtypeStruct(c.shape, jnp.bfloat16) for c in cast],
        grid=(steps,),
        in_specs=[
            pl.BlockSpec((tm, D_MODEL), lambda i: (i, 0)),
            pl.BlockSpec((1, D_MODEL), lambda i: (0, 0)),
            pl.BlockSpec((1, D_MODEL), lambda i: (0, 0)),
            pl.BlockSpec((D_MODEL, QKV_WIDTH), lambda i: (0, 0)),
        ] + [expert_block(c) for c in cast],
        out_specs=[pl.BlockSpec((tm, QKV_WIDTH), lambda i: (i, 0))] + [expert_block(c) for c in cast],
        compiler_params=_cparams("arbitrary" if cast else "parallel"),
        name="qkv",
    )(x2, ln_g, ln_b, w_qkv, *cast)
    return outs[0], tuple(outs[1:])


WIN_BLK = 128
WIN_LOOKAHEAD = 2


def _win_bias_table():
    qi = np.arange(WIN_BLK)[:, None]
    kj = np.arange(3 * WIN_BLK)[None, :]
    dist = np.abs(kj - WIN_BLK - qi).astype(np.float64)
    slopes = 2.0 ** (-8.0 * np.arange(1, N_HEADS_A + 1) / N_HEADS_A)
    per_head = np.where(dist <= WINDOW, -slopes[:, None, None] * dist[None] * LOG2E, NEG_BIG)
    groups = [np.concatenate([per_head[j], per_head[j + 4]], axis=0) for j in range(4)]
    return np.stack(groups).astype(np.float32)


def _win_kernel(sink_ref, q_ref, kp_ref, km_ref, kn_ref, vp_ref, vm_ref, vn_ref, bias_ref, o_ref,
                *, nsub, nblk_seq):
    i = pl.program_id(1)
    kcat = jnp.concatenate([kp_ref[...], km_ref[...], kn_ref[...]], axis=0)
    vcat = jnp.concatenate([vp_ref[...], vm_ref[...], vn_ref[...]], axis=0)
    lo = lax.broadcasted_iota(jnp.int32, (1, LANES), 1) < HEAD_DIM
    col = lax.broadcasted_iota(jnp.int32, (1, 3 * WIN_BLK), 1)
    top = lax.broadcasted_iota(jnp.int32, (2 * WIN_BLK, 1), 0) < WIN_BLK
    zero = jnp.zeros((), jnp.bfloat16)

    def scores(j, g):
        n = i * nsub + j
        off_seq = ((col < WIN_BLK) & (n == 0)) | ((col >= 2 * WIN_BLK) & (n == nblk_seq - 1))
        edge = jnp.where(off_seq, NEG_BIG, 0.0)
        qg = q_ref[WIN_BLK * j:WIN_BLK * (j + 1), LANES * g:LANES * (g + 1)]
        qm = jnp.concatenate([jnp.where(lo, qg, zero), jnp.where(lo, zero, qg)], axis=0)
        kj = kcat[WIN_BLK * j:WIN_BLK * (j + 3)]
        s = lax.dot_general(qm, kj, (((1,), (1,)), ((), ())), preferred_element_type=jnp.float32)
        return s + bias_ref[g] + edge

    def attend(s, j, g):
        vj = vcat[WIN_BLK * j:WIN_BLK * (j + 3)]
        sink = jnp.where(top, sink_ref[g], sink_ref[g + 4]) * LOG2E
        m = jnp.maximum(jnp.max(s, axis=-1, keepdims=True), sink)
        p = jnp.exp2(s - m)
        l = jnp.sum(p, axis=-1, keepdims=True) + jnp.exp2(sink - m)
        o2 = jnp.dot(p.astype(jnp.bfloat16), vj, preferred_element_type=jnp.float32)
        o2 = o2 * (1.0 / l)
        o_ref[WIN_BLK * j:WIN_BLK * (j + 1), LANES * g:LANES * (g + 1)] = (
            jnp.where(lo, o2[:WIN_BLK], o2[WIN_BLK:]).astype(jnp.bfloat16))

    chains = [(j, g) for j in range(nsub) for g in range(4)]
    pending = [scores(*c) for c in chains[:WIN_LOOKAHEAD]]
    for idx, c in enumerate(chains):
        s = pending.pop(0)
        if idx + WIN_LOOKAHEAD < len(chains):
            pending.append(scores(*chains[idx + WIN_LOOKAHEAD]))
        attend(s, *c)


def _win_attention(qkv, sink, bsz, t, tq):
    n = bsz * t
    nsub = tq // WIN_BLK
    nblk_seq = t // WIN_BLK
    ntile = t // tq
    bias = jnp.asarray(_win_bias_table())

    def main_map(col):
        return lambda b, i, *_: (b * ntile + i, col)

    def prev_map(col):
        return lambda b, i, *_: (b * nblk_seq + jnp.maximum(i * nsub - 1, 0), col)

    def next_map(col):
        return lambda b, i, *_: (b * nblk_seq + jnp.minimum(i * nsub + nsub, nblk_seq - 1), col)

    halo = (WIN_BLK, LANES)
    ka, va = KA_COL // LANES, VA_COL // LANES
    grid_spec = pltpu.PrefetchScalarGridSpec(
        num_scalar_prefetch=1,
        grid=(bsz, ntile),
        in_specs=[
            pl.BlockSpec((tq, WIDTH_A), main_map(QA_COL // WIDTH_A)),
            pl.BlockSpec(halo, prev_map(ka)),
            pl.BlockSpec((tq, LANES), main_map(ka)),
            pl.BlockSpec(halo, next_map(ka)),
            pl.BlockSpec(halo, prev_map(va)),
            pl.BlockSpec((tq, LANES), main_map(va)),
            pl.BlockSpec(halo, next_map(va)),
            pl.BlockSpec((4, 2 * WIN_BLK, 3 * WIN_BLK), lambda b, i, *_: (0, 0, 0)),
        ],
        out_specs=pl.BlockSpec((tq, WIDTH_A), main_map(0)),
    )
    return pl.pallas_call(
        functools.partial(_win_kernel, nsub=nsub, nblk_seq=nblk_seq),
        out_shape=jax.ShapeDtypeStruct((n, WIDTH_A), jnp.bfloat16),
        grid_spec=grid_spec,
        compiler_params=_cparams("parallel", "parallel"),
        name="win",
    )(sink, qkv, qkv, qkv, qkv, qkv, qkv, qkv, bias)


NAT_ROWS_PER_STEP = 16
NAT_HALO_ROWS = NA_ROWS // 2
NAT_KEYS = NA_ROWS * GRID_W
NAT_ROWS_PER_TRIP = 8
NAT_HEADS_PER_CHAIN = 2
NAT_LOOKAHEAD = 4


def _nat_bias_table(rpb):
    c = np.arange(GRID_W)
    cs = np.clip(c - NA_COLS // 2, 0, GRID_W - NA_COLS)
    col_mask = (c[None, :] >= cs[:, None]) & (c[None, :] < cs[:, None] + NA_COLS)
    dc = np.clip(c[None, :] - c[:, None] + (NA_COLS - 1), 0, 2 * NA_COLS - 2)
    onehot = jnp.asarray(dc[None] == np.arange(2 * NA_COLS - 1)[:, None, None], jnp.float32)
    picked = jnp.einsum("hdj,jqc->hqdc", rpb, onehot, precision=lax.Precision.HIGHEST)
    t1 = jnp.where(col_mask[None, :, None, :], picked * LOG2E, NEG_BIG)
    flat = t1.reshape(N_HEADS_B, GRID_W, (2 * NA_ROWS - 1) * GRID_W)
    shifts = jnp.stack([flat[:, :, sh * GRID_W:sh * GRID_W + NAT_KEYS] for sh in range(NA_ROWS)], axis=1)
    tb = shifts.reshape(N_HEADS_B // 2, 2, NA_ROWS, GRID_W, NAT_KEYS).transpose(0, 2, 1, 3, 4)
    return tb.reshape(N_HEADS_B // 2, NA_ROWS, 2 * GRID_W, NAT_KEYS)


def _nat_kernel(q_ref, kp_ref, km_ref, kn_ref, vp_ref, vm_ref, vn_ref, tb_ref, o_ref, kcat, vcat,
                *, rows_seq):
    i = pl.program_id(1)
    halo = NAT_HALO_ROWS * GRID_W
    main = NAT_ROWS_PER_STEP * GRID_W
    kcat[0:halo] = kp_ref[...]
    kcat[halo:halo + main] = km_ref[...]
    kcat[halo + main:2 * halo + main] = kn_ref[...]
    vcat[0:halo] = vp_ref[...]
    vcat[halo:halo + main] = vm_ref[...]
    vcat[halo + main:2 * halo + main] = vn_ref[...]
    width = NAT_HEADS_PER_CHAIN * HEAD_DIM
    head_of_lane = lax.broadcasted_iota(jnp.int32, (1, width), 1) // HEAD_DIM
    zero = jnp.zeros((), jnp.bfloat16)
    r0 = i * NAT_ROWS_PER_STEP

    def scores(qr, c):
        r = r0 + qr
        rs = jnp.clip(r - NA_ROWS // 2, 0, rows_seq - NA_ROWS)
        koff = pl.multiple_of((rs - r0 + NAT_HALO_ROWS) * GRID_W, GRID_W)
        sh = rs - r + (NA_ROWS - 1)
        qoff = pl.multiple_of(qr * GRID_W, GRID_W)
        cols = slice(width * c, width * (c + 1))
        qc = q_ref[pl.ds(qoff, GRID_W), cols]
        qm = jnp.concatenate([jnp.where(head_of_lane == h, qc, zero) for h in range(NAT_HEADS_PER_CHAIN)], axis=0)
        kw = kcat[pl.ds(koff, NAT_KEYS), cols]
        s = lax.dot_general(qm, kw, (((1,), (1,)), ((), ())), preferred_element_type=jnp.float32)
        pairs = NAT_HEADS_PER_CHAIN // 2
        bias = jnp.concatenate([tb_ref[pairs * c + k, sh] for k in range(pairs)], axis=0)
        return s + bias, koff, qoff

    def attend(s, koff, qoff, c):
        cols = slice(width * c, width * (c + 1))
        vw = vcat[pl.ds(koff, NAT_KEYS), cols]
        m = jnp.max(s, axis=-1, keepdims=True)
        pe = jnp.exp2(s - m)
        l = jnp.sum(pe, axis=-1, keepdims=True)
        o2 = jnp.dot(pe.astype(jnp.bfloat16), vw, preferred_element_type=jnp.float32)
        o2 = o2 * (1.0 / l)
        out = o2[:GRID_W]
        for h in range(1, NAT_HEADS_PER_CHAIN):
            out = jnp.where(head_of_lane == h, o2[GRID_W * h:GRID_W * (h + 1)], out)
        o_ref[pl.ds(qoff, GRID_W), cols] = out.astype(jnp.bfloat16)

    def trip(j, carry):
        chains = [(j * NAT_ROWS_PER_TRIP + q, c) for q in range(NAT_ROWS_PER_TRIP)
                  for c in range(N_HEADS_B // NAT_HEADS_PER_CHAIN)]
        pending = [scores(*c) for c in chains[:NAT_LOOKAHEAD]]
        for idx, (_, p) in enumerate(chains):
            s, koff, qoff = pending.pop(0)
            if idx + NAT_LOOKAHEAD < len(chains):
                pending.append(scores(*chains[idx + NAT_LOOKAHEAD]))
            attend(s, koff, qoff, p)
        return carry

    lax.fori_loop(0, NAT_ROWS_PER_STEP // NAT_ROWS_PER_TRIP, trip, 0)


def _nat_attention(qkv, tb, bsz, t):
    n = bsz * t
    rows_seq = t // GRID_W
    main = NAT_ROWS_PER_STEP * GRID_W
    halo = NAT_HALO_ROWS * GRID_W
    ntile = t // main
    nhalo_seq = t // halo
    per = main // halo

    def main_map(col):
        return lambda b, i: (b * ntile + i, col)

    def prev_map(col):
        return lambda b, i: (b * nhalo_seq + jnp.maximum(i * per - 1, 0), col)

    def next_map(col):
        return lambda b, i: (b * nhalo_seq + jnp.minimum(i * per + per, nhalo_seq - 1), col)

    qb, kb, vb = QB_COL // WIDTH_B, KB_COL // WIDTH_B, VB_COL // WIDTH_B
    return pl.pallas_call(
        functools.partial(_nat_kernel, rows_seq=rows_seq),
        out_shape=jax.ShapeDtypeStruct((n, WIDTH_B), jnp.bfloat16),
        grid=(bsz, ntile),
        in_specs=[
            pl.BlockSpec((main, WIDTH_B), main_map(qb)),
            pl.BlockSpec((halo, WIDTH_B), prev_map(kb)),
            pl.BlockSpec((main, WIDTH_B), main_map(kb)),
            pl.BlockSpec((halo, WIDTH_B), next_map(kb)),
            pl.BlockSpec((halo, WIDTH_B), prev_map(vb)),
            pl.BlockSpec((main, WIDTH_B), main_map(vb)),
            pl.BlockSpec((halo, WIDTH_B), next_map(vb)),
            pl.BlockSpec((N_HEADS_B // 2, NA_ROWS, 2 * GRID_W, NAT_KEYS), lambda b, i: (0, 0, 0, 0)),
        ],
        out_specs=pl.BlockSpec((main, WIDTH_B), main_map(0)),
        scratch_shapes=[pltpu.VMEM((main + 2 * halo, WIDTH_B), jnp.bfloat16),
                        pltpu.VMEM((main + 2 * halo, WIDTH_B), jnp.bfloat16)],
        compiler_params=_cparams("parallel", "parallel"),
        name="nat",
    )(qkv, qkv, qkv, qkv, qkv, qkv, qkv, tb)


EXPERT_ROW0 = N_GROUPS
ROUTE_ROWS = 48
INFO_E1, INFO_E2, INFO_R1, INFO_R2, INFO_W1, INFO_W2 = range(6)
INFO_ROWS = 8
MERGE_SUBTILES = 4


def _route(lt, carry, tri):
    rr, tm = lt.shape
    row = lax.broadcasted_iota(jnp.int32, (rr, tm), 0).astype(jnp.float32)
    none = jnp.float32(rr)

    def first_max(sel):
        m = jnp.max(jnp.where(sel, lt, NEG_BIG), axis=0, keepdims=True)
        idx = jnp.min(jnp.where(sel & (lt == m), row, none), axis=0, keepdims=True)
        return m, idx

    is_group = row < N_GROUPS
    mg, g = first_max(is_group)
    pg_sel = 1.0 / jnp.sum(jnp.where(is_group, jnp.exp(jnp.where(is_group, lt, mg) - mg), 0.0),
                           axis=0, keepdims=True)
    row0 = EXPERT_ROW0 + EXPERTS_PER_GROUP * g
    in_group = (row >= row0) & (row < row0 + EXPERTS_PER_GROUP)
    m1, i1 = first_max(in_group)
    m2, i2 = first_max(in_group & (row != i1))
    e2 = jnp.exp(m2 - m1)
    w1 = pg_sel / (1.0 + e2)
    w2 = pg_sel * e2 / (1.0 + e2)

    oh1 = row == i1
    oh2 = row == i2
    both = (oh1 | oh2).astype(jnp.bfloat16)
    before = jnp.dot(both, tri, preferred_element_type=jnp.float32) + carry
    r1 = jnp.sum(jnp.where(oh1, before, 0.0), axis=0, keepdims=True)
    r2 = jnp.sum(jnp.where(oh2, before, 0.0), axis=0, keepdims=True)
    new_carry = carry + jnp.sum(both.astype(jnp.float32), axis=1, keepdims=True)

    field = lax.broadcasted_iota(jnp.int32, (INFO_ROWS, tm), 0)
    info = jnp.zeros((INFO_ROWS, tm), jnp.float32)
    for k, v in ((INFO_E1, i1 - EXPERT_ROW0), (INFO_E2, i2 - EXPERT_ROW0), (INFO_R1, r1), (INFO_R2, r2),
                 (INFO_W1, w1), (INFO_W2, w2)):
        info = jnp.where(field == k, v, info)
    return info, new_carry


def _merge_kernel(x_ref, oa_ref, ob_ref, lng_ref, lnb_ref, wg_ref, wpa_ref, wpb_ref, wo_ref,
                  l1g_ref, l1b_ref, wr_ref, br_ref, cnt0_ref,
                  h1_ref, h1p_ref, info_ref, cnt_ref, carry_ref, tri_ref):
    tm = x_ref.shape[0]

    @pl.when(pl.program_id(0) == 0)
    def _():
        carry_ref[...] = cnt0_ref[...]
        r = lax.broadcasted_iota(jnp.int32, (tm, tm), 0)
        c = lax.broadcasted_iota(jnp.int32, (tm, tm), 1)
        tri_ref[...] = (r < c).astype(jnp.bfloat16)

    def project(rows):
        h = _layer_norm(x_ref[rows], lng_ref[...], lnb_ref[...])
        gates = jnp.dot(h.astype(jnp.bfloat16), wg_ref[...], preferred_element_type=jnp.float32)
        pa = jnp.dot(oa_ref[rows], wpa_ref[...], preferred_element_type=jnp.float32)
        pb = jnp.dot(ob_ref[rows], wpb_ref[...], preferred_element_type=jnp.float32)
        return h, gates, pa, pb

    def mix(h, gates, pa, pb):
        mixin = jax.nn.sigmoid(gates[:, :D_MODEL]) * pa + jax.nn.sigmoid(gates[:, D_MODEL:]) * pb
        return DEEPNORM_ALPHA * h + jnp.dot(mixin.astype(jnp.bfloat16), wo_ref[...],
                                            preferred_element_type=jnp.float32)

    def norm_and_logits(pre, rows):
        h1 = _layer_norm(pre, l1g_ref[...], l1b_ref[...])
        h1_ref[rows] = h1
        h1p_ref[rows] = _pack_rows(h1)
        return lax.dot_general(wr_ref[...], h1.astype(jnp.bfloat16), (((1,), (1,)), ((), ())),
                               preferred_element_type=jnp.float32)

    sub = tm // MERGE_SUBTILES
    parts = [slice(k * sub, (k + 1) * sub) for k in range(MERGE_SUBTILES)]
    projected = [project(rows) for rows in parts]
    mixed = [mix(*pr) for pr in projected]
    logits_t = jnp.concatenate([norm_and_logits(pre, rows) for pre, rows in zip(mixed, parts)], axis=1)
    logits_t = logits_t + br_ref[...]
    info, carry = _route(logits_t, carry_ref[...], tri_ref[...])
    info_ref[...] = info
    carry_ref[...] = carry
    cnt_ref[...] = carry[:, :LANES]


def _merge(x2, oa, ob, ln_g, ln_b, w_gates, w_pa, w_pb, w_o, l1g, l1b, w_r, b_r, cnt0, tm):
    n = x2.shape[0]

    def const(shape):
        return pl.BlockSpec(shape, lambda i: (0,) * len(shape))

    def rows(width):
        return pl.BlockSpec((tm, width), lambda i: (i, 0))

    return pl.pallas_call(
        _merge_kernel,
        out_shape=(jax.ShapeDtypeStruct((n, D_MODEL), jnp.float32),
                   jax.ShapeDtypeStruct((n, PACKED_WIDTH), jnp.uint32),
                   jax.ShapeDtypeStruct((INFO_ROWS, n), jnp.float32),
                   jax.ShapeDtypeStruct((ROUTE_ROWS, LANES), jnp.float32)),
        grid=(n // tm,),
        in_specs=[
            rows(D_MODEL), rows(WIDTH_A), rows(WIDTH_B),
            const((1, D_MODEL)), const((1, D_MODEL)),
            const((D_MODEL, 2 * D_MODEL)),
            const((WIDTH_A, D_MODEL)), const((WIDTH_B, D_MODEL)),
            const((D_MODEL, D_MODEL)),
            const((1, D_MODEL)), const((1, D_MODEL)),
            const((ROUTE_ROWS, D_MODEL)), const((ROUTE_ROWS, tm)), const((ROUTE_ROWS, tm)),
        ],
        out_specs=(rows(D_MODEL), rows(PACKED_WIDTH), pl.BlockSpec((INFO_ROWS, tm), lambda i: (0, i)),
                   const((ROUTE_ROWS, LANES))),
        scratch_shapes=[pltpu.VMEM((ROUTE_ROWS, tm), jnp.float32), pltpu.VMEM((tm, tm), jnp.bfloat16)],
        compiler_params=_cparams("arbitrary"),
        name="merge",
    )(x2, oa, ob, ln_g, ln_b, w_gates, w_pa, w_pb, w_o, l1g, l1b, w_r, b_r, cnt0)


SC_CORES = 2
SC_SUBCORES = 16
SC_WORKERS = SC_CORES * SC_SUBCORES
SC_ROWS_PER_STREAM = 64


def _sc_worker():
    return lax.axis_index("s") * SC_CORES + lax.axis_index("c")


def _scatter_indices(dest_t, pad_rows):
    n = dest_t.shape[1]
    nchunks = n // (SC_WORKERS * SC_ROWS_PER_STREAM)
    npad = pad_rows.size // (SC_WORKERS * SC_ROWS_PER_STREAM)
    assert nchunks * SC_ROWS_PER_STREAM * SC_WORKERS == n and nchunks % 2 == 0
    assert npad * SC_WORKERS * SC_ROWS_PER_STREAM == pad_rows.size
    return (dest_t.reshape(TOP_K, SC_WORKERS, nchunks, SC_ROWS_PER_STREAM),
            pad_rows.reshape(SC_WORKERS, npad, SC_ROWS_PER_STREAM))


def _dispatch(h1p, idx, pad_idx, zeros, nrows_out):
    n, width = h1p.shape
    per_worker = n // SC_WORKERS
    nchunks, npad = idx.shape[2], pad_idx.shape[1]
    assert zeros.shape == (SC_ROWS_PER_STREAM, width) and zeros.dtype == h1p.dtype
    mesh = plsc.VectorSubcoreMesh(core_axis_name="c", subcore_axis_name="s")

    @functools.partial(
        pl.kernel, out_type=jax.ShapeDtypeStruct((nrows_out, width), h1p.dtype), mesh=mesh,
        scratch_types=[pltpu.VMEM((TOP_K, nchunks, SC_ROWS_PER_STREAM), jnp.int32),
                       pltpu.VMEM((npad, SC_ROWS_PER_STREAM), jnp.int32),
                       pltpu.VMEM((2, SC_ROWS_PER_STREAM, width), h1p.dtype),
                       pltpu.SemaphoreType.DMA((2,)), pltpu.SemaphoreType.DMA((2,))],
        name="sc_dispatch")
    def scatter_kernel(src_hbm, idx_hbm, pad_hbm, zeros_hbm, out_hbm, idx_v, pad_v, rows_v, rsem, ssem):
        wid = _sc_worker()
        base = wid * per_worker
        for k in range(TOP_K):
            pltpu.sync_copy(idx_hbm.at[k, wid], idx_v.at[k])
        pltpu.sync_copy(pad_hbm.at[wid], pad_v)

        def read(j, slot):
            src = src_hbm.at[pl.ds(base + j * SC_ROWS_PER_STREAM, SC_ROWS_PER_STREAM)]
            return pltpu.make_async_copy(src, rows_v.at[slot], rsem.at[slot])

        def scatter(j, slot, k):
            return pltpu.make_async_copy(rows_v.at[slot], out_hbm.at[idx_v.at[k, j]], ssem.at[slot])

        read(0, 0).start()

        @pl.loop(0, nchunks, step=2)
        def _(j0):
            for slot in range(2):
                j = j0 + slot
                read(j, slot).wait()

                @pl.when(j + 1 < nchunks)
                def _():
                    @pl.when(j >= 1)
                    def _():
                        for k in range(TOP_K):
                            scatter(j - 1, 1 - slot, k).wait()
                    read(j + 1, 1 - slot).start()

                for k in range(TOP_K):
                    scatter(j, slot, k).start()

        for k in range(TOP_K):
            scatter(nchunks - 2, 0, k).wait()
            scatter(nchunks - 1, 1, k).wait()

        pltpu.sync_copy(zeros_hbm, rows_v.at[0])
        fills = [pltpu.make_async_copy(rows_v.at[0], out_hbm.at[pad_v.at[c]], ssem.at[0]) for c in range(npad)]
        for f in fills:
            f.start()
        for f in fills:
            f.wait()

    return scatter_kernel(h1p, idx, pad_idx, zeros)


def _sorted_layout(counts, eid, rank, bm):
    n = eid.shape[1]
    nblocks = TOP_K * n // bm + N_EXPERTS
    expert = jnp.arange(N_EXPERTS, dtype=jnp.int32)
    before = expert[None, :] < expert[:, None]
    blocks = (counts + bm - 1) // bm
    first_blk = jnp.sum(jnp.where(before, blocks[None, :], 0), axis=1)
    starts = first_blk * bm
    dest = rank + jnp.sum(jnp.where(eid[None] == expert[:, None, None], starts[:, None, None], 0), axis=0)
    total = jnp.sum(blocks)
    j = jnp.arange(bm, dtype=jnp.int32)[None, :]
    npad = blocks * bm - counts
    spare_before = jnp.sum(jnp.where(before, (bm - npad)[None, :], 0), axis=1)
    pad_rows = jnp.where(j < npad[:, None], (starts + counts)[:, None] + j,
                         (total * bm + spare_before - npad)[:, None] + j).astype(jnp.int32)
    w = jnp.arange(nblocks, dtype=jnp.int32)
    src = jnp.minimum(w, total - 1)
    blk_end = first_blk + blocks
    owner = jnp.minimum(jnp.sum((blk_end[None, :] <= src[:, None]).astype(jnp.int32), axis=1), N_EXPERTS - 1)
    valid = (w < total).astype(jnp.int32)
    return dest.astype(jnp.int32), pad_rows, (src.astype(jnp.int32), owner.astype(jnp.int32), valid)


def _expert_kernel(src_ref, e_ref, valid_ref, x_ref, wg_ref, wu_ref, wd_ref, o_ref):
    w = pl.program_id(0)

    @pl.when(valid_ref[w] != 0)
    def _():
        x = _unpack_rows(x_ref[...]).astype(jnp.bfloat16)
        g = jnp.dot(x, wg_ref[0], preferred_element_type=jnp.float32)
        u = jnp.dot(x, wu_ref[0], preferred_element_type=jnp.float32)
        hmid = (jax.nn.silu(g) * u).astype(jnp.bfloat16)
        o_ref[...] = _pack_rows(jnp.dot(hmid, wd_ref[0], preferred_element_type=jnp.float32))

    @pl.when(valid_ref[w] == 0)
    def _():
        o_ref[...] = jnp.zeros(o_ref.shape, o_ref.dtype)


def _experts(xs, plan, w_gate, w_up, w_down, bm):
    nblocks = plan[0].shape[0]
    assert xs.shape[0] == nblocks * bm

    def weights(shape):
        return pl.BlockSpec((1,) + shape, lambda w, src, e, *_: (e[w], 0, 0))

    grid_spec = pltpu.PrefetchScalarGridSpec(
        num_scalar_prefetch=len(plan),
        grid=(nblocks,),
        in_specs=[
            pl.BlockSpec((bm, PACKED_WIDTH), lambda w, src, *_: (src[w], 0)),
            weights((D_MODEL, D_EXPERT)), weights((D_MODEL, D_EXPERT)), weights((D_EXPERT, D_MODEL)),
        ],
        out_specs=pl.BlockSpec((bm, PACKED_WIDTH), lambda w, *_: (w, 0)),
    )
    return pl.pallas_call(
        _expert_kernel,
        out_shape=jax.ShapeDtypeStruct(xs.shape, jnp.uint32),
        grid_spec=grid_spec,
        compiler_params=_cparams("parallel"),
        name="experts",
    )(*plan, xs, w_gate, w_up, w_down)


def _sc_gather_rows(table, idx):
    nrows = idx.shape[0]
    width = table.shape[1]
    per_worker = nrows // SC_WORKERS
    nchunks = per_worker // SC_ROWS_PER_STREAM
    assert nchunks * SC_ROWS_PER_STREAM * SC_WORKERS == nrows and nchunks % 2 == 0
    mesh = plsc.VectorSubcoreMesh(core_axis_name="c", subcore_axis_name="s")

    @functools.partial(
        pl.kernel, out_type=jax.ShapeDtypeStruct((nrows, width), table.dtype), mesh=mesh,
        scratch_types=[pltpu.VMEM((per_worker,), jnp.int32),
                       pltpu.VMEM((2, SC_ROWS_PER_STREAM, width), table.dtype),
                       pltpu.SemaphoreType.DMA((2,)), pltpu.SemaphoreType.DMA((2,))],
        name="sc_gather")
    def gather_kernel(table_hbm, idx_hbm, out_hbm, idx_v, rows_v, gsem, wsem):
        base = _sc_worker() * per_worker
        pltpu.sync_copy(idx_hbm.at[pl.ds(base, per_worker)], idx_v)

        def gather(j, slot):
            rows = idx_v.at[pl.ds(j * SC_ROWS_PER_STREAM, SC_ROWS_PER_STREAM)]
            return pltpu.make_async_copy(table_hbm.at[rows], rows_v.at[slot], gsem.at[slot])

        def write(j, slot):
            dst = out_hbm.at[pl.ds(base + j * SC_ROWS_PER_STREAM, SC_ROWS_PER_STREAM)]
            return pltpu.make_async_copy(rows_v.at[slot], dst, wsem.at[slot])

        gather(0, 0).start()

        @pl.loop(0, nchunks, step=2)
        def _(j0):
            for slot in range(2):
                j = j0 + slot
                gather(j, slot).wait()

                @pl.when(j + 1 < nchunks)
                def _():
                    @pl.when(j >= 1)
                    def _():
                        write(j - 1, 1 - slot).wait()
                    gather(j + 1, 1 - slot).start()

                write(j, slot).start()

        write(nchunks - 2, 0).wait()
        write(nchunks - 1, 1).wait()

    return gather_kernel(table, idx)


def _finalize_kernel(h1_ref, y1_ref, y2_ref, info_ref, g_ref, b_ref, o_ref):
    tt = h1_ref.shape[0]
    pad = jnp.zeros((LANES - INFO_ROWS, tt), jnp.float32)
    info = jnp.concatenate([info_ref[...], pad], axis=0).T
    moe = (_unpack_rows(y1_ref[...]) * info[:, INFO_W1:INFO_W1 + 1]
           + _unpack_rows(y2_ref[...]) * info[:, INFO_W2:INFO_W2 + 1])
    o_ref[...] = _layer_norm(DEEPNORM_ALPHA * h1_ref[...] + moe, g_ref[...], b_ref[...])


def _combine(h1, info, dest_flat, ys, ln_g, ln_b, tt):
    n = h1.shape[0]
    nsteps = n // tt
    yg = _sc_gather_rows(ys, dest_flat)
    return pl.pallas_call(
        _finalize_kernel,
        out_shape=jax.ShapeDtypeStruct((n, D_MODEL), jnp.float32),
        grid=(nsteps,),
        in_specs=[
            pl.BlockSpec((tt, D_MODEL), lambda i: (i, 0)),
            pl.BlockSpec((tt, PACKED_WIDTH), lambda i: (i, 0)),
            pl.BlockSpec((tt, PACKED_WIDTH), lambda i: (nsteps + i, 0)),
            pl.BlockSpec((INFO_ROWS, tt), lambda i: (0, i)),
            pl.BlockSpec((1, D_MODEL), lambda i: (0, 0)),
            pl.BlockSpec((1, D_MODEL), lambda i: (0, 0)),
        ],
        out_specs=pl.BlockSpec((tt, D_MODEL), lambda i: (i, 0)),
        compiler_params=_cparams("parallel"),
        name="finalize",
    )(h1, yg, yg, info, ln_g, ln_b)


TM_QKV = 1024
TQ_WIN = 1024
TM_MERGE = 1024
TT_ROWS = 1024
BM_EXPERT = 512


def _prepare_weights(ln_in_g, ln_in_b, w_in, attn_sink, rel_pos_bias, w_proj_a, w_proj_b, w_out,
                     ln1_g, ln1_b, w_route_group, b_route_group, w_route_expert, b_route_expert,
                     ln2_g, ln2_b):
    bf = jnp.bfloat16
    w = w_in[0]
    splits = np.cumsum([WIDTH_A, KV_WIDTH_A, KV_WIDTH_A, WIDTH_B, WIDTH_B, WIDTH_B, D_MODEL])
    wqa, wka, wva, wqb, wkb, wvb, wga, wgb = jnp.split(w, [int(s) for s in splits], axis=1)
    wqa = (wqa.reshape(D_MODEL, N_KV_HEADS_A, GQA_GROUP, HEAD_DIM).transpose(0, 2, 1, 3)
           .reshape(D_MODEL, WIDTH_A))
    w_qkv = jnp.concatenate([wqa, wqb, wkb, wvb, wka, wva], axis=1).astype(bf)
    w_gates = jnp.concatenate([wga, wgb], axis=1).astype(bf)
    w_pa = (w_proj_a[0].reshape(N_KV_HEADS_A, GQA_GROUP, HEAD_DIM, D_MODEL).transpose(1, 0, 2, 3)
            .reshape(WIDTH_A, D_MODEL).astype(bf))
    w_pb = w_proj_b[0].astype(bf)
    w_o = w_out[0].astype(bf)
    pad = ROUTE_ROWS - N_GROUPS - N_EXPERTS
    w_r = jnp.concatenate([w_route_group[0].T, w_route_expert[0].T, jnp.zeros((pad, D_MODEL), jnp.float32)], axis=0)
    w_route = w_r.astype(bf)
    b_r = jnp.concatenate([b_route_group[0], b_route_expert[0], jnp.zeros((pad,), jnp.float32)])
    b_r = jnp.broadcast_to(b_r[:, None], (ROUTE_ROWS, TM_MERGE))
    row = lambda v: v.reshape(1, D_MODEL)
    return dict(
        ln_in_g=row(ln_in_g), ln_in_b=row(ln_in_b), w_qkv=w_qkv, w_gates=w_gates,
        sink=attn_sink[0].astype(jnp.float32), nat_bias=_nat_bias_table(rel_pos_bias[0]),
        w_pa=w_pa, w_pb=w_pb, w_o=w_o, ln1_g=row(ln1_g[0]), ln1_b=row(ln1_b[0]),
        w_route=w_route, b_r=b_r, ln2_g=row(ln2_g[0]), ln2_b=row(ln2_b[0]))


def _after(value, other):
    if other is None:
        return value
    other = other.astype(jnp.float32)
    zero = jnp.where(jnp.isfinite(other), other, 0.0) * 0.0
    return value + zero.astype(value.dtype)


def _attend_and_route(x, p, after=None, cast=()):
    bsz, t, _ = x.shape
    n = bsz * t
    x2 = x.reshape(n, D_MODEL)
    qkv, cast_bf16 = _qkv(x2, _after(p["ln_in_g"], after), p["ln_in_b"], p["w_qkv"], TM_QKV, cast)
    oa = _win_attention(qkv, p["sink"], bsz, t, TQ_WIN)
    ob = _nat_attention(qkv, p["nat_bias"], bsz, t)
    cnt0 = jnp.zeros((ROUTE_ROWS, TM_MERGE), jnp.float32)
    h1, h1p, info, cnt = _merge(x2, oa, ob, p["ln_in_g"], p["ln_in_b"], p["w_gates"], p["w_pa"], p["w_pb"],
                                p["w_o"], p["ln1_g"], p["ln1_b"], p["w_route"], p["b_r"], cnt0, TM_MERGE)
    counts = cnt[EXPERT_ROW0:EXPERT_ROW0 + N_EXPERTS, 0].astype(jnp.int32)
    eid = info[INFO_E1:INFO_E2 + 1].astype(jnp.int32)
    rank = info[INFO_R1:INFO_R2 + 1].astype(jnp.int32)
    dest_t, pad_rows, plan = _sorted_layout(counts, eid, rank, BM_EXPERT)
    zero_rows = jnp.zeros((SC_ROWS_PER_STREAM, PACKED_WIDTH), jnp.uint32)
    moves = lax.optimization_barrier(_scatter_indices(dest_t, pad_rows) + (zero_rows, dest_t.reshape(TOP_K * n)))
    return dict(shape=x.shape, h1=h1, h1p=h1p, info=info, counts=counts, moves=moves, plan=plan, cast=cast_bf16)


def _run_experts(r, expert_weights, after=None):
    src, owner, valid = r["plan"]
    scatter_idx, pad_idx, zero_rows, _ = r["moves"]
    xs = _dispatch(r["h1p"], scatter_idx, pad_idx, zero_rows, src.shape[0] * BM_EXPERT)
    return _experts(xs, (src, owner, _after(valid, after)), *expert_weights, BM_EXPERT)


def _finish(r, ys, p, after=None):
    out = _combine(r["h1"], r["info"], r["moves"][3], ys, _after(p["ln2_g"], after), p["ln2_b"], TT_ROWS)
    return out.reshape(r["shape"])


def kernel(x_prompt, x_sample, ln_in_g, ln_in_b, w_in, attn_sink, rel_pos_bias, w_proj_a, w_proj_b, w_out,
           ln1_g, ln1_b, w_route_group, b_route_group, w_route_expert, b_route_expert,
           w_gate, w_up, w_down, ln2_g, ln2_b):
    p = _prepare_weights(ln_in_g, ln_in_b, w_in, attn_sink, rel_pos_bias, w_proj_a, w_proj_b, w_out,
                         ln1_g, ln1_b, w_route_group, b_route_group, w_route_expert, b_route_expert,
                         ln2_g, ln2_b)
    rp = _attend_and_route(x_prompt, p, cast=(w_gate[0], w_up[0], w_down[0]))
    glue_done = sum(a.reshape(-1)[0].astype(jnp.float32) for a in rp["moves"] + rp["plan"])
    rs = _attend_and_route(x_sample, p, after=glue_done)
    ys_p = _run_experts(rp, rp["cast"])
    ys_s = _run_experts(rs, rp["cast"], after=ys_p[0, 0])
    y_prompt = _finish(rp, ys_p, p)
    y_sample = _finish(rs, ys_s, p, after=y_prompt[0, 0, 0])
    return (y_prompt, y_sample)
```

```python
import functools

import numpy as np
import jax
import jax.numpy as jnp
from jax import lax
from jax.experimental import pallas as pl
from jax.experimental.pallas import tpu as pltpu
from jax.experimental.pallas import tpu_sc as plsc

D_MODEL = 1024
HEAD_DIM = 64
N_HEADS_A = 8
N_KV_HEADS_A = 2
WINDOW = 128
N_HEADS_B = 8
GRID_W = 64
NA_ROWS = 8
NA_COLS = 16
N_GROUPS = 4
EXPERTS_PER_GROUP = 8
N_EXPERTS = N_GROUPS * EXPERTS_PER_GROUP
TOP_K = 2
D_EXPERT = D_MODEL // 2
LN_EPS = 1e-5
DEPTH = 1
DEEPNORM_ALPHA = (2.0 * DEPTH) ** 0.25
WIDTH_A = N_HEADS_A * HEAD_DIM
KV_WIDTH_A = N_KV_HEADS_A * HEAD_DIM
WIDTH_B = N_HEADS_B * HEAD_DIM
QKV_WIDTH = WIDTH_A + 2 * KV_WIDTH_A + 3 * WIDTH_B

LANES = 128
VMEM_LIMIT_BYTES = 56 * 1024 * 1024

NEG_BIG = -1e30
LOG2E = float(np.log2(np.e))

QA_COL, QB_COL, KB_COL, VB_COL = 0, WIDTH_A, WIDTH_A + WIDTH_B, WIDTH_A + 2 * WIDTH_B
KA_COL = WIDTH_A + 3 * WIDTH_B
VA_COL = KA_COL + KV_WIDTH_A

GQA_GROUP = N_HEADS_A // N_KV_HEADS_A


def _cparams(*sem):
    return pltpu.CompilerParams(dimension_semantics=sem, vmem_limit_bytes=VMEM_LIMIT_BYTES)


def _layer_norm(x, g, b):
    mu = jnp.mean(x, axis=-1, keepdims=True)
    xc = x - mu
    var = jnp.mean(xc * xc, axis=-1, keepdims=True)
    return xc * lax.rsqrt(var + LN_EPS) * g + b


PACKED_WIDTH = D_MODEL // 2


def _pack_rows(x):
    def bits(v):
        return lax.bitcast_convert_type(v.astype(jnp.bfloat16).astype(jnp.float32), jnp.uint32)
    return bits(x[:, :PACKED_WIDTH]) | (bits(x[:, PACKED_WIDTH:]) >> 16)


def _unpack_rows(w):
    hi = lax.bitcast_convert_type(w & jnp.uint32(0xFFFF0000), jnp.float32)
    lo = lax.bitcast_convert_type(w << 16, jnp.float32)
    return jnp.concatenate([hi, lo], axis=1)


def _qkv_kernel(x_ref, g_ref, b_ref, w_ref, *rest):
    o_ref = rest[len(rest) // 2]
    h = _layer_norm(x_ref[...], g_ref[...], b_ref[...])
    y = jnp.dot(h.astype(jnp.bfloat16), w_ref[...], preferred_element_type=jnp.float32)
    col = lax.broadcasted_iota(jnp.int32, (1, QKV_WIDTH), 1)
    y = y * jnp.where(col < KB_COL, HEAD_DIM ** -0.5 * LOG2E, 1.0)
    o_ref[...] = y.astype(jnp.bfloat16)
    ncast = len(rest) // 2
    for src_ref, dst_ref in zip(rest[:ncast], rest[ncast + 1:]):
        dst_ref[...] = src_ref[...].astype(jnp.bfloat16)


def _qkv(x2, ln_g, ln_b, w_qkv, tm, cast=()):
    n = x2.shape[0]
    steps = n // tm
    per_step = -(-N_EXPERTS // steps)
    assert all(c.shape[0] == N_EXPERTS for c in cast) and (steps * per_step) % N_EXPERTS == 0
    revisit = steps * per_step // N_EXPERTS

    def expert_block(c):
        return pl.BlockSpec((per_step,) + c.shape[1:], lambda i: (i // revisit, 0, 0))

    outs = pl.pallas_call(
        _qkv_kernel,
        out_shape=[jax.ShapeDtypeStruct((n, QKV_WIDTH), jnp.bfloat16)]
        + [jax.ShapeDtypeStruct(c.shape, jnp.bfloat16) for c in cast],
        grid=(steps,),
        in_specs=[
            pl.BlockSpec((tm, D_MODEL), lambda i: (i, 0)),
            pl.BlockSpec((1, D_MODEL), lambda i: (0, 0)),
            pl.BlockSpec((1, D_MODEL), lambda i: (0, 0)),
            pl.BlockSpec((D_MODEL, QKV_WIDTH), lambda i: (0, 0)),
        ] + [expert_block(c) for c in cast],
        out_specs=[pl.BlockSpec((tm, QKV_WIDTH), lambda i: (i, 0))] + [expert_block(c) for c in cast],
        compiler_params=_cparams("arbitrary" if cast else "parallel"),
        name="qkv",
    )(x2, ln_g, ln_b, w_qkv, *cast)
    return outs[0], tuple(outs[1:])


WIN_BLK = 128
WIN_LOOKAHEAD = 2


def _win_bias_table():
    qi = np.arange(WIN_BLK)[:, None]
    kj = np.arange(3 * WIN_BLK)[None, :]
    dist = np.abs(kj - WIN_BLK - qi).astype(np.float64)
    slopes = 2.0 ** (-8.0 * np.arange(1, N_HEADS_A + 1) / N_HEADS_A)
    per_head = np.where(dist <= WINDOW, -slopes[:, None, None] * dist[None] * LOG2E, NEG_BIG)
    groups = [np.concatenate([per_head[j], per_head[j + 4]], axis=0) for j in range(4)]
    return np.stack(groups).astype(np.float32)


def _win_kernel(sink_ref, q_ref, kp_ref, km_ref, kn_ref, vp_ref, vm_ref, vn_ref, bias_ref, o_ref,
                *, nsub, nblk_seq):
    i = pl.program_id(1)
    kcat = jnp.concatenate([kp_ref[...], km_ref[...], kn_ref[...]], axis=0)
    vcat = jnp.concatenate([vp_ref[...], vm_ref[...], vn_ref[...]], axis=0)
    lo = lax.broadcasted_iota(jnp.int32, (1, LANES), 1) < HEAD_DIM
    col = lax.broadcasted_iota(jnp.int32, (1, 3 * WIN_BLK), 1)
    top = lax.broadcasted_iota(jnp.int32, (2 * WIN_BLK, 1), 0) < WIN_BLK
    zero = jnp.zeros((), jnp.bfloat16)

    def scores(j, g):
        n = i * nsub + j
        off_seq = ((col < WIN_BLK) & (n == 0)) | ((col >= 2 * WIN_BLK) & (n == nblk_seq - 1))
        edge = jnp.where(off_seq, NEG_BIG, 0.0)
        qg = q_ref[WIN_BLK * j:WIN_BLK * (j + 1), LANES * g:LANES * (g + 1)]
        qm = jnp.concatenate([jnp.where(lo, qg, zero), jnp.where(lo, zero, qg)], axis=0)
        kj = kcat[WIN_BLK * j:WIN_BLK * (j + 3)]
        s = lax.dot_general(qm, kj, (((1,), (1,)), ((), ())), preferred_element_type=jnp.float32)
        return s + bias_ref[g] + edge

    def attend(s, j, g):
        vj = vcat[WIN_BLK * j:WIN_BLK * (j + 3)]
        sink = jnp.where(top, sink_ref[g], sink_ref[g + 4]) * LOG2E
        m = jnp.maximum(jnp.max(s, axis=-1, keepdims=True), sink)
        p = jnp.exp2(s - m)
        l = jnp.sum(p, axis=-1, keepdims=True) + jnp.exp2(sink - m)
        o2 = jnp.dot(p.astype(jnp.bfloat16), vj, preferred_element_type=jnp.float32)
        o2 = o2 * (1.0 / l)
        o_ref[WIN_BLK * j:WIN_BLK * (j + 1), LANES * g:LANES * (g + 1)] = (
            jnp.where(lo, o2[:WIN_BLK], o2[WIN_BLK:]).astype(jnp.bfloat16))

    chains = [(j, g) for j in range(nsub) for g in range(4)]
    pending = [scores(*c) for c in chains[:WIN_LOOKAHEAD]]
    for idx, c in enumerate(chains):
        s = pending.pop(0)
        if idx + WIN_LOOKAHEAD < len(chains):
            pending.append(scores(*chains[idx + WIN_LOOKAHEAD]))
        attend(s, *c)


def _win_attention(qkv, sink, bsz, t, tq):
    n = bsz * t
    nsub = tq // WIN_BLK
    nblk_seq = t // WIN_BLK
    ntile = t // tq
    bias = jnp.asarray(_win_bias_table())

    def main_map(col):
        return lambda b, i, *_: (b * ntile + i, col)

    def prev_map(col):
        return lambda b, i, *_: (b * nblk_seq + jnp.maximum(i * nsub - 1, 0), col)

    def next_map(col):
        return lambda b, i, *_: (b * nblk_seq + jnp.minimum(i * nsub + nsub, nblk_seq - 1), col)

    halo = (WIN_BLK, LANES)
    ka, va = KA_COL // LANES, VA_COL // LANES
    grid_spec = pltpu.PrefetchScalarGridSpec(
        num_scalar_prefetch=1,
        grid=(bsz, ntile),
        in_specs=[
            pl.BlockSpec((tq, WIDTH_A), main_map(QA_COL // WIDTH_A)),
            pl.BlockSpec(halo, prev_map(ka)),
            pl.BlockSpec((tq, LANES), main_map(ka)),
            pl.BlockSpec(halo, next_map(ka)),
            pl.BlockSpec(halo, prev_map(va)),
            pl.BlockSpec((tq, LANES), main_map(va)),
            pl.BlockSpec(halo, next_map(va)),
            pl.BlockSpec((4, 2 * WIN_BLK, 3 * WIN_BLK), lambda b, i, *_: (0, 0, 0)),
        ],
        out_specs=pl.BlockSpec((tq, WIDTH_A), main_map(0)),
    )
    return pl.pallas_call(
        functools.partial(_win_kernel, nsub=nsub, nblk_seq=nblk_seq),
        out_shape=jax.ShapeDtypeStruct((n, WIDTH_A), jnp.bfloat16),
        grid_spec=grid_spec,
        compiler_params=_cparams("parallel", "parallel"),
        name="win",
    )(sink, qkv, qkv, qkv, qkv, qkv, qkv, qkv, bias)


NAT_ROWS_PER_STEP = 16
NAT_HALO_ROWS = NA_ROWS // 2
NAT_KEYS = NA_ROWS * GRID_W
NAT_ROWS_PER_TRIP = 8
NAT_HEADS_PER_CHAIN = 2
NAT_LOOKAHEAD = 4


def _nat_bias_table(rpb):
    c = np.arange(GRID_W)
    cs = np.clip(c - NA_COLS // 2, 0, GRID_W - NA_COLS)
    col_mask = (c[None, :] >= cs[:, None]) & (c[None, :] < cs[:, None] + NA_COLS)
    dc = np.clip(c[None, :] - c[:, None] + (NA_COLS - 1), 0, 2 * NA_COLS - 2)
    onehot = jnp.asarray(dc[None] == np.arange(2 * NA_COLS - 1)[:, None, None], jnp.float32)
    picked = jnp.einsum("hdj,jqc->hqdc", rpb, onehot, precision=lax.Precision.HIGHEST)
    t1 = jnp.where(col_mask[None, :, None, :], picked * LOG2E, NEG_BIG)
    flat = t1.reshape(N_HEADS_B, GRID_W, (2 * NA_ROWS - 1) * GRID_W)
    shifts = jnp.stack([flat[:, :, sh * GRID_W:sh * GRID_W + NAT_KEYS] for sh in range(NA_ROWS)], axis=1)
    tb = shifts.reshape(N_HEADS_B // 2, 2, NA_ROWS, GRID_W, NAT_KEYS).transpose(0, 2, 1, 3, 4)
    return tb.reshape(N_HEADS_B // 2, NA_ROWS, 2 * GRID_W, NAT_KEYS)


def _nat_kernel(q_ref, kp_ref, km_ref, kn_ref, vp_ref, vm_ref, vn_ref, tb_ref, o_ref, kcat, vcat,
                *, rows_seq):
    i = pl.program_id(1)
    halo = NAT_HALO_ROWS * GRID_W
    main = NAT_ROWS_PER_STEP * GRID_W
    kcat[0:halo] = kp_ref[...]
    kcat[halo:halo + main] = km_ref[...]
    kcat[halo + main:2 * halo + main] = kn_ref[...]
    vcat[0:halo] = vp_ref[...]
    vcat[halo:halo + main] = vm_ref[...]
    vcat[halo + main:2 * halo + main] = vn_ref[...]
    width = NAT_HEADS_PER_CHAIN * HEAD_DIM
    head_of_lane = lax.broadcasted_iota(jnp.int32, (1, width), 1) // HEAD_DIM
    zero = jnp.zeros((), jnp.bfloat16)
    r0 = i * NAT_ROWS_PER_STEP

    def scores(qr, c):
        r = r0 + qr
        rs = jnp.clip(r - NA_ROWS // 2, 0, rows_seq - NA_ROWS)
        koff = pl.multiple_of((rs - r0 + NAT_HALO_ROWS) * GRID_W, GRID_W)
        sh = rs - r + (NA_ROWS - 1)
        qoff = pl.multiple_of(qr * GRID_W, GRID_W)
        cols = slice(width * c, width * (c + 1))
        qc = q_ref[pl.ds(qoff, GRID_W), cols]
        qm = jnp.concatenate([jnp.where(head_of_lane == h, qc, zero) for h in range(NAT_HEADS_PER_CHAIN)], axis=0)
        kw = kcat[pl.ds(koff, NAT_KEYS), cols]
        s = lax.dot_general(qm, kw, (((1,), (1,)), ((), ())), preferred_element_type=jnp.float32)
        pairs = NAT_HEADS_PER_CHAIN // 2
        bias = jnp.concatenate([tb_ref[pairs * c + k, sh] for k in range(pairs)], axis=0)
        return s + bias, koff, qoff

    def attend(s, koff, qoff, c):
        cols = slice(width * c, width * (c + 1))
        vw = vcat[pl.ds(koff, NAT_KEYS), cols]
        m = jnp.max(s, axis=-1, keepdims=True)
        pe = jnp.exp2(s - m)
        l = jnp.sum(pe, axis=-1, keepdims=True)
        o2 = jnp.dot(pe.astype(jnp.bfloat16), vw, preferred_element_type=jnp.float32)
        o2 = o2 * (1.0 / l)
        out = o2[:GRID_W]
        for h in range(1, NAT_HEADS_PER_CHAIN):
            out = jnp.where(head_of_lane == h, o2[GRID_W * h:GRID_W * (h + 1)], out)
        o_ref[pl.ds(qoff, GRID_W), cols] = out.astype(jnp.bfloat16)

    def trip(j, carry):
        chains = [(j * NAT_ROWS_PER_TRIP + q, c) for q in range(NAT_ROWS_PER_TRIP)
                  for c in range(N_HEADS_B // NAT_HEADS_PER_CHAIN)]
        pending = [scores(*c) for c in chains[:NAT_LOOKAHEAD]]
        for idx, (_, p) in enumerate(chains):
            s, koff, qoff = pending.pop(0)
            if idx + NAT_LOOKAHEAD < len(chains):
                pending.append(scores(*chains[idx + NAT_LOOKAHEAD]))
            attend(s, koff, qoff, p)
        return carry

    lax.fori_loop(0, NAT_ROWS_PER_STEP // NAT_ROWS_PER_TRIP, trip, 0)


def _nat_attention(qkv, tb, bsz, t):
    n = bsz * t
    rows_seq = t // GRID_W
    main = NAT_ROWS_PER_STEP * GRID_W
    halo = NAT_HALO_ROWS * GRID_W
    ntile = t // main
    nhalo_seq = t // halo
    per = main // halo

    def main_map(col):
        return lambda b, i: (b * ntile + i, col)

    def prev_map(col):
        return lambda b, i: (b * nhalo_seq + jnp.maximum(i * per - 1, 0), col)

    def next_map(col):
        return lambda b, i: (b * nhalo_seq + jnp.minimum(i * per + per, nhalo_seq - 1), col)

    qb, kb, vb = QB_COL // WIDTH_B, KB_COL // WIDTH_B, VB_COL // WIDTH_B
    return pl.pallas_call(
        functools.partial(_nat_kernel, rows_seq=rows_seq),
        out_shape=jax.ShapeDtypeStruct((n, WIDTH_B), jnp.bfloat16),
        grid=(bsz, ntile),
        in_specs=[
            pl.BlockSpec((main, WIDTH_B), main_map(qb)),
            pl.BlockSpec((halo, WIDTH_B), prev_map(kb)),
            pl.BlockSpec((main, WIDTH_B), main_map(kb)),
            pl.BlockSpec((halo, WIDTH_B), next_map(kb)),
            pl.BlockSpec((halo, WIDTH_B), prev_map(vb)),
            pl.BlockSpec((main, WIDTH_B), main_map(vb)),
            pl.BlockSpec((halo, WIDTH_B), next_map(vb)),
            pl.BlockSpec((N_HEADS_B // 2, NA_ROWS, 2 * GRID_W, NAT_KEYS), lambda b, i: (0, 0, 0, 0)),
        ],
        out_specs=pl.BlockSpec((main, WIDTH_B), main_map(0)),
        scratch_shapes=[pltpu.VMEM((main + 2 * halo, WIDTH_B), jnp.bfloat16),
                        pltpu.VMEM((main + 2 * halo, WIDTH_B), jnp.bfloat16)],
        compiler_params=_cparams("parallel", "parallel"),
        name="nat",
    )(qkv, qkv, qkv, qkv, qkv, qkv, qkv, tb)


EXPERT_ROW0 = N_GROUPS
ROUTE_ROWS = 48
INFO_E1, INFO_E2, INFO_R1, INFO_R2, INFO_W1, INFO_W2 = range(6)
INFO_ROWS = 8
MERGE_SUBTILES = 4


def _route(lt, carry, tri):
    rr, tm = lt.shape
    row = lax.broadcasted_iota(jnp.int32, (rr, tm), 0).astype(jnp.float32)
    none = jnp.float32(rr)

    def first_max(sel):
        m = jnp.max(jnp.where(sel, lt, NEG_BIG), axis=0, keepdims=True)
        idx = jnp.min(jnp.where(sel & (lt == m), row, none), axis=0, keepdims=True)
        return m, idx

    is_group = row < N_GROUPS
    mg, g = first_max(is_group)
    pg_sel = 1.0 / jnp.sum(jnp.where(is_group, jnp.exp(jnp.where(is_group, lt, mg) - mg), 0.0),
                           axis=0, keepdims=True)
    row0 = EXPERT_ROW0 + EXPERTS_PER_GROUP * g
    in_group = (row >= row0) & (row < row0 + EXPERTS_PER_GROUP)
    m1, i1 = first_max(in_group)
    m2, i2 = first_max(in_group & (row != i1))
    e2 = jnp.exp(m2 - m1)
    w1 = pg_sel / (1.0 + e2)
    w2 = pg_sel * e2 / (1.0 + e2)

    oh1 = row == i1
    oh2 = row == i2
    both = (oh1 | oh2).astype(jnp.bfloat16)
    before = jnp.dot(both, tri, preferred_element_type=jnp.float32) + carry
    r1 = jnp.sum(jnp.where(oh1, before, 0.0), axis=0, keepdims=True)
    r2 = jnp.sum(jnp.where(oh2, before, 0.0), axis=0, keepdims=True)
    new_carry = carry + jnp.sum(both.astype(jnp.float32), axis=1, keepdims=True)

    field = lax.broadcasted_iota(jnp.int32, (INFO_ROWS, tm), 0)
    info = jnp.zeros((INFO_ROWS, tm), jnp.float32)
    for k, v in ((INFO_E1, i1 - EXPERT_ROW0), (INFO_E2, i2 - EXPERT_ROW0), (INFO_R1, r1), (INFO_R2, r2),
                 (INFO_W1, w1), (INFO_W2, w2)):
        info = jnp.where(field == k, v, info)
    return info, new_carry


def _merge_kernel(x_ref, oa_ref, ob_ref, lng_ref, lnb_ref, wg_ref, wpa_ref, wpb_ref, wo_ref,
                  l1g_ref, l1b_ref, wr_ref, br_ref, cnt0_ref,
                  h1_ref, h1p_ref, info_ref, cnt_ref, carry_ref, tri_ref):
    tm = x_ref.shape[0]

    @pl.when(pl.program_id(0) == 0)
    def _():
        carry_ref[...] = cnt0_ref[...]
        r = lax.broadcasted_iota(jnp.int32, (tm, tm), 0)
        c = lax.broadcasted_iota(jnp.int32, (tm, tm), 1)
        tri_ref[...] = (r < c).astype(jnp.bfloat16)

    def project(rows):
        h = _layer_norm(x_ref[rows], lng_ref[...], lnb_ref[...])
        gates = jnp.dot(h.astype(jnp.bfloat16), wg_ref[...], preferred_element_type=jnp.float32)
        pa = jnp.dot(oa_ref[rows], wpa_ref[...], preferred_element_type=jnp.float32)
        pb = jnp.dot(ob_ref[rows], wpb_ref[...], preferred_element_type=jnp.float32)
        return h, gates, pa, pb

    def mix(h, gates, pa, pb):
        mixin = jax.nn.sigmoid(gates[:, :D_MODEL]) * pa + jax.nn.sigmoid(gates[:, D_MODEL:]) * pb
        return DEEPNORM_ALPHA * h + jnp.dot(mixin.astype(jnp.bfloat16), wo_ref[...],
                                            preferred_element_type=jnp.float32)

    def norm_and_logits(pre, rows):
        h1 = _layer_norm(pre, l1g_ref[...], l1b_ref[...])
        h1_ref[rows] = h1
        h1p_ref[rows] = _pack_rows(h1)
        return lax.dot_general(wr_ref[...], h1.astype(jnp.bfloat16), (((1,), (1,)), ((), ())),
                               preferred_element_type=jnp.float32)

    sub = tm // MERGE_SUBTILES
    parts = [slice(k * sub, (k + 1) * sub) for k in range(MERGE_SUBTILES)]
    projected = [project(rows) for rows in parts]
    mixed = [mix(*pr) for pr in projected]
    logits_t = jnp.concatenate([norm_and_logits(pre, rows) for pre, rows in zip(mixed, parts)], axis=1)
    logits_t = logits_t + br_ref[...]
    info, carry = _route(logits_t, carry_ref[...], tri_ref[...])
    info_ref[...] = info
    carry_ref[...] = carry
    cnt_ref[...] = carry[:, :LANES]


def _merge(x2, oa, ob, ln_g, ln_b, w_gates, w_pa, w_pb, w_o, l1g, l1b, w_r, b_r, cnt0, tm):
    n = x2.shape[0]

    def const(shape):
        return pl.BlockSpec(shape, lambda i: (0,) * len(shape))

    def rows(width):
        return pl.BlockSpec((tm, width), lambda i: (i, 0))

    return pl.pallas_call(
        _merge_kernel,
        out_shape=(jax.ShapeDtypeStruct((n, D_MODEL), jnp.float32),
                   jax.ShapeDtypeStruct((n, PACKED_WIDTH), jnp.uint32),
                   jax.ShapeDtypeStruct((INFO_ROWS, n), jnp.float32),
                   jax.ShapeDtypeStruct((ROUTE_ROWS, LANES), jnp.float32)),
        grid=(n // tm,),
        in_specs=[
            rows(D_MODEL), rows(WIDTH_A), rows(WIDTH_B),
            const((1, D_MODEL)), const((1, D_MODEL)),
            const((D_MODEL, 2 * D_MODEL)),
            const((WIDTH_A, D_MODEL)), const((WIDTH_B, D_MODEL)),
            const((D_MODEL, D_MODEL)),
            const((1, D_MODEL)), const((1, D_MODEL)),
            const((ROUTE_ROWS, D_MODEL)), const((ROUTE_ROWS, tm)), const((ROUTE_ROWS, tm)),
        ],
        out_specs=(rows(D_MODEL), rows(PACKED_WIDTH), pl.BlockSpec((INFO_ROWS, tm), lambda i: (0, i)),
                   const((ROUTE_ROWS, LANES))),
        scratch_shapes=[pltpu.VMEM((ROUTE_ROWS, tm), jnp.float32), pltpu.VMEM((tm, tm), jnp.bfloat16)],
        compiler_params=_cparams("arbitrary"),
        name="merge",
    )(x2, oa, ob, ln_g, ln_b, w_gates, w_pa, w_pb, w_o, l1g, l1b, w_r, b_r, cnt0)


SC_CORES = 2
SC_SUBCORES = 16
SC_WORKERS = SC_CORES * SC_SUBCORES
SC_ROWS_PER_STREAM = 64


def _sc_worker():
    return lax.axis_index("s") * SC_CORES + lax.axis_index("c")


def _scatter_indices(dest_t, pad_rows):
    n = dest_t.shape[1]
    nchunks = n // (SC_WORKERS * SC_ROWS_PER_STREAM)
    npad = pad_rows.size // (SC_WORKERS * SC_ROWS_PER_STREAM)
    assert nchunks * SC_ROWS_PER_STREAM * SC_WORKERS == n and nchunks % 2 == 0
    assert npad * SC_WORKERS * SC_ROWS_PER_STREAM == pad_rows.size
    return (dest_t.reshape(TOP_K, SC_WORKERS, nchunks, SC_ROWS_PER_STREAM),
            pad_rows.reshape(SC_WORKERS, npad, SC_ROWS_PER_STREAM))


def _dispatch(h1p, idx, pad_idx, zeros, nrows_out):
    n, width = h1p.shape
    per_worker = n // SC_WORKERS
    nchunks, npad = idx.shape[2], pad_idx.shape[1]
    assert zeros.shape == (SC_ROWS_PER_STREAM, width) and zeros.dtype == h1p.dtype
    mesh = plsc.VectorSubcoreMesh(core_axis_name="c", subcore_axis_name="s")

    @functools.partial(
        pl.kernel, out_type=jax.ShapeDtypeStruct((nrows_out, width), h1p.dtype), mesh=mesh,
        scratch_types=[pltpu.VMEM((TOP_K, nchunks, SC_ROWS_PER_STREAM), jnp.int32),
                       pltpu.VMEM((npad, SC_ROWS_PER_STREAM), jnp.int32),
                       pltpu.VMEM((2, SC_ROWS_PER_STREAM, width), h1p.dtype),
                       pltpu.SemaphoreType.DMA((2,)), pltpu.SemaphoreType.DMA((2,))],
        name="sc_dispatch")
    def scatter_kernel(src_hbm, idx_hbm, pad_hbm, zeros_hbm, out_hbm, idx_v, pad_v, rows_v, rsem, ssem):
        wid = _sc_worker()
        base = wid * per_worker
        for k in range(TOP_K):
            pltpu.sync_copy(idx_hbm.at[k, wid], idx_v.at[k])
        pltpu.sync_copy(pad_hbm.at[wid], pad_v)

        def read(j, slot):
            src = src_hbm.at[pl.ds(base + j * SC_ROWS_PER_STREAM, SC_ROWS_PER_STREAM)]
            return pltpu.make_async_copy(src, rows_v.at[slot], rsem.at[slot])

        def scatter(j, slot, k):
            return pltpu.make_async_copy(rows_v.at[slot], out_hbm.at[idx_v.at[k, j]], ssem.at[slot])

        read(0, 0).start()

        @pl.loop(0, nchunks, step=2)
        def _(j0):
            for slot in range(2):
                j = j0 + slot
                read(j, slot).wait()

                @pl.when(j + 1 < nchunks)
                def _():
                    @pl.when(j >= 1)
                    def _():
                        for k in range(TOP_K):
                            scatter(j - 1, 1 - slot, k).wait()
                    read(j + 1, 1 - slot).start()

                for k in range(TOP_K):
                    scatter(j, slot, k).start()

        for k in range(TOP_K):
            scatter(nchunks - 2, 0, k).wait()
            scatter(nchunks - 1, 1, k).wait()

        pltpu.sync_copy(zeros_hbm, rows_v.at[0])
        fills = [pltpu.make_async_copy(rows_v.at[0], out_hbm.at[pad_v.at[c]], ssem.at[0]) for c in range(npad)]
        for f in fills:
            f.start()
        for f in fills:
            f.wait()

    return scatter_kernel(h1p, idx, pad_idx, zeros)


def _sorted_layout(counts, eid, rank, bm):
    n = eid.shape[1]
    nblocks = TOP_K * n // bm + N_EXPERTS
    expert = jnp.arange(N_EXPERTS, dtype=jnp.int32)
    before = expert[None, :] < expert[:, None]
    blocks = (counts + bm - 1) // bm
    first_blk = jnp.sum(jnp.where(before, blocks[None, :], 0), axis=1)
    starts = first_blk * bm
    dest = rank + jnp.sum(jnp.where(eid[None] == expert[:, None, None], starts[:, None, None], 0), axis=0)
    total = jnp.sum(blocks)
    j = jnp.arange(bm, dtype=jnp.int32)[None, :]
    npad = blocks * bm - counts
    spare_before = jnp.sum(jnp.where(before, (bm - npad)[None, :], 0), axis=1)
    pad_rows = jnp.where(j < npad[:, None], (starts + counts)[:, None] + j,
                         (total * bm + spare_before - npad)[:, None] + j).astype(jnp.int32)
    w = jnp.arange(nblocks, dtype=jnp.int32)
    src = jnp.minimum(w, total - 1)
    blk_end = first_blk + blocks
    owner = jnp.minimum(jnp.sum((blk_end[None, :] <= src[:, None]).astype(jnp.int32), axis=1), N_EXPERTS - 1)
    valid = (w < total).astype(jnp.int32)
    return dest.astype(jnp.int32), pad_rows, (src.astype(jnp.int32), owner.astype(jnp.int32), valid)


def _expert_kernel(src_ref, e_ref, valid_ref, x_ref, wg_ref, wu_ref, wd_ref, o_ref):
    w = pl.program_id(0)

    @pl.when(valid_ref[w] != 0)
    def _():
        x = _unpack_rows(x_ref[...]).astype(jnp.bfloat16)
        g = jnp.dot(x, wg_ref[0], preferred_element_type=jnp.float32)
        u = jnp.dot(x, wu_ref[0], preferred_element_type=jnp.float32)
        hmid = (jax.nn.silu(g) * u).astype(jnp.bfloat16)
        o_ref[...] = _pack_rows(jnp.dot(hmid, wd_ref[0], preferred_element_type=jnp.float32))

    @pl.when(valid_ref[w] == 0)
    def _():
        o_ref[...] = jnp.zeros(o_ref.shape, o_ref.dtype)


def _experts(xs, plan, w_gate, w_up, w_down, bm):
    nblocks = plan[0].shape[0]
    assert xs.shape[0] == nblocks * bm

    def weights(shape):
        return pl.BlockSpec((1,) + shape, lambda w, src, e, *_: (e[w], 0, 0))

    grid_spec = pltpu.PrefetchScalarGridSpec(
        num_scalar_prefetch=len(plan),
        grid=(nblocks,),
        in_specs=[
            pl.BlockSpec((bm, PACKED_WIDTH), lambda w, src, *_: (src[w], 0)),
            weights((D_MODEL, D_EXPERT)), weights((D_MODEL, D_EXPERT)), weights((D_EXPERT, D_MODEL)),
        ],
        out_specs=pl.BlockSpec((bm, PACKED_WIDTH), lambda w, *_: (w, 0)),
    )
    return pl.pallas_call(
        _expert_kernel,
        out_shape=jax.ShapeDtypeStruct(xs.shape, jnp.uint32),
        grid_spec=grid_spec,
        compiler_params=_cparams("parallel"),
        name="experts",
    )(*plan, xs, w_gate, w_up, w_down)


def _sc_gather_rows(table, idx):
    nrows = idx.shape[0]
    width = table.shape[1]
    per_worker = nrows // SC_WORKERS
    nchunks = per_worker // SC_ROWS_PER_STREAM
    assert nchunks * SC_ROWS_PER_STREAM * SC_WORKERS == nrows and nchunks % 2 == 0
    mesh = plsc.VectorSubcoreMesh(core_axis_name="c", subcore_axis_name="s")

    @functools.partial(
        pl.kernel, out_type=jax.ShapeDtypeStruct((nrows, width), table.dtype), mesh=mesh,
        scratch_types=[pltpu.VMEM((per_worker,), jnp.int32),
                       pltpu.VMEM((2, SC_ROWS_PER_STREAM, width), table.dtype),
                       pltpu.SemaphoreType.DMA((2,)), pltpu.SemaphoreType.DMA((2,))],
        name="sc_gather")
    def gather_kernel(table_hbm, idx_hbm, out_hbm, idx_v, rows_v, gsem, wsem):
        base = _sc_worker() * per_worker
        pltpu.sync_copy(idx_hbm.at[pl.ds(base, per_worker)], idx_v)

        def gather(j, slot):
            rows = idx_v.at[pl.ds(j * SC_ROWS_PER_STREAM, SC_ROWS_PER_STREAM)]
            return pltpu.make_async_copy(table_hbm.at[rows], rows_v.at[slot], gsem.at[slot])

        def write(j, slot):
            dst = out_hbm.at[pl.ds(base + j * SC_ROWS_PER_STREAM, SC_ROWS_PER_STREAM)]
            return pltpu.make_async_copy(rows_v.at[slot], dst, wsem.at[slot])

        gather(0, 0).start()

        @pl.loop(0, nchunks, step=2)
        def _(j0):
            for slot in range(2):
                j = j0 + slot
                gather(j, slot).wait()

                @pl.when(j + 1 < nchunks)
                def _():
                    @pl.when(j >= 1)
                    def _():
                        write(j - 1, 1 - slot).wait()
                    gather(j + 1, 1 - slot).start()

                write(j, slot).start()

        write(nchunks - 2, 0).wait()
        write(nchunks - 1, 1).wait()

    return gather_kernel(table, idx)


def _finalize_kernel(h1_ref, y1_ref, y2_ref, info_ref, g_ref, b_ref, o_ref):
    tt = h1_ref.shape[0]
    pad = jnp.zeros((LANES - INFO_ROWS, tt), jnp.float32)
    info = jnp.concatenate([info_ref[...], pad], axis=0).T
    moe = (_unpack_rows(y1_ref[...]) * info[:, INFO_W1:INFO_W1 + 1]
           + _unpack_rows(y2_ref[...]) * info[:, INFO_W2:INFO_W2 + 1])
    o_ref[...] = _layer_norm(DEEPNORM_ALPHA * h1_ref[...] + moe, g_ref[...], b_ref[...])


def _combine(h1, info, dest_flat, ys, ln_g, ln_b, tt):
    n = h1.shape[0]
    nsteps = n // tt
    yg = _sc_gather_rows(ys, dest_flat)
    return pl.pallas_call(
        _finalize_kernel,
        out_shape=jax.ShapeDtypeStruct((n, D_MODEL), jnp.float32),
        grid=(nsteps,),
        in_specs=[
            pl.BlockSpec((tt, D_MODEL), lambda i: (i, 0)),
            pl.BlockSpec((tt, PACKED_WIDTH), lambda i: (i, 0)),
            pl.BlockSpec((tt, PACKED_WIDTH), lambda i: (nsteps + i, 0)),
            pl.BlockSpec((INFO_ROWS, tt), lambda i: (0, i)),
            pl.BlockSpec((1, D_MODEL), lambda i: (0, 0)),
            pl.BlockSpec((1, D_MODEL), lambda i: (0, 0)),
        ],
        out_specs=pl.BlockSpec((tt, D_MODEL), lambda i: (i, 0)),
        compiler_params=_cparams("parallel"),
        name="finalize",
    )(h1, yg, yg, info, ln_g, ln_b)


TM_QKV = 1024
TQ_WIN = 1024
TM_MERGE = 1024
TT_ROWS = 1024
BM_EXPERT = 512


def _prepare_weights(ln_in_g, ln_in_b, w_in, attn_sink, rel_pos_bias, w_proj_a, w_proj_b, w_out,
                     ln1_g, ln1_b, w_route_group, b_route_group, w_route_expert, b_route_expert,
                     ln2_g, ln2_b):
    bf = jnp.bfloat16
    w = w_in[0]
    splits = np.cumsum([WIDTH_A, KV_WIDTH_A, KV_WIDTH_A, WIDTH_B, WIDTH_B, WIDTH_B, D_MODEL])
    wqa, wka, wva, wqb, wkb, wvb, wga, wgb = jnp.split(w, [int(s) for s in splits], axis=1)
    wqa = (wqa.reshape(D_MODEL, N_KV_HEADS_A, GQA_GROUP, HEAD_DIM).transpose(0, 2, 1, 3)
           .reshape(D_MODEL, WIDTH_A))
    w_qkv = jnp.concatenate([wqa, wqb, wkb, wvb, wka, wva], axis=1).astype(bf)
    w_gates = jnp.concatenate([wga, wgb], axis=1).astype(bf)
    w_pa = (w_proj_a[0].reshape(N_KV_HEADS_A, GQA_GROUP, HEAD_DIM, D_MODEL).transpose(1, 0, 2, 3)
            .reshape(WIDTH_A, D_MODEL).astype(bf))
    w_pb = w_proj_b[0].astype(bf)
    w_o = w_out[0].astype(bf)
    pad = ROUTE_ROWS - N_GROUPS - N_EXPERTS
    w_r = jnp.concatenate([w_route_group[0].T, w_route_expert[0].T, jnp.zeros((pad, D_MODEL), jnp.float32)], axis=0)
    w_route = w_r.astype(bf)
    b_r = jnp.concatenate([b_route_group[0], b_route_expert[0], jnp.zeros((pad,), jnp.float32)])
    b_r = jnp.broadcast_to(b_r[:, None], (ROUTE_ROWS, TM_MERGE))
    row = lambda v: v.reshape(1, D_MODEL)
    return dict(
        ln_in_g=row(ln_in_g), ln_in_b=row(ln_in_b), w_qkv=w_qkv, w_gates=w_gates,
        sink=attn_sink[0].astype(jnp.float32), nat_bias=_nat_bias_table(rel_pos_bias[0]),
        w_pa=w_pa, w_pb=w_pb, w_o=w_o, ln1_g=row(ln1_g[0]), ln1_b=row(ln1_b[0]),
        w_route=w_route, b_r=b_r, ln2_g=row(ln2_g[0]), ln2_b=row(ln2_b[0]))


def _after(value, other):
    if other is None:
        return value
    other = other.astype(jnp.float32)
    zero = jnp.where(jnp.isfinite(other), other, 0.0) * 0.0
    return value + zero.astype(value.dtype)


def _attend_and_route(x, p, after=None, cast=()):
    bsz, t, _ = x.shape
    n = bsz * t
    x2 = x.reshape(n, D_MODEL)
    qkv, cast_bf16 = _qkv(x2, _after(p["ln_in_g"], after), p["ln_in_b"], p["w_qkv"], TM_QKV, cast)
    oa = _win_attention(qkv, p["sink"], bsz, t, TQ_WIN)
    ob = _nat_attention(qkv, p["nat_bias"], bsz, t)
    cnt0 = jnp.zeros((ROUTE_ROWS, TM_MERGE), jnp.float32)
    h1, h1p, info, cnt = _merge(x2, oa, ob, p["ln_in_g"], p["ln_in_b"], p["w_gates"], p["w_pa"], p["w_pb"],
                                p["w_o"], p["ln1_g"], p["ln1_b"], p["w_route"], p["b_r"], cnt0, TM_MERGE)
    counts = cnt[EXPERT_ROW0:EXPERT_ROW0 + N_EXPERTS, 0].astype(jnp.int32)
    eid = info[INFO_E1:INFO_E2 + 1].astype(jnp.int32)
    rank = info[INFO_R1:INFO_R2 + 1].astype(jnp.int32)
    dest_t, pad_rows, plan = _sorted_layout(counts, eid, rank, BM_EXPERT)
    zero_rows = jnp.zeros((SC_ROWS_PER_STREAM, PACKED_WIDTH), jnp.uint32)
    moves = lax.optimization_barrier(_scatter_indices(dest_t, pad_rows) + (zero_rows, dest_t.reshape(TOP_K * n)))
    return dict(shape=x.shape, h1=h1, h1p=h1p, info=info, counts=counts, moves=moves, plan=plan, cast=cast_bf16)


def _run_experts(r, expert_weights, after=None):
    src, owner, valid = r["plan"]
    scatter_idx, pad_idx, zero_rows, _ = r["moves"]
    xs = _dispatch(r["h1p"], scatter_idx, pad_idx, zero_rows, src.shape[0] * BM_EXPERT)
    return _experts(xs, (src, owner, _after(valid, after)), *expert_weights, BM_EXPERT)


def _finish(r, ys, p, after=None):
    out = _combine(r["h1"], r["info"], r["moves"][3], ys, _after(p["ln2_g"], after), p["ln2_b"], TT_ROWS)
    return out.reshape(r["shape"])


def kernel(x_prompt, x_sample, ln_in_g, ln_in_b, w_in, attn_sink, rel_pos_bias, w_proj_a, w_proj_b, w_out,
           ln1_g, ln1_b, w_route_group, b_route_group, w_route_expert, b_route_expert,
           w_gate, w_up, w_down, ln2_g, ln2_b):
    p = _prepare_weights(ln_in_g, ln_in_b, w_in, attn_sink, rel_pos_bias, w_proj_a, w_proj_b, w_out,
                         ln1_g, ln1_b, w_route_group, b_route_group, w_route_expert, b_route_expert,
                         ln2_g, ln2_b)
    rp = _attend_and_route(x_prompt, p, cast=(w_gate[0], w_up[0], w_down[0]))
    glue_done = sum(a.reshape(-1)[0].astype(jnp.float32) for a in rp["moves"] + rp["plan"])
    rs = _attend_and_route(x_sample, p, after=glue_done)
    ys_p = _run_experts(rp, rp["cast"])
    ys_s = _run_experts(rs, rp["cast"], after=ys_p[0, 0])
    y_prompt = _finish(rp, ys_p, p)
    y_sample = _finish(rs, ys_s, p, after=y_prompt[0, 0, 0])
    return (y_prompt, y_sample)
```

```python
import functools

import numpy as np
import jax
import jax.numpy as jnp
from jax import lax
from jax.experimental import pallas as pl
from jax.experimental.pallas import tpu as pltpu
from jax.experimental.pallas import tpu_sc as plsc

D_MODEL = 1024
HEAD_DIM = 64
N_HEADS_A = 8
N_KV_HEADS_A = 2
WINDOW = 128
N_HEADS_B = 8
GRID_W = 64
NA_ROWS = 8
NA_COLS = 16
N_GROUPS = 4
EXPERTS_PER_GROUP = 8
N_EXPERTS = N_GROUPS * EXPERTS_PER_GROUP
TOP_K = 2
D_EXPERT = D_MODEL // 2
LN_EPS = 1e-5
DEPTH = 1
DEEPNORM_ALPHA = (2.0 * DEPTH) ** 0.25
WIDTH_A = N_HEADS_A * HEAD_DIM
KV_WIDTH_A = N_KV_HEADS_A * HEAD_DIM
WIDTH_B = N_HEADS_B * HEAD_DIM
QKV_WIDTH = WIDTH_A + 2 * KV_WIDTH_A + 3 * WIDTH_B

LANES = 128
VMEM_LIMIT_BYTES = 56 * 1024 * 1024

NEG_BIG = -1e30
LOG2E = float(np.log2(np.e))

QA_COL, QB_COL, KB_COL, VB_COL = 0, WIDTH_A, WIDTH_A + WIDTH_B, WIDTH_A + 2 * WIDTH_B
KA_COL = WIDTH_A + 3 * WIDTH_B
VA_COL = KA_COL + KV_WIDTH_A

GQA_GROUP = N_HEADS_A // N_KV_HEADS_A


def _cparams(*sem):
    return pltpu.CompilerParams(dimension_semantics=sem, vmem_limit_bytes=VMEM_LIMIT_BYTES)


def _layer_norm(x, g, b):
    mu = jnp.mean(x, axis=-1, keepdims=True)
    xc = x - mu
    var = jnp.mean(xc * xc, axis=-1, keepdims=True)
    return xc * lax.rsqrt(var + LN_EPS) * g + b


PACKED_WIDTH = D_MODEL // 2


def _pack_rows(x):
    def bits(v):
        return lax.bitcast_convert_type(v.astype(jnp.bfloat16).astype(jnp.float32), jnp.uint32)
    return bits(x[:, :PACKED_WIDTH]) | (bits(x[:, PACKED_WIDTH:]) >> 16)


def _unpack_rows(w):
    hi = lax.bitcast_convert_type(w & jnp.uint32(0xFFFF0000), jnp.float32)
    lo = lax.bitcast_convert_type(w << 16, jnp.float32)
    return jnp.concatenate([hi, lo], axis=1)


def _qkv_kernel(x_ref, g_ref, b_ref, w_ref, *rest):
    o_ref = rest[len(rest) // 2]
    h = _layer_norm(x_ref[...], g_ref[...], b_ref[...])
    y = jnp.dot(h.astype(jnp.bfloat16), w_ref[...], preferred_element_type=jnp.float32)
    col = lax.broadcasted_iota(jnp.int32, (1, QKV_WIDTH), 1)
    y = y * jnp.where(col < KB_COL, HEAD_DIM ** -0.5 * LOG2E, 1.0)
    o_ref[...] = y.astype(jnp.bfloat16)
    ncast = len(rest) // 2
    for src_ref, dst_ref in zip(rest[:ncast], rest[ncast + 1:]):
        dst_ref[...] = src_ref[...].astype(jnp.bfloat16)


def _qkv(x2, ln_g, ln_b, w_qkv, tm, cast=()):
    n = x2.shape[0]
    steps = n // tm
    per_step = -(-N_EXPERTS // steps)
    assert all(c.shape[0] == N_EXPERTS for c in cast) and (steps * per_step) % N_EXPERTS == 0
    revisit = steps * per_step // N_EXPERTS

    def expert_block(c):
        return pl.BlockSpec((per_step,) + c.shape[1:], lambda i: (i // revisit, 0, 0))

    outs = pl.pallas_call(
        _qkv_kernel,
        out_shape=[jax.ShapeDtypeStruct((n, QKV_WIDTH), jnp.bfloat16)]
        + [jax.ShapeDtypeStruct(c.shape, jnp.bfloat16) for c in cast],
        grid=(steps,),
        in_specs=[
            pl.BlockSpec((tm, D_MODEL), lambda i: (i, 0)),
            pl.BlockSpec((1, D_MODEL), lambda i: (0, 0)),
            pl.BlockSpec((1, D_MODEL), lambda i: (0, 0)),
            pl.BlockSpec((D_MODEL, QKV_WIDTH), lambda i: (0, 0)),
        ] + [expert_block(c) for c in cast],
        out_specs=[pl.BlockSpec((tm, QKV_WIDTH), lambda i: (i, 0))] + [expert_block(c) for c in cast],
        compiler_params=_cparams("arbitrary" if cast else "parallel"),
        name="qkv",
    )(x2, ln_g, ln_b, w_qkv, *cast)
    return outs[0], tuple(outs[1:])


WIN_BLK = 128
WIN_LOOKAHEAD = 2


def _win_bias_table():
    qi = np.arange(WIN_BLK)[:, None]
    kj = np.arange(3 * WIN_BLK)[None, :]
    dist = np.abs(kj - WIN_BLK - qi).astype(np.float64)
    slopes = 2.0 ** (-8.0 * np.arange(1, N_HEADS_A + 1) / N_HEADS_A)
    per_head = np.where(dist <= WINDOW, -slopes[:, None, None] * dist[None] * LOG2E, NEG_BIG)
    groups = [np.concatenate([per_head[j], per_head[j + 4]], axis=0) for j in range(4)]
    return np.stack(groups).astype(np.float32)


def _win_kernel(sink_ref, q_ref, kp_ref, km_ref, kn_ref, vp_ref, vm_ref, vn_ref, bias_ref, o_ref,
                *, nsub, nblk_seq):
    i = pl.program_id(1)
    kcat = jnp.concatenate([kp_ref[...], km_ref[...], kn_ref[...]], axis=0)
    vcat = jnp.concatenate([vp_ref[...], vm_ref[...], vn_ref[...]], axis=0)
    lo = lax.broadcasted_iota(jnp.int32, (1, LANES), 1) < HEAD_DIM
    col = lax.broadcasted_iota(jnp.int32, (1, 3 * WIN_BLK), 1)
    top = lax.broadcasted_iota(jnp.int32, (2 * WIN_BLK, 1), 0) < WIN_BLK
    zero = jnp.zeros((), jnp.bfloat16)

    def scores(j, g):
        n = i * nsub + j
        off_seq = ((col < WIN_BLK) & (n == 0)) | ((col >= 2 * WIN_BLK) & (n == nblk_seq - 1))
        edge = jnp.where(off_seq, NEG_BIG, 0.0)
        qg = q_ref[WIN_BLK * j:WIN_BLK * (j + 1), LANES * g:LANES * (g + 1)]
        qm = jnp.concatenate([jnp.where(lo, qg, zero), jnp.where(lo, zero, qg)], axis=0)
        kj = kcat[WIN_BLK * j:WIN_BLK * (j + 3)]
        s = lax.dot_general(qm, kj, (((1,), (1,)), ((), ())), preferred_element_type=jnp.float32)
        return s + bias_ref[g] + edge

    def attend(s, j, g):
        vj = vcat[WIN_BLK * j:WIN_BLK * (j + 3)]
        sink = jnp.where(top, sink_ref[g], sink_ref[g + 4]) * LOG2E
        m = jnp.maximum(jnp.max(s, axis=-1, keepdims=True), sink)
        p = jnp.exp2(s - m)
        l = jnp.sum(p, axis=-1, keepdims=True) + jnp.exp2(sink - m)
        o2 = jnp.dot(p.astype(jnp.bfloat16), vj, preferred_element_type=jnp.float32)
        o2 = o2 * (1.0 / l)
        o_ref[WIN_BLK * j:WIN_BLK * (j + 1), LANES * g:LANES * (g + 1)] = (
            jnp.where(lo, o2[:WIN_BLK], o2[WIN_BLK:]).astype(jnp.bfloat16))

    chains = [(j, g) for j in range(nsub) for g in range(4)]
    pending = [scores(*c) for c in chains[:WIN_LOOKAHEAD]]
    for idx, c in enumerate(chains):
        s = pending.pop(0)
        if idx + WIN_LOOKAHEAD < len(chains):
            pending.append(scores(*chains[idx + WIN_LOOKAHEAD]))
        attend(s, *c)


def _win_attention(qkv, sink, bsz, t, tq):
    n = bsz * t
    nsub = tq // WIN_BLK
    nblk_seq = t // WIN_BLK
    ntile = t // tq
    bias = jnp.asarray(_win_bias_table())

    def main_map(col):
        return lambda b, i, *_: (b * ntile + i, col)

    def prev_map(col):
        return lambda b, i, *_: (b * nblk_seq + jnp.maximum(i * nsub - 1, 0), col)

    def next_map(col):
        return lambda b, i, *_: (b * nblk_seq + jnp.minimum(i * nsub + nsub, nblk_seq - 1), col)

    halo = (WIN_BLK, LANES)
    ka, va = KA_COL // LANES, VA_COL // LANES
    grid_spec = pltpu.PrefetchScalarGridSpec(
        num_scalar_prefetch=1,
        grid=(bsz, ntile),
        in_specs=[
            pl.BlockSpec((tq, WIDTH_A), main_map(QA_COL // WIDTH_A)),
            pl.BlockSpec(halo, prev_map(ka)),
            pl.BlockSpec((tq, LANES), main_map(ka)),
            pl.BlockSpec(halo, next_map(ka)),
            pl.BlockSpec(halo, prev_map(va)),
            pl.BlockSpec((tq, LANES), main_map(va)),
            pl.BlockSpec(halo, next_map(va)),
            pl.BlockSpec((4, 2 * WIN_BLK, 3 * WIN_BLK), lambda b, i, *_: (0, 0, 0)),
        ],
        out_specs=pl.BlockSpec((tq, WIDTH_A), main_map(0)),
    )
    return pl.pallas_call(
        functools.partial(_win_kernel, nsub=nsub, nblk_seq=nblk_seq),
        out_shape=jax.ShapeDtypeStruct((n, WIDTH_A), jnp.bfloat16),
        grid_spec=grid_spec,
        compiler_params=_cparams("parallel", "parallel"),
        name="win",
    )(sink, qkv, qkv, qkv, qkv, qkv, qkv, qkv, bias)


NAT_ROWS_PER_STEP = 16
NAT_HALO_ROWS = NA_ROWS // 2
NAT_KEYS = NA_ROWS * GRID_W
NAT_ROWS_PER_TRIP = 8
NAT_HEADS_PER_CHAIN = 2
NAT_LOOKAHEAD = 4


def _nat_bias_table(rpb):
    c = np.arange(GRID_W)
    cs = np.clip(c - NA_COLS // 2, 0, GRID_W - NA_COLS)
    col_mask = (c[None, :] >= cs[:, None]) & (c[None, :] < cs[:, None] + NA_COLS)
    dc = np.clip(c[None, :] - c[:, None] + (NA_COLS - 1), 0, 2 * NA_COLS - 2)
    onehot = jnp.asarray(dc[None] == np.arange(2 * NA_COLS - 1)[:, None, None], jnp.float32)
    picked = jnp.einsum("hdj,jqc->hqdc", rpb, onehot, precision=lax.Precision.HIGHEST)
    t1 = jnp.where(col_mask[None, :, None, :], picked * LOG2E, NEG_BIG)
    flat = t1.reshape(N_HEADS_B, GRID_W, (2 * NA_ROWS - 1) * GRID_W)
    shifts = jnp.stack([flat[:, :, sh * GRID_W:sh * GRID_W + NAT_KEYS] for sh in range(NA_ROWS)], axis=1)
    tb = shifts.reshape(N_HEADS_B // 2, 2, NA_ROWS, GRID_W, NAT_KEYS).transpose(0, 2, 1, 3, 4)
    return tb.reshape(N_HEADS_B // 2, NA_ROWS, 2 * GRID_W, NAT_KEYS)


def _nat_kernel(q_ref, kp_ref, km_ref, kn_ref, vp_ref, vm_ref, vn_ref, tb_ref, o_ref, kcat, vcat,
                *, rows_seq):
    i = pl.program_id(1)
    halo = NAT_HALO_ROWS * GRID_W
    main = NAT_ROWS_PER_STEP * GRID_W
    kcat[0:halo] = kp_ref[...]
    kcat[halo:halo + main] = km_ref[...]
    kcat[halo + main:2 * halo + main] = kn_ref[...]
    vcat[0:halo] = vp_ref[...]
    vcat[halo:halo + main] = vm_ref[...]
    vcat[halo + main:2 * halo + main] = vn_ref[...]
    width = NAT_HEADS_PER_CHAIN * HEAD_DIM
    head_of_lane = lax.broadcasted_iota(jnp.int32, (1, width), 1) // HEAD_DIM
    zero = jnp.zeros((), jnp.bfloat16)
    r0 = i * NAT_ROWS_PER_STEP

    def scores(qr, c):
        r = r0 + qr
        rs = jnp.clip(r - NA_ROWS // 2, 0, rows_seq - NA_ROWS)
        koff = pl.multiple_of((rs - r0 + NAT_HALO_ROWS) * GRID_W, GRID_W)
        sh = rs - r + (NA_ROWS - 1)
        qoff = pl.multiple_of(qr * GRID_W, GRID_W)
        cols = slice(width * c, width * (c + 1))
        qc = q_ref[pl.ds(qoff, GRID_W), cols]
        qm = jnp.concatenate([jnp.where(head_of_lane == h, qc, zero) for h in range(NAT_HEADS_PER_CHAIN)], axis=0)
        kw = kcat[pl.ds(koff, NAT_KEYS), cols]
        s = lax.dot_general(qm, kw, (((1,), (1,)), ((), ())), preferred_element_type=jnp.float32)
        pairs = NAT_HEADS_PER_CHAIN // 2
        bias = jnp.concatenate([tb_ref[pairs * c + k, sh] for k in range(pairs)], axis=0)
        return s + bias, koff, qoff

    def attend(s, koff, qoff, c):
        cols = slice(width * c, width * (c + 1))
        vw = vcat[pl.ds(koff, NAT_KEYS), cols]
        m = jnp.max(s, axis=-1, keepdims=True)
        pe = jnp.exp2(s - m)
        l = jnp.sum(pe, axis=-1, keepdims=True)
        o2 = jnp.dot(pe.astype(jnp.bfloat16), vw, preferred_element_type=jnp.float32)
        o2 = o2 * (1.0 / l)
        out = o2[:GRID_W]
        for h in range(1, NAT_HEADS_PER_CHAIN):
            out = jnp.where(head_of_lane == h, o2[GRID_W * h:GRID_W * (h + 1)], out)
        o_ref[pl.ds(qoff, GRID_W), cols] = out.astype(jnp.bfloat16)

    def trip(j, carry):
        chains = [(j * NAT_ROWS_PER_TRIP + q, c) for q in range(NAT_ROWS_PER_TRIP)
                  for c in range(N_HEADS_B // NAT_HEADS_PER_CHAIN)]
        pending = [scores(*c) for c in chains[:NAT_LOOKAHEAD]]
        for idx, (_, p) in enumerate(chains):
            s, koff, qoff = pending.pop(0)
            if idx + NAT_LOOKAHEAD < len(chains):
                pending.append(scores(*chains[idx + NAT_LOOKAHEAD]))
            attend(s, koff, qoff, p)
        return carry

    lax.fori_loop(0, NAT_ROWS_PER_STEP // NAT_ROWS_PER_TRIP, trip, 0)


def _nat_attention(qkv, tb, bsz, t):
    n = bsz * t
    rows_seq = t // GRID_W
    main = NAT_ROWS_PER_STEP * GRID_W
    halo = NAT_HALO_ROWS * GRID_W
    ntile = t // main
    nhalo_seq = t // halo
    per = main // halo

    def main_map(col):
        return lambda b, i: (b * ntile + i, col)

    def prev_map(col):
        return lambda b, i: (b * nhalo_seq + jnp.maximum(i * per - 1, 0), col)

    def next_map(col):
        return lambda b, i: (b * nhalo_seq + jnp.minimum(i * per + per, nhalo_seq - 1), col)

    qb, kb, vb = QB_COL // WIDTH_B, KB_COL // WIDTH_B, VB_COL // WIDTH_B
    return pl.pallas_call(
        functools.partial(_nat_kernel, rows_seq=rows_seq),
        out_shape=jax.ShapeDtypeStruct((n, WIDTH_B), jnp.bfloat16),
        grid=(bsz, ntile),
        in_specs=[
            pl.BlockSpec((main, WIDTH_B), main_map(qb)),
            pl.BlockSpec((halo, WIDTH_B), prev_map(kb)),
            pl.BlockSpec((main, WIDTH_B), main_map(kb)),
            pl.BlockSpec((halo, WIDTH_B), next_map(kb)),
            pl.BlockSpec((halo, WIDTH_B), prev_map(vb)),
            pl.BlockSpec((main, WIDTH_B), main_map(vb)),
            pl.BlockSpec((halo, WIDTH_B), next_map(vb)),
            pl.BlockSpec((N_HEADS_B // 2, NA_ROWS, 2 * GRID_W, NAT_KEYS), lambda b, i: (0, 0, 0, 0)),
        ],
        out_specs=pl.BlockSpec((main, WIDTH_B), main_map(0)),
        scratch_shapes=[pltpu.VMEM((main + 2 * halo, WIDTH_B), jnp.bfloat16),
                        pltpu.VMEM((main + 2 * halo, WIDTH_B), jnp.bfloat16)],
        compiler_params=_cparams("parallel", "parallel"),
        name="nat",
    )(qkv, qkv, qkv, qkv, qkv, qkv, qkv, tb)


EXPERT_ROW0 = N_GROUPS
ROUTE_ROWS = 48
INFO_E1, INFO_E2, INFO_R1, INFO_R2, INFO_W1, INFO_W2 = range(6)
INFO_ROWS = 8
MERGE_SUBTILES = 4


def _route(lt, carry, tri):
    rr, tm = lt.shape
    row = lax.broadcasted_iota(jnp.int32, (rr, tm), 0).astype(jnp.float32)
    none = jnp.float32(rr)

    def first_max(sel):
        m = jnp.max(jnp.where(sel, lt, NEG_BIG), axis=0, keepdims=True)
        idx = jnp.min(jnp.where(sel & (lt == m), row, none), axis=0, keepdims=True)
        return m, idx

    is_group = row < N_GROUPS
    mg, g = first_max(is_group)
    pg_sel = 1.0 / jnp.sum(jnp.where(is_group, jnp.exp(jnp.where(is_group, lt, mg) - mg), 0.0),
                           axis=0, keepdims=True)
    row0 = EXPERT_ROW0 + EXPERTS_PER_GROUP * g
    in_group = (row >= row0) & (row < row0 + EXPERTS_PER_GROUP)
    m1, i1 = first_max(in_group)
    m2, i2 = first_max(in_group & (row != i1))
    e2 = jnp.exp(m2 - m1)
    w1 = pg_sel / (1.0 + e2)
    w2 = pg_sel * e2 / (1.0 + e2)

    oh1 = row == i1
    oh2 = row == i2
    both = (oh1 | oh2).astype(jnp.bfloat16)
    before = jnp.dot(both, tri, preferred_element_type=jnp.float32) + carry
    r1 = jnp.sum(jnp.where(oh1, before, 0.0), axis=0, keepdims=True)
    r2 = jnp.sum(jnp.where(oh2, before, 0.0), axis=0, keepdims=True)
    new_carry = carry + jnp.sum(both.astype(jnp.float32), axis=1, keepdims=True)

    field = lax.broadcasted_iota(jnp.int32, (INFO_ROWS, tm), 0)
    info = jnp.zeros((INFO_ROWS, tm), jnp.float32)
    for k, v in ((INFO_E1, i1 - EXPERT_ROW0), (INFO_E2, i2 - EXPERT_ROW0), (INFO_R1, r1), (INFO_R2, r2),
                 (INFO_W1, w1), (INFO_W2, w2)):
        info = jnp.where(field == k, v, info)
    return info, new_carry


def _merge_kernel(x_ref, oa_ref, ob_ref, lng_ref, lnb_ref, wg_ref, wpa_ref, wpb_ref, wo_ref,
                  l1g_ref, l1b_ref, wr_ref, br_ref, cnt0_ref,
                  h1_ref, h1p_ref, info_ref, cnt_ref, carry_ref, tri_ref):
    tm = x_ref.shape[0]

    @pl.when(pl.program_id(0) == 0)
    def _():
        carry_ref[...] = cnt0_ref[...]
        r = lax.broadcasted_iota(jnp.int32, (tm, tm), 0)
        c = lax.broadcasted_iota(jnp.int32, (tm, tm), 1)
        tri_ref[...] = (r < c).astype(jnp.bfloat16)

    def project(rows):
        h = _layer_norm(x_ref[rows], lng_ref[...], lnb_ref[...])
        gates = jnp.dot(h.astype(jnp.bfloat16), wg_ref[...], preferred_element_type=jnp.float32)
        pa = jnp.dot(oa_ref[rows], wpa_ref[...], preferred_element_type=jnp.float32)
        pb = jnp.dot(ob_ref[rows], wpb_ref[...], preferred_element_type=jnp.float32)
        return h, gates, pa, pb

    def mix(h, gates, pa, pb):
        mixin = jax.nn.sigmoid(gates[:, :D_MODEL]) * pa + jax.nn.sigmoid(gates[:, D_MODEL:]) * pb
        return DEEPNORM_ALPHA * h + jnp.dot(mixin.astype(jnp.bfloat16), wo_ref[...],
                                            preferred_element_type=jnp.float32)

    def norm_and_logits(pre, rows):
        h1 = _layer_norm(pre, l1g_ref[...], l1b_ref[...])
        h1_ref[rows] = h1
        h1p_ref[rows] = _pack_rows(h1)
        return lax.dot_general(wr_ref[...], h1.astype(jnp.bfloat16), (((1,), (1,)), ((), ())),
                               preferred_element_type=jnp.float32)

    sub = tm // MERGE_SUBTILES
    parts = [slice(k * sub, (k + 1) * sub) for k in range(MERGE_SUBTILES)]
    projected = [project(rows) for rows in parts]
    mixed = [mix(*pr) for pr in projected]
    logits_t = jnp.concatenate([norm_and_logits(pre, rows) for pre, rows in zip(mixed, parts)], axis=1)
    logits_t = logits_t + br_ref[...]
    info, carry = _route(logits_t, carry_ref[...], tri_ref[...])
    info_ref[...] = info
    carry_ref[...] = carry
    cnt_ref[...] = carry[:, :LANES]


def _merge(x2, oa, ob, ln_g, ln_b, w_gates, w_pa, w_pb, w_o, l1g, l1b, w_r, b_r, cnt0, tm):
    n = x2.shape[0]

    def const(shape):
        return pl.BlockSpec(shape, lambda i: (0,) * len(shape))

    def rows(width):
        return pl.BlockSpec((tm, width), lambda i: (i, 0))

    return pl.pallas_call(
        _merge_kernel,
        out_shape=(jax.ShapeDtypeStruct((n, D_MODEL), jnp.float32),
                   jax.ShapeDtypeStruct((n, PACKED_WIDTH), jnp.uint32),
                   jax.ShapeDtypeStruct((INFO_ROWS, n), jnp.float32),
                   jax.ShapeDtypeStruct((ROUTE_ROWS, LANES), jnp.float32)),
        grid=(n // tm,),
        in_specs=[
            rows(D_MODEL), rows(WIDTH_A), rows(WIDTH_B),
            const((1, D_MODEL)), const((1, D_MODEL)),
            const((D_MODEL, 2 * D_MODEL)),
            const((WIDTH_A, D_MODEL)), const((WIDTH_B, D_MODEL)),
            const((D_MODEL, D_MODEL)),
            const((1, D_MODEL)), const((1, D_MODEL)),
            const((ROUTE_ROWS, D_MODEL)), const((ROUTE_ROWS, tm)), const((ROUTE_ROWS, tm)),
        ],
        out_specs=(rows(D_MODEL), rows(PACKED_WIDTH), pl.BlockSpec((INFO_ROWS, tm), lambda i: (0, i)),
                   const((ROUTE_ROWS, LANES))),
        scratch_shapes=[pltpu.VMEM((ROUTE_ROWS, tm), jnp.float32), pltpu.VMEM((tm, tm), jnp.bfloat16)],
        compiler_params=_cparams("arbitrary"),
        name="merge",
    )(x2, oa, ob, ln_g, ln_b, w_gates, w_pa, w_pb, w_o, l1g, l1b, w_r, b_r, cnt0)


SC_CORES = 2
SC_SUBCORES = 16
SC_WORKERS = SC_CORES * SC_SUBCORES
SC_ROWS_PER_STREAM = 64


def _sc_worker():
    return lax.axis_index("s") * SC_CORES + lax.axis_index("c")


def _scatter_indices(dest_t, pad_rows):
    n = dest_t.shape[1]
    nchunks = n // (SC_WORKERS * SC_ROWS_PER_STREAM)
    npad = pad_rows.size // (SC_WORKERS * SC_ROWS_PER_STREAM)
    assert nchunks * SC_ROWS_PER_STREAM * SC_WORKERS == n and nchunks % 2 == 0
    assert npad * SC_WORKERS * SC_ROWS_PER_STREAM == pad_rows.size
    return (dest_t.reshape(TOP_K, SC_WORKERS, nchunks, SC_ROWS_PER_STREAM),
            pad_rows.reshape(SC_WORKERS, npad, SC_ROWS_PER_STREAM))


def _dispatch(h1p, idx, pad_idx, zeros, nrows_out):
    n, width = h1p.shape
    per_worker = n // SC_WORKERS
    nchunks, npad = idx.shape[2], pad_idx.shape[1]
    assert zeros.shape == (SC_ROWS_PER_STREAM, width) and zeros.dtype == h1p.dtype
    mesh = plsc.VectorSubcoreMesh(core_axis_name="c", subcore_axis_name="s")

    @functools.partial(
        pl.kernel, out_type=jax.ShapeDtypeStruct((nrows_out, width), h1p.dtype), mesh=mesh,
        scratch_types=[pltpu.VMEM((TOP_K, nchunks, SC_ROWS_PER_STREAM), jnp.int32),
                       pltpu.VMEM((npad, SC_ROWS_PER_STREAM), jnp.int32),
                       pltpu.VMEM((2, SC_ROWS_PER_STREAM, width), h1p.dtype),
                       pltpu.SemaphoreType.DMA((2,)), pltpu.SemaphoreType.DMA((2,))],
        name="sc_dispatch")
    def scatter_kernel(src_hbm, idx_hbm, pad_hbm, zeros_hbm, out_hbm, idx_v, pad_v, rows_v, rsem, ssem):
        wid = _sc_worker()
        base = wid * per_worker
        for k in range(TOP_K):
            pltpu.sync_copy(idx_hbm.at[k, wid], idx_v.at[k])
        pltpu.sync_copy(pad_hbm.at[wid], pad_v)

        def read(j, slot):
            src = src_hbm.at[pl.ds(base + j * SC_ROWS_PER_STREAM, SC_ROWS_PER_STREAM)]
            return pltpu.make_async_copy(src, rows_v.at[slot], rsem.at[slot])

        def scatter(j, slot, k):
            return pltpu.make_async_copy(rows_v.at[slot], out_hbm.at[idx_v.at[k, j]], ssem.at[slot])

        read(0, 0).start()

        @pl.loop(0, nchunks, step=2)
        def _(j0):
            for slot in range(2):
                j = j0 + slot
                read(j, slot).wait()

                @pl.when(j + 1 < nchunks)
                def _():
                    @pl.when(j >= 1)
                    def _():
                        for k in range(TOP_K):
                            scatter(j - 1, 1 - slot, k).wait()
                    read(j + 1, 1 - slot).start()

                for k in range(TOP_K):
                    scatter(j, slot, k).start()

        for k in range(TOP_K):
            scatter(nchunks - 2, 0, k).wait()
            scatter(nchunks - 1, 1, k).wait()

        pltpu.sync_copy(zeros_hbm, rows_v.at[0])
        fills = [pltpu.make_async_copy(rows_v.at[0], out_hbm.at[pad_v.at[c]], ssem.at[0]) for c in range(npad)]
        for f in fills:
            f.start()
        for f in fills:
            f.wait()

    return scatter_kernel(h1p, idx, pad_idx, zeros)


def _sorted_layout(counts, eid, rank, bm):
    n = eid.shape[1]
    nblocks = TOP_K * n // bm + N_EXPERTS
    expert = jnp.arange(N_EXPERTS, dtype=jnp.int32)
    before = expert[None, :] < expert[:, None]
    blocks = (counts + bm - 1) // bm
    first_blk = jnp.sum(jnp.where(before, blocks[None, :], 0), axis=1)
    starts = first_blk * bm
    dest = rank + jnp.sum(jnp.where(eid[None] == expert[:, None, None], starts[:, None, None], 0), axis=0)
    total = jnp.sum(blocks)
    j = jnp.arange(bm, dtype=jnp.int32)[None, :]
    npad = blocks * bm - counts
    spare_before = jnp.sum(jnp.where(before, (bm - npad)[None, :], 0), axis=1)
    pad_rows = jnp.where(j < npad[:, None], (starts + counts)[:, None] + j,
                         (total * bm + spare_before - npad)[:, None] + j).astype(jnp.int32)
    w = jnp.arange(nblocks, dtype=jnp.int32)
    real = jnp.minimum(w, total - 1)
    blk_end = first_blk + blocks
    owner = jnp.minimum(jnp.sum((blk_end[None, :] <= real[:, None]).astype(jnp.int32), axis=1), N_EXPERTS - 1)
    valid = (w < total).astype(jnp.int32)
    return dest.astype(jnp.int32), pad_rows, (owner.astype(jnp.int32), valid)


EXPERT_BLOCKS_PER_STEP = 2


def _expert_kernel(e_ref, valid_ref, x_ref, *refs):
    o_ref = refs[-1]
    bm = x_ref.shape[0] // EXPERT_BLOCKS_PER_STEP
    first = pl.program_id(0) * EXPERT_BLOCKS_PER_STEP

    def compute(i):
        wg_ref, wu_ref, wd_ref = refs[3 * i:3 * i + 3]
        x = _unpack_rows(x_ref[i * bm:(i + 1) * bm]).astype(jnp.bfloat16)
        g = jnp.dot(x, wg_ref[0], preferred_element_type=jnp.float32)
        u = jnp.dot(x, wu_ref[0], preferred_element_type=jnp.float32)
        hmid = (jax.nn.silu(g) * u).astype(jnp.bfloat16)
        o_ref[i * bm:(i + 1) * bm] = _pack_rows(jnp.dot(hmid, wd_ref[0], preferred_element_type=jnp.float32))

    def clear(i):
        o_ref[i * bm:(i + 1) * bm] = jnp.zeros((bm, PACKED_WIDTH), o_ref.dtype)

    nvalid = sum(valid_ref[first + i] for i in range(EXPERT_BLOCKS_PER_STEP))
    for k in range(EXPERT_BLOCKS_PER_STEP + 1):
        @pl.when(nvalid == k)
        def _(k=k):
            for i in range(EXPERT_BLOCKS_PER_STEP):
                compute(i) if i < k else clear(i)


def _experts(xs, plan, w_gate, w_up, w_down, bm):
    nblocks = plan[0].shape[0]
    nsteps = nblocks // EXPERT_BLOCKS_PER_STEP
    assert xs.shape[0] == nblocks * bm and nsteps * EXPERT_BLOCKS_PER_STEP == nblocks
    step_rows = EXPERT_BLOCKS_PER_STEP * bm

    def weights(shape, i):
        return pl.BlockSpec((1,) + shape, lambda w, e, valid: (e[w * EXPERT_BLOCKS_PER_STEP + i], 0, 0))

    grid_spec = pltpu.PrefetchScalarGridSpec(
        num_scalar_prefetch=len(plan),
        grid=(nsteps,),
        in_specs=[pl.BlockSpec((step_rows, PACKED_WIDTH), lambda w, *_: (w, 0))] + [
            weights(shape, i) for i in range(EXPERT_BLOCKS_PER_STEP)
            for shape in ((D_MODEL, D_EXPERT), (D_MODEL, D_EXPERT), (D_EXPERT, D_MODEL))],
        out_specs=pl.BlockSpec((step_rows, PACKED_WIDTH), lambda w, *_: (w, 0)),
    )
    return pl.pallas_call(
        _expert_kernel,
        out_shape=jax.ShapeDtypeStruct(xs.shape, jnp.uint32),
        grid_spec=grid_spec,
        compiler_params=_cparams("parallel"),
        name="experts",
    )(*plan, xs, *([w_gate, w_up, w_down] * EXPERT_BLOCKS_PER_STEP))


def _sc_gather_rows(table, idx):
    nrows = idx.shape[0]
    width = table.shape[1]
    per_worker = nrows // SC_WORKERS
    nchunks = per_worker // SC_ROWS_PER_STREAM
    assert nchunks * SC_ROWS_PER_STREAM * SC_WORKERS == nrows and nchunks % 2 == 0
    mesh = plsc.VectorSubcoreMesh(core_axis_name="c", subcore_axis_name="s")

    @functools.partial(
        pl.kernel, out_type=jax.ShapeDtypeStruct((nrows, width), table.dtype), mesh=mesh,
        scratch_types=[pltpu.VMEM((per_worker,), jnp.int32),
                       pltpu.VMEM((2, SC_ROWS_PER_STREAM, width), table.dtype),
                       pltpu.SemaphoreType.DMA((2,)), pltpu.SemaphoreType.DMA((2,))],
        name="sc_gather")
    def gather_kernel(table_hbm, idx_hbm, out_hbm, idx_v, rows_v, gsem, wsem):
        base = _sc_worker() * per_worker
        pltpu.sync_copy(idx_hbm.at[pl.ds(base, per_worker)], idx_v)

        def gather(j, slot):
            rows = idx_v.at[pl.ds(j * SC_ROWS_PER_STREAM, SC_ROWS_PER_STREAM)]
            return pltpu.make_async_copy(table_hbm.at[rows], rows_v.at[slot], gsem.at[slot])

        def write(j, slot):
            dst = out_hbm.at[pl.ds(base + j * SC_ROWS_PER_STREAM, SC_ROWS_PER_STREAM)]
            return pltpu.make_async_copy(rows_v.at[slot], dst, wsem.at[slot])

        gather(0, 0).start()

        @pl.loop(0, nchunks, step=2)
        def _(j0):
            for slot in range(2):
                j = j0 + slot
                gather(j, slot).wait()

                @pl.when(j + 1 < nchunks)
                def _():
                    @pl.when(j >= 1)
                    def _():
                        write(j - 1, 1 - slot).wait()
                    gather(j + 1, 1 - slot).start()

                write(j, slot).start()

        write(nchunks - 2, 0).wait()
        write(nchunks - 1, 1).wait()

    return gather_kernel(table, idx)


def _finalize_kernel(h1_ref, y1_ref, y2_ref, info_ref, g_ref, b_ref, o_ref):
    tt = h1_ref.shape[0]
    pad = jnp.zeros((LANES - INFO_ROWS, tt), jnp.float32)
    info = jnp.concatenate([info_ref[...], pad], axis=0).T
    moe = (_unpack_rows(y1_ref[...]) * info[:, INFO_W1:INFO_W1 + 1]
           + _unpack_rows(y2_ref[...]) * info[:, INFO_W2:INFO_W2 + 1])
    o_ref[...] = _layer_norm(DEEPNORM_ALPHA * h1_ref[...] + moe, g_ref[...], b_ref[...])


def _combine(h1, info, dest_flat, ys, ln_g, ln_b, tt):
    n = h1.shape[0]
    nsteps = n // tt
    yg = _sc_gather_rows(ys, dest_flat)
    return pl.pallas_call(
        _finalize_kernel,
        out_shape=jax.ShapeDtypeStruct((n, D_MODEL), jnp.float32),
        grid=(nsteps,),
        in_specs=[
            pl.BlockSpec((tt, D_MODEL), lambda i: (i, 0)),
            pl.BlockSpec((tt, PACKED_WIDTH), lambda i: (i, 0)),
            pl.BlockSpec((tt, PACKED_WIDTH), lambda i: (nsteps + i, 0)),
            pl.BlockSpec((INFO_ROWS, tt), lambda i: (0, i)),
            pl.BlockSpec((1, D_MODEL), lambda i: (0, 0)),
            pl.BlockSpec((1, D_MODEL), lambda i: (0, 0)),
        ],
        out_specs=pl.BlockSpec((tt, D_MODEL), lambda i: (i, 0)),
        compiler_params=_cparams("parallel"),
        name="finalize",
    )(h1, yg, yg, info, ln_g, ln_b)


TM_QKV = 1024
TQ_WIN = 1024
TM_MERGE = 1024
TT_ROWS = 1024
BM_EXPERT = 512


def _prepare_weights(ln_in_g, ln_in_b, w_in, attn_sink, rel_pos_bias, w_proj_a, w_proj_b, w_out,
                     ln1_g, ln1_b, w_route_group, b_route_group, w_route_expert, b_route_expert,
                     ln2_g, ln2_b):
    bf = jnp.bfloat16
    w = w_in[0]
    splits = np.cumsum([WIDTH_A, KV_WIDTH_A, KV_WIDTH_A, WIDTH_B, WIDTH_B, WIDTH_B, D_MODEL])
    wqa, wka, wva, wqb, wkb, wvb, wga, wgb = jnp.split(w, [int(s) for s in splits], axis=1)
    wqa = (wqa.reshape(D_MODEL, N_KV_HEADS_A, GQA_GROUP, HEAD_DIM).transpose(0, 2, 1, 3)
           .reshape(D_MODEL, WIDTH_A))
    w_qkv = jnp.concatenate([wqa, wqb, wkb, wvb, wka, wva], axis=1).astype(bf)
    w_gates = jnp.concatenate([wga, wgb], axis=1).astype(bf)
    w_pa = (w_proj_a[0].reshape(N_KV_HEADS_A, GQA_GROUP, HEAD_DIM, D_MODEL).transpose(1, 0, 2, 3)
            .reshape(WIDTH_A, D_MODEL).astype(bf))
    w_pb = w_proj_b[0].astype(bf)
    w_o = w_out[0].astype(bf)
    pad = ROUTE_ROWS - N_GROUPS - N_EXPERTS
    w_r = jnp.concatenate([w_route_group[0].T, w_route_expert[0].T, jnp.zeros((pad, D_MODEL), jnp.float32)], axis=0)
    w_route = w_r.astype(bf)
    b_r = jnp.concatenate([b_route_group[0], b_route_expert[0], jnp.zeros((pad,), jnp.float32)])
    b_r = jnp.broadcast_to(b_r[:, None], (ROUTE_ROWS, TM_MERGE))
    row = lambda v: v.reshape(1, D_MODEL)
    return dict(
        ln_in_g=row(ln_in_g), ln_in_b=row(ln_in_b), w_qkv=w_qkv, w_gates=w_gates,
        sink=attn_sink[0].astype(jnp.float32), nat_bias=_nat_bias_table(rel_pos_bias[0]),
        w_pa=w_pa, w_pb=w_pb, w_o=w_o, ln1_g=row(ln1_g[0]), ln1_b=row(ln1_b[0]),
        w_route=w_route, b_r=b_r, ln2_g=row(ln2_g[0]), ln2_b=row(ln2_b[0]))


def _after(value, other):
    if other is None:
        return value
    other = other.astype(jnp.float32)
    zero = jnp.where(jnp.isfinite(other), other, 0.0) * 0.0
    return value + zero.astype(value.dtype)


def _attend_and_route(x, p, after=None, cast=()):
    bsz, t, _ = x.shape
    n = bsz * t
    x2 = x.reshape(n, D_MODEL)
    qkv, cast_bf16 = _qkv(x2, _after(p["ln_in_g"], after), p["ln_in_b"], p["w_qkv"], TM_QKV, cast)
    oa = _win_attention(qkv, p["sink"], bsz, t, TQ_WIN)
    ob = _nat_attention(qkv, p["nat_bias"], bsz, t)
    cnt0 = jnp.zeros((ROUTE_ROWS, TM_MERGE), jnp.float32)
    h1, h1p, info, cnt = _merge(x2, oa, ob, p["ln_in_g"], p["ln_in_b"], p["w_gates"], p["w_pa"], p["w_pb"],
                                p["w_o"], p["ln1_g"], p["ln1_b"], p["w_route"], p["b_r"], cnt0, TM_MERGE)
    counts = cnt[EXPERT_ROW0:EXPERT_ROW0 + N_EXPERTS, 0].astype(jnp.int32)
    eid = info[INFO_E1:INFO_E2 + 1].astype(jnp.int32)
    rank = info[INFO_R1:INFO_R2 + 1].astype(jnp.int32)
    dest_t, pad_rows, plan = _sorted_layout(counts, eid, rank, BM_EXPERT)
    zero_rows = jnp.zeros((SC_ROWS_PER_STREAM, PACKED_WIDTH), jnp.uint32)
    moves = lax.optimization_barrier(_scatter_indices(dest_t, pad_rows) + (zero_rows, dest_t.reshape(TOP_K * n)))
    return dict(shape=x.shape, h1=h1, h1p=h1p, info=info, counts=counts, moves=moves, plan=plan, cast=cast_bf16)


def _run_experts(r, expert_weights, after=None):
    owner, valid = r["plan"]
    scatter_idx, pad_idx, zero_rows, _ = r["moves"]
    xs = _dispatch(r["h1p"], scatter_idx, pad_idx, zero_rows, owner.shape[0] * BM_EXPERT)
    return _experts(xs, (owner, _after(valid, after)), *expert_weights, BM_EXPERT)


def _finish(r, ys, p, after=None):
    out = _combine(r["h1"], r["info"], r["moves"][3], ys, _after(p["ln2_g"], after), p["ln2_b"], TT_ROWS)
    return out.reshape(r["shape"])


def kernel(x_prompt, x_sample, ln_in_g, ln_in_b, w_in, attn_sink, rel_pos_bias, w_proj_a, w_proj_b, w_out,
           ln1_g, ln1_b, w_route_group, b_route_group, w_route_expert, b_route_expert,
           w_gate, w_up, w_down, ln2_g, ln2_b):
    p = _prepare_weights(ln_in_g, ln_in_b, w_in, attn_sink, rel_pos_bias, w_proj_a, w_proj_b, w_out,
                         ln1_g, ln1_b, w_route_group, b_route_group, w_route_expert, b_route_expert,
                         ln2_g, ln2_b)
    rp = _attend_and_route(x_prompt, p, cast=(w_gate[0], w_up[0], w_down[0]))
    glue_done = sum(a.reshape(-1)[0].astype(jnp.float32) for a in rp["moves"] + rp["plan"])
    rs = _attend_and_route(x_sample, p, after=glue_done)
    ys_p = _run_experts(rp, rp["cast"])
    ys_s = _run_experts(rs, rp["cast"], after=ys_p[0, 0])
    y_prompt = _finish(rp, ys_p, p)
    y_sample = _finish(rs, ys_s, p, after=y_prompt[0, 0, 0])
    return (y_prompt, y_sample)
```

```python
import functools

import numpy as np
import jax
import jax.numpy as jnp
from jax import lax
from jax.experimental import pallas as pl
from jax.experimental.pallas import tpu as pltpu
from jax.experimental.pallas import tpu_sc as plsc

D_MODEL = 1024
HEAD_DIM = 64
N_HEADS_A = 8
N_KV_HEADS_A = 2
WINDOW = 128
N_HEADS_B = 8
GRID_W = 64
NA_ROWS = 8
NA_COLS = 16
N_GROUPS = 4
EXPERTS_PER_GROUP = 8
N_EXPERTS = N_GROUPS * EXPERTS_PER_GROUP
TOP_K = 2
D_EXPERT = D_MODEL // 2
LN_EPS = 1e-5
DEPTH = 1
DEEPNORM_ALPHA = (2.0 * DEPTH) ** 0.25
WIDTH_A = N_HEADS_A * HEAD_DIM
KV_WIDTH_A = N_KV_HEADS_A * HEAD_DIM
WIDTH_B = N_HEADS_B * HEAD_DIM
QKV_WIDTH = WIDTH_A + 2 * KV_WIDTH_A + 3 * WIDTH_B

LANES = 128
VMEM_LIMIT_BYTES = 56 * 1024 * 1024

NEG_BIG = -1e30
LOG2E = float(np.log2(np.e))

QA_COL, QB_COL, KB_COL, VB_COL = 0, WIDTH_A, WIDTH_A + WIDTH_B, WIDTH_A + 2 * WIDTH_B
KA_COL = WIDTH_A + 3 * WIDTH_B
VA_COL = KA_COL + KV_WIDTH_A

GQA_GROUP = N_HEADS_A // N_KV_HEADS_A


def _cparams(*sem):
    return pltpu.CompilerParams(dimension_semantics=sem, vmem_limit_bytes=VMEM_LIMIT_BYTES)


def _layer_norm(x, g, b):
    mu = jnp.mean(x, axis=-1, keepdims=True)
    xc = x - mu
    var = jnp.mean(xc * xc, axis=-1, keepdims=True)
    return xc * lax.rsqrt(var + LN_EPS) * g + b


PACKED_WIDTH = D_MODEL // 2


def _pack_rows(x):
    def bits(v):
        return lax.bitcast_convert_type(v.astype(jnp.bfloat16).astype(jnp.float32), jnp.uint32)
    return bits(x[:, :PACKED_WIDTH]) | (bits(x[:, PACKED_WIDTH:]) >> 16)


def _unpack_rows(w):
    hi = lax.bitcast_convert_type(w & jnp.uint32(0xFFFF0000), jnp.float32)
    lo = lax.bitcast_convert_type(w << 16, jnp.float32)
    return jnp.concatenate([hi, lo], axis=1)


def _qkv_kernel(x_ref, g_ref, b_ref, w_ref, *rest):
    o_ref = rest[len(rest) // 2]
    h = _layer_norm(x_ref[...], g_ref[...], b_ref[...])
    y = jnp.dot(h.astype(jnp.bfloat16), w_ref[...], preferred_element_type=jnp.float32)
    col = lax.broadcasted_iota(jnp.int32, (1, QKV_WIDTH), 1)
    y = y * jnp.where(col < KB_COL, HEAD_DIM ** -0.5 * LOG2E, 1.0)
    o_ref[...] = y.astype(jnp.bfloat16)
    ncast = len(rest) // 2
    for src_ref, dst_ref in zip(rest[:ncast], rest[ncast + 1:]):
        dst_ref[...] = src_ref[...].astype(jnp.bfloat16)


def _qkv(x2, ln_g, ln_b, w_qkv, tm, cast=()):
    n = x2.shape[0]
    steps = n // tm
    per_step = -(-N_EXPERTS // steps)
    assert all(c.shape[0] == N_EXPERTS for c in cast) and (steps * per_step) % N_EXPERTS == 0
    revisit = steps * per_step // N_EXPERTS

    def expert_block(c):
        return pl.BlockSpec((per_step,) + c.shape[1:], lambda i: (i // revisit, 0, 0))

    outs = pl.pallas_call(
        _qkv_kernel,
        out_shape=[jax.ShapeDtypeStruct((n, QKV_WIDTH), jnp.bfloat16)]
        + [jax.ShapeDtypeStruct(c.shape, jnp.bfloat16) for c in cast],
        grid=(steps,),
        in_specs=[
            pl.BlockSpec((tm, D_MODEL), lambda i: (i, 0)),
            pl.BlockSpec((1, D_MODEL), lambda i: (0, 0)),
            pl.BlockSpec((1, D_MODEL), lambda i: (0, 0)),
            pl.BlockSpec((D_MODEL, QKV_WIDTH), lambda i: (0, 0)),
        ] + [expert_block(c) for c in cast],
        out_specs=[pl.BlockSpec((tm, QKV_WIDTH), lambda i: (i, 0))] + [expert_block(c) for c in cast],
        compiler_params=_cparams("arbitrary" if cast else "parallel"),
        name="qkv",
    )(x2, ln_g, ln_b, w_qkv, *cast)
    return outs[0], tuple(outs[1:])


WIN_BLK = 128
WIN_LOOKAHEAD = 2


def _win_bias_table():
    qi = np.arange(WIN_BLK)[:, None]
    kj = np.arange(3 * WIN_BLK)[None, :]
    dist = np.abs(kj - WIN_BLK - qi).astype(np.float64)
    slopes = 2.0 ** (-8.0 * np.arange(1, N_HEADS_A + 1) / N_HEADS_A)
    per_head = np.where(dist <= WINDOW, -slopes[:, None, None] * dist[None] * LOG2E, NEG_BIG)
    groups = [np.concatenate([per_head[j], per_head[j + 4]], axis=0) for j in range(4)]
    return np.stack(groups).astype(np.float32)


def _win_kernel(sink_ref, q_ref, kp_ref, km_ref, kn_ref, vp_ref, vm_ref, vn_ref, bias_ref, o_ref,
                *, nsub, nblk_seq):
    i = pl.program_id(1)
    kcat = jnp.concatenate([kp_ref[...], km_ref[...], kn_ref[...]], axis=0)
    vcat = jnp.concatenate([vp_ref[...], vm_ref[...], vn_ref[...]], axis=0)
    lo = lax.broadcasted_iota(jnp.int32, (1, LANES), 1) < HEAD_DIM
    col = lax.broadcasted_iota(jnp.int32, (1, 3 * WIN_BLK), 1)
    top = lax.broadcasted_iota(jnp.int32, (2 * WIN_BLK, 1), 0) < WIN_BLK
    zero = jnp.zeros((), jnp.bfloat16)

    def scores(j, g):
        n = i * nsub + j
        off_seq = ((col < WIN_BLK) & (n == 0)) | ((col >= 2 * WIN_BLK) & (n == nblk_seq - 1))
        edge = jnp.where(off_seq, NEG_BIG, 0.0)
        qg = q_ref[WIN_BLK * j:WIN_BLK * (j + 1), LANES * g:LANES * (g + 1)]
        qm = jnp.concatenate([jnp.where(lo, qg, zero), jnp.where(lo, zero, qg)], axis=0)
        kj = kcat[WIN_BLK * j:WIN_BLK * (j + 3)]
        s = lax.dot_general(qm, kj, (((1,), (1,)), ((), ())), preferred_element_type=jnp.float32)
        return s + bias_ref[g] + edge

    def attend(s, j, g):
        vj = vcat[WIN_BLK * j:WIN_BLK * (j + 3)]
        sink = jnp.where(top, sink_ref[g], sink_ref[g + 4]) * LOG2E
        m = jnp.maximum(jnp.max(s, axis=-1, keepdims=True), sink)
        p = jnp.exp2(s - m)
        l = jnp.sum(p, axis=-1, keepdims=True) + jnp.exp2(sink - m)
        o2 = jnp.dot(p.astype(jnp.bfloat16), vj, preferred_element_type=jnp.float32)
        o2 = o2 * (1.0 / l)
        o_ref[WIN_BLK * j:WIN_BLK * (j + 1), LANES * g:LANES * (g + 1)] = (
            jnp.where(lo, o2[:WIN_BLK], o2[WIN_BLK:]).astype(jnp.bfloat16))

    chains = [(j, g) for j in range(nsub) for g in range(4)]
    pending = [scores(*c) for c in chains[:WIN_LOOKAHEAD]]
    for idx, c in enumerate(chains):
        s = pending.pop(0)
        if idx + WIN_LOOKAHEAD < len(chains):
            pending.append(scores(*chains[idx + WIN_LOOKAHEAD]))
        attend(s, *c)


def _win_attention(qkv, sink, bsz, t, tq):
    n = bsz * t
    nsub = tq // WIN_BLK
    nblk_seq = t // WIN_BLK
    ntile = t // tq
    bias = jnp.asarray(_win_bias_table())

    def main_map(col):
        return lambda b, i, *_: (b * ntile + i, col)

    def prev_map(col):
        return lambda b, i, *_: (b * nblk_seq + jnp.maximum(i * nsub - 1, 0), col)

    def next_map(col):
        return lambda b, i, *_: (b * nblk_seq + jnp.minimum(i * nsub + nsub, nblk_seq - 1), col)

    halo = (WIN_BLK, LANES)
    ka, va = KA_COL // LANES, VA_COL // LANES
    grid_spec = pltpu.PrefetchScalarGridSpec(
        num_scalar_prefetch=1,
        grid=(bsz, ntile),
        in_specs=[
            pl.BlockSpec((tq, WIDTH_A), main_map(QA_COL // WIDTH_A)),
            pl.BlockSpec(halo, prev_map(ka)),
            pl.BlockSpec((tq, LANES), main_map(ka)),
            pl.BlockSpec(halo, next_map(ka)),
            pl.BlockSpec(halo, prev_map(va)),
            pl.BlockSpec((tq, LANES), main_map(va)),
            pl.BlockSpec(halo, next_map(va)),
            pl.BlockSpec((4, 2 * WIN_BLK, 3 * WIN_BLK), lambda b, i, *_: (0, 0, 0)),
        ],
        out_specs=pl.BlockSpec((tq, WIDTH_A), main_map(0)),
    )
    return pl.pallas_call(
        functools.partial(_win_kernel, nsub=nsub, nblk_seq=nblk_seq),
        out_shape=jax.ShapeDtypeStruct((n, WIDTH_A), jnp.bfloat16),
        grid_spec=grid_spec,
        compiler_params=_cparams("parallel", "parallel"),
        name="win",
    )(sink, qkv, qkv, qkv, qkv, qkv, qkv, qkv, bias)


NAT_ROWS_PER_STEP = 16
NAT_HALO_ROWS = NA_ROWS // 2
NAT_KEYS = NA_ROWS * GRID_W
NAT_ROWS_PER_TRIP = 8
NAT_HEADS_PER_CHAIN = 2
NAT_LOOKAHEAD = 4


def _nat_bias_table(rpb):
    c = np.arange(GRID_W)
    cs = np.clip(c - NA_COLS // 2, 0, GRID_W - NA_COLS)
    col_mask = (c[None, :] >= cs[:, None]) & (c[None, :] < cs[:, None] + NA_COLS)
    dc = np.clip(c[None, :] - c[:, None] + (NA_COLS - 1), 0, 2 * NA_COLS - 2)
    onehot = jnp.asarray(dc[None] == np.arange(2 * NA_COLS - 1)[:, None, None], jnp.float32)
    picked = jnp.einsum("hdj,jqc->hqdc", rpb, onehot, precision=lax.Precision.HIGHEST)
    t1 = jnp.where(col_mask[None, :, None, :], picked * LOG2E, NEG_BIG)
    flat = t1.reshape(N_HEADS_B, GRID_W, (2 * NA_ROWS - 1) * GRID_W)
    shifts = jnp.stack([flat[:, :, sh * GRID_W:sh * GRID_W + NAT_KEYS] for sh in range(NA_ROWS)], axis=1)
    tb = shifts.reshape(N_HEADS_B // 2, 2, NA_ROWS, GRID_W, NAT_KEYS).transpose(0, 2, 1, 3, 4)
    return tb.reshape(N_HEADS_B // 2, NA_ROWS, 2 * GRID_W, NAT_KEYS)


def _nat_kernel(q_ref, kp_ref, km_ref, kn_ref, vp_ref, vm_ref, vn_ref, tb_ref, o_ref, kcat, vcat,
                *, rows_seq):
    i = pl.program_id(1)
    halo = NAT_HALO_ROWS * GRID_W
    main = NAT_ROWS_PER_STEP * GRID_W
    kcat[0:halo] = kp_ref[...]
    kcat[halo:halo + main] = km_ref[...]
    kcat[halo + main:2 * halo + main] = kn_ref[...]
    vcat[0:halo] = vp_ref[...]
    vcat[halo:halo + main] = vm_ref[...]
    vcat[halo + main:2 * halo + main] = vn_ref[...]
    width = NAT_HEADS_PER_CHAIN * HEAD_DIM
    head_of_lane = lax.broadcasted_iota(jnp.int32, (1, width), 1) // HEAD_DIM
    zero = jnp.zeros((), jnp.bfloat16)
    r0 = i * NAT_ROWS_PER_STEP

    def scores(qr, c):
        r = r0 + qr
        rs = jnp.clip(r - NA_ROWS // 2, 0, rows_seq - NA_ROWS)
        koff = pl.multiple_of((rs - r0 + NAT_HALO_ROWS) * GRID_W, GRID_W)
        sh = rs - r + (NA_ROWS - 1)
        qoff = pl.multiple_of(qr * GRID_W, GRID_W)
        cols = slice(width * c, width * (c + 1))
        qc = q_ref[pl.ds(qoff, GRID_W), cols]
        qm = jnp.concatenate([jnp.where(head_of_lane == h, qc, zero) for h in range(NAT_HEADS_PER_CHAIN)], axis=0)
        kw = kcat[pl.ds(koff, NAT_KEYS), cols]
        s = lax.dot_general(qm, kw, (((1,), (1,)), ((), ())), preferred_element_type=jnp.float32)
        pairs = NAT_HEADS_PER_CHAIN // 2
        bias = jnp.concatenate([tb_ref[pairs * c + k, sh] for k in range(pairs)], axis=0)
        return s + bias, koff, qoff

    def attend(s, koff, qoff, c):
        cols = slice(width * c, width * (c + 1))
        vw = vcat[pl.ds(koff, NAT_KEYS), cols]
        m = jnp.max(s, axis=-1, keepdims=True)
        pe = jnp.exp2(s - m)
        l = jnp.sum(pe, axis=-1, keepdims=True)
        o2 = jnp.dot(pe.astype(jnp.bfloat16), vw, preferred_element_type=jnp.float32)
        o2 = o2 * (1.0 / l)
        out = o2[:GRID_W]
        for h in range(1, NAT_HEADS_PER_CHAIN):
            out = jnp.where(head_of_lane == h, o2[GRID_W * h:GRID_W * (h + 1)], out)
        o_ref[pl.ds(qoff, GRID_W), cols] = out.astype(jnp.bfloat16)

    def trip(j, carry):
        chains = [(j * NAT_ROWS_PER_TRIP + q, c) for q in range(NAT_ROWS_PER_TRIP)
                  for c in range(N_HEADS_B // NAT_HEADS_PER_CHAIN)]
        pending = [scores(*c) for c in chains[:NAT_LOOKAHEAD]]
        for idx, (_, p) in enumerate(chains):
            s, koff, qoff = pending.pop(0)
            if idx + NAT_LOOKAHEAD < len(chains):
                pending.append(scores(*chains[idx + NAT_LOOKAHEAD]))
            attend(s, koff, qoff, p)
        return carry

    lax.fori_loop(0, NAT_ROWS_PER_STEP // NAT_ROWS_PER_TRIP, trip, 0)


def _nat_attention(qkv, tb, bsz, t):
    n = bsz * t
    rows_seq = t // GRID_W
    main = NAT_ROWS_PER_STEP * GRID_W
    halo = NAT_HALO_ROWS * GRID_W
    ntile = t // main
    nhalo_seq = t // halo
    per = main // halo

    def main_map(col):
        return lambda b, i: (b * ntile + i, col)

    def prev_map(col):
        return lambda b, i: (b * nhalo_seq + jnp.maximum(i * per - 1, 0), col)

    def next_map(col):
        return lambda b, i: (b * nhalo_seq + jnp.minimum(i * per + per, nhalo_seq - 1), col)

    qb, kb, vb = QB_COL // WIDTH_B, KB_COL // WIDTH_B, VB_COL // WIDTH_B
    return pl.pallas_call(
        functools.partial(_nat_kernel, rows_seq=rows_seq),
        out_shape=jax.ShapeDtypeStruct((n, WIDTH_B), jnp.bfloat16),
        grid=(bsz, ntile),
        in_specs=[
            pl.BlockSpec((main, WIDTH_B), main_map(qb)),
            pl.BlockSpec((halo, WIDTH_B), prev_map(kb)),
            pl.BlockSpec((main, WIDTH_B), main_map(kb)),
            pl.BlockSpec((halo, WIDTH_B), next_map(kb)),
            pl.BlockSpec((halo, WIDTH_B), prev_map(vb)),
            pl.BlockSpec((main, WIDTH_B), main_map(vb)),
            pl.BlockSpec((halo, WIDTH_B), next_map(vb)),
            pl.BlockSpec((N_HEADS_B // 2, NA_ROWS, 2 * GRID_W, NAT_KEYS), lambda b, i: (0, 0, 0, 0)),
        ],
        out_specs=pl.BlockSpec((main, WIDTH_B), main_map(0)),
        scratch_shapes=[pltpu.VMEM((main + 2 * halo, WIDTH_B), jnp.bfloat16),
                        pltpu.VMEM((main + 2 * halo, WIDTH_B), jnp.bfloat16)],
        compiler_params=_cparams("parallel", "parallel"),
        name="nat",
    )(qkv, qkv, qkv, qkv, qkv, qkv, qkv, tb)


EXPERT_ROW0 = N_GROUPS
ROUTE_ROWS = 48
INFO_E1, INFO_E2, INFO_R1, INFO_R2, INFO_W1, INFO_W2 = range(6)
INFO_ROWS = 8
MERGE_SUBTILES = 4
RANK_CHUNK = 256


def _route(lt, carry, tri):
    rr, tm = lt.shape
    row = lax.broadcasted_iota(jnp.int32, (rr, tm), 0).astype(jnp.float32)
    none = jnp.float32(rr)

    def first_max(sel):
        m = jnp.max(jnp.where(sel, lt, NEG_BIG), axis=0, keepdims=True)
        idx = jnp.min(jnp.where(sel & (lt == m), row, none), axis=0, keepdims=True)
        return m, idx

    is_group = row < N_GROUPS
    mg, g = first_max(is_group)
    pg_sel = 1.0 / jnp.sum(jnp.where(is_group, jnp.exp(jnp.where(is_group, lt, mg) - mg), 0.0),
                           axis=0, keepdims=True)
    row0 = EXPERT_ROW0 + EXPERTS_PER_GROUP * g
    in_group = (row >= row0) & (row < row0 + EXPERTS_PER_GROUP)
    m1, i1 = first_max(in_group)
    m2, i2 = first_max(in_group & (row != i1))
    e2 = jnp.exp(m2 - m1)
    w1 = pg_sel / (1.0 + e2)
    w2 = pg_sel * e2 / (1.0 + e2)

    oh1 = row == i1
    oh2 = row == i2
    both = (oh1 | oh2).astype(jnp.bfloat16)
    chunk = tri.shape[0]
    before = []
    for c in range(0, tm, chunk):
        part = both[:, c:c + chunk]
        before.append(jnp.dot(part, tri, preferred_element_type=jnp.float32) + carry[:, :chunk])
        carry = carry + jnp.sum(part.astype(jnp.float32), axis=1, keepdims=True)
    before = jnp.concatenate(before, axis=1)
    r1 = jnp.sum(jnp.where(oh1, before, 0.0), axis=0, keepdims=True)
    r2 = jnp.sum(jnp.where(oh2, before, 0.0), axis=0, keepdims=True)
    new_carry = carry

    field = lax.broadcasted_iota(jnp.int32, (INFO_ROWS, tm), 0)
    info = jnp.zeros((INFO_ROWS, tm), jnp.float32)
    for k, v in ((INFO_E1, i1 - EXPERT_ROW0), (INFO_E2, i2 - EXPERT_ROW0), (INFO_R1, r1), (INFO_R2, r2),
                 (INFO_W1, w1), (INFO_W2, w2)):
        info = jnp.where(field == k, v, info)
    return info, new_carry


def _merge_kernel(x_ref, oa_ref, ob_ref, lng_ref, lnb_ref, wg_ref, wpa_ref, wpb_ref, wo_ref,
                  l1g_ref, l1b_ref, wr_ref, br_ref, cnt0_ref,
                  h1_ref, h1p_ref, info_ref, cnt_ref, carry_ref, tri_ref, pre_ref):
    tm = x_ref.shape[0]
    step = pl.program_id(0)
    last = pl.num_programs(0) - 1

    @pl.when(step == 0)
    def _():
        carry_ref[...] = cnt0_ref[...]
        r = lax.broadcasted_iota(jnp.int32, tri_ref.shape, 0)
        c = lax.broadcasted_iota(jnp.int32, tri_ref.shape, 1)
        tri_ref[...] = (r < c).astype(jnp.bfloat16)

    def project(rows):
        h = _layer_norm(x_ref[rows], lng_ref[...], lnb_ref[...])
        gates = jnp.dot(h.astype(jnp.bfloat16), wg_ref[...], preferred_element_type=jnp.float32)
        pa = jnp.dot(oa_ref[rows], wpa_ref[...], preferred_element_type=jnp.float32)
        pb = jnp.dot(ob_ref[rows], wpb_ref[...], preferred_element_type=jnp.float32)
        return h, gates, pa, pb

    def mix(h, gates, pa, pb):
        mixin = jax.nn.sigmoid(gates[:, :D_MODEL]) * pa + jax.nn.sigmoid(gates[:, D_MODEL:]) * pb
        return DEEPNORM_ALPHA * h + jnp.dot(mixin.astype(jnp.bfloat16), wo_ref[...],
                                            preferred_element_type=jnp.float32)

    def norm_and_logits(pre, rows):
        h1 = _layer_norm(pre, l1g_ref[...], l1b_ref[...])
        h1_ref[rows] = h1
        h1p_ref[rows] = _pack_rows(h1)
        return lax.dot_general(wr_ref[...], h1.astype(jnp.bfloat16), (((1,), (1,)), ((), ())),
                               preferred_element_type=jnp.float32)

    sub = tm // MERGE_SUBTILES
    parts = [slice(k * sub, (k + 1) * sub) for k in range(MERGE_SUBTILES)]

    def logits_of_previous(k):
        return norm_and_logits(pre_ref[parts[k]], parts[k])

    def route_previous(logits):
        logits_t = jnp.concatenate(logits, axis=1) + br_ref[...]
        info, carry = _route(logits_t, carry_ref[...], tri_ref[...])
        info_ref[...] = info
        carry_ref[...] = carry
        cnt_ref[...] = carry[:, :LANES]

    @pl.when(step == 0)
    def _():
        for pr, rows in zip([project(rows) for rows in parts], parts):
            pre_ref[rows] = mix(*pr)

    @pl.when(jnp.logical_and(step > 0, step < last))
    def _():
        projected = [project(parts[0])]
        logits = []
        for k in range(MERGE_SUBTILES):
            if k + 1 < MERGE_SUBTILES:
                projected.append(project(parts[k + 1]))
            else:
                pre_ref[parts[0]] = mix(*projected[0])
            logits.append(logits_of_previous(k))
        pre_ref[parts[1]] = mix(*projected[1])
        route_previous(logits)
        for k in range(2, MERGE_SUBTILES):
            pre_ref[parts[k]] = mix(*projected[k])

    @pl.when(step == last)
    def _():
        route_previous([logits_of_previous(k) for k in range(MERGE_SUBTILES)])


def _merge(x2, oa, ob, ln_g, ln_b, w_gates, w_pa, w_pb, w_o, l1g, l1b, w_r, b_r, cnt0, tm):
    n = x2.shape[0]
    ntiles = n // tm

    def const(shape):
        return pl.BlockSpec(shape, lambda i: (0,) * len(shape))

    def rows(width):
        return pl.BlockSpec((tm, width), lambda i: (jnp.minimum(i, ntiles - 1), 0))

    def rows_out(width):
        return pl.BlockSpec((tm, width), lambda i: (jnp.maximum(i - 1, 0), 0))

    return pl.pallas_call(
        _merge_kernel,
        out_shape=(jax.ShapeDtypeStruct((n, D_MODEL), jnp.float32),
                   jax.ShapeDtypeStruct((n, PACKED_WIDTH), jnp.uint32),
                   jax.ShapeDtypeStruct((INFO_ROWS, n), jnp.float32),
                   jax.ShapeDtypeStruct((ROUTE_ROWS, LANES), jnp.float32)),
        grid=(ntiles + 1,),
        in_specs=[
            rows(D_MODEL), rows(WIDTH_A), rows(WIDTH_B),
            const((1, D_MODEL)), const((1, D_MODEL)),
            const((D_MODEL, 2 * D_MODEL)),
            const((WIDTH_A, D_MODEL)), const((WIDTH_B, D_MODEL)),
            const((D_MODEL, D_MODEL)),
            const((1, D_MODEL)), const((1, D_MODEL)),
            const((ROUTE_ROWS, D_MODEL)), const((ROUTE_ROWS, tm)), const((ROUTE_ROWS, tm)),
        ],
        out_specs=(rows_out(D_MODEL), rows_out(PACKED_WIDTH),
                   pl.BlockSpec((INFO_ROWS, tm), lambda i: (0, jnp.maximum(i - 1, 0))),
                   const((ROUTE_ROWS, LANES))),
        scratch_shapes=[pltpu.VMEM((ROUTE_ROWS, tm), jnp.float32),
                        pltpu.VMEM((RANK_CHUNK, RANK_CHUNK), jnp.bfloat16),
                        pltpu.VMEM((tm, D_MODEL), jnp.float32)],
        compiler_params=_cparams("arbitrary"),
        name="merge",
    )(x2, oa, ob, ln_g, ln_b, w_gates, w_pa, w_pb, w_o, l1g, l1b, w_r, b_r, cnt0)


SC_CORES = 2
SC_SUBCORES = 16
SC_WORKERS = SC_CORES * SC_SUBCORES
SC_ROWS_PER_STREAM = 64


def _sc_worker():
    return lax.axis_index("s") * SC_CORES + lax.axis_index("c")


def _scatter_indices(dest_t, pad_rows):
    n = dest_t.shape[1]
    nchunks = n // (SC_WORKERS * SC_ROWS_PER_STREAM)
    npad = pad_rows.size // (SC_WORKERS * SC_ROWS_PER_STREAM)
    assert nchunks * SC_ROWS_PER_STREAM * SC_WORKERS == n and nchunks % 2 == 0
    assert npad * SC_WORKERS * SC_ROWS_PER_STREAM == pad_rows.size
    return (dest_t.reshape(TOP_K, SC_WORKERS, nchunks, SC_ROWS_PER_STREAM),
            pad_rows.reshape(SC_WORKERS, npad, SC_ROWS_PER_STREAM))


def _dispatch(h1p, idx, pad_idx, zeros, nrows_out):
    n, width = h1p.shape
    per_worker = n // SC_WORKERS
    nchunks, npad = idx.shape[2], pad_idx.shape[1]
    assert zeros.shape == (SC_ROWS_PER_STREAM, width) and zeros.dtype == h1p.dtype
    mesh = plsc.VectorSubcoreMesh(core_axis_name="c", subcore_axis_name="s")

    @functools.partial(
        pl.kernel, out_type=jax.ShapeDtypeStruct((nrows_out, width), h1p.dtype), mesh=mesh,
        scratch_types=[pltpu.VMEM((TOP_K, nchunks, SC_ROWS_PER_STREAM), jnp.int32),
                       pltpu.VMEM((npad, SC_ROWS_PER_STREAM), jnp.int32),
                       pltpu.VMEM((2, SC_ROWS_PER_STREAM, width), h1p.dtype),
                       pltpu.SemaphoreType.DMA((2,)), pltpu.SemaphoreType.DMA((2,))],
        name="sc_dispatch")
    def scatter_kernel(src_hbm, idx_hbm, pad_hbm, zeros_hbm, out_hbm, idx_v, pad_v, rows_v, rsem, ssem):
        wid = _sc_worker()
        base = wid * per_worker
        for k in range(TOP_K):
            pltpu.sync_copy(idx_hbm.at[k, wid], idx_v.at[k])
        pltpu.sync_copy(pad_hbm.at[wid], pad_v)

        def read(j, slot):
            src = src_hbm.at[pl.ds(base + j * SC_ROWS_PER_STREAM, SC_ROWS_PER_STREAM)]
            return pltpu.make_async_copy(src, rows_v.at[slot], rsem.at[slot])

        def scatter(j, slot, k):
            return pltpu.make_async_copy(rows_v.at[slot], out_hbm.at[idx_v.at[k, j]], ssem.at[slot])

        read(0, 0).start()

        @pl.loop(0, nchunks, step=2)
        def _(j0):
            for slot in range(2):
                j = j0 + slot
                read(j, slot).wait()

                @pl.when(j + 1 < nchunks)
                def _():
                    @pl.when(j >= 1)
                    def _():
                        for k in range(TOP_K):
                            scatter(j - 1, 1 - slot, k).wait()
                    read(j + 1, 1 - slot).start()

                for k in range(TOP_K):
                    scatter(j, slot, k).start()

        for k in range(TOP_K):
            scatter(nchunks - 2, 0, k).wait()
            scatter(nchunks - 1, 1, k).wait()

        pltpu.sync_copy(zeros_hbm, rows_v.at[0])
        fills = [pltpu.make_async_copy(rows_v.at[0], out_hbm.at[pad_v.at[c]], ssem.at[0]) for c in range(npad)]
        for f in fills:
            f.start()
        for f in fills:
            f.wait()

    return scatter_kernel(h1p, idx, pad_idx, zeros)


def _sorted_layout(counts, eid, rank, bm):
    n = eid.shape[1]
    nblocks = TOP_K * n // bm + N_EXPERTS
    expert = jnp.arange(N_EXPERTS, dtype=jnp.int32)
    before = expert[None, :] < expert[:, None]
    blocks = (counts + bm - 1) // bm
    first_blk = jnp.sum(jnp.where(before, blocks[None, :], 0), axis=1)
    starts = first_blk * bm
    dest = rank + jnp.sum(jnp.where(eid[None] == expert[:, None, None], starts[:, None, None], 0), axis=0)
    total = jnp.sum(blocks)
    j = jnp.arange(bm, dtype=jnp.int32)[None, :]
    npad = blocks * bm - counts
    spare_before = jnp.sum(jnp.where(before, (bm - npad)[None, :], 0), axis=1)
    pad_rows = jnp.where(j < npad[:, None], (starts + counts)[:, None] + j,
                         (total * bm + spare_before - npad)[:, None] + j).astype(jnp.int32)
    w = jnp.arange(nblocks, dtype=jnp.int32)
    real = jnp.minimum(w, total - 1)
    blk_end = first_blk + blocks
    owner = jnp.minimum(jnp.sum((blk_end[None, :] <= real[:, None]).astype(jnp.int32), axis=1), N_EXPERTS - 1)
    valid = (w < total).astype(jnp.int32)
    return dest.astype(jnp.int32), pad_rows, (owner.astype(jnp.int32), valid)


EXPERT_BLOCKS_PER_STEP = 2


def _expert_kernel(e_ref, valid_ref, x_ref, *refs):
    o_ref = refs[-1]
    bm = x_ref.shape[0] // EXPERT_BLOCKS_PER_STEP
    first = pl.program_id(0) * EXPERT_BLOCKS_PER_STEP

    def compute(i):
        wg_ref, wu_ref, wd_ref = refs[3 * i:3 * i + 3]
        x = _unpack_rows(x_ref[i * bm:(i + 1) * bm]).astype(jnp.bfloat16)
        g = jnp.dot(x, wg_ref[0], preferred_element_type=jnp.float32)
        u = jnp.dot(x, wu_ref[0], preferred_element_type=jnp.float32)
        hmid = (jax.nn.silu(g) * u).astype(jnp.bfloat16)
        o_ref[i * bm:(i + 1) * bm] = _pack_rows(jnp.dot(hmid, wd_ref[0], preferred_element_type=jnp.float32))

    def clear(i):
        o_ref[i * bm:(i + 1) * bm] = jnp.zeros((bm, PACKED_WIDTH), o_ref.dtype)

    nvalid = sum(valid_ref[first + i] for i in range(EXPERT_BLOCKS_PER_STEP))
    for k in range(EXPERT_BLOCKS_PER_STEP + 1):
        @pl.when(nvalid == k)
        def _(k=k):
            for i in range(EXPERT_BLOCKS_PER_STEP):
                compute(i) if i < k else clear(i)


def _experts(xs, plan, w_gate, w_up, w_down, bm):
    nblocks = plan[0].shape[0]
    nsteps = nblocks // EXPERT_BLOCKS_PER_STEP
    assert xs.shape[0] == nblocks * bm and nsteps * EXPERT_BLOCKS_PER_STEP == nblocks
    step_rows = EXPERT_BLOCKS_PER_STEP * bm

    def weights(shape, i):
        return pl.BlockSpec((1,) + shape, lambda w, e, valid: (e[w * EXPERT_BLOCKS_PER_STEP + i], 0, 0))

    grid_spec = pltpu.PrefetchScalarGridSpec(
        num_scalar_prefetch=len(plan),
        grid=(nsteps,),
        in_specs=[pl.BlockSpec((step_rows, PACKED_WIDTH), lambda w, *_: (w, 0))] + [
            weights(shape, i) for i in range(EXPERT_BLOCKS_PER_STEP)
            for shape in ((D_MODEL, D_EXPERT), (D_MODEL, D_EXPERT), (D_EXPERT, D_MODEL))],
        out_specs=pl.BlockSpec((step_rows, PACKED_WIDTH), lambda w, *_: (w, 0)),
    )
    return pl.pallas_call(
        _expert_kernel,
        out_shape=jax.ShapeDtypeStruct(xs.shape, jnp.uint32),
        grid_spec=grid_spec,
        compiler_params=_cparams("parallel"),
        name="experts",
    )(*plan, xs, *([w_gate, w_up, w_down] * EXPERT_BLOCKS_PER_STEP))


def _sc_gather_rows(table, idx):
    nrows = idx.shape[0]
    width = table.shape[1]
    per_worker = nrows // SC_WORKERS
    nchunks = per_worker // SC_ROWS_PER_STREAM
    assert nchunks * SC_ROWS_PER_STREAM * SC_WORKERS == nrows and nchunks % 2 == 0
    mesh = plsc.VectorSubcoreMesh(core_axis_name="c", subcore_axis_name="s")

    @functools.partial(
        pl.kernel, out_type=jax.ShapeDtypeStruct((nrows, width), table.dtype), mesh=mesh,
        scratch_types=[pltpu.VMEM((per_worker,), jnp.int32),
                       pltpu.VMEM((2, SC_ROWS_PER_STREAM, width), table.dtype),
                       pltpu.SemaphoreType.DMA((2,)), pltpu.SemaphoreType.DMA((2,))],
        name="sc_gather")
    def gather_kernel(table_hbm, idx_hbm, out_hbm, idx_v, rows_v, gsem, wsem):
        base = _sc_worker() * per_worker
        pltpu.sync_copy(idx_hbm.at[pl.ds(base, per_worker)], idx_v)

        def gather(j, slot):
            rows = idx_v.at[pl.ds(j * SC_ROWS_PER_STREAM, SC_ROWS_PER_STREAM)]
            return pltpu.make_async_copy(table_hbm.at[rows], rows_v.at[slot], gsem.at[slot])

        def write(j, slot):
            dst = out_hbm.at[pl.ds(base + j * SC_ROWS_PER_STREAM, SC_ROWS_PER_STREAM)]
            return pltpu.make_async_copy(rows_v.at[slot], dst, wsem.at[slot])

        gather(0, 0).start()

        @pl.loop(0, nchunks, step=2)
        def _(j0):
            for slot in range(2):
                j = j0 + slot
                gather(j, slot).wait()

                @pl.when(j + 1 < nchunks)
                def _():
                    @pl.when(j >= 1)
                    def _():
                        write(j - 1, 1 - slot).wait()
                    gather(j + 1, 1 - slot).start()

                write(j, slot).start()

        write(nchunks - 2, 0).wait()
        write(nchunks - 1, 1).wait()

    return gather_kernel(table, idx)


def _finalize_kernel(h1_ref, y1_ref, y2_ref, info_ref, g_ref, b_ref, o_ref):
    tt = h1_ref.shape[0]
    pad = jnp.zeros((LANES - INFO_ROWS, tt), jnp.float32)
    info = jnp.concatenate([info_ref[...], pad], axis=0).T
    moe = (_unpack_rows(y1_ref[...]) * info[:, INFO_W1:INFO_W1 + 1]
           + _unpack_rows(y2_ref[...]) * info[:, INFO_W2:INFO_W2 + 1])
    o_ref[...] = _layer_norm(DEEPNORM_ALPHA * h1_ref[...] + moe, g_ref[...], b_ref[...])


def _combine(h1, info, dest_flat, ys, ln_g, ln_b, tt):
    n = h1.shape[0]
    nsteps = n // tt
    yg = _sc_gather_rows(ys, dest_flat)
    return pl.pallas_call(
        _finalize_kernel,
        out_shape=jax.ShapeDtypeStruct((n, D_MODEL), jnp.float32),
        grid=(nsteps,),
        in_specs=[
            pl.BlockSpec((tt, D_MODEL), lambda i: (i, 0)),
            pl.BlockSpec((tt, PACKED_WIDTH), lambda i: (i, 0)),
            pl.BlockSpec((tt, PACKED_WIDTH), lambda i: (nsteps + i, 0)),
            pl.BlockSpec((INFO_ROWS, tt), lambda i: (0, i)),
            pl.BlockSpec((1, D_MODEL), lambda i: (0, 0)),
            pl.BlockSpec((1, D_MODEL), lambda i: (0, 0)),
        ],
        out_specs=pl.BlockSpec((tt, D_MODEL), lambda i: (i, 0)),
        compiler_params=_cparams("parallel"),
        name="finalize",
    )(h1, yg, yg, info, ln_g, ln_b)


TM_QKV = 1024
TQ_WIN = 1024
TM_MERGE = 1024
TT_ROWS = 1024
BM_EXPERT = 512


def _prepare_weights(ln_in_g, ln_in_b, w_in, attn_sink, rel_pos_bias, w_proj_a, w_proj_b, w_out,
                     ln1_g, ln1_b, w_route_group, b_route_group, w_route_expert, b_route_expert,
                     ln2_g, ln2_b):
    bf = jnp.bfloat16
    w = w_in[0]
    splits = np.cumsum([WIDTH_A, KV_WIDTH_A, KV_WIDTH_A, WIDTH_B, WIDTH_B, WIDTH_B, D_MODEL])
    wqa, wka, wva, wqb, wkb, wvb, wga, wgb = jnp.split(w, [int(s) for s in splits], axis=1)
    wqa = (wqa.reshape(D_MODEL, N_KV_HEADS_A, GQA_GROUP, HEAD_DIM).transpose(0, 2, 1, 3)
           .reshape(D_MODEL, WIDTH_A))
    w_qkv = jnp.concatenate([wqa, wqb, wkb, wvb, wka, wva], axis=1).astype(bf)
    w_gates = jnp.concatenate([wga, wgb], axis=1).astype(bf)
    w_pa = (w_proj_a[0].reshape(N_KV_HEADS_A, GQA_GROUP, HEAD_DIM, D_MODEL).transpose(1, 0, 2, 3)
            .reshape(WIDTH_A, D_MODEL).astype(bf))
    w_pb = w_proj_b[0].astype(bf)
    w_o = w_out[0].astype(bf)
    pad = ROUTE_ROWS - N_GROUPS - N_EXPERTS
    w_r = jnp.concatenate([w_route_group[0].T, w_route_expert[0].T, jnp.zeros((pad, D_MODEL), jnp.float32)], axis=0)
    w_route = w_r.astype(bf)
    b_r = jnp.concatenate([b_route_group[0], b_route_expert[0], jnp.zeros((pad,), jnp.float32)])
    b_r = jnp.broadcast_to(b_r[:, None], (ROUTE_ROWS, TM_MERGE))
    row = lambda v: v.reshape(1, D_MODEL)
    return dict(
        ln_in_g=row(ln_in_g), ln_in_b=row(ln_in_b), w_qkv=w_qkv, w_gates=w_gates,
        sink=attn_sink[0].astype(jnp.float32), nat_bias=_nat_bias_table(rel_pos_bias[0]),
        w_pa=w_pa, w_pb=w_pb, w_o=w_o, ln1_g=row(ln1_g[0]), ln1_b=row(ln1_b[0]),
        w_route=w_route, b_r=b_r, ln2_g=row(ln2_g[0]), ln2_b=row(ln2_b[0]))


def _after(value, other):
    if other is None:
        return value
    other = other.astype(jnp.float32)
    zero = jnp.where(jnp.isfinite(other), other, 0.0) * 0.0
    return value + zero.astype(value.dtype)


def _attend_and_route(x, p, after=None, cast=()):
    bsz, t, _ = x.shape
    n = bsz * t
    x2 = x.reshape(n, D_MODEL)
    qkv, cast_bf16 = _qkv(x2, _after(p["ln_in_g"], after), p["ln_in_b"], p["w_qkv"], TM_QKV, cast)
    oa = _win_attention(qkv, p["sink"], bsz, t, TQ_WIN)
    ob = _nat_attention(qkv, p["nat_bias"], bsz, t)
    cnt0 = jnp.zeros((ROUTE_ROWS, TM_MERGE), jnp.float32)
    h1, h1p, info, cnt = _merge(x2, oa, ob, p["ln_in_g"], p["ln_in_b"], p["w_gates"], p["w_pa"], p["w_pb"],
                                p["w_o"], p["ln1_g"], p["ln1_b"], p["w_route"], p["b_r"], cnt0, TM_MERGE)
    counts = cnt[EXPERT_ROW0:EXPERT_ROW0 + N_EXPERTS, 0].astype(jnp.int32)
    eid = info[INFO_E1:INFO_E2 + 1].astype(jnp.int32)
    rank = info[INFO_R1:INFO_R2 + 1].astype(jnp.int32)
    dest_t, pad_rows, plan = _sorted_layout(counts, eid, rank, BM_EXPERT)
    zero_rows = jnp.zeros((SC_ROWS_PER_STREAM, PACKED_WIDTH), jnp.uint32)
    moves = lax.optimization_barrier(_scatter_indices(dest_t, pad_rows) + (zero_rows, dest_t.reshape(TOP_K * n)))
    return dict(shape=x.shape, h1=h1, h1p=h1p, info=info, counts=counts, moves=moves, plan=plan, cast=cast_bf16)


def _run_experts(r, expert_weights, after=None):
    owner, valid = r["plan"]
    scatter_idx, pad_idx, zero_rows, _ = r["moves"]
    xs = _dispatch(r["h1p"], scatter_idx, pad_idx, zero_rows, owner.shape[0] * BM_EXPERT)
    return _experts(xs, (owner, _after(valid, after)), *expert_weights, BM_EXPERT)


def _finish(r, ys, p, after=None):
    out = _combine(r["h1"], r["info"], r["moves"][3], ys, _after(p["ln2_g"], after), p["ln2_b"], TT_ROWS)
    return out.reshape(r["shape"])


def kernel(x_prompt, x_sample, ln_in_g, ln_in_b, w_in, attn_sink, rel_pos_bias, w_proj_a, w_proj_b, w_out,
           ln1_g, ln1_b, w_route_group, b_route_group, w_route_expert, b_route_expert,
           w_gate, w_up, w_down, ln2_g, ln2_b):
    p = _prepare_weights(ln_in_g, ln_in_b, w_in, attn_sink, rel_pos_bias, w_proj_a, w_proj_b, w_out,
                         ln1_g, ln1_b, w_route_group, b_route_group, w_route_expert, b_route_expert,
                         ln2_g, ln2_b)
    rp = _attend_and_route(x_prompt, p, cast=(w_gate[0], w_up[0], w_down[0]))
    glue_done = sum(a.reshape(-1)[0].astype(jnp.float32) for a in rp["moves"] + rp["plan"])
    rs = _attend_and_route(x_sample, p, after=glue_done)
    ys_p = _run_experts(rp, rp["cast"])
    ys_s = _run_experts(rs, rp["cast"], after=ys_p[0, 0])
    y_prompt = _finish(rp, ys_p, p)
    y_sample = _finish(rs, ys_s, p, after=y_prompt[0, 0, 0])
    return (y_prompt, y_sample)
```

```python
import functools

import numpy as np
import jax
import jax.numpy as jnp
from jax import lax
from jax.experimental import pallas as pl
from jax.experimental.pallas import tpu as pltpu
from jax.experimental.pallas import tpu_sc as plsc

D_MODEL = 1024
HEAD_DIM = 64
N_HEADS_A = 8
N_KV_HEADS_A = 2
WINDOW = 128
N_HEADS_B = 8
GRID_W = 64
NA_ROWS = 8
NA_COLS = 16
N_GROUPS = 4
EXPERTS_PER_GROUP = 8
N_EXPERTS = N_GROUPS * EXPERTS_PER_GROUP
TOP_K = 2
D_EXPERT = D_MODEL // 2
LN_EPS = 1e-5
DEPTH = 1
DEEPNORM_ALPHA = (2.0 * DEPTH) ** 0.25
WIDTH_A = N_HEADS_A * HEAD_DIM
KV_WIDTH_A = N_KV_HEADS_A * HEAD_DIM
WIDTH_B = N_HEADS_B * HEAD_DIM
QKV_WIDTH = WIDTH_A + 2 * KV_WIDTH_A + 3 * WIDTH_B

LANES = 128
VMEM_LIMIT_BYTES = 56 * 1024 * 1024

NEG_BIG = -1e30
LOG2E = float(np.log2(np.e))

QA_COL, QB_COL, KB_COL, VB_COL = 0, WIDTH_A, WIDTH_A + WIDTH_B, WIDTH_A + 2 * WIDTH_B
KA_COL = WIDTH_A + 3 * WIDTH_B
VA_COL = KA_COL + KV_WIDTH_A

GQA_GROUP = N_HEADS_A // N_KV_HEADS_A


def _cparams(*sem):
    return pltpu.CompilerParams(dimension_semantics=sem, vmem_limit_bytes=VMEM_LIMIT_BYTES)


def _layer_norm(x, g, b):
    mu = jnp.mean(x, axis=-1, keepdims=True)
    xc = x - mu
    var = jnp.mean(xc * xc, axis=-1, keepdims=True)
    return xc * lax.rsqrt(var + LN_EPS) * g + b


PACKED_WIDTH = D_MODEL // 2


def _pack_rows(x):
    def bits(v):
        return lax.bitcast_convert_type(v.astype(jnp.bfloat16).astype(jnp.float32), jnp.uint32)
    return bits(x[:, :PACKED_WIDTH]) | (bits(x[:, PACKED_WIDTH:]) >> 16)


def _unpack_rows(w):
    hi = lax.bitcast_convert_type(w & jnp.uint32(0xFFFF0000), jnp.float32)
    lo = lax.bitcast_convert_type(w << 16, jnp.float32)
    return jnp.concatenate([hi, lo], axis=1)


QKV_SUBTILES = 4


def _qkv_kernel(x_ref, g_ref, b_ref, w_ref, *rest):
    o_ref = rest[len(rest) // 2]
    col = lax.broadcasted_iota(jnp.int32, (1, QKV_WIDTH), 1)
    scale = jnp.where(col < KB_COL, HEAD_DIM ** -0.5 * LOG2E, 1.0)
    sub = x_ref.shape[0] // QKV_SUBTILES
    for k in range(QKV_SUBTILES):
        rows = slice(k * sub, (k + 1) * sub)
        h = _layer_norm(x_ref[rows], g_ref[...], b_ref[...])
        y = jnp.dot(h.astype(jnp.bfloat16), w_ref[...], preferred_element_type=jnp.float32)
        o_ref[rows] = (y * scale).astype(jnp.bfloat16)
    ncast = len(rest) // 2
    for src_ref, dst_ref in zip(rest[:ncast], rest[ncast + 1:]):
        dst_ref[...] = src_ref[...].astype(jnp.bfloat16)


def _qkv(x2, ln_g, ln_b, w_qkv, tm, cast=()):
    n = x2.shape[0]
    steps = n // tm
    per_step = -(-N_EXPERTS // steps)
    assert all(c.shape[0] == N_EXPERTS for c in cast) and (steps * per_step) % N_EXPERTS == 0
    revisit = steps * per_step // N_EXPERTS

    def expert_block(c):
        return pl.BlockSpec((per_step,) + c.shape[1:], lambda i: (i // revisit, 0, 0))

    outs = pl.pallas_call(
        _qkv_kernel,
        out_shape=[jax.ShapeDtypeStruct((n, QKV_WIDTH), jnp.bfloat16)]
        + [jax.ShapeDtypeStruct(c.shape, jnp.bfloat16) for c in cast],
        grid=(steps,),
        in_specs=[
            pl.BlockSpec((tm, D_MODEL), lambda i: (i, 0)),
            pl.BlockSpec((1, D_MODEL), lambda i: (0, 0)),
            pl.BlockSpec((1, D_MODEL), lambda i: (0, 0)),
            pl.BlockSpec((D_MODEL, QKV_WIDTH), lambda i: (0, 0)),
        ] + [expert_block(c) for c in cast],
        out_specs=[pl.BlockSpec((tm, QKV_WIDTH), lambda i: (i, 0))] + [expert_block(c) for c in cast],
        compiler_params=_cparams("arbitrary" if cast else "parallel"),
        name="qkv",
    )(x2, ln_g, ln_b, w_qkv, *cast)
    return outs[0], tuple(outs[1:])


WIN_BLK = 128
WIN_LOOKAHEAD = 2


def _win_bias_table():
    qi = np.arange(WIN_BLK)[:, None]
    kj = np.arange(3 * WIN_BLK)[None, :]
    dist = np.abs(kj - WIN_BLK - qi).astype(np.float64)
    slopes = 2.0 ** (-8.0 * np.arange(1, N_HEADS_A + 1) / N_HEADS_A)
    per_head = np.where(dist <= WINDOW, -slopes[:, None, None] * dist[None] * LOG2E, NEG_BIG)
    groups = [np.concatenate([per_head[j], per_head[j + 4]], axis=0) for j in range(4)]
    return np.stack(groups).astype(np.float32)


def _win_kernel(sink_ref, q_ref, kp_ref, km_ref, kn_ref, vp_ref, vm_ref, vn_ref, bias_ref, o_ref,
                *, nsub, nblk_seq):
    i = pl.program_id(1)
    kcat = jnp.concatenate([kp_ref[...], km_ref[...], kn_ref[...]], axis=0)
    vcat = jnp.concatenate([vp_ref[...], vm_ref[...], vn_ref[...]], axis=0)
    lo = lax.broadcasted_iota(jnp.int32, (1, LANES), 1) < HEAD_DIM
    col = lax.broadcasted_iota(jnp.int32, (1, 3 * WIN_BLK), 1)
    top = lax.broadcasted_iota(jnp.int32, (2 * WIN_BLK, 1), 0) < WIN_BLK
    zero = jnp.zeros((), jnp.bfloat16)

    def scores(j, g):
        n = i * nsub + j
        off_seq = ((col < WIN_BLK) & (n == 0)) | ((col >= 2 * WIN_BLK) & (n == nblk_seq - 1))
        edge = jnp.where(off_seq, NEG_BIG, 0.0)
        qg = q_ref[WIN_BLK * j:WIN_BLK * (j + 1), LANES * g:LANES * (g + 1)]
        qm = jnp.concatenate([jnp.where(lo, qg, zero), jnp.where(lo, zero, qg)], axis=0)
        kj = kcat[WIN_BLK * j:WIN_BLK * (j + 3)]
        s = lax.dot_general(qm, kj, (((1,), (1,)), ((), ())), preferred_element_type=jnp.float32)
        return s + bias_ref[g] + edge

    def attend(s, j, g):
        vj = vcat[WIN_BLK * j:WIN_BLK * (j + 3)]
        sink = jnp.where(top, sink_ref[g], sink_ref[g + 4]) * LOG2E
        m = jnp.maximum(jnp.max(s, axis=-1, keepdims=True), sink)
        p = jnp.exp2(s - m)
        l = jnp.sum(p, axis=-1, keepdims=True) + jnp.exp2(sink - m)
        o2 = jnp.dot(p.astype(jnp.bfloat16), vj, preferred_element_type=jnp.float32)
        o2 = o2 * (1.0 / l)
        o_ref[WIN_BLK * j:WIN_BLK * (j + 1), LANES * g:LANES * (g + 1)] = (
            jnp.where(lo, o2[:WIN_BLK], o2[WIN_BLK:]).astype(jnp.bfloat16))

    chains = [(j, g) for j in range(nsub) for g in range(4)]
    pending = [scores(*c) for c in chains[:WIN_LOOKAHEAD]]
    for idx, c in enumerate(chains):
        s = pending.pop(0)
        if idx + WIN_LOOKAHEAD < len(chains):
            pending.append(scores(*chains[idx + WIN_LOOKAHEAD]))
        attend(s, *c)


def _win_attention(qkv, sink, bsz, t, tq):
    n = bsz * t
    nsub = tq // WIN_BLK
    nblk_seq = t // WIN_BLK
    ntile = t // tq
    bias = jnp.asarray(_win_bias_table())

    def main_map(col):
        return lambda b, i, *_: (b * ntile + i, col)

    def prev_map(col):
        return lambda b, i, *_: (b * nblk_seq + jnp.maximum(i * nsub - 1, 0), col)

    def next_map(col):
        return lambda b, i, *_: (b * nblk_seq + jnp.minimum(i * nsub + nsub, nblk_seq - 1), col)

    halo = (WIN_BLK, LANES)
    ka, va = KA_COL // LANES, VA_COL // LANES
    grid_spec = pltpu.PrefetchScalarGridSpec(
        num_scalar_prefetch=1,
        grid=(bsz, ntile),
        in_specs=[
            pl.BlockSpec((tq, WIDTH_A), main_map(QA_COL // WIDTH_A)),
            pl.BlockSpec(halo, prev_map(ka)),
            pl.BlockSpec((tq, LANES), main_map(ka)),
            pl.BlockSpec(halo, next_map(ka)),
            pl.BlockSpec(halo, prev_map(va)),
            pl.BlockSpec((tq, LANES), main_map(va)),
            pl.BlockSpec(halo, next_map(va)),
            pl.BlockSpec((4, 2 * WIN_BLK, 3 * WIN_BLK), lambda b, i, *_: (0, 0, 0)),
        ],
        out_specs=pl.BlockSpec((tq, WIDTH_A), main_map(0)),
    )
    return pl.pallas_call(
        functools.partial(_win_kernel, nsub=nsub, nblk_seq=nblk_seq),
        out_shape=jax.ShapeDtypeStruct((n, WIDTH_A), jnp.bfloat16),
        grid_spec=grid_spec,
        compiler_params=_cparams("parallel", "parallel"),
        name="win",
    )(sink, qkv, qkv, qkv, qkv, qkv, qkv, qkv, bias)


NAT_ROWS_PER_STEP = 16
NAT_HALO_ROWS = NA_ROWS // 2
NAT_KEYS = NA_ROWS * GRID_W
NAT_ROWS_PER_TRIP = 8
NAT_HEADS_PER_CHAIN = 2
NAT_LOOKAHEAD = 4


def _nat_bias_table(rpb):
    c = np.arange(GRID_W)
    cs = np.clip(c - NA_COLS // 2, 0, GRID_W - NA_COLS)
    col_mask = (c[None, :] >= cs[:, None]) & (c[None, :] < cs[:, None] + NA_COLS)
    dc = np.clip(c[None, :] - c[:, None] + (NA_COLS - 1), 0, 2 * NA_COLS - 2)
    onehot = jnp.asarray(dc[None] == np.arange(2 * NA_COLS - 1)[:, None, None], jnp.float32)
    picked = jnp.einsum("hdj,jqc->hqdc", rpb, onehot, precision=lax.Precision.HIGHEST)
    t1 = jnp.where(col_mask[None, :, None, :], picked * LOG2E, NEG_BIG)
    flat = t1.reshape(N_HEADS_B, GRID_W, (2 * NA_ROWS - 1) * GRID_W)
    shifts = jnp.stack([flat[:, :, sh * GRID_W:sh * GRID_W + NAT_KEYS] for sh in range(NA_ROWS)], axis=1)
    tb = shifts.reshape(N_HEADS_B // 2, 2, NA_ROWS, GRID_W, NAT_KEYS).transpose(0, 2, 1, 3, 4)
    return tb.reshape(N_HEADS_B // 2, NA_ROWS, 2 * GRID_W, NAT_KEYS)


def _nat_kernel(q_ref, kp_ref, km_ref, kn_ref, vp_ref, vm_ref, vn_ref, tb_ref, o_ref, kcat, vcat,
                *, rows_seq):
    i = pl.program_id(1)
    halo = NAT_HALO_ROWS * GRID_W
    main = NAT_ROWS_PER_STEP * GRID_W
    kcat[0:halo] = kp_ref[...]
    kcat[halo:halo + main] = km_ref[...]
    kcat[halo + main:2 * halo + main] = kn_ref[...]
    vcat[0:halo] = vp_ref[...]
    vcat[halo:halo + main] = vm_ref[...]
    vcat[halo + main:2 * halo + main] = vn_ref[...]
    width = NAT_HEADS_PER_CHAIN * HEAD_DIM
    head_of_lane = lax.broadcasted_iota(jnp.int32, (1, width), 1) // HEAD_DIM
    zero = jnp.zeros((), jnp.bfloat16)
    r0 = i * NAT_ROWS_PER_STEP

    def scores(qr, c):
        r = r0 + qr
        rs = jnp.clip(r - NA_ROWS // 2, 0, rows_seq - NA_ROWS)
        koff = pl.multiple_of((rs - r0 + NAT_HALO_ROWS) * GRID_W, GRID_W)
        sh = rs - r + (NA_ROWS - 1)
        qoff = pl.multiple_of(qr * GRID_W, GRID_W)
        cols = slice(width * c, width * (c + 1))
        qc = q_ref[pl.ds(qoff, GRID_W), cols]
        qm = jnp.concatenate([jnp.where(head_of_lane == h, qc, zero) for h in range(NAT_HEADS_PER_CHAIN)], axis=0)
        kw = kcat[pl.ds(koff, NAT_KEYS), cols]
        s = lax.dot_general(qm, kw, (((1,), (1,)), ((), ())), preferred_element_type=jnp.float32)
        pairs = NAT_HEADS_PER_CHAIN // 2
        bias = jnp.concatenate([tb_ref[pairs * c + k, sh] for k in range(pairs)], axis=0)
        return s + bias, koff, qoff

    def attend(s, koff, qoff, c):
        cols = slice(width * c, width * (c + 1))
        vw = vcat[pl.ds(koff, NAT_KEYS), cols]
        m = jnp.max(s, axis=-1, keepdims=True)
        pe = jnp.exp2(s - m)
        l = jnp.sum(pe, axis=-1, keepdims=True)
        o2 = jnp.dot(pe.astype(jnp.bfloat16), vw, preferred_element_type=jnp.float32)
        o2 = o2 * (1.0 / l)
        out = o2[:GRID_W]
        for h in range(1, NAT_HEADS_PER_CHAIN):
            out = jnp.where(head_of_lane == h, o2[GRID_W * h:GRID_W * (h + 1)], out)
        o_ref[pl.ds(qoff, GRID_W), cols] = out.astype(jnp.bfloat16)

    def trip(j, carry):
        chains = [(j * NAT_ROWS_PER_TRIP + q, c) for q in range(NAT_ROWS_PER_TRIP)
                  for c in range(N_HEADS_B // NAT_HEADS_PER_CHAIN)]
        pending = [scores(*c) for c in chains[:NAT_LOOKAHEAD]]
        for idx, (_, p) in enumerate(chains):
            s, koff, qoff = pending.pop(0)
            if idx + NAT_LOOKAHEAD < len(chains):
                pending.append(scores(*chains[idx + NAT_LOOKAHEAD]))
            attend(s, koff, qoff, p)
        return carry

    lax.fori_loop(0, NAT_ROWS_PER_STEP // NAT_ROWS_PER_TRIP, trip, 0)


def _nat_attention(qkv, tb, bsz, t):
    n = bsz * t
    rows_seq = t // GRID_W
    main = NAT_ROWS_PER_STEP * GRID_W
    halo = NAT_HALO_ROWS * GRID_W
    ntile = t // main
    nhalo_seq = t // halo
    per = main // halo

    def main_map(col):
        return lambda b, i: (b * ntile + i, col)

    def prev_map(col):
        return lambda b, i: (b * nhalo_seq + jnp.maximum(i * per - 1, 0), col)

    def next_map(col):
        return lambda b, i: (b * nhalo_seq + jnp.minimum(i * per + per, nhalo_seq - 1), col)

    qb, kb, vb = QB_COL // WIDTH_B, KB_COL // WIDTH_B, VB_COL // WIDTH_B
    return pl.pallas_call(
        functools.partial(_nat_kernel, rows_seq=rows_seq),
        out_shape=jax.ShapeDtypeStruct((n, WIDTH_B), jnp.bfloat16),
        grid=(bsz, ntile),
        in_specs=[
            pl.BlockSpec((main, WIDTH_B), main_map(qb)),
            pl.BlockSpec((halo, WIDTH_B), prev_map(kb)),
            pl.BlockSpec((main, WIDTH_B), main_map(kb)),
            pl.BlockSpec((halo, WIDTH_B), next_map(kb)),
            pl.BlockSpec((halo, WIDTH_B), prev_map(vb)),
            pl.BlockSpec((main, WIDTH_B), main_map(vb)),
            pl.BlockSpec((halo, WIDTH_B), next_map(vb)),
            pl.BlockSpec((N_HEADS_B // 2, NA_ROWS, 2 * GRID_W, NAT_KEYS), lambda b, i: (0, 0, 0, 0)),
        ],
        out_specs=pl.BlockSpec((main, WIDTH_B), main_map(0)),
        scratch_shapes=[pltpu.VMEM((main + 2 * halo, WIDTH_B), jnp.bfloat16),
                        pltpu.VMEM((main + 2 * halo, WIDTH_B), jnp.bfloat16)],
        compiler_params=_cparams("parallel", "parallel"),
        name="nat",
    )(qkv, qkv, qkv, qkv, qkv, qkv, qkv, tb)


EXPERT_ROW0 = N_GROUPS
ROUTE_ROWS = 48
INFO_E1, INFO_E2, INFO_R1, INFO_R2, INFO_W1, INFO_W2 = range(6)
INFO_ROWS = 8
MERGE_SUBTILES = 4
RANK_CHUNK = 256


def _route(lt, carry, tri):
    rr, tm = lt.shape
    row = lax.broadcasted_iota(jnp.int32, (rr, tm), 0).astype(jnp.float32)
    none = jnp.float32(rr)

    def first_max(sel):
        m = jnp.max(jnp.where(sel, lt, NEG_BIG), axis=0, keepdims=True)
        idx = jnp.min(jnp.where(sel & (lt == m), row, none), axis=0, keepdims=True)
        return m, idx

    is_group = row < N_GROUPS
    mg, g = first_max(is_group)
    pg_sel = 1.0 / jnp.sum(jnp.where(is_group, jnp.exp(jnp.where(is_group, lt, mg) - mg), 0.0),
                           axis=0, keepdims=True)
    row0 = EXPERT_ROW0 + EXPERTS_PER_GROUP * g
    in_group = (row >= row0) & (row < row0 + EXPERTS_PER_GROUP)
    m1, i1 = first_max(in_group)
    m2, i2 = first_max(in_group & (row != i1))
    e2 = jnp.exp(m2 - m1)
    w1 = pg_sel / (1.0 + e2)
    w2 = pg_sel * e2 / (1.0 + e2)

    oh1 = row == i1
    oh2 = row == i2
    both = (oh1 | oh2).astype(jnp.bfloat16)
    chunk = tri.shape[0]
    before = []
    for c in range(0, tm, chunk):
        part = both[:, c:c + chunk]
        before.append(jnp.dot(part, tri, preferred_element_type=jnp.float32) + carry[:, :chunk])
        carry = carry + jnp.sum(part.astype(jnp.float32), axis=1, keepdims=True)
    before = jnp.concatenate(before, axis=1)
    r1 = jnp.sum(jnp.where(oh1, before, 0.0), axis=0, keepdims=True)
    r2 = jnp.sum(jnp.where(oh2, before, 0.0), axis=0, keepdims=True)
    new_carry = carry

    field = lax.broadcasted_iota(jnp.int32, (INFO_ROWS, tm), 0)
    info = jnp.zeros((INFO_ROWS, tm), jnp.float32)
    for k, v in ((INFO_E1, i1 - EXPERT_ROW0), (INFO_E2, i2 - EXPERT_ROW0), (INFO_R1, r1), (INFO_R2, r2),
                 (INFO_W1, w1), (INFO_W2, w2)):
        info = jnp.where(field == k, v, info)
    return info, new_carry


def _merge_kernel(x_ref, oa_ref, ob_ref, lng_ref, lnb_ref, wg_ref, wpa_ref, wpb_ref, wo_ref,
                  l1g_ref, l1b_ref, wr_ref, br_ref, cnt0_ref,
                  h1_ref, h1p_ref, info_ref, cnt_ref, carry_ref, tri_ref):
    tm = x_ref.shape[0]

    @pl.when(pl.program_id(0) == 0)
    def _():
        carry_ref[...] = cnt0_ref[...]
        r = lax.broadcasted_iota(jnp.int32, tri_ref.shape, 0)
        c = lax.broadcasted_iota(jnp.int32, tri_ref.shape, 1)
        tri_ref[...] = (r < c).astype(jnp.bfloat16)

    def project(rows):
        h = _layer_norm(x_ref[rows], lng_ref[...], lnb_ref[...])
        gates = jnp.dot(h.astype(jnp.bfloat16), wg_ref[...], preferred_element_type=jnp.float32)
        pa = jnp.dot(oa_ref[rows], wpa_ref[...], preferred_element_type=jnp.float32)
        pb = jnp.dot(ob_ref[rows], wpb_ref[...], preferred_element_type=jnp.float32)
        return h, gates, pa, pb

    def mix(h, gates, pa, pb):
        mixin = jax.nn.sigmoid(gates[:, :D_MODEL]) * pa + jax.nn.sigmoid(gates[:, D_MODEL:]) * pb
        return DEEPNORM_ALPHA * h + jnp.dot(mixin.astype(jnp.bfloat16), wo_ref[...],
                                            preferred_element_type=jnp.float32)

    def norm_and_logits(pre, rows):
        h1 = _layer_norm(pre, l1g_ref[...], l1b_ref[...])
        h1_ref[rows] = h1
        h1p_ref[rows] = _pack_rows(h1)
        return lax.dot_general(wr_ref[...], h1.astype(jnp.bfloat16), (((1,), (1,)), ((), ())),
                               preferred_element_type=jnp.float32)

    sub = tm // MERGE_SUBTILES
    parts = [slice(k * sub, (k + 1) * sub) for k in range(MERGE_SUBTILES)]
    projected = [project(rows) for rows in parts]
    mixed = [mix(*pr) for pr in projected]
    logits_t = jnp.concatenate([norm_and_logits(pre, rows) for pre, rows in zip(mixed, parts)], axis=1)
    logits_t = logits_t + br_ref[...]
    info, carry = _route(logits_t, carry_ref[...], tri_ref[...])
    info_ref[...] = info
    carry_ref[...] = carry
    cnt_ref[...] = carry[:, :LANES]


def _merge(x2, oa, ob, ln_g, ln_b, w_gates, w_pa, w_pb, w_o, l1g, l1b, w_r, b_r, cnt0, tm):
    n = x2.shape[0]

    def const(shape):
        return pl.BlockSpec(shape, lambda i: (0,) * len(shape))

    def rows(width):
        return pl.BlockSpec((tm, width), lambda i: (i, 0))

    return pl.pallas_call(
        _merge_kernel,
        out_shape=(jax.ShapeDtypeStruct((n, D_MODEL), jnp.float32),
                   jax.ShapeDtypeStruct((n, PACKED_WIDTH), jnp.uint32),
                   jax.ShapeDtypeStruct((INFO_ROWS, n), jnp.float32),
                   jax.ShapeDtypeStruct((ROUTE_ROWS, LANES), jnp.float32)),
        grid=(n // tm,),
        in_specs=[
            rows(D_MODEL), rows(WIDTH_A), rows(WIDTH_B),
            const((1, D_MODEL)), const((1, D_MODEL)),
            const((D_MODEL, 2 * D_MODEL)),
            const((WIDTH_A, D_MODEL)), const((WIDTH_B, D_MODEL)),
            const((D_MODEL, D_MODEL)),
            const((1, D_MODEL)), const((1, D_MODEL)),
            const((ROUTE_ROWS, D_MODEL)), const((ROUTE_ROWS, tm)), const((ROUTE_ROWS, tm)),
        ],
        out_specs=(rows(D_MODEL), rows(PACKED_WIDTH), pl.BlockSpec((INFO_ROWS, tm), lambda i: (0, i)),
                   const((ROUTE_ROWS, LANES))),
        scratch_shapes=[pltpu.VMEM((ROUTE_ROWS, tm), jnp.float32),
                        pltpu.VMEM((RANK_CHUNK, RANK_CHUNK), jnp.bfloat16)],
        compiler_params=_cparams("arbitrary"),
        name="merge",
    )(x2, oa, ob, ln_g, ln_b, w_gates, w_pa, w_pb, w_o, l1g, l1b, w_r, b_r, cnt0)


SC_CORES = 2
SC_SUBCORES = 16
SC_WORKERS = SC_CORES * SC_SUBCORES
SC_ROWS_PER_STREAM = 64


def _sc_worker():
    return lax.axis_index("s") * SC_CORES + lax.axis_index("c")


def _scatter_indices(dest_t, pad_rows):
    n = dest_t.shape[1]
    nchunks = n // (SC_WORKERS * SC_ROWS_PER_STREAM)
    npad = pad_rows.size // (SC_WORKERS * SC_ROWS_PER_STREAM)
    assert nchunks * SC_ROWS_PER_STREAM * SC_WORKERS == n and nchunks % 2 == 0
    assert npad * SC_WORKERS * SC_ROWS_PER_STREAM == pad_rows.size
    return (dest_t.reshape(TOP_K, SC_WORKERS, nchunks, SC_ROWS_PER_STREAM),
            pad_rows.reshape(SC_WORKERS, npad, SC_ROWS_PER_STREAM))


def _dispatch(h1p, idx, pad_idx, zeros, nrows_out):
    n, width = h1p.shape
    per_worker = n // SC_WORKERS
    nchunks, npad = idx.shape[2], pad_idx.shape[1]
    assert zeros.shape == (SC_ROWS_PER_STREAM, width) and zeros.dtype == h1p.dtype
    mesh = plsc.VectorSubcoreMesh(core_axis_name="c", subcore_axis_name="s")

    @functools.partial(
        pl.kernel, out_type=jax.ShapeDtypeStruct((nrows_out, width), h1p.dtype), mesh=mesh,
        scratch_types=[pltpu.VMEM((TOP_K, nchunks, SC_ROWS_PER_STREAM), jnp.int32),
                       pltpu.VMEM((npad, SC_ROWS_PER_STREAM), jnp.int32),
                       pltpu.VMEM((2, SC_ROWS_PER_STREAM, width), h1p.dtype),
                       pltpu.SemaphoreType.DMA((2,)), pltpu.SemaphoreType.DMA((2,))],
        name="sc_dispatch")
    def scatter_kernel(src_hbm, idx_hbm, pad_hbm, zeros_hbm, out_hbm, idx_v, pad_v, rows_v, rsem, ssem):
        wid = _sc_worker()
        base = wid * per_worker
        for k in range(TOP_K):
            pltpu.sync_copy(idx_hbm.at[k, wid], idx_v.at[k])
        pltpu.sync_copy(pad_hbm.at[wid], pad_v)

        def read(j, slot):
            src = src_hbm.at[pl.ds(base + j * SC_ROWS_PER_STREAM, SC_ROWS_PER_STREAM)]
            return pltpu.make_async_copy(src, rows_v.at[slot], rsem.at[slot])

        def scatter(j, slot, k):
            return pltpu.make_async_copy(rows_v.at[slot], out_hbm.at[idx_v.at[k, j]], ssem.at[slot])

        read(0, 0).start()

        @pl.loop(0, nchunks, step=2)
        def _(j0):
            for slot in range(2):
                j = j0 + slot
                read(j, slot).wait()

                @pl.when(j + 1 < nchunks)
                def _():
                    @pl.when(j >= 1)
                    def _():
                        for k in range(TOP_K):
                            scatter(j - 1, 1 - slot, k).wait()
                    read(j + 1, 1 - slot).start()

                for k in range(TOP_K):
                    scatter(j, slot, k).start()

        for k in range(TOP_K):
            scatter(nchunks - 2, 0, k).wait()
            scatter(nchunks - 1, 1, k).wait()

        pltpu.sync_copy(zeros_hbm, rows_v.at[0])
        fills = [pltpu.make_async_copy(rows_v.at[0], out_hbm.at[pad_v.at[c]], ssem.at[0]) for c in range(npad)]
        for f in fills:
            f.start()
        for f in fills:
            f.wait()

    return scatter_kernel(h1p, idx, pad_idx, zeros)


def _sorted_layout(counts, eid, rank, bm):
    n = eid.shape[1]
    nblocks = TOP_K * n // bm + N_EXPERTS
    expert = jnp.arange(N_EXPERTS, dtype=jnp.int32)
    before = expert[None, :] < expert[:, None]
    blocks = (counts + bm - 1) // bm
    first_blk = jnp.sum(jnp.where(before, blocks[None, :], 0), axis=1)
    starts = first_blk * bm
    dest = rank + jnp.sum(jnp.where(eid[None] == expert[:, None, None], starts[:, None, None], 0), axis=0)
    total = jnp.sum(blocks)
    j = jnp.arange(bm, dtype=jnp.int32)[None, :]
    npad = blocks * bm - counts
    spare_before = jnp.sum(jnp.where(before, (bm - npad)[None, :], 0), axis=1)
    pad_rows = jnp.where(j < npad[:, None], (starts + counts)[:, None] + j,
                         (total * bm + spare_before - npad)[:, None] + j).astype(jnp.int32)
    w = jnp.arange(nblocks, dtype=jnp.int32)
    real = jnp.minimum(w, total - 1)
    blk_end = first_blk + blocks
    owner = jnp.minimum(jnp.sum((blk_end[None, :] <= real[:, None]).astype(jnp.int32), axis=1), N_EXPERTS - 1)
    valid = (w < total).astype(jnp.int32)
    return dest.astype(jnp.int32), pad_rows, (owner.astype(jnp.int32), valid)


EXPERT_BLOCKS_PER_STEP = 2


def _expert_kernel(e_ref, valid_ref, x_ref, *refs):
    o_ref = refs[-1]
    bm = x_ref.shape[0] // EXPERT_BLOCKS_PER_STEP
    first = pl.program_id(0) * EXPERT_BLOCKS_PER_STEP

    def compute(i):
        wg_ref, wu_ref, wd_ref = refs[3 * i:3 * i + 3]
        x = _unpack_rows(x_ref[i * bm:(i + 1) * bm]).astype(jnp.bfloat16)
        g = jnp.dot(x, wg_ref[0], preferred_element_type=jnp.float32)
        u = jnp.dot(x, wu_ref[0], preferred_element_type=jnp.float32)
        hmid = (jax.nn.silu(g) * u).astype(jnp.bfloat16)
        o_ref[i * bm:(i + 1) * bm] = _pack_rows(jnp.dot(hmid, wd_ref[0], preferred_element_type=jnp.float32))

    def clear(i):
        o_ref[i * bm:(i + 1) * bm] = jnp.zeros((bm, PACKED_WIDTH), o_ref.dtype)

    nvalid = sum(valid_ref[first + i] for i in range(EXPERT_BLOCKS_PER_STEP))
    for k in range(EXPERT_BLOCKS_PER_STEP + 1):
        @pl.when(nvalid == k)
        def _(k=k):
            for i in range(EXPERT_BLOCKS_PER_STEP):
                compute(i) if i < k else clear(i)


def _experts(xs, plan, w_gate, w_up, w_down, bm):
    nblocks = plan[0].shape[0]
    nsteps = nblocks // EXPERT_BLOCKS_PER_STEP
    assert xs.shape[0] == nblocks * bm and nsteps * EXPERT_BLOCKS_PER_STEP == nblocks
    step_rows = EXPERT_BLOCKS_PER_STEP * bm

    def weights(shape, i):
        return pl.BlockSpec((1,) + shape, lambda w, e, valid: (e[w * EXPERT_BLOCKS_PER_STEP + i], 0, 0))

    grid_spec = pltpu.PrefetchScalarGridSpec(
        num_scalar_prefetch=len(plan),
        grid=(nsteps,),
        in_specs=[pl.BlockSpec((step_rows, PACKED_WIDTH), lambda w, *_: (w, 0))] + [
            weights(shape, i) for i in range(EXPERT_BLOCKS_PER_STEP)
            for shape in ((D_MODEL, D_EXPERT), (D_MODEL, D_EXPERT), (D_EXPERT, D_MODEL))],
        out_specs=pl.BlockSpec((step_rows, PACKED_WIDTH), lambda w, *_: (w, 0)),
    )
    return pl.pallas_call(
        _expert_kernel,
        out_shape=jax.ShapeDtypeStruct(xs.shape, jnp.uint32),
        grid_spec=grid_spec,
        compiler_params=_cparams("parallel"),
        name="experts",
    )(*plan, xs, *([w_gate, w_up, w_down] * EXPERT_BLOCKS_PER_STEP))


def _sc_gather_rows(table, idx):
    nrows = idx.shape[0]
    width = table.shape[1]
    per_worker = nrows // SC_WORKERS
    nchunks = per_worker // SC_ROWS_PER_STREAM
    assert nchunks * SC_ROWS_PER_STREAM * SC_WORKERS == nrows and nchunks % 2 == 0
    mesh = plsc.VectorSubcoreMesh(core_axis_name="c", subcore_axis_name="s")

    @functools.partial(
        pl.kernel, out_type=jax.ShapeDtypeStruct((nrows, width), table.dtype), mesh=mesh,
        scratch_types=[pltpu.VMEM((per_worker,), jnp.int32),
                       pltpu.VMEM((2, SC_ROWS_PER_STREAM, width), table.dtype),
                       pltpu.SemaphoreType.DMA((2,)), pltpu.SemaphoreType.DMA((2,))],
        name="sc_gather")
    def gather_kernel(table_hbm, idx_hbm, out_hbm, idx_v, rows_v, gsem, wsem):
        base = _sc_worker() * per_worker
        pltpu.sync_copy(idx_hbm.at[pl.ds(base, per_worker)], idx_v)

        def gather(j, slot):
            rows = idx_v.at[pl.ds(j * SC_ROWS_PER_STREAM, SC_ROWS_PER_STREAM)]
            return pltpu.make_async_copy(table_hbm.at[rows], rows_v.at[slot], gsem.at[slot])

        def write(j, slot):
            dst = out_hbm.at[pl.ds(base + j * SC_ROWS_PER_STREAM, SC_ROWS_PER_STREAM)]
            return pltpu.make_async_copy(rows_v.at[slot], dst, wsem.at[slot])

        gather(0, 0).start()

        @pl.loop(0, nchunks, step=2)
        def _(j0):
            for slot in range(2):
                j = j0 + slot
                gather(j, slot).wait()

                @pl.when(j + 1 < nchunks)
                def _():
                    @pl.when(j >= 1)
                    def _():
                        write(j - 1, 1 - slot).wait()
                    gather(j + 1, 1 - slot).start()

                write(j, slot).start()

        write(nchunks - 2, 0).wait()
        write(nchunks - 1, 1).wait()

    return gather_kernel(table, idx)


def _finalize_kernel(h1_ref, y1_ref, y2_ref, info_ref, g_ref, b_ref, o_ref):
    tt = h1_ref.shape[0]
    pad = jnp.zeros((LANES - INFO_ROWS, tt), jnp.float32)
    info = jnp.concatenate([info_ref[...], pad], axis=0).T
    moe = (_unpack_rows(y1_ref[...]) * info[:, INFO_W1:INFO_W1 + 1]
           + _unpack_rows(y2_ref[...]) * info[:, INFO_W2:INFO_W2 + 1])
    o_ref[...] = _layer_norm(DEEPNORM_ALPHA * h1_ref[...] + moe, g_ref[...], b_ref[...])


def _combine(h1, info, dest_flat, ys, ln_g, ln_b, tt):
    n = h1.shape[0]
    nsteps = n // tt
    yg = _sc_gather_rows(ys, dest_flat)
    return pl.pallas_call(
        _finalize_kernel,
        out_shape=jax.ShapeDtypeStruct((n, D_MODEL), jnp.float32),
        grid=(nsteps,),
        in_specs=[
            pl.BlockSpec((tt, D_MODEL), lambda i: (i, 0)),
            pl.BlockSpec((tt, PACKED_WIDTH), lambda i: (i, 0)),
            pl.BlockSpec((tt, PACKED_WIDTH), lambda i: (nsteps + i, 0)),
            pl.BlockSpec((INFO_ROWS, tt), lambda i: (0, i)),
            pl.BlockSpec((1, D_MODEL), lambda i: (0, 0)),
            pl.BlockSpec((1, D_MODEL), lambda i: (0, 0)),
        ],
        out_specs=pl.BlockSpec((tt, D_MODEL), lambda i: (i, 0)),
        compiler_params=_cparams("parallel"),
        name="finalize",
    )(h1, yg, yg, info, ln_g, ln_b)


TM_QKV = 1024
TQ_WIN = 1024
TM_MERGE = 1024
TT_ROWS = 1024
BM_EXPERT = 512


def _prepare_weights(ln_in_g, ln_in_b, w_in, attn_sink, rel_pos_bias, w_proj_a, w_proj_b, w_out,
                     ln1_g, ln1_b, w_route_group, b_route_group, w_route_expert, b_route_expert,
                     ln2_g, ln2_b):
    bf = jnp.bfloat16
    w = w_in[0]
    splits = np.cumsum([WIDTH_A, KV_WIDTH_A, KV_WIDTH_A, WIDTH_B, WIDTH_B, WIDTH_B, D_MODEL])
    wqa, wka, wva, wqb, wkb, wvb, wga, wgb = jnp.split(w, [int(s) for s in splits], axis=1)
    wqa = (wqa.reshape(D_MODEL, N_KV_HEADS_A, GQA_GROUP, HEAD_DIM).transpose(0, 2, 1, 3)
           .reshape(D_MODEL, WIDTH_A))
    w_qkv = jnp.concatenate([wqa, wqb, wkb, wvb, wka, wva], axis=1).astype(bf)
    w_gates = jnp.concatenate([wga, wgb], axis=1).astype(bf)
    w_pa = (w_proj_a[0].reshape(N_KV_HEADS_A, GQA_GROUP, HEAD_DIM, D_MODEL).transpose(1, 0, 2, 3)
            .reshape(WIDTH_A, D_MODEL).astype(bf))
    w_pb = w_proj_b[0].astype(bf)
    w_o = w_out[0].astype(bf)
    pad = ROUTE_ROWS - N_GROUPS - N_EXPERTS
    w_r = jnp.concatenate([w_route_group[0].T, w_route_expert[0].T, jnp.zeros((pad, D_MODEL), jnp.float32)], axis=0)
    w_route = w_r.astype(bf)
    b_r = jnp.concatenate([b_route_group[0], b_route_expert[0], jnp.zeros((pad,), jnp.float32)])
    b_r = jnp.broadcast_to(b_r[:, None], (ROUTE_ROWS, TM_MERGE))
    row = lambda v: v.reshape(1, D_MODEL)
    return dict(
        ln_in_g=row(ln_in_g), ln_in_b=row(ln_in_b), w_qkv=w_qkv, w_gates=w_gates,
        sink=attn_sink[0].astype(jnp.float32), nat_bias=_nat_bias_table(rel_pos_bias[0]),
        w_pa=w_pa, w_pb=w_pb, w_o=w_o, ln1_g=row(ln1_g[0]), ln1_b=row(ln1_b[0]),
        w_route=w_route, b_r=b_r, ln2_g=row(ln2_g[0]), ln2_b=row(ln2_b[0]))


def _after(value, other):
    if other is None:
        return value
    other = other.astype(jnp.float32)
    zero = jnp.where(jnp.isfinite(other), other, 0.0) * 0.0
    return value + zero.astype(value.dtype)


def _attend_and_route(x, p, after=None, cast=()):
    bsz, t, _ = x.shape
    n = bsz * t
    x2 = x.reshape(n, D_MODEL)
    qkv, cast_bf16 = _qkv(x2, _after(p["ln_in_g"], after), p["ln_in_b"], p["w_qkv"], TM_QKV, cast)
    oa = _win_attention(qkv, p["sink"], bsz, t, TQ_WIN)
    ob = _nat_attention(qkv, p["nat_bias"], bsz, t)
    cnt0 = jnp.zeros((ROUTE_ROWS, TM_MERGE), jnp.float32)
    h1, h1p, info, cnt = _merge(x2, oa, ob, p["ln_in_g"], p["ln_in_b"], p["w_gates"], p["w_pa"], p["w_pb"],
                                p["w_o"], p["ln1_g"], p["ln1_b"], p["w_route"], p["b_r"], cnt0, TM_MERGE)
    counts = cnt[EXPERT_ROW0:EXPERT_ROW0 + N_EXPERTS, 0].astype(jnp.int32)
    eid = info[INFO_E1:INFO_E2 + 1].astype(jnp.int32)
    rank = info[INFO_R1:INFO_R2 + 1].astype(jnp.int32)
    dest_t, pad_rows, plan = _sorted_layout(counts, eid, rank, BM_EXPERT)
    zero_rows = jnp.zeros((SC_ROWS_PER_STREAM, PACKED_WIDTH), jnp.uint32)
    moves = lax.optimization_barrier(_scatter_indices(dest_t, pad_rows) + (zero_rows, dest_t.reshape(TOP_K * n)))
    return dict(shape=x.shape, h1=h1, h1p=h1p, info=info, counts=counts, moves=moves, plan=plan, cast=cast_bf16)


def _run_experts(r, expert_weights, after=None):
    owner, valid = r["plan"]
    scatter_idx, pad_idx, zero_rows, _ = r["moves"]
    xs = _dispatch(r["h1p"], scatter_idx, pad_idx, zero_rows, owner.shape[0] * BM_EXPERT)
    return _experts(xs, (owner, _after(valid, after)), *expert_weights, BM_EXPERT)


def _finish(r, ys, p, after=None):
    out = _combine(r["h1"], r["info"], r["moves"][3], ys, _after(p["ln2_g"], after), p["ln2_b"], TT_ROWS)
    return out.reshape(r["shape"])


def kernel(x_prompt, x_sample, ln_in_g, ln_in_b, w_in, attn_sink, rel_pos_bias, w_proj_a, w_proj_b, w_out,
           ln1_g, ln1_b, w_route_group, b_route_group, w_route_expert, b_route_expert,
           w_gate, w_up, w_down, ln2_g, ln2_b):
    p = _prepare_weights(ln_in_g, ln_in_b, w_in, attn_sink, rel_pos_bias, w_proj_a, w_proj_b, w_out,
                         ln1_g, ln1_b, w_route_group, b_route_group, w_route_expert, b_route_expert,
                         ln2_g, ln2_b)
    rp = _attend_and_route(x_prompt, p, cast=(w_gate[0], w_up[0], w_down[0]))
    glue_done = sum(a.reshape(-1)[0].astype(jnp.float32) for a in rp["moves"] + rp["plan"])
    rs = _attend_and_route(x_sample, p, after=glue_done)
    ys_p = _run_experts(rp, rp["cast"])
    ys_s = _run_experts(rs, rp["cast"], after=ys_p[0, 0])
    y_prompt = _finish(rp, ys_p, p)
    y_sample = _finish(rs, ys_s, p, after=y_prompt[0, 0, 0])
    return (y_prompt, y_sample)
```
